```python
import math
import jax
import jax.numpy as jnp
from jax import lax
import numpy as np

D_MODEL = 1024
BATCH = 4
SEQ = 8192
DEPTH = 1

GRID_W = 64
CTX_LEN = 256
EPS = 1e-6

GLA_HEADS = 4
GLA_DK = 64
GLA_DV = 128
GLA_QK = GLA_HEADS * GLA_DK
GLA_V = GLA_HEADS * GLA_DV
GLA_RANK = 16
GLA_GATE_NORM = 16.0
GLA_CHUNK = 16

SSD_HEADDIM = 64
SSD_INNER = 512
SSD_HEADS = SSD_INNER // SSD_HEADDIM
SSD_GROUPS = 2
SSD_HEADS_PER_GROUP = SSD_HEADS // SSD_GROUPS
SSD_STATE = 128
SSD_CHUNK = 128
SSD_CONV_DIM = SSD_INNER + 2 * SSD_GROUPS * SSD_STATE
SSD_CONV_K = 3

MIX_WIDTH = GLA_V + SSD_INNER

IN_SPLITS = (GLA_QK, GLA_QK, GLA_V, GLA_V, GLA_RANK, SSD_INNER, SSD_CONV_DIM, SSD_HEADS)
IN_DIM = sum(IN_SPLITS)
IN_OFFSETS = tuple(int(o) for o in np.cumsum(IN_SPLITS)[:-1])

N_EXPERTS = 32
TOP_K = 4
D_FF = 1024
SWIGLU_LIMIT = 7.0
SWIGLU_ALPHA = 1.702
MOE_BLOCK = 256

kernel_name = "hybrid_gla_ssd_moe_prefix_dit"


def rmsnorm(x, w):
    x32 = x.astype(jnp.float32)
    y = x32 * lax.rsqrt(jnp.mean(x32 * x32, axis=-1, keepdims=True) + EPS)
    return (y * w.astype(jnp.float32)).astype(x.dtype)


def modulate(h, shift, scale):
    return h * (1.0 + scale) + shift


def _identity(a):
    return a


def _flip(a):
    return jnp.flip(a, axis=1)


def gla_scan(q, k, v, log_a, s0):
    Bsz, L, H, DK = q.shape
    DV = v.shape[-1]
    C = GLA_CHUNK
    n = L // C
    qc = q.reshape(Bsz, n, C, H, DK).astype(jnp.float32)
    kc = k.reshape(Bsz, n, C, H, DK).astype(jnp.float32)
    vc = v.reshape(Bsz, n, C, H, DV).astype(jnp.float32)
    b = jnp.cumsum(log_a.reshape(Bsz, n, C, H, DK).astype(jnp.float32), axis=2)
    b_last = b[:, :, -1]
    mask = jnp.tril(jnp.ones((C, C), dtype=bool))[:, :, None, None]
    decay = jnp.exp(jnp.where(mask, b[:, :, :, None] - b[:, :, None], -jnp.inf))
    scores = jnp.einsum('bnihd,bnijhd,bnjhd->bnhij', qc, decay, kc)
    o_intra = jnp.einsum('bnhij,bnjhv->bnihv', scores, vc)
    q_dec = qc * jnp.exp(b)
    k_dec = kc * jnp.exp(b_last[:, :, None] - b)
    u = jnp.einsum('bnjhd,bnjhv->bnhdv', k_dec, vc)

    def step(S, inp):
        qd, uc, a_tot = inp
        o = jnp.einsum('bihd,bhdv->bihv', qd, S)
        return S * a_tot[..., None] + uc, o

    s_fin, o_inter = lax.scan(step, s0, (jnp.moveaxis(q_dec, 1, 0), jnp.moveaxis(u, 1, 0),
                                         jnp.moveaxis(jnp.exp(b_last), 1, 0)))
    o = o_intra + jnp.moveaxis(o_inter, 0, 1)
    return o.reshape(Bsz, L, H, DV).astype(v.dtype), s_fin


def ssd_scan(xh, dt, A, Bm, Cm, s0):
    Bsz, L, G, R, P = xh.shape
    N = Bm.shape[-1]
    C = SSD_CHUNK
    n = L // C
    dt32 = dt.astype(jnp.float32)
    a = (dt32 * A).reshape(Bsz, n, C, G, R)
    xdt = (xh.astype(jnp.float32) * dt32[..., None]).reshape(Bsz, n, C, G, R, P)
    Bc = Bm.reshape(Bsz, n, C, G, N).astype(jnp.float32)
    Cc = Cm.reshape(Bsz, n, C, G, N).astype(jnp.float32)
    acum = jnp.cumsum(a, axis=2)
    mask = jnp.tril(jnp.ones((C, C), dtype=bool))[:, :, None, None]
    seg = jnp.exp(jnp.where(mask, acum[:, :, :, None] - acum[:, :, None], -jnp.inf))
    cb = jnp.einsum('bnigs,bnjgs->bnijg', Cc, Bc)
    y_diag = jnp.einsum('bnijg,bnijgr,bnjgrp->bnigrp', cb, seg, xdt)
    a_last = acum[:, :, -1]
    w_state = jnp.exp(a_last[:, :, None] - acum)
    u = jnp.einsum('bnjgs,bnjgr,bnjgrp->bngrps', Bc, w_state, xdt)
    c_dec = jnp.exp(acum)

    def step(S, inp):
        cn, cd, uc, a_tot = inp
        o = jnp.einsum('bigs,bigr,bgrps->bigrp', cn, cd, S)
        return S * a_tot[..., None, None] + uc, o

    s_fin, y_off = lax.scan(step, s0, (jnp.moveaxis(Cc, 1, 0), jnp.moveaxis(c_dec, 1, 0),
                                       jnp.moveaxis(u, 1, 0), jnp.moveaxis(jnp.exp(a_last), 1, 0)))
    y = y_diag + jnp.moveaxis(y_off, 0, 1)
    return y.reshape(Bsz, L, G, R, P).astype(xh.dtype), s_fin


def gla_heads(q, k, v):
    Bsz, L, _ = q.shape
    return (q.reshape(Bsz, L, GLA_HEADS, GLA_DK) * (GLA_DK ** -0.5),
            k.reshape(Bsz, L, GLA_HEADS, GLA_DK),
            v.reshape(Bsz, L, GLA_HEADS, GLA_DV))


def gla_log_decay(gk_low, w_up, b_up):
    Bsz, L, _ = gk_low.shape
    g = jax.nn.log_sigmoid((gk_low @ w_up + b_up).astype(jnp.float32)) / GLA_GATE_NORM
    return g.reshape(Bsz, L, GLA_HEADS, GLA_DK)


def gla_output(o, g, w_norm):
    Bsz, L = o.shape[:2]
    y = rmsnorm(o, w_norm) * jax.nn.silu(g.reshape(Bsz, L, GLA_HEADS, GLA_DV))
    return y.reshape(Bsz, L, GLA_V)


def gla_mixer(p, pc, w_gk_up, b_gk, w_norm, with_ctx):
    q, k, v, g, gk_low = p
    qc, kc, vc, gc, gkc_low = pc
    ql, kl, vl = gla_heads(q, k, v)
    qx, kx, vx = gla_heads(qc, kc, vc)
    s0 = jnp.zeros((qx.shape[0], GLA_HEADS, GLA_DK, GLA_DV), jnp.float32)
    o_lat, o_ctx = None, None
    for d in range(2):
        f = _identity if d == 0 else _flip
        ld_c = gla_log_decay(gkc_low, w_gk_up[d], b_gk[d])
        ld_l = gla_log_decay(gk_low, w_gk_up[d], b_gk[d])
        oc, sc = gla_scan(f(qx), f(kx), f(vx), f(ld_c), s0)
        ol, _ = gla_scan(f(ql), f(kl), f(vl), f(ld_l), sc)
        o_lat = f(ol) if o_lat is None else o_lat + f(ol)
        o_ctx = f(oc) if o_ctx is None else o_ctx + f(oc)
    y = gla_output(o_lat, g, w_norm)
    yc = gla_output(o_ctx, gc, w_norm) if with_ctx else None
    return y, yc


def conv_latent(xbc, w, b):
    Bsz, L, Ch = xbc.shape
    rows = L // GRID_W
    y = lax.conv_general_dilated(xbc.reshape(Bsz, rows, GRID_W, Ch), w[:, :, None, :], (1, 1), 'SAME',
                                 dimension_numbers=('NHWC', 'HWIO', 'NHWC'), feature_group_count=Ch)
    return jax.nn.silu(y.reshape(Bsz, L, Ch) + b)


def conv_context(xbc, w, b):
    Ch = xbc.shape[-1]
    y = lax.conv_general_dilated(xbc, w[1][:, None, :], (1,), 'SAME',
                                 dimension_numbers=('NWC', 'WIO', 'NWC'), feature_group_count=Ch)
    return jax.nn.silu(y + b)


def ssd_heads(xbc_act):
    Bsz, L, _ = xbc_act.shape
    xs, Bm, Cm = jnp.split(xbc_act, [SSD_INNER, SSD_INNER + SSD_GROUPS * SSD_STATE], axis=-1)
    return (xs.reshape(Bsz, L, SSD_GROUPS, SSD_HEADS_PER_GROUP, SSD_HEADDIM),
            Bm.reshape(Bsz, L, SSD_GROUPS, SSD_STATE),
            Cm.reshape(Bsz, L, SSD_GROUPS, SSD_STATE))


def ssd_output(y, z, w_norm):
    Bsz, L = y.shape[:2]
    u = (y.reshape(Bsz, L, SSD_INNER) * jax.nn.silu(z)).reshape(Bsz, L, SSD_GROUPS, SSD_INNER // SSD_GROUPS)
    return rmsnorm(u, w_norm.reshape(SSD_GROUPS, SSD_INNER // SSD_GROUPS)).reshape(Bsz, L, SSD_INNER)


def ssd_mixer(p, pc, conv_w, conv_b, dt_bias, A_log, D_skip, w_norm, with_ctx):
    z, xbc, dt_raw = p
    zc, xbcc, dtc_raw = pc
    xl, Bl, Cl = ssd_heads(conv_latent(xbc, conv_w, conv_b))
    xc, Bc, Cc = ssd_heads(conv_context(xbcc, conv_w, conv_b))
    Dg = D_skip.reshape(SSD_GROUPS, SSD_HEADS_PER_GROUP, 1)
    y_lat = Dg * xl
    y_ctx = Dg * xc
    gr = (SSD_GROUPS, SSD_HEADS_PER_GROUP)
    s0 = jnp.zeros((xc.shape[0], SSD_GROUPS, SSD_HEADS_PER_GROUP, SSD_HEADDIM, SSD_STATE), jnp.float32)
    for d in range(2):
        f = _identity if d == 0 else _flip
        A = -jnp.exp(A_log[d].astype(jnp.float32)).reshape(gr)
        bias = dt_bias[d].astype(jnp.float32)
        dtl = jax.nn.softplus(dt_raw.astype(jnp.float32) + bias).reshape(dt_raw.shape[:2] + gr)
        dtc = jax.nn.softplus(dtc_raw.astype(jnp.float32) + bias).reshape(dtc_raw.shape[:2] + gr)
        yc, sc = ssd_scan(f(xc), f(dtc), A, f(Bc), f(Cc), s0)
        yl, _ = ssd_scan(f(xl), f(dtl), A, f(Bl), f(Cl), sc)
        y_lat = y_lat + f(yl)
        y_ctx = y_ctx + f(yc)
    y = ssd_output(y_lat, z, w_norm)
    yc_out = ssd_output(y_ctx, zc, w_norm) if with_ctx else None
    return y, yc_out


def moe_ffn(h, w_router, b_router, w_gate_up, b_gate_up, w_down, b_down):
    T, D = h.shape
    logits = (h @ w_router + b_router).astype(jnp.float32)
    top_val, top_idx = lax.top_k(logits, TOP_K)
    gates = jax.nn.softmax(top_val, axis=-1).astype(h.dtype)
    A_n = T * TOP_K
    flat_e = top_idx.reshape(-1).astype(jnp.int32)
    flat_tok = jnp.repeat(jnp.arange(T, dtype=jnp.int32), TOP_K)
    flat_w = gates.reshape(-1)
    order = jnp.argsort(flat_e)
    se, stok, sw = flat_e[order], flat_tok[order], flat_w[order]
    counts = jnp.bincount(flat_e, length=N_EXPERTS).astype(jnp.int32)
    padded = ((counts + MOE_BLOCK - 1) // MOE_BLOCK) * MOE_BLOCK
    start = jnp.cumsum(counts) - counts
    pend = jnp.cumsum(padded)
    pstart = pend - padded
    dest = pstart[se] + (jnp.arange(A_n, dtype=jnp.int32) - start[se])
    n_blocks = -(-A_n // MOE_BLOCK) + N_EXPERTS
    P = n_blocks * MOE_BLOCK
    tok_buf = jnp.full((P,), T, jnp.int32).at[dest].set(stok)
    w_buf = jnp.zeros((P,), h.dtype).at[dest].set(sw)
    blk_e = jnp.minimum(jnp.searchsorted(pend, jnp.arange(n_blocks, dtype=jnp.int32) * MOE_BLOCK,
                                         side='right'), N_EXPERTS - 1)
    h_pad = jnp.concatenate([h, jnp.zeros((1, D), h.dtype)], axis=0)

    def expert_block(args):
        e, toks, wts = args
        xb = h_pad[toks]
        gu = xb @ w_gate_up[e] + b_gate_up[e]
        gate, up = gu[:, :D_FF], gu[:, D_FF:]
        gate = jnp.minimum(gate, SWIGLU_LIMIT)
        up = jnp.clip(up, -SWIGLU_LIMIT, SWIGLU_LIMIT)
        glu = gate * jax.nn.sigmoid(SWIGLU_ALPHA * gate)
        y = ((up + 1.0) * glu) @ w_down[e] + b_down[e]
        return y * wts[:, None]

    y = lax.map(expert_block, (blk_e, tok_buf.reshape(n_blocks, MOE_BLOCK), w_buf.reshape(n_blocks, MOE_BLOCK)))
    out = jnp.zeros((T + 1, D), h.dtype).at[tok_buf].add(y.reshape(P, D).astype(h.dtype))
    return out[:T]


def setup_inputs(seed: int = 0) -> dict:
    key = jax.random.key(seed)
    ks = jax.random.split(key, 26)

    def nrm(k, shape, scale):
        return jax.random.normal(k, shape, jnp.float32) * scale

    dt0 = jnp.exp(jax.random.uniform(ks[13], (DEPTH, 2, SSD_HEADS), jnp.float32,
                                     minval=math.log(1e-3), maxval=math.log(1e-1)))
    return {
        "x": nrm(ks[0], (BATCH, SEQ, D_MODEL), 1.0),
        "c": nrm(ks[1], (BATCH, D_MODEL), 1.0),
        "ctx": nrm(ks[2], (BATCH, CTX_LEN, D_MODEL), 1.0),
        "c_ctx": nrm(ks[3], (D_MODEL,), 1.0),
        "w_mod": nrm(ks[4], (DEPTH, D_MODEL, 6 * D_MODEL), D_MODEL ** -0.5),
        "b_mod": nrm(ks[5], (DEPTH, 6 * D_MODEL), 0.02),
        "norm1": 1.0 + nrm(ks[6], (DEPTH, D_MODEL), 0.02),
        "w_in": nrm(ks[7], (DEPTH, D_MODEL, IN_DIM), D_MODEL ** -0.5),
        "gla_w_gk_up": nrm(ks[8], (DEPTH, 2, GLA_RANK, GLA_QK), GLA_RANK ** -0.5),
        "gla_b_gk": nrm(ks[9], (DEPTH, 2, GLA_QK), 0.1),
        "gla_norm": 1.0 + nrm(ks[10], (DEPTH, GLA_DV), 0.02),
        "ssd_conv_w": nrm(ks[11], (DEPTH, SSD_CONV_K, SSD_CONV_K, SSD_CONV_DIM), 1.0 / SSD_CONV_K),
        "ssd_conv_b": nrm(ks[12], (DEPTH, SSD_CONV_DIM), 0.02),
        "ssd_dt_bias": dt0 + jnp.log(-jnp.expm1(-dt0)),
        "ssd_A_log": jnp.log(jax.random.uniform(ks[14], (DEPTH, 2, SSD_HEADS), jnp.float32, minval=1.0, maxval=16.0)),
        "ssd_D": 1.0 + nrm(ks[15], (DEPTH, SSD_HEADS), 0.02),
        "ssd_norm": 1.0 + nrm(ks[16], (DEPTH, SSD_INNER), 0.02),
        "w_out": nrm(ks[17], (DEPTH, MIX_WIDTH, D_MODEL), MIX_WIDTH ** -0.5),
        "norm2": 1.0 + nrm(ks[18], (DEPTH, D_MODEL), 0.02),
        "w_router": nrm(ks[19], (DEPTH, D_MODEL, N_EXPERTS), D_MODEL ** -0.5),
        "b_router": nrm(ks[20], (DEPTH, N_EXPERTS), 0.01),
        "w_gate_up": nrm(ks[21], (DEPTH, N_EXPERTS, D_MODEL, 2 * D_FF), D_MODEL ** -0.5),
        "b_gate_up": nrm(ks[22], (DEPTH, N_EXPERTS, 2 * D_FF), 0.01),
        "w_down": nrm(ks[23], (DEPTH, N_EXPERTS, D_FF, D_MODEL), D_FF ** -0.5),
        "b_down": nrm(ks[24], (DEPTH, N_EXPERTS, D_MODEL), 0.01),
        "final_norm": 1.0 + nrm(ks[25], (D_MODEL,), 0.02),
    }


def reference(x, c, ctx, c_ctx, w_mod, b_mod, norm1, w_in, gla_w_gk_up, gla_b_gk, gla_norm,
              ssd_conv_w, ssd_conv_b, ssd_dt_bias, ssd_A_log, ssd_D, ssd_norm, w_out,
              norm2, w_router, b_router, w_gate_up, b_gate_up, w_down, b_down, final_norm):
    Bsz, L, D = x.shape
    xc = ctx
    for l in range(DEPTH):
        update_ctx = l < DEPTH - 1
        mod = (jax.nn.silu(c) @ w_mod[l] + b_mod[l])[:, None, :]
        mod_c = jax.nn.silu(c_ctx) @ w_mod[l] + b_mod[l]
        sh1, sc1, g1, sh2, sc2, g2 = jnp.split(mod, 6, axis=-1)
        sh1c, sc1c, g1c, sh2c, sc2c, g2c = jnp.split(mod_c, 6, axis=-1)
        h = modulate(rmsnorm(x, norm1[l]), sh1, sc1)
        hc = modulate(rmsnorm(xc, norm1[l]), sh1c, sc1c)
        p = jnp.split(h @ w_in[l], IN_OFFSETS, axis=-1)
        pc = jnp.split(hc @ w_in[l], IN_OFFSETS, axis=-1)
        y_gla, yc_gla = gla_mixer(p[:5], pc[:5], gla_w_gk_up[l], gla_b_gk[l], gla_norm[l], update_ctx)
        y_ssd, yc_ssd = ssd_mixer(p[5:], pc[5:], ssd_conv_w[l], ssd_conv_b[l], ssd_dt_bias[l],
                                  ssd_A_log[l], ssd_D[l], ssd_norm[l], update_ctx)
        x = x + g1 * (jnp.concatenate([y_gla, y_ssd], axis=-1) @ w_out[l])
        h2 = modulate(rmsnorm(x, norm2[l]), sh2, sc2).reshape(Bsz * L, D)
        if update_ctx:
            xc = xc + g1c * (jnp.concatenate([yc_gla, yc_ssd], axis=-1) @ w_out[l])
            h2c = modulate(rmsnorm(xc, norm2[l]), sh2c, sc2c).reshape(-1, D)
            f_all = moe_ffn(jnp.concatenate([h2, h2c], axis=0), w_router[l], b_router[l],
                            w_gate_up[l], b_gate_up[l], w_down[l], b_down[l])
            x = x + g2 * f_all[:Bsz * L].reshape(Bsz, L, D)
            xc = xc + g2c * f_all[Bsz * L:].reshape(xc.shape)
        else:
            f_lat = moe_ffn(h2, w_router[l], b_router[l], w_gate_up[l], b_gate_up[l], w_down[l], b_down[l])
            x = x + g2 * f_lat.reshape(Bsz, L, D)
    return rmsnorm(x, final_norm)
```

```python
import functools

import numpy as np
import jax
import jax.numpy as jnp
from jax import lax
from jax.experimental import pallas as pl
from jax.experimental.pallas import tpu as pltpu

F32 = jnp.float32
BF16 = jnp.bfloat16

EPS = 1e-6
GRID_W = 64
GLA_HEADS = 4
GLA_DK = 64
GLA_DV = 128
GLA_QK = GLA_HEADS * GLA_DK
GLA_V = GLA_HEADS * GLA_DV
GLA_RANK = 16
GLA_GATE_NORM = 16.0
SSD_HEADDIM = 64
SSD_INNER = 512
SSD_HEADS = 8
SSD_GROUPS = 2
SSD_HPG = 4
SSD_STATE = 128
SSD_CONV_DIM = 1024
N_EXPERTS = 32
TOP_K = 4
D_FF = 1024
SWIGLU_LIMIT = 7.0
SWIGLU_ALPHA = 1.702
MOE_BLOCK = 256

TB = 256
GLA_C = 64
SSD_C = 128
LANES = 128
EXP_CLAMP = 80.0
DT_F = 16
DT_B = 24
NEG_BIG = -1e30
VMEM_LIMIT = 56 * 1024 * 1024


def _dot(a, b):
    return jnp.dot(a, b, preferred_element_type=F32)


def _dot_nt(a, b):
    return lax.dot_general(a, b, (((1,), (1,)), ((), ())), preferred_element_type=F32)


def _dot_tn(a, b):
    return lax.dot_general(a, b, (((0,), (0,)), ((), ())), preferred_element_type=F32)


def _split3(a):
    hi = a.astype(BF16)
    r1 = a - hi.astype(F32)
    mid = r1.astype(BF16)
    lo = (r1 - mid.astype(F32)).astype(BF16)
    return hi, mid, lo


def _dot_exact_r(m, a):
    hi, mid, lo = _split3(a)
    return _dot(m, hi) + _dot(m, mid) + _dot(m, lo)


def _dot_exact_l(a, m):
    hi, mid, lo = _split3(a)
    return _dot(hi, m) + _dot(mid, m) + _dot(lo, m)


def _sigmoid(x):
    return 1.0 / (1.0 + jnp.exp(-x))


def _softplus(x):
    return jnp.maximum(x, 0.0) + jnp.log1p(jnp.exp(-jnp.abs(x)))


def _params(*sem):
    return pltpu.CompilerParams(dimension_semantics=sem, vmem_limit_bytes=VMEM_LIMIT)


def _mod_kernel(c_ref, w_ref, b_ref, o_ref):
    c = c_ref[...]
    s = c * _sigmoid(c)
    s_hi = s.astype(BF16)
    s_lo = (s - s_hi.astype(F32)).astype(BF16)
    w = w_ref[...]
    w_hi = w.astype(BF16)
    w_lo = (w - w_hi.astype(F32)).astype(BF16)
    o_ref[...] = _dot(s_hi, w_hi) + _dot(s_lo, w_hi) + _dot(s_hi, w_lo) + b_ref[...]


def _mod_call(cin, w, b):
    rows, d = cin.shape
    n = w.shape[1]
    tn = 1536
    return pl.pallas_call(
        _mod_kernel,
        grid=(n // tn,),
        in_specs=[pl.BlockSpec((rows, d), lambda i: (0, 0)),
                  pl.BlockSpec((d, tn), lambda i: (0, i)),
                  pl.BlockSpec((1, tn), lambda i: (0, i))],
        out_specs=pl.BlockSpec((rows, tn), lambda i: (0, i)),
        out_shape=jax.ShapeDtypeStruct((rows, n), F32),
        compiler_params=_params("arbitrary"),
    )(cin, w, b)


_C_Q, _C_K, _C_V, _C_G, _C_Z, _C_X, _C_M, _C_END = 0, 256, 512, 1024, 1536, 2048, 3072, 3200


def _inproj_kernel(x_ref, ctx_ref, sh_ref, sc_ref, n1_ref, w_ref, wup_ref, bup_ref, dtb_ref,
                   q_ref, k_ref, v_ref, g_ref, z_ref, xbc_ref, ld_ref, misc_ref, h_scr):
    j = pl.program_id(1)

    def normmod(xv):
        ms = jnp.mean(xv * xv, axis=-1, keepdims=True)
        y = xv * lax.rsqrt(ms + EPS) * n1_ref[...]
        return (y * (1.0 + sc_ref[0]) + sh_ref[0]).astype(BF16)

    @pl.when(j == 0)
    def _():
        h_scr[...] = normmod(ctx_ref[0])

    @pl.when(j > 0)
    def _():
        h_scr[...] = normmod(x_ref[0])

    h = h_scr[...]

    def mm(lo, hi):
        return _dot(h, w_ref[:, lo:hi])

    q_ref[0] = (mm(_C_Q, _C_K) * (GLA_DK ** -0.5)).astype(BF16)
    k_ref[0] = mm(_C_K, _C_V).astype(BF16)
    v_ref[0] = mm(_C_V, _C_G).astype(BF16)
    g_ref[0] = mm(_C_G, _C_Z)
    z_ref[0] = mm(_C_Z, _C_X)
    xbc_ref[0] = mm(_C_X, _C_M)
    m = mm(_C_M, _C_END)
    zz = _dot(m.astype(BF16), wup_ref[...]) + bup_ref[...]
    ld_ref[0] = -_softplus(-zz) * (1.0 / GLA_GATE_NORM)
    misc_ref[0] = _softplus(m + dtb_ref[...])


def _inproj_call(x, ctx, sh1, sc1, n1, w_cat, wup, bup, dtb):
    bsz, L, d = x.shape
    lc = ctx.shape[1]
    ls = lc + L
    nj = ls // TB
    tok = lambda n: pl.BlockSpec((1, TB, n), lambda b, j: (b, j, 0))
    const = lambda a: pl.BlockSpec(a.shape, lambda b, j: (0,) * a.ndim)
    modspec = pl.BlockSpec((1, 1, d), lambda b, j: (jnp.where(j == 0, bsz, b), 0, 0))
    outs = [(GLA_QK, BF16), (GLA_QK, BF16), (GLA_V, BF16), (GLA_V, F32), (SSD_INNER, F32),
            (SSD_CONV_DIM, F32), (2 * GLA_QK, F32), (LANES, F32)]
    return pl.pallas_call(
        _inproj_kernel,
        grid=(bsz, nj),
        in_specs=[pl.BlockSpec((1, TB, d), lambda b, j: (b, jnp.maximum(j - 1, 0), 0)),
                  pl.BlockSpec((1, TB, d), lambda b, j: (b, 0, 0)),
                  modspec, modspec, const(n1), const(w_cat), const(wup), const(bup), const(dtb)],
        out_specs=[tok(n) for n, _ in outs],
        out_shape=[jax.ShapeDtypeStruct((bsz, ls, n), dt) for n, dt in outs],
        scratch_shapes=[pltpu.VMEM((TB, d), BF16)],
        compiler_params=_params("arbitrary", "arbitrary"),
    )(x, ctx, sh1, sc1, n1, w_cat, wup, bup, dtb)


_EXT_PAD = 8
_EXT_BASE = _EXT_PAD + GRID_W
_EXT_ROWS = 2 * _EXT_PAD + 2 * GRID_W + TB


def _conv_kernel(prev_ref, cur_ref, next_ref, w_ref, b_ref, xs_ref, bc_ref, ext):
    j = pl.program_id(1)
    nj = pl.num_programs(1)
    is_ctx = j == 0
    zpad = jnp.zeros((_EXT_PAD, SSD_CONV_DIM), F32)
    ext[0:_EXT_PAD, :] = zpad
    ext[_EXT_ROWS - _EXT_PAD:_EXT_ROWS, :] = zpad
    ext[_EXT_PAD:_EXT_BASE, :] = jnp.where(j >= 2, prev_ref[0], 0.0)
    ext[_EXT_BASE:_EXT_BASE + TB, :] = cur_ref[0]
    ext[_EXT_BASE + TB:_EXT_BASE + TB + GRID_W, :] = jnp.where(
        jnp.logical_and(j >= 1, j <= nj - 2), next_ref[0], 0.0)

    t = lax.broadcasted_iota(jnp.int32, (TB, LANES), 0)
    col = t & (GRID_W - 1)
    ok_l = jnp.where(is_ctx, t, col) >= 1
    ok_r = jnp.where(is_ctx, t - (TB - GRID_W), col) <= GRID_W - 2
    lat = jnp.where(is_ctx, 0.0, 1.0)

    for c in range(SSD_CONV_DIM // LANES):
        lo, hi = c * LANES, (c + 1) * LANES
        acc = jnp.zeros((TB, LANES), F32)
        for dr in (-1, 0, 1):
            for dc in (-1, 0, 1):
                start = _EXT_BASE + GRID_W * dr + dc
                tap = ext[start:start + TB, lo:hi]
                wi = 3 * (dr + 1) + (dc + 1)
                wv = w_ref[wi:wi + 1, lo:hi]
                if dr != 0:
                    wv = wv * lat
                if dc == -1:
                    tap = jnp.where(ok_l, tap, 0.0)
                elif dc == 1:
                    tap = jnp.where(ok_r, tap, 0.0)
                acc = acc + tap * wv
        y = acc + b_ref[:, lo:hi]
        y = y * _sigmoid(y)
        if c < SSD_INNER // LANES:
            xs_ref[0, :, lo:hi] = y
        else:
            bc_ref[0, :, lo - SSD_INNER:hi - SSD_INNER] = y.astype(BF16)


def _conv_call(xbc, w9, bias):
    bsz, ls, ch = xbc.shape
    nj = ls // TB
    rpb = TB // GRID_W
    nrow = ls // GRID_W
    return pl.pallas_call(
        _conv_kernel,
        grid=(bsz, nj),
        in_specs=[pl.BlockSpec((1, GRID_W, ch), lambda b, j: (b, jnp.maximum(rpb * j - 1, 0), 0)),
                  pl.BlockSpec((1, TB, ch), lambda b, j: (b, j, 0)),
                  pl.BlockSpec((1, GRID_W, ch), lambda b, j: (b, jnp.minimum(rpb * j + rpb, nrow - 1), 0)),
                  pl.BlockSpec((9, ch), lambda b, j: (0, 0)),
                  pl.BlockSpec((1, ch), lambda b, j: (0, 0))],
        out_specs=[pl.BlockSpec((1, TB, SSD_INNER), lambda b, j: (b, j, 0)),
                   pl.BlockSpec((1, TB, ch - SSD_INNER), lambda b, j: (b, j, 0))],
        out_shape=[jax.ShapeDtypeStruct((bsz, ls, SSD_INNER), F32),
                   jax.ShapeDtypeStruct((bsz, ls, ch - SSD_INNER), BF16)],
        scratch_shapes=[pltpu.VMEM((_EXT_ROWS, ch), F32)],
        compiler_params=_params("arbitrary", "arbitrary"),
    )(xbc, xbc, xbc, w9, bias)


def _fwd_blk(s):
    return s


def _bwd_blk(s, ns):
    return jnp.where(s == 0, 0, ns - s)


def _gla_dir(q, k, v, la, st_ref, tri_m, fwd):
    C = GLA_C
    b = _dot_exact_r(tri_m, la)
    bt = b[C - 1:C, :] if fwd else b[0:1, :]
    r = 0.5 * bt
    qf = q.astype(F32)
    kf = k.astype(F32)
    qt = (qf * jnp.exp(jnp.minimum(b - r, EXP_CLAMP))).astype(BF16)
    kt = (kf * jnp.exp(jnp.minimum(r - b, EXP_CLAMP))).astype(BF16)
    head_k = lax.broadcasted_iota(jnp.int32, (C, GLA_QK), 1) >> 6
    kst = jnp.concatenate([jnp.where(head_k == h, kt, jnp.zeros_like(kt)) for h in range(GLA_HEADS)], axis=0)
    sc = _dot_nt(qt, kst)
    ii = lax.broadcasted_iota(jnp.int32, (C, GLA_HEADS * C), 0)
    jj = lax.broadcasted_iota(jnp.int32, (C, GLA_HEADS * C), 1) & (C - 1)
    causal = (jj <= ii) if fwd else (jj >= ii)
    p = jnp.where(causal, sc, 0.0).astype(BF16)
    head_v = lax.broadcasted_iota(jnp.int32, (C, GLA_V), 1) >> 7
    vst = jnp.concatenate([jnp.where(head_v == h, v, jnp.zeros_like(v)) for h in range(GLA_HEADS)], axis=0)
    o = _dot(p, vst)
    st = st_ref[...]
    qd = (qf * jnp.exp(b)).astype(BF16)
    o = o + _dot_nt(qd, st.astype(BF16))
    kd = (kf * jnp.exp(bt - b)).astype(BF16)
    u = _dot_tn(v, kd)
    rv = lax.broadcasted_iota(jnp.int32, (GLA_V, GLA_QK), 0) >> 7
    cv = lax.broadcasted_iota(jnp.int32, (GLA_V, GLA_QK), 1) >> 6
    st_ref[...] = st * jnp.exp(bt) + jnp.where(rv == cv, u, 0.0)
    return o


def _gla_kernel(qf_ref, kf_ref, vf_ref, lf_ref, qb_ref, kb_ref, vb_ref, lb_ref, trif_ref, trib_ref,
                of_ref, ob_ref, stf, stb):
    s = pl.program_id(1)

    @pl.when(s == 0)
    def _():
        stf[...] = jnp.zeros_like(stf)
        stb[...] = jnp.zeros_like(stb)

    nsub = TB // GLA_C

    def body(i, carry):
        rf = pl.multiple_of(i * GLA_C, GLA_C)
        rb = pl.multiple_of((nsub - 1 - i) * GLA_C, GLA_C)
        sf = pl.ds(rf, GLA_C)
        sb = pl.ds(rb, GLA_C)
        of_ref[0, sf, :] = _gla_dir(qf_ref[0, sf, :], kf_ref[0, sf, :], vf_ref[0, sf, :], lf_ref[0, sf, :],
                                    stf, trif_ref[...], True)
        ob_ref[0, sb, :] = _gla_dir(qb_ref[0, sb, :], kb_ref[0, sb, :], vb_ref[0, sb, :], lb_ref[0, sb, :],
                                    stb, trib_ref[...], False)
        return carry

    lax.fori_loop(0, nsub, body, 0)


def _gla_call(q, k, v, ld, L):
    bsz, ls, _ = q.shape
    ns = ls // TB
    nx = L // TB
    trif = jnp.asarray(np.tril(np.ones((GLA_C, GLA_C), np.float32)), BF16)
    trib = jnp.asarray(np.triu(np.ones((GLA_C, GLA_C), np.float32)), BF16)
    f = lambda n, lane=0: pl.BlockSpec((1, TB, n), lambda b, s: (b, _fwd_blk(s), lane))
    r = lambda n, lane=0: pl.BlockSpec((1, TB, n), lambda b, s: (b, _bwd_blk(s, ns), lane))
    tri = pl.BlockSpec((GLA_C, GLA_C), lambda b, s: (0, 0))
    return pl.pallas_call(
        _gla_kernel,
        grid=(bsz, ns),
        in_specs=[f(GLA_QK), f(GLA_QK), f(GLA_V), f(GLA_QK, 0),
                  r(GLA_QK), r(GLA_QK), r(GLA_V), r(GLA_QK, 1), tri, tri],
        out_specs=[pl.BlockSpec((1, TB, GLA_V), lambda b, s: (b, jnp.maximum(s - 1, 0), 0)),
                   pl.BlockSpec((1, TB, GLA_V), lambda b, s: (b, jnp.where(s == 0, nx - 1, nx - s), 0))],
        out_shape=[jax.ShapeDtypeStruct((bsz, L, GLA_V), F32)] * 2,
        scratch_shapes=[pltpu.VMEM((GLA_V, GLA_QK), F32)] * 2,
        compiler_params=_params("arbitrary", "arbitrary"),
    )(q, k, v, ld, q, k, v, ld, trif, trib)


def _ssd_dir(xs, bc, dtm, avec, dvec, st_ref, tri_m, e_m, base, fwd):
    C = SSD_C
    dt_exp = _dot_exact_l(dtm, e_m)
    a = dtm * avec
    acum = _dot_exact_r(tri_m, a)
    acum_exp = _dot_exact_l(acum, e_m)
    acum_t = acum.T
    xdt = xs * dt_exp
    xdt_b = xdt.astype(BF16)
    ii = lax.broadcasted_iota(jnp.int32, (C, C), 0)
    jj = lax.broadcasted_iota(jnp.int32, (C, C), 1)
    tri = (jj <= ii) if fwd else (jj >= ii)
    al_exp = acum_exp[C - 1:C, :] if fwd else acum_exp[0:1, :]
    head2 = lax.broadcasted_iota(jnp.int32, (C, 256), 1) >> 6
    ys = []
    for g in range(SSD_GROUPS):
        gl, gh = 256 * g, 256 * (g + 1)
        bg = bc[:, 128 * g:128 * (g + 1)]
        cg = bc[:, 256 + 128 * g:256 + 128 * (g + 1)]
        cb = _dot_nt(cg, bg)
        ms = []
        for rr in range(SSD_HPG):
            ln = base + SSD_HPG * g + rr
            diff = acum[:, ln:ln + 1] - acum_t[ln:ln + 1, :]
            seg = jnp.where(tri, jnp.exp(jnp.minimum(diff, 0.0)), 0.0)
            ms.append((cb * seg).astype(BF16))
        res = _dot(jnp.concatenate(ms, axis=0), xdt_b[:, gl:gh])
        yg = jnp.zeros((C, 256), F32)
        for rr in range(SSD_HPG):
            yg = yg + jnp.where(head2 == rr, res[rr * C:(rr + 1) * C, :], 0.0)
        sg = st_ref[g]
        yoff = _dot(cg, sg.astype(BF16)) * jnp.exp(acum_exp[:, gl:gh])
        wst = jnp.exp(al_exp[:, gl:gh] - acum_exp[:, gl:gh])
        ug = _dot_tn(bg, (xdt[:, gl:gh] * wst).astype(BF16))
        st_ref[g] = sg * jnp.exp(al_exp[:, gl:gh]) + ug
        ys.append(yg + yoff)
    y = jnp.concatenate(ys, axis=1)
    if dvec is not None:
        y = y + dvec * xs
    return y


def _ssd_kernel(xf_ref, bcf_ref, mf_ref, xb_ref, bcb_ref, mb_ref, af_ref, ab_ref, d_ref,
                trif_ref, trib_ref, ef_ref, eb_ref, yf_ref, yb_ref, stf, stb):
    s = pl.program_id(1)

    @pl.when(s == 0)
    def _():
        stf[...] = jnp.zeros_like(stf)
        stb[...] = jnp.zeros_like(stb)

    nsub = TB // SSD_C

    def body(i, carry):
        rf = pl.multiple_of(i * SSD_C, SSD_C)
        rb = pl.multiple_of((nsub - 1 - i) * SSD_C, SSD_C)
        sf = pl.ds(rf, SSD_C)
        sb = pl.ds(rb, SSD_C)
        yf_ref[0, sf, :] = _ssd_dir(xf_ref[0, sf, :], bcf_ref[0, sf, :], mf_ref[0, sf, :], af_ref[...], d_ref[...],
                                    stf, trif_ref[...], ef_ref[...], DT_F, True)
        yb_ref[0, sb, :] = _ssd_dir(xb_ref[0, sb, :], bcb_ref[0, sb, :], mb_ref[0, sb, :], ab_ref[...], None,
                                    stb, trib_ref[...], eb_ref[...], DT_B, False)
        return carry

    lax.fori_loop(0, nsub, body, 0)


def _expand_matrix(base):
    e = np.zeros((LANES, SSD_INNER), np.float32)
    for h in range(SSD_HEADS):
        e[base + h, SSD_HEADDIM * h:SSD_HEADDIM * (h + 1)] = 1.0
    return jnp.asarray(e, BF16)


def _ssd_call(xs, bc, misc, a_f, a_b, dvec, L):
    bsz, ls, _ = xs.shape
    ns = ls // TB
    nx = L // TB
    trif = jnp.asarray(np.tril(np.ones((SSD_C, SSD_C), np.float32)), BF16)
    trib = jnp.asarray(np.triu(np.ones((SSD_C, SSD_C), np.float32)), BF16)
    ef, eb = _expand_matrix(DT_F), _expand_matrix(DT_B)
    f = lambda n: pl.BlockSpec((1, TB, n), lambda b, s: (b, _fwd_blk(s), 0))
    r = lambda n: pl.BlockSpec((1, TB, n), lambda b, s: (b, _bwd_blk(s, ns), 0))
    const = lambda a: pl.BlockSpec(a.shape, lambda b, s: (0,) * a.ndim)
    return pl.pallas_call(
        _ssd_kernel,
        grid=(bsz, ns),
        in_specs=[f(SSD_INNER), f(512), f(LANES), r(SSD_INNER), r(512), r(LANES),
                  const(a_f), const(a_b), const(dvec), const(trif), const(trib), const(ef), const(eb)],
        out_specs=[pl.BlockSpec((1, TB, SSD_INNER), lambda b, s: (b, jnp.maximum(s - 1, 0), 0)),
                   pl.BlockSpec((1, TB, SSD_INNER), lambda b, s: (b, jnp.where(s == 0, nx - 1, nx - s), 0))],
        out_shape=[jax.ShapeDtypeStruct((bsz, L, SSD_INNER), F32)] * 2,
        scratch_shapes=[pltpu.VMEM((SSD_GROUPS, SSD_STATE, SSD_HPG * SSD_HEADDIM), F32)] * 2,
        compiler_params=_params("arbitrary", "arbitrary"),
    )(xs, bc, misc, xs, bc, misc, a_f, a_b, dvec, trif, trib, ef, eb)


def _outproj_kernel(of_ref, ob_ref, g_ref, yf_ref, yb_ref, z_ref, x_ref, g1_ref, sh2_ref, sc2_ref,
                    gn_ref, sn_ref, wo_ref, n2_ref, wrh_ref, wrl_ref, br_ref,
                    x1_ref, h2_ref, eidx_ref, gate_ref, cnt_ref):
    first = jnp.logical_and(pl.program_id(0) == 0, pl.program_id(1) == 0)

    @pl.when(first)
    def _():
        cnt_ref[...] = jnp.zeros_like(cnt_ref)

    o = of_ref[0] + ob_ref[0]
    gg = g_ref[0]
    parts = []
    for h in range(GLA_HEADS):
        lo, hi = GLA_DV * h, GLA_DV * (h + 1)
        oh = o[:, lo:hi]
        ms = jnp.mean(oh * oh, axis=-1, keepdims=True)
        gh = gg[:, lo:hi]
        parts.append((oh * lax.rsqrt(ms + EPS) * gn_ref[:, lo:hi] * (gh * _sigmoid(gh))).astype(BF16))
    zz = z_ref[0]
    u = (yf_ref[0] + yb_ref[0]) * (zz * _sigmoid(zz))
    gw = SSD_INNER // SSD_GROUPS
    for g in range(SSD_GROUPS):
        lo, hi = gw * g, gw * (g + 1)
        ug = u[:, lo:hi]
        ms = jnp.mean(ug * ug, axis=-1, keepdims=True)
        parts.append((ug * lax.rsqrt(ms + EPS) * sn_ref[:, lo:hi]).astype(BF16))
    mix = jnp.concatenate(parts, axis=1)
    x1 = x_ref[0] + g1_ref[0] * _dot(mix, wo_ref[...])
    x1_ref[0] = x1
    ms = jnp.mean(x1 * x1, axis=-1, keepdims=True)
    h2 = (x1 * lax.rsqrt(ms + EPS) * n2_ref[...]) * (1.0 + sc2_ref[0]) + sh2_ref[0]
    h2_ref[0] = h2
    h_hi = h2.astype(BF16)
    h_lo = (h2 - h_hi.astype(F32)).astype(BF16)
    logits = (_dot(h_hi, wrh_ref[...]) + _dot(h_lo, wrh_ref[...]) + _dot(h_hi, wrl_ref[...])) + br_ref[...]
    lane = lax.broadcasted_iota(jnp.int32, (TB, LANES), 1)
    work = logits
    eidx = jnp.full((TB, LANES), -1, jnp.int32)
    gates = jnp.zeros((TB, LANES), F32)
    sel = jnp.zeros((TB, LANES), F32)
    m0 = None
    den = jnp.zeros((TB, 1), F32)
    for kk in range(TOP_K):
        m = jnp.max(work, axis=-1, keepdims=True)
        idx = jnp.min(jnp.where(work == m, lane, LANES), axis=-1, keepdims=True)
        hit = lane == idx
        if m0 is None:
            m0 = m
        e = jnp.exp(m - m0)
        den = den + e
        eidx = jnp.where(lane == kk, idx, eidx)
        gates = jnp.where(lane == kk, e, gates)
        sel = jnp.where(hit, 1.0, sel)
        work = jnp.where(hit, NEG_BIG, work)
    eidx_ref[0] = eidx
    gate_ref[0] = gates / den
    cnt_ref[...] += jnp.sum(sel, axis=0, keepdims=True)


def _outproj_call(o_f, o_b, g_all, y_f, y_b, z_all, x, g1, sh2, sc2, gn, sn, wo, n2, wr_hi, wr_lo, br):
    bsz, L, d = x.shape
    nj = L // TB
    tok = lambda n: pl.BlockSpec((1, TB, n), lambda b, j: (b, j, 0))
    tok_off = lambda n: pl.BlockSpec((1, TB, n), lambda b, j: (b, j + 1, 0))
    const = lambda a: pl.BlockSpec(a.shape, lambda b, j: (0,) * a.ndim)
    mod = pl.BlockSpec((1, 1, d), lambda b, j: (b, 0, 0))
    return pl.pallas_call(
        _outproj_kernel,
        grid=(bsz, nj),
        in_specs=[tok(GLA_V), tok(GLA_V), tok_off(GLA_V), tok(SSD_INNER), tok(SSD_INNER), tok_off(SSD_INNER),
                  tok(d), mod, mod, mod, const(gn), const(sn), const(wo), const(n2),
                  const(wr_hi), const(wr_lo), const(br)],
        out_specs=[tok(d), tok(d), tok(LANES), tok(LANES), pl.BlockSpec((1, LANES), lambda b, j: (0, 0))],
        out_shape=[jax.ShapeDtypeStruct((bsz, L, d), F32), jax.ShapeDtypeStruct((bsz, L, d), F32),
                   jax.ShapeDtypeStruct((bsz, L, LANES), jnp.int32), jax.ShapeDtypeStruct((bsz, L, LANES), F32),
                   jax.ShapeDtypeStruct((1, LANES), F32)],
        compiler_params=_params("arbitrary", "arbitrary"),
    )(o_f, o_b, g_all, y_f, y_b, z_all, x, g1, sh2, sc2, gn, sn, wo, n2, wr_hi, wr_lo, br)


def _pos_kernel(eidx_ref, pstart_ref, lst_ref, dest_ref, carry):
    @pl.when(pl.program_id(0) == 0)
    def _():
        carry[...] = pstart_ref[...]

    eidx = eidx_ref[...]
    lane = lax.broadcasted_iota(jnp.int32, (TB, LANES), 1)
    hits = [lane == eidx[:, kk:kk + 1] for kk in range(TOP_K)]
    sel = jnp.zeros((TB, LANES), F32)
    for hmask in hits:
        sel = jnp.where(hmask, 1.0, sel)
    pos = carry[...] + _dot(lst_ref[...], sel.astype(BF16))
    dest = jnp.zeros((TB, LANES), jnp.int32)
    for kk, hmask in enumerate(hits):
        dk = jnp.sum(jnp.where(hmask, pos, 0.0), axis=-1, keepdims=True)
        dest = jnp.where(lane == kk, dk.astype(jnp.int32), dest)
    dest_ref[...] = dest
    carry[...] += jnp.sum(sel, axis=0, keepdims=True)


def _pos_call(eidx, pstart):
    T = eidx.shape[0]
    lst = jnp.asarray(np.tril(np.ones((TB, TB), np.float32), -1), BF16)
    return pl.pallas_call(
        _pos_kernel,
        grid=(T // TB,),
        in_specs=[pl.BlockSpec((TB, LANES), lambda i: (i, 0)),
                  pl.BlockSpec((1, LANES), lambda i: (0, 0)),
                  pl.BlockSpec((TB, TB), lambda i: (0, 0))],
        out_specs=pl.BlockSpec((TB, LANES), lambda i: (i, 0)),
        out_shape=jax.ShapeDtypeStruct((T, LANES), jnp.int32),
        scratch_shapes=[pltpu.VMEM((1, LANES), F32)],
        compiler_params=_params("arbitrary"),
    )(eidx, pstart, lst)


def _dispatch_kernel(dest_ref, h_ref, xg_in_ref, xg_ref, sem):
    del xg_in_ref

    def body(r, carry):
        for kk in range(TOP_K):
            d = dest_ref[0, 0, TOP_K * r + kk]
            pltpu.make_async_copy(h_ref.at[pl.ds(r, 1), :], xg_ref.at[pl.ds(d, 1), :], sem).start()
        return carry

    lax.fori_loop(0, TB, body, 0)
    for kk in range(TOP_K):
        pltpu.make_async_copy(h_ref, xg_ref.at[pl.ds(0, TB), :], sem).wait()


def _dispatch_call(dest3, h2, xg0):
    T, d = h2.shape
    return pl.pallas_call(
        _dispatch_kernel,
        grid=(T // TB,),
        in_specs=[pl.BlockSpec((1, 1, TB * TOP_K), lambda i: (i, 0, 0), memory_space=pltpu.SMEM),
                  pl.BlockSpec((TB, d), lambda i: (i, 0)),
                  pl.BlockSpec(memory_space=pl.ANY)],
        out_specs=pl.BlockSpec(memory_space=pl.ANY),
        out_shape=jax.ShapeDtypeStruct(xg0.shape, xg0.dtype),
        scratch_shapes=[pltpu.SemaphoreType.DMA(())],
        input_output_aliases={2: 0},
        compiler_params=_params("arbitrary"),
    )(dest3, h2, xg0)


def _moe_kernel(be_ref, x_ref, wgu_ref, bgu_ref, wd_ref, bd_ref, y_ref, wgu_b, wd_b):
    i = pl.program_id(0)
    changed = jnp.logical_or(i == 0, be_ref[i] != be_ref[jnp.maximum(i - 1, 0)])

    @pl.when(changed)
    def _():
        wgu_b[...] = wgu_ref[0].astype(BF16)
        wd_b[...] = wd_ref[0].astype(BF16)

    xb = x_ref[...].astype(BF16)
    acc = jnp.zeros((MOE_BLOCK, wd_b.shape[1]), F32) + bd_ref[0]
    cw = 256
    for c in range(D_FF // cw):
        lo, hi = c * cw, (c + 1) * cw
        gate = _dot(xb, wgu_b[:, lo:hi]) + bgu_ref[0, :, lo:hi]
        up = _dot(xb, wgu_b[:, D_FF + lo:D_FF + hi]) + bgu_ref[0, :, D_FF + lo:D_FF + hi]
        gate = jnp.minimum(gate, SWIGLU_LIMIT)
        up = jnp.clip(up, -SWIGLU_LIMIT, SWIGLU_LIMIT)
        act = ((up + 1.0) * (gate * _sigmoid(SWIGLU_ALPHA * gate))).astype(BF16)
        acc = acc + _dot(act, wd_b[lo:hi, :])
    y_ref[...] = acc


def _moe_call(blk_e, xg, wgu, bgu, wd, bd):
    P, d = xg.shape
    nb = P // MOE_BLOCK
    ne, _, f2 = wgu.shape
    grid_spec = pltpu.PrefetchScalarGridSpec(
        num_scalar_prefetch=1,
        grid=(nb,),
        in_specs=[pl.BlockSpec((MOE_BLOCK, d), lambda i, be: (i, 0)),
                  pl.BlockSpec((1, d, f2), lambda i, be: (be[i], 0, 0)),
                  pl.BlockSpec((1, 1, f2), lambda i, be: (be[i], 0, 0)),
                  pl.BlockSpec((1, f2 // 2, d), lambda i, be: (be[i], 0, 0)),
                  pl.BlockSpec((1, 1, d), lambda i, be: (be[i], 0, 0))],
        out_specs=pl.BlockSpec((MOE_BLOCK, d), lambda i, be: (i, 0)),
        scratch_shapes=[pltpu.VMEM((d, f2), BF16), pltpu.VMEM((f2 // 2, d), BF16)],
    )
    return pl.pallas_call(
        _moe_kernel,
        grid_spec=grid_spec,
        out_shape=jax.ShapeDtypeStruct((P, d), F32),
        compiler_params=_params("arbitrary"),
    )(blk_e, xg, wgu, bgu, wd, bd)


def _combine_kernel(dest_ref, y_ref, x1_ref, gate_ref, g2_ref, fn_ref, o_ref, buf, sem):
    def body(r, carry):
        for kk in range(TOP_K):
            d = dest_ref[0, 0, TOP_K * r + kk]
            pltpu.make_async_copy(y_ref.at[pl.ds(d, 1), :], buf.at[kk, pl.ds(r, 1), :], sem).start()
        return carry

    lax.fori_loop(0, TB, body, 0)
    for kk in range(TOP_K):
        pltpu.make_async_copy(y_ref.at[pl.ds(0, TB), :], buf.at[kk], sem).wait()

    gates = gate_ref[0]
    f = jnp.zeros(o_ref.shape[1:], F32)
    for kk in range(TOP_K):
        f = f + gates[:, kk:kk + 1] * buf[kk]
    xo = x1_ref[0] + g2_ref[0] * f
    ms = jnp.mean(xo * xo, axis=-1, keepdims=True)
    o_ref[0] = xo * lax.rsqrt(ms + EPS) * fn_ref[...]


def _combine_call(dest3, y, x1, gates, g2, fn):
    bsz, L, d = x1.shape
    nj = L // TB
    return pl.pallas_call(
        _combine_kernel,
        grid=(bsz, nj),
        in_specs=[pl.BlockSpec((1, 1, TB * TOP_K), lambda b, j: (b * nj + j, 0, 0), memory_space=pltpu.SMEM),
                  pl.BlockSpec(memory_space=pl.ANY),
                  pl.BlockSpec((1, TB, d), lambda b, j: (b, j, 0)),
                  pl.BlockSpec((1, TB, LANES), lambda b, j: (b, j, 0)),
                  pl.BlockSpec((1, 1, d), lambda b, j: (b, 0, 0)),
                  pl.BlockSpec((1, d), lambda b, j: (0, 0))],
        out_specs=pl.BlockSpec((1, TB, d), lambda b, j: (b, j, 0)),
        out_shape=jax.ShapeDtypeStruct((bsz, L, d), F32),
        scratch_shapes=[pltpu.VMEM((TOP_K, TB, d), F32), pltpu.SemaphoreType.DMA(())],
        compiler_params=_params("arbitrary", "arbitrary"),
    )(dest3, y, x1, gates, g2, fn)


def _layer(x, c, ctx, c_ctx, w_mod, b_mod, norm1, w_in, gla_w_gk_up, gla_b_gk, gla_norm,
           ssd_conv_w, ssd_conv_b, ssd_dt_bias, ssd_A_log, ssd_D, ssd_norm, w_out,
           norm2, w_router, b_router, w_gate_up, b_gate_up, w_down, b_down, final_norm):
    bsz, L, d = x.shape
    lc = ctx.shape[1]
    assert lc == TB and L % TB == 0 and TB % GRID_W == 0

    cin = jnp.zeros((8, d), F32).at[:bsz].set(c).at[bsz].set(c_ctx)
    mod = _mod_call(cin, w_mod, b_mod.reshape(1, -1))[:bsz + 1]
    sh1, sc1, g1, sh2, sc2, g2 = [m.reshape(bsz + 1, 1, d) for m in jnp.split(mod, 6, axis=-1)]

    o = np.cumsum((0, GLA_QK, GLA_QK, GLA_V, GLA_V, GLA_RANK, SSD_INNER, SSD_CONV_DIM, SSD_HEADS))
    wq, wk, wv, wg, wlow, wz, wx, wdt = [w_in[:, int(a):int(b)] for a, b in zip(o[:-1], o[1:])]
    w_misc = jnp.concatenate([wlow, wdt, wdt, jnp.zeros((d, LANES - GLA_RANK - 2 * SSD_HEADS), F32)], axis=1)
    w_cat = jnp.concatenate([wq, wk, wv, wg, wz, wx, w_misc], axis=1).astype(BF16)
    wup = jnp.zeros((LANES, 2 * GLA_QK), F32).at[:GLA_RANK].set(
        jnp.concatenate([gla_w_gk_up[0], gla_w_gk_up[1]], axis=1)).astype(BF16)
    bup = jnp.concatenate([gla_b_gk[0], gla_b_gk[1]]).reshape(1, -1)
    dtb = jnp.zeros((1, LANES), F32).at[0, DT_F:DT_F + SSD_HEADS].set(ssd_dt_bias[0]) \
                                    .at[0, DT_B:DT_B + SSD_HEADS].set(ssd_dt_bias[1])
    q, k, v, g_all, z_all, xbc, ld, misc = _inproj_call(
        x, ctx, sh1, sc1, norm1.reshape(1, d), w_cat, wup, bup, dtb)

    xs, bc = _conv_call(xbc, ssd_conv_w.reshape(9, SSD_CONV_DIM), ssd_conv_b.reshape(1, -1))

    o_f, o_b = _gla_call(q, k, v, ld, L)

    a_neg = -jnp.exp(ssd_A_log.astype(F32))
    a_f = jnp.zeros((1, LANES), F32).at[0, DT_F:DT_F + SSD_HEADS].set(a_neg[0])
    a_b = jnp.zeros((1, LANES), F32).at[0, DT_B:DT_B + SSD_HEADS].set(a_neg[1])
    dvec = jnp.repeat(ssd_D, SSD_HEADDIM).reshape(1, SSD_INNER)
    y_f, y_b = _ssd_call(xs, bc, misc, a_f, a_b, dvec, L)

    wr = jnp.zeros((d, LANES), F32).at[:, :N_EXPERTS].set(w_router)
    wr_hi = wr.astype(BF16)
    wr_lo = (wr - wr_hi.astype(F32)).astype(BF16)
    br = jnp.full((1, LANES), NEG_BIG, F32).at[0, :N_EXPERTS].set(b_router)
    x1, h2, eidx, gates, counts = _outproj_call(
        o_f, o_b, g_all, y_f, y_b, z_all, x, g1[:bsz], sh2[:bsz], sc2[:bsz],
        jnp.tile(gla_norm, GLA_HEADS).reshape(1, -1), ssd_norm.reshape(1, -1), w_out.astype(BF16),
        norm2.reshape(1, d), wr_hi, wr_lo, br)

    T = bsz * L
    cnt = counts[0, :N_EXPERTS].astype(jnp.int32)
    padded = ((cnt + MOE_BLOCK - 1) // MOE_BLOCK) * MOE_BLOCK
    pend = jnp.cumsum(padded)
    pstart = pend - padded
    n_blocks = -(-(T * TOP_K) // MOE_BLOCK) + N_EXPERTS
    blk_e = jnp.minimum(jnp.searchsorted(pend, jnp.arange(n_blocks, dtype=jnp.int32) * MOE_BLOCK, side='right'),
                        N_EXPERTS - 1).astype(jnp.int32)
    pstart_row = jnp.zeros((1, LANES), F32).at[0, :N_EXPERTS].set(pstart.astype(F32))
    dest = _pos_call(eidx.reshape(T, LANES), pstart_row)
    dest3 = dest[:, :TOP_K].reshape(T // TB, 1, TB * TOP_K)

    P = n_blocks * MOE_BLOCK
    xg = _dispatch_call(dest3, h2.reshape(T, d), jnp.zeros((P, d), F32))
    y = _moe_call(blk_e, xg, w_gate_up, b_gate_up.reshape(N_EXPERTS, 1, -1), w_down,
                  b_down.reshape(N_EXPERTS, 1, -1))
    return _combine_call(dest3, y, x1, gates, g2[:bsz], final_norm.reshape(1, d))


def kernel(x, c, ctx, c_ctx, w_mod, b_mod, norm1, w_in, gla_w_gk_up, gla_b_gk, gla_norm, ssd_conv_w, ssd_conv_b, ssd_dt_bias, ssd_A_log, ssd_D, ssd_norm, w_out, norm2, w_router, b_router, w_gate_up, b_gate_up, w_down, b_down, final_norm):
    assert w_mod.shape[0] == 1, "single-layer kernel"
    return _layer(x, c, ctx, c_ctx, w_mod[0], b_mod[0], norm1[0], w_in[0], gla_w_gk_up[0], gla_b_gk[0],
                  gla_norm[0], ssd_conv_w[0], ssd_conv_b[0], ssd_dt_bias[0], ssd_A_log[0], ssd_D[0],
                  ssd_norm[0], w_out[0], norm2[0], w_router[0], b_router[0], w_gate_up[0], b_gate_up[0],
                  w_down[0], b_down[0], final_norm)
```

```python
import functools

import numpy as np
import jax
import jax.numpy as jnp
from jax import lax
from jax.experimental import pallas as pl
from jax.experimental.pallas import tpu as pltpu

F32 = jnp.float32
BF16 = jnp.bfloat16

EPS = 1e-6
GRID_W = 64
GLA_HEADS = 4
GLA_DK = 64
GLA_DV = 128
GLA_QK = GLA_HEADS * GLA_DK
GLA_V = GLA_HEADS * GLA_DV
GLA_RANK = 16
GLA_GATE_NORM = 16.0
SSD_HEADDIM = 64
SSD_INNER = 512
SSD_HEADS = 8
SSD_GROUPS = 2
SSD_HPG = 4
SSD_STATE = 128
SSD_CONV_DIM = 1024
N_EXPERTS = 32
TOP_K = 4
D_FF = 1024
SWIGLU_LIMIT = 7.0
SWIGLU_ALPHA = 1.702
MOE_BLOCK = 256

TB = 256
GLA_C = 64
SSD_C = 128
LANES = 128
EXP_CLAMP = 80.0
DT_F = 16
DT_B = 24
NEG_BIG = -1e30
VMEM_LIMIT = 56 * 1024 * 1024


def _dot(a, b):
    return jnp.dot(a, b, preferred_element_type=F32)


def _dot_nt(a, b):
    return lax.dot_general(a, b, (((1,), (1,)), ((), ())), preferred_element_type=F32)


def _dot_tn(a, b):
    return lax.dot_general(a, b, (((0,), (0,)), ((), ())), preferred_element_type=F32)


def _split3(a):
    hi = a.astype(BF16)
    r1 = a - hi.astype(F32)
    mid = r1.astype(BF16)
    lo = (r1 - mid.astype(F32)).astype(BF16)
    return hi, mid, lo


def _dot_exact_r(m, a):
    hi, mid, lo = _split3(a)
    return _dot(m, hi) + _dot(m, mid) + _dot(m, lo)


def _dot_exact_l(a, m):
    hi, mid, lo = _split3(a)
    return _dot(hi, m) + _dot(mid, m) + _dot(lo, m)


def _sigmoid(x):
    return 1.0 / (1.0 + jnp.exp(-x))


def _softplus(x):
    return jnp.maximum(x, 0.0) + jnp.log1p(jnp.exp(-jnp.abs(x)))


def _params(*sem):
    return pltpu.CompilerParams(dimension_semantics=sem, vmem_limit_bytes=VMEM_LIMIT)


def _mod_kernel(c_ref, w_ref, b_ref, o_ref):
    c = c_ref[...]
    s = c * _sigmoid(c)
    s_hi = s.astype(BF16)
    s_lo = (s - s_hi.astype(F32)).astype(BF16)
    w = w_ref[...]
    w_hi = w.astype(BF16)
    w_lo = (w - w_hi.astype(F32)).astype(BF16)
    o_ref[...] = _dot(s_hi, w_hi) + _dot(s_lo, w_hi) + _dot(s_hi, w_lo) + b_ref[...]


def _mod_call(cin, w, b):
    rows, d = cin.shape
    n = w.shape[1]
    tn = 1536
    return pl.pallas_call(
        _mod_kernel,
        grid=(n // tn,),
        in_specs=[pl.BlockSpec((rows, d), lambda i: (0, 0)),
                  pl.BlockSpec((d, tn), lambda i: (0, i)),
                  pl.BlockSpec((1, tn), lambda i: (0, i))],
        out_specs=pl.BlockSpec((rows, tn), lambda i: (0, i)),
        out_shape=jax.ShapeDtypeStruct((rows, n), F32),
        compiler_params=_params("arbitrary"),
    )(cin, w, b)


_C_Q, _C_K, _C_V, _C_G, _C_Z, _C_X, _C_M, _C_END = 0, 256, 512, 1024, 1536, 2048, 3072, 3200


def _inproj_kernel(x_ref, ctx_ref, sh_ref, sc_ref, n1_ref, w_ref, wup_ref, bup_ref, dtb_ref,
                   q_ref, k_ref, v_ref, g_ref, z_ref, xbc_ref, ld_ref, misc_ref, h_scr):
    j = pl.program_id(1)

    def normmod(xv):
        ms = jnp.mean(xv * xv, axis=-1, keepdims=True)
        y = xv * lax.rsqrt(ms + EPS) * n1_ref[...]
        return (y * (1.0 + sc_ref[0]) + sh_ref[0]).astype(BF16)

    @pl.when(j == 0)
    def _():
        h_scr[...] = normmod(ctx_ref[0])

    @pl.when(j > 0)
    def _():
        h_scr[...] = normmod(x_ref[0])

    h = h_scr[...]

    def mm(lo, hi):
        return _dot(h, w_ref[:, lo:hi])

    q_ref[0] = (mm(_C_Q, _C_K) * (GLA_DK ** -0.5)).astype(BF16)
    k_ref[0] = mm(_C_K, _C_V).astype(BF16)
    v_ref[0] = mm(_C_V, _C_G).astype(BF16)
    g_ref[0] = mm(_C_G, _C_Z)
    z_ref[0] = mm(_C_Z, _C_X)
    xbc_ref[0] = mm(_C_X, _C_M)
    m = mm(_C_M, _C_END)
    zz = _dot(m.astype(BF16), wup_ref[...]) + bup_ref[...]
    ld_ref[0] = -_softplus(-zz) * (1.0 / GLA_GATE_NORM)
    misc_ref[0] = _softplus(m + dtb_ref[...])


def _inproj_call(x, ctx, sh1, sc1, n1, w_cat, wup, bup, dtb):
    bsz, L, d = x.shape
    lc = ctx.shape[1]
    ls = lc + L
    nj = ls // TB
    tok = lambda n: pl.BlockSpec((1, TB, n), lambda b, j: (b, j, 0))
    const = lambda a: pl.BlockSpec(a.shape, lambda b, j: (0,) * a.ndim)
    modspec = pl.BlockSpec((1, 1, d), lambda b, j: (jnp.where(j == 0, bsz, b), 0, 0))
    outs = [(GLA_QK, BF16), (GLA_QK, BF16), (GLA_V, BF16), (GLA_V, F32), (SSD_INNER, F32),
            (SSD_CONV_DIM, F32), (2 * GLA_QK, F32), (LANES, F32)]
    return pl.pallas_call(
        _inproj_kernel,
        grid=(bsz, nj),
        in_specs=[pl.BlockSpec((1, TB, d), lambda b, j: (b, jnp.maximum(j - 1, 0), 0)),
                  pl.BlockSpec((1, TB, d), lambda b, j: (b, 0, 0)),
                  modspec, modspec, const(n1), const(w_cat), const(wup), const(bup), const(dtb)],
        out_specs=[tok(n) for n, _ in outs],
        out_shape=[jax.ShapeDtypeStruct((bsz, ls, n), dt) for n, dt in outs],
        scratch_shapes=[pltpu.VMEM((TB, d), BF16)],
        compiler_params=_params("arbitrary", "arbitrary"),
    )(x, ctx, sh1, sc1, n1, w_cat, wup, bup, dtb)


_EXT_PAD = 8
_EXT_BASE = _EXT_PAD + GRID_W
_EXT_ROWS = 2 * _EXT_PAD + 2 * GRID_W + TB


def _conv_kernel(prev_ref, cur_ref, next_ref, w_ref, b_ref, xs_ref, bc_ref, ext):
    j = pl.program_id(1)
    nj = pl.num_programs(1)
    is_ctx = j == 0
    zpad = jnp.zeros((_EXT_PAD, SSD_CONV_DIM), F32)
    ext[0:_EXT_PAD, :] = zpad
    ext[_EXT_ROWS - _EXT_PAD:_EXT_ROWS, :] = zpad
    ext[_EXT_PAD:_EXT_BASE, :] = jnp.where(j >= 2, prev_ref[0], 0.0)
    ext[_EXT_BASE:_EXT_BASE + TB, :] = cur_ref[0]
    ext[_EXT_BASE + TB:_EXT_BASE + TB + GRID_W, :] = jnp.where(
        jnp.logical_and(j >= 1, j <= nj - 2), next_ref[0], 0.0)

    t = lax.broadcasted_iota(jnp.int32, (TB, LANES), 0)
    col = t & (GRID_W - 1)
    ok_l = jnp.where(is_ctx, t, col) >= 1
    ok_r = jnp.where(is_ctx, t - (TB - GRID_W), col) <= GRID_W - 2
    lat = jnp.where(is_ctx, 0.0, 1.0)

    for c in range(SSD_CONV_DIM // LANES):
        lo, hi = c * LANES, (c + 1) * LANES
        acc = jnp.zeros((TB, LANES), F32)
        for dr in (-1, 0, 1):
            for dc in (-1, 0, 1):
                start = _EXT_BASE + GRID_W * dr + dc
                tap = ext[start:start + TB, lo:hi]
                wi = 3 * (dr + 1) + (dc + 1)
                wv = w_ref[wi:wi + 1, lo:hi]
                if dr != 0:
                    wv = wv * lat
                if dc == -1:
                    tap = jnp.where(ok_l, tap, 0.0)
                elif dc == 1:
                    tap = jnp.where(ok_r, tap, 0.0)
                acc = acc + tap * wv
        y = acc + b_ref[:, lo:hi]
        y = y * _sigmoid(y)
        if c < SSD_INNER // LANES:
            xs_ref[0, :, lo:hi] = y
        else:
            bc_ref[0, :, lo - SSD_INNER:hi - SSD_INNER] = y.astype(BF16)


def _conv_call(xbc, w9, bias):
    bsz, ls, ch = xbc.shape
    nj = ls // TB
    rpb = TB // GRID_W
    nrow = ls // GRID_W
    return pl.pallas_call(
        _conv_kernel,
        grid=(bsz, nj),
        in_specs=[pl.BlockSpec((1, GRID_W, ch), lambda b, j: (b, jnp.maximum(rpb * j - 1, 0), 0)),
                  pl.BlockSpec((1, TB, ch), lambda b, j: (b, j, 0)),
                  pl.BlockSpec((1, GRID_W, ch), lambda b, j: (b, jnp.minimum(rpb * j + rpb, nrow - 1), 0)),
                  pl.BlockSpec((9, ch), lambda b, j: (0, 0)),
                  pl.BlockSpec((1, ch), lambda b, j: (0, 0))],
        out_specs=[pl.BlockSpec((1, TB, SSD_INNER), lambda b, j: (b, j, 0)),
                   pl.BlockSpec((1, TB, ch - SSD_INNER), lambda b, j: (b, j, 0))],
        out_shape=[jax.ShapeDtypeStruct((bsz, ls, SSD_INNER), F32),
                   jax.ShapeDtypeStruct((bsz, ls, ch - SSD_INNER), BF16)],
        scratch_shapes=[pltpu.VMEM((_EXT_ROWS, ch), F32)],
        compiler_params=_params("arbitrary", "arbitrary"),
    )(xbc, xbc, xbc, w9, bias)


def _fwd_blk(s):
    return s


def _bwd_blk(s, ns):
    return jnp.where(s == 0, 0, ns - s)


def _gla_dir(q, k, v, la, st_ref, tri_m, fwd):
    C = GLA_C
    b = _dot_exact_r(tri_m, la)
    bt = b[C - 1:C, :] if fwd else b[0:1, :]
    r = 0.5 * bt
    qf = q.astype(F32)
    kf = k.astype(F32)
    qt = (qf * jnp.exp(jnp.minimum(b - r, EXP_CLAMP))).astype(BF16)
    kt = (kf * jnp.exp(jnp.minimum(r - b, EXP_CLAMP))).astype(BF16)
    head_k = lax.broadcasted_iota(jnp.int32, (C, GLA_QK), 1) >> 6
    kst = jnp.concatenate([jnp.where(head_k == h, kt, jnp.zeros_like(kt)) for h in range(GLA_HEADS)], axis=0)
    sc = _dot_nt(qt, kst)
    ii = lax.broadcasted_iota(jnp.int32, (C, GLA_HEADS * C), 0)
    jj = lax.broadcasted_iota(jnp.int32, (C, GLA_HEADS * C), 1) & (C - 1)
    causal = (jj <= ii) if fwd else (jj >= ii)
    p = jnp.where(causal, sc, 0.0).astype(BF16)
    head_v = lax.broadcasted_iota(jnp.int32, (C, GLA_V), 1) >> 7
    vst = jnp.concatenate([jnp.where(head_v == h, v, jnp.zeros_like(v)) for h in range(GLA_HEADS)], axis=0)
    o = _dot(p, vst)
    st = st_ref[...]
    qd = (qf * jnp.exp(b)).astype(BF16)
    o = o + _dot_nt(qd, st.astype(BF16))
    kd = (kf * jnp.exp(bt - b)).astype(BF16)
    u = _dot_tn(v, kd)
    rv = lax.broadcasted_iota(jnp.int32, (GLA_V, GLA_QK), 0) >> 7
    cv = lax.broadcasted_iota(jnp.int32, (GLA_V, GLA_QK), 1) >> 6
    st_ref[...] = st * jnp.exp(bt) + jnp.where(rv == cv, u, 0.0)
    return o


def _gla_kernel(qf_ref, kf_ref, vf_ref, lf_ref, qb_ref, kb_ref, vb_ref, lb_ref, trif_ref, trib_ref,
                of_ref, ob_ref, stf, stb):
    s = pl.program_id(1)

    @pl.when(s == 0)
    def _():
        stf[...] = jnp.zeros_like(stf)
        stb[...] = jnp.zeros_like(stb)

    nsub = TB // GLA_C

    def body(i, carry):
        rf = pl.multiple_of(i * GLA_C, GLA_C)
        rb = pl.multiple_of((nsub - 1 - i) * GLA_C, GLA_C)
        sf = pl.ds(rf, GLA_C)
        sb = pl.ds(rb, GLA_C)
        of_ref[0, sf, :] = _gla_dir(qf_ref[0, sf, :], kf_ref[0, sf, :], vf_ref[0, sf, :], lf_ref[0, sf, :],
                                    stf, trif_ref[...], True)
        ob_ref[0, sb, :] = _gla_dir(qb_ref[0, sb, :], kb_ref[0, sb, :], vb_ref[0, sb, :], lb_ref[0, sb, :],
                                    stb, trib_ref[...], False)
        return carry

    lax.fori_loop(0, nsub, body, 0)


def _gla_call(q, k, v, ld, L):
    bsz, ls, _ = q.shape
    ns = ls // TB
    nx = L // TB
    trif = jnp.asarray(np.tril(np.ones((GLA_C, GLA_C), np.float32)), BF16)
    trib = jnp.asarray(np.triu(np.ones((GLA_C, GLA_C), np.float32)), BF16)
    f = lambda n, lane=0: pl.BlockSpec((1, TB, n), lambda b, s: (b, _fwd_blk(s), lane))
    r = lambda n, lane=0: pl.BlockSpec((1, TB, n), lambda b, s: (b, _bwd_blk(s, ns), lane))
    tri = pl.BlockSpec((GLA_C, GLA_C), lambda b, s: (0, 0))
    return pl.pallas_call(
        _gla_kernel,
        grid=(bsz, ns),
        in_specs=[f(GLA_QK), f(GLA_QK), f(GLA_V), f(GLA_QK, 0),
                  r(GLA_QK), r(GLA_QK), r(GLA_V), r(GLA_QK, 1), tri, tri],
        out_specs=[pl.BlockSpec((1, TB, GLA_V), lambda b, s: (b, jnp.maximum(s - 1, 0), 0)),
                   pl.BlockSpec((1, TB, GLA_V), lambda b, s: (b, jnp.where(s == 0, nx - 1, nx - s), 0))],
        out_shape=[jax.ShapeDtypeStruct((bsz, L, GLA_V), F32)] * 2,
        scratch_shapes=[pltpu.VMEM((GLA_V, GLA_QK), F32)] * 2,
        compiler_params=_params("arbitrary", "arbitrary"),
    )(q, k, v, ld, q, k, v, ld, trif, trib)


def _ssd_dir(xs, bc, dtm, avec, dvec, st_ref, tri_m, e_m, base, fwd):
    C = SSD_C
    dt_exp = _dot_exact_l(dtm, e_m)
    a = dtm * avec
    acum = _dot_exact_r(tri_m, a)
    acum_exp = _dot_exact_l(acum, e_m)
    acum_t = acum.T
    xdt = xs * dt_exp
    xdt_b = xdt.astype(BF16)
    ii = lax.broadcasted_iota(jnp.int32, (C, C), 0)
    jj = lax.broadcasted_iota(jnp.int32, (C, C), 1)
    tri = (jj <= ii) if fwd else (jj >= ii)
    al_exp = acum_exp[C - 1:C, :] if fwd else acum_exp[0:1, :]
    head2 = lax.broadcasted_iota(jnp.int32, (C, 256), 1) >> 6
    ys = []
    for g in range(SSD_GROUPS):
        gl, gh = 256 * g, 256 * (g + 1)
        bg = bc[:, 128 * g:128 * (g + 1)]
        cg = bc[:, 256 + 128 * g:256 + 128 * (g + 1)]
        cb = _dot_nt(cg, bg)
        ms = []
        for rr in range(SSD_HPG):
            ln = base + SSD_HPG * g + rr
            diff = acum[:, ln:ln + 1] - acum_t[ln:ln + 1, :]
            seg = jnp.where(tri, jnp.exp(jnp.minimum(diff, 0.0)), 0.0)
            ms.append((cb * seg).astype(BF16))
        res = _dot(jnp.concatenate(ms, axis=0), xdt_b[:, gl:gh])
        yg = jnp.zeros((C, 256), F32)
        for rr in range(SSD_HPG):
            yg = yg + jnp.where(head2 == rr, res[rr * C:(rr + 1) * C, :], 0.0)
        sg = st_ref[g]
        yoff = _dot(cg, sg.astype(BF16)) * jnp.exp(acum_exp[:, gl:gh])
        wst = jnp.exp(al_exp[:, gl:gh] - acum_exp[:, gl:gh])
        ug = _dot_tn(bg, (xdt[:, gl:gh] * wst).astype(BF16))
        st_ref[g] = sg * jnp.exp(al_exp[:, gl:gh]) + ug
        ys.append(yg + yoff)
    y = jnp.concatenate(ys, axis=1)
    if dvec is not None:
        y = y + dvec * xs
    return y


def _ssd_kernel(xf_ref, bcf_ref, mf_ref, xb_ref, bcb_ref, mb_ref, af_ref, ab_ref, d_ref,
                trif_ref, trib_ref, ef_ref, eb_ref, yf_ref, yb_ref, stf, stb):
    s = pl.program_id(1)

    @pl.when(s == 0)
    def _():
        stf[...] = jnp.zeros_like(stf)
        stb[...] = jnp.zeros_like(stb)

    nsub = TB // SSD_C

    def body(i, carry):
        rf = pl.multiple_of(i * SSD_C, SSD_C)
        rb = pl.multiple_of((nsub - 1 - i) * SSD_C, SSD_C)
        sf = pl.ds(rf, SSD_C)
        sb = pl.ds(rb, SSD_C)
        yf_ref[0, sf, :] = _ssd_dir(xf_ref[0, sf, :], bcf_ref[0, sf, :], mf_ref[0, sf, :], af_ref[...], d_ref[...],
                                    stf, trif_ref[...], ef_ref[...], DT_F, True)
        yb_ref[0, sb, :] = _ssd_dir(xb_ref[0, sb, :], bcb_ref[0, sb, :], mb_ref[0, sb, :], ab_ref[...], None,
                                    stb, trib_ref[...], eb_ref[...], DT_B, False)
        return carry

    lax.fori_loop(0, nsub, body, 0)


def _expand_matrix(base):
    e = np.zeros((LANES, SSD_INNER), np.float32)
    for h in range(SSD_HEADS):
        e[base + h, SSD_HEADDIM * h:SSD_HEADDIM * (h + 1)] = 1.0
    return jnp.asarray(e, BF16)


def _ssd_call(xs, bc, misc, a_f, a_b, dvec, L):
    bsz, ls, _ = xs.shape
    ns = ls // TB
    nx = L // TB
    trif = jnp.asarray(np.tril(np.ones((SSD_C, SSD_C), np.float32)), BF16)
    trib = jnp.asarray(np.triu(np.ones((SSD_C, SSD_C), np.float32)), BF16)
    ef, eb = _expand_matrix(DT_F), _expand_matrix(DT_B)
    f = lambda n: pl.BlockSpec((1, TB, n), lambda b, s: (b, _fwd_blk(s), 0))
    r = lambda n: pl.BlockSpec((1, TB, n), lambda b, s: (b, _bwd_blk(s, ns), 0))
    const = lambda a: pl.BlockSpec(a.shape, lambda b, s: (0,) * a.ndim)
    return pl.pallas_call(
        _ssd_kernel,
        grid=(bsz, ns),
        in_specs=[f(SSD_INNER), f(512), f(LANES), r(SSD_INNER), r(512), r(LANES),
                  const(a_f), const(a_b), const(dvec), const(trif), const(trib), const(ef), const(eb)],
        out_specs=[pl.BlockSpec((1, TB, SSD_INNER), lambda b, s: (b, jnp.maximum(s - 1, 0), 0)),
                   pl.BlockSpec((1, TB, SSD_INNER), lambda b, s: (b, jnp.where(s == 0, nx - 1, nx - s), 0))],
        out_shape=[jax.ShapeDtypeStruct((bsz, L, SSD_INNER), F32)] * 2,
        scratch_shapes=[pltpu.VMEM((SSD_GROUPS, SSD_STATE, SSD_HPG * SSD_HEADDIM), F32)] * 2,
        compiler_params=_params("arbitrary", "arbitrary"),
    )(xs, bc, misc, xs, bc, misc, a_f, a_b, dvec, trif, trib, ef, eb)


def _outproj_kernel(of_ref, ob_ref, g_ref, yf_ref, yb_ref, z_ref, x_ref, g1_ref, sh2_ref, sc2_ref,
                    gn_ref, sn_ref, wo_ref, n2_ref, wrh_ref, wrl_ref, br_ref,
                    x1_ref, h2_ref, eidx_ref, gate_ref, cnt_ref):
    first = jnp.logical_and(pl.program_id(0) == 0, pl.program_id(1) == 0)

    @pl.when(first)
    def _():
        cnt_ref[...] = jnp.zeros_like(cnt_ref)

    o = of_ref[0] + ob_ref[0]
    gg = g_ref[0]
    parts = []
    for h in range(GLA_HEADS):
        lo, hi = GLA_DV * h, GLA_DV * (h + 1)
        oh = o[:, lo:hi]
        ms = jnp.mean(oh * oh, axis=-1, keepdims=True)
        gh = gg[:, lo:hi]
        parts.append((oh * lax.rsqrt(ms + EPS) * gn_ref[:, lo:hi] * (gh * _sigmoid(gh))).astype(BF16))
    zz = z_ref[0]
    u = (yf_ref[0] + yb_ref[0]) * (zz * _sigmoid(zz))
    gw = SSD_INNER // SSD_GROUPS
    for g in range(SSD_GROUPS):
        lo, hi = gw * g, gw * (g + 1)
        ug = u[:, lo:hi]
        ms = jnp.mean(ug * ug, axis=-1, keepdims=True)
        parts.append((ug * lax.rsqrt(ms + EPS) * sn_ref[:, lo:hi]).astype(BF16))
    mix = jnp.concatenate(parts, axis=1)
    x1 = x_ref[0] + g1_ref[0] * _dot(mix, wo_ref[...])
    x1_ref[0] = x1
    ms = jnp.mean(x1 * x1, axis=-1, keepdims=True)
    h2 = (x1 * lax.rsqrt(ms + EPS) * n2_ref[...]) * (1.0 + sc2_ref[0]) + sh2_ref[0]
    h2_ref[0] = h2
    h_hi = h2.astype(BF16)
    h_lo = (h2 - h_hi.astype(F32)).astype(BF16)
    logits = (_dot(h_hi, wrh_ref[...]) + _dot(h_lo, wrh_ref[...]) + _dot(h_hi, wrl_ref[...])) + br_ref[...]
    lane = lax.broadcasted_iota(jnp.int32, (TB, LANES), 1)
    work = logits
    eidx = jnp.full((TB, LANES), -1, jnp.int32)
    gates = jnp.zeros((TB, LANES), F32)
    sel = jnp.zeros((TB, LANES), F32)
    m0 = None
    den = jnp.zeros((TB, 1), F32)
    for kk in range(TOP_K):
        m = jnp.max(work, axis=-1, keepdims=True)
        idx = jnp.min(jnp.where(work == m, lane, LANES), axis=-1, keepdims=True)
        hit = lane == idx
        if m0 is None:
            m0 = m
        e = jnp.exp(m - m0)
        den = den + e
        eidx = jnp.where(lane == kk, idx, eidx)
        gates = jnp.where(lane == kk, e, gates)
        sel = jnp.where(hit, 1.0, sel)
        work = jnp.where(hit, NEG_BIG, work)
    eidx_ref[0] = eidx
    gate_ref[0] = gates / den
    cnt_ref[...] += jnp.sum(sel, axis=0, keepdims=True)


def _outproj_call(o_f, o_b, g_all, y_f, y_b, z_all, x, g1, sh2, sc2, gn, sn, wo, n2, wr_hi, wr_lo, br):
    bsz, L, d = x.shape
    nj = L // TB
    tok = lambda n: pl.BlockSpec((1, TB, n), lambda b, j: (b, j, 0))
    tok_off = lambda n: pl.BlockSpec((1, TB, n), lambda b, j: (b, j + 1, 0))
    const = lambda a: pl.BlockSpec(a.shape, lambda b, j: (0,) * a.ndim)
    mod = pl.BlockSpec((1, 1, d), lambda b, j: (b, 0, 0))
    return pl.pallas_call(
        _outproj_kernel,
        grid=(bsz, nj),
        in_specs=[tok(GLA_V), tok(GLA_V), tok_off(GLA_V), tok(SSD_INNER), tok(SSD_INNER), tok_off(SSD_INNER),
                  tok(d), mod, mod, mod, const(gn), const(sn), const(wo), const(n2),
                  const(wr_hi), const(wr_lo), const(br)],
        out_specs=[tok(d), tok(d), tok(LANES), tok(LANES), pl.BlockSpec((1, LANES), lambda b, j: (0, 0))],
        out_shape=[jax.ShapeDtypeStruct((bsz, L, d), F32), jax.ShapeDtypeStruct((bsz, L, d), F32),
                   jax.ShapeDtypeStruct((bsz, L, LANES), jnp.int32), jax.ShapeDtypeStruct((bsz, L, LANES), F32),
                   jax.ShapeDtypeStruct((1, LANES), F32)],
        compiler_params=_params("arbitrary", "arbitrary"),
    )(o_f, o_b, g_all, y_f, y_b, z_all, x, g1, sh2, sc2, gn, sn, wo, n2, wr_hi, wr_lo, br)


def _pos_kernel(eidx_ref, pstart_ref, lst_ref, dest_ref, carry):
    @pl.when(pl.program_id(0) == 0)
    def _():
        carry[...] = pstart_ref[...]

    eidx = eidx_ref[...]
    lane = lax.broadcasted_iota(jnp.int32, (TB, LANES), 1)
    hits = [lane == eidx[:, kk:kk + 1] for kk in range(TOP_K)]
    sel = jnp.zeros((TB, LANES), F32)
    for hmask in hits:
        sel = jnp.where(hmask, 1.0, sel)
    pos = carry[...] + _dot(lst_ref[...], sel.astype(BF16))
    dest = jnp.zeros((TB, LANES), jnp.int32)
    for kk, hmask in enumerate(hits):
        dk = jnp.sum(jnp.where(hmask, pos, 0.0), axis=-1, keepdims=True)
        dest = jnp.where(lane == kk, dk.astype(jnp.int32), dest)
    dest_ref[...] = dest
    carry[...] += jnp.sum(sel, axis=0, keepdims=True)


def _pos_call(eidx, pstart):
    T = eidx.shape[0]
    lst = jnp.asarray(np.tril(np.ones((TB, TB), np.float32), -1), BF16)
    return pl.pallas_call(
        _pos_kernel,
        grid=(T // TB,),
        in_specs=[pl.BlockSpec((TB, LANES), lambda i: (i, 0)),
                  pl.BlockSpec((1, LANES), lambda i: (0, 0)),
                  pl.BlockSpec((TB, TB), lambda i: (0, 0))],
        out_specs=pl.BlockSpec((TB, LANES), lambda i: (i, 0)),
        out_shape=jax.ShapeDtypeStruct((T, LANES), jnp.int32),
        scratch_shapes=[pltpu.VMEM((1, LANES), F32)],
        compiler_params=_params("arbitrary"),
    )(eidx, pstart, lst)


def _dispatch_kernel(pad_ref, dest_ref, h_ref, xg_ref, sem, pad_sem):
    def body(r, carry):
        for kk in range(TOP_K):
            d = dest_ref[0, 0, TOP_K * r + kk]
            pltpu.make_async_copy(h_ref.at[pl.ds(r, 1), :], xg_ref.at[pl.ds(d, 1), :], sem).start(priority=kk % 2)
        return carry

    lax.fori_loop(0, TB, body, 0)

    @pl.when(pl.program_id(0) == pl.num_programs(0) - 1)
    def _():
        def pad_row(p):
            return pltpu.make_async_copy(h_ref.at[pl.ds(0, 1), :], xg_ref.at[pl.ds(p, 1), :], pad_sem)

        def per_expert(e, carry):
            lax.fori_loop(pad_ref[0, e], pad_ref[1, e], lambda p, c: (pad_row(p).start(), c)[1], 0)
            return carry

        def per_expert_wait(e, carry):
            lax.fori_loop(pad_ref[0, e], pad_ref[1, e], lambda p, c: (pad_row(p).wait(), c)[1], 0)
            return carry

        lax.fori_loop(0, N_EXPERTS, per_expert, 0)
        lax.fori_loop(0, N_EXPERTS, per_expert_wait, 0)

    for kk in range(TOP_K):
        pltpu.make_async_copy(h_ref, xg_ref.at[pl.ds(0, TB), :], sem).wait()


def _dispatch_call(pad_rows, dest3, h2, P):
    T, d = h2.shape
    grid_spec = pltpu.PrefetchScalarGridSpec(
        num_scalar_prefetch=1,
        grid=(T // TB,),
        in_specs=[pl.BlockSpec((1, 1, TB * TOP_K), lambda i, pad: (i, 0, 0), memory_space=pltpu.SMEM),
                  pl.BlockSpec((TB, d), lambda i, pad: (i, 0))],
        out_specs=pl.BlockSpec(memory_space=pl.ANY),
        scratch_shapes=[pltpu.SemaphoreType.DMA(()), pltpu.SemaphoreType.DMA(())],
    )
    return pl.pallas_call(
        _dispatch_kernel,
        grid_spec=grid_spec,
        out_shape=jax.ShapeDtypeStruct((P, d), F32),
        compiler_params=_params("arbitrary"),
    )(pad_rows, dest3, h2)


def _moe_kernel(be_ref, x_ref, wgu_ref, bgu_ref, wd_ref, bd_ref, y_ref, wgu_b, wd_b):
    i = pl.program_id(0)
    n_used = be_ref[pl.num_programs(0)]
    changed = jnp.logical_or(i == 0, be_ref[i] != be_ref[jnp.maximum(i - 1, 0)])

    @pl.when(changed)
    def _():
        wgu_b[...] = wgu_ref[0].astype(BF16)
        wd_b[...] = wd_ref[0].astype(BF16)

    @pl.when(i < n_used)
    def _():
        xb = x_ref[...].astype(BF16)
        acc = jnp.zeros((MOE_BLOCK, wd_b.shape[1]), F32) + bd_ref[0]
        cw = 256
        for c in range(D_FF // cw):
            lo, hi = c * cw, (c + 1) * cw
            gate = _dot(xb, wgu_b[:, lo:hi]) + bgu_ref[0, :, lo:hi]
            up = _dot(xb, wgu_b[:, D_FF + lo:D_FF + hi]) + bgu_ref[0, :, D_FF + lo:D_FF + hi]
            gate = jnp.minimum(gate, SWIGLU_LIMIT)
            up = jnp.clip(up, -SWIGLU_LIMIT, SWIGLU_LIMIT)
            act = ((up + 1.0) * (gate * _sigmoid(SWIGLU_ALPHA * gate))).astype(BF16)
            acc = acc + _dot(act, wd_b[lo:hi, :])
        y_ref[...] = acc

    @pl.when(i >= n_used)
    def _():
        y_ref[...] = jnp.zeros_like(y_ref)


def _moe_call(blk_e, xg, wgu, bgu, wd, bd):
    P, d = xg.shape
    nb = P // MOE_BLOCK
    ne, _, f2 = wgu.shape
    grid_spec = pltpu.PrefetchScalarGridSpec(
        num_scalar_prefetch=1,
        grid=(nb,),
        in_specs=[pl.BlockSpec((MOE_BLOCK, d), lambda i, be: (jnp.minimum(i, be[nb] - 1), 0)),
                  pl.BlockSpec((1, d, f2), lambda i, be: (be[i], 0, 0)),
                  pl.BlockSpec((1, 1, f2), lambda i, be: (be[i], 0, 0)),
                  pl.BlockSpec((1, f2 // 2, d), lambda i, be: (be[i], 0, 0)),
                  pl.BlockSpec((1, 1, d), lambda i, be: (be[i], 0, 0))],
        out_specs=pl.BlockSpec((MOE_BLOCK, d), lambda i, be: (i, 0)),
        scratch_shapes=[pltpu.VMEM((d, f2), BF16), pltpu.VMEM((f2 // 2, d), BF16)],
    )
    return pl.pallas_call(
        _moe_kernel,
        grid_spec=grid_spec,
        out_shape=jax.ShapeDtypeStruct((P, d), F32),
        compiler_params=_params("arbitrary"),
    )(blk_e, xg, wgu, bgu, wd, bd)


def _combine_kernel(dcur_ref, dnext_ref, y_ref, x1_ref, gate_ref, g2_ref, fn_ref, o_ref, buf, sems):
    i = pl.program_id(0)
    n = pl.num_programs(0)
    slot = i & 1

    def gather(dref, s):
        def body(r, carry):
            for kk in range(TOP_K):
                d = dref[0, 0, TOP_K * r + kk]
                pltpu.make_async_copy(y_ref.at[pl.ds(d, 1), :], buf.at[s, kk, pl.ds(r, 1), :],
                                      sems.at[s]).start(priority=kk % 2)
            return carry

        lax.fori_loop(0, TB, body, 0)

    @pl.when(i == 0)
    def _():
        gather(dcur_ref, 0)

    @pl.when(i + 1 < n)
    def _():
        gather(dnext_ref, 1 - slot)

    for kk in range(TOP_K):
        pltpu.make_async_copy(y_ref.at[pl.ds(0, TB), :], buf.at[slot, kk], sems.at[slot]).wait()

    gates = gate_ref[...]
    f = jnp.zeros(o_ref.shape, F32)
    for kk in range(TOP_K):
        f = f + gates[:, kk:kk + 1] * buf[slot, kk]
    xo = x1_ref[...] + g2_ref[0] * f
    ms = jnp.mean(xo * xo, axis=-1, keepdims=True)
    o_ref[...] = xo * lax.rsqrt(ms + EPS) * fn_ref[...]


def _combine_call(dest3, y, x1, gates, g2, fn):
    T, d = x1.shape
    n = T // TB
    per_batch = n // g2.shape[0]
    return pl.pallas_call(
        _combine_kernel,
        grid=(n,),
        in_specs=[pl.BlockSpec((1, 1, TB * TOP_K), lambda i: (i, 0, 0), memory_space=pltpu.SMEM),
                  pl.BlockSpec((1, 1, TB * TOP_K), lambda i: (jnp.minimum(i + 1, n - 1), 0, 0),
                               memory_space=pltpu.SMEM),
                  pl.BlockSpec(memory_space=pl.ANY),
                  pl.BlockSpec((TB, d), lambda i: (i, 0)),
                  pl.BlockSpec((TB, LANES), lambda i: (i, 0)),
                  pl.BlockSpec((1, 1, d), lambda i: (i // per_batch, 0, 0)),
                  pl.BlockSpec((1, d), lambda i: (0, 0))],
        out_specs=pl.BlockSpec((TB, d), lambda i: (i, 0)),
        out_shape=jax.ShapeDtypeStruct((T, d), F32),
        scratch_shapes=[pltpu.VMEM((2, TOP_K, TB, d), F32), pltpu.SemaphoreType.DMA((2,))],
        compiler_params=_params("arbitrary"),
    )(dest3, dest3, y, x1, gates, g2, fn)


def _layer(x, c, ctx, c_ctx, w_mod, b_mod, norm1, w_in, gla_w_gk_up, gla_b_gk, gla_norm,
           ssd_conv_w, ssd_conv_b, ssd_dt_bias, ssd_A_log, ssd_D, ssd_norm, w_out,
           norm2, w_router, b_router, w_gate_up, b_gate_up, w_down, b_down, final_norm):
    bsz, L, d = x.shape
    lc = ctx.shape[1]
    assert lc == TB and L % TB == 0 and TB % GRID_W == 0

    cin = jnp.zeros((8, d), F32).at[:bsz].set(c).at[bsz].set(c_ctx)
    mod = _mod_call(cin, w_mod, b_mod.reshape(1, -1))[:bsz + 1]
    sh1, sc1, g1, sh2, sc2, g2 = [m.reshape(bsz + 1, 1, d) for m in jnp.split(mod, 6, axis=-1)]

    o = np.cumsum((0, GLA_QK, GLA_QK, GLA_V, GLA_V, GLA_RANK, SSD_INNER, SSD_CONV_DIM, SSD_HEADS))
    wq, wk, wv, wg, wlow, wz, wx, wdt = [w_in[:, int(a):int(b)] for a, b in zip(o[:-1], o[1:])]
    w_misc = jnp.concatenate([wlow, wdt, wdt, jnp.zeros((d, LANES - GLA_RANK - 2 * SSD_HEADS), F32)], axis=1)
    w_cat = jnp.concatenate([wq, wk, wv, wg, wz, wx, w_misc], axis=1).astype(BF16)
    wup = jnp.zeros((LANES, 2 * GLA_QK), F32).at[:GLA_RANK].set(
        jnp.concatenate([gla_w_gk_up[0], gla_w_gk_up[1]], axis=1)).astype(BF16)
    bup = jnp.concatenate([gla_b_gk[0], gla_b_gk[1]]).reshape(1, -1)
    dtb = jnp.zeros((1, LANES), F32).at[0, DT_F:DT_F + SSD_HEADS].set(ssd_dt_bias[0]) \
                                    .at[0, DT_B:DT_B + SSD_HEADS].set(ssd_dt_bias[1])
    q, k, v, g_all, z_all, xbc, ld, misc = _inproj_call(
        x, ctx, sh1, sc1, norm1.reshape(1, d), w_cat, wup, bup, dtb)

    xs, bc = _conv_call(xbc, ssd_conv_w.reshape(9, SSD_CONV_DIM), ssd_conv_b.reshape(1, -1))

    o_f, o_b = _gla_call(q, k, v, ld, L)

    a_neg = -jnp.exp(ssd_A_log.astype(F32))
    a_f = jnp.zeros((1, LANES), F32).at[0, DT_F:DT_F + SSD_HEADS].set(a_neg[0])
    a_b = jnp.zeros((1, LANES), F32).at[0, DT_B:DT_B + SSD_HEADS].set(a_neg[1])
    dvec = jnp.repeat(ssd_D, SSD_HEADDIM).reshape(1, SSD_INNER)
    y_f, y_b = _ssd_call(xs, bc, misc, a_f, a_b, dvec, L)

    wr = jnp.zeros((d, LANES), F32).at[:, :N_EXPERTS].set(w_router)
    wr_hi = wr.astype(BF16)
    wr_lo = (wr - wr_hi.astype(F32)).astype(BF16)
    br = jnp.full((1, LANES), NEG_BIG, F32).at[0, :N_EXPERTS].set(b_router)
    x1, h2, eidx, gates, counts = _outproj_call(
        o_f, o_b, g_all, y_f, y_b, z_all, x, g1[:bsz], sh2[:bsz], sc2[:bsz],
        jnp.tile(gla_norm, GLA_HEADS).reshape(1, -1), ssd_norm.reshape(1, -1), w_out.astype(BF16),
        norm2.reshape(1, d), wr_hi, wr_lo, br)

    T = bsz * L
    cnt = counts[0, :N_EXPERTS].astype(jnp.int32)
    padded = ((cnt + MOE_BLOCK - 1) // MOE_BLOCK) * MOE_BLOCK
    pend = jnp.cumsum(padded)
    pstart = pend - padded
    n_blocks = -(-(T * TOP_K) // MOE_BLOCK) + N_EXPERTS
    blk_start = jnp.arange(n_blocks, dtype=jnp.int32) * MOE_BLOCK
    blk_e = jnp.minimum(jnp.sum(pend[None, :] <= blk_start[:, None], axis=1), N_EXPERTS - 1).astype(jnp.int32)
    blk_info = jnp.concatenate([blk_e, (pend[-1:] // MOE_BLOCK).astype(jnp.int32)])
    pad_rows = jnp.stack([pstart + cnt, pend.at[-1].set(n_blocks * MOE_BLOCK)]).astype(jnp.int32)
    pstart_row = jnp.zeros((1, LANES), F32).at[0, :N_EXPERTS].set(pstart.astype(F32))
    dest = _pos_call(eidx.reshape(T, LANES), pstart_row)
    dest3 = dest[:, :TOP_K].reshape(T // TB, 1, TB * TOP_K)

    xg = _dispatch_call(pad_rows, dest3, h2.reshape(T, d), n_blocks * MOE_BLOCK)
    y = _moe_call(blk_info, xg, w_gate_up, b_gate_up.reshape(N_EXPERTS, 1, -1), w_down,
                  b_down.reshape(N_EXPERTS, 1, -1))
    out = _combine_call(dest3, y, x1.reshape(T, d), gates.reshape(T, LANES), g2[:bsz], final_norm.reshape(1, d))
    return out.reshape(bsz, L, d)


def kernel(x, c, ctx, c_ctx, w_mod, b_mod, norm1, w_in, gla_w_gk_up, gla_b_gk, gla_norm, ssd_conv_w, ssd_conv_b, ssd_dt_bias, ssd_A_log, ssd_D, ssd_norm, w_out, norm2, w_router, b_router, w_gate_up, b_gate_up, w_down, b_down, final_norm):
    assert w_mod.shape[0] == 1, "single-layer kernel"
    return _layer(x, c, ctx, c_ctx, w_mod[0], b_mod[0], norm1[0], w_in[0], gla_w_gk_up[0], gla_b_gk[0],
                  gla_norm[0], ssd_conv_w[0], ssd_conv_b[0], ssd_dt_bias[0], ssd_A_log[0], ssd_D[0],
                  ssd_norm[0], w_out[0], norm2[0], w_router[0], b_router[0], w_gate_up[0], b_gate_up[0],
                  w_down[0], b_down[0], final_norm)
```

```python
import functools

import numpy as np
import jax
import jax.numpy as jnp
from jax import lax
from jax.experimental import pallas as pl
from jax.experimental.pallas import tpu as pltpu

F32 = jnp.float32
BF16 = jnp.bfloat16

EPS = 1e-6
GRID_W = 64
GLA_HEADS = 4
GLA_DK = 64
GLA_DV = 128
GLA_QK = GLA_HEADS * GLA_DK
GLA_V = GLA_HEADS * GLA_DV
GLA_RANK = 16
GLA_GATE_NORM = 16.0
SSD_HEADDIM = 64
SSD_INNER = 512
SSD_HEADS = 8
SSD_GROUPS = 2
SSD_HPG = 4
SSD_STATE = 128
SSD_CONV_DIM = 1024
N_EXPERTS = 32
TOP_K = 4
D_FF = 1024
SWIGLU_LIMIT = 7.0
SWIGLU_ALPHA = 1.702
MOE_BLOCK = 256

TB = 256
GLA_C = 64
SSD_C = 128
LANES = 128
EXP_CLAMP = 80.0
DT_F = 16
DT_B = 24
NEG_BIG = -1e30
VMEM_LIMIT = 56 * 1024 * 1024


def _dot(a, b):
    return jnp.dot(a, b, preferred_element_type=F32)


def _dot_nt(a, b):
    return lax.dot_general(a, b, (((1,), (1,)), ((), ())), preferred_element_type=F32)


def _dot_tn(a, b):
    return lax.dot_general(a, b, (((0,), (0,)), ((), ())), preferred_element_type=F32)


def _split3(a):
    hi = a.astype(BF16)
    r1 = a - hi.astype(F32)
    mid = r1.astype(BF16)
    lo = (r1 - mid.astype(F32)).astype(BF16)
    return hi, mid, lo


def _dot_exact_r(m, a):
    hi, mid, lo = _split3(a)
    return _dot(m, hi) + _dot(m, mid) + _dot(m, lo)


def _dot_hilo_l(a, m):
    hi = a.astype(BF16)
    lo = (a - hi.astype(F32)).astype(BF16)
    return _dot(hi, m) + _dot(lo, m)


def _sigmoid(x):
    return 1.0 / (1.0 + jnp.exp(-x))


def _softplus(x):
    return jnp.maximum(x, 0.0) + jnp.log1p(jnp.exp(-jnp.abs(x)))


def _params(*sem):
    return pltpu.CompilerParams(dimension_semantics=sem, vmem_limit_bytes=VMEM_LIMIT)


def _mod_kernel(c_ref, w_ref, b_ref, o_ref):
    c = c_ref[...]
    s = c * _sigmoid(c)
    s_hi = s.astype(BF16)
    s_lo = (s - s_hi.astype(F32)).astype(BF16)
    w = w_ref[...]
    w_hi = w.astype(BF16)
    w_lo = (w - w_hi.astype(F32)).astype(BF16)
    o_ref[...] = _dot(s_hi, w_hi) + _dot(s_lo, w_hi) + _dot(s_hi, w_lo) + b_ref[...]


def _mod_call(cin, w, b):
    rows, d = cin.shape
    n = w.shape[1]
    tn = 1536
    return pl.pallas_call(
        _mod_kernel,
        grid=(n // tn,),
        in_specs=[pl.BlockSpec((rows, d), lambda i: (0, 0)),
                  pl.BlockSpec((d, tn), lambda i: (0, i)),
                  pl.BlockSpec((1, tn), lambda i: (0, i))],
        out_specs=pl.BlockSpec((rows, tn), lambda i: (0, i)),
        out_shape=jax.ShapeDtypeStruct((rows, n), F32),
        compiler_params=_params("arbitrary"),
    )(cin, w, b)


_C_Q, _C_K, _C_V, _C_G, _C_Z, _C_X, _C_M, _C_END = 0, 256, 512, 1024, 1536, 2048, 3072, 3200


def _inproj_kernel(x_ref, ctx_ref, sh_ref, sc_ref, n1_ref, w_ref, wup_ref, bup_ref, dtb_ref,
                   q_ref, k_ref, v_ref, g_ref, z_ref, xbc_ref, ld_ref, misc_ref, h_scr):
    j = pl.program_id(1)

    def normmod(xv):
        ms = jnp.mean(xv * xv, axis=-1, keepdims=True)
        y = xv * lax.rsqrt(ms + EPS) * n1_ref[...]
        return (y * (1.0 + sc_ref[0]) + sh_ref[0]).astype(BF16)

    @pl.when(j == 0)
    def _():
        h_scr[...] = normmod(ctx_ref[0])

    @pl.when(j > 0)
    def _():
        h_scr[...] = normmod(x_ref[0])

    h = h_scr[...]

    def mm(lo, hi):
        return _dot(h, w_ref[:, lo:hi])

    q_ref[0] = (mm(_C_Q, _C_K) * (GLA_DK ** -0.5)).astype(BF16)
    k_ref[0] = mm(_C_K, _C_V).astype(BF16)
    v_ref[0] = mm(_C_V, _C_G).astype(BF16)
    g_ref[0] = mm(_C_G, _C_Z)
    z_ref[0] = mm(_C_Z, _C_X)
    xbc_ref[0] = mm(_C_X, _C_M)
    m = mm(_C_M, _C_END)
    zz = _dot(m.astype(BF16), wup_ref[...]) + bup_ref[...]
    ld_ref[0] = -_softplus(-zz) * (1.0 / GLA_GATE_NORM)
    misc_ref[0] = _softplus(m + dtb_ref[...])


def _inproj_call(x, ctx, sh1, sc1, n1, w_cat, wup, bup, dtb):
    bsz, L, d = x.shape
    lc = ctx.shape[1]
    ls = lc + L
    nj = ls // TB
    tok = lambda n: pl.BlockSpec((1, TB, n), lambda b, j: (b, j, 0))
    const = lambda a: pl.BlockSpec(a.shape, lambda b, j: (0,) * a.ndim)
    modspec = pl.BlockSpec((1, 1, d), lambda b, j: (jnp.where(j == 0, bsz, b), 0, 0))
    outs = [(GLA_QK, BF16), (GLA_QK, BF16), (GLA_V, BF16), (GLA_V, F32), (SSD_INNER, F32),
            (SSD_CONV_DIM, F32), (2 * GLA_QK, F32), (LANES, F32)]
    return pl.pallas_call(
        _inproj_kernel,
        grid=(bsz, nj),
        in_specs=[pl.BlockSpec((1, TB, d), lambda b, j: (b, jnp.maximum(j - 1, 0), 0)),
                  pl.BlockSpec((1, TB, d), lambda b, j: (b, 0, 0)),
                  modspec, modspec, const(n1), const(w_cat), const(wup), const(bup), const(dtb)],
        out_specs=[tok(n) for n, _ in outs],
        out_shape=[jax.ShapeDtypeStruct((bsz, ls, n), dt) for n, dt in outs],
        scratch_shapes=[pltpu.VMEM((TB, d), BF16)],
        compiler_params=_params("arbitrary", "arbitrary"),
    )(x, ctx, sh1, sc1, n1, w_cat, wup, bup, dtb)


_EXT_PAD = 8
_EXT_BASE = _EXT_PAD + GRID_W
_EXT_ROWS = 2 * _EXT_PAD + 2 * GRID_W + TB


def _conv_kernel(prev_ref, cur_ref, next_ref, w_ref, b_ref, xs_ref, bc_ref, ext):
    j = pl.program_id(1)
    nj = pl.num_programs(1)
    is_ctx = j == 0
    zpad = jnp.zeros((_EXT_PAD, SSD_CONV_DIM), F32)
    ext[0:_EXT_PAD, :] = zpad
    ext[_EXT_ROWS - _EXT_PAD:_EXT_ROWS, :] = zpad
    ext[_EXT_PAD:_EXT_BASE, :] = jnp.where(j >= 2, prev_ref[0], 0.0)
    ext[_EXT_BASE:_EXT_BASE + TB, :] = cur_ref[0]
    ext[_EXT_BASE + TB:_EXT_BASE + TB + GRID_W, :] = jnp.where(
        jnp.logical_and(j >= 1, j <= nj - 2), next_ref[0], 0.0)

    t = lax.broadcasted_iota(jnp.int32, (TB, LANES), 0)
    col = t & (GRID_W - 1)
    ok_l = jnp.where(is_ctx, t, col) >= 1
    ok_r = jnp.where(is_ctx, t - (TB - GRID_W), col) <= GRID_W - 2
    lat = jnp.where(is_ctx, 0.0, 1.0)

    for c in range(SSD_CONV_DIM // LANES):
        lo, hi = c * LANES, (c + 1) * LANES
        acc = jnp.zeros((TB, LANES), F32)
        for dr in (-1, 0, 1):
            for dc in (-1, 0, 1):
                start = _EXT_BASE + GRID_W * dr + dc
                tap = ext[start:start + TB, lo:hi]
                wi = 3 * (dr + 1) + (dc + 1)
                wv = w_ref[wi:wi + 1, lo:hi]
                if dr != 0:
                    wv = wv * lat
                if dc == -1:
                    tap = jnp.where(ok_l, tap, 0.0)
                elif dc == 1:
                    tap = jnp.where(ok_r, tap, 0.0)
                acc = acc + tap * wv
        y = acc + b_ref[:, lo:hi]
        y = y * _sigmoid(y)
        if c < SSD_INNER // LANES:
            xs_ref[0, :, lo:hi] = y
        else:
            bc_ref[0, :, lo - SSD_INNER:hi - SSD_INNER] = y.astype(BF16)


def _conv_call(xbc, w9, bias):
    bsz, ls, ch = xbc.shape
    nj = ls // TB
    rpb = TB // GRID_W
    nrow = ls // GRID_W
    return pl.pallas_call(
        _conv_kernel,
        grid=(bsz, nj),
        in_specs=[pl.BlockSpec((1, GRID_W, ch), lambda b, j: (b, jnp.maximum(rpb * j - 1, 0), 0)),
                  pl.BlockSpec((1, TB, ch), lambda b, j: (b, j, 0)),
                  pl.BlockSpec((1, GRID_W, ch), lambda b, j: (b, jnp.minimum(rpb * j + rpb, nrow - 1), 0)),
                  pl.BlockSpec((9, ch), lambda b, j: (0, 0)),
                  pl.BlockSpec((1, ch), lambda b, j: (0, 0))],
        out_specs=[pl.BlockSpec((1, TB, SSD_INNER), lambda b, j: (b, j, 0)),
                   pl.BlockSpec((1, TB, ch - SSD_INNER), lambda b, j: (b, j, 0))],
        out_shape=[jax.ShapeDtypeStruct((bsz, ls, SSD_INNER), F32),
                   jax.ShapeDtypeStruct((bsz, ls, ch - SSD_INNER), BF16)],
        scratch_shapes=[pltpu.VMEM((_EXT_ROWS, ch), F32)],
        compiler_params=_params("arbitrary", "arbitrary"),
    )(xbc, xbc, xbc, w9, bias)


def _fwd_blk(s):
    return s


def _bwd_blk(s, ns):
    return jnp.where(s == 0, 0, ns - s)


def _gla_dir(q, k, v, la, st_ref, tri_m, fwd):
    C = GLA_C
    b = _dot_exact_r(tri_m, la)
    bt = b[C - 1:C, :] if fwd else b[0:1, :]
    r = 0.5 * bt
    qf = q.astype(F32)
    kf = k.astype(F32)
    qt = (qf * jnp.exp(jnp.minimum(b - r, EXP_CLAMP))).astype(BF16)
    kt = (kf * jnp.exp(jnp.minimum(r - b, EXP_CLAMP))).astype(BF16)
    head_k = lax.broadcasted_iota(jnp.int32, (C, GLA_QK), 1) >> 6
    kst = jnp.concatenate([jnp.where(head_k == h, kt, jnp.zeros_like(kt)) for h in range(GLA_HEADS)], axis=0)
    sc = _dot_nt(qt, kst)
    ii = lax.broadcasted_iota(jnp.int32, (C, GLA_HEADS * C), 0)
    jj = lax.broadcasted_iota(jnp.int32, (C, GLA_HEADS * C), 1) & (C - 1)
    causal = (jj <= ii) if fwd else (jj >= ii)
    p = jnp.where(causal, sc, 0.0).astype(BF16)
    head_v = lax.broadcasted_iota(jnp.int32, (C, GLA_V), 1) >> 7
    vst = jnp.concatenate([jnp.where(head_v == h, v, jnp.zeros_like(v)) for h in range(GLA_HEADS)], axis=0)
    o = _dot(p, vst)
    st = st_ref[...]
    qd = (qf * jnp.exp(b)).astype(BF16)
    o = o + _dot_nt(qd, st.astype(BF16))
    kd = (kf * jnp.exp(bt - b)).astype(BF16)
    u = _dot_tn(v, kd)
    rv = lax.broadcasted_iota(jnp.int32, (GLA_V, GLA_QK), 0) >> 7
    cv = lax.broadcasted_iota(jnp.int32, (GLA_V, GLA_QK), 1) >> 6
    st_ref[...] = st * jnp.exp(bt) + jnp.where(rv == cv, u, 0.0)
    return o


def _gla_kernel(qf_ref, kf_ref, vf_ref, lf_ref, qb_ref, kb_ref, vb_ref, lb_ref, trif_ref, trib_ref,
                of_ref, ob_ref, stf, stb):
    s = pl.program_id(1)

    @pl.when(s == 0)
    def _():
        stf[...] = jnp.zeros_like(stf)
        stb[...] = jnp.zeros_like(stb)

    nsub = TB // GLA_C

    for i in range(nsub):
        sf = pl.ds(i * GLA_C, GLA_C)
        sb = pl.ds((nsub - 1 - i) * GLA_C, GLA_C)
        of_ref[0, sf, :] = _gla_dir(qf_ref[0, sf, :], kf_ref[0, sf, :], vf_ref[0, sf, :], lf_ref[0, sf, :],
                                    stf, trif_ref[...], True)
        ob_ref[0, sb, :] = _gla_dir(qb_ref[0, sb, :], kb_ref[0, sb, :], vb_ref[0, sb, :], lb_ref[0, sb, :],
                                    stb, trib_ref[...], False)


def _gla_call(q, k, v, ld, L):
    bsz, ls, _ = q.shape
    ns = ls // TB
    nx = L // TB
    trif = jnp.asarray(np.tril(np.ones((GLA_C, GLA_C), np.float32)), BF16)
    trib = jnp.asarray(np.triu(np.ones((GLA_C, GLA_C), np.float32)), BF16)
    f = lambda n, lane=0: pl.BlockSpec((1, TB, n), lambda b, s: (b, _fwd_blk(s), lane))
    r = lambda n, lane=0: pl.BlockSpec((1, TB, n), lambda b, s: (b, _bwd_blk(s, ns), lane))
    tri = pl.BlockSpec((GLA_C, GLA_C), lambda b, s: (0, 0))
    return pl.pallas_call(
        _gla_kernel,
        grid=(bsz, ns),
        in_specs=[f(GLA_QK), f(GLA_QK), f(GLA_V), f(GLA_QK, 0),
                  r(GLA_QK), r(GLA_QK), r(GLA_V), r(GLA_QK, 1), tri, tri],
        out_specs=[pl.BlockSpec((1, TB, GLA_V), lambda b, s: (b, jnp.maximum(s - 1, 0), 0)),
                   pl.BlockSpec((1, TB, GLA_V), lambda b, s: (b, jnp.where(s == 0, nx - 1, nx - s), 0))],
        out_shape=[jax.ShapeDtypeStruct((bsz, L, GLA_V), F32)] * 2,
        scratch_shapes=[pltpu.VMEM((GLA_V, GLA_QK), F32)] * 2,
        compiler_params=_params("arbitrary", "arbitrary"),
    )(q, k, v, ld, q, k, v, ld, trif, trib)


def _ssd_dir(xs, bc, dtm, avec, dvec, st_ref, tri_m, e_m, base, fwd):
    C = SSD_C
    dt_exp = _dot_hilo_l(dtm, e_m)
    a = dtm * avec
    acum = _dot_exact_r(tri_m, a)
    acum_exp = _dot_hilo_l(acum, e_m)
    acum_t = acum.T
    xdt = xs * dt_exp
    xdt_b = xdt.astype(BF16)
    ii = lax.broadcasted_iota(jnp.int32, (C, C), 0)
    jj = lax.broadcasted_iota(jnp.int32, (C, C), 1)
    tri = (jj <= ii) if fwd else (jj >= ii)
    al_exp = acum_exp[C - 1:C, :] if fwd else acum_exp[0:1, :]
    head2 = lax.broadcasted_iota(jnp.int32, (C, 256), 1) >> 6
    ys = []
    for g in range(SSD_GROUPS):
        gl, gh = 256 * g, 256 * (g + 1)
        bg = bc[:, 128 * g:128 * (g + 1)]
        cg = bc[:, 256 + 128 * g:256 + 128 * (g + 1)]
        cb = _dot_nt(cg, bg)
        ms = []
        for rr in range(SSD_HPG):
            ln = base + SSD_HPG * g + rr
            diff = acum[:, ln:ln + 1] - acum_t[ln:ln + 1, :]
            seg = jnp.where(tri, jnp.exp(jnp.minimum(diff, 0.0)), 0.0)
            ms.append((cb * seg).astype(BF16))
        res = _dot(jnp.concatenate(ms, axis=0), xdt_b[:, gl:gh])
        yg = jnp.zeros((C, 256), F32)
        for rr in range(SSD_HPG):
            yg = yg + jnp.where(head2 == rr, res[rr * C:(rr + 1) * C, :], 0.0)
        sg = st_ref[g]
        yoff = _dot(cg, sg.astype(BF16)) * jnp.exp(acum_exp[:, gl:gh])
        wst = jnp.exp(al_exp[:, gl:gh] - acum_exp[:, gl:gh])
        ug = _dot_tn(bg, (xdt[:, gl:gh] * wst).astype(BF16))
        st_ref[g] = sg * jnp.exp(al_exp[:, gl:gh]) + ug
        ys.append(yg + yoff)
    y = jnp.concatenate(ys, axis=1)
    if dvec is not None:
        y = y + dvec * xs
    return y


def _ssd_kernel(xf_ref, bcf_ref, mf_ref, xb_ref, bcb_ref, mb_ref, af_ref, ab_ref, d_ref,
                trif_ref, trib_ref, ef_ref, eb_ref, yf_ref, yb_ref, stf, stb):
    s = pl.program_id(1)

    @pl.when(s == 0)
    def _():
        stf[...] = jnp.zeros_like(stf)
        stb[...] = jnp.zeros_like(stb)

    nsub = TB // SSD_C

    for i in range(nsub):
        sf = pl.ds(i * SSD_C, SSD_C)
        sb = pl.ds((nsub - 1 - i) * SSD_C, SSD_C)
        yf_ref[0, sf, :] = _ssd_dir(xf_ref[0, sf, :], bcf_ref[0, sf, :], mf_ref[0, sf, :], af_ref[...], d_ref[...],
                                    stf, trif_ref[...], ef_ref[...], DT_F, True)
        yb_ref[0, sb, :] = _ssd_dir(xb_ref[0, sb, :], bcb_ref[0, sb, :], mb_ref[0, sb, :], ab_ref[...], None,
                                    stb, trib_ref[...], eb_ref[...], DT_B, False)


def _expand_matrix(base):
    e = np.zeros((LANES, SSD_INNER), np.float32)
    for h in range(SSD_HEADS):
        e[base + h, SSD_HEADDIM * h:SSD_HEADDIM * (h + 1)] = 1.0
    return jnp.asarray(e, BF16)


def _ssd_call(xs, bc, misc, a_f, a_b, dvec, L):
    bsz, ls, _ = xs.shape
    ns = ls // TB
    nx = L // TB
    trif = jnp.asarray(np.tril(np.ones((SSD_C, SSD_C), np.float32)), BF16)
    trib = jnp.asarray(np.triu(np.ones((SSD_C, SSD_C), np.float32)), BF16)
    ef, eb = _expand_matrix(DT_F), _expand_matrix(DT_B)
    f = lambda n: pl.BlockSpec((1, TB, n), lambda b, s: (b, _fwd_blk(s), 0))
    r = lambda n: pl.BlockSpec((1, TB, n), lambda b, s: (b, _bwd_blk(s, ns), 0))
    const = lambda a: pl.BlockSpec(a.shape, lambda b, s: (0,) * a.ndim)
    return pl.pallas_call(
        _ssd_kernel,
        grid=(bsz, ns),
        in_specs=[f(SSD_INNER), f(512), f(LANES), r(SSD_INNER), r(512), r(LANES),
                  const(a_f), const(a_b), const(dvec), const(trif), const(trib), const(ef), const(eb)],
        out_specs=[pl.BlockSpec((1, TB, SSD_INNER), lambda b, s: (b, jnp.maximum(s - 1, 0), 0)),
                   pl.BlockSpec((1, TB, SSD_INNER), lambda b, s: (b, jnp.where(s == 0, nx - 1, nx - s), 0))],
        out_shape=[jax.ShapeDtypeStruct((bsz, L, SSD_INNER), F32)] * 2,
        scratch_shapes=[pltpu.VMEM((SSD_GROUPS, SSD_STATE, SSD_HPG * SSD_HEADDIM), F32)] * 2,
        compiler_params=_params("arbitrary", "arbitrary"),
    )(xs, bc, misc, xs, bc, misc, a_f, a_b, dvec, trif, trib, ef, eb)


def _outproj_kernel(of_ref, ob_ref, g_ref, yf_ref, yb_ref, z_ref, x_ref, g1_ref, sh2_ref, sc2_ref,
                    gn_ref, sn_ref, wo_ref, n2_ref, wrh_ref, wrl_ref, br_ref,
                    x1_ref, h2_ref, eidx_ref, gate_ref, cnt_ref):
    first = jnp.logical_and(pl.program_id(0) == 0, pl.program_id(1) == 0)

    @pl.when(first)
    def _():
        cnt_ref[...] = jnp.zeros_like(cnt_ref)

    o = of_ref[0] + ob_ref[0]
    gg = g_ref[0]
    parts = []
    for h in range(GLA_HEADS):
        lo, hi = GLA_DV * h, GLA_DV * (h + 1)
        oh = o[:, lo:hi]
        ms = jnp.mean(oh * oh, axis=-1, keepdims=True)
        gh = gg[:, lo:hi]
        parts.append((oh * lax.rsqrt(ms + EPS) * gn_ref[:, lo:hi] * (gh * _sigmoid(gh))).astype(BF16))
    zz = z_ref[0]
    u = (yf_ref[0] + yb_ref[0]) * (zz * _sigmoid(zz))
    gw = SSD_INNER // SSD_GROUPS
    for g in range(SSD_GROUPS):
        lo, hi = gw * g, gw * (g + 1)
        ug = u[:, lo:hi]
        ms = jnp.mean(ug * ug, axis=-1, keepdims=True)
        parts.append((ug * lax.rsqrt(ms + EPS) * sn_ref[:, lo:hi]).astype(BF16))
    mix = jnp.concatenate(parts, axis=1)
    x1 = x_ref[0] + g1_ref[0] * _dot(mix, wo_ref[...])
    x1_ref[0] = x1
    ms = jnp.mean(x1 * x1, axis=-1, keepdims=True)
    h2 = (x1 * lax.rsqrt(ms + EPS) * n2_ref[...]) * (1.0 + sc2_ref[0]) + sh2_ref[0]
    h2_ref[0] = h2
    h_hi = h2.astype(BF16)
    h_lo = (h2 - h_hi.astype(F32)).astype(BF16)
    logits = (_dot(h_hi, wrh_ref[...]) + _dot(h_lo, wrh_ref[...]) + _dot(h_hi, wrl_ref[...])) + br_ref[...]
    lane = lax.broadcasted_iota(jnp.int32, (TB, LANES), 1).astype(F32)
    work = logits
    eidx = jnp.full((TB, LANES), -1.0, F32)
    gates = jnp.zeros((TB, LANES), F32)
    sel = jnp.zeros((TB, LANES), F32)
    m0 = None
    den = jnp.zeros((TB, 1), F32)
    for kk in range(TOP_K):
        m = jnp.max(work, axis=-1, keepdims=True)
        idx = jnp.min(jnp.where(work == m, lane, float(LANES)), axis=-1, keepdims=True)
        hit = lane == idx
        if m0 is None:
            m0 = m
        e = jnp.exp(m - m0)
        den = den + e
        eidx = jnp.where(lane == float(kk), idx, eidx)
        gates = jnp.where(lane == float(kk), e, gates)
        sel = jnp.where(hit, 1.0, sel)
        work = jnp.where(hit, NEG_BIG, work)
    eidx_ref[0] = eidx.astype(jnp.int32)
    gate_ref[0] = gates / den
    cnt_ref[...] += jnp.sum(sel, axis=0, keepdims=True)


def _outproj_call(o_f, o_b, g_all, y_f, y_b, z_all, x, g1, sh2, sc2, gn, sn, wo, n2, wr_hi, wr_lo, br):
    bsz, L, d = x.shape
    nj = L // TB
    tok = lambda n: pl.BlockSpec((1, TB, n), lambda b, j: (b, j, 0))
    tok_off = lambda n: pl.BlockSpec((1, TB, n), lambda b, j: (b, j + 1, 0))
    const = lambda a: pl.BlockSpec(a.shape, lambda b, j: (0,) * a.ndim)
    mod = pl.BlockSpec((1, 1, d), lambda b, j: (b, 0, 0))
    return pl.pallas_call(
        _outproj_kernel,
        grid=(bsz, nj),
        in_specs=[tok(GLA_V), tok(GLA_V), tok_off(GLA_V), tok(SSD_INNER), tok(SSD_INNER), tok_off(SSD_INNER),
                  tok(d), mod, mod, mod, const(gn), const(sn), const(wo), const(n2),
                  const(wr_hi), const(wr_lo), const(br)],
        out_specs=[tok(d), tok(d), tok(LANES), tok(LANES), pl.BlockSpec((1, LANES), lambda b, j: (0, 0))],
        out_shape=[jax.ShapeDtypeStruct((bsz, L, d), F32), jax.ShapeDtypeStruct((bsz, L, d), F32),
                   jax.ShapeDtypeStruct((bsz, L, LANES), jnp.int32), jax.ShapeDtypeStruct((bsz, L, LANES), F32),
                   jax.ShapeDtypeStruct((1, LANES), F32)],
        compiler_params=_params("arbitrary", "arbitrary"),
    )(o_f, o_b, g_all, y_f, y_b, z_all, x, g1, sh2, sc2, gn, sn, wo, n2, wr_hi, wr_lo, br)


def _pos_kernel(eidx_ref, pstart_ref, lst_ref, dest_ref, carry):
    @pl.when(pl.program_id(0) == 0)
    def _():
        carry[...] = pstart_ref[...]

    eidx = eidx_ref[...]
    lane = lax.broadcasted_iota(jnp.int32, (TB, LANES), 1)
    hits = [lane == eidx[:, kk:kk + 1] for kk in range(TOP_K)]
    sel = jnp.zeros((TB, LANES), F32)
    for hmask in hits:
        sel = jnp.where(hmask, 1.0, sel)
    pos = carry[...] + _dot(lst_ref[...], sel.astype(BF16))
    dest = jnp.zeros((TB, LANES), jnp.int32)
    for kk, hmask in enumerate(hits):
        dk = jnp.sum(jnp.where(hmask, pos, 0.0), axis=-1, keepdims=True)
        dest = jnp.where(lane == kk, dk.astype(jnp.int32), dest)
    dest_ref[...] = dest
    carry[...] += jnp.sum(sel, axis=0, keepdims=True)


def _pos_call(eidx, pstart):
    T = eidx.shape[0]
    lst = jnp.asarray(np.tril(np.ones((TB, TB), np.float32), -1), BF16)
    return pl.pallas_call(
        _pos_kernel,
        grid=(T // TB,),
        in_specs=[pl.BlockSpec((TB, LANES), lambda i: (i, 0)),
                  pl.BlockSpec((1, LANES), lambda i: (0, 0)),
                  pl.BlockSpec((TB, TB), lambda i: (0, 0))],
        out_specs=pl.BlockSpec((TB, LANES), lambda i: (i, 0)),
        out_shape=jax.ShapeDtypeStruct((T, LANES), jnp.int32),
        scratch_shapes=[pltpu.VMEM((1, LANES), F32)],
        compiler_params=_params("arbitrary"),
    )(eidx, pstart, lst)


def _dispatch_kernel(fill_ref, dest_ref, h_ref, xg_ref, sem, fill_sem):
    @pl.when(pl.program_id(0) == 0)
    def _():
        def fill(n):
            blk = pl.multiple_of(fill_ref[n] * MOE_BLOCK, MOE_BLOCK)
            return pltpu.make_async_copy(h_ref, xg_ref.at[pl.ds(blk, MOE_BLOCK), :], fill_sem)

        for n in range(2 * N_EXPERTS):
            pl.when(fill_ref[n] >= 0)(lambda n=n: fill(n).start())
        for n in range(2 * N_EXPERTS):
            pl.when(fill_ref[n] >= 0)(lambda n=n: fill(n).wait())

    def body(r, carry):
        for kk in range(TOP_K):
            d = dest_ref[0, 0, TOP_K * r + kk]
            pltpu.make_async_copy(h_ref.at[pl.ds(r, 1), :], xg_ref.at[pl.ds(d, 1), :], sem).start()
        return carry

    lax.fori_loop(0, TB, body, 0)
    for kk in range(TOP_K):
        pltpu.make_async_copy(h_ref, xg_ref.at[pl.ds(0, TB), :], sem).wait()


def _dispatch_call(fill_blocks, dest3, h2, P):
    assert TB == MOE_BLOCK
    T, d = h2.shape
    grid_spec = pltpu.PrefetchScalarGridSpec(
        num_scalar_prefetch=1,
        grid=(T // TB,),
        in_specs=[pl.BlockSpec((1, 1, TB * TOP_K), lambda i, pad: (i, 0, 0), memory_space=pltpu.SMEM),
                  pl.BlockSpec((TB, d), lambda i, pad: (i, 0))],
        out_specs=pl.BlockSpec(memory_space=pl.ANY),
        scratch_shapes=[pltpu.SemaphoreType.DMA(()), pltpu.SemaphoreType.DMA(())],
    )
    return pl.pallas_call(
        _dispatch_kernel,
        grid_spec=grid_spec,
        out_shape=jax.ShapeDtypeStruct((P, d), F32),
        compiler_params=_params("arbitrary"),
    )(fill_blocks, dest3, h2)


def _moe_kernel(be_ref, x_ref, wgu_ref, bgu_ref, wd_ref, bd_ref, y_ref, wgu_b, wd_b, act_s):
    i = pl.program_id(0)
    n_used = be_ref[pl.num_programs(0)]
    changed = jnp.logical_or(i == 0, be_ref[i] != be_ref[jnp.maximum(i - 1, 0)])

    @pl.when(changed)
    def _():
        wgu_b[...] = wgu_ref[0].astype(BF16)
        wd_b[...] = wd_ref[0].astype(BF16)

    @pl.when(i < n_used)
    def _():
        xb = x_ref[...].astype(BF16)
        cw = 256
        for c in range(D_FF // cw):
            lo, hi = c * cw, (c + 1) * cw
            gate = _dot(xb, wgu_b[:, lo:hi]) + bgu_ref[0, :, lo:hi]
            up = _dot(xb, wgu_b[:, D_FF + lo:D_FF + hi]) + bgu_ref[0, :, D_FF + lo:D_FF + hi]
            gate = jnp.minimum(gate, SWIGLU_LIMIT)
            up = jnp.clip(up, -SWIGLU_LIMIT, SWIGLU_LIMIT)
            act_s[:, lo:hi] = ((up + 1.0) * (gate * _sigmoid(SWIGLU_ALPHA * gate))).astype(BF16)
        y_ref[...] = _dot(act_s[...], wd_b[...]) + bd_ref[0]

    @pl.when(i >= n_used)
    def _():
        y_ref[...] = jnp.zeros_like(y_ref)


def _moe_call(blk_e, xg, wgu, bgu, wd, bd):
    P, d = xg.shape
    nb = P // MOE_BLOCK
    ne, _, f2 = wgu.shape
    grid_spec = pltpu.PrefetchScalarGridSpec(
        num_scalar_prefetch=1,
        grid=(nb,),
        in_specs=[pl.BlockSpec((MOE_BLOCK, d), lambda i, be: (jnp.minimum(i, be[nb] - 1), 0)),
                  pl.BlockSpec((1, d, f2), lambda i, be: (be[i], 0, 0)),
                  pl.BlockSpec((1, 1, f2), lambda i, be: (be[i], 0, 0)),
                  pl.BlockSpec((1, f2 // 2, d), lambda i, be: (be[i], 0, 0)),
                  pl.BlockSpec((1, 1, d), lambda i, be: (be[i], 0, 0))],
        out_specs=pl.BlockSpec((MOE_BLOCK, d), lambda i, be: (i, 0)),
        scratch_shapes=[pltpu.VMEM((d, f2), BF16), pltpu.VMEM((f2 // 2, d), BF16),
                        pltpu.VMEM((MOE_BLOCK, f2 // 2), BF16)],
    )
    return pl.pallas_call(
        _moe_kernel,
        grid_spec=grid_spec,
        out_shape=jax.ShapeDtypeStruct((P, d), F32),
        compiler_params=_params("arbitrary"),
    )(blk_e, xg, wgu, bgu, wd, bd)


def _combine_kernel(dcur_ref, dnext_ref, y_ref, x1_ref, gate_ref, g2_ref, fn_ref, o_ref, buf, sems):
    i = pl.program_id(0)
    n = pl.num_programs(0)
    slot = i & 1

    def gather(dref, s):
        def body(r, carry):
            for kk in range(TOP_K):
                d = dref[0, 0, TOP_K * r + kk]
                pltpu.make_async_copy(y_ref.at[pl.ds(d, 1), :], buf.at[s, kk, pl.ds(r, 1), :],
                                      sems.at[s]).start(priority=kk % 2)
            return carry

        lax.fori_loop(0, TB, body, 0)

    @pl.when(i == 0)
    def _():
        gather(dcur_ref, 0)

    @pl.when(i + 1 < n)
    def _():
        gather(dnext_ref, 1 - slot)

    for kk in range(TOP_K):
        pltpu.make_async_copy(y_ref.at[pl.ds(0, TB), :], buf.at[slot, kk], sems.at[slot]).wait()

    gates = gate_ref[...]
    f = jnp.zeros(o_ref.shape, F32)
    for kk in range(TOP_K):
        f = f + gates[:, kk:kk + 1] * buf[slot, kk]
    xo = x1_ref[...] + g2_ref[0] * f
    ms = jnp.mean(xo * xo, axis=-1, keepdims=True)
    o_ref[...] = xo * lax.rsqrt(ms + EPS) * fn_ref[...]


def _combine_call(dest3, y, x1, gates, g2, fn):
    T, d = x1.shape
    n = T // TB
    per_batch = n // g2.shape[0]
    return pl.pallas_call(
        _combine_kernel,
        grid=(n,),
        in_specs=[pl.BlockSpec((1, 1, TB * TOP_K), lambda i: (i, 0, 0), memory_space=pltpu.SMEM),
                  pl.BlockSpec((1, 1, TB * TOP_K), lambda i: (jnp.minimum(i + 1, n - 1), 0, 0),
                               memory_space=pltpu.SMEM),
                  pl.BlockSpec(memory_space=pl.ANY),
                  pl.BlockSpec((TB, d), lambda i: (i, 0)),
                  pl.BlockSpec((TB, LANES), lambda i: (i, 0)),
                  pl.BlockSpec((1, 1, d), lambda i: (i // per_batch, 0, 0)),
                  pl.BlockSpec((1, d), lambda i: (0, 0))],
        out_specs=pl.BlockSpec((TB, d), lambda i: (i, 0)),
        out_shape=jax.ShapeDtypeStruct((T, d), F32),
        scratch_shapes=[pltpu.VMEM((2, TOP_K, TB, d), F32), pltpu.SemaphoreType.DMA((2,))],
        compiler_params=_params("arbitrary"),
    )(dest3, dest3, y, x1, gates, g2, fn)


def _layer(x, c, ctx, c_ctx, w_mod, b_mod, norm1, w_in, gla_w_gk_up, gla_b_gk, gla_norm,
           ssd_conv_w, ssd_conv_b, ssd_dt_bias, ssd_A_log, ssd_D, ssd_norm, w_out,
           norm2, w_router, b_router, w_gate_up, b_gate_up, w_down, b_down, final_norm):
    bsz, L, d = x.shape
    lc = ctx.shape[1]
    assert lc == TB and L % TB == 0 and TB % GRID_W == 0

    cin = jnp.zeros((8, d), F32).at[:bsz].set(c).at[bsz].set(c_ctx)
    mod = _mod_call(cin, w_mod, b_mod.reshape(1, -1))[:bsz + 1]
    sh1, sc1, g1, sh2, sc2, g2 = [m.reshape(bsz + 1, 1, d) for m in jnp.split(mod, 6, axis=-1)]

    o = np.cumsum((0, GLA_QK, GLA_QK, GLA_V, GLA_V, GLA_RANK, SSD_INNER, SSD_CONV_DIM, SSD_HEADS))
    wq, wk, wv, wg, wlow, wz, wx, wdt = [w_in[:, int(a):int(b)] for a, b in zip(o[:-1], o[1:])]
    w_misc = jnp.concatenate([wlow, wdt, wdt, jnp.zeros((d, LANES - GLA_RANK - 2 * SSD_HEADS), F32)], axis=1)
    w_cat = jnp.concatenate([wq, wk, wv, wg, wz, wx, w_misc], axis=1).astype(BF16)
    wup = jnp.zeros((LANES, 2 * GLA_QK), F32).at[:GLA_RANK].set(
        jnp.concatenate([gla_w_gk_up[0], gla_w_gk_up[1]], axis=1)).astype(BF16)
    bup = jnp.concatenate([gla_b_gk[0], gla_b_gk[1]]).reshape(1, -1)
    dtb = jnp.zeros((1, LANES), F32).at[0, DT_F:DT_F + SSD_HEADS].set(ssd_dt_bias[0]) \
                                    .at[0, DT_B:DT_B + SSD_HEADS].set(ssd_dt_bias[1])
    q, k, v, g_all, z_all, xbc, ld, misc = _inproj_call(
        x, ctx, sh1, sc1, norm1.reshape(1, d), w_cat, wup, bup, dtb)

    xs, bc = _conv_call(xbc, ssd_conv_w.reshape(9, SSD_CONV_DIM), ssd_conv_b.reshape(1, -1))

    o_f, o_b = _gla_call(q, k, v, ld, L)

    a_neg = -jnp.exp(ssd_A_log.astype(F32))
    a_f = jnp.zeros((1, LANES), F32).at[0, DT_F:DT_F + SSD_HEADS].set(a_neg[0])
    a_b = jnp.zeros((1, LANES), F32).at[0, DT_B:DT_B + SSD_HEADS].set(a_neg[1])
    dvec = jnp.repeat(ssd_D, SSD_HEADDIM).reshape(1, SSD_INNER)
    y_f, y_b = _ssd_call(xs, bc, misc, a_f, a_b, dvec, L)

    wr = jnp.zeros((d, LANES), F32).at[:, :N_EXPERTS].set(w_router)
    wr_hi = wr.astype(BF16)
    wr_lo = (wr - wr_hi.astype(F32)).astype(BF16)
    br = jnp.full((1, LANES), NEG_BIG, F32).at[0, :N_EXPERTS].set(b_router)
    x1, h2, eidx, gates, counts = _outproj_call(
        o_f, o_b, g_all, y_f, y_b, z_all, x, g1[:bsz], sh2[:bsz], sc2[:bsz],
        jnp.tile(gla_norm, GLA_HEADS).reshape(1, -1), ssd_norm.reshape(1, -1), w_out.astype(BF16),
        norm2.reshape(1, d), wr_hi, wr_lo, br)

    T = bsz * L
    cnt = counts[0, :N_EXPERTS].astype(jnp.int32)
    padded = ((cnt + MOE_BLOCK - 1) // MOE_BLOCK) * MOE_BLOCK
    pend = jnp.cumsum(padded)
    pstart = pend - padded
    n_blocks = -(-(T * TOP_K) // MOE_BLOCK) + N_EXPERTS
    blk_start = jnp.arange(n_blocks, dtype=jnp.int32) * MOE_BLOCK
    blk_e = jnp.minimum(jnp.sum(pend[None, :] <= blk_start[:, None], axis=1), N_EXPERTS - 1).astype(jnp.int32)
    n_used = (pend[-1] // MOE_BLOCK).astype(jnp.int32)
    blk_info = jnp.concatenate([blk_e, n_used[None]])
    tail = n_used + jnp.arange(N_EXPERTS, dtype=jnp.int32)
    fill_blocks = jnp.concatenate([
        jnp.where(padded > 0, pend // MOE_BLOCK - 1, -1),
        jnp.where(tail < n_blocks, tail, -1)]).astype(jnp.int32)
    pstart_row = jnp.zeros((1, LANES), F32).at[0, :N_EXPERTS].set(pstart.astype(F32))
    dest = _pos_call(eidx.reshape(T, LANES), pstart_row)
    dest3 = dest[:, :TOP_K].reshape(T // TB, 1, TB * TOP_K)

    xg = _dispatch_call(fill_blocks, dest3, h2.reshape(T, d), n_blocks * MOE_BLOCK)
    y = _moe_call(blk_info, xg, w_gate_up, b_gate_up.reshape(N_EXPERTS, 1, -1), w_down,
                  b_down.reshape(N_EXPERTS, 1, -1))
    out = _combine_call(dest3, y, x1.reshape(T, d), gates.reshape(T, LANES), g2[:bsz], final_norm.reshape(1, d))
    return out.reshape(bsz, L, d)


def kernel(x, c, ctx, c_ctx, w_mod, b_mod, norm1, w_in, gla_w_gk_up, gla_b_gk, gla_norm, ssd_conv_w, ssd_conv_b, ssd_dt_bias, ssd_A_log, ssd_D, ssd_norm, w_out, norm2, w_router, b_router, w_gate_up, b_gate_up, w_down, b_down, final_norm):
    assert w_mod.shape[0] == 1, "single-layer kernel"
    return _layer(x, c, ctx, c_ctx, w_mod[0], b_mod[0], norm1[0], w_in[0], gla_w_gk_up[0], gla_b_gk[0],
                  gla_norm[0], ssd_conv_w[0], ssd_conv_b[0], ssd_dt_bias[0], ssd_A_log[0], ssd_D[0],
                  ssd_norm[0], w_out[0], norm2[0], w_router[0], b_router[0], w_gate_up[0], b_gate_up[0],
                  w_down[0], b_down[0], final_norm)
```

```python
import functools

import numpy as np
import jax
import jax.numpy as jnp
from jax import lax
from jax.experimental import pallas as pl
from jax.experimental.pallas import tpu as pltpu

F32 = jnp.float32
BF16 = jnp.bfloat16

EPS = 1e-6
GRID_W = 64
GLA_HEADS = 4
GLA_DK = 64
GLA_DV = 128
GLA_QK = GLA_HEADS * GLA_DK
GLA_V = GLA_HEADS * GLA_DV
GLA_RANK = 16
GLA_GATE_NORM = 16.0
SSD_HEADDIM = 64
SSD_INNER = 512
SSD_HEADS = 8
SSD_GROUPS = 2
SSD_HPG = 4
SSD_STATE = 128
SSD_CONV_DIM = 1024
N_EXPERTS = 32
TOP_K = 4
D_FF = 1024
SWIGLU_LIMIT = 7.0
SWIGLU_ALPHA = 1.702
MOE_BLOCK = 256

TB = 256
GLA_C = 64
SSD_C = 128
LANES = 128
EXP_CLAMP = 80.0
DT_F = 16
DT_B = 24
NEG_BIG = -1e30
RUN = 8
RL = 1280
CD_LANES = 256
VMEM_LIMIT = 56 * 1024 * 1024


def _dot(a, b):
    return jnp.dot(a, b, preferred_element_type=F32)


def _dot_nt(a, b):
    return lax.dot_general(a, b, (((1,), (1,)), ((), ())), preferred_element_type=F32)


def _dot_tn(a, b):
    return lax.dot_general(a, b, (((0,), (0,)), ((), ())), preferred_element_type=F32)


def _split3(a):
    hi = a.astype(BF16)
    r1 = a - hi.astype(F32)
    mid = r1.astype(BF16)
    lo = (r1 - mid.astype(F32)).astype(BF16)
    return hi, mid, lo


def _dot_exact_r(m, a):
    hi, mid, lo = _split3(a)
    return _dot(m, hi) + _dot(m, mid) + _dot(m, lo)


def _dot_hilo_l(a, m):
    hi = a.astype(BF16)
    lo = (a - hi.astype(F32)).astype(BF16)
    return _dot(hi, m) + _dot(lo, m)


def _sigmoid(x):
    return 1.0 / (1.0 + jnp.exp(-x))


def _softplus(x):
    return jnp.maximum(x, 0.0) + jnp.log1p(jnp.exp(-jnp.abs(x)))


def _params(*sem):
    return pltpu.CompilerParams(dimension_semantics=sem, vmem_limit_bytes=VMEM_LIMIT)


def _mod_kernel(c_ref, w_ref, b_ref, o_ref):
    c = c_ref[...]
    s = c * _sigmoid(c)
    s_hi = s.astype(BF16)
    s_lo = (s - s_hi.astype(F32)).astype(BF16)
    w = w_ref[...]
    w_hi = w.astype(BF16)
    w_lo = (w - w_hi.astype(F32)).astype(BF16)
    o_ref[...] = _dot(s_hi, w_hi) + _dot(s_lo, w_hi) + _dot(s_hi, w_lo) + b_ref[...]


def _mod_call(cin, w, b):
    rows, d = cin.shape
    n = w.shape[1]
    tn = 1536
    return pl.pallas_call(
        _mod_kernel,
        grid=(n // tn,),
        in_specs=[pl.BlockSpec((rows, d), lambda i: (0, 0)),
                  pl.BlockSpec((d, tn), lambda i: (0, i)),
                  pl.BlockSpec((1, tn), lambda i: (0, i))],
        out_specs=pl.BlockSpec((rows, tn), lambda i: (0, i)),
        out_shape=jax.ShapeDtypeStruct((rows, n), F32),
        compiler_params=_params("arbitrary"),
    )(cin, w, b)


_C_Q, _C_K, _C_V, _C_G, _C_Z, _C_X, _C_M, _C_END = 0, 256, 512, 1024, 1536, 2048, 3072, 3200


def _inproj_kernel(x_ref, ctx_ref, sh_ref, sc_ref, n1_ref, w_ref, wup_ref, bup_ref, dtb_ref,
                   q_ref, k_ref, v_ref, g_ref, z_ref, xbc_ref, ld_ref, misc_ref, h_scr):
    j = pl.program_id(1)

    def normmod(xv):
        ms = jnp.mean(xv * xv, axis=-1, keepdims=True)
        y = xv * lax.rsqrt(ms + EPS) * n1_ref[...]
        return (y * (1.0 + sc_ref[0]) + sh_ref[0]).astype(BF16)

    @pl.when(j == 0)
    def _():
        h_scr[...] = normmod(ctx_ref[0])

    @pl.when(j > 0)
    def _():
        h_scr[...] = normmod(x_ref[0])

    h = h_scr[...]

    def mm(lo, hi):
        return _dot(h, w_ref[:, lo:hi])

    q_ref[0] = (mm(_C_Q, _C_K) * (GLA_DK ** -0.5)).astype(BF16)
    k_ref[0] = mm(_C_K, _C_V).astype(BF16)
    v_ref[0] = mm(_C_V, _C_G).astype(BF16)
    g_ref[0] = mm(_C_G, _C_Z)
    z_ref[0] = mm(_C_Z, _C_X)
    xbc_ref[0] = mm(_C_X, _C_M)
    m = mm(_C_M, _C_END)
    zz = _dot(m.astype(BF16), wup_ref[...]) + bup_ref[...]
    ld_ref[0] = -_softplus(-zz) * (1.0 / GLA_GATE_NORM)
    misc_ref[0] = _softplus(m + dtb_ref[...])


def _inproj_call(x, ctx, sh1, sc1, n1, w_cat, wup, bup, dtb):
    bsz, L, d = x.shape
    lc = ctx.shape[1]
    ls = lc + L
    nj = ls // TB
    tok = lambda n: pl.BlockSpec((1, TB, n), lambda b, j: (b, j, 0))
    const = lambda a: pl.BlockSpec(a.shape, lambda b, j: (0,) * a.ndim)
    modspec = pl.BlockSpec((1, 1, d), lambda b, j: (jnp.where(j == 0, bsz, b), 0, 0))
    outs = [(GLA_QK, BF16), (GLA_QK, BF16), (GLA_V, BF16), (GLA_V, F32), (SSD_INNER, F32),
            (SSD_CONV_DIM, F32), (2 * GLA_QK, F32), (LANES, F32)]
    return pl.pallas_call(
        _inproj_kernel,
        grid=(bsz, nj),
        in_specs=[pl.BlockSpec((1, TB, d), lambda b, j: (b, jnp.maximum(j - 1, 0), 0)),
                  pl.BlockSpec((1, TB, d), lambda b, j: (b, 0, 0)),
                  modspec, modspec, const(n1), const(w_cat), const(wup), const(bup), const(dtb)],
        out_specs=[tok(n) for n, _ in outs],
        out_shape=[jax.ShapeDtypeStruct((bsz, ls, n), dt) for n, dt in outs],
        scratch_shapes=[pltpu.VMEM((TB, d), BF16)],
        compiler_params=_params("arbitrary", "arbitrary"),
    )(x, ctx, sh1, sc1, n1, w_cat, wup, bup, dtb)


_EXT_PAD = 8
_EXT_BASE = _EXT_PAD + GRID_W
_EXT_ROWS = 2 * _EXT_PAD + 2 * GRID_W + TB


def _conv_kernel(prev_ref, cur_ref, next_ref, w_ref, b_ref, xs_ref, bc_ref, ext):
    j = pl.program_id(1)
    nj = pl.num_programs(1)
    is_ctx = j == 0
    zpad = jnp.zeros((_EXT_PAD, SSD_CONV_DIM), F32)
    ext[0:_EXT_PAD, :] = zpad
    ext[_EXT_ROWS - _EXT_PAD:_EXT_ROWS, :] = zpad
    ext[_EXT_PAD:_EXT_BASE, :] = jnp.where(j >= 2, prev_ref[0], 0.0)
    ext[_EXT_BASE:_EXT_BASE + TB, :] = cur_ref[0]
    ext[_EXT_BASE + TB:_EXT_BASE + TB + GRID_W, :] = jnp.where(
        jnp.logical_and(j >= 1, j <= nj - 2), next_ref[0], 0.0)

    t = lax.broadcasted_iota(jnp.int32, (TB, LANES), 0)
    col = t & (GRID_W - 1)
    ok_l = jnp.where(is_ctx, t, col) >= 1
    ok_r = jnp.where(is_ctx, t - (TB - GRID_W), col) <= GRID_W - 2
    lat = jnp.where(is_ctx, 0.0, 1.0)

    for c in range(SSD_CONV_DIM // LANES):
        lo, hi = c * LANES, (c + 1) * LANES
        acc = jnp.zeros((TB, LANES), F32)
        for dr in (-1, 0, 1):
            for dc in (-1, 0, 1):
                start = _EXT_BASE + GRID_W * dr + dc
                tap = ext[start:start + TB, lo:hi]
                wi = 3 * (dr + 1) + (dc + 1)
                wv = w_ref[wi:wi + 1, lo:hi]
                if dr != 0:
                    wv = wv * lat
                if dc == -1:
                    tap = jnp.where(ok_l, tap, 0.0)
                elif dc == 1:
                    tap = jnp.where(ok_r, tap, 0.0)
                acc = acc + tap * wv
        y = acc + b_ref[:, lo:hi]
        y = y * _sigmoid(y)
        if c < SSD_INNER // LANES:
            xs_ref[0, :, lo:hi] = y
        else:
            bc_ref[0, :, lo - SSD_INNER:hi - SSD_INNER] = y.astype(BF16)


def _conv_call(xbc, w9, bias):
    bsz, ls, ch = xbc.shape
    nj = ls // TB
    rpb = TB // GRID_W
    nrow = ls // GRID_W
    return pl.pallas_call(
        _conv_kernel,
        grid=(bsz, nj),
        in_specs=[pl.BlockSpec((1, GRID_W, ch), lambda b, j: (b, jnp.maximum(rpb * j - 1, 0), 0)),
                  pl.BlockSpec((1, TB, ch), lambda b, j: (b, j, 0)),
                  pl.BlockSpec((1, GRID_W, ch), lambda b, j: (b, jnp.minimum(rpb * j + rpb, nrow - 1), 0)),
                  pl.BlockSpec((9, ch), lambda b, j: (0, 0)),
                  pl.BlockSpec((1, ch), lambda b, j: (0, 0))],
        out_specs=[pl.BlockSpec((1, TB, SSD_INNER), lambda b, j: (b, j, 0)),
                   pl.BlockSpec((1, TB, ch - SSD_INNER), lambda b, j: (b, j, 0))],
        out_shape=[jax.ShapeDtypeStruct((bsz, ls, SSD_INNER), F32),
                   jax.ShapeDtypeStruct((bsz, ls, ch - SSD_INNER), BF16)],
        scratch_shapes=[pltpu.VMEM((_EXT_ROWS, ch), F32)],
        compiler_params=_params("arbitrary", "arbitrary"),
    )(xbc, xbc, xbc, w9, bias)


def _fwd_blk(s):
    return s


def _bwd_blk(s, ns):
    return jnp.where(s == 0, 0, ns - s)


def _gla_dir(q, k, v, la, st_ref, tri_m, fwd):
    C = GLA_C
    b = _dot_exact_r(tri_m, la)
    bt = b[C - 1:C, :] if fwd else b[0:1, :]
    r = 0.5 * bt
    qf = q.astype(F32)
    kf = k.astype(F32)
    qt = (qf * jnp.exp(jnp.minimum(b - r, EXP_CLAMP))).astype(BF16)
    kt = (kf * jnp.exp(jnp.minimum(r - b, EXP_CLAMP))).astype(BF16)
    head_k = lax.broadcasted_iota(jnp.int32, (C, GLA_QK), 1) >> 6
    kst = jnp.concatenate([jnp.where(head_k == h, kt, jnp.zeros_like(kt)) for h in range(GLA_HEADS)], axis=0)
    sc = _dot_nt(qt, kst)
    ii = lax.broadcasted_iota(jnp.int32, (C, GLA_HEADS * C), 0)
    jj = lax.broadcasted_iota(jnp.int32, (C, GLA_HEADS * C), 1) & (C - 1)
    causal = (jj <= ii) if fwd else (jj >= ii)
    p = jnp.where(causal, sc, 0.0).astype(BF16)
    head_v = lax.broadcasted_iota(jnp.int32, (C, GLA_V), 1) >> 7
    vst = jnp.concatenate([jnp.where(head_v == h, v, jnp.zeros_like(v)) for h in range(GLA_HEADS)], axis=0)
    o = _dot(p, vst)
    st = st_ref[...]
    qd = (qf * jnp.exp(b)).astype(BF16)
    o = o + _dot_nt(qd, st.astype(BF16))
    kd = (kf * jnp.exp(bt - b)).astype(BF16)
    u = _dot_tn(v, kd)
    rv = lax.broadcasted_iota(jnp.int32, (GLA_V, GLA_QK), 0) >> 7
    cv = lax.broadcasted_iota(jnp.int32, (GLA_V, GLA_QK), 1) >> 6
    st_ref[...] = st * jnp.exp(bt) + jnp.where(rv == cv, u, 0.0)
    return o


def _gla_kernel(qf_ref, kf_ref, vf_ref, lf_ref, qb_ref, kb_ref, vb_ref, lb_ref, trif_ref, trib_ref,
                of_ref, ob_ref, stf, stb):
    s = pl.program_id(1)

    @pl.when(s == 0)
    def _():
        stf[...] = jnp.zeros_like(stf)
        stb[...] = jnp.zeros_like(stb)

    nsub = TB // GLA_C

    for i in range(nsub):
        sf = pl.ds(i * GLA_C, GLA_C)
        sb = pl.ds((nsub - 1 - i) * GLA_C, GLA_C)
        of_ref[0, sf, :] = _gla_dir(qf_ref[0, sf, :], kf_ref[0, sf, :], vf_ref[0, sf, :], lf_ref[0, sf, :],
                                    stf, trif_ref[...], True)
        ob_ref[0, sb, :] = _gla_dir(qb_ref[0, sb, :], kb_ref[0, sb, :], vb_ref[0, sb, :], lb_ref[0, sb, :],
                                    stb, trib_ref[...], False)


def _gla_call(q, k, v, ld, L):
    bsz, ls, _ = q.shape
    ns = ls // TB
    nx = L // TB
    trif = jnp.asarray(np.tril(np.ones((GLA_C, GLA_C), np.float32)), BF16)
    trib = jnp.asarray(np.triu(np.ones((GLA_C, GLA_C), np.float32)), BF16)
    f = lambda n, lane=0: pl.BlockSpec((1, TB, n), lambda b, s: (b, _fwd_blk(s), lane))
    r = lambda n, lane=0: pl.BlockSpec((1, TB, n), lambda b, s: (b, _bwd_blk(s, ns), lane))
    tri = pl.BlockSpec((GLA_C, GLA_C), lambda b, s: (0, 0))
    return pl.pallas_call(
        _gla_kernel,
        grid=(bsz, ns),
        in_specs=[f(GLA_QK), f(GLA_QK), f(GLA_V), f(GLA_QK, 0),
                  r(GLA_QK), r(GLA_QK), r(GLA_V), r(GLA_QK, 1), tri, tri],
        out_specs=[pl.BlockSpec((1, TB, GLA_V), lambda b, s: (b, jnp.maximum(s - 1, 0), 0)),
                   pl.BlockSpec((1, TB, GLA_V), lambda b, s: (b, jnp.where(s == 0, nx - 1, nx - s), 0))],
        out_shape=[jax.ShapeDtypeStruct((bsz, L, GLA_V), F32)] * 2,
        scratch_shapes=[pltpu.VMEM((GLA_V, GLA_QK), F32)] * 2,
        compiler_params=_params("arbitrary", "arbitrary"),
    )(q, k, v, ld, q, k, v, ld, trif, trib)


def _ssd_dir(xs, bc, dtm, avec, dvec, st_ref, tri_m, e_m, base, fwd):
    C = SSD_C
    dt_exp = _dot_hilo_l(dtm, e_m)
    a = dtm * avec
    acum = _dot_exact_r(tri_m, a)
    acum_exp = _dot_hilo_l(acum, e_m)
    acum_t = acum.T
    xdt = xs * dt_exp
    xdt_b = xdt.astype(BF16)
    ii = lax.broadcasted_iota(jnp.int32, (C, C), 0)
    jj = lax.broadcasted_iota(jnp.int32, (C, C), 1)
    tri = (jj <= ii) if fwd else (jj >= ii)
    al_exp = acum_exp[C - 1:C, :] if fwd else acum_exp[0:1, :]
    head2 = lax.broadcasted_iota(jnp.int32, (C, 256), 1) >> 6
    ys = []
    for g in range(SSD_GROUPS):
        gl, gh = 256 * g, 256 * (g + 1)
        bg = bc[:, 128 * g:128 * (g + 1)]
        cg = bc[:, 256 + 128 * g:256 + 128 * (g + 1)]
        cb = _dot_nt(cg, bg)
        ms = []
        for rr in range(SSD_HPG):
            ln = base + SSD_HPG * g + rr
            diff = acum[:, ln:ln + 1] - acum_t[ln:ln + 1, :]
            seg = jnp.where(tri, jnp.exp(jnp.minimum(diff, 0.0)), 0.0)
            ms.append((cb * seg).astype(BF16))
        res = _dot(jnp.concatenate(ms, axis=0), xdt_b[:, gl:gh])
        yg = jnp.zeros((C, 256), F32)
        for rr in range(SSD_HPG):
            yg = yg + jnp.where(head2 == rr, res[rr * C:(rr + 1) * C, :], 0.0)
        sg = st_ref[g]
        yoff = _dot(cg, sg.astype(BF16)) * jnp.exp(acum_exp[:, gl:gh])
        wst = jnp.exp(al_exp[:, gl:gh] - acum_exp[:, gl:gh])
        ug = _dot_tn(bg, (xdt[:, gl:gh] * wst).astype(BF16))
        st_ref[g] = sg * jnp.exp(al_exp[:, gl:gh]) + ug
        ys.append(yg + yoff)
    y = jnp.concatenate(ys, axis=1)
    if dvec is not None:
        y = y + dvec * xs
    return y


def _ssd_kernel(xf_ref, bcf_ref, mf_ref, xb_ref, bcb_ref, mb_ref, af_ref, ab_ref, d_ref,
                trif_ref, trib_ref, ef_ref, eb_ref, yf_ref, yb_ref, stf, stb):
    s = pl.program_id(1)

    @pl.when(s == 0)
    def _():
        stf[...] = jnp.zeros_like(stf)
        stb[...] = jnp.zeros_like(stb)

    nsub = TB // SSD_C

    for i in range(nsub):
        sf = pl.ds(i * SSD_C, SSD_C)
        sb = pl.ds((nsub - 1 - i) * SSD_C, SSD_C)
        yf_ref[0, sf, :] = _ssd_dir(xf_ref[0, sf, :], bcf_ref[0, sf, :], mf_ref[0, sf, :], af_ref[...], d_ref[...],
                                    stf, trif_ref[...], ef_ref[...], DT_F, True)
        yb_ref[0, sb, :] = _ssd_dir(xb_ref[0, sb, :], bcb_ref[0, sb, :], mb_ref[0, sb, :], ab_ref[...], None,
                                    stb, trib_ref[...], eb_ref[...], DT_B, False)


def _expand_matrix(base):
    e = np.zeros((LANES, SSD_INNER), np.float32)
    for h in range(SSD_HEADS):
        e[base + h, SSD_HEADDIM * h:SSD_HEADDIM * (h + 1)] = 1.0
    return jnp.asarray(e, BF16)


def _ssd_call(xs, bc, misc, a_f, a_b, dvec, L):
    bsz, ls, _ = xs.shape
    ns = ls // TB
    nx = L // TB
    trif = jnp.asarray(np.tril(np.ones((SSD_C, SSD_C), np.float32)), BF16)
    trib = jnp.asarray(np.triu(np.ones((SSD_C, SSD_C), np.float32)), BF16)
    ef, eb = _expand_matrix(DT_F), _expand_matrix(DT_B)
    f = lambda n: pl.BlockSpec((1, TB, n), lambda b, s: (b, _fwd_blk(s), 0))
    r = lambda n: pl.BlockSpec((1, TB, n), lambda b, s: (b, _bwd_blk(s, ns), 0))
    const = lambda a: pl.BlockSpec(a.shape, lambda b, s: (0,) * a.ndim)
    return pl.pallas_call(
        _ssd_kernel,
        grid=(bsz, ns),
        in_specs=[f(SSD_INNER), f(512), f(LANES), r(SSD_INNER), r(512), r(LANES),
                  const(a_f), const(a_b), const(dvec), const(trif), const(trib), const(ef), const(eb)],
        out_specs=[pl.BlockSpec((1, TB, SSD_INNER), lambda b, s: (b, jnp.maximum(s - 1, 0), 0)),
                   pl.BlockSpec((1, TB, SSD_INNER), lambda b, s: (b, jnp.where(s == 0, nx - 1, nx - s), 0))],
        out_shape=[jax.ShapeDtypeStruct((bsz, L, SSD_INNER), F32)] * 2,
        scratch_shapes=[pltpu.VMEM((SSD_GROUPS, SSD_STATE, SSD_HPG * SSD_HEADDIM), F32)] * 2,
        compiler_params=_params("arbitrary", "arbitrary"),
    )(xs, bc, misc, xs, bc, misc, a_f, a_b, dvec, trif, trib, ef, eb)


def _outproj_kernel(of_ref, ob_ref, g_ref, yf_ref, yb_ref, z_ref, x_ref, g1_ref, sh2_ref, sc2_ref,
                    gn_ref, sn_ref, wo_ref, n2_ref, wrh_ref, wrl_ref, br_ref,
                    x1_ref, h2_ref, eidx_ref, gate_ref, cnt_ref):
    first = jnp.logical_and(pl.program_id(0) == 0, pl.program_id(1) == 0)

    @pl.when(first)
    def _():
        cnt_ref[...] = jnp.zeros_like(cnt_ref)

    o = of_ref[0] + ob_ref[0]
    gg = g_ref[0]
    parts = []
    for h in range(GLA_HEADS):
        lo, hi = GLA_DV * h, GLA_DV * (h + 1)
        oh = o[:, lo:hi]
        ms = jnp.mean(oh * oh, axis=-1, keepdims=True)
        gh = gg[:, lo:hi]
        parts.append((oh * lax.rsqrt(ms + EPS) * gn_ref[:, lo:hi] * (gh * _sigmoid(gh))).astype(BF16))
    zz = z_ref[0]
    u = (yf_ref[0] + yb_ref[0]) * (zz * _sigmoid(zz))
    gw = SSD_INNER // SSD_GROUPS
    for g in range(SSD_GROUPS):
        lo, hi = gw * g, gw * (g + 1)
        ug = u[:, lo:hi]
        ms = jnp.mean(ug * ug, axis=-1, keepdims=True)
        parts.append((ug * lax.rsqrt(ms + EPS) * sn_ref[:, lo:hi]).astype(BF16))
    mix = jnp.concatenate(parts, axis=1)
    x1 = x_ref[0] + g1_ref[0] * _dot(mix, wo_ref[...])
    x1_ref[0] = x1
    ms = jnp.mean(x1 * x1, axis=-1, keepdims=True)
    h2 = (x1 * lax.rsqrt(ms + EPS) * n2_ref[...]) * (1.0 + sc2_ref[0]) + sh2_ref[0]
    h_hi = h2.astype(BF16)
    h2_ref[0] = h_hi
    h_lo = (h2 - h_hi.astype(F32)).astype(BF16)
    logits = (_dot(h_hi, wrh_ref[...]) + _dot(h_lo, wrh_ref[...]) + _dot(h_hi, wrl_ref[...])) + br_ref[...]
    lane = lax.broadcasted_iota(jnp.int32, (TB, LANES), 1).astype(F32)
    work = logits
    eidx = jnp.full((TB, LANES), -1.0, F32)
    gates = jnp.zeros((TB, LANES), F32)
    sel = jnp.zeros((TB, LANES), F32)
    m0 = None
    den = jnp.zeros((TB, 1), F32)
    for kk in range(TOP_K):
        m = jnp.max(work, axis=-1, keepdims=True)
        idx = jnp.min(jnp.where(work == m, lane, float(LANES)), axis=-1, keepdims=True)
        hit = lane == idx
        if m0 is None:
            m0 = m
        e = jnp.exp(m - m0)
        den = den + e
        eidx = jnp.where(lane == float(kk), idx, eidx)
        gates = jnp.where(lane == float(kk), e, gates)
        sel = jnp.where(hit, 1.0, sel)
        work = jnp.where(hit, NEG_BIG, work)
    eidx_ref[0] = eidx.astype(jnp.int32)
    gate_ref[0] = gates / den
    cnt = jnp.sum(sel, axis=0, keepdims=True)
    cnt_ref[...] += jnp.floor((cnt + (RUN - 1.0)) * (1.0 / RUN)) * RUN


def _outproj_call(o_f, o_b, g_all, y_f, y_b, z_all, x, g1, sh2, sc2, gn, sn, wo, n2, wr_hi, wr_lo, br):
    bsz, L, d = x.shape
    nj = L // TB
    tok = lambda n: pl.BlockSpec((1, TB, n), lambda b, j: (b, j, 0))
    tok_off = lambda n: pl.BlockSpec((1, TB, n), lambda b, j: (b, j + 1, 0))
    const = lambda a: pl.BlockSpec(a.shape, lambda b, j: (0,) * a.ndim)
    mod = pl.BlockSpec((1, 1, d), lambda b, j: (b, 0, 0))
    return pl.pallas_call(
        _outproj_kernel,
        grid=(bsz, nj),
        in_specs=[tok(GLA_V), tok(GLA_V), tok_off(GLA_V), tok(SSD_INNER), tok(SSD_INNER), tok_off(SSD_INNER),
                  tok(d), mod, mod, mod, const(gn), const(sn), const(wo), const(n2),
                  const(wr_hi), const(wr_lo), const(br)],
        out_specs=[tok(d), tok(d), tok(LANES), tok(LANES), pl.BlockSpec((1, LANES), lambda b, j: (0, 0))],
        out_shape=[jax.ShapeDtypeStruct((bsz, L, d), F32), jax.ShapeDtypeStruct((bsz, L, d), BF16),
                   jax.ShapeDtypeStruct((bsz, L, LANES), jnp.int32), jax.ShapeDtypeStruct((bsz, L, LANES), F32),
                   jax.ShapeDtypeStruct((1, LANES), F32)],
        compiler_params=_params("arbitrary", "arbitrary"),
    )(o_f, o_b, g_all, y_f, y_b, z_all, x, g1, sh2, sc2, gn, sn, wo, n2, wr_hi, wr_lo, br)


def _pos_kernel(eidx_ref, pstart_ref, lst_ref, ust_ref, lp_ref, cd_ref, carry):
    @pl.when(pl.program_id(0) == 0)
    def _():
        carry[...] = pstart_ref[...]

    eidx = eidx_ref[...]
    lane = lax.broadcasted_iota(jnp.int32, (TB, LANES), 1)
    hits = [lane == eidx[:, kk:kk + 1] for kk in range(TOP_K)]
    sel = jnp.zeros((TB, LANES), F32)
    for hmask in hits:
        sel = jnp.where(hmask, 1.0, sel)
    cnt = jnp.sum(sel, axis=0, keepdims=True)
    run = jnp.floor((cnt + (RUN - 1.0)) * (1.0 / RUN)) * RUN
    rank = _dot(lst_ref[...], sel.astype(BF16))
    loff = _dot(jnp.broadcast_to(run, (8, LANES)).astype(BF16), ust_ref[...])[0:1]
    pos = loff + rank
    lp = jnp.zeros((TB, LANES), jnp.int32)
    for kk, hmask in enumerate(hits):
        lk = jnp.sum(jnp.where(hmask, pos, 0.0), axis=-1, keepdims=True)
        lp = jnp.where(lane == kk, lk.astype(jnp.int32), lp)
    lp_ref[...] = lp

    base = carry[...]
    eye = lax.broadcasted_iota(jnp.int32, (LANES, LANES), 0) == lax.broadcasted_iota(jnp.int32, (LANES, LANES), 1)
    col = lambda v: jnp.sum(jnp.where(eye, v, 0.0), axis=1, keepdims=True)
    loff_c, run_c, shift_c = col(loff), col(run), col(base - loff)
    row0 = (lax.broadcasted_iota(jnp.int32, (LANES, CD_LANES), 1) * RUN).astype(F32)
    inside = jnp.where(row0 >= loff_c, 1.0, 0.0) * jnp.where(row0 < loff_c + run_c, 1.0, 0.0)
    valid = jnp.sum(inside, axis=0, keepdims=True)
    dest = jnp.sum(inside * shift_c, axis=0, keepdims=True) + row0[0:1]
    cd_ref[0] = jnp.where(valid > 0.0, dest, -1.0).astype(jnp.int32)
    carry[...] = base + run


def _pos_call(eidx, pstart):
    T = eidx.shape[0]
    lst = jnp.asarray(np.tril(np.ones((TB, TB), np.float32), -1), BF16)
    ust = jnp.asarray(np.triu(np.ones((LANES, LANES), np.float32), 1), BF16)
    return pl.pallas_call(
        _pos_kernel,
        grid=(T // TB,),
        in_specs=[pl.BlockSpec((TB, LANES), lambda i: (i, 0)),
                  pl.BlockSpec((1, LANES), lambda i: (0, 0)),
                  pl.BlockSpec((TB, TB), lambda i: (0, 0)),
                  pl.BlockSpec((LANES, LANES), lambda i: (0, 0))],
        out_specs=[pl.BlockSpec((TB, LANES), lambda i: (i, 0)),
                   pl.BlockSpec((1, 1, CD_LANES), lambda i: (i, 0, 0))],
        out_shape=[jax.ShapeDtypeStruct((T, LANES), jnp.int32),
                   jax.ShapeDtypeStruct((T // TB, 1, CD_LANES), jnp.int32)],
        scratch_shapes=[pltpu.VMEM((1, LANES), F32)],
        compiler_params=_params("arbitrary"),
    )(eidx, pstart, lst, ust)


def _pair_matrix(lp, weights):
    j = lax.broadcasted_iota(jnp.int32, (TB, RL), 1)
    m = jnp.zeros((TB, RL), F32)
    for kk in range(TOP_K):
        w = 1.0 if weights is None else weights[:, kk:kk + 1]
        m = m + jnp.where(j == lp[:, kk:kk + 1], w, 0.0)
    return m


def _dispatch_kernel(fill_ref, cd_ref, h_ref, lp_ref, xg_ref, sorted_s, sem, fill_sem):
    sorted_s[...] = _dot_tn(_pair_matrix(lp_ref[...], None).astype(BF16), h_ref[...])

    @pl.when(pl.program_id(0) == 0)
    def _():
        def fill(n):
            blk = pl.multiple_of(fill_ref[n] * MOE_BLOCK, MOE_BLOCK)
            return pltpu.make_async_copy(sorted_s.at[pl.ds(0, MOE_BLOCK), :],
                                         xg_ref.at[pl.ds(blk, MOE_BLOCK), :], fill_sem)

        for n in range(fill_ref.shape[0]):
            pl.when(fill_ref[n] >= 0)(lambda n=n: fill(n).start())
        for n in range(fill_ref.shape[0]):
            pl.when(fill_ref[n] >= 0)(lambda n=n: fill(n).wait())

    spill = xg_ref.shape[0] - RL

    def body(c, carry):
        d = cd_ref[0, 0, c]
        d = pl.multiple_of(jnp.where(d < 0, spill + RUN * c, d), RUN)
        src = pl.multiple_of(RUN * c, RUN)
        pltpu.make_async_copy(sorted_s.at[pl.ds(src, RUN), :], xg_ref.at[pl.ds(d, RUN), :], sem).start()
        return carry

    lax.fori_loop(0, RL // RUN, body, 0, unroll=4)
    pltpu.make_async_copy(sorted_s, xg_ref.at[pl.ds(0, RL), :], sem).wait()


def _dispatch_call(fill_blocks, cd, h2, lp, P):
    T, d = h2.shape
    grid_spec = pltpu.PrefetchScalarGridSpec(
        num_scalar_prefetch=1,
        grid=(T // TB,),
        in_specs=[pl.BlockSpec((1, 1, CD_LANES), lambda i, fb: (i, 0, 0), memory_space=pltpu.SMEM),
                  pl.BlockSpec((TB, d), lambda i, fb: (i, 0)),
                  pl.BlockSpec((TB, LANES), lambda i, fb: (i, 0))],
        out_specs=pl.BlockSpec(memory_space=pl.ANY),
        scratch_shapes=[pltpu.VMEM((RL, d), F32), pltpu.SemaphoreType.DMA(()), pltpu.SemaphoreType.DMA(())],
    )
    return pl.pallas_call(
        _dispatch_kernel,
        grid_spec=grid_spec,
        out_shape=jax.ShapeDtypeStruct((P + RL, d), F32),
        compiler_params=_params("arbitrary"),
    )(fill_blocks, cd, h2, lp)


def _moe_kernel(be_ref, x_ref, wgu_ref, bgu_ref, wd_ref, bd_ref, y_ref, wgu_b, wd_b, act_s):
    i = pl.program_id(0)
    n_used = be_ref[pl.num_programs(0)]
    changed = jnp.logical_or(i == 0, be_ref[i] != be_ref[jnp.maximum(i - 1, 0)])

    @pl.when(changed)
    def _():
        wgu_b[...] = wgu_ref[0].astype(BF16)
        wd_b[...] = wd_ref[0].astype(BF16)

    @pl.when(i < n_used)
    def _():
        xb = x_ref[...].astype(BF16)
        cw = 256
        for c in range(D_FF // cw):
            lo, hi = c * cw, (c + 1) * cw
            gate = _dot(xb, wgu_b[:, lo:hi]) + bgu_ref[0, :, lo:hi]
            up = _dot(xb, wgu_b[:, D_FF + lo:D_FF + hi]) + bgu_ref[0, :, D_FF + lo:D_FF + hi]
            gate = jnp.minimum(gate, SWIGLU_LIMIT)
            up = jnp.clip(up, -SWIGLU_LIMIT, SWIGLU_LIMIT)
            act_s[:, lo:hi] = ((up + 1.0) * (gate * _sigmoid(SWIGLU_ALPHA * gate))).astype(BF16)
        y_ref[...] = _dot(act_s[...], wd_b[...]) + bd_ref[0]

    @pl.when(i >= n_used)
    def _():
        y_ref[...] = jnp.zeros_like(y_ref)


def _moe_call(blk_e, xg, wgu, bgu, wd, bd, nb):
    d = xg.shape[1]
    P = nb * MOE_BLOCK
    ne, _, f2 = wgu.shape
    grid_spec = pltpu.PrefetchScalarGridSpec(
        num_scalar_prefetch=1,
        grid=(nb,),
        in_specs=[pl.BlockSpec((MOE_BLOCK, d), lambda i, be: (jnp.minimum(i, be[nb] - 1), 0)),
                  pl.BlockSpec((1, d, f2), lambda i, be: (be[i], 0, 0)),
                  pl.BlockSpec((1, 1, f2), lambda i, be: (be[i], 0, 0)),
                  pl.BlockSpec((1, f2 // 2, d), lambda i, be: (be[i], 0, 0)),
                  pl.BlockSpec((1, 1, d), lambda i, be: (be[i], 0, 0))],
        out_specs=pl.BlockSpec((MOE_BLOCK, d), lambda i, be: (i, 0)),
        scratch_shapes=[pltpu.VMEM((d, f2), BF16), pltpu.VMEM((f2 // 2, d), BF16),
                        pltpu.VMEM((MOE_BLOCK, f2 // 2), BF16)],
    )
    return pl.pallas_call(
        _moe_kernel,
        grid_spec=grid_spec,
        out_shape=jax.ShapeDtypeStruct((P, d), F32),
        compiler_params=_params("arbitrary"),
    )(blk_e, xg, wgu, bgu, wd, bd)


def _combine_kernel(dcur_ref, dnext_ref, y_ref, lp_ref, x1_ref, gate_ref, g2_ref, fn_ref, o_ref, buf, sems):
    i = pl.program_id(0)
    n = pl.num_programs(0)
    slot = i & 1

    def gather(dref, s):
        def body(c, carry):
            d = pl.multiple_of(jnp.maximum(dref[0, 0, c], 0), RUN)
            dst = pl.multiple_of(RUN * c, RUN)
            pltpu.make_async_copy(y_ref.at[pl.ds(d, RUN), :], buf.at[s, pl.ds(dst, RUN), :], sems.at[s]).start()
            return carry

        lax.fori_loop(0, RL // RUN, body, 0, unroll=4)

    @pl.when(i == 0)
    def _():
        gather(dcur_ref, 0)

    @pl.when(i + 1 < n)
    def _():
        gather(dnext_ref, 1 - slot)

    pltpu.make_async_copy(y_ref.at[pl.ds(0, RL), :], buf.at[slot], sems.at[slot]).wait()

    g = _pair_matrix(lp_ref[...], gate_ref[...])
    g_hi = g.astype(BF16)
    g_lo = (g - g_hi.astype(F32)).astype(BF16)
    yb = buf[slot].astype(BF16)
    f = _dot(g_hi, yb) + _dot(g_lo, yb)
    xo = x1_ref[...] + g2_ref[0] * f
    ms = jnp.mean(xo * xo, axis=-1, keepdims=True)
    o_ref[...] = xo * lax.rsqrt(ms + EPS) * fn_ref[...]


def _combine_call(cd, y, lp, x1, gates, g2, fn):
    T, d = x1.shape
    n = T // TB
    per_batch = n // g2.shape[0]
    return pl.pallas_call(
        _combine_kernel,
        grid=(n,),
        in_specs=[pl.BlockSpec((1, 1, CD_LANES), lambda i: (i, 0, 0), memory_space=pltpu.SMEM),
                  pl.BlockSpec((1, 1, CD_LANES), lambda i: (jnp.minimum(i + 1, n - 1), 0, 0),
                               memory_space=pltpu.SMEM),
                  pl.BlockSpec(memory_space=pl.ANY),
                  pl.BlockSpec((TB, LANES), lambda i: (i, 0)),
                  pl.BlockSpec((TB, d), lambda i: (i, 0)),
                  pl.BlockSpec((TB, LANES), lambda i: (i, 0)),
                  pl.BlockSpec((1, 1, d), lambda i: (i // per_batch, 0, 0)),
                  pl.BlockSpec((1, d), lambda i: (0, 0))],
        out_specs=pl.BlockSpec((TB, d), lambda i: (i, 0)),
        out_shape=jax.ShapeDtypeStruct((T, d), F32),
        scratch_shapes=[pltpu.VMEM((2, RL, d), F32), pltpu.SemaphoreType.DMA((2,))],
        compiler_params=_params("arbitrary"),
    )(cd, cd, y, lp, x1, gates, g2, fn)


def _layer(x, c, ctx, c_ctx, w_mod, b_mod, norm1, w_in, gla_w_gk_up, gla_b_gk, gla_norm,
           ssd_conv_w, ssd_conv_b, ssd_dt_bias, ssd_A_log, ssd_D, ssd_norm, w_out,
           norm2, w_router, b_router, w_gate_up, b_gate_up, w_down, b_down, final_norm):
    bsz, L, d = x.shape
    lc = ctx.shape[1]
    assert lc == TB and L % TB == 0 and TB % GRID_W == 0

    cin = jnp.zeros((8, d), F32).at[:bsz].set(c).at[bsz].set(c_ctx)
    mod = _mod_call(cin, w_mod, b_mod.reshape(1, -1))[:bsz + 1]
    sh1, sc1, g1, sh2, sc2, g2 = [m.reshape(bsz + 1, 1, d) for m in jnp.split(mod, 6, axis=-1)]

    o = np.cumsum((0, GLA_QK, GLA_QK, GLA_V, GLA_V, GLA_RANK, SSD_INNER, SSD_CONV_DIM, SSD_HEADS))
    wq, wk, wv, wg, wlow, wz, wx, wdt = [w_in[:, int(a):int(b)] for a, b in zip(o[:-1], o[1:])]
    w_misc = jnp.concatenate([wlow, wdt, wdt, jnp.zeros((d, LANES - GLA_RANK - 2 * SSD_HEADS), F32)], axis=1)
    w_cat = jnp.concatenate([wq, wk, wv, wg, wz, wx, w_misc], axis=1).astype(BF16)
    wup = jnp.zeros((LANES, 2 * GLA_QK), F32).at[:GLA_RANK].set(
        jnp.concatenate([gla_w_gk_up[0], gla_w_gk_up[1]], axis=1)).astype(BF16)
    bup = jnp.concatenate([gla_b_gk[0], gla_b_gk[1]]).reshape(1, -1)
    dtb = jnp.zeros((1, LANES), F32).at[0, DT_F:DT_F + SSD_HEADS].set(ssd_dt_bias[0]) \
                                    .at[0, DT_B:DT_B + SSD_HEADS].set(ssd_dt_bias[1])
    q, k, v, g_all, z_all, xbc, ld, misc = _inproj_call(
        x, ctx, sh1, sc1, norm1.reshape(1, d), w_cat, wup, bup, dtb)

    xs, bc = _conv_call(xbc, ssd_conv_w.reshape(9, SSD_CONV_DIM), ssd_conv_b.reshape(1, -1))

    o_f, o_b = _gla_call(q, k, v, ld, L)

    a_neg = -jnp.exp(ssd_A_log.astype(F32))
    a_f = jnp.zeros((1, LANES), F32).at[0, DT_F:DT_F + SSD_HEADS].set(a_neg[0])
    a_b = jnp.zeros((1, LANES), F32).at[0, DT_B:DT_B + SSD_HEADS].set(a_neg[1])
    dvec = jnp.repeat(ssd_D, SSD_HEADDIM).reshape(1, SSD_INNER)
    y_f, y_b = _ssd_call(xs, bc, misc, a_f, a_b, dvec, L)

    wr = jnp.zeros((d, LANES), F32).at[:, :N_EXPERTS].set(w_router)
    wr_hi = wr.astype(BF16)
    wr_lo = (wr - wr_hi.astype(F32)).astype(BF16)
    br = jnp.full((1, LANES), NEG_BIG, F32).at[0, :N_EXPERTS].set(b_router)
    x1, h2, eidx, gates, counts = _outproj_call(
        o_f, o_b, g_all, y_f, y_b, z_all, x, g1[:bsz], sh2[:bsz], sc2[:bsz],
        jnp.tile(gla_norm, GLA_HEADS).reshape(1, -1), ssd_norm.reshape(1, -1), w_out.astype(BF16),
        norm2.reshape(1, d), wr_hi, wr_lo, br)

    T = bsz * L
    cnt = counts[0, :N_EXPERTS].astype(jnp.int32)
    padded = ((cnt + MOE_BLOCK - 1) // MOE_BLOCK) * MOE_BLOCK
    pend = jnp.cumsum(padded)
    pstart = pend - padded
    max_rows = T * TOP_K + (T // TB) * N_EXPERTS * (RUN - 1)
    n_blocks = -(-max_rows // MOE_BLOCK) + N_EXPERTS
    blk_start = jnp.arange(n_blocks, dtype=jnp.int32) * MOE_BLOCK
    blk_e = jnp.minimum(jnp.sum(pend[None, :] <= blk_start[:, None], axis=1), N_EXPERTS - 1).astype(jnp.int32)
    n_used = (pend[-1] // MOE_BLOCK).astype(jnp.int32)
    blk_info = jnp.concatenate([blk_e, n_used[None]])
    n_tail = n_blocks - (T * TOP_K) // MOE_BLOCK
    tail = n_used + jnp.arange(n_tail, dtype=jnp.int32)
    fill_blocks = jnp.concatenate([
        jnp.where(padded > 0, pend // MOE_BLOCK - 1, -1),
        jnp.where(tail < n_blocks, tail, -1),
        n_blocks + jnp.arange(RL // MOE_BLOCK, dtype=jnp.int32)]).astype(jnp.int32)
    pstart_row = jnp.zeros((1, LANES), F32).at[0, :N_EXPERTS].set(pstart.astype(F32))
    lp, cd = _pos_call(eidx.reshape(T, LANES), pstart_row)

    xg = _dispatch_call(fill_blocks, cd, h2.reshape(T, d), lp, n_blocks * MOE_BLOCK)
    y = _moe_call(blk_info, xg, w_gate_up, b_gate_up.reshape(N_EXPERTS, 1, -1), w_down,
                  b_down.reshape(N_EXPERTS, 1, -1), n_blocks)
    out = _combine_call(cd, y, lp, x1.reshape(T, d), gates.reshape(T, LANES), g2[:bsz], final_norm.reshape(1, d))
    return out.reshape(bsz, L, d)


def kernel(x, c, ctx, c_ctx, w_mod, b_mod, norm1, w_in, gla_w_gk_up, gla_b_gk, gla_norm, ssd_conv_w, ssd_conv_b, ssd_dt_bias, ssd_A_log, ssd_D, ssd_norm, w_out, norm2, w_router, b_router, w_gate_up, b_gate_up, w_down, b_down, final_norm):
    assert w_mod.shape[0] == 1, "single-layer kernel"
    return _layer(x, c, ctx, c_ctx, w_mod[0], b_mod[0], norm1[0], w_in[0], gla_w_gk_up[0], gla_b_gk[0],
                  gla_norm[0], ssd_conv_w[0], ssd_conv_b[0], ssd_dt_bias[0], ssd_A_log[0], ssd_D[0],
                  ssd_norm[0], w_out[0], norm2[0], w_router[0], b_router[0], w_gate_up[0], b_gate_up[0],
                  w_down[0], b_down[0], final_norm)
```

```python
import functools

import numpy as np
import jax
import jax.numpy as jnp
from jax import lax
from jax.experimental import pallas as pl
from jax.experimental.pallas import tpu as pltpu

F32 = jnp.float32
BF16 = jnp.bfloat16

EPS = 1e-6
GRID_W = 64
GLA_HEADS = 4
GLA_DK = 64
GLA_DV = 128
GLA_QK = GLA_HEADS * GLA_DK
GLA_V = GLA_HEADS * GLA_DV
GLA_RANK = 16
GLA_GATE_NORM = 16.0
SSD_HEADDIM = 64
SSD_INNER = 512
SSD_HEADS = 8
SSD_GROUPS = 2
SSD_HPG = 4
SSD_STATE = 128
SSD_CONV_DIM = 1024
N_EXPERTS = 32
TOP_K = 4
D_FF = 1024
SWIGLU_LIMIT = 7.0
SWIGLU_ALPHA = 1.702
MOE_BLOCK = 512

TB = 256
GLA_C = 64
SSD_C = 128
LANES = 128
EXP_CLAMP = 80.0
DT_F = 16
DT_B = 24
NEG_BIG = -1e30
RUN = 8
RL = 1280
CD_LANES = 256
SPILL_ROWS = -(-RL // MOE_BLOCK) * MOE_BLOCK
D_HALF = 512
VMEM_LIMIT = 56 * 1024 * 1024


def _dot(a, b):
    return jnp.dot(a, b, preferred_element_type=F32)


def _dot_nt(a, b):
    return lax.dot_general(a, b, (((1,), (1,)), ((), ())), preferred_element_type=F32)


def _dot_tn(a, b):
    return lax.dot_general(a, b, (((0,), (0,)), ((), ())), preferred_element_type=F32)


def _split3(a):
    hi = a.astype(BF16)
    r1 = a - hi.astype(F32)
    mid = r1.astype(BF16)
    lo = (r1 - mid.astype(F32)).astype(BF16)
    return hi, mid, lo


def _dot_exact_r(m, a):
    hi, mid, lo = _split3(a)
    return _dot(m, hi) + _dot(m, mid) + _dot(m, lo)


def _dot_hilo_l(a, m):
    hi = a.astype(BF16)
    lo = (a - hi.astype(F32)).astype(BF16)
    return _dot(hi, m) + _dot(lo, m)


def _sigmoid(x):
    return 1.0 / (1.0 + jnp.exp(-x))


def _softplus(x):
    return jnp.maximum(x, 0.0) + jnp.log1p(jnp.exp(-jnp.abs(x)))


def _params(*sem):
    return pltpu.CompilerParams(dimension_semantics=sem, vmem_limit_bytes=VMEM_LIMIT)


def _mod_kernel(c_ref, w_ref, b_ref, o_ref):
    c = c_ref[...]
    s = c * _sigmoid(c)
    s_hi = s.astype(BF16)
    s_lo = (s - s_hi.astype(F32)).astype(BF16)
    w = w_ref[...]
    w_hi = w.astype(BF16)
    w_lo = (w - w_hi.astype(F32)).astype(BF16)
    o_ref[...] = _dot(s_hi, w_hi) + _dot(s_lo, w_hi) + _dot(s_hi, w_lo) + b_ref[...]


def _mod_call(cin, w, b):
    rows, d = cin.shape
    n = w.shape[1]
    tn = 1536
    return pl.pallas_call(
        _mod_kernel,
        grid=(n // tn,),
        in_specs=[pl.BlockSpec((rows, d), lambda i: (0, 0)),
                  pl.BlockSpec((d, tn), lambda i: (0, i)),
                  pl.BlockSpec((1, tn), lambda i: (0, i))],
        out_specs=pl.BlockSpec((rows, tn), lambda i: (0, i)),
        out_shape=jax.ShapeDtypeStruct((rows, n), F32),
        compiler_params=_params("arbitrary"),
    )(cin, w, b)


_C_Q, _C_K, _C_V, _C_G, _C_Z, _C_X, _C_M, _C_END = 0, 256, 512, 1024, 1536, 2048, 3072, 3200


def _inproj_kernel(x_ref, ctx_ref, sh_ref, sc_ref, n1_ref, w_ref, wup_ref, bup_ref, dtb_ref,
                   q_ref, k_ref, v_ref, g_ref, z_ref, xbc_ref, ld_ref, misc_ref, h_scr):
    j = pl.program_id(1)

    def normmod(xv):
        ms = jnp.mean(xv * xv, axis=-1, keepdims=True)
        y = xv * lax.rsqrt(ms + EPS) * n1_ref[...]
        return (y * (1.0 + sc_ref[0]) + sh_ref[0]).astype(BF16)

    @pl.when(j == 0)
    def _():
        h_scr[...] = normmod(ctx_ref[0])

    @pl.when(j > 0)
    def _():
        h_scr[...] = normmod(x_ref[0])

    h = h_scr[...]

    def mm(lo, hi):
        return _dot(h, w_ref[:, lo:hi])

    q_ref[0] = (mm(_C_Q, _C_K) * (GLA_DK ** -0.5)).astype(BF16)
    k_ref[0] = mm(_C_K, _C_V).astype(BF16)
    v_ref[0] = mm(_C_V, _C_G).astype(BF16)
    g_ref[0] = mm(_C_G, _C_Z)
    z_ref[0] = mm(_C_Z, _C_X)
    xbc_ref[0] = mm(_C_X, _C_M)
    m = mm(_C_M, _C_END)
    zz = _dot(m.astype(BF16), wup_ref[...]) + bup_ref[...]
    ld_ref[0] = -_softplus(-zz) * (1.0 / GLA_GATE_NORM)
    misc_ref[0] = _softplus(m + dtb_ref[...])


def _inproj_call(x, ctx, sh1, sc1, n1, w_cat, wup, bup, dtb):
    bsz, L, d = x.shape
    lc = ctx.shape[1]
    ls = lc + L
    nj = ls // TB
    tok = lambda n: pl.BlockSpec((1, TB, n), lambda b, j: (b, j, 0))
    const = lambda a: pl.BlockSpec(a.shape, lambda b, j: (0,) * a.ndim)
    modspec = pl.BlockSpec((1, 1, d), lambda b, j: (jnp.where(j == 0, bsz, b), 0, 0))
    outs = [(GLA_QK, BF16), (GLA_QK, BF16), (GLA_V, BF16), (GLA_V, F32), (SSD_INNER, F32),
            (SSD_CONV_DIM, F32), (2 * GLA_QK, F32), (LANES, F32)]
    return pl.pallas_call(
        _inproj_kernel,
        grid=(bsz, nj),
        in_specs=[pl.BlockSpec((1, TB, d), lambda b, j: (b, jnp.maximum(j - 1, 0), 0)),
                  pl.BlockSpec((1, TB, d), lambda b, j: (b, 0, 0)),
                  modspec, modspec, const(n1), const(w_cat), const(wup), const(bup), const(dtb)],
        out_specs=[tok(n) for n, _ in outs],
        out_shape=[jax.ShapeDtypeStruct((bsz, ls, n), dt) for n, dt in outs],
        scratch_shapes=[pltpu.VMEM((TB, d), BF16)],
        compiler_params=_params("arbitrary", "arbitrary"),
    )(x, ctx, sh1, sc1, n1, w_cat, wup, bup, dtb)


_EXT_PAD = 8
_EXT_BASE = _EXT_PAD + GRID_W
_EXT_ROWS = 2 * _EXT_PAD + 2 * GRID_W + TB


def _conv_kernel(prev_ref, cur_ref, next_ref, w_ref, b_ref, xs_ref, bc_ref, ext):
    j = pl.program_id(1)
    nj = pl.num_programs(1)
    is_ctx = j == 0
    zpad = jnp.zeros((_EXT_PAD, SSD_CONV_DIM), F32)
    ext[0:_EXT_PAD, :] = zpad
    ext[_EXT_ROWS - _EXT_PAD:_EXT_ROWS, :] = zpad
    ext[_EXT_PAD:_EXT_BASE, :] = jnp.where(j >= 2, prev_ref[0], 0.0)
    ext[_EXT_BASE:_EXT_BASE + TB, :] = cur_ref[0]
    ext[_EXT_BASE + TB:_EXT_BASE + TB + GRID_W, :] = jnp.where(
        jnp.logical_and(j >= 1, j <= nj - 2), next_ref[0], 0.0)

    t = lax.broadcasted_iota(jnp.int32, (TB, LANES), 0)
    col = t & (GRID_W - 1)
    ok_l = jnp.where(is_ctx, t, col) >= 1
    ok_r = jnp.where(is_ctx, t - (TB - GRID_W), col) <= GRID_W - 2
    lat = jnp.where(is_ctx, 0.0, 1.0)

    for c in range(SSD_CONV_DIM // LANES):
        lo, hi = c * LANES, (c + 1) * LANES
        acc = jnp.zeros((TB, LANES), F32)
        for dr in (-1, 0, 1):
            for dc in (-1, 0, 1):
                start = _EXT_BASE + GRID_W * dr + dc
                tap = ext[start:start + TB, lo:hi]
                wi = 3 * (dr + 1) + (dc + 1)
                wv = w_ref[wi:wi + 1, lo:hi]
                if dr != 0:
                    wv = wv * lat
                if dc == -1:
                    tap = jnp.where(ok_l, tap, 0.0)
                elif dc == 1:
                    tap = jnp.where(ok_r, tap, 0.0)
                acc = acc + tap * wv
        y = acc + b_ref[:, lo:hi]
        y = y * _sigmoid(y)
        if c < SSD_INNER // LANES:
            xs_ref[0, :, lo:hi] = y
        else:
            bc_ref[0, :, lo - SSD_INNER:hi - SSD_INNER] = y.astype(BF16)


def _conv_call(xbc, w9, bias):
    bsz, ls, ch = xbc.shape
    nj = ls // TB
    rpb = TB // GRID_W
    nrow = ls // GRID_W
    return pl.pallas_call(
        _conv_kernel,
        grid=(bsz, nj),
        in_specs=[pl.BlockSpec((1, GRID_W, ch), lambda b, j: (b, jnp.maximum(rpb * j - 1, 0), 0)),
                  pl.BlockSpec((1, TB, ch), lambda b, j: (b, j, 0)),
                  pl.BlockSpec((1, GRID_W, ch), lambda b, j: (b, jnp.minimum(rpb * j + rpb, nrow - 1), 0)),
                  pl.BlockSpec((9, ch), lambda b, j: (0, 0)),
                  pl.BlockSpec((1, ch), lambda b, j: (0, 0))],
        out_specs=[pl.BlockSpec((1, TB, SSD_INNER), lambda b, j: (b, j, 0)),
                   pl.BlockSpec((1, TB, ch - SSD_INNER), lambda b, j: (b, j, 0))],
        out_shape=[jax.ShapeDtypeStruct((bsz, ls, SSD_INNER), F32),
                   jax.ShapeDtypeStruct((bsz, ls, ch - SSD_INNER), BF16)],
        scratch_shapes=[pltpu.VMEM((_EXT_ROWS, ch), F32)],
        compiler_params=_params("arbitrary", "arbitrary"),
    )(xbc, xbc, xbc, w9, bias)


def _fwd_blk(s):
    return s


def _bwd_blk(s, ns):
    return jnp.where(s == 0, 0, ns - s)


def _gla_dir(q, k, v, la, st_ref, tri_m, fwd):
    C = GLA_C
    b = _dot_exact_r(tri_m, la)
    bt = b[C - 1:C, :] if fwd else b[0:1, :]
    r = 0.5 * bt
    qf = q.astype(F32)
    kf = k.astype(F32)
    qt = (qf * jnp.exp(jnp.minimum(b - r, EXP_CLAMP))).astype(BF16)
    kt = (kf * jnp.exp(jnp.minimum(r - b, EXP_CLAMP))).astype(BF16)
    head_k = lax.broadcasted_iota(jnp.int32, (C, GLA_QK), 1) >> 6
    kst = jnp.concatenate([jnp.where(head_k == h, kt, jnp.zeros_like(kt)) for h in range(GLA_HEADS)], axis=0)
    sc = _dot_nt(qt, kst)
    ii = lax.broadcasted_iota(jnp.int32, (C, GLA_HEADS * C), 0)
    jj = lax.broadcasted_iota(jnp.int32, (C, GLA_HEADS * C), 1) & (C - 1)
    causal = (jj <= ii) if fwd else (jj >= ii)
    p = jnp.where(causal, sc, 0.0).astype(BF16)
    head_v = lax.broadcasted_iota(jnp.int32, (C, GLA_V), 1) >> 7
    vst = jnp.concatenate([jnp.where(head_v == h, v, jnp.zeros_like(v)) for h in range(GLA_HEADS)], axis=0)
    o = _dot(p, vst)
    st = st_ref[...]
    qd = (qf * jnp.exp(b)).astype(BF16)
    o = o + _dot_nt(qd, st.astype(BF16))
    kd = (kf * jnp.exp(bt - b)).astype(BF16)
    u = _dot_tn(v, kd)
    rv = lax.broadcasted_iota(jnp.int32, (GLA_V, GLA_QK), 0) >> 7
    cv = lax.broadcasted_iota(jnp.int32, (GLA_V, GLA_QK), 1) >> 6
    st_ref[...] = st * jnp.exp(bt) + jnp.where(rv == cv, u, 0.0)
    return o


def _gla_kernel(qf_ref, kf_ref, vf_ref, lf_ref, qb_ref, kb_ref, vb_ref, lb_ref, trif_ref, trib_ref,
                of_ref, ob_ref, stf, stb):
    s = pl.program_id(1)

    @pl.when(s == 0)
    def _():
        stf[...] = jnp.zeros_like(stf)
        stb[...] = jnp.zeros_like(stb)

    nsub = TB // GLA_C

    for i in range(nsub):
        sf = pl.ds(i * GLA_C, GLA_C)
        sb = pl.ds((nsub - 1 - i) * GLA_C, GLA_C)
        of_ref[0, sf, :] = _gla_dir(qf_ref[0, sf, :], kf_ref[0, sf, :], vf_ref[0, sf, :], lf_ref[0, sf, :],
                                    stf, trif_ref[...], True)
        ob_ref[0, sb, :] = _gla_dir(qb_ref[0, sb, :], kb_ref[0, sb, :], vb_ref[0, sb, :], lb_ref[0, sb, :],
                                    stb, trib_ref[...], False)


def _gla_call(q, k, v, ld, L):
    bsz, ls, _ = q.shape
    ns = ls // TB
    nx = L // TB
    trif = jnp.asarray(np.tril(np.ones((GLA_C, GLA_C), np.float32)), BF16)
    trib = jnp.asarray(np.triu(np.ones((GLA_C, GLA_C), np.float32)), BF16)
    f = lambda n, lane=0: pl.BlockSpec((1, TB, n), lambda b, s: (b, _fwd_blk(s), lane))
    r = lambda n, lane=0: pl.BlockSpec((1, TB, n), lambda b, s: (b, _bwd_blk(s, ns), lane))
    tri = pl.BlockSpec((GLA_C, GLA_C), lambda b, s: (0, 0))
    return pl.pallas_call(
        _gla_kernel,
        grid=(bsz, ns),
        in_specs=[f(GLA_QK), f(GLA_QK), f(GLA_V), f(GLA_QK, 0),
                  r(GLA_QK), r(GLA_QK), r(GLA_V), r(GLA_QK, 1), tri, tri],
        out_specs=[pl.BlockSpec((1, TB, GLA_V), lambda b, s: (b, jnp.maximum(s - 1, 0), 0)),
                   pl.BlockSpec((1, TB, GLA_V), lambda b, s: (b, jnp.where(s == 0, nx - 1, nx - s), 0))],
        out_shape=[jax.ShapeDtypeStruct((bsz, L, GLA_V), F32)] * 2,
        scratch_shapes=[pltpu.VMEM((GLA_V, GLA_QK), F32)] * 2,
        compiler_params=_params("arbitrary", "arbitrary"),
    )(q, k, v, ld, q, k, v, ld, trif, trib)


def _ssd_dir(xs, bc, dtm, avec, dvec, st_ref, tri_m, e_m, base, fwd):
    C = SSD_C
    dt_exp = _dot_hilo_l(dtm, e_m)
    a = dtm * avec
    acum = _dot_exact_r(tri_m, a)
    acum_exp = _dot_hilo_l(acum, e_m)
    acum_t = acum.T
    xdt = xs * dt_exp
    xdt_b = xdt.astype(BF16)
    ii = lax.broadcasted_iota(jnp.int32, (C, C), 0)
    jj = lax.broadcasted_iota(jnp.int32, (C, C), 1)
    tri = (jj <= ii) if fwd else (jj >= ii)
    al_exp = acum_exp[C - 1:C, :] if fwd else acum_exp[0:1, :]
    head2 = lax.broadcasted_iota(jnp.int32, (C, 256), 1) >> 6
    ys = []
    for g in range(SSD_GROUPS):
        gl, gh = 256 * g, 256 * (g + 1)
        bg = bc[:, 128 * g:128 * (g + 1)]
        cg = bc[:, 256 + 128 * g:256 + 128 * (g + 1)]
        cb = _dot_nt(cg, bg)
        ms = []
        for rr in range(SSD_HPG):
            ln = base + SSD_HPG * g + rr
            diff = acum[:, ln:ln + 1] - acum_t[ln:ln + 1, :]
            seg = jnp.where(tri, jnp.exp(jnp.minimum(diff, 0.0)), 0.0)
            ms.append((cb * seg).astype(BF16))
        res = _dot(jnp.concatenate(ms, axis=0), xdt_b[:, gl:gh])
        yg = jnp.zeros((C, 256), F32)
        for rr in range(SSD_HPG):
            yg = yg + jnp.where(head2 == rr, res[rr * C:(rr + 1) * C, :], 0.0)
        sg = st_ref[g]
        yoff = _dot(cg, sg.astype(BF16)) * jnp.exp(acum_exp[:, gl:gh])
        wst = jnp.exp(al_exp[:, gl:gh] - acum_exp[:, gl:gh])
        ug = _dot_tn(bg, (xdt[:, gl:gh] * wst).astype(BF16))
        st_ref[g] = sg * jnp.exp(al_exp[:, gl:gh]) + ug
        ys.append(yg + yoff)
    y = jnp.concatenate(ys, axis=1)
    if dvec is not None:
        y = y + dvec * xs
    return y


def _ssd_kernel(xf_ref, bcf_ref, mf_ref, xb_ref, bcb_ref, mb_ref, af_ref, ab_ref, d_ref,
                trif_ref, trib_ref, ef_ref, eb_ref, yf_ref, yb_ref, stf, stb):
    s = pl.program_id(1)

    @pl.when(s == 0)
    def _():
        stf[...] = jnp.zeros_like(stf)
        stb[...] = jnp.zeros_like(stb)

    nsub = TB // SSD_C

    for i in range(nsub):
        sf = pl.ds(i * SSD_C, SSD_C)
        sb = pl.ds((nsub - 1 - i) * SSD_C, SSD_C)
        yf_ref[0, sf, :] = _ssd_dir(xf_ref[0, sf, :], bcf_ref[0, sf, :], mf_ref[0, sf, :], af_ref[...], d_ref[...],
                                    stf, trif_ref[...], ef_ref[...], DT_F, True)
        yb_ref[0, sb, :] = _ssd_dir(xb_ref[0, sb, :], bcb_ref[0, sb, :], mb_ref[0, sb, :], ab_ref[...], None,
                                    stb, trib_ref[...], eb_ref[...], DT_B, False)


def _expand_matrix(base):
    e = np.zeros((LANES, SSD_INNER), np.float32)
    for h in range(SSD_HEADS):
        e[base + h, SSD_HEADDIM * h:SSD_HEADDIM * (h + 1)] = 1.0
    return jnp.asarray(e, BF16)


def _ssd_call(xs, bc, misc, a_f, a_b, dvec, L):
    bsz, ls, _ = xs.shape
    ns = ls // TB
    nx = L // TB
    trif = jnp.asarray(np.tril(np.ones((SSD_C, SSD_C), np.float32)), BF16)
    trib = jnp.asarray(np.triu(np.ones((SSD_C, SSD_C), np.float32)), BF16)
    ef, eb = _expand_matrix(DT_F), _expand_matrix(DT_B)
    f = lambda n: pl.BlockSpec((1, TB, n), lambda b, s: (b, _fwd_blk(s), 0))
    r = lambda n: pl.BlockSpec((1, TB, n), lambda b, s: (b, _bwd_blk(s, ns), 0))
    const = lambda a: pl.BlockSpec(a.shape, lambda b, s: (0,) * a.ndim)
    return pl.pallas_call(
        _ssd_kernel,
        grid=(bsz, ns),
        in_specs=[f(SSD_INNER), f(512), f(LANES), r(SSD_INNER), r(512), r(LANES),
                  const(a_f), const(a_b), const(dvec), const(trif), const(trib), const(ef), const(eb)],
        out_specs=[pl.BlockSpec((1, TB, SSD_INNER), lambda b, s: (b, jnp.maximum(s - 1, 0), 0)),
                   pl.BlockSpec((1, TB, SSD_INNER), lambda b, s: (b, jnp.where(s == 0, nx - 1, nx - s), 0))],
        out_shape=[jax.ShapeDtypeStruct((bsz, L, SSD_INNER), F32)] * 2,
        scratch_shapes=[pltpu.VMEM((SSD_GROUPS, SSD_STATE, SSD_HPG * SSD_HEADDIM), F32)] * 2,
        compiler_params=_params("arbitrary", "arbitrary"),
    )(xs, bc, misc, xs, bc, misc, a_f, a_b, dvec, trif, trib, ef, eb)


def _outproj_kernel(of_ref, ob_ref, g_ref, yf_ref, yb_ref, z_ref, x_ref, g1_ref, sh2_ref, sc2_ref,
                    gn_ref, sn_ref, wo_ref, n2_ref, wrh_ref, wrl_ref, br_ref,
                    x1_ref, h2_ref, eidx_ref, gate_ref, cnt_ref):
    first = jnp.logical_and(pl.program_id(0) == 0, pl.program_id(1) == 0)

    @pl.when(first)
    def _():
        cnt_ref[...] = jnp.zeros_like(cnt_ref)

    o = of_ref[0] + ob_ref[0]
    gg = g_ref[0]
    parts = []
    for h in range(GLA_HEADS):
        lo, hi = GLA_DV * h, GLA_DV * (h + 1)
        oh = o[:, lo:hi]
        ms = jnp.mean(oh * oh, axis=-1, keepdims=True)
        gh = gg[:, lo:hi]
        parts.append((oh * lax.rsqrt(ms + EPS) * gn_ref[:, lo:hi] * (gh * _sigmoid(gh))).astype(BF16))
    zz = z_ref[0]
    u = (yf_ref[0] + yb_ref[0]) * (zz * _sigmoid(zz))
    gw = SSD_INNER // SSD_GROUPS
    for g in range(SSD_GROUPS):
        lo, hi = gw * g, gw * (g + 1)
        ug = u[:, lo:hi]
        ms = jnp.mean(ug * ug, axis=-1, keepdims=True)
        parts.append((ug * lax.rsqrt(ms + EPS) * sn_ref[:, lo:hi]).astype(BF16))
    mix = jnp.concatenate(parts, axis=1)
    x1 = x_ref[0] + g1_ref[0] * _dot(mix, wo_ref[...])
    x1_ref[0] = x1
    ms = jnp.mean(x1 * x1, axis=-1, keepdims=True)
    h2 = (x1 * lax.rsqrt(ms + EPS) * n2_ref[...]) * (1.0 + sc2_ref[0]) + sh2_ref[0]
    h_hi = h2.astype(BF16)
    h2_ref[0] = h_hi
    h_lo = (h2 - h_hi.astype(F32)).astype(BF16)
    logits = (_dot(h_hi, wrh_ref[...]) + _dot(h_lo, wrh_ref[...]) + _dot(h_hi, wrl_ref[...])) + br_ref[...]
    lane = lax.broadcasted_iota(jnp.int32, (TB, LANES), 1).astype(F32)
    work = logits
    eidx = jnp.full((TB, LANES), -1.0, F32)
    gates = jnp.zeros((TB, LANES), F32)
    sel = jnp.zeros((TB, LANES), F32)
    m0 = None
    den = jnp.zeros((TB, 1), F32)
    for kk in range(TOP_K):
        m = jnp.max(work, axis=-1, keepdims=True)
        idx = jnp.min(jnp.where(work == m, lane, float(LANES)), axis=-1, keepdims=True)
        hit = lane == idx
        if m0 is None:
            m0 = m
        e = jnp.exp(m - m0)
        den = den + e
        eidx = jnp.where(lane == float(kk), idx, eidx)
        gates = jnp.where(lane == float(kk), e, gates)
        sel = jnp.where(hit, 1.0, sel)
        work = jnp.where(hit, NEG_BIG, work)
    eidx_ref[0] = eidx.astype(jnp.int32)
    gate_ref[0] = gates / den
    cnt = jnp.sum(sel, axis=0, keepdims=True)
    cnt_ref[...] += jnp.floor((cnt + (RUN - 1.0)) * (1.0 / RUN)) * RUN


def _outproj_call(o_f, o_b, g_all, y_f, y_b, z_all, x, g1, sh2, sc2, gn, sn, wo, n2, wr_hi, wr_lo, br):
    bsz, L, d = x.shape
    nj = L // TB
    tok = lambda n: pl.BlockSpec((1, TB, n), lambda b, j: (b, j, 0))
    tok_off = lambda n: pl.BlockSpec((1, TB, n), lambda b, j: (b, j + 1, 0))
    const = lambda a: pl.BlockSpec(a.shape, lambda b, j: (0,) * a.ndim)
    mod = pl.BlockSpec((1, 1, d), lambda b, j: (b, 0, 0))
    return pl.pallas_call(
        _outproj_kernel,
        grid=(bsz, nj),
        in_specs=[tok(GLA_V), tok(GLA_V), tok_off(GLA_V), tok(SSD_INNER), tok(SSD_INNER), tok_off(SSD_INNER),
                  tok(d), mod, mod, mod, const(gn), const(sn), const(wo), const(n2),
                  const(wr_hi), const(wr_lo), const(br)],
        out_specs=[tok(d), tok(d), tok(LANES), tok(LANES), pl.BlockSpec((1, LANES), lambda b, j: (0, 0))],
        out_shape=[jax.ShapeDtypeStruct((bsz, L, d), F32), jax.ShapeDtypeStruct((bsz, L, d), BF16),
                   jax.ShapeDtypeStruct((bsz, L, LANES), jnp.int32), jax.ShapeDtypeStruct((bsz, L, LANES), F32),
                   jax.ShapeDtypeStruct((1, LANES), F32)],
        compiler_params=_params("arbitrary", "arbitrary"),
    )(o_f, o_b, g_all, y_f, y_b, z_all, x, g1, sh2, sc2, gn, sn, wo, n2, wr_hi, wr_lo, br)


def _pos_kernel(eidx_ref, pstart_ref, lst_ref, ust_ref, lp_ref, cd_ref, carry):
    @pl.when(pl.program_id(0) == 0)
    def _():
        carry[...] = pstart_ref[...]

    eidx = eidx_ref[...]
    lane = lax.broadcasted_iota(jnp.int32, (TB, LANES), 1)
    hits = [lane == eidx[:, kk:kk + 1] for kk in range(TOP_K)]
    sel = jnp.zeros((TB, LANES), F32)
    for hmask in hits:
        sel = jnp.where(hmask, 1.0, sel)
    cnt = jnp.sum(sel, axis=0, keepdims=True)
    run = jnp.floor((cnt + (RUN - 1.0)) * (1.0 / RUN)) * RUN
    rank = _dot(lst_ref[...], sel.astype(BF16))
    loff = _dot(jnp.broadcast_to(run, (8, LANES)).astype(BF16), ust_ref[...])[0:1]
    pos = loff + rank
    lp = jnp.zeros((TB, LANES), jnp.int32)
    for kk, hmask in enumerate(hits):
        lk = jnp.sum(jnp.where(hmask, pos, 0.0), axis=-1, keepdims=True)
        lp = jnp.where(lane == kk, lk.astype(jnp.int32), lp)
    lp_ref[...] = lp

    base = carry[...]
    eye = lax.broadcasted_iota(jnp.int32, (LANES, LANES), 0) == lax.broadcasted_iota(jnp.int32, (LANES, LANES), 1)
    col = lambda v: jnp.sum(jnp.where(eye, v, 0.0), axis=1, keepdims=True)
    loff_c, run_c, shift_c = col(loff), col(run), col(base - loff)
    row0 = (lax.broadcasted_iota(jnp.int32, (LANES, CD_LANES), 1) * RUN).astype(F32)
    inside = jnp.where(row0 >= loff_c, 1.0, 0.0) * jnp.where(row0 < loff_c + run_c, 1.0, 0.0)
    valid = jnp.sum(inside, axis=0, keepdims=True)
    dest = jnp.sum(inside * shift_c, axis=0, keepdims=True) + row0[0:1]
    cd_ref[0] = jnp.where(valid > 0.0, dest, -1.0).astype(jnp.int32)
    carry[...] = base + run


def _pos_call(eidx, pstart):
    T = eidx.shape[0]
    lst = jnp.asarray(np.tril(np.ones((TB, TB), np.float32), -1), BF16)
    ust = jnp.asarray(np.triu(np.ones((LANES, LANES), np.float32), 1), BF16)
    return pl.pallas_call(
        _pos_kernel,
        grid=(T // TB,),
        in_specs=[pl.BlockSpec((TB, LANES), lambda i: (i, 0)),
                  pl.BlockSpec((1, LANES), lambda i: (0, 0)),
                  pl.BlockSpec((TB, TB), lambda i: (0, 0)),
                  pl.BlockSpec((LANES, LANES), lambda i: (0, 0))],
        out_specs=[pl.BlockSpec((TB, LANES), lambda i: (i, 0)),
                   pl.BlockSpec((1, 1, CD_LANES), lambda i: (i, 0, 0))],
        out_shape=[jax.ShapeDtypeStruct((T, LANES), jnp.int32),
                   jax.ShapeDtypeStruct((T // TB, 1, CD_LANES), jnp.int32)],
        scratch_shapes=[pltpu.VMEM((1, LANES), F32)],
        compiler_params=_params("arbitrary"),
    )(eidx, pstart, lst, ust)


def _pair_matrix(lp, weights):
    j = lax.broadcasted_iota(jnp.int32, (TB, RL), 1)
    m = jnp.zeros((TB, RL), F32)
    for kk in range(TOP_K):
        w = 1.0 if weights is None else weights[:, kk:kk + 1]
        m = m + jnp.where(j == lp[:, kk:kk + 1], w, 0.0)
    return m


def _pack_bf16_pairs(v):
    bits = pltpu.bitcast(v.astype(BF16).astype(F32), jnp.uint32)
    return bits[:, D_HALF:] | (bits[:, :D_HALF] >> 16)


def _unpack_bf16_pairs(w):
    lo = pltpu.bitcast(w << 16, F32).astype(BF16)
    hi = pltpu.bitcast(w & jnp.uint32(0xFFFF0000), F32).astype(BF16)
    return lo, hi


def _dispatch_kernel(fill_ref, cd_ref, h_ref, lp_ref, xg_ref, sorted_s, sem, fill_sem):
    sorted_s[...] = _pack_bf16_pairs(_dot_tn(_pair_matrix(lp_ref[...], None).astype(BF16), h_ref[...]))

    @pl.when(pl.program_id(0) == 0)
    def _():
        def fill(n):
            blk = pl.multiple_of(fill_ref[n] * MOE_BLOCK, MOE_BLOCK)
            return pltpu.make_async_copy(sorted_s.at[pl.ds(0, MOE_BLOCK), :],
                                         xg_ref.at[pl.ds(blk, MOE_BLOCK), :], fill_sem)

        for n in range(fill_ref.shape[0]):
            pl.when(fill_ref[n] >= 0)(lambda n=n: fill(n).start())
        for n in range(fill_ref.shape[0]):
            pl.when(fill_ref[n] >= 0)(lambda n=n: fill(n).wait())

    spill = xg_ref.shape[0] - SPILL_ROWS

    def body(c, carry):
        d = cd_ref[0, 0, c]
        d = pl.multiple_of(jnp.where(d < 0, spill + RUN * c, d), RUN)
        src = pl.multiple_of(RUN * c, RUN)
        pltpu.make_async_copy(sorted_s.at[pl.ds(src, RUN), :], xg_ref.at[pl.ds(d, RUN), :], sem).start()
        return carry

    lax.fori_loop(0, RL // RUN, body, 0, unroll=4)
    pltpu.make_async_copy(sorted_s, xg_ref.at[pl.ds(0, RL), :], sem).wait()


def _dispatch_call(fill_blocks, cd, h2, lp, P):
    T, d = h2.shape
    grid_spec = pltpu.PrefetchScalarGridSpec(
        num_scalar_prefetch=1,
        grid=(T // TB,),
        in_specs=[pl.BlockSpec((1, 1, CD_LANES), lambda i, fb: (i, 0, 0), memory_space=pltpu.SMEM),
                  pl.BlockSpec((TB, d), lambda i, fb: (i, 0)),
                  pl.BlockSpec((TB, LANES), lambda i, fb: (i, 0))],
        out_specs=pl.BlockSpec(memory_space=pl.ANY),
        scratch_shapes=[pltpu.VMEM((RL, D_HALF), jnp.uint32), pltpu.SemaphoreType.DMA(()),
                        pltpu.SemaphoreType.DMA(())],
    )
    return pl.pallas_call(
        _dispatch_kernel,
        grid_spec=grid_spec,
        out_shape=jax.ShapeDtypeStruct((P + SPILL_ROWS, D_HALF), jnp.uint32),
        compiler_params=_params("arbitrary"),
    )(fill_blocks, cd, h2, lp)


def _moe_kernel(be_ref, x_ref, wgu_ref, bgu_ref, wd_ref, bd_ref, y_ref, wgu_b, wd_b, act_s):
    i = pl.program_id(0)
    n_used = be_ref[pl.num_programs(0)]
    changed = jnp.logical_or(i == 0, be_ref[i] != be_ref[jnp.maximum(i - 1, 0)])

    @pl.when(changed)
    def _():
        wgu_b[...] = wgu_ref[0].astype(BF16)
        wd_b[...] = wd_ref[0].astype(BF16)

    @pl.when(i < n_used)
    def _():
        xb = jnp.concatenate(_unpack_bf16_pairs(x_ref[...]), axis=1)
        cw = 256
        for c in range(D_FF // cw):
            lo, hi = c * cw, (c + 1) * cw
            gate = _dot(xb, wgu_b[:, lo:hi]) + bgu_ref[0, :, lo:hi]
            up = _dot(xb, wgu_b[:, D_FF + lo:D_FF + hi]) + bgu_ref[0, :, D_FF + lo:D_FF + hi]
            gate = jnp.minimum(gate, SWIGLU_LIMIT)
            up = jnp.clip(up, -SWIGLU_LIMIT, SWIGLU_LIMIT)
            act_s[:, lo:hi] = ((up + 1.0) * (gate * _sigmoid(SWIGLU_ALPHA * gate))).astype(BF16)
        y_ref[...] = _pack_bf16_pairs(_dot(act_s[...], wd_b[...]) + bd_ref[0])

    @pl.when(i >= n_used)
    def _():
        y_ref[...] = jnp.zeros_like(y_ref)


def _moe_call(blk_e, xg, wgu, bgu, wd, bd, nb):
    P = nb * MOE_BLOCK
    ne, d, f2 = wgu.shape
    grid_spec = pltpu.PrefetchScalarGridSpec(
        num_scalar_prefetch=1,
        grid=(nb,),
        in_specs=[pl.BlockSpec((MOE_BLOCK, D_HALF), lambda i, be: (jnp.minimum(i, be[nb] - 1), 0)),
                  pl.BlockSpec((1, d, f2), lambda i, be: (be[i], 0, 0)),
                  pl.BlockSpec((1, 1, f2), lambda i, be: (be[i], 0, 0)),
                  pl.BlockSpec((1, f2 // 2, d), lambda i, be: (be[i], 0, 0)),
                  pl.BlockSpec((1, 1, d), lambda i, be: (be[i], 0, 0))],
        out_specs=pl.BlockSpec((MOE_BLOCK, D_HALF), lambda i, be: (i, 0)),
        scratch_shapes=[pltpu.VMEM((d, f2), BF16), pltpu.VMEM((f2 // 2, d), BF16),
                        pltpu.VMEM((MOE_BLOCK, f2 // 2), BF16)],
    )
    return pl.pallas_call(
        _moe_kernel,
        grid_spec=grid_spec,
        out_shape=jax.ShapeDtypeStruct((P, D_HALF), jnp.uint32),
        compiler_params=_params("arbitrary"),
    )(blk_e, xg, wgu, bgu, wd, bd)


def _combine_kernel(dcur_ref, dnext_ref, y_ref, lp_ref, x1_ref, gate_ref, g2_ref, fn_ref, o_ref, buf, sems):
    i = pl.program_id(0)
    n = pl.num_programs(0)
    slot = i & 1

    def gather(dref, s):
        def body(c, carry):
            d = pl.multiple_of(jnp.maximum(dref[0, 0, c], 0), RUN)
            dst = pl.multiple_of(RUN * c, RUN)
            pltpu.make_async_copy(y_ref.at[pl.ds(d, RUN), :], buf.at[s, pl.ds(dst, RUN), :], sems.at[s]).start()
            return carry

        lax.fori_loop(0, RL // RUN, body, 0, unroll=4)

    @pl.when(i == 0)
    def _():
        gather(dcur_ref, 0)

    @pl.when(i + 1 < n)
    def _():
        gather(dnext_ref, 1 - slot)

    pltpu.make_async_copy(y_ref.at[pl.ds(0, RL), :], buf.at[slot], sems.at[slot]).wait()

    g = _pair_matrix(lp_ref[...], gate_ref[...])
    g_hi = g.astype(BF16)
    g_lo = (g - g_hi.astype(F32)).astype(BF16)
    halves = [_dot(g_hi, yb) + _dot(g_lo, yb) for yb in _unpack_bf16_pairs(buf[slot])]
    xo = x1_ref[...] + g2_ref[0] * jnp.concatenate(halves, axis=1)
    ms = jnp.mean(xo * xo, axis=-1, keepdims=True)
    o_ref[...] = xo * lax.rsqrt(ms + EPS) * fn_ref[...]


def _combine_call(cd, y, lp, x1, gates, g2, fn):
    T, d = x1.shape
    n = T // TB
    per_batch = n // g2.shape[0]
    return pl.pallas_call(
        _combine_kernel,
        grid=(n,),
        in_specs=[pl.BlockSpec((1, 1, CD_LANES), lambda i: (i, 0, 0), memory_space=pltpu.SMEM),
                  pl.BlockSpec((1, 1, CD_LANES), lambda i: (jnp.minimum(i + 1, n - 1), 0, 0),
                               memory_space=pltpu.SMEM),
                  pl.BlockSpec(memory_space=pl.ANY),
                  pl.BlockSpec((TB, LANES), lambda i: (i, 0)),
                  pl.BlockSpec((TB, d), lambda i: (i, 0)),
                  pl.BlockSpec((TB, LANES), lambda i: (i, 0)),
                  pl.BlockSpec((1, 1, d), lambda i: (i // per_batch, 0, 0)),
                  pl.BlockSpec((1, d), lambda i: (0, 0))],
        out_specs=pl.BlockSpec((TB, d), lambda i: (i, 0)),
        out_shape=jax.ShapeDtypeStruct((T, d), F32),
        scratch_shapes=[pltpu.VMEM((2, RL, D_HALF), jnp.uint32), pltpu.SemaphoreType.DMA((2,))],
        compiler_params=_params("arbitrary"),
    )(cd, cd, y, lp, x1, gates, g2, fn)


def _layer(x, c, ctx, c_ctx, w_mod, b_mod, norm1, w_in, gla_w_gk_up, gla_b_gk, gla_norm,
           ssd_conv_w, ssd_conv_b, ssd_dt_bias, ssd_A_log, ssd_D, ssd_norm, w_out,
           norm2, w_router, b_router, w_gate_up, b_gate_up, w_down, b_down, final_norm):
    bsz, L, d = x.shape
    lc = ctx.shape[1]
    assert lc == TB and L % TB == 0 and TB % GRID_W == 0

    cin = jnp.zeros((8, d), F32).at[:bsz].set(c).at[bsz].set(c_ctx)
    mod = _mod_call(cin, w_mod, b_mod.reshape(1, -1))[:bsz + 1]
    sh1, sc1, g1, sh2, sc2, g2 = [m.reshape(bsz + 1, 1, d) for m in jnp.split(mod, 6, axis=-1)]

    o = np.cumsum((0, GLA_QK, GLA_QK, GLA_V, GLA_V, GLA_RANK, SSD_INNER, SSD_CONV_DIM, SSD_HEADS))
    wq, wk, wv, wg, wlow, wz, wx, wdt = [w_in[:, int(a):int(b)] for a, b in zip(o[:-1], o[1:])]
    w_misc = jnp.concatenate([wlow, wdt, wdt, jnp.zeros((d, LANES - GLA_RANK - 2 * SSD_HEADS), F32)], axis=1)
    w_cat = jnp.concatenate([wq, wk, wv, wg, wz, wx, w_misc], axis=1).astype(BF16)
    wup = jnp.zeros((LANES, 2 * GLA_QK), F32).at[:GLA_RANK].set(
        jnp.concatenate([gla_w_gk_up[0], gla_w_gk_up[1]], axis=1)).astype(BF16)
    bup = jnp.concatenate([gla_b_gk[0], gla_b_gk[1]]).reshape(1, -1)
    dtb = jnp.zeros((1, LANES), F32).at[0, DT_F:DT_F + SSD_HEADS].set(ssd_dt_bias[0]) \
                                    .at[0, DT_B:DT_B + SSD_HEADS].set(ssd_dt_bias[1])
    q, k, v, g_all, z_all, xbc, ld, misc = _inproj_call(
        x, ctx, sh1, sc1, norm1.reshape(1, d), w_cat, wup, bup, dtb)

    xs, bc = _conv_call(xbc, ssd_conv_w.reshape(9, SSD_CONV_DIM), ssd_conv_b.reshape(1, -1))

    o_f, o_b = _gla_call(q, k, v, ld, L)

    a_neg = -jnp.exp(ssd_A_log.astype(F32))
    a_f = jnp.zeros((1, LANES), F32).at[0, DT_F:DT_F + SSD_HEADS].set(a_neg[0])
    a_b = jnp.zeros((1, LANES), F32).at[0, DT_B:DT_B + SSD_HEADS].set(a_neg[1])
    dvec = jnp.repeat(ssd_D, SSD_HEADDIM).reshape(1, SSD_INNER)
    y_f, y_b = _ssd_call(xs, bc, misc, a_f, a_b, dvec, L)

    wr = jnp.zeros((d, LANES), F32).at[:, :N_EXPERTS].set(w_router)
    wr_hi = wr.astype(BF16)
    wr_lo = (wr - wr_hi.astype(F32)).astype(BF16)
    br = jnp.full((1, LANES), NEG_BIG, F32).at[0, :N_EXPERTS].set(b_router)
    x1, h2, eidx, gates, counts = _outproj_call(
        o_f, o_b, g_all, y_f, y_b, z_all, x, g1[:bsz], sh2[:bsz], sc2[:bsz],
        jnp.tile(gla_norm, GLA_HEADS).reshape(1, -1), ssd_norm.reshape(1, -1), w_out.astype(BF16),
        norm2.reshape(1, d), wr_hi, wr_lo, br)

    T = bsz * L
    cnt = counts[0, :N_EXPERTS].astype(jnp.int32)
    padded = ((cnt + MOE_BLOCK - 1) // MOE_BLOCK) * MOE_BLOCK
    pend = jnp.cumsum(padded)
    pstart = pend - padded
    max_rows = T * TOP_K + (T // TB) * N_EXPERTS * (RUN - 1)
    n_blocks = -(-max_rows // MOE_BLOCK) + N_EXPERTS
    blk_start = jnp.arange(n_blocks, dtype=jnp.int32) * MOE_BLOCK
    blk_e = jnp.minimum(jnp.sum(pend[None, :] <= blk_start[:, None], axis=1), N_EXPERTS - 1).astype(jnp.int32)
    n_used = (pend[-1] // MOE_BLOCK).astype(jnp.int32)
    blk_info = jnp.concatenate([blk_e, n_used[None]])
    n_tail = n_blocks - (T * TOP_K) // MOE_BLOCK
    tail = n_used + jnp.arange(n_tail, dtype=jnp.int32)
    fill_blocks = jnp.concatenate([
        jnp.where(padded > 0, pend // MOE_BLOCK - 1, -1),
        jnp.where(tail < n_blocks, tail, -1),
        n_blocks + jnp.arange(SPILL_ROWS // MOE_BLOCK, dtype=jnp.int32)]).astype(jnp.int32)
    pstart_row = jnp.zeros((1, LANES), F32).at[0, :N_EXPERTS].set(pstart.astype(F32))
    lp, cd = _pos_call(eidx.reshape(T, LANES), pstart_row)

    xg = _dispatch_call(fill_blocks, cd, h2.reshape(T, d), lp, n_blocks * MOE_BLOCK)
    y = _moe_call(blk_info, xg, w_gate_up, b_gate_up.reshape(N_EXPERTS, 1, -1), w_down,
                  b_down.reshape(N_EXPERTS, 1, -1), n_blocks)
    out = _combine_call(cd, y, lp, x1.reshape(T, d), gates.reshape(T, LANES), g2[:bsz], final_norm.reshape(1, d))
    return out.reshape(bsz, L, d)


def kernel(x, c, ctx, c_ctx, w_mod, b_mod, norm1, w_in, gla_w_gk_up, gla_b_gk, gla_norm, ssd_conv_w, ssd_conv_b, ssd_dt_bias, ssd_A_log, ssd_D, ssd_norm, w_out, norm2, w_router, b_router, w_gate_up, b_gate_up, w_down, b_down, final_norm):
    assert w_mod.shape[0] == 1, "single-layer kernel"
    return _layer(x, c, ctx, c_ctx, w_mod[0], b_mod[0], norm1[0], w_in[0], gla_w_gk_up[0], gla_b_gk[0],
                  gla_norm[0], ssd_conv_w[0], ssd_conv_b[0], ssd_dt_bias[0], ssd_A_log[0], ssd_D[0],
                  ssd_norm[0], w_out[0], norm2[0], w_router[0], b_router[0], w_gate_up[0], b_gate_up[0],
                  w_down[0], b_down[0], final_norm)
```

```python
import functools

import numpy as np
import jax
import jax.numpy as jnp
from jax import lax
from jax.experimental import pallas as pl
from jax.experimental.pallas import tpu as pltpu

F32 = jnp.float32
BF16 = jnp.bfloat16

EPS = 1e-6
GRID_W = 64
GLA_HEADS = 4
GLA_DK = 64
GLA_DV = 128
GLA_QK = GLA_HEADS * GLA_DK
GLA_V = GLA_HEADS * GLA_DV
GLA_RANK = 16
GLA_GATE_NORM = 16.0
SSD_HEADDIM = 64
SSD_INNER = 512
SSD_HEADS = 8
SSD_GROUPS = 2
SSD_HPG = 4
SSD_STATE = 128
SSD_CONV_DIM = 1024
N_EXPERTS = 32
TOP_K = 4
D_FF = 1024
SWIGLU_LIMIT = 7.0
SWIGLU_ALPHA = 1.702
MOE_BLOCK = 512

TB = 256
GLA_C = 64
SSD_C = 128
LANES = 128
EXP_CLAMP = 80.0
DT_F = 16
DT_B = 24
NEG_BIG = -1e30
RUN = 8
RL = 1280
CD_LANES = 256
SPILL_ROWS = -(-RL // MOE_BLOCK) * MOE_BLOCK
D_HALF = 512
VMEM_LIMIT = 56 * 1024 * 1024


def _dot(a, b):
    return jnp.dot(a, b, preferred_element_type=F32)


def _dot_nt(a, b):
    return lax.dot_general(a, b, (((1,), (1,)), ((), ())), preferred_element_type=F32)


def _dot_tn(a, b):
    return lax.dot_general(a, b, (((0,), (0,)), ((), ())), preferred_element_type=F32)


def _split3(a):
    hi = a.astype(BF16)
    r1 = a - hi.astype(F32)
    mid = r1.astype(BF16)
    lo = (r1 - mid.astype(F32)).astype(BF16)
    return hi, mid, lo


def _dot_exact_r(m, a):
    hi, mid, lo = _split3(a)
    return _dot(m, hi) + _dot(m, mid) + _dot(m, lo)


def _dot_hilo_r(m, a):
    hi = a.astype(BF16)
    lo = (a - hi.astype(F32)).astype(BF16)
    return _dot(m, hi) + _dot(m, lo)


def _dot_hilo_l(a, m):
    hi = a.astype(BF16)
    lo = (a - hi.astype(F32)).astype(BF16)
    return _dot(hi, m) + _dot(lo, m)


def _sigmoid(x):
    return 1.0 / (1.0 + jnp.exp(-x))


def _softplus(x):
    return jnp.maximum(x, 0.0) + jnp.log1p(jnp.exp(-jnp.abs(x)))


def _params(*sem):
    return pltpu.CompilerParams(dimension_semantics=sem, vmem_limit_bytes=VMEM_LIMIT)


def _mod_kernel(c_ref, w_ref, b_ref, o_ref):
    c = c_ref[...]
    s = c * _sigmoid(c)
    s_hi = s.astype(BF16)
    s_lo = (s - s_hi.astype(F32)).astype(BF16)
    w = w_ref[...]
    w_hi = w.astype(BF16)
    w_lo = (w - w_hi.astype(F32)).astype(BF16)
    o_ref[...] = _dot(s_hi, w_hi) + _dot(s_lo, w_hi) + _dot(s_hi, w_lo) + b_ref[...]


def _mod_call(cin, w, b):
    rows, d = cin.shape
    n = w.shape[1]
    tn = 1536
    return pl.pallas_call(
        _mod_kernel,
        grid=(n // tn,),
        in_specs=[pl.BlockSpec((rows, d), lambda i: (0, 0)),
                  pl.BlockSpec((d, tn), lambda i: (0, i)),
                  pl.BlockSpec((1, tn), lambda i: (0, i))],
        out_specs=pl.BlockSpec((rows, tn), lambda i: (0, i)),
        out_shape=jax.ShapeDtypeStruct((rows, n), F32),
        compiler_params=_params("arbitrary"),
    )(cin, w, b)


_C_Q, _C_K, _C_V, _C_G, _C_Z, _C_X, _C_M, _C_END = 0, 256, 512, 1024, 1536, 2048, 3072, 3200


def _inproj_kernel(xa_ref, xb_ref, ctx_ref, shl_ref, scl_ref, shc_ref, scc_ref, n1_ref, w_ref, wup_ref, bup_ref,
                   dtb_ref, q_ref, k_ref, v_ref, g_ref, z_ref, xbc_ref, ld_ref, misc_ref, h_scr):
    j = pl.program_id(1)

    def normmod(xv, sh_ref, sc_ref):
        ms = jnp.mean(xv * xv, axis=-1, keepdims=True)
        y = xv * lax.rsqrt(ms + EPS) * n1_ref[...]
        return (y * (1.0 + sc_ref[0]) + sh_ref[0]).astype(BF16)

    @pl.when(j == 0)
    def _():
        h_scr[0:TB, :] = normmod(ctx_ref[0], shc_ref, scc_ref)

    @pl.when(j > 0)
    def _():
        h_scr[0:TB, :] = normmod(xa_ref[0], shl_ref, scl_ref)

    h_scr[TB:2 * TB, :] = normmod(xb_ref[0], shl_ref, scl_ref)
    h = h_scr[...]

    def mm(lo, hi):
        return _dot(h, w_ref[:, lo:hi])

    q_ref[0] = (mm(_C_Q, _C_K) * (GLA_DK ** -0.5)).astype(BF16)
    k_ref[0] = mm(_C_K, _C_V).astype(BF16)
    v_ref[0] = mm(_C_V, _C_G).astype(BF16)
    g_ref[0] = mm(_C_G, _C_Z)
    z_ref[0] = mm(_C_Z, _C_X)
    xbc_ref[0] = mm(_C_X, _C_M)
    m = mm(_C_M, _C_END)
    zz = _dot(m.astype(BF16), wup_ref[...]) + bup_ref[...]
    ld_ref[0] = -_softplus(-zz) * (1.0 / GLA_GATE_NORM)
    misc_ref[0] = _softplus(m + dtb_ref[...])


def _inproj_call(x, ctx, sh1, sc1, n1, w_cat, wup, bup, dtb):
    bsz, L, d = x.shape
    nx = L // TB
    nj = -(-(nx + 1) // 2)
    tok = lambda n: pl.BlockSpec((1, 2 * TB, n), lambda b, j: (b, j, 0))
    const = lambda a: pl.BlockSpec(a.shape, lambda b, j: (0,) * a.ndim)
    xblk = lambda off: pl.BlockSpec((1, TB, d), lambda b, j: (b, jnp.clip(2 * j + off, 0, nx - 1), 0))
    mod_lat = pl.BlockSpec((1, 1, d), lambda b, j: (b, 0, 0))
    mod_ctx = pl.BlockSpec((1, 1, d), lambda b, j: (bsz, 0, 0))
    outs = [(GLA_QK, BF16), (GLA_QK, BF16), (GLA_V, BF16), (GLA_V, F32), (SSD_INNER, F32),
            (SSD_CONV_DIM, F32), (2 * GLA_QK, F32), (LANES, F32)]
    return pl.pallas_call(
        _inproj_kernel,
        grid=(bsz, nj),
        in_specs=[xblk(-1), xblk(0), pl.BlockSpec((1, TB, d), lambda b, j: (b, 0, 0)),
                  mod_lat, mod_lat, mod_ctx, mod_ctx,
                  const(n1), const(w_cat), const(wup), const(bup), const(dtb)],
        out_specs=[tok(n) for n, _ in outs],
        out_shape=[jax.ShapeDtypeStruct((bsz, nj * 2 * TB, n), dt) for n, dt in outs],
        scratch_shapes=[pltpu.VMEM((2 * TB, d), BF16)],
        compiler_params=_params("arbitrary", "arbitrary"),
    )(x, x, ctx, sh1, sc1, sh1, sc1, n1, w_cat, wup, bup, dtb)


_EXT_PAD = 8
_EXT_BASE = _EXT_PAD + GRID_W
_EXT_ROWS = 2 * _EXT_PAD + 2 * GRID_W + TB


def _conv_kernel(prev_ref, cur_ref, next_ref, w_ref, b_ref, xs_ref, bc_ref, ext):
    j = pl.program_id(1)
    nj = pl.num_programs(1)
    is_ctx = j == 0
    zpad = jnp.zeros((_EXT_PAD, SSD_CONV_DIM), F32)
    ext[0:_EXT_PAD, :] = zpad
    ext[_EXT_ROWS - _EXT_PAD:_EXT_ROWS, :] = zpad
    ext[_EXT_PAD:_EXT_BASE, :] = jnp.where(j >= 2, prev_ref[0], 0.0)
    ext[_EXT_BASE:_EXT_BASE + TB, :] = cur_ref[0]
    ext[_EXT_BASE + TB:_EXT_BASE + TB + GRID_W, :] = jnp.where(
        jnp.logical_and(j >= 1, j <= nj - 2), next_ref[0], 0.0)

    t = lax.broadcasted_iota(jnp.int32, (TB, LANES), 0)
    col = t & (GRID_W - 1)
    ok_l = jnp.where(is_ctx, t, col) >= 1
    ok_r = jnp.where(is_ctx, t - (TB - GRID_W), col) <= GRID_W - 2
    lat = jnp.where(is_ctx, 0.0, 1.0)

    for c in range(SSD_CONV_DIM // LANES):
        lo, hi = c * LANES, (c + 1) * LANES
        acc = jnp.zeros((TB, LANES), F32)
        for dr in (-1, 0, 1):
            for dc in (-1, 0, 1):
                start = _EXT_BASE + GRID_W * dr + dc
                tap = ext[start:start + TB, lo:hi]
                wi = 3 * (dr + 1) + (dc + 1)
                wv = w_ref[wi:wi + 1, lo:hi]
                if dr != 0:
                    wv = wv * lat
                if dc == -1:
                    tap = jnp.where(ok_l, tap, 0.0)
                elif dc == 1:
                    tap = jnp.where(ok_r, tap, 0.0)
                acc = acc + tap * wv
        y = acc + b_ref[:, lo:hi]
        y = y * _sigmoid(y)
        if c < SSD_INNER // LANES:
            xs_ref[0, :, lo:hi] = y
        else:
            bc_ref[0, :, lo - SSD_INNER:hi - SSD_INNER] = y.astype(BF16)


def _conv_call(xbc, w9, bias, ls):
    bsz, _, ch = xbc.shape
    nj = ls // TB
    rpb = TB // GRID_W
    nrow = ls // GRID_W
    return pl.pallas_call(
        _conv_kernel,
        grid=(bsz, nj),
        in_specs=[pl.BlockSpec((1, GRID_W, ch), lambda b, j: (b, jnp.maximum(rpb * j - 1, 0), 0)),
                  pl.BlockSpec((1, TB, ch), lambda b, j: (b, j, 0)),
                  pl.BlockSpec((1, GRID_W, ch), lambda b, j: (b, jnp.minimum(rpb * j + rpb, nrow - 1), 0)),
                  pl.BlockSpec((9, ch), lambda b, j: (0, 0)),
                  pl.BlockSpec((1, ch), lambda b, j: (0, 0))],
        out_specs=[pl.BlockSpec((1, TB, SSD_INNER), lambda b, j: (b, j, 0)),
                   pl.BlockSpec((1, TB, ch - SSD_INNER), lambda b, j: (b, j, 0))],
        out_shape=[jax.ShapeDtypeStruct((bsz, ls, SSD_INNER), F32),
                   jax.ShapeDtypeStruct((bsz, ls, ch - SSD_INNER), BF16)],
        scratch_shapes=[pltpu.VMEM((_EXT_ROWS, ch), F32)],
        compiler_params=_params("arbitrary", "arbitrary"),
    )(xbc, xbc, xbc, w9, bias)


def _fwd_blk(s):
    return s


def _bwd_blk(s, ns):
    return jnp.where(s == 0, 0, ns - s)


def _gla_dir(q, k, v, la, st_ref, tri_m, fwd):
    C = GLA_C
    b = _dot_hilo_r(tri_m, la)
    bt = b[C - 1:C, :] if fwd else b[0:1, :]
    r = 0.5 * bt
    qf = q.astype(F32)
    kf = k.astype(F32)
    qt = (qf * jnp.exp(jnp.minimum(b - r, EXP_CLAMP))).astype(BF16)
    kt = (kf * jnp.exp(jnp.minimum(r - b, EXP_CLAMP))).astype(BF16)
    head_k = lax.broadcasted_iota(jnp.int32, (C, GLA_QK), 1) >> 6
    kst = jnp.concatenate([jnp.where(head_k == h, kt, jnp.zeros_like(kt)) for h in range(GLA_HEADS)], axis=0)
    sc = _dot_nt(qt, kst)
    ii = lax.broadcasted_iota(jnp.int32, (C, GLA_HEADS * C), 0)
    jj = lax.broadcasted_iota(jnp.int32, (C, GLA_HEADS * C), 1) & (C - 1)
    causal = (jj <= ii) if fwd else (jj >= ii)
    p = jnp.where(causal, sc, 0.0).astype(BF16)
    head_v = lax.broadcasted_iota(jnp.int32, (C, GLA_V), 1) >> 7
    vst = jnp.concatenate([jnp.where(head_v == h, v, jnp.zeros_like(v)) for h in range(GLA_HEADS)], axis=0)
    o = _dot(p, vst)
    st = st_ref[...]
    qd = (qf * jnp.exp(b)).astype(BF16)
    o = o + _dot_nt(qd, st.astype(BF16))
    kd = (kf * jnp.exp(bt - b)).astype(BF16)
    u = _dot_tn(v, kd)
    rv = lax.broadcasted_iota(jnp.int32, (GLA_V, GLA_QK), 0) >> 7
    cv = lax.broadcasted_iota(jnp.int32, (GLA_V, GLA_QK), 1) >> 6
    st_ref[...] = st * jnp.exp(bt) + jnp.where(rv == cv, u, 0.0)
    return o


def _gla_kernel(qf_ref, kf_ref, vf_ref, lf_ref, qb_ref, kb_ref, vb_ref, lb_ref, trif_ref, trib_ref,
                of_ref, ob_ref, stf, stb):
    s = pl.program_id(1)

    @pl.when(s == 0)
    def _():
        stf[...] = jnp.zeros_like(stf)
        stb[...] = jnp.zeros_like(stb)

    nsub = TB // GLA_C

    for i in range(nsub):
        sf = pl.ds(i * GLA_C, GLA_C)
        sb = pl.ds((nsub - 1 - i) * GLA_C, GLA_C)
        of_ref[0, sf, :] = _gla_dir(qf_ref[0, sf, :], kf_ref[0, sf, :], vf_ref[0, sf, :], lf_ref[0, sf, :],
                                    stf, trif_ref[...], True)
        ob_ref[0, sb, :] = _gla_dir(qb_ref[0, sb, :], kb_ref[0, sb, :], vb_ref[0, sb, :], lb_ref[0, sb, :],
                                    stb, trib_ref[...], False)


def _gla_call(q, k, v, ld, L):
    bsz = q.shape[0]
    nx = L // TB
    ns = nx + 1
    trif = jnp.asarray(np.tril(np.ones((GLA_C, GLA_C), np.float32)), BF16)
    trib = jnp.asarray(np.triu(np.ones((GLA_C, GLA_C), np.float32)), BF16)
    f = lambda n, lane=0: pl.BlockSpec((1, TB, n), lambda b, s: (b, _fwd_blk(s), lane))
    r = lambda n, lane=0: pl.BlockSpec((1, TB, n), lambda b, s: (b, _bwd_blk(s, ns), lane))
    tri = pl.BlockSpec((GLA_C, GLA_C), lambda b, s: (0, 0))
    return pl.pallas_call(
        _gla_kernel,
        grid=(bsz, ns),
        in_specs=[f(GLA_QK), f(GLA_QK), f(GLA_V), f(GLA_QK, 0),
                  r(GLA_QK), r(GLA_QK), r(GLA_V), r(GLA_QK, 1), tri, tri],
        out_specs=[pl.BlockSpec((1, TB, GLA_V), lambda b, s: (b, jnp.maximum(s - 1, 0), 0)),
                   pl.BlockSpec((1, TB, GLA_V), lambda b, s: (b, jnp.where(s == 0, nx - 1, nx - s), 0))],
        out_shape=[jax.ShapeDtypeStruct((bsz, L, GLA_V), F32)] * 2,
        scratch_shapes=[pltpu.VMEM((GLA_V, GLA_QK), F32)] * 2,
        compiler_params=_params("arbitrary", "arbitrary"),
    )(q, k, v, ld, q, k, v, ld, trif, trib)


def _ssd_dir(xs, bc, dtm, avec, dvec, st_ref, tri_m, e_m, base, fwd):
    C = SSD_C
    dt_exp = _dot_hilo_l(dtm, e_m)
    a = dtm * avec
    acum = _dot_exact_r(tri_m, a)
    acum_exp = _dot_hilo_l(acum, e_m)
    acum_t = acum.T
    xdt = xs * dt_exp
    xdt_b = xdt.astype(BF16)
    ii = lax.broadcasted_iota(jnp.int32, (C, C), 0)
    jj = lax.broadcasted_iota(jnp.int32, (C, C), 1)
    tri = (jj <= ii) if fwd else (jj >= ii)
    al_exp = acum_exp[C - 1:C, :] if fwd else acum_exp[0:1, :]
    head2 = lax.broadcasted_iota(jnp.int32, (C, 256), 1) >> 6
    ys = []
    for g in range(SSD_GROUPS):
        gl, gh = 256 * g, 256 * (g + 1)
        bg = bc[:, 128 * g:128 * (g + 1)]
        cg = bc[:, 256 + 128 * g:256 + 128 * (g + 1)]
        cb = _dot_nt(cg, bg)
        ms = []
        for rr in range(SSD_HPG):
            ln = base + SSD_HPG * g + rr
            diff = acum[:, ln:ln + 1] - acum_t[ln:ln + 1, :]
            seg = jnp.where(tri, jnp.exp(jnp.minimum(diff, 0.0)), 0.0)
            ms.append((cb * seg).astype(BF16))
        res = _dot(jnp.concatenate(ms, axis=0), xdt_b[:, gl:gh])
        yg = jnp.zeros((C, 256), F32)
        for rr in range(SSD_HPG):
            yg = yg + jnp.where(head2 == rr, res[rr * C:(rr + 1) * C, :], 0.0)
        sg = st_ref[g]
        yoff = _dot(cg, sg.astype(BF16)) * jnp.exp(acum_exp[:, gl:gh])
        wst = jnp.exp(al_exp[:, gl:gh] - acum_exp[:, gl:gh])
        ug = _dot_tn(bg, (xdt[:, gl:gh] * wst).astype(BF16))
        st_ref[g] = sg * jnp.exp(al_exp[:, gl:gh]) + ug
        ys.append(yg + yoff)
    y = jnp.concatenate(ys, axis=1)
    if dvec is not None:
        y = y + dvec * xs
    return y


def _ssd_kernel(xf_ref, bcf_ref, mf_ref, xb_ref, bcb_ref, mb_ref, af_ref, ab_ref, d_ref,
                trif_ref, trib_ref, ef_ref, eb_ref, yf_ref, yb_ref, stf, stb):
    s = pl.program_id(1)

    @pl.when(s == 0)
    def _():
        stf[...] = jnp.zeros_like(stf)
        stb[...] = jnp.zeros_like(stb)

    nsub = TB // SSD_C

    for i in range(nsub):
        sf = pl.ds(i * SSD_C, SSD_C)
        sb = pl.ds((nsub - 1 - i) * SSD_C, SSD_C)
        yf_ref[0, sf, :] = _ssd_dir(xf_ref[0, sf, :], bcf_ref[0, sf, :], mf_ref[0, sf, :], af_ref[...], d_ref[...],
                                    stf, trif_ref[...], ef_ref[...], DT_F, True)
        yb_ref[0, sb, :] = _ssd_dir(xb_ref[0, sb, :], bcb_ref[0, sb, :], mb_ref[0, sb, :], ab_ref[...], None,
                                    stb, trib_ref[...], eb_ref[...], DT_B, False)


def _expand_matrix(base):
    e = np.zeros((LANES, SSD_INNER), np.float32)
    for h in range(SSD_HEADS):
        e[base + h, SSD_HEADDIM * h:SSD_HEADDIM * (h + 1)] = 1.0
    return jnp.asarray(e, BF16)


def _ssd_call(xs, bc, misc, a_f, a_b, dvec, L):
    bsz, ls, _ = xs.shape
    ns = ls // TB
    nx = L // TB
    trif = jnp.asarray(np.tril(np.ones((SSD_C, SSD_C), np.float32)), BF16)
    trib = jnp.asarray(np.triu(np.ones((SSD_C, SSD_C), np.float32)), BF16)
    ef, eb = _expand_matrix(DT_F), _expand_matrix(DT_B)
    f = lambda n: pl.BlockSpec((1, TB, n), lambda b, s: (b, _fwd_blk(s), 0))
    r = lambda n: pl.BlockSpec((1, TB, n), lambda b, s: (b, _bwd_blk(s, ns), 0))
    const = lambda a: pl.BlockSpec(a.shape, lambda b, s: (0,) * a.ndim)
    return pl.pallas_call(
        _ssd_kernel,
        grid=(bsz, ns),
        in_specs=[f(SSD_INNER), f(512), f(LANES), r(SSD_INNER), r(512), r(LANES),
                  const(a_f), const(a_b), const(dvec), const(trif), const(trib), const(ef), const(eb)],
        out_specs=[pl.BlockSpec((1, TB, SSD_INNER), lambda b, s: (b, jnp.maximum(s - 1, 0), 0)),
                   pl.BlockSpec((1, TB, SSD_INNER), lambda b, s: (b, jnp.where(s == 0, nx - 1, nx - s), 0))],
        out_shape=[jax.ShapeDtypeStruct((bsz, L, SSD_INNER), F32)] * 2,
        scratch_shapes=[pltpu.VMEM((SSD_GROUPS, SSD_STATE, SSD_HPG * SSD_HEADDIM), F32)] * 2,
        compiler_params=_params("arbitrary", "arbitrary"),
    )(xs, bc, misc, xs, bc, misc, a_f, a_b, dvec, trif, trib, ef, eb)


def _outproj_kernel(of_ref, ob_ref, g_ref, yf_ref, yb_ref, z_ref, x_ref, g1_ref, sh2_ref, sc2_ref,
                    gn_ref, sn_ref, wo_ref, n2_ref, wrh_ref, wrl_ref, br_ref,
                    x1_ref, h2_ref, eidx_ref, gate_ref, cnt_ref):
    first = jnp.logical_and(pl.program_id(0) == 0, pl.program_id(1) == 0)

    @pl.when(first)
    def _():
        cnt_ref[...] = jnp.zeros_like(cnt_ref)

    o = of_ref[0] + ob_ref[0]
    gg = g_ref[0]
    parts = []
    for h in range(GLA_HEADS):
        lo, hi = GLA_DV * h, GLA_DV * (h + 1)
        oh = o[:, lo:hi]
        ms = jnp.mean(oh * oh, axis=-1, keepdims=True)
        gh = gg[:, lo:hi]
        parts.append((oh * lax.rsqrt(ms + EPS) * gn_ref[:, lo:hi] * (gh * _sigmoid(gh))).astype(BF16))
    zz = z_ref[0]
    u = (yf_ref[0] + yb_ref[0]) * (zz * _sigmoid(zz))
    gw = SSD_INNER // SSD_GROUPS
    for g in range(SSD_GROUPS):
        lo, hi = gw * g, gw * (g + 1)
        ug = u[:, lo:hi]
        ms = jnp.mean(ug * ug, axis=-1, keepdims=True)
        parts.append((ug * lax.rsqrt(ms + EPS) * sn_ref[:, lo:hi]).astype(BF16))
    mix = jnp.concatenate(parts, axis=1)
    x1 = x_ref[0] + g1_ref[0] * _dot(mix, wo_ref[...])
    x1_ref[0] = x1
    ms = jnp.mean(x1 * x1, axis=-1, keepdims=True)
    h2 = (x1 * lax.rsqrt(ms + EPS) * n2_ref[...]) * (1.0 + sc2_ref[0]) + sh2_ref[0]
    h_hi = h2.astype(BF16)
    h2_ref[0] = h_hi
    h_lo = (h2 - h_hi.astype(F32)).astype(BF16)
    logits = (_dot(h_hi, wrh_ref[...]) + _dot(h_lo, wrh_ref[...]) + _dot(h_hi, wrl_ref[...])) + br_ref[...]
    lane = lax.broadcasted_iota(jnp.int32, (TB, LANES), 1).astype(F32)
    work = logits
    eidx = jnp.full((TB, LANES), -1.0, F32)
    gates = jnp.zeros((TB, LANES), F32)
    sel = jnp.zeros((TB, LANES), F32)
    m0 = None
    den = jnp.zeros((TB, 1), F32)
    for kk in range(TOP_K):
        m = jnp.max(work, axis=-1, keepdims=True)
        idx = jnp.min(jnp.where(work == m, lane, float(LANES)), axis=-1, keepdims=True)
        hit = lane == idx
        if m0 is None:
            m0 = m
        e = jnp.exp(m - m0)
        den = den + e
        eidx = jnp.where(lane == float(kk), idx, eidx)
        gates = jnp.where(lane == float(kk), e, gates)
        sel = jnp.where(hit, 1.0, sel)
        work = jnp.where(hit, NEG_BIG, work)
    eidx_ref[0] = eidx.astype(jnp.int32)
    gate_ref[0] = gates / den
    cnt = jnp.sum(sel, axis=0, keepdims=True)
    cnt_ref[...] += jnp.floor((cnt + (RUN - 1.0)) * (1.0 / RUN)) * RUN


def _outproj_call(o_f, o_b, g_all, y_f, y_b, z_all, x, g1, sh2, sc2, gn, sn, wo, n2, wr_hi, wr_lo, br):
    bsz, L, d = x.shape
    nj = L // TB
    tok = lambda n: pl.BlockSpec((1, TB, n), lambda b, j: (b, j, 0))
    tok_off = lambda n: pl.BlockSpec((1, TB, n), lambda b, j: (b, j + 1, 0))
    const = lambda a: pl.BlockSpec(a.shape, lambda b, j: (0,) * a.ndim)
    mod = pl.BlockSpec((1, 1, d), lambda b, j: (b, 0, 0))
    return pl.pallas_call(
        _outproj_kernel,
        grid=(bsz, nj),
        in_specs=[tok(GLA_V), tok(GLA_V), tok_off(GLA_V), tok(SSD_INNER), tok(SSD_INNER), tok_off(SSD_INNER),
                  tok(d), mod, mod, mod, const(gn), const(sn), const(wo), const(n2),
                  const(wr_hi), const(wr_lo), const(br)],
        out_specs=[tok(d), tok(d), tok(LANES), tok(LANES), pl.BlockSpec((1, LANES), lambda b, j: (0, 0))],
        out_shape=[jax.ShapeDtypeStruct((bsz, L, d), F32), jax.ShapeDtypeStruct((bsz, L, d), BF16),
                   jax.ShapeDtypeStruct((bsz, L, LANES), jnp.int32), jax.ShapeDtypeStruct((bsz, L, LANES), F32),
                   jax.ShapeDtypeStruct((1, LANES), F32)],
        compiler_params=_params("arbitrary", "arbitrary"),
    )(o_f, o_b, g_all, y_f, y_b, z_all, x, g1, sh2, sc2, gn, sn, wo, n2, wr_hi, wr_lo, br)


def _pos_kernel(eidx_ref, pstart_ref, lst_ref, ust_ref, lp_ref, cd_ref, carry):
    @pl.when(pl.program_id(0) == 0)
    def _():
        carry[...] = pstart_ref[...]

    eidx = eidx_ref[...]
    lane = lax.broadcasted_iota(jnp.int32, (TB, LANES), 1)
    hits = [lane == eidx[:, kk:kk + 1] for kk in range(TOP_K)]
    sel = jnp.zeros((TB, LANES), F32)
    for hmask in hits:
        sel = jnp.where(hmask, 1.0, sel)
    cnt = jnp.sum(sel, axis=0, keepdims=True)
    run = jnp.floor((cnt + (RUN - 1.0)) * (1.0 / RUN)) * RUN
    rank = _dot(lst_ref[...], sel.astype(BF16))
    loff = _dot(jnp.broadcast_to(run, (8, LANES)).astype(BF16), ust_ref[...])[0:1]
    pos = loff + rank
    lp = jnp.zeros((TB, LANES), jnp.int32)
    for kk, hmask in enumerate(hits):
        lk = jnp.sum(jnp.where(hmask, pos, 0.0), axis=-1, keepdims=True)
        lp = jnp.where(lane == kk, lk.astype(jnp.int32), lp)
    lp_ref[...] = lp

    base = carry[...]
    eye = lax.broadcasted_iota(jnp.int32, (LANES, LANES), 0) == lax.broadcasted_iota(jnp.int32, (LANES, LANES), 1)
    col = lambda v: jnp.sum(jnp.where(eye, v, 0.0), axis=1, keepdims=True)
    loff_c, run_c, shift_c = col(loff), col(run), col(base - loff)
    row0 = (lax.broadcasted_iota(jnp.int32, (LANES, CD_LANES), 1) * RUN).astype(F32)
    inside = jnp.where(row0 >= loff_c, 1.0, 0.0) * jnp.where(row0 < loff_c + run_c, 1.0, 0.0)
    valid = jnp.sum(inside, axis=0, keepdims=True)
    dest = jnp.sum(inside * shift_c, axis=0, keepdims=True) + row0[0:1]
    cd_ref[0] = jnp.where(valid > 0.0, dest, -1.0).astype(jnp.int32)
    carry[...] = base + run


def _pos_call(eidx, pstart):
    T = eidx.shape[0]
    lst = jnp.asarray(np.tril(np.ones((TB, TB), np.float32), -1), BF16)
    ust = jnp.asarray(np.triu(np.ones((LANES, LANES), np.float32), 1), BF16)
    return pl.pallas_call(
        _pos_kernel,
        grid=(T // TB,),
        in_specs=[pl.BlockSpec((TB, LANES), lambda i: (i, 0)),
                  pl.BlockSpec((1, LANES), lambda i: (0, 0)),
                  pl.BlockSpec((TB, TB), lambda i: (0, 0)),
                  pl.BlockSpec((LANES, LANES), lambda i: (0, 0))],
        out_specs=[pl.BlockSpec((TB, LANES), lambda i: (i, 0)),
                   pl.BlockSpec((1, 1, CD_LANES), lambda i: (i, 0, 0))],
        out_shape=[jax.ShapeDtypeStruct((T, LANES), jnp.int32),
                   jax.ShapeDtypeStruct((T // TB, 1, CD_LANES), jnp.int32)],
        scratch_shapes=[pltpu.VMEM((1, LANES), F32)],
        compiler_params=_params("arbitrary"),
    )(eidx, pstart, lst, ust)


def _pair_matrix(lp, weights):
    j = lax.broadcasted_iota(jnp.int32, (TB, RL), 1)
    m = jnp.zeros((TB, RL), F32)
    for kk in range(TOP_K):
        w = 1.0 if weights is None else weights[:, kk:kk + 1]
        m = m + jnp.where(j == lp[:, kk:kk + 1], w, 0.0)
    return m


def _pack_bf16_pairs(v):
    bits = pltpu.bitcast(v.astype(BF16).astype(F32), jnp.uint32)
    return bits[:, D_HALF:] | (bits[:, :D_HALF] >> 16)


def _unpack_bf16_pairs(w):
    lo = pltpu.bitcast(w << 16, F32).astype(BF16)
    hi = pltpu.bitcast(w & jnp.uint32(0xFFFF0000), F32).astype(BF16)
    return lo, hi


def _dispatch_kernel(fill_ref, cd_ref, h_ref, lp_ref, xg_ref, sorted_s, sems, fill_sem):
    i = pl.program_id(0)
    n = pl.num_programs(0)
    slot = i & 1
    sorted_s[slot] = _pack_bf16_pairs(_dot_tn(_pair_matrix(lp_ref[...], None).astype(BF16), h_ref[...]))

    @pl.when(i == 0)
    def _():
        def fill(k):
            blk = pl.multiple_of(fill_ref[k] * MOE_BLOCK, MOE_BLOCK)
            return pltpu.make_async_copy(sorted_s.at[0, pl.ds(0, MOE_BLOCK), :],
                                         xg_ref.at[pl.ds(blk, MOE_BLOCK), :], fill_sem)

        for k in range(fill_ref.shape[0]):
            pl.when(fill_ref[k] >= 0)(lambda k=k: fill(k).start())
        for k in range(fill_ref.shape[0]):
            pl.when(fill_ref[k] >= 0)(lambda k=k: fill(k).wait())

    spill = xg_ref.shape[0] - (2 - slot) * SPILL_ROWS

    def body(c, carry):
        d = cd_ref[0, 0, c]
        d = pl.multiple_of(jnp.where(d < 0, spill + RUN * c, d), RUN)
        src = pl.multiple_of(RUN * c, RUN)
        pltpu.make_async_copy(sorted_s.at[slot, pl.ds(src, RUN), :], xg_ref.at[pl.ds(d, RUN), :],
                              sems.at[slot]).start()
        return carry

    lax.fori_loop(0, RL // RUN, body, 0, unroll=4)

    def drain(s):
        pltpu.make_async_copy(sorted_s.at[s], xg_ref.at[pl.ds(0, RL), :], sems.at[s]).wait()

    pl.when(i > 0)(lambda: drain(1 - slot))
    pl.when(i == n - 1)(lambda: drain(slot))


def _dispatch_call(fill_blocks, cd, h2, lp, P):
    T, d = h2.shape
    grid_spec = pltpu.PrefetchScalarGridSpec(
        num_scalar_prefetch=1,
        grid=(T // TB,),
        in_specs=[pl.BlockSpec((1, 1, CD_LANES), lambda i, fb: (i, 0, 0), memory_space=pltpu.SMEM),
                  pl.BlockSpec((TB, d), lambda i, fb: (i, 0)),
                  pl.BlockSpec((TB, LANES), lambda i, fb: (i, 0))],
        out_specs=pl.BlockSpec(memory_space=pl.ANY),
        scratch_shapes=[pltpu.VMEM((2, RL, D_HALF), jnp.uint32), pltpu.SemaphoreType.DMA((2,)),
                        pltpu.SemaphoreType.DMA(())],
    )
    return pl.pallas_call(
        _dispatch_kernel,
        grid_spec=grid_spec,
        out_shape=jax.ShapeDtypeStruct((P + 2 * SPILL_ROWS, D_HALF), jnp.uint32),
        compiler_params=_params("arbitrary"),
    )(fill_blocks, cd, h2, lp)


def _moe_kernel(be_ref, x_ref, wgu_ref, bgu_ref, wd_ref, bd_ref, y_ref, wgu_b, wd_b, act_s):
    i = pl.program_id(0)
    n_used = be_ref[pl.num_programs(0)]
    changed = jnp.logical_or(i == 0, be_ref[i] != be_ref[jnp.maximum(i - 1, 0)])

    @pl.when(changed)
    def _():
        wgu_b[...] = wgu_ref[0].astype(BF16)
        wd_b[...] = wd_ref[0].astype(BF16)

    @pl.when(i < n_used)
    def _():
        xb = jnp.concatenate(_unpack_bf16_pairs(x_ref[...]), axis=1)
        cw = 256
        for c in range(D_FF // cw):
            lo, hi = c * cw, (c + 1) * cw
            gate = _dot(xb, wgu_b[:, lo:hi]) + bgu_ref[0, :, lo:hi]
            up = _dot(xb, wgu_b[:, D_FF + lo:D_FF + hi]) + bgu_ref[0, :, D_FF + lo:D_FF + hi]
            gate = jnp.minimum(gate, SWIGLU_LIMIT)
            up = jnp.clip(up, -SWIGLU_LIMIT, SWIGLU_LIMIT)
            act_s[:, lo:hi] = ((up + 1.0) * (gate * _sigmoid(SWIGLU_ALPHA * gate))).astype(BF16)
        y_ref[...] = _pack_bf16_pairs(_dot(act_s[...], wd_b[...]) + bd_ref[0])

    @pl.when(i >= n_used)
    def _():
        y_ref[...] = jnp.zeros_like(y_ref)


def _moe_call(blk_e, xg, wgu, bgu, wd, bd, nb):
    P = nb * MOE_BLOCK
    ne, d, f2 = wgu.shape
    grid_spec = pltpu.PrefetchScalarGridSpec(
        num_scalar_prefetch=1,
        grid=(nb,),
        in_specs=[pl.BlockSpec((MOE_BLOCK, D_HALF), lambda i, be: (jnp.minimum(i, be[nb] - 1), 0)),
                  pl.BlockSpec((1, d, f2), lambda i, be: (be[i], 0, 0)),
                  pl.BlockSpec((1, 1, f2), lambda i, be: (be[i], 0, 0)),
                  pl.BlockSpec((1, f2 // 2, d), lambda i, be: (be[i], 0, 0)),
                  pl.BlockSpec((1, 1, d), lambda i, be: (be[i], 0, 0))],
        out_specs=pl.BlockSpec((MOE_BLOCK, D_HALF), lambda i, be: (i, 0)),
        scratch_shapes=[pltpu.VMEM((d, f2), BF16), pltpu.VMEM((f2 // 2, d), BF16),
                        pltpu.VMEM((MOE_BLOCK, f2 // 2), BF16)],
    )
    return pl.pallas_call(
        _moe_kernel,
        grid_spec=grid_spec,
        out_shape=jax.ShapeDtypeStruct((P, D_HALF), jnp.uint32),
        compiler_params=_params("arbitrary"),
    )(blk_e, xg, wgu, bgu, wd, bd)


def _combine_kernel(dcur_ref, dnext_ref, y_ref, lp_ref, x1_ref, gate_ref, g2_ref, fn_ref, o_ref, buf, sems):
    i = pl.program_id(0)
    n = pl.num_programs(0)
    slot = i & 1

    def gather(dref, s):
        def body(c, carry):
            d = pl.multiple_of(jnp.maximum(dref[0, 0, c], 0), RUN)
            dst = pl.multiple_of(RUN * c, RUN)
            pltpu.make_async_copy(y_ref.at[pl.ds(d, RUN), :], buf.at[s, pl.ds(dst, RUN), :], sems.at[s]).start()
            return carry

        lax.fori_loop(0, RL // RUN, body, 0, unroll=4)

    @pl.when(i == 0)
    def _():
        gather(dcur_ref, 0)

    @pl.when(i + 1 < n)
    def _():
        gather(dnext_ref, 1 - slot)

    pltpu.make_async_copy(y_ref.at[pl.ds(0, RL), :], buf.at[slot], sems.at[slot]).wait()

    g = _pair_matrix(lp_ref[...], gate_ref[...]).astype(BF16)
    halves = [_dot(g, yb) for yb in _unpack_bf16_pairs(buf[slot])]
    xo = x1_ref[...] + g2_ref[0] * jnp.concatenate(halves, axis=1)
    ms = jnp.mean(xo * xo, axis=-1, keepdims=True)
    o_ref[...] = xo * lax.rsqrt(ms + EPS) * fn_ref[...]


def _combine_call(cd, y, lp, x1, gates, g2, fn):
    T, d = x1.shape
    n = T // TB
    per_batch = n // g2.shape[0]
    return pl.pallas_call(
        _combine_kernel,
        grid=(n,),
        in_specs=[pl.BlockSpec((1, 1, CD_LANES), lambda i: (i, 0, 0), memory_space=pltpu.SMEM),
                  pl.BlockSpec((1, 1, CD_LANES), lambda i: (jnp.minimum(i + 1, n - 1), 0, 0),
                               memory_space=pltpu.SMEM),
                  pl.BlockSpec(memory_space=pl.ANY),
                  pl.BlockSpec((TB, LANES), lambda i: (i, 0)),
                  pl.BlockSpec((TB, d), lambda i: (i, 0)),
                  pl.BlockSpec((TB, LANES), lambda i: (i, 0)),
                  pl.BlockSpec((1, 1, d), lambda i: (i // per_batch, 0, 0)),
                  pl.BlockSpec((1, d), lambda i: (0, 0))],
        out_specs=pl.BlockSpec((TB, d), lambda i: (i, 0)),
        out_shape=jax.ShapeDtypeStruct((T, d), F32),
        scratch_shapes=[pltpu.VMEM((2, RL, D_HALF), jnp.uint32), pltpu.SemaphoreType.DMA((2,))],
        compiler_params=_params("arbitrary"),
    )(cd, cd, y, lp, x1, gates, g2, fn)


def _layer(x, c, ctx, c_ctx, w_mod, b_mod, norm1, w_in, gla_w_gk_up, gla_b_gk, gla_norm,
           ssd_conv_w, ssd_conv_b, ssd_dt_bias, ssd_A_log, ssd_D, ssd_norm, w_out,
           norm2, w_router, b_router, w_gate_up, b_gate_up, w_down, b_down, final_norm):
    bsz, L, d = x.shape
    lc = ctx.shape[1]
    assert lc == TB and L % TB == 0 and TB % GRID_W == 0

    cin = jnp.zeros((8, d), F32).at[:bsz].set(c).at[bsz].set(c_ctx)
    mod = _mod_call(cin, w_mod, b_mod.reshape(1, -1))[:bsz + 1]
    sh1, sc1, g1, sh2, sc2, g2 = [m.reshape(bsz + 1, 1, d) for m in jnp.split(mod, 6, axis=-1)]

    o = np.cumsum((0, GLA_QK, GLA_QK, GLA_V, GLA_V, GLA_RANK, SSD_INNER, SSD_CONV_DIM, SSD_HEADS))
    wq, wk, wv, wg, wlow, wz, wx, wdt = [w_in[:, int(a):int(b)] for a, b in zip(o[:-1], o[1:])]
    w_misc = jnp.concatenate([wlow, wdt, wdt, jnp.zeros((d, LANES - GLA_RANK - 2 * SSD_HEADS), F32)], axis=1)
    w_cat = jnp.concatenate([wq, wk, wv, wg, wz, wx, w_misc], axis=1).astype(BF16)
    wup = jnp.zeros((LANES, 2 * GLA_QK), F32).at[:GLA_RANK].set(
        jnp.concatenate([gla_w_gk_up[0], gla_w_gk_up[1]], axis=1)).astype(BF16)
    bup = jnp.concatenate([gla_b_gk[0], gla_b_gk[1]]).reshape(1, -1)
    dtb = jnp.zeros((1, LANES), F32).at[0, DT_F:DT_F + SSD_HEADS].set(ssd_dt_bias[0]) \
                                    .at[0, DT_B:DT_B + SSD_HEADS].set(ssd_dt_bias[1])
    q, k, v, g_all, z_all, xbc, ld, misc = _inproj_call(
        x, ctx, sh1, sc1, norm1.reshape(1, d), w_cat, wup, bup, dtb)

    xs, bc = _conv_call(xbc, ssd_conv_w.reshape(9, SSD_CONV_DIM), ssd_conv_b.reshape(1, -1), lc + L)

    o_f, o_b = _gla_call(q, k, v, ld, L)

    a_neg = -jnp.exp(ssd_A_log.astype(F32))
    a_f = jnp.zeros((1, LANES), F32).at[0, DT_F:DT_F + SSD_HEADS].set(a_neg[0])
    a_b = jnp.zeros((1, LANES), F32).at[0, DT_B:DT_B + SSD_HEADS].set(a_neg[1])
    dvec = jnp.repeat(ssd_D, SSD_HEADDIM).reshape(1, SSD_INNER)
    y_f, y_b = _ssd_call(xs, bc, misc, a_f, a_b, dvec, L)

    wr = jnp.zeros((d, LANES), F32).at[:, :N_EXPERTS].set(w_router)
    wr_hi = wr.astype(BF16)
    wr_lo = (wr - wr_hi.astype(F32)).astype(BF16)
    br = jnp.full((1, LANES), NEG_BIG, F32).at[0, :N_EXPERTS].set(b_router)
    x1, h2, eidx, gates, counts = _outproj_call(
        o_f, o_b, g_all, y_f, y_b, z_all, x, g1[:bsz], sh2[:bsz], sc2[:bsz],
        jnp.tile(gla_norm, GLA_HEADS).reshape(1, -1), ssd_norm.reshape(1, -1), w_out.astype(BF16),
        norm2.reshape(1, d), wr_hi, wr_lo, br)

    T = bsz * L
    cnt = counts[0, :N_EXPERTS].astype(jnp.int32)
    padded = ((cnt + MOE_BLOCK - 1) // MOE_BLOCK) * MOE_BLOCK
    pend = jnp.cumsum(padded)
    pstart = pend - padded
    max_rows = T * TOP_K + (T // TB) * N_EXPERTS * (RUN - 1)
    n_blocks = -(-max_rows // MOE_BLOCK) + N_EXPERTS
    blk_start = jnp.arange(n_blocks, dtype=jnp.int32) * MOE_BLOCK
    blk_e = jnp.minimum(jnp.sum(pend[None, :] <= blk_start[:, None], axis=1), N_EXPERTS - 1).astype(jnp.int32)
    n_used = (pend[-1] // MOE_BLOCK).astype(jnp.int32)
    blk_info = jnp.concatenate([blk_e, n_used[None]])
    n_tail = n_blocks - (T * TOP_K) // MOE_BLOCK
    tail = n_used + jnp.arange(n_tail, dtype=jnp.int32)
    fill_blocks = jnp.concatenate([
        jnp.where(padded > 0, pend // MOE_BLOCK - 1, -1),
        jnp.where(tail < n_blocks, tail, -1),
        n_blocks + jnp.arange(2 * SPILL_ROWS // MOE_BLOCK, dtype=jnp.int32)]).astype(jnp.int32)
    pstart_row = jnp.zeros((1, LANES), F32).at[0, :N_EXPERTS].set(pstart.astype(F32))
    lp, cd = _pos_call(eidx.reshape(T, LANES), pstart_row)

    xg = _dispatch_call(fill_blocks, cd, h2.reshape(T, d), lp, n_blocks * MOE_BLOCK)
    y = _moe_call(blk_info, xg, w_gate_up, b_gate_up.reshape(N_EXPERTS, 1, -1), w_down,
                  b_down.reshape(N_EXPERTS, 1, -1), n_blocks)
    out = _combine_call(cd, y, lp, x1.reshape(T, d), gates.reshape(T, LANES), g2[:bsz], final_norm.reshape(1, d))
    return out.reshape(bsz, L, d)


def kernel(x, c, ctx, c_ctx, w_mod, b_mod, norm1, w_in, gla_w_gk_up, gla_b_gk, gla_norm, ssd_conv_w, ssd_conv_b, ssd_dt_bias, ssd_A_log, ssd_D, ssd_norm, w_out, norm2, w_router, b_router, w_gate_up, b_gate_up, w_down, b_down, final_norm):
    assert w_mod.shape[0] == 1, "single-layer kernel"
    return _layer(x, c, ctx, c_ctx, w_mod[0], b_mod[0], norm1[0], w_in[0], gla_w_gk_up[0], gla_b_gk[0],
                  gla_norm[0], ssd_conv_w[0], ssd_conv_b[0], ssd_dt_bias[0], ssd_A_log[0], ssd_D[0],
                  ssd_norm[0], w_out[0], norm2[0], w_router[0], b_router[0], w_gate_up[0], b_gate_up[0],
                  w_down[0], b_down[0], final_norm)
```

```python
import functools

import numpy as np
import jax
import jax.numpy as jnp
from jax import lax
from jax.experimental import pallas as pl
from jax.experimental.pallas import tpu as pltpu

F32 = jnp.float32
BF16 = jnp.bfloat16

EPS = 1e-6
GRID_W = 64
GLA_HEADS = 4
GLA_DK = 64
GLA_DV = 128
GLA_QK = GLA_HEADS * GLA_DK
GLA_V = GLA_HEADS * GLA_DV
GLA_RANK = 16
GLA_GATE_NORM = 16.0
SSD_HEADDIM = 64
SSD_INNER = 512
SSD_HEADS = 8
SSD_GROUPS = 2
SSD_HPG = 4
SSD_STATE = 128
SSD_CONV_DIM = 1024
N_EXPERTS = 32
TOP_K = 4
D_FF = 1024
SWIGLU_LIMIT = 7.0
SWIGLU_ALPHA = 1.702
MOE_BLOCK = 512

TB = 256
GLA_C = 64
SSD_C = 128
LANES = 128
EXP_CLAMP = 80.0
DT_F = 16
DT_B = 24
NEG_BIG = -1e30
RUN = 8
RL = 1280
CD_LANES = 256
SPILL_ROWS = -(-RL // MOE_BLOCK) * MOE_BLOCK
D_HALF = 512
VMEM_LIMIT = 56 * 1024 * 1024


def _dot(a, b):
    return jnp.dot(a, b, preferred_element_type=F32)


def _dot_nt(a, b):
    return lax.dot_general(a, b, (((1,), (1,)), ((), ())), preferred_element_type=F32)


def _dot_tn(a, b):
    return lax.dot_general(a, b, (((0,), (0,)), ((), ())), preferred_element_type=F32)


def _split3(a):
    hi = a.astype(BF16)
    r1 = a - hi.astype(F32)
    mid = r1.astype(BF16)
    lo = (r1 - mid.astype(F32)).astype(BF16)
    return hi, mid, lo


def _dot_exact_r(m, a):
    hi, mid, lo = _split3(a)
    return _dot(m, hi) + _dot(m, mid) + _dot(m, lo)


def _dot_hilo_r(m, a):
    hi = a.astype(BF16)
    lo = (a - hi.astype(F32)).astype(BF16)
    return _dot(m, hi) + _dot(m, lo)


def _dot_hilo_l(a, m):
    hi = a.astype(BF16)
    lo = (a - hi.astype(F32)).astype(BF16)
    return _dot(hi, m) + _dot(lo, m)


def _sigmoid(x):
    return 1.0 / (1.0 + jnp.exp(-x))


def _softplus(x):
    return jnp.maximum(x, 0.0) + jnp.log1p(jnp.exp(-jnp.abs(x)))


def _params(*sem):
    return pltpu.CompilerParams(dimension_semantics=sem, vmem_limit_bytes=VMEM_LIMIT)


def _mod_kernel(c_ref, w_ref, b_ref, o_ref):
    c = c_ref[...]
    s = c * _sigmoid(c)
    s_hi = s.astype(BF16)
    s_lo = (s - s_hi.astype(F32)).astype(BF16)
    w = w_ref[...]
    w_hi = w.astype(BF16)
    w_lo = (w - w_hi.astype(F32)).astype(BF16)
    o_ref[...] = _dot(s_hi, w_hi) + _dot(s_lo, w_hi) + _dot(s_hi, w_lo) + b_ref[...]


def _mod_call(cin, w, b):
    rows, d = cin.shape
    n = w.shape[1]
    tn = 1536
    return pl.pallas_call(
        _mod_kernel,
        grid=(n // tn,),
        in_specs=[pl.BlockSpec((rows, d), lambda i: (0, 0)),
                  pl.BlockSpec((d, tn), lambda i: (0, i)),
                  pl.BlockSpec((1, tn), lambda i: (0, i))],
        out_specs=pl.BlockSpec((rows, tn), lambda i: (0, i)),
        out_shape=jax.ShapeDtypeStruct((rows, n), F32),
        compiler_params=_params("arbitrary"),
    )(cin, w, b)


_C_Q, _C_K, _C_V, _C_G, _C_Z, _C_X, _C_M, _C_END = 0, 256, 512, 1024, 1536, 2048, 3072, 3200


def _inproj_kernel(xa_ref, xb_ref, ctx_ref, shl_ref, scl_ref, shc_ref, scc_ref, n1_ref, w_ref, wup_ref, bup_ref,
                   dtb_ref, q_ref, k_ref, v_ref, g_ref, z_ref, xbc_ref, ld_ref, misc_ref, h_scr):
    j = pl.program_id(1)

    def normmod(xv, sh_ref, sc_ref):
        ms = jnp.mean(xv * xv, axis=-1, keepdims=True)
        y = xv * lax.rsqrt(ms + EPS) * n1_ref[...]
        return (y * (1.0 + sc_ref[0]) + sh_ref[0]).astype(BF16)

    @pl.when(j == 0)
    def _():
        h_scr[0:TB, :] = normmod(ctx_ref[0], shc_ref, scc_ref)

    @pl.when(j > 0)
    def _():
        h_scr[0:TB, :] = normmod(xa_ref[0], shl_ref, scl_ref)

    h_scr[TB:2 * TB, :] = normmod(xb_ref[0], shl_ref, scl_ref)
    h = h_scr[...]

    def mm(lo, hi):
        return _dot(h, w_ref[:, lo:hi])

    q_ref[0] = (mm(_C_Q, _C_K) * (GLA_DK ** -0.5)).astype(BF16)
    k_ref[0] = mm(_C_K, _C_V).astype(BF16)
    v_ref[0] = mm(_C_V, _C_G).astype(BF16)
    g_ref[0] = mm(_C_G, _C_Z)
    z_ref[0] = mm(_C_Z, _C_X)
    xbc_ref[0] = mm(_C_X, _C_M)
    m = mm(_C_M, _C_END)
    zz = _dot(m.astype(BF16), wup_ref[...]) + bup_ref[...]
    ld_ref[0] = -_softplus(-zz) * (1.0 / GLA_GATE_NORM)
    misc_ref[0] = _softplus(m + dtb_ref[...])


def _inproj_call(x, ctx, sh1, sc1, n1, w_cat, wup, bup, dtb):
    bsz, L, d = x.shape
    nx = L // TB
    nj = -(-(nx + 1) // 2)
    tok = lambda n: pl.BlockSpec((1, 2 * TB, n), lambda b, j: (b, j, 0))
    const = lambda a: pl.BlockSpec(a.shape, lambda b, j: (0,) * a.ndim)
    xblk = lambda off: pl.BlockSpec((1, TB, d), lambda b, j: (b, jnp.clip(2 * j + off, 0, nx - 1), 0))
    mod_lat = pl.BlockSpec((1, 1, d), lambda b, j: (b, 0, 0))
    mod_ctx = pl.BlockSpec((1, 1, d), lambda b, j: (bsz, 0, 0))
    outs = [(GLA_QK, BF16), (GLA_QK, BF16), (GLA_V, BF16), (GLA_V, F32), (SSD_INNER, F32),
            (SSD_CONV_DIM, F32), (2 * GLA_QK, F32), (LANES, F32)]
    return pl.pallas_call(
        _inproj_kernel,
        grid=(bsz, nj),
        in_specs=[xblk(-1), xblk(0), pl.BlockSpec((1, TB, d), lambda b, j: (b, 0, 0)),
                  mod_lat, mod_lat, mod_ctx, mod_ctx,
                  const(n1), const(w_cat), const(wup), const(bup), const(dtb)],
        out_specs=[tok(n) for n, _ in outs],
        out_shape=[jax.ShapeDtypeStruct((bsz, nj * 2 * TB, n), dt) for n, dt in outs],
        scratch_shapes=[pltpu.VMEM((2 * TB, d), BF16)],
        compiler_params=_params("arbitrary", "arbitrary"),
    )(x, x, ctx, sh1, sc1, sh1, sc1, n1, w_cat, wup, bup, dtb)


_EXT_PAD = 8
_EXT_BASE = _EXT_PAD + GRID_W
_EXT_ROWS = 2 * _EXT_PAD + 2 * GRID_W + TB


def _conv_kernel(prev_ref, cur_ref, next_ref, w_ref, b_ref, xs_ref, bc_ref, ext):
    j = pl.program_id(1)
    nj = pl.num_programs(1)
    is_ctx = j == 0
    zpad = jnp.zeros((_EXT_PAD, SSD_CONV_DIM), F32)
    ext[0:_EXT_PAD, :] = zpad
    ext[_EXT_ROWS - _EXT_PAD:_EXT_ROWS, :] = zpad
    ext[_EXT_PAD:_EXT_BASE, :] = jnp.where(j >= 2, prev_ref[0], 0.0)
    ext[_EXT_BASE:_EXT_BASE + TB, :] = cur_ref[0]
    ext[_EXT_BASE + TB:_EXT_BASE + TB + GRID_W, :] = jnp.where(
        jnp.logical_and(j >= 1, j <= nj - 2), next_ref[0], 0.0)

    t = lax.broadcasted_iota(jnp.int32, (TB, LANES), 0)
    col = t & (GRID_W - 1)
    ok_l = jnp.where(is_ctx, t, col) >= 1
    ok_r = jnp.where(is_ctx, t - (TB - GRID_W), col) <= GRID_W - 2
    lat = jnp.where(is_ctx, 0.0, 1.0)

    for c in range(SSD_CONV_DIM // LANES):
        lo, hi = c * LANES, (c + 1) * LANES
        acc = jnp.zeros((TB, LANES), F32)
        for dr in (-1, 0, 1):
            for dc in (-1, 0, 1):
                start = _EXT_BASE + GRID_W * dr + dc
                tap = ext[start:start + TB, lo:hi]
                wi = 3 * (dr + 1) + (dc + 1)
                wv = w_ref[wi:wi + 1, lo:hi]
                if dr != 0:
                    wv = wv * lat
                if dc == -1:
                    tap = jnp.where(ok_l, tap, 0.0)
                elif dc == 1:
                    tap = jnp.where(ok_r, tap, 0.0)
                acc = acc + tap * wv
        y = acc + b_ref[:, lo:hi]
        y = y * _sigmoid(y)
        if c < SSD_INNER // LANES:
            xs_ref[0, :, lo:hi] = y
        else:
            bc_ref[0, :, lo - SSD_INNER:hi - SSD_INNER] = y.astype(BF16)


def _conv_call(xbc, w9, bias, ls):
    bsz, _, ch = xbc.shape
    nj = ls // TB
    rpb = TB // GRID_W
    nrow = ls // GRID_W
    return pl.pallas_call(
        _conv_kernel,
        grid=(bsz, nj),
        in_specs=[pl.BlockSpec((1, GRID_W, ch), lambda b, j: (b, jnp.maximum(rpb * j - 1, 0), 0)),
                  pl.BlockSpec((1, TB, ch), lambda b, j: (b, j, 0)),
                  pl.BlockSpec((1, GRID_W, ch), lambda b, j: (b, jnp.minimum(rpb * j + rpb, nrow - 1), 0)),
                  pl.BlockSpec((9, ch), lambda b, j: (0, 0)),
                  pl.BlockSpec((1, ch), lambda b, j: (0, 0))],
        out_specs=[pl.BlockSpec((1, TB, SSD_INNER), lambda b, j: (b, j, 0)),
                   pl.BlockSpec((1, TB, ch - SSD_INNER), lambda b, j: (b, j, 0))],
        out_shape=[jax.ShapeDtypeStruct((bsz, ls, SSD_INNER), F32),
                   jax.ShapeDtypeStruct((bsz, ls, ch - SSD_INNER), BF16)],
        scratch_shapes=[pltpu.VMEM((_EXT_ROWS, ch), F32)],
        compiler_params=_params("arbitrary", "arbitrary"),
    )(xbc, xbc, xbc, w9, bias)


def _fwd_blk(s):
    return s


def _bwd_blk(s, ns):
    return jnp.where(s == 0, 0, ns - s)


def _scan_batch(bsz):
    return 2 if bsz % 2 == 0 else 1


def _gla_dir(q, k, v, la, st_ref, tri_m, fwd):
    C = GLA_C
    b = _dot_hilo_r(tri_m, la)
    bt = b[C - 1:C, :] if fwd else b[0:1, :]
    r = 0.5 * bt
    er = jnp.exp(r)
    qt = (q.astype(F32) * jnp.exp(jnp.minimum(b - r, EXP_CLAMP))).astype(BF16)
    kt = (k.astype(F32) * jnp.exp(jnp.minimum(r - b, EXP_CLAMP))).astype(BF16)
    head_k = lax.broadcasted_iota(jnp.int32, (C, GLA_QK), 1) >> 6
    zero = jnp.zeros_like(kt)
    kh = [jnp.where(head_k == h, kt, zero) for h in range(GLA_HEADS)]
    qh = [jnp.where(head_k == h, qt, zero) for h in range(GLA_HEADS)]
    sc = _dot_nt(qt, jnp.concatenate(kh, axis=0))
    ii = lax.broadcasted_iota(jnp.int32, (C, GLA_HEADS * C), 0)
    jj = lax.broadcasted_iota(jnp.int32, (C, GLA_HEADS * C), 1) & (C - 1)
    causal = (jj <= ii) if fwd else (jj >= ii)
    p = jnp.where(causal, sc, 0.0).astype(BF16)
    head_v = lax.broadcasted_iota(jnp.int32, (C, GLA_V), 1) >> 7
    vst = jnp.concatenate([jnp.where(head_v == h, v, jnp.zeros_like(v)) for h in range(GLA_HEADS)], axis=0)
    o = _dot(p, vst)
    st = st_ref[...]
    ster = (st * er).astype(BF16)
    o = o + jnp.concatenate([_dot_nt(qh[h], ster) for h in range(GLA_HEADS)], axis=1)
    u = _dot_tn(v[:, 0:GLA_DV], kh[0])
    for h in range(1, GLA_HEADS):
        u = u + _dot_tn(v[:, h * GLA_DV:(h + 1) * GLA_DV], kh[h])
    st_ref[...] = st * jnp.exp(bt) + u * er
    return o


def _gla_kernel(qf_ref, kf_ref, vf_ref, lf_ref, qb_ref, kb_ref, vb_ref, lb_ref, trif_ref, trib_ref,
                of_ref, ob_ref, stf, stb):
    s = pl.program_id(1)

    @pl.when(s == 0)
    def _():
        stf[...] = jnp.zeros_like(stf)
        stb[...] = jnp.zeros_like(stb)

    nsub = TB // GLA_C

    for i in range(nsub):
        sf = pl.ds(i * GLA_C, GLA_C)
        sb = pl.ds((nsub - 1 - i) * GLA_C, GLA_C)
        for bb in range(qf_ref.shape[0]):
            of_ref[bb, sf, :] = _gla_dir(qf_ref[bb, sf, :], kf_ref[bb, sf, :], vf_ref[bb, sf, :], lf_ref[bb, sf, :],
                                         stf.at[bb], trif_ref[...], True)
            ob_ref[bb, sb, :] = _gla_dir(qb_ref[bb, sb, :], kb_ref[bb, sb, :], vb_ref[bb, sb, :], lb_ref[bb, sb, :],
                                         stb.at[bb], trib_ref[...], False)


def _gla_call(q, k, v, ld, L):
    bsz = q.shape[0]
    nx = L // TB
    ns = nx + 1
    trif = jnp.asarray(np.tril(np.ones((GLA_C, GLA_C), np.float32)), BF16)
    trib = jnp.asarray(np.triu(np.ones((GLA_C, GLA_C), np.float32)), BF16)
    nb = _scan_batch(bsz)
    f = lambda n, lane=0: pl.BlockSpec((nb, TB, n), lambda b, s: (b, _fwd_blk(s), lane))
    r = lambda n, lane=0: pl.BlockSpec((nb, TB, n), lambda b, s: (b, _bwd_blk(s, ns), lane))
    tri = pl.BlockSpec((GLA_C, GLA_C), lambda b, s: (0, 0))
    return pl.pallas_call(
        _gla_kernel,
        grid=(bsz // nb, ns),
        in_specs=[f(GLA_QK), f(GLA_QK), f(GLA_V), f(GLA_QK, 0),
                  r(GLA_QK), r(GLA_QK), r(GLA_V), r(GLA_QK, 1), tri, tri],
        out_specs=[pl.BlockSpec((nb, TB, GLA_V), lambda b, s: (b, jnp.maximum(s - 1, 0), 0)),
                   pl.BlockSpec((nb, TB, GLA_V), lambda b, s: (b, jnp.where(s == 0, nx - 1, nx - s), 0))],
        out_shape=[jax.ShapeDtypeStruct((bsz, L, GLA_V), F32)] * 2,
        scratch_shapes=[pltpu.VMEM((nb, GLA_DV, GLA_QK), F32)] * 2,
        compiler_params=_params("arbitrary", "arbitrary"),
    )(q, k, v, ld, q, k, v, ld, trif, trib)


def _ssd_dir(xs, bc, dtm, avec, dvec, st_ref, tri_m, e_m, base, fwd):
    C = SSD_C
    dt_exp = _dot_hilo_l(dtm, e_m)
    a = dtm * avec
    acum = _dot_exact_r(tri_m, a)
    acum_exp = _dot_hilo_l(acum, e_m)
    acum_t = acum.T
    xdt = xs * dt_exp
    xdt_b = xdt.astype(BF16)
    ii = lax.broadcasted_iota(jnp.int32, (C, C), 0)
    jj = lax.broadcasted_iota(jnp.int32, (C, C), 1)
    tri = (jj <= ii) if fwd else (jj >= ii)
    al_exp = acum_exp[C - 1:C, :] if fwd else acum_exp[0:1, :]
    head2 = lax.broadcasted_iota(jnp.int32, (C, 256), 1) >> 6
    ys = []
    for g in range(SSD_GROUPS):
        gl, gh = 256 * g, 256 * (g + 1)
        bg = bc[:, 128 * g:128 * (g + 1)]
        cg = bc[:, 256 + 128 * g:256 + 128 * (g + 1)]
        cb = _dot_nt(cg, bg)
        ms = []
        for rr in range(SSD_HPG):
            ln = base + SSD_HPG * g + rr
            diff = acum[:, ln:ln + 1] - acum_t[ln:ln + 1, :]
            seg = jnp.where(tri, jnp.exp(jnp.minimum(diff, 0.0)), 0.0)
            ms.append((cb * seg).astype(BF16))
        res = _dot(jnp.concatenate(ms, axis=0), xdt_b[:, gl:gh])
        yg = jnp.zeros((C, 256), F32)
        for rr in range(SSD_HPG):
            yg = yg + jnp.where(head2 == rr, res[rr * C:(rr + 1) * C, :], 0.0)
        sg = st_ref[g]
        yoff = _dot(cg, sg.astype(BF16)) * jnp.exp(acum_exp[:, gl:gh])
        wst = jnp.exp(al_exp[:, gl:gh] - acum_exp[:, gl:gh])
        ug = _dot_tn(bg, (xdt[:, gl:gh] * wst).astype(BF16))
        st_ref[g] = sg * jnp.exp(al_exp[:, gl:gh]) + ug
        ys.append(yg + yoff)
    y = jnp.concatenate(ys, axis=1)
    if dvec is not None:
        y = y + dvec * xs
    return y


def _ssd_kernel(xf_ref, bcf_ref, mf_ref, xb_ref, bcb_ref, mb_ref, af_ref, ab_ref, d_ref,
                trif_ref, trib_ref, ef_ref, eb_ref, yf_ref, yb_ref, stf, stb):
    s = pl.program_id(1)

    @pl.when(s == 0)
    def _():
        stf[...] = jnp.zeros_like(stf)
        stb[...] = jnp.zeros_like(stb)

    nsub = TB // SSD_C

    for i in range(nsub):
        sf = pl.ds(i * SSD_C, SSD_C)
        sb = pl.ds((nsub - 1 - i) * SSD_C, SSD_C)
        for bb in range(xf_ref.shape[0]):
            yf_ref[bb, sf, :] = _ssd_dir(xf_ref[bb, sf, :], bcf_ref[bb, sf, :], mf_ref[bb, sf, :], af_ref[...],
                                         d_ref[...], stf.at[bb], trif_ref[...], ef_ref[...], DT_F, True)
            yb_ref[bb, sb, :] = _ssd_dir(xb_ref[bb, sb, :], bcb_ref[bb, sb, :], mb_ref[bb, sb, :], ab_ref[...],
                                         None, stb.at[bb], trib_ref[...], eb_ref[...], DT_B, False)


def _expand_matrix(base):
    e = np.zeros((LANES, SSD_INNER), np.float32)
    for h in range(SSD_HEADS):
        e[base + h, SSD_HEADDIM * h:SSD_HEADDIM * (h + 1)] = 1.0
    return jnp.asarray(e, BF16)


def _ssd_call(xs, bc, misc, a_f, a_b, dvec, L):
    bsz, ls, _ = xs.shape
    ns = ls // TB
    nx = L // TB
    trif = jnp.asarray(np.tril(np.ones((SSD_C, SSD_C), np.float32)), BF16)
    trib = jnp.asarray(np.triu(np.ones((SSD_C, SSD_C), np.float32)), BF16)
    ef, eb = _expand_matrix(DT_F), _expand_matrix(DT_B)
    nb = _scan_batch(bsz)
    f = lambda n: pl.BlockSpec((nb, TB, n), lambda b, s: (b, _fwd_blk(s), 0))
    r = lambda n: pl.BlockSpec((nb, TB, n), lambda b, s: (b, _bwd_blk(s, ns), 0))
    const = lambda a: pl.BlockSpec(a.shape, lambda b, s: (0,) * a.ndim)
    return pl.pallas_call(
        _ssd_kernel,
        grid=(bsz // nb, ns),
        in_specs=[f(SSD_INNER), f(512), f(LANES), r(SSD_INNER), r(512), r(LANES),
                  const(a_f), const(a_b), const(dvec), const(trif), const(trib), const(ef), const(eb)],
        out_specs=[pl.BlockSpec((nb, TB, SSD_INNER), lambda b, s: (b, jnp.maximum(s - 1, 0), 0)),
                   pl.BlockSpec((nb, TB, SSD_INNER), lambda b, s: (b, jnp.where(s == 0, nx - 1, nx - s), 0))],
        out_shape=[jax.ShapeDtypeStruct((bsz, L, SSD_INNER), F32)] * 2,
        scratch_shapes=[pltpu.VMEM((nb, SSD_GROUPS, SSD_STATE, SSD_HPG * SSD_HEADDIM), F32)] * 2,
        compiler_params=_params("arbitrary", "arbitrary"),
    )(xs, bc, misc, xs, bc, misc, a_f, a_b, dvec, trif, trib, ef, eb)


def _outproj_kernel(of_ref, ob_ref, g_ref, yf_ref, yb_ref, z_ref, x_ref, g1_ref, sh2_ref, sc2_ref,
                    gn_ref, sn_ref, wo_ref, n2_ref, wrh_ref, wrl_ref, br_ref,
                    x1_ref, h2_ref, eidx_ref, gate_ref, cnt_ref):
    first = jnp.logical_and(pl.program_id(0) == 0, pl.program_id(1) == 0)

    @pl.when(first)
    def _():
        cnt_ref[...] = jnp.zeros_like(cnt_ref)

    o = of_ref[0] + ob_ref[0]
    gg = g_ref[0]
    parts = []
    for h in range(GLA_HEADS):
        lo, hi = GLA_DV * h, GLA_DV * (h + 1)
        oh = o[:, lo:hi]
        ms = jnp.mean(oh * oh, axis=-1, keepdims=True)
        gh = gg[:, lo:hi]
        parts.append((oh * lax.rsqrt(ms + EPS) * gn_ref[:, lo:hi] * (gh * _sigmoid(gh))).astype(BF16))
    zz = z_ref[0]
    u = (yf_ref[0] + yb_ref[0]) * (zz * _sigmoid(zz))
    gw = SSD_INNER // SSD_GROUPS
    for g in range(SSD_GROUPS):
        lo, hi = gw * g, gw * (g + 1)
        ug = u[:, lo:hi]
        ms = jnp.mean(ug * ug, axis=-1, keepdims=True)
        parts.append((ug * lax.rsqrt(ms + EPS) * sn_ref[:, lo:hi]).astype(BF16))
    mix = jnp.concatenate(parts, axis=1)
    x1 = x_ref[0] + g1_ref[0] * _dot(mix, wo_ref[...])
    x1_ref[0] = x1
    ms = jnp.mean(x1 * x1, axis=-1, keepdims=True)
    h2 = (x1 * lax.rsqrt(ms + EPS) * n2_ref[...]) * (1.0 + sc2_ref[0]) + sh2_ref[0]
    h_hi = h2.astype(BF16)
    h2_ref[0] = h_hi
    h_lo = (h2 - h_hi.astype(F32)).astype(BF16)
    logits = (_dot(h_hi, wrh_ref[...]) + _dot(h_lo, wrh_ref[...]) + _dot(h_hi, wrl_ref[...])) + br_ref[...]
    lane = lax.broadcasted_iota(jnp.int32, (TB, LANES), 1).astype(F32)
    work = logits
    eidx = jnp.full((TB, LANES), -1.0, F32)
    gates = jnp.zeros((TB, LANES), F32)
    sel = jnp.zeros((TB, LANES), F32)
    m0 = None
    den = jnp.zeros((TB, 1), F32)
    for kk in range(TOP_K):
        m = jnp.max(work, axis=-1, keepdims=True)
        idx = jnp.min(jnp.where(work == m, lane, float(LANES)), axis=-1, keepdims=True)
        hit = lane == idx
        if m0 is None:
            m0 = m
        e = jnp.exp(m - m0)
        den = den + e
        eidx = jnp.where(lane == float(kk), idx, eidx)
        gates = jnp.where(lane == float(kk), e, gates)
        sel = jnp.where(hit, 1.0, sel)
        work = jnp.where(hit, NEG_BIG, work)
    eidx_ref[0] = eidx.astype(jnp.int32)
    gate_ref[0] = gates / den
    cnt = jnp.sum(sel, axis=0, keepdims=True)
    cnt_ref[...] += jnp.floor((cnt + (RUN - 1.0)) * (1.0 / RUN)) * RUN


def _outproj_call(o_f, o_b, g_all, y_f, y_b, z_all, x, g1, sh2, sc2, gn, sn, wo, n2, wr_hi, wr_lo, br):
    bsz, L, d = x.shape
    nj = L // TB
    tok = lambda n: pl.BlockSpec((1, TB, n), lambda b, j: (b, j, 0))
    tok_off = lambda n: pl.BlockSpec((1, TB, n), lambda b, j: (b, j + 1, 0))
    const = lambda a: pl.BlockSpec(a.shape, lambda b, j: (0,) * a.ndim)
    mod = pl.BlockSpec((1, 1, d), lambda b, j: (b, 0, 0))
    return pl.pallas_call(
        _outproj_kernel,
        grid=(bsz, nj),
        in_specs=[tok(GLA_V), tok(GLA_V), tok_off(GLA_V), tok(SSD_INNER), tok(SSD_INNER), tok_off(SSD_INNER),
                  tok(d), mod, mod, mod, const(gn), const(sn), const(wo), const(n2),
                  const(wr_hi), const(wr_lo), const(br)],
        out_specs=[tok(d), tok(d), tok(LANES), tok(LANES), pl.BlockSpec((1, LANES), lambda b, j: (0, 0))],
        out_shape=[jax.ShapeDtypeStruct((bsz, L, d), F32), jax.ShapeDtypeStruct((bsz, L, d), BF16),
                   jax.ShapeDtypeStruct((bsz, L, LANES), jnp.int32), jax.ShapeDtypeStruct((bsz, L, LANES), F32),
                   jax.ShapeDtypeStruct((1, LANES), F32)],
        compiler_params=_params("arbitrary", "arbitrary"),
    )(o_f, o_b, g_all, y_f, y_b, z_all, x, g1, sh2, sc2, gn, sn, wo, n2, wr_hi, wr_lo, br)


def _pos_kernel(eidx_ref, pstart_ref, lst_ref, ust_ref, lp_ref, cd_ref, carry):
    @pl.when(pl.program_id(0) == 0)
    def _():
        carry[...] = pstart_ref[...]

    eidx = eidx_ref[...]
    lane = lax.broadcasted_iota(jnp.int32, (TB, LANES), 1)
    hits = [lane == eidx[:, kk:kk + 1] for kk in range(TOP_K)]
    sel = jnp.zeros((TB, LANES), F32)
    for hmask in hits:
        sel = jnp.where(hmask, 1.0, sel)
    cnt = jnp.sum(sel, axis=0, keepdims=True)
    run = jnp.floor((cnt + (RUN - 1.0)) * (1.0 / RUN)) * RUN
    rank = _dot(lst_ref[...], sel.astype(BF16))
    loff = _dot(jnp.broadcast_to(run, (8, LANES)).astype(BF16), ust_ref[...])[0:1]
    pos = loff + rank
    lp = jnp.zeros((TB, LANES), jnp.int32)
    for kk, hmask in enumerate(hits):
        lk = jnp.sum(jnp.where(hmask, pos, 0.0), axis=-1, keepdims=True)
        lp = jnp.where(lane == kk, lk.astype(jnp.int32), lp)
    lp_ref[...] = lp

    base = carry[...]
    eye = lax.broadcasted_iota(jnp.int32, (LANES, LANES), 0) == lax.broadcasted_iota(jnp.int32, (LANES, LANES), 1)
    col = lambda v: jnp.sum(jnp.where(eye, v, 0.0), axis=1, keepdims=True)
    loff_c, run_c, shift_c = col(loff), col(run), col(base - loff)
    row0 = (lax.broadcasted_iota(jnp.int32, (LANES, CD_LANES), 1) * RUN).astype(F32)
    inside = jnp.where(row0 >= loff_c, 1.0, 0.0) * jnp.where(row0 < loff_c + run_c, 1.0, 0.0)
    valid = jnp.sum(inside, axis=0, keepdims=True)
    dest = jnp.sum(inside * shift_c, axis=0, keepdims=True) + row0[0:1]
    cd_ref[0] = jnp.where(valid > 0.0, dest, -1.0).astype(jnp.int32)
    carry[...] = base + run


def _pos_call(eidx, pstart):
    T = eidx.shape[0]
    lst = jnp.asarray(np.tril(np.ones((TB, TB), np.float32), -1), BF16)
    ust = jnp.asarray(np.triu(np.ones((LANES, LANES), np.float32), 1), BF16)
    return pl.pallas_call(
        _pos_kernel,
        grid=(T // TB,),
        in_specs=[pl.BlockSpec((TB, LANES), lambda i: (i, 0)),
                  pl.BlockSpec((1, LANES), lambda i: (0, 0)),
                  pl.BlockSpec((TB, TB), lambda i: (0, 0)),
                  pl.BlockSpec((LANES, LANES), lambda i: (0, 0))],
        out_specs=[pl.BlockSpec((TB, LANES), lambda i: (i, 0)),
                   pl.BlockSpec((1, 1, CD_LANES), lambda i: (i, 0, 0))],
        out_shape=[jax.ShapeDtypeStruct((T, LANES), jnp.int32),
                   jax.ShapeDtypeStruct((T // TB, 1, CD_LANES), jnp.int32)],
        scratch_shapes=[pltpu.VMEM((1, LANES), F32)],
        compiler_params=_params("arbitrary"),
    )(eidx, pstart, lst, ust)


def _pair_matrix(lp, weights):
    j = lax.broadcasted_iota(jnp.int32, (TB, RL), 1)
    m = jnp.zeros((TB, RL), F32)
    for kk in range(TOP_K):
        w = 1.0 if weights is None else weights[:, kk:kk + 1]
        m = m + jnp.where(j == lp[:, kk:kk + 1], w, 0.0)
    return m


def _pack_bf16_pairs(v):
    bits = pltpu.bitcast(v.astype(BF16).astype(F32), jnp.uint32)
    return bits[:, D_HALF:] | (bits[:, :D_HALF] >> 16)


def _unpack_bf16_pairs(w):
    lo = pltpu.bitcast(w << 16, F32).astype(BF16)
    hi = pltpu.bitcast(w & jnp.uint32(0xFFFF0000), F32).astype(BF16)
    return lo, hi


def _dispatch_kernel(fill_ref, cd_ref, h_ref, lp_ref, xg_ref, sorted_s, sems, fill_sem):
    i = pl.program_id(0)
    n = pl.num_programs(0)
    slot = i & 1
    sorted_s[slot] = _pack_bf16_pairs(_dot_tn(_pair_matrix(lp_ref[...], None).astype(BF16), h_ref[...]))

    @pl.when(i == 0)
    def _():
        def fill(k):
            blk = pl.multiple_of(fill_ref[k] * MOE_BLOCK, MOE_BLOCK)
            return pltpu.make_async_copy(sorted_s.at[0, pl.ds(0, MOE_BLOCK), :],
                                         xg_ref.at[pl.ds(blk, MOE_BLOCK), :], fill_sem)

        for k in range(fill_ref.shape[0]):
            pl.when(fill_ref[k] >= 0)(lambda k=k: fill(k).start())
        for k in range(fill_ref.shape[0]):
            pl.when(fill_ref[k] >= 0)(lambda k=k: fill(k).wait())

    spill = xg_ref.shape[0] - (2 - slot) * SPILL_ROWS

    def body(c, carry):
        d = cd_ref[0, 0, c]
        d = pl.multiple_of(jnp.where(d < 0, spill + RUN * c, d), RUN)
        src = pl.multiple_of(RUN * c, RUN)
        pltpu.make_async_copy(sorted_s.at[slot, pl.ds(src, RUN), :], xg_ref.at[pl.ds(d, RUN), :],
                              sems.at[slot]).start()
        return carry

    lax.fori_loop(0, RL // RUN, body, 0, unroll=4)

    def drain(s):
        pltpu.make_async_copy(sorted_s.at[s], xg_ref.at[pl.ds(0, RL), :], sems.at[s]).wait()

    pl.when(i > 0)(lambda: drain(1 - slot))
    pl.when(i == n - 1)(lambda: drain(slot))


def _dispatch_call(fill_blocks, cd, h2, lp, P):
    T, d = h2.shape
    grid_spec = pltpu.PrefetchScalarGridSpec(
        num_scalar_prefetch=1,
        grid=(T // TB,),
        in_specs=[pl.BlockSpec((1, 1, CD_LANES), lambda i, fb: (i, 0, 0), memory_space=pltpu.SMEM),
                  pl.BlockSpec((TB, d), lambda i, fb: (i, 0)),
                  pl.BlockSpec((TB, LANES), lambda i, fb: (i, 0))],
        out_specs=pl.BlockSpec(memory_space=pl.ANY),
        scratch_shapes=[pltpu.VMEM((2, RL, D_HALF), jnp.uint32), pltpu.SemaphoreType.DMA((2,)),
                        pltpu.SemaphoreType.DMA(())],
    )
    return pl.pallas_call(
        _dispatch_kernel,
        grid_spec=grid_spec,
        out_shape=jax.ShapeDtypeStruct((P + 2 * SPILL_ROWS, D_HALF), jnp.uint32),
        compiler_params=_params("arbitrary"),
    )(fill_blocks, cd, h2, lp)


def _moe_kernel(be_ref, x_ref, wgu_ref, bgu_ref, wd_ref, bd_ref, y_ref, wgu_b, wd_b, act_s):
    i = pl.program_id(0)
    n_used = be_ref[pl.num_programs(0)]
    changed = jnp.logical_or(i == 0, be_ref[i] != be_ref[jnp.maximum(i - 1, 0)])

    @pl.when(changed)
    def _():
        wgu_b[...] = wgu_ref[0].astype(BF16)
        wd_b[...] = wd_ref[0].astype(BF16)

    @pl.when(i < n_used)
    def _():
        xb = jnp.concatenate(_unpack_bf16_pairs(x_ref[...]), axis=1)
        cw = 256
        for c in range(D_FF // cw):
            lo, hi = c * cw, (c + 1) * cw
            gate = _dot(xb, wgu_b[:, lo:hi]) + bgu_ref[0, :, lo:hi]
            up = _dot(xb, wgu_b[:, D_FF + lo:D_FF + hi]) + bgu_ref[0, :, D_FF + lo:D_FF + hi]
            gate = jnp.minimum(gate, SWIGLU_LIMIT)
            up = jnp.clip(up, -SWIGLU_LIMIT, SWIGLU_LIMIT)
            act_s[:, lo:hi] = ((up + 1.0) * (gate * _sigmoid(SWIGLU_ALPHA * gate))).astype(BF16)
        y_ref[...] = _pack_bf16_pairs(_dot(act_s[...], wd_b[...]) + bd_ref[0])

    @pl.when(i >= n_used)
    def _():
        y_ref[...] = jnp.zeros_like(y_ref)


def _moe_call(blk_e, xg, wgu, bgu, wd, bd, nb):
    P = nb * MOE_BLOCK
    ne, d, f2 = wgu.shape
    grid_spec = pltpu.PrefetchScalarGridSpec(
        num_scalar_prefetch=1,
        grid=(nb,),
        in_specs=[pl.BlockSpec((MOE_BLOCK, D_HALF), lambda i, be: (jnp.minimum(i, be[nb] - 1), 0)),
                  pl.BlockSpec((1, d, f2), lambda i, be: (be[i], 0, 0)),
                  pl.BlockSpec((1, 1, f2), lambda i, be: (be[i], 0, 0)),
                  pl.BlockSpec((1, f2 // 2, d), lambda i, be: (be[i], 0, 0)),
                  pl.BlockSpec((1, 1, d), lambda i, be: (be[i], 0, 0))],
        out_specs=pl.BlockSpec((MOE_BLOCK, D_HALF), lambda i, be: (i, 0)),
        scratch_shapes=[pltpu.VMEM((d, f2), BF16), pltpu.VMEM((f2 // 2, d), BF16),
                        pltpu.VMEM((MOE_BLOCK, f2 // 2), BF16)],
    )
    return pl.pallas_call(
        _moe_kernel,
        grid_spec=grid_spec,
        out_shape=jax.ShapeDtypeStruct((P, D_HALF), jnp.uint32),
        compiler_params=_params("arbitrary"),
    )(blk_e, xg, wgu, bgu, wd, bd)


def _combine_kernel(dcur_ref, dnext_ref, y_ref, lp_ref, x1_ref, gate_ref, g2_ref, fn_ref, o_ref, buf, sems):
    i = pl.program_id(0)
    n = pl.num_programs(0)
    slot = i & 1

    def gather(dref, s):
        def body(c, carry):
            d = pl.multiple_of(jnp.maximum(dref[0, 0, c], 0), RUN)
            dst = pl.multiple_of(RUN * c, RUN)
            pltpu.make_async_copy(y_ref.at[pl.ds(d, RUN), :], buf.at[s, pl.ds(dst, RUN), :], sems.at[s]).start()
            return carry

        lax.fori_loop(0, RL // RUN, body, 0, unroll=4)

    @pl.when(i == 0)
    def _():
        gather(dcur_ref, 0)

    @pl.when(i + 1 < n)
    def _():
        gather(dnext_ref, 1 - slot)

    pltpu.make_async_copy(y_ref.at[pl.ds(0, RL), :], buf.at[slot], sems.at[slot]).wait()

    g = _pair_matrix(lp_ref[...], gate_ref[...]).astype(BF16)
    halves = [_dot(g, yb) for yb in _unpack_bf16_pairs(buf[slot])]
    xo = x1_ref[...] + g2_ref[0] * jnp.concatenate(halves, axis=1)
    ms = jnp.mean(xo * xo, axis=-1, keepdims=True)
    o_ref[...] = xo * lax.rsqrt(ms + EPS) * fn_ref[...]


def _combine_call(cd, y, lp, x1, gates, g2, fn):
    T, d = x1.shape
    n = T // TB
    per_batch = n // g2.shape[0]
    return pl.pallas_call(
        _combine_kernel,
        grid=(n,),
        in_specs=[pl.BlockSpec((1, 1, CD_LANES), lambda i: (i, 0, 0), memory_space=pltpu.SMEM),
                  pl.BlockSpec((1, 1, CD_LANES), lambda i: (jnp.minimum(i + 1, n - 1), 0, 0),
                               memory_space=pltpu.SMEM),
                  pl.BlockSpec(memory_space=pl.ANY),
                  pl.BlockSpec((TB, LANES), lambda i: (i, 0)),
                  pl.BlockSpec((TB, d), lambda i: (i, 0)),
                  pl.BlockSpec((TB, LANES), lambda i: (i, 0)),
                  pl.BlockSpec((1, 1, d), lambda i: (i // per_batch, 0, 0)),
                  pl.BlockSpec((1, d), lambda i: (0, 0))],
        out_specs=pl.BlockSpec((TB, d), lambda i: (i, 0)),
        out_shape=jax.ShapeDtypeStruct((T, d), F32),
        scratch_shapes=[pltpu.VMEM((2, RL, D_HALF), jnp.uint32), pltpu.SemaphoreType.DMA((2,))],
        compiler_params=_params("arbitrary"),
    )(cd, cd, y, lp, x1, gates, g2, fn)


def _layer(x, c, ctx, c_ctx, w_mod, b_mod, norm1, w_in, gla_w_gk_up, gla_b_gk, gla_norm,
           ssd_conv_w, ssd_conv_b, ssd_dt_bias, ssd_A_log, ssd_D, ssd_norm, w_out,
           norm2, w_router, b_router, w_gate_up, b_gate_up, w_down, b_down, final_norm):
    bsz, L, d = x.shape
    lc = ctx.shape[1]
    assert lc == TB and L % TB == 0 and TB % GRID_W == 0

    cin = jnp.zeros((8, d), F32).at[:bsz].set(c).at[bsz].set(c_ctx)
    mod = _mod_call(cin, w_mod, b_mod.reshape(1, -1))[:bsz + 1]
    sh1, sc1, g1, sh2, sc2, g2 = [m.reshape(bsz + 1, 1, d) for m in jnp.split(mod, 6, axis=-1)]

    o = np.cumsum((0, GLA_QK, GLA_QK, GLA_V, GLA_V, GLA_RANK, SSD_INNER, SSD_CONV_DIM, SSD_HEADS))
    wq, wk, wv, wg, wlow, wz, wx, wdt = [w_in[:, int(a):int(b)] for a, b in zip(o[:-1], o[1:])]
    w_misc = jnp.concatenate([wlow, wdt, wdt, jnp.zeros((d, LANES - GLA_RANK - 2 * SSD_HEADS), F32)], axis=1)
    w_cat = jnp.concatenate([wq, wk, wv, wg, wz, wx, w_misc], axis=1).astype(BF16)
    wup = jnp.zeros((LANES, 2 * GLA_QK), F32).at[:GLA_RANK].set(
        jnp.concatenate([gla_w_gk_up[0], gla_w_gk_up[1]], axis=1)).astype(BF16)
    bup = jnp.concatenate([gla_b_gk[0], gla_b_gk[1]]).reshape(1, -1)
    dtb = jnp.zeros((1, LANES), F32).at[0, DT_F:DT_F + SSD_HEADS].set(ssd_dt_bias[0]) \
                                    .at[0, DT_B:DT_B + SSD_HEADS].set(ssd_dt_bias[1])
    q, k, v, g_all, z_all, xbc, ld, misc = _inproj_call(
        x, ctx, sh1, sc1, norm1.reshape(1, d), w_cat, wup, bup, dtb)

    xs, bc = _conv_call(xbc, ssd_conv_w.reshape(9, SSD_CONV_DIM), ssd_conv_b.reshape(1, -1), lc + L)

    o_f, o_b = _gla_call(q, k, v, ld, L)

    a_neg = -jnp.exp(ssd_A_log.astype(F32))
    a_f = jnp.zeros((1, LANES), F32).at[0, DT_F:DT_F + SSD_HEADS].set(a_neg[0])
    a_b = jnp.zeros((1, LANES), F32).at[0, DT_B:DT_B + SSD_HEADS].set(a_neg[1])
    dvec = jnp.repeat(ssd_D, SSD_HEADDIM).reshape(1, SSD_INNER)
    y_f, y_b = _ssd_call(xs, bc, misc, a_f, a_b, dvec, L)

    wr = jnp.zeros((d, LANES), F32).at[:, :N_EXPERTS].set(w_router)
    wr_hi = wr.astype(BF16)
    wr_lo = (wr - wr_hi.astype(F32)).astype(BF16)
    br = jnp.full((1, LANES), NEG_BIG, F32).at[0, :N_EXPERTS].set(b_router)
    x1, h2, eidx, gates, counts = _outproj_call(
        o_f, o_b, g_all, y_f, y_b, z_all, x, g1[:bsz], sh2[:bsz], sc2[:bsz],
        jnp.tile(gla_norm, GLA_HEADS).reshape(1, -1), ssd_norm.reshape(1, -1), w_out.astype(BF16),
        norm2.reshape(1, d), wr_hi, wr_lo, br)

    T = bsz * L
    cnt = counts[0, :N_EXPERTS].astype(jnp.int32)
    padded = ((cnt + MOE_BLOCK - 1) // MOE_BLOCK) * MOE_BLOCK
    pend = jnp.cumsum(padded)
    pstart = pend - padded
    max_rows = T * TOP_K + (T // TB) * N_EXPERTS * (RUN - 1)
    n_blocks = -(-max_rows // MOE_BLOCK) + N_EXPERTS
    blk_start = jnp.arange(n_blocks, dtype=jnp.int32) * MOE_BLOCK
    blk_e = jnp.minimum(jnp.sum(pend[None, :] <= blk_start[:, None], axis=1), N_EXPERTS - 1).astype(jnp.int32)
    n_used = (pend[-1] // MOE_BLOCK).astype(jnp.int32)
    blk_info = jnp.concatenate([blk_e, n_used[None]])
    n_tail = n_blocks - (T * TOP_K) // MOE_BLOCK
    tail = n_used + jnp.arange(n_tail, dtype=jnp.int32)
    fill_blocks = jnp.concatenate([
        jnp.where(padded > 0, pend // MOE_BLOCK - 1, -1),
        jnp.where(tail < n_blocks, tail, -1),
        n_blocks + jnp.arange(2 * SPILL_ROWS // MOE_BLOCK, dtype=jnp.int32)]).astype(jnp.int32)
    pstart_row = jnp.zeros((1, LANES), F32).at[0, :N_EXPERTS].set(pstart.astype(F32))
    lp, cd = _pos_call(eidx.reshape(T, LANES), pstart_row)

    xg = _dispatch_call(fill_blocks, cd, h2.reshape(T, d), lp, n_blocks * MOE_BLOCK)
    y = _moe_call(blk_info, xg, w_gate_up, b_gate_up.reshape(N_EXPERTS, 1, -1), w_down,
                  b_down.reshape(N_EXPERTS, 1, -1), n_blocks)
    out = _combine_call(cd, y, lp, x1.reshape(T, d), gates.reshape(T, LANES), g2[:bsz], final_norm.reshape(1, d))
    return out.reshape(bsz, L, d)


def kernel(x, c, ctx, c_ctx, w_mod, b_mod, norm1, w_in, gla_w_gk_up, gla_b_gk, gla_norm, ssd_conv_w, ssd_conv_b, ssd_dt_bias, ssd_A_log, ssd_D, ssd_norm, w_out, norm2, w_router, b_router, w_gate_up, b_gate_up, w_down, b_down, final_norm):
    assert w_mod.shape[0] == 1, "single-layer kernel"
    return _layer(x, c, ctx, c_ctx, w_mod[0], b_mod[0], norm1[0], w_in[0], gla_w_gk_up[0], gla_b_gk[0],
                  gla_norm[0], ssd_conv_w[0], ssd_conv_b[0], ssd_dt_bias[0], ssd_A_log[0], ssd_D[0],
                  ssd_norm[0], w_out[0], norm2[0], w_router[0], b_router[0], w_gate_up[0], b_gate_up[0],
                  w_down[0], b_down[0], final_norm)
```

```python
import functools

import numpy as np
import jax
import jax.numpy as jnp
from jax import lax
from jax.experimental import pallas as pl
from jax.experimental.pallas import tpu as pltpu

F32 = jnp.float32
BF16 = jnp.bfloat16

EPS = 1e-6
GRID_W = 64
GLA_HEADS = 4
GLA_DK = 64
GLA_DV = 128
GLA_QK = GLA_HEADS * GLA_DK
GLA_V = GLA_HEADS * GLA_DV
GLA_RANK = 16
GLA_GATE_NORM = 16.0
SSD_HEADDIM = 64
SSD_INNER = 512
SSD_HEADS = 8
SSD_GROUPS = 2
SSD_HPG = 4
SSD_STATE = 128
SSD_CONV_DIM = 1024
N_EXPERTS = 32
TOP_K = 4
D_FF = 1024
SWIGLU_LIMIT = 7.0
SWIGLU_ALPHA = 1.702
MOE_BLOCK = 512

TB = 256
GLA_C = 64
GLA_STAGE_GROUP = 4
SSD_C = 128
SSD_STAGE_GROUP = 1
LANES = 128
EXP_CLAMP = 80.0
DT_F = 16
DT_B = 24
NEG_BIG = -1e30
RUN = 8
RL = 1280
CD_LANES = 256
SPILL_ROWS = -(-RL // MOE_BLOCK) * MOE_BLOCK
D_HALF = 512
VMEM_LIMIT = 56 * 1024 * 1024


def _dot(a, b):
    return jnp.dot(a, b, preferred_element_type=F32)


def _dot_nt(a, b):
    return lax.dot_general(a, b, (((1,), (1,)), ((), ())), preferred_element_type=F32)


def _dot_tn(a, b):
    return lax.dot_general(a, b, (((0,), (0,)), ((), ())), preferred_element_type=F32)


def _split3(a):
    hi = a.astype(BF16)
    r1 = a - hi.astype(F32)
    mid = r1.astype(BF16)
    lo = (r1 - mid.astype(F32)).astype(BF16)
    return hi, mid, lo


def _dot_exact_r(m, a):
    hi, mid, lo = _split3(a)
    return _dot(m, hi) + _dot(m, mid) + _dot(m, lo)


def _dot_hilo_r(m, a):
    hi = a.astype(BF16)
    lo = (a - hi.astype(F32)).astype(BF16)
    return _dot(m, hi) + _dot(m, lo)


def _dot_hilo_l(a, m2):
    hi = a.astype(BF16)
    lo = (a - hi.astype(F32)).astype(BF16)
    return _dot(jnp.concatenate([hi, lo], axis=1), m2)


def _sigmoid(x):
    return 1.0 / (1.0 + jnp.exp(-x))


def _softplus(x):
    return jnp.maximum(x, 0.0) + jnp.log1p(jnp.exp(-jnp.abs(x)))


def _params(*sem):
    return pltpu.CompilerParams(dimension_semantics=sem, vmem_limit_bytes=VMEM_LIMIT)


def _mod_kernel(c_ref, w_ref, b_ref, o_ref):
    c = c_ref[...]
    s = c * _sigmoid(c)
    s_hi = s.astype(BF16)
    s_lo = (s - s_hi.astype(F32)).astype(BF16)
    w = w_ref[...]
    w_hi = w.astype(BF16)
    w_lo = (w - w_hi.astype(F32)).astype(BF16)
    o_ref[...] = _dot(s_hi, w_hi) + _dot(s_lo, w_hi) + _dot(s_hi, w_lo) + b_ref[...]


def _mod_call(cin, w, b):
    rows, d = cin.shape
    n = w.shape[1]
    tn = 1536
    return pl.pallas_call(
        _mod_kernel,
        grid=(n // tn,),
        in_specs=[pl.BlockSpec((rows, d), lambda i: (0, 0)),
                  pl.BlockSpec((d, tn), lambda i: (0, i)),
                  pl.BlockSpec((1, tn), lambda i: (0, i))],
        out_specs=pl.BlockSpec((rows, tn), lambda i: (0, i)),
        out_shape=jax.ShapeDtypeStruct((rows, n), F32),
        compiler_params=_params("arbitrary"),
    )(cin, w, b)


_C_Q, _C_K, _C_V, _C_G, _C_Z, _C_X, _C_M, _C_END = 0, 256, 512, 1024, 1536, 2048, 3072, 3200


def _inproj_kernel(xa_ref, xb_ref, ctx_ref, shl_ref, scl_ref, shc_ref, scc_ref, n1_ref, w_ref, wup_ref, bup_ref,
                   dtb_ref, q_ref, k_ref, v_ref, g_ref, z_ref, xbc_ref, ld_ref, misc_ref, h_scr):
    j = pl.program_id(1)

    def normmod(xv, sh_ref, sc_ref):
        ms = jnp.mean(xv * xv, axis=-1, keepdims=True)
        y = xv * lax.rsqrt(ms + EPS) * n1_ref[...]
        return (y * (1.0 + sc_ref[0]) + sh_ref[0]).astype(BF16)

    @pl.when(j == 0)
    def _():
        h_scr[0:TB, :] = normmod(ctx_ref[0], shc_ref, scc_ref)

    @pl.when(j > 0)
    def _():
        h_scr[0:TB, :] = normmod(xa_ref[0], shl_ref, scl_ref)

    h_scr[TB:2 * TB, :] = normmod(xb_ref[0], shl_ref, scl_ref)
    h = h_scr[...]

    def mm(lo, hi):
        return _dot(h, w_ref[:, lo:hi])

    q_ref[0] = (mm(_C_Q, _C_K) * (GLA_DK ** -0.5)).astype(BF16)
    k_ref[0] = mm(_C_K, _C_V).astype(BF16)
    v_ref[0] = mm(_C_V, _C_G).astype(BF16)
    g_ref[0] = mm(_C_G, _C_Z)
    z_ref[0] = mm(_C_Z, _C_X)
    xbc_ref[0] = mm(_C_X, _C_M)
    m = mm(_C_M, _C_END)
    zz = _dot(m.astype(BF16), wup_ref[...]) + bup_ref[...]
    ld_ref[0] = -_softplus(-zz) * (1.0 / GLA_GATE_NORM)
    misc_ref[0] = _softplus(m + dtb_ref[...])


def _inproj_call(x, ctx, sh1, sc1, n1, w_cat, wup, bup, dtb):
    bsz, L, d = x.shape
    nx = L // TB
    nj = -(-(nx + 1) // 2)
    tok = lambda n: pl.BlockSpec((1, 2 * TB, n), lambda b, j: (b, j, 0))
    const = lambda a: pl.BlockSpec(a.shape, lambda b, j: (0,) * a.ndim)
    xblk = lambda off: pl.BlockSpec((1, TB, d), lambda b, j: (b, jnp.clip(2 * j + off, 0, nx - 1), 0))
    mod_lat = pl.BlockSpec((1, 1, d), lambda b, j: (b, 0, 0))
    mod_ctx = pl.BlockSpec((1, 1, d), lambda b, j: (bsz, 0, 0))
    outs = [(GLA_QK, BF16), (GLA_QK, BF16), (GLA_V, BF16), (GLA_V, F32), (SSD_INNER, F32),
            (SSD_CONV_DIM, F32), (2 * GLA_QK, F32), (LANES, F32)]
    return pl.pallas_call(
        _inproj_kernel,
        grid=(bsz, nj),
        in_specs=[xblk(-1), xblk(0), pl.BlockSpec((1, TB, d), lambda b, j: (b, 0, 0)),
                  mod_lat, mod_lat, mod_ctx, mod_ctx,
                  const(n1), const(w_cat), const(wup), const(bup), const(dtb)],
        out_specs=[tok(n) for n, _ in outs],
        out_shape=[jax.ShapeDtypeStruct((bsz, nj * 2 * TB, n), dt) for n, dt in outs],
        scratch_shapes=[pltpu.VMEM((2 * TB, d), BF16)],
        compiler_params=_params("arbitrary", "arbitrary"),
    )(x, x, ctx, sh1, sc1, sh1, sc1, n1, w_cat, wup, bup, dtb)


_EXT_PAD = 8
_EXT_BASE = _EXT_PAD + GRID_W
_EXT_ROWS = 2 * _EXT_PAD + 2 * GRID_W + TB


def _conv_kernel(prev_ref, cur_ref, next_ref, w_ref, b_ref, xs_ref, bc_ref, ext):
    j = pl.program_id(1)
    nj = pl.num_programs(1)
    is_ctx = j == 0
    zpad = jnp.zeros((_EXT_PAD, SSD_CONV_DIM), F32)
    ext[0:_EXT_PAD, :] = zpad
    ext[_EXT_ROWS - _EXT_PAD:_EXT_ROWS, :] = zpad
    ext[_EXT_PAD:_EXT_BASE, :] = jnp.where(j >= 2, prev_ref[0], 0.0)
    ext[_EXT_BASE:_EXT_BASE + TB, :] = cur_ref[0]
    ext[_EXT_BASE + TB:_EXT_BASE + TB + GRID_W, :] = jnp.where(
        jnp.logical_and(j >= 1, j <= nj - 2), next_ref[0], 0.0)

    t = lax.broadcasted_iota(jnp.int32, (TB, LANES), 0)
    col = t & (GRID_W - 1)
    ok_l = jnp.where(is_ctx, t, col) >= 1
    ok_r = jnp.where(is_ctx, t - (TB - GRID_W), col) <= GRID_W - 2
    lat = jnp.where(is_ctx, 0.0, 1.0)

    for c in range(SSD_CONV_DIM // LANES):
        lo, hi = c * LANES, (c + 1) * LANES
        acc = jnp.zeros((TB, LANES), F32)
        for dr in (-1, 0, 1):
            for dc in (-1, 0, 1):
                start = _EXT_BASE + GRID_W * dr + dc
                tap = ext[start:start + TB, lo:hi]
                wi = 3 * (dr + 1) + (dc + 1)
                wv = w_ref[wi:wi + 1, lo:hi]
                if dr != 0:
                    wv = wv * lat
                if dc == -1:
                    tap = jnp.where(ok_l, tap, 0.0)
                elif dc == 1:
                    tap = jnp.where(ok_r, tap, 0.0)
                acc = acc + tap * wv
        y = acc + b_ref[:, lo:hi]
        y = y * _sigmoid(y)
        if c < SSD_INNER // LANES:
            xs_ref[0, :, lo:hi] = y
        else:
            bc_ref[0, :, lo - SSD_INNER:hi - SSD_INNER] = y.astype(BF16)


def _conv_call(xbc, w9, bias, ls):
    bsz, _, ch = xbc.shape
    nj = ls // TB
    rpb = TB // GRID_W
    nrow = ls // GRID_W
    return pl.pallas_call(
        _conv_kernel,
        grid=(bsz, nj),
        in_specs=[pl.BlockSpec((1, GRID_W, ch), lambda b, j: (b, jnp.maximum(rpb * j - 1, 0), 0)),
                  pl.BlockSpec((1, TB, ch), lambda b, j: (b, j, 0)),
                  pl.BlockSpec((1, GRID_W, ch), lambda b, j: (b, jnp.minimum(rpb * j + rpb, nrow - 1), 0)),
                  pl.BlockSpec((9, ch), lambda b, j: (0, 0)),
                  pl.BlockSpec((1, ch), lambda b, j: (0, 0))],
        out_specs=[pl.BlockSpec((1, TB, SSD_INNER), lambda b, j: (b, j, 0)),
                   pl.BlockSpec((1, TB, ch - SSD_INNER), lambda b, j: (b, j, 0))],
        out_shape=[jax.ShapeDtypeStruct((bsz, ls, SSD_INNER), F32),
                   jax.ShapeDtypeStruct((bsz, ls, ch - SSD_INNER), BF16)],
        scratch_shapes=[pltpu.VMEM((_EXT_ROWS, ch), F32)],
        compiler_params=_params("arbitrary", "arbitrary"),
    )(xbc, xbc, xbc, w9, bias)


def _fwd_blk(s):
    return s


def _bwd_blk(s, ns):
    return jnp.where(s == 0, 0, ns - s)


def _scan_batch(bsz):
    return 2 if bsz % 2 == 0 else 1


class _GlaChunk:
    def __init__(self, q, k, v, la, st_ref, tri_m, fwd, store):
        self.q, self.k, self.v, self.la, self.st_ref, self.tri_m, self.fwd, self.store = (
            q, k, v, la, st_ref, tri_m, fwd, store)

    def stage_sums(self):
        self.b = _dot_hilo_r(self.tri_m, self.la)

    def stage_factors(self):
        C = GLA_C
        b = self.b
        self.bt = b[C - 1:C, :] if self.fwd else b[0:1, :]
        r = 0.5 * self.bt
        self.er = jnp.exp(r)
        self.qt = (self.q.astype(F32) * jnp.exp(jnp.minimum(b - r, EXP_CLAMP))).astype(BF16)
        kt = (self.k.astype(F32) * jnp.exp(jnp.minimum(r - b, EXP_CLAMP))).astype(BF16)
        head_k = lax.broadcasted_iota(jnp.int32, (C, GLA_QK), 1) >> 6
        zero = jnp.zeros_like(kt)
        self.kh = [jnp.where(head_k == h, kt, zero) for h in range(GLA_HEADS)]
        self.qh = [jnp.where(head_k == h, self.qt, zero) for h in range(GLA_HEADS)]

    def stage_products(self):
        v = self.v
        kcat = jnp.concatenate(self.kh, axis=0)
        self.sc = _dot_nt(self.qt, kcat)
        vcat = jnp.concatenate([v[:, h * GLA_DV:(h + 1) * GLA_DV] for h in range(GLA_HEADS)], axis=0)
        self.u = _dot_tn(vcat, kcat) * self.er

    def stage_mask(self):
        C = GLA_C
        ii = lax.broadcasted_iota(jnp.int32, (C, GLA_HEADS * C), 0)
        jj = lax.broadcasted_iota(jnp.int32, (C, GLA_HEADS * C), 1) & (C - 1)
        causal = (jj <= ii) if self.fwd else (jj >= ii)
        self.p = jnp.where(causal, self.sc, 0.0).astype(BF16)
        v = self.v
        head_v = lax.broadcasted_iota(jnp.int32, (C, GLA_V), 1) >> 7
        self.vst = jnp.concatenate([jnp.where(head_v == h, v, jnp.zeros_like(v)) for h in range(GLA_HEADS)], axis=0)

    def stage_intra(self):
        self.o = _dot(self.p, self.vst)

    def stage_inter(self):
        st = self.st_ref[...]
        ster = (st * self.er).astype(BF16)
        self.st_ref[...] = st * jnp.exp(self.bt) + self.u
        res = _dot_nt(jnp.concatenate(self.qh, axis=0), ster)
        self.inter = jnp.concatenate([res[h * GLA_C:(h + 1) * GLA_C, :] for h in range(GLA_HEADS)], axis=1)

    def stage_out(self):
        self.store(self.o + self.inter)


def _gla_kernel(qf_ref, kf_ref, vf_ref, lf_ref, qb_ref, kb_ref, vb_ref, lb_ref, trif_ref, trib_ref,
                of_ref, ob_ref, stf, stb):
    s = pl.program_id(1)

    @pl.when(s == 0)
    def _():
        stf[...] = jnp.zeros_like(stf)
        stb[...] = jnp.zeros_like(stb)

    nsub = TB // GLA_C

    def store_to(ref, bb, sl):
        def store(val):
            ref[bb, sl, :] = val
        return store

    group = GLA_STAGE_GROUP
    for g0 in range(0, nsub, group):
        steps = []
        for i in range(g0, g0 + group):
            sf = pl.ds(i * GLA_C, GLA_C)
            sb = pl.ds((nsub - 1 - i) * GLA_C, GLA_C)
            chunks = []
            for bb in range(qf_ref.shape[0]):
                chunks.append(_GlaChunk(qf_ref[bb, sf, :], kf_ref[bb, sf, :], vf_ref[bb, sf, :], lf_ref[bb, sf, :],
                                        stf.at[bb], trif_ref[...], True, store_to(of_ref, bb, sf)))
                chunks.append(_GlaChunk(qb_ref[bb, sb, :], kb_ref[bb, sb, :], vb_ref[bb, sb, :], lb_ref[bb, sb, :],
                                        stb.at[bb], trib_ref[...], False, store_to(ob_ref, bb, sb)))
            steps.append(chunks)
        for stage in ("stage_sums", "stage_factors", "stage_products", "stage_mask", "stage_intra"):
            for chunks in steps:
                for ch in chunks:
                    getattr(ch, stage)()
        for chunks in steps:
            for stage in ("stage_inter", "stage_out"):
                for ch in chunks:
                    getattr(ch, stage)()


def _gla_call(q, k, v, ld, L):
    bsz = q.shape[0]
    nx = L // TB
    ns = nx + 1
    trif = jnp.asarray(np.tril(np.ones((GLA_C, GLA_C), np.float32)), BF16)
    trib = jnp.asarray(np.triu(np.ones((GLA_C, GLA_C), np.float32)), BF16)
    nb = _scan_batch(bsz)
    f = lambda n, lane=0: pl.BlockSpec((nb, TB, n), lambda b, s: (b, _fwd_blk(s), lane))
    r = lambda n, lane=0: pl.BlockSpec((nb, TB, n), lambda b, s: (b, _bwd_blk(s, ns), lane))
    tri = pl.BlockSpec((GLA_C, GLA_C), lambda b, s: (0, 0))
    return pl.pallas_call(
        _gla_kernel,
        grid=(bsz // nb, ns),
        in_specs=[f(GLA_QK), f(GLA_QK), f(GLA_V), f(GLA_QK, 0),
                  r(GLA_QK), r(GLA_QK), r(GLA_V), r(GLA_QK, 1), tri, tri],
        out_specs=[pl.BlockSpec((nb, TB, GLA_V), lambda b, s: (b, jnp.maximum(s - 1, 0), 0)),
                   pl.BlockSpec((nb, TB, GLA_V), lambda b, s: (b, jnp.where(s == 0, nx - 1, nx - s), 0))],
        out_shape=[jax.ShapeDtypeStruct((bsz, L, GLA_V), F32)] * 2,
        scratch_shapes=[pltpu.VMEM((nb, GLA_DV, GLA_QK), F32)] * 2,
        compiler_params=_params("arbitrary", "arbitrary"),
    )(q, k, v, ld, q, k, v, ld, trif, trib)


class _SsdChunk:
    def __init__(self, xs, bc, dtm, avec, dvec, st_ref, tri_m, e_m, base, fwd, store):
        self.xs, self.bc, self.dtm, self.avec, self.dvec, self.st_ref = xs, bc, dtm, avec, dvec, st_ref
        self.tri_m, self.e_m, self.base, self.fwd, self.store = tri_m, e_m, base, fwd, store

    def stage_sums(self):
        self.dt_exp = _dot_hilo_l(self.dtm, self.e_m)
        self.acum = _dot_exact_r(self.tri_m, self.dtm * self.avec)

    def stage_expand(self):
        self.acum_exp = _dot_hilo_l(self.acum, self.e_m)
        self.acum_t = self.acum.T
        self.xdt = self.xs * self.dt_exp
        bc = self.bc
        self.bg = [bc[:, 128 * g:128 * (g + 1)] for g in range(SSD_GROUPS)]
        self.cg = [bc[:, 256 + 128 * g:256 + 128 * (g + 1)] for g in range(SSD_GROUPS)]
        self.cb = [_dot_nt(self.cg[g], self.bg[g]) for g in range(SSD_GROUPS)]

    def stage_decay(self):
        C = SSD_C
        ii = lax.broadcasted_iota(jnp.int32, (C, C), 0)
        jj = lax.broadcasted_iota(jnp.int32, (C, C), 1)
        tri = (jj <= ii) if self.fwd else (jj >= ii)
        self.ms = []
        for g in range(SSD_GROUPS):
            for rr in range(SSD_HPG):
                ln = self.base + SSD_HPG * g + rr
                diff = self.acum[:, ln:ln + 1] - self.acum_t[ln:ln + 1, :]
                seg = jnp.where(tri, jnp.exp(jnp.minimum(diff, 0.0)), 0.0)
                self.ms.append((self.cb[g] * seg).astype(BF16))
        ae = self.acum_exp
        self.al_exp = ae[C - 1:C, :] if self.fwd else ae[0:1, :]
        self.xw = (self.xdt * jnp.exp(self.al_exp - ae)).astype(BF16)
        xdt_b = self.xdt.astype(BF16)
        head = lax.broadcasted_iota(jnp.int32, (C, SSD_INNER), 1) >> 6
        zero = jnp.zeros_like(xdt_b)
        self.xh = [jnp.where(head == h, xdt_b, zero) for h in range(SSD_HEADS)]

    def stage_products(self):
        self.yg, self.ug = [], []
        for g in range(SSD_GROUPS):
            gl, gh = 256 * g, 256 * (g + 1)
            yg = _dot(self.ms[SSD_HPG * g], self.xh[SSD_HPG * g][:, gl:gh])
            for rr in range(1, SSD_HPG):
                yg = yg + _dot(self.ms[SSD_HPG * g + rr], self.xh[SSD_HPG * g + rr][:, gl:gh])
            self.yg.append(yg)
            self.ug.append(_dot_tn(self.bg[g], self.xw[:, gl:gh]))

    def stage_state(self):
        ys = []
        for g in range(SSD_GROUPS):
            gl, gh = 256 * g, 256 * (g + 1)
            sg = self.st_ref[g]
            yoff = _dot(self.cg[g], sg.astype(BF16)) * jnp.exp(self.acum_exp[:, gl:gh])
            self.st_ref[g] = sg * jnp.exp(self.al_exp[:, gl:gh]) + self.ug[g]
            ys.append(self.yg[g] + yoff)
        y = jnp.concatenate(ys, axis=1)
        if self.dvec is not None:
            y = y + self.dvec * self.xs
        self.store(y)


def _ssd_kernel(xf_ref, bcf_ref, mf_ref, xb_ref, bcb_ref, mb_ref, af_ref, ab_ref, d_ref,
                trif_ref, trib_ref, ef_ref, eb_ref, yf_ref, yb_ref, stf, stb):
    s = pl.program_id(1)

    @pl.when(s == 0)
    def _():
        stf[...] = jnp.zeros_like(stf)
        stb[...] = jnp.zeros_like(stb)

    nsub = TB // SSD_C

    def store_to(ref, bb, sl):
        def store(val):
            ref[bb, sl, :] = val
        return store

    steps = []
    for i in range(nsub):
        sf = pl.ds(i * SSD_C, SSD_C)
        sb = pl.ds((nsub - 1 - i) * SSD_C, SSD_C)
        chunks = []
        for bb in range(xf_ref.shape[0]):
            chunks.append(_SsdChunk(xf_ref[bb, sf, :], bcf_ref[bb, sf, :], mf_ref[bb, sf, :], af_ref[...],
                                    d_ref[...], stf.at[bb], trif_ref[...], ef_ref[...], DT_F, True,
                                    store_to(yf_ref, bb, sf)))
            chunks.append(_SsdChunk(xb_ref[bb, sb, :], bcb_ref[bb, sb, :], mb_ref[bb, sb, :], ab_ref[...],
                                    None, stb.at[bb], trib_ref[...], eb_ref[...], DT_B, False,
                                    store_to(yb_ref, bb, sb)))
        steps.append(chunks)
    for g0 in range(0, nsub, SSD_STAGE_GROUP):
        group = steps[g0:g0 + SSD_STAGE_GROUP]
        for stage in ("stage_sums", "stage_expand", "stage_decay", "stage_products"):
            for chunks in group:
                for ch in chunks:
                    getattr(ch, stage)()
        for chunks in group:
            for ch in chunks:
                ch.stage_state()


def _expand_matrix(base):
    e = np.zeros((LANES, SSD_INNER), np.float32)
    for h in range(SSD_HEADS):
        e[base + h, SSD_HEADDIM * h:SSD_HEADDIM * (h + 1)] = 1.0
    return jnp.asarray(np.concatenate([e, e], axis=0), BF16)


def _ssd_call(xs, bc, misc, a_f, a_b, dvec, L):
    bsz, ls, _ = xs.shape
    ns = ls // TB
    nx = L // TB
    trif = jnp.asarray(np.tril(np.ones((SSD_C, SSD_C), np.float32)), BF16)
    trib = jnp.asarray(np.triu(np.ones((SSD_C, SSD_C), np.float32)), BF16)
    ef, eb = _expand_matrix(DT_F), _expand_matrix(DT_B)
    nb = _scan_batch(bsz)
    f = lambda n: pl.BlockSpec((nb, TB, n), lambda b, s: (b, _fwd_blk(s), 0))
    r = lambda n: pl.BlockSpec((nb, TB, n), lambda b, s: (b, _bwd_blk(s, ns), 0))
    const = lambda a: pl.BlockSpec(a.shape, lambda b, s: (0,) * a.ndim)
    return pl.pallas_call(
        _ssd_kernel,
        grid=(bsz // nb, ns),
        in_specs=[f(SSD_INNER), f(512), f(LANES), r(SSD_INNER), r(512), r(LANES),
                  const(a_f), const(a_b), const(dvec), const(trif), const(trib), const(ef), const(eb)],
        out_specs=[pl.BlockSpec((nb, TB, SSD_INNER), lambda b, s: (b, jnp.maximum(s - 1, 0), 0)),
                   pl.BlockSpec((nb, TB, SSD_INNER), lambda b, s: (b, jnp.where(s == 0, nx - 1, nx - s), 0))],
        out_shape=[jax.ShapeDtypeStruct((bsz, L, SSD_INNER), F32)] * 2,
        scratch_shapes=[pltpu.VMEM((nb, SSD_GROUPS, SSD_STATE, SSD_HPG * SSD_HEADDIM), F32)] * 2,
        compiler_params=_params("arbitrary", "arbitrary"),
    )(xs, bc, misc, xs, bc, misc, a_f, a_b, dvec, trif, trib, ef, eb)


def _outproj_kernel(of_ref, ob_ref, ga_ref, gb_ref, yf_ref, yb_ref, za_ref, zb_ref, x_ref, g1_ref, sh2_ref, sc2_ref,
                    gn_ref, sn_ref, wo_ref, n2_ref, wrh_ref, wrl_ref, br_ref,
                    x1_ref, h2_ref, eidx_ref, gate_ref, cnt_ref, mix_s, hl_s):
    first = jnp.logical_and(pl.program_id(0) == 0, pl.program_id(1) == 0)

    @pl.when(first)
    def _():
        cnt_ref[...] = jnp.zeros_like(cnt_ref)

    halves = [(pl.ds(0, TB), ga_ref, za_ref), (pl.ds(TB, TB), gb_ref, zb_ref)]

    for rows, g_ref, z_ref in halves:
        o = of_ref[0, rows, :] + ob_ref[0, rows, :]
        gg = g_ref[0]
        for h in range(GLA_HEADS):
            lo, hi = GLA_DV * h, GLA_DV * (h + 1)
            oh = o[:, lo:hi]
            ms = jnp.mean(oh * oh, axis=-1, keepdims=True)
            gh = gg[:, lo:hi]
            mix_s[rows, lo:hi] = (oh * lax.rsqrt(ms + EPS) * gn_ref[:, lo:hi] * (gh * _sigmoid(gh))).astype(BF16)
        zz = z_ref[0]
        u = (yf_ref[0, rows, :] + yb_ref[0, rows, :]) * (zz * _sigmoid(zz))
        gw = SSD_INNER // SSD_GROUPS
        for g in range(SSD_GROUPS):
            lo, hi = gw * g, gw * (g + 1)
            ug = u[:, lo:hi]
            ms = jnp.mean(ug * ug, axis=-1, keepdims=True)
            mix_s[rows, GLA_V + lo:GLA_V + hi] = (ug * lax.rsqrt(ms + EPS) * sn_ref[:, lo:hi]).astype(BF16)

    x1_ref[0] = x_ref[0] + g1_ref[0] * _dot(mix_s[...], wo_ref[...])

    for rows, _, _ in halves:
        x1 = x1_ref[0, rows, :]
        ms = jnp.mean(x1 * x1, axis=-1, keepdims=True)
        h2 = (x1 * lax.rsqrt(ms + EPS) * n2_ref[...]) * (1.0 + sc2_ref[0]) + sh2_ref[0]
        h_hi = h2.astype(BF16)
        h2_ref[0, rows, :] = h_hi
        hl_s[rows, :] = (h2 - h_hi.astype(F32)).astype(BF16)

    h_hi = h2_ref[0]
    logits = (_dot(h_hi, wrh_ref[...]) + _dot(hl_s[...], wrh_ref[...]) + _dot(h_hi, wrl_ref[...])) + br_ref[...]

    lane = lax.broadcasted_iota(jnp.int32, (TB, LANES), 1).astype(F32)
    st = [dict(work=logits[i * TB:(i + 1) * TB, :], eidx=jnp.full((TB, LANES), -1.0, F32),
               gates=jnp.zeros((TB, LANES), F32), sel=jnp.zeros((TB, LANES), F32), m0=None,
               den=jnp.zeros((TB, 1), F32)) for i in range(2)]
    for kk in range(TOP_K):
        for t in st:
            t["m"] = jnp.max(t["work"], axis=-1, keepdims=True)
        for t in st:
            t["idx"] = jnp.min(jnp.where(t["work"] == t["m"], lane, float(LANES)), axis=-1, keepdims=True)
        for t in st:
            hit = lane == t["idx"]
            if t["m0"] is None:
                t["m0"] = t["m"]
            e = jnp.exp(t["m"] - t["m0"])
            t["den"] = t["den"] + e
            t["eidx"] = jnp.where(lane == float(kk), t["idx"], t["eidx"])
            t["gates"] = jnp.where(lane == float(kk), e, t["gates"])
            t["sel"] = jnp.where(hit, 1.0, t["sel"])
            t["work"] = jnp.where(hit, NEG_BIG, t["work"])
    for (rows, _, _), t in zip(halves, st):
        eidx_ref[0, rows, :] = t["eidx"].astype(jnp.int32)
        gate_ref[0, rows, :] = t["gates"] / t["den"]
        cnt = jnp.sum(t["sel"], axis=0, keepdims=True)
        cnt_ref[...] += jnp.floor((cnt + (RUN - 1.0)) * (1.0 / RUN)) * RUN


def _outproj_call(o_f, o_b, g_all, y_f, y_b, z_all, x, g1, sh2, sc2, gn, sn, wo, n2, wr_hi, wr_lo, br):
    bsz, L, d = x.shape
    nj = L // (2 * TB)
    tok = lambda n: pl.BlockSpec((1, 2 * TB, n), lambda b, j: (b, j, 0))
    tok_off = lambda n, half: pl.BlockSpec((1, TB, n), lambda b, j: (b, 2 * j + 1 + half, 0))
    const = lambda a: pl.BlockSpec(a.shape, lambda b, j: (0,) * a.ndim)
    mod = pl.BlockSpec((1, 1, d), lambda b, j: (b, 0, 0))
    return pl.pallas_call(
        _outproj_kernel,
        grid=(bsz, nj),
        in_specs=[tok(GLA_V), tok(GLA_V), tok_off(GLA_V, 0), tok_off(GLA_V, 1),
                  tok(SSD_INNER), tok(SSD_INNER), tok_off(SSD_INNER, 0), tok_off(SSD_INNER, 1),
                  tok(d), mod, mod, mod, const(gn), const(sn), const(wo), const(n2),
                  const(wr_hi), const(wr_lo), const(br)],
        out_specs=[tok(d), tok(d), tok(LANES), tok(LANES), pl.BlockSpec((1, LANES), lambda b, j: (0, 0))],
        out_shape=[jax.ShapeDtypeStruct((bsz, L, d), F32), jax.ShapeDtypeStruct((bsz, L, d), BF16),
                   jax.ShapeDtypeStruct((bsz, L, LANES), jnp.int32), jax.ShapeDtypeStruct((bsz, L, LANES), F32),
                   jax.ShapeDtypeStruct((1, LANES), F32)],
        scratch_shapes=[pltpu.VMEM((2 * TB, d), BF16), pltpu.VMEM((2 * TB, d), BF16)],
        compiler_params=_params("arbitrary", "arbitrary"),
    )(o_f, o_b, g_all, g_all, y_f, y_b, z_all, z_all, x, g1, sh2, sc2, gn, sn, wo, n2, wr_hi, wr_lo, br)


def _pos_kernel(eidx_ref, pstart_ref, lst_ref, ust_ref, lp_ref, cd_ref, carry):
    @pl.when(pl.program_id(0) == 0)
    def _():
        carry[...] = pstart_ref[...]

    eidx = eidx_ref[...]
    lane = lax.broadcasted_iota(jnp.int32, (TB, LANES), 1)
    hits = [lane == eidx[:, kk:kk + 1] for kk in range(TOP_K)]
    sel = jnp.zeros((TB, LANES), F32)
    for hmask in hits:
        sel = jnp.where(hmask, 1.0, sel)
    cnt = jnp.sum(sel, axis=0, keepdims=True)
    run = jnp.floor((cnt + (RUN - 1.0)) * (1.0 / RUN)) * RUN
    rank = _dot(lst_ref[...], sel.astype(BF16))
    loff = _dot(jnp.broadcast_to(run, (8, LANES)).astype(BF16), ust_ref[...])[0:1]
    pos = loff + rank
    lp = jnp.zeros((TB, LANES), jnp.int32)
    for kk, hmask in enumerate(hits):
        lk = jnp.sum(jnp.where(hmask, pos, 0.0), axis=-1, keepdims=True)
        lp = jnp.where(lane == kk, lk.astype(jnp.int32), lp)
    lp_ref[...] = lp

    base = carry[...]
    eye = lax.broadcasted_iota(jnp.int32, (LANES, LANES), 0) == lax.broadcasted_iota(jnp.int32, (LANES, LANES), 1)
    col = lambda v: jnp.sum(jnp.where(eye, v, 0.0), axis=1, keepdims=True)
    loff_c, run_c, shift_c = col(loff), col(run), col(base - loff)
    row0 = (lax.broadcasted_iota(jnp.int32, (LANES, CD_LANES), 1) * RUN).astype(F32)
    inside = jnp.where(row0 >= loff_c, 1.0, 0.0) * jnp.where(row0 < loff_c + run_c, 1.0, 0.0)
    valid = jnp.sum(inside, axis=0, keepdims=True)
    dest = jnp.sum(inside * shift_c, axis=0, keepdims=True) + row0[0:1]
    cd_ref[0] = jnp.where(valid > 0.0, dest, -1.0).astype(jnp.int32)
    carry[...] = base + run


def _pos_call(eidx, pstart):
    T = eidx.shape[0]
    lst = jnp.asarray(np.tril(np.ones((TB, TB), np.float32), -1), BF16)
    ust = jnp.asarray(np.triu(np.ones((LANES, LANES), np.float32), 1), BF16)
    return pl.pallas_call(
        _pos_kernel,
        grid=(T // TB,),
        in_specs=[pl.BlockSpec((TB, LANES), lambda i: (i, 0)),
                  pl.BlockSpec((1, LANES), lambda i: (0, 0)),
                  pl.BlockSpec((TB, TB), lambda i: (0, 0)),
                  pl.BlockSpec((LANES, LANES), lambda i: (0, 0))],
        out_specs=[pl.BlockSpec((TB, LANES), lambda i: (i, 0)),
                   pl.BlockSpec((1, 1, CD_LANES), lambda i: (i, 0, 0))],
        out_shape=[jax.ShapeDtypeStruct((T, LANES), jnp.int32),
                   jax.ShapeDtypeStruct((T // TB, 1, CD_LANES), jnp.int32)],
        scratch_shapes=[pltpu.VMEM((1, LANES), F32)],
        compiler_params=_params("arbitrary"),
    )(eidx, pstart, lst, ust)


def _pair_matrix(lp, weights):
    j = lax.broadcasted_iota(jnp.int32, (TB, RL), 1)
    m = jnp.zeros((TB, RL), F32)
    for kk in range(TOP_K):
        w = 1.0 if weights is None else weights[:, kk:kk + 1]
        m = m + jnp.where(j == lp[:, kk:kk + 1], w, 0.0)
    return m


def _pack_bf16_pairs(v):
    bits = pltpu.bitcast(v.astype(BF16).astype(F32), jnp.uint32)
    return bits[:, D_HALF:] | (bits[:, :D_HALF] >> 16)


def _unpack_bf16_pairs(w):
    lo = pltpu.bitcast(w << 16, F32).astype(BF16)
    hi = pltpu.bitcast(w & jnp.uint32(0xFFFF0000), F32).astype(BF16)
    return lo, hi


def _dispatch_kernel(fill_ref, cd_ref, h_ref, lp_ref, xg_ref, sorted_s, sems, fill_sem):
    i = pl.program_id(0)
    n = pl.num_programs(0)
    slot = i & 1
    sorted_s[slot] = _pack_bf16_pairs(_dot_tn(_pair_matrix(lp_ref[...], None).astype(BF16), h_ref[...]))

    @pl.when(i == 0)
    def _():
        def fill(k):
            blk = pl.multiple_of(fill_ref[k] * MOE_BLOCK, MOE_BLOCK)
            return pltpu.make_async_copy(sorted_s.at[0, pl.ds(0, MOE_BLOCK), :],
                                         xg_ref.at[pl.ds(blk, MOE_BLOCK), :], fill_sem)

        for k in range(fill_ref.shape[0]):
            pl.when(fill_ref[k] >= 0)(lambda k=k: fill(k).start())
        for k in range(fill_ref.shape[0]):
            pl.when(fill_ref[k] >= 0)(lambda k=k: fill(k).wait())

    spill = xg_ref.shape[0] - (2 - slot) * SPILL_ROWS

    def body(c, carry):
        d = cd_ref[0, 0, c]
        d = pl.multiple_of(jnp.where(d < 0, spill + RUN * c, d), RUN)
        src = pl.multiple_of(RUN * c, RUN)
        pltpu.make_async_copy(sorted_s.at[slot, pl.ds(src, RUN), :], xg_ref.at[pl.ds(d, RUN), :],
                              sems.at[slot]).start()
        return carry

    lax.fori_loop(0, RL // RUN, body, 0, unroll=4)

    def drain(s):
        pltpu.make_async_copy(sorted_s.at[s], xg_ref.at[pl.ds(0, RL), :], sems.at[s]).wait()

    pl.when(i > 0)(lambda: drain(1 - slot))
    pl.when(i == n - 1)(lambda: drain(slot))


def _dispatch_call(fill_blocks, cd, h2, lp, P):
    T, d = h2.shape
    grid_spec = pltpu.PrefetchScalarGridSpec(
        num_scalar_prefetch=1,
        grid=(T // TB,),
        in_specs=[pl.BlockSpec((1, 1, CD_LANES), lambda i, fb: (i, 0, 0), memory_space=pltpu.SMEM),
                  pl.BlockSpec((TB, d), lambda i, fb: (i, 0)),
                  pl.BlockSpec((TB, LANES), lambda i, fb: (i, 0))],
        out_specs=pl.BlockSpec(memory_space=pl.ANY),
        scratch_shapes=[pltpu.VMEM((2, RL, D_HALF), jnp.uint32), pltpu.SemaphoreType.DMA((2,)),
                        pltpu.SemaphoreType.DMA(())],
    )
    return pl.pallas_call(
        _dispatch_kernel,
        grid_spec=grid_spec,
        out_shape=jax.ShapeDtypeStruct((P + 2 * SPILL_ROWS, D_HALF), jnp.uint32),
        compiler_params=_params("arbitrary"),
    )(fill_blocks, cd, h2, lp)


def _moe_kernel(be_ref, x_ref, wgu_ref, bgu_ref, wd_ref, bd_ref, y_ref, wgu_b, wd_b, act_s):
    i = pl.program_id(0)
    n_used = be_ref[pl.num_programs(0)]
    changed = jnp.logical_or(i == 0, be_ref[i] != be_ref[jnp.maximum(i - 1, 0)])

    @pl.when(changed)
    def _():
        wgu_b[...] = wgu_ref[0].astype(BF16)
        wd_b[...] = wd_ref[0].astype(BF16)

    @pl.when(i < n_used)
    def _():
        xb = jnp.concatenate(_unpack_bf16_pairs(x_ref[...]), axis=1)
        cw = 256
        for c in range(D_FF // cw):
            lo, hi = c * cw, (c + 1) * cw
            gate = _dot(xb, wgu_b[:, lo:hi]) + bgu_ref[0, :, lo:hi]
            up = _dot(xb, wgu_b[:, D_FF + lo:D_FF + hi]) + bgu_ref[0, :, D_FF + lo:D_FF + hi]
            gate = jnp.minimum(gate, SWIGLU_LIMIT)
            up = jnp.clip(up, -SWIGLU_LIMIT, SWIGLU_LIMIT)
            act_s[:, lo:hi] = ((up + 1.0) * (gate * _sigmoid(SWIGLU_ALPHA * gate))).astype(BF16)
        y_ref[...] = _pack_bf16_pairs(_dot(act_s[...], wd_b[...]) + bd_ref[0])

    @pl.when(i >= n_used)
    def _():
        y_ref[...] = jnp.zeros_like(y_ref)


def _moe_call(blk_e, xg, wgu, bgu, wd, bd, nb):
    P = nb * MOE_BLOCK
    ne, d, f2 = wgu.shape
    grid_spec = pltpu.PrefetchScalarGridSpec(
        num_scalar_prefetch=1,
        grid=(nb,),
        in_specs=[pl.BlockSpec((MOE_BLOCK, D_HALF), lambda i, be: (jnp.minimum(i, be[nb] - 1), 0)),
                  pl.BlockSpec((1, d, f2), lambda i, be: (be[i], 0, 0)),
                  pl.BlockSpec((1, 1, f2), lambda i, be: (be[i], 0, 0)),
                  pl.BlockSpec((1, f2 // 2, d), lambda i, be: (be[i], 0, 0)),
                  pl.BlockSpec((1, 1, d), lambda i, be: (be[i], 0, 0))],
        out_specs=pl.BlockSpec((MOE_BLOCK, D_HALF), lambda i, be: (i, 0)),
        scratch_shapes=[pltpu.VMEM((d, f2), BF16), pltpu.VMEM((f2 // 2, d), BF16),
                        pltpu.VMEM((MOE_BLOCK, f2 // 2), BF16)],
    )
    return pl.pallas_call(
        _moe_kernel,
        grid_spec=grid_spec,
        out_shape=jax.ShapeDtypeStruct((P, D_HALF), jnp.uint32),
        compiler_params=_params("arbitrary"),
    )(blk_e, xg, wgu, bgu, wd, bd)


def _combine_kernel(dcur_ref, dnext_ref, y_ref, lp_ref, x1_ref, gate_ref, g2_ref, fn_ref, o_ref, buf, sems):
    i = pl.program_id(0)
    n = pl.num_programs(0)
    slot = i & 1

    def gather(dref, s):
        def body(c, carry):
            d = pl.multiple_of(jnp.maximum(dref[0, 0, c], 0), RUN)
            dst = pl.multiple_of(RUN * c, RUN)
            pltpu.make_async_copy(y_ref.at[pl.ds(d, RUN), :], buf.at[s, pl.ds(dst, RUN), :], sems.at[s]).start()
            return carry

        lax.fori_loop(0, RL // RUN, body, 0, unroll=4)

    @pl.when(i == 0)
    def _():
        gather(dcur_ref, 0)

    @pl.when(i + 1 < n)
    def _():
        gather(dnext_ref, 1 - slot)

    pltpu.make_async_copy(y_ref.at[pl.ds(0, RL), :], buf.at[slot], sems.at[slot]).wait()

    g = _pair_matrix(lp_ref[...], gate_ref[...]).astype(BF16)
    halves = [_dot(g, yb) for yb in _unpack_bf16_pairs(buf[slot])]
    xo = x1_ref[...] + g2_ref[0] * jnp.concatenate(halves, axis=1)
    ms = jnp.mean(xo * xo, axis=-1, keepdims=True)
    o_ref[...] = xo * lax.rsqrt(ms + EPS) * fn_ref[...]


def _combine_call(cd, y, lp, x1, gates, g2, fn):
    T, d = x1.shape
    n = T // TB
    per_batch = n // g2.shape[0]
    return pl.pallas_call(
        _combine_kernel,
        grid=(n,),
        in_specs=[pl.BlockSpec((1, 1, CD_LANES), lambda i: (i, 0, 0), memory_space=pltpu.SMEM),
                  pl.BlockSpec((1, 1, CD_LANES), lambda i: (jnp.minimum(i + 1, n - 1), 0, 0),
                               memory_space=pltpu.SMEM),
                  pl.BlockSpec(memory_space=pl.ANY),
                  pl.BlockSpec((TB, LANES), lambda i: (i, 0)),
                  pl.BlockSpec((TB, d), lambda i: (i, 0)),
                  pl.BlockSpec((TB, LANES), lambda i: (i, 0)),
                  pl.BlockSpec((1, 1, d), lambda i: (i // per_batch, 0, 0)),
                  pl.BlockSpec((1, d), lambda i: (0, 0))],
        out_specs=pl.BlockSpec((TB, d), lambda i: (i, 0)),
        out_shape=jax.ShapeDtypeStruct((T, d), F32),
        scratch_shapes=[pltpu.VMEM((2, RL, D_HALF), jnp.uint32), pltpu.SemaphoreType.DMA((2,))],
        compiler_params=_params("arbitrary"),
    )(cd, cd, y, lp, x1, gates, g2, fn)


def _layer(x, c, ctx, c_ctx, w_mod, b_mod, norm1, w_in, gla_w_gk_up, gla_b_gk, gla_norm,
           ssd_conv_w, ssd_conv_b, ssd_dt_bias, ssd_A_log, ssd_D, ssd_norm, w_out,
           norm2, w_router, b_router, w_gate_up, b_gate_up, w_down, b_down, final_norm):
    bsz, L, d = x.shape
    lc = ctx.shape[1]
    assert lc == TB and L % TB == 0 and TB % GRID_W == 0

    cin = jnp.zeros((8, d), F32).at[:bsz].set(c).at[bsz].set(c_ctx)
    mod = _mod_call(cin, w_mod, b_mod.reshape(1, -1))[:bsz + 1]
    sh1, sc1, g1, sh2, sc2, g2 = [m.reshape(bsz + 1, 1, d) for m in jnp.split(mod, 6, axis=-1)]

    o = np.cumsum((0, GLA_QK, GLA_QK, GLA_V, GLA_V, GLA_RANK, SSD_INNER, SSD_CONV_DIM, SSD_HEADS))
    wq, wk, wv, wg, wlow, wz, wx, wdt = [w_in[:, int(a):int(b)] for a, b in zip(o[:-1], o[1:])]
    w_misc = jnp.concatenate([wlow, wdt, wdt, jnp.zeros((d, LANES - GLA_RANK - 2 * SSD_HEADS), F32)], axis=1)
    w_cat = jnp.concatenate([wq, wk, wv, wg, wz, wx, w_misc], axis=1).astype(BF16)
    wup = jnp.zeros((LANES, 2 * GLA_QK), F32).at[:GLA_RANK].set(
        jnp.concatenate([gla_w_gk_up[0], gla_w_gk_up[1]], axis=1)).astype(BF16)
    bup = jnp.concatenate([gla_b_gk[0], gla_b_gk[1]]).reshape(1, -1)
    dtb = jnp.zeros((1, LANES), F32).at[0, DT_F:DT_F + SSD_HEADS].set(ssd_dt_bias[0]) \
                                    .at[0, DT_B:DT_B + SSD_HEADS].set(ssd_dt_bias[1])
    q, k, v, g_all, z_all, xbc, ld, misc = _inproj_call(
        x, ctx, sh1, sc1, norm1.reshape(1, d), w_cat, wup, bup, dtb)

    xs, bc = _conv_call(xbc, ssd_conv_w.reshape(9, SSD_CONV_DIM), ssd_conv_b.reshape(1, -1), lc + L)

    o_f, o_b = _gla_call(q, k, v, ld, L)

    a_neg = -jnp.exp(ssd_A_log.astype(F32))
    a_f = jnp.zeros((1, LANES), F32).at[0, DT_F:DT_F + SSD_HEADS].set(a_neg[0])
    a_b = jnp.zeros((1, LANES), F32).at[0, DT_B:DT_B + SSD_HEADS].set(a_neg[1])
    dvec = jnp.repeat(ssd_D, SSD_HEADDIM).reshape(1, SSD_INNER)
    y_f, y_b = _ssd_call(xs, bc, misc, a_f, a_b, dvec, L)

    wr = jnp.zeros((d, LANES), F32).at[:, :N_EXPERTS].set(w_router)
    wr_hi = wr.astype(BF16)
    wr_lo = (wr - wr_hi.astype(F32)).astype(BF16)
    br = jnp.full((1, LANES), NEG_BIG, F32).at[0, :N_EXPERTS].set(b_router)
    x1, h2, eidx, gates, counts = _outproj_call(
        o_f, o_b, g_all, y_f, y_b, z_all, x, g1[:bsz], sh2[:bsz], sc2[:bsz],
        jnp.tile(gla_norm, GLA_HEADS).reshape(1, -1), ssd_norm.reshape(1, -1), w_out.astype(BF16),
        norm2.reshape(1, d), wr_hi, wr_lo, br)

    T = bsz * L
    cnt = counts[0, :N_EXPERTS].astype(jnp.int32)
    padded = ((cnt + MOE_BLOCK - 1) // MOE_BLOCK) * MOE_BLOCK
    pend = jnp.cumsum(padded)
    pstart = pend - padded
    max_rows = T * TOP_K + (T // TB) * N_EXPERTS * (RUN - 1)
    n_blocks = -(-max_rows // MOE_BLOCK) + N_EXPERTS
    blk_start = jnp.arange(n_blocks, dtype=jnp.int32) * MOE_BLOCK
    blk_e = jnp.minimum(jnp.sum(pend[None, :] <= blk_start[:, None], axis=1), N_EXPERTS - 1).astype(jnp.int32)
    n_used = (pend[-1] // MOE_BLOCK).astype(jnp.int32)
    blk_info = jnp.concatenate([blk_e, n_used[None]])
    n_tail = n_blocks - (T * TOP_K) // MOE_BLOCK
    tail = n_used + jnp.arange(n_tail, dtype=jnp.int32)
    fill_blocks = jnp.concatenate([
        jnp.where(padded > 0, pend // MOE_BLOCK - 1, -1),
        jnp.where(tail < n_blocks, tail, -1),
        n_blocks + jnp.arange(2 * SPILL_ROWS // MOE_BLOCK, dtype=jnp.int32)]).astype(jnp.int32)
    pstart_row = jnp.zeros((1, LANES), F32).at[0, :N_EXPERTS].set(pstart.astype(F32))
    lp, cd = _pos_call(eidx.reshape(T, LANES), pstart_row)

    xg = _dispatch_call(fill_blocks, cd, h2.reshape(T, d), lp, n_blocks * MOE_BLOCK)
    y = _moe_call(blk_info, xg, w_gate_up, b_gate_up.reshape(N_EXPERTS, 1, -1), w_down,
                  b_down.reshape(N_EXPERTS, 1, -1), n_blocks)
    out = _combine_call(cd, y, lp, x1.reshape(T, d), gates.reshape(T, LANES), g2[:bsz], final_norm.reshape(1, d))
    return out.reshape(bsz, L, d)


def kernel(x, c, ctx, c_ctx, w_mod, b_mod, norm1, w_in, gla_w_gk_up, gla_b_gk, gla_norm, ssd_conv_w, ssd_conv_b, ssd_dt_bias, ssd_A_log, ssd_D, ssd_norm, w_out, norm2, w_router, b_router, w_gate_up, b_gate_up, w_down, b_down, final_norm):
    assert w_mod.shape[0] == 1, "single-layer kernel"
    return _layer(x, c, ctx, c_ctx, w_mod[0], b_mod[0], norm1[0], w_in[0], gla_w_gk_up[0], gla_b_gk[0],
                  gla_norm[0], ssd_conv_w[0], ssd_conv_b[0], ssd_dt_bias[0], ssd_A_log[0], ssd_D[0],
                  ssd_norm[0], w_out[0], norm2[0], w_router[0], b_router[0], w_gate_up[0], b_gate_up[0],
                  w_down[0], b_down[0], final_norm)
```

```python
import functools

import numpy as np
import jax
import jax.numpy as jnp
from jax import lax
from jax.experimental import pallas as pl
from jax.experimental.pallas import tpu as pltpu

F32 = jnp.float32
BF16 = jnp.bfloat16

EPS = 1e-6
GRID_W = 64
GLA_HEADS = 4
GLA_DK = 64
GLA_DV = 128
GLA_QK = GLA_HEADS * GLA_DK
GLA_V = GLA_HEADS * GLA_DV
GLA_RANK = 16
GLA_GATE_NORM = 16.0
SSD_HEADDIM = 64
SSD_INNER = 512
SSD_HEADS = 8
SSD_GROUPS = 2
SSD_HPG = 4
SSD_STATE = 128
SSD_CONV_DIM = 1024
N_EXPERTS = 32
TOP_K = 4
D_FF = 1024
SWIGLU_LIMIT = 7.0
SWIGLU_ALPHA = 1.702
MOE_BLOCK = 512

TB = 256
GLA_C = 64
GLA_STAGE_GROUP = 4
SSD_C = 128
SSD_STAGE_GROUP = 1
LANES = 128
EXP_CLAMP = 80.0
DT_F = 16
DT_B = 24
NEG_BIG = -1e30
RUN = 8
RL = 1280
CD_LANES = 256
SPILL_ROWS = -(-RL // MOE_BLOCK) * MOE_BLOCK
D_HALF = 512
VMEM_LIMIT = 56 * 1024 * 1024


def _dot(a, b):
    return jnp.dot(a, b, preferred_element_type=F32)


def _dot_nt(a, b):
    return lax.dot_general(a, b, (((1,), (1,)), ((), ())), preferred_element_type=F32)


def _dot_tn(a, b):
    return lax.dot_general(a, b, (((0,), (0,)), ((), ())), preferred_element_type=F32)


def _split3(a):
    hi = a.astype(BF16)
    r1 = a - hi.astype(F32)
    mid = r1.astype(BF16)
    lo = (r1 - mid.astype(F32)).astype(BF16)
    return hi, mid, lo


def _dot_exact_r(m, a):
    hi, mid, lo = _split3(a)
    return _dot(m, hi) + _dot(m, mid) + _dot(m, lo)


def _dot_hilo_r(m, a):
    hi = a.astype(BF16)
    lo = (a - hi.astype(F32)).astype(BF16)
    return _dot(m, hi) + _dot(m, lo)


def _dot_hilo_l(a, m2):
    hi = a.astype(BF16)
    lo = (a - hi.astype(F32)).astype(BF16)
    return _dot(jnp.concatenate([hi, lo], axis=1), m2)


def _sigmoid(x):
    return 1.0 / (1.0 + jnp.exp(-x))


def _softplus(x):
    return jnp.maximum(x, 0.0) + jnp.log1p(jnp.exp(-jnp.abs(x)))


def _params(*sem):
    return pltpu.CompilerParams(dimension_semantics=sem, vmem_limit_bytes=VMEM_LIMIT)


def _mod_kernel(c_ref, w_ref, b_ref, o_ref):
    c = c_ref[...]
    s = c * _sigmoid(c)
    s_hi = s.astype(BF16)
    s_lo = (s - s_hi.astype(F32)).astype(BF16)
    w = w_ref[...]
    w_hi = w.astype(BF16)
    w_lo = (w - w_hi.astype(F32)).astype(BF16)
    o_ref[...] = _dot(s_hi, w_hi) + _dot(s_lo, w_hi) + _dot(s_hi, w_lo) + b_ref[...]


def _mod_call(cin, w, b):
    rows, d = cin.shape
    n = w.shape[1]
    tn = 1536
    return pl.pallas_call(
        _mod_kernel,
        grid=(n // tn,),
        in_specs=[pl.BlockSpec((rows, d), lambda i: (0, 0)),
                  pl.BlockSpec((d, tn), lambda i: (0, i)),
                  pl.BlockSpec((1, tn), lambda i: (0, i))],
        out_specs=pl.BlockSpec((rows, tn), lambda i: (0, i)),
        out_shape=jax.ShapeDtypeStruct((rows, n), F32),
        compiler_params=_params("arbitrary"),
    )(cin, w, b)


_C_Q, _C_K, _C_V, _C_G, _C_Z, _C_X, _C_M, _C_END = 0, 256, 512, 1024, 1536, 2048, 3072, 3200


def _inproj_kernel(xa_ref, xb_ref, ctx_ref, shl_ref, scl_ref, shc_ref, scc_ref, n1_ref, w_ref, wup_ref, bup_ref,
                   dtb_ref, q_ref, k_ref, v_ref, g_ref, z_ref, xbc_ref, ld_ref, misc_ref, h_scr):
    j = pl.program_id(1)

    def normmod(xv, sh_ref, sc_ref):
        ms = jnp.mean(xv * xv, axis=-1, keepdims=True)
        y = xv * lax.rsqrt(ms + EPS) * n1_ref[...]
        return (y * (1.0 + sc_ref[0]) + sh_ref[0]).astype(BF16)

    @pl.when(j == 0)
    def _():
        h_scr[0:TB, :] = normmod(ctx_ref[0], shc_ref, scc_ref)

    @pl.when(j > 0)
    def _():
        h_scr[0:TB, :] = normmod(xa_ref[0], shl_ref, scl_ref)

    h_scr[TB:2 * TB, :] = normmod(xb_ref[0], shl_ref, scl_ref)
    h = h_scr[...]

    def mm(lo, hi):
        return _dot(h, w_ref[:, lo:hi])

    q_ref[0] = (mm(_C_Q, _C_K) * (GLA_DK ** -0.5)).astype(BF16)
    k_ref[0] = mm(_C_K, _C_V).astype(BF16)
    v_ref[0] = mm(_C_V, _C_G).astype(BF16)
    g_ref[0] = mm(_C_G, _C_Z)
    z_ref[0] = mm(_C_Z, _C_X)
    xbc_ref[0] = mm(_C_X, _C_M)
    m = mm(_C_M, _C_END)
    zz = _dot(m.astype(BF16), wup_ref[...]) + bup_ref[...]
    ld_ref[0] = -_softplus(-zz) * (1.0 / GLA_GATE_NORM)
    misc_ref[0] = _softplus(m + dtb_ref[...])


def _inproj_call(x, ctx, sh1, sc1, n1, w_cat, wup, bup, dtb):
    bsz, L, d = x.shape
    nx = L // TB
    nj = -(-(nx + 1) // 2)
    tok = lambda n: pl.BlockSpec((1, 2 * TB, n), lambda b, j: (b, j, 0))
    const = lambda a: pl.BlockSpec(a.shape, lambda b, j: (0,) * a.ndim)
    xblk = lambda off: pl.BlockSpec((1, TB, d), lambda b, j: (b, jnp.clip(2 * j + off, 0, nx - 1), 0))
    mod_lat = pl.BlockSpec((1, 1, d), lambda b, j: (b, 0, 0))
    mod_ctx = pl.BlockSpec((1, 1, d), lambda b, j: (bsz, 0, 0))
    outs = [(GLA_QK, BF16), (GLA_QK, BF16), (GLA_V, BF16), (GLA_V, F32), (SSD_INNER, F32),
            (SSD_CONV_DIM, F32), (2 * GLA_QK, F32), (LANES, F32)]
    return pl.pallas_call(
        _inproj_kernel,
        grid=(bsz, nj),
        in_specs=[xblk(-1), xblk(0), pl.BlockSpec((1, TB, d), lambda b, j: (b, 0, 0)),
                  mod_lat, mod_lat, mod_ctx, mod_ctx,
                  const(n1), const(w_cat), const(wup), const(bup), const(dtb)],
        out_specs=[tok(n) for n, _ in outs],
        out_shape=[jax.ShapeDtypeStruct((bsz, nj * 2 * TB, n), dt) for n, dt in outs],
        scratch_shapes=[pltpu.VMEM((2 * TB, d), BF16)],
        compiler_params=_params("arbitrary", "arbitrary"),
    )(x, x, ctx, sh1, sc1, sh1, sc1, n1, w_cat, wup, bup, dtb)


_EXT_PAD = 8
_EXT_BASE = _EXT_PAD + GRID_W
_EXT_ROWS = 2 * _EXT_PAD + 2 * GRID_W + TB


def _conv_kernel(prev_ref, cur_ref, next_ref, w_ref, b_ref, xs_ref, bc_ref, ext):
    j = pl.program_id(1)
    nj = pl.num_programs(1)
    is_ctx = j == 0
    zpad = jnp.zeros((_EXT_PAD, SSD_CONV_DIM), F32)
    ext[0:_EXT_PAD, :] = zpad
    ext[_EXT_ROWS - _EXT_PAD:_EXT_ROWS, :] = zpad
    ext[_EXT_PAD:_EXT_BASE, :] = jnp.where(j >= 2, prev_ref[0], 0.0)
    ext[_EXT_BASE:_EXT_BASE + TB, :] = cur_ref[0]
    ext[_EXT_BASE + TB:_EXT_BASE + TB + GRID_W, :] = jnp.where(
        jnp.logical_and(j >= 1, j <= nj - 2), next_ref[0], 0.0)

    t = lax.broadcasted_iota(jnp.int32, (TB, LANES), 0)
    col = t & (GRID_W - 1)
    ok_l = jnp.where(is_ctx, t, col) >= 1
    ok_r = jnp.where(is_ctx, t - (TB - GRID_W), col) <= GRID_W - 2
    lat = jnp.where(is_ctx, 0.0, 1.0)

    for c in range(SSD_CONV_DIM // LANES):
        lo, hi = c * LANES, (c + 1) * LANES
        acc = jnp.zeros((TB, LANES), F32)
        for dr in (-1, 0, 1):
            for dc in (-1, 0, 1):
                start = _EXT_BASE + GRID_W * dr + dc
                tap = ext[start:start + TB, lo:hi]
                wi = 3 * (dr + 1) + (dc + 1)
                wv = w_ref[wi:wi + 1, lo:hi]
                if dr != 0:
                    wv = wv * lat
                if dc == -1:
                    tap = jnp.where(ok_l, tap, 0.0)
                elif dc == 1:
                    tap = jnp.where(ok_r, tap, 0.0)
                acc = acc + tap * wv
        y = acc + b_ref[:, lo:hi]
        y = y * _sigmoid(y)
        if c < SSD_INNER // LANES:
            xs_ref[0, :, lo:hi] = y
        else:
            bc_ref[0, :, lo - SSD_INNER:hi - SSD_INNER] = y.astype(BF16)


def _conv_call(xbc, w9, bias, ls):
    bsz, _, ch = xbc.shape
    nj = ls // TB
    rpb = TB // GRID_W
    nrow = ls // GRID_W
    return pl.pallas_call(
        _conv_kernel,
        grid=(bsz, nj),
        in_specs=[pl.BlockSpec((1, GRID_W, ch), lambda b, j: (b, jnp.maximum(rpb * j - 1, 0), 0)),
                  pl.BlockSpec((1, TB, ch), lambda b, j: (b, j, 0)),
                  pl.BlockSpec((1, GRID_W, ch), lambda b, j: (b, jnp.minimum(rpb * j + rpb, nrow - 1), 0)),
                  pl.BlockSpec((9, ch), lambda b, j: (0, 0)),
                  pl.BlockSpec((1, ch), lambda b, j: (0, 0))],
        out_specs=[pl.BlockSpec((1, TB, SSD_INNER), lambda b, j: (b, j, 0)),
                   pl.BlockSpec((1, TB, ch - SSD_INNER), lambda b, j: (b, j, 0))],
        out_shape=[jax.ShapeDtypeStruct((bsz, ls, SSD_INNER), F32),
                   jax.ShapeDtypeStruct((bsz, ls, ch - SSD_INNER), BF16)],
        scratch_shapes=[pltpu.VMEM((_EXT_ROWS, ch), F32)],
        compiler_params=_params("arbitrary", "arbitrary"),
    )(xbc, xbc, xbc, w9, bias)


def _fwd_blk(s):
    return s


def _bwd_blk(s, ns):
    return jnp.where(s == 0, 0, ns - s)


def _scan_batch(bsz):
    return 2 if bsz % 2 == 0 else 1


class _GlaChunk:
    def __init__(self, q, k, v, la, st_ref, tri_m, fwd, store):
        self.q, self.k, self.v, self.la, self.st_ref, self.tri_m, self.fwd, self.store = (
            q, k, v, la, st_ref, tri_m, fwd, store)

    def stage_sums(self):
        self.b = _dot_hilo_r(self.tri_m, self.la)

    def stage_factors(self):
        C = GLA_C
        b = self.b
        self.bt = b[C - 1:C, :] if self.fwd else b[0:1, :]
        r = 0.5 * self.bt
        self.er = jnp.exp(r)
        self.qt = (self.q.astype(F32) * jnp.exp(jnp.minimum(b - r, EXP_CLAMP))).astype(BF16)
        kt = (self.k.astype(F32) * jnp.exp(jnp.minimum(r - b, EXP_CLAMP))).astype(BF16)
        head_k = lax.broadcasted_iota(jnp.int32, (C, GLA_QK), 1) >> 6
        zero = jnp.zeros_like(kt)
        self.kh = [jnp.where(head_k == h, kt, zero) for h in range(GLA_HEADS)]
        self.qh = [jnp.where(head_k == h, self.qt, zero) for h in range(GLA_HEADS)]

    def stage_products(self):
        v = self.v
        kcat = jnp.concatenate(self.kh, axis=0)
        self.sc = _dot_nt(self.qt, kcat)
        vcat = jnp.concatenate([v[:, h * GLA_DV:(h + 1) * GLA_DV] for h in range(GLA_HEADS)], axis=0)
        self.u = _dot_tn(vcat, kcat) * self.er

    def stage_mask(self):
        C = GLA_C
        ii = lax.broadcasted_iota(jnp.int32, (C, GLA_HEADS * C), 0)
        jj = lax.broadcasted_iota(jnp.int32, (C, GLA_HEADS * C), 1) & (C - 1)
        causal = (jj <= ii) if self.fwd else (jj >= ii)
        self.p = jnp.where(causal, self.sc, 0.0).astype(BF16)
        v = self.v
        head_v = lax.broadcasted_iota(jnp.int32, (C, GLA_V), 1) >> 7
        self.vst = jnp.concatenate([jnp.where(head_v == h, v, jnp.zeros_like(v)) for h in range(GLA_HEADS)], axis=0)

    def stage_intra(self):
        self.o = _dot(self.p, self.vst)

    def stage_inter(self):
        st = self.st_ref[...]
        ster = (st * self.er).astype(BF16)
        self.st_ref[...] = st * jnp.exp(self.bt) + self.u
        res = _dot_nt(jnp.concatenate(self.qh, axis=0), ster)
        self.inter = jnp.concatenate([res[h * GLA_C:(h + 1) * GLA_C, :] for h in range(GLA_HEADS)], axis=1)

    def stage_out(self):
        self.store(self.o + self.inter)


def _gla_kernel(qf_ref, kf_ref, vf_ref, lf_ref, qb_ref, kb_ref, vb_ref, lb_ref, trif_ref, trib_ref,
                of_ref, ob_ref, stf, stb):
    s = pl.program_id(1)

    @pl.when(s == 0)
    def _():
        stf[...] = jnp.zeros_like(stf)
        stb[...] = jnp.zeros_like(stb)

    nsub = TB // GLA_C

    def store_to(ref, bb, sl):
        def store(val):
            ref[bb, sl, :] = val
        return store

    group = GLA_STAGE_GROUP
    for g0 in range(0, nsub, group):
        steps = []
        for i in range(g0, g0 + group):
            sf = pl.ds(i * GLA_C, GLA_C)
            sb = pl.ds((nsub - 1 - i) * GLA_C, GLA_C)
            chunks = []
            for bb in range(qf_ref.shape[0]):
                chunks.append(_GlaChunk(qf_ref[bb, sf, :], kf_ref[bb, sf, :], vf_ref[bb, sf, :], lf_ref[bb, sf, :],
                                        stf.at[bb], trif_ref[...], True, store_to(of_ref, bb, sf)))
                chunks.append(_GlaChunk(qb_ref[bb, sb, :], kb_ref[bb, sb, :], vb_ref[bb, sb, :], lb_ref[bb, sb, :],
                                        stb.at[bb], trib_ref[...], False, store_to(ob_ref, bb, sb)))
            steps.append(chunks)
        for stage in ("stage_sums", "stage_factors", "stage_products", "stage_mask", "stage_intra"):
            for chunks in steps:
                for ch in chunks:
                    getattr(ch, stage)()
        for chunks in steps:
            for stage in ("stage_inter", "stage_out"):
                for ch in chunks:
                    getattr(ch, stage)()


def _gla_call(q, k, v, ld, L):
    bsz = q.shape[0]
    nx = L // TB
    ns = nx + 1
    trif = jnp.asarray(np.tril(np.ones((GLA_C, GLA_C), np.float32)), BF16)
    trib = jnp.asarray(np.triu(np.ones((GLA_C, GLA_C), np.float32)), BF16)
    nb = _scan_batch(bsz)
    f = lambda n, lane=0: pl.BlockSpec((nb, TB, n), lambda b, s: (b, _fwd_blk(s), lane))
    r = lambda n, lane=0: pl.BlockSpec((nb, TB, n), lambda b, s: (b, _bwd_blk(s, ns), lane))
    tri = pl.BlockSpec((GLA_C, GLA_C), lambda b, s: (0, 0))
    return pl.pallas_call(
        _gla_kernel,
        grid=(bsz // nb, ns),
        in_specs=[f(GLA_QK), f(GLA_QK), f(GLA_V), f(GLA_QK, 0),
                  r(GLA_QK), r(GLA_QK), r(GLA_V), r(GLA_QK, 1), tri, tri],
        out_specs=[pl.BlockSpec((nb, TB, GLA_V), lambda b, s: (b, jnp.maximum(s - 1, 0), 0)),
                   pl.BlockSpec((nb, TB, GLA_V), lambda b, s: (b, jnp.where(s == 0, nx - 1, nx - s), 0))],
        out_shape=[jax.ShapeDtypeStruct((bsz, L, GLA_V), F32)] * 2,
        scratch_shapes=[pltpu.VMEM((nb, GLA_DV, GLA_QK), F32)] * 2,
        compiler_params=_params("arbitrary", "arbitrary"),
    )(q, k, v, ld, q, k, v, ld, trif, trib)


class _SsdChunk:
    def __init__(self, xs, bc, dtm, avec, dvec, st_ref, tri_m, e_m, base, fwd, store):
        self.xs, self.bc, self.dtm, self.avec, self.dvec, self.st_ref = xs, bc, dtm, avec, dvec, st_ref
        self.tri_m, self.e_m, self.base, self.fwd, self.store = tri_m, e_m, base, fwd, store

    def stage_sums(self):
        self.dt_exp = _dot_hilo_l(self.dtm, self.e_m)
        self.acum = _dot_exact_r(self.tri_m, self.dtm * self.avec)

    def stage_expand(self):
        self.acum_exp = _dot_hilo_l(self.acum, self.e_m)
        self.acum_t = self.acum.T
        self.xdt = self.xs * self.dt_exp
        bc = self.bc
        self.bg = [bc[:, 128 * g:128 * (g + 1)] for g in range(SSD_GROUPS)]
        self.cg = [bc[:, 256 + 128 * g:256 + 128 * (g + 1)] for g in range(SSD_GROUPS)]
        self.cb = [_dot_nt(self.cg[g], self.bg[g]) for g in range(SSD_GROUPS)]

    def stage_decay(self):
        C = SSD_C
        ii = lax.broadcasted_iota(jnp.int32, (C, C), 0)
        jj = lax.broadcasted_iota(jnp.int32, (C, C), 1)
        tri = (jj <= ii) if self.fwd else (jj >= ii)
        self.ms = []
        for g in range(SSD_GROUPS):
            for rr in range(SSD_HPG):
                ln = self.base + SSD_HPG * g + rr
                diff = self.acum[:, ln:ln + 1] - self.acum_t[ln:ln + 1, :]
                seg = jnp.where(tri, jnp.exp(jnp.minimum(diff, 0.0)), 0.0)
                self.ms.append((self.cb[g] * seg).astype(BF16))
        ae = self.acum_exp
        self.al_exp = ae[C - 1:C, :] if self.fwd else ae[0:1, :]
        self.xw = (self.xdt * jnp.exp(self.al_exp - ae)).astype(BF16)
        xdt_b = self.xdt.astype(BF16)
        head = lax.broadcasted_iota(jnp.int32, (C, SSD_INNER), 1) >> 6
        zero = jnp.zeros_like(xdt_b)
        self.xh = [jnp.where(head == h, xdt_b, zero) for h in range(SSD_HEADS)]

    def stage_products(self):
        self.yg, self.ug = [], []
        for g in range(SSD_GROUPS):
            gl, gh = 256 * g, 256 * (g + 1)
            yg = _dot(self.ms[SSD_HPG * g], self.xh[SSD_HPG * g][:, gl:gh])
            for rr in range(1, SSD_HPG):
                yg = yg + _dot(self.ms[SSD_HPG * g + rr], self.xh[SSD_HPG * g + rr][:, gl:gh])
            self.yg.append(yg)
            self.ug.append(_dot_tn(self.bg[g], self.xw[:, gl:gh]))

    def stage_state(self):
        ys = []
        for g in range(SSD_GROUPS):
            gl, gh = 256 * g, 256 * (g + 1)
            sg = self.st_ref[g]
            yoff = _dot(self.cg[g], sg.astype(BF16)) * jnp.exp(self.acum_exp[:, gl:gh])
            self.st_ref[g] = sg * jnp.exp(self.al_exp[:, gl:gh]) + self.ug[g]
            ys.append(self.yg[g] + yoff)
        y = jnp.concatenate(ys, axis=1)
        if self.dvec is not None:
            y = y + self.dvec * self.xs
        self.store(y)


def _ssd_kernel(xf_ref, bcf_ref, mf_ref, xb_ref, bcb_ref, mb_ref, af_ref, ab_ref, d_ref,
                trif_ref, trib_ref, ef_ref, eb_ref, yf_ref, yb_ref, stf, stb):
    s = pl.program_id(1)

    @pl.when(s == 0)
    def _():
        stf[...] = jnp.zeros_like(stf)
        stb[...] = jnp.zeros_like(stb)

    nsub = TB // SSD_C

    def store_to(ref, bb, sl):
        def store(val):
            ref[bb, sl, :] = val
        return store

    steps = []
    for i in range(nsub):
        sf = pl.ds(i * SSD_C, SSD_C)
        sb = pl.ds((nsub - 1 - i) * SSD_C, SSD_C)
        chunks = []
        for bb in range(xf_ref.shape[0]):
            chunks.append(_SsdChunk(xf_ref[bb, sf, :], bcf_ref[bb, sf, :], mf_ref[bb, sf, :], af_ref[...],
                                    d_ref[...], stf.at[bb], trif_ref[...], ef_ref[...], DT_F, True,
                                    store_to(yf_ref, bb, sf)))
            chunks.append(_SsdChunk(xb_ref[bb, sb, :], bcb_ref[bb, sb, :], mb_ref[bb, sb, :], ab_ref[...],
                                    None, stb.at[bb], trib_ref[...], eb_ref[...], DT_B, False,
                                    store_to(yb_ref, bb, sb)))
        steps.append(chunks)
    for g0 in range(0, nsub, SSD_STAGE_GROUP):
        group = steps[g0:g0 + SSD_STAGE_GROUP]
        for stage in ("stage_sums", "stage_expand", "stage_decay", "stage_products"):
            for chunks in group:
                for ch in chunks:
                    getattr(ch, stage)()
        for chunks in group:
            for ch in chunks:
                ch.stage_state()


def _expand_matrix(base):
    e = np.zeros((LANES, SSD_INNER), np.float32)
    for h in range(SSD_HEADS):
        e[base + h, SSD_HEADDIM * h:SSD_HEADDIM * (h + 1)] = 1.0
    return jnp.asarray(np.concatenate([e, e], axis=0), BF16)


def _ssd_call(xs, bc, misc, a_f, a_b, dvec, L):
    bsz, ls, _ = xs.shape
    ns = ls // TB
    nx = L // TB
    trif = jnp.asarray(np.tril(np.ones((SSD_C, SSD_C), np.float32)), BF16)
    trib = jnp.asarray(np.triu(np.ones((SSD_C, SSD_C), np.float32)), BF16)
    ef, eb = _expand_matrix(DT_F), _expand_matrix(DT_B)
    nb = _scan_batch(bsz)
    f = lambda n: pl.BlockSpec((nb, TB, n), lambda b, s: (b, _fwd_blk(s), 0))
    r = lambda n: pl.BlockSpec((nb, TB, n), lambda b, s: (b, _bwd_blk(s, ns), 0))
    const = lambda a: pl.BlockSpec(a.shape, lambda b, s: (0,) * a.ndim)
    return pl.pallas_call(
        _ssd_kernel,
        grid=(bsz // nb, ns),
        in_specs=[f(SSD_INNER), f(512), f(LANES), r(SSD_INNER), r(512), r(LANES),
                  const(a_f), const(a_b), const(dvec), const(trif), const(trib), const(ef), const(eb)],
        out_specs=[pl.BlockSpec((nb, TB, SSD_INNER), lambda b, s: (b, jnp.maximum(s - 1, 0), 0)),
                   pl.BlockSpec((nb, TB, SSD_INNER), lambda b, s: (b, jnp.where(s == 0, nx - 1, nx - s), 0))],
        out_shape=[jax.ShapeDtypeStruct((bsz, L, SSD_INNER), F32)] * 2,
        scratch_shapes=[pltpu.VMEM((nb, SSD_GROUPS, SSD_STATE, SSD_HPG * SSD_HEADDIM), F32)] * 2,
        compiler_params=_params("arbitrary", "arbitrary"),
    )(xs, bc, misc, xs, bc, misc, a_f, a_b, dvec, trif, trib, ef, eb)


def _outproj_kernel(of_ref, ob_ref, ga_ref, gb_ref, yf_ref, yb_ref, za_ref, zb_ref, x_ref, g1_ref, sh2_ref, sc2_ref,
                    gn_ref, sn_ref, wo_ref, n2_ref, wrh_ref, wrl_ref, br_ref,
                    x1_ref, h2_ref, eidx_ref, gate_ref, cnt_ref, mix_s, hl_s):
    first = jnp.logical_and(pl.program_id(0) == 0, pl.program_id(1) == 0)

    @pl.when(first)
    def _():
        cnt_ref[...] = jnp.zeros_like(cnt_ref)

    halves = [(pl.ds(0, TB), ga_ref, za_ref), (pl.ds(TB, TB), gb_ref, zb_ref)]

    for rows, g_ref, z_ref in halves:
        o = of_ref[0, rows, :] + ob_ref[0, rows, :]
        gg = g_ref[0]
        for h in range(GLA_HEADS):
            lo, hi = GLA_DV * h, GLA_DV * (h + 1)
            oh = o[:, lo:hi]
            ms = jnp.mean(oh * oh, axis=-1, keepdims=True)
            gh = gg[:, lo:hi]
            mix_s[rows, lo:hi] = (oh * lax.rsqrt(ms + EPS) * gn_ref[:, lo:hi] * (gh * _sigmoid(gh))).astype(BF16)
        zz = z_ref[0]
        u = (yf_ref[0, rows, :] + yb_ref[0, rows, :]) * (zz * _sigmoid(zz))
        gw = SSD_INNER // SSD_GROUPS
        for g in range(SSD_GROUPS):
            lo, hi = gw * g, gw * (g + 1)
            ug = u[:, lo:hi]
            ms = jnp.mean(ug * ug, axis=-1, keepdims=True)
            mix_s[rows, GLA_V + lo:GLA_V + hi] = (ug * lax.rsqrt(ms + EPS) * sn_ref[:, lo:hi]).astype(BF16)

    x1_ref[0] = x_ref[0] + g1_ref[0] * _dot(mix_s[...], wo_ref[...])

    for rows, _, _ in halves:
        x1 = x1_ref[0, rows, :]
        ms = jnp.mean(x1 * x1, axis=-1, keepdims=True)
        h2 = (x1 * lax.rsqrt(ms + EPS) * n2_ref[...]) * (1.0 + sc2_ref[0]) + sh2_ref[0]
        h_hi = h2.astype(BF16)
        h2_ref[0, rows, :] = h_hi
        hl_s[rows, :] = (h2 - h_hi.astype(F32)).astype(BF16)

    h_hi = h2_ref[0]
    logits = (_dot(h_hi, wrh_ref[...]) + _dot(hl_s[...], wrh_ref[...]) + _dot(h_hi, wrl_ref[...])) + br_ref[...]

    lane = lax.broadcasted_iota(jnp.int32, (TB, LANES), 1).astype(F32)
    st = [dict(work=logits[i * TB:(i + 1) * TB, :], eidx=jnp.full((TB, LANES), -1.0, F32),
               gates=jnp.zeros((TB, LANES), F32), sel=jnp.zeros((TB, LANES), F32), m0=None,
               den=jnp.zeros((TB, 1), F32)) for i in range(2)]
    for kk in range(TOP_K):
        for t in st:
            t["m"] = jnp.max(t["work"], axis=-1, keepdims=True)
        for t in st:
            t["idx"] = jnp.min(jnp.where(t["work"] == t["m"], lane, float(LANES)), axis=-1, keepdims=True)
        for t in st:
            hit = lane == t["idx"]
            if t["m0"] is None:
                t["m0"] = t["m"]
            e = jnp.exp(t["m"] - t["m0"])
            t["den"] = t["den"] + e
            t["eidx"] = jnp.where(lane == float(kk), t["idx"], t["eidx"])
            t["gates"] = jnp.where(lane == float(kk), e, t["gates"])
            t["sel"] = jnp.where(hit, 1.0, t["sel"])
            t["work"] = jnp.where(hit, NEG_BIG, t["work"])
    for (rows, _, _), t in zip(halves, st):
        eidx_ref[0, rows, :] = t["eidx"].astype(jnp.int32)
        gate_ref[0, rows, :] = t["gates"] / t["den"]
        cnt = jnp.sum(t["sel"], axis=0, keepdims=True)
        cnt_ref[...] += jnp.floor((cnt + (RUN - 1.0)) * (1.0 / RUN)) * RUN


def _outproj_call(o_f, o_b, g_all, y_f, y_b, z_all, x, g1, sh2, sc2, gn, sn, wo, n2, wr_hi, wr_lo, br):
    bsz, L, d = x.shape
    nj = L // (2 * TB)
    tok = lambda n: pl.BlockSpec((1, 2 * TB, n), lambda b, j: (b, j, 0))
    tok_off = lambda n, half: pl.BlockSpec((1, TB, n), lambda b, j: (b, 2 * j + 1 + half, 0))
    const = lambda a: pl.BlockSpec(a.shape, lambda b, j: (0,) * a.ndim)
    mod = pl.BlockSpec((1, 1, d), lambda b, j: (b, 0, 0))
    return pl.pallas_call(
        _outproj_kernel,
        grid=(bsz, nj),
        in_specs=[tok(GLA_V), tok(GLA_V), tok_off(GLA_V, 0), tok_off(GLA_V, 1),
                  tok(SSD_INNER), tok(SSD_INNER), tok_off(SSD_INNER, 0), tok_off(SSD_INNER, 1),
                  tok(d), mod, mod, mod, const(gn), const(sn), const(wo), const(n2),
                  const(wr_hi), const(wr_lo), const(br)],
        out_specs=[tok(d), tok(d), tok(LANES), tok(LANES), pl.BlockSpec((1, LANES), lambda b, j: (0, 0))],
        out_shape=[jax.ShapeDtypeStruct((bsz, L, d), F32), jax.ShapeDtypeStruct((bsz, L, d), BF16),
                   jax.ShapeDtypeStruct((bsz, L, LANES), jnp.int32), jax.ShapeDtypeStruct((bsz, L, LANES), F32),
                   jax.ShapeDtypeStruct((1, LANES), F32)],
        scratch_shapes=[pltpu.VMEM((2 * TB, d), BF16), pltpu.VMEM((2 * TB, d), BF16)],
        compiler_params=_params("arbitrary", "arbitrary"),
    )(o_f, o_b, g_all, g_all, y_f, y_b, z_all, z_all, x, g1, sh2, sc2, gn, sn, wo, n2, wr_hi, wr_lo, br)


def _pos_kernel(eidx_ref, pstart_ref, lst_ref, ust_ref, lp_ref, cd_ref, carry):
    @pl.when(pl.program_id(0) == 0)
    def _():
        carry[...] = pstart_ref[...]

    eidx = eidx_ref[...]
    lane = lax.broadcasted_iota(jnp.int32, (TB, LANES), 1)
    hits = [lane == eidx[:, kk:kk + 1] for kk in range(TOP_K)]
    sel = jnp.zeros((TB, LANES), F32)
    for hmask in hits:
        sel = jnp.where(hmask, 1.0, sel)
    cnt = jnp.sum(sel, axis=0, keepdims=True)
    run = jnp.floor((cnt + (RUN - 1.0)) * (1.0 / RUN)) * RUN
    rank = _dot(lst_ref[...], sel.astype(BF16))
    loff = _dot(jnp.broadcast_to(run, (8, LANES)).astype(BF16), ust_ref[...])[0:1]
    pos = loff + rank
    lp = jnp.zeros((TB, LANES), jnp.int32)
    for kk, hmask in enumerate(hits):
        lk = jnp.sum(jnp.where(hmask, pos, 0.0), axis=-1, keepdims=True)
        lp = jnp.where(lane == kk, lk.astype(jnp.int32), lp)
    lp_ref[...] = lp

    base = carry[...]
    eye = lax.broadcasted_iota(jnp.int32, (LANES, LANES), 0) == lax.broadcasted_iota(jnp.int32, (LANES, LANES), 1)
    col = lambda v: jnp.sum(jnp.where(eye, v, 0.0), axis=1, keepdims=True)
    loff_c, run_c, shift_c = col(loff), col(run), col(base - loff)
    row0 = (lax.broadcasted_iota(jnp.int32, (LANES, CD_LANES), 1) * RUN).astype(F32)
    inside = jnp.where(row0 >= loff_c, 1.0, 0.0) * jnp.where(row0 < loff_c + run_c, 1.0, 0.0)
    valid = jnp.sum(inside, axis=0, keepdims=True)
    dest = jnp.sum(inside * shift_c, axis=0, keepdims=True) + row0[0:1]
    cd_ref[0] = jnp.where(valid > 0.0, dest, -1.0).astype(jnp.int32)
    carry[...] = base + run


def _pos_call(eidx, pstart):
    T = eidx.shape[0]
    lst = jnp.asarray(np.tril(np.ones((TB, TB), np.float32), -1), BF16)
    ust = jnp.asarray(np.triu(np.ones((LANES, LANES), np.float32), 1), BF16)
    return pl.pallas_call(
        _pos_kernel,
        grid=(T // TB,),
        in_specs=[pl.BlockSpec((TB, LANES), lambda i: (i, 0)),
                  pl.BlockSpec((1, LANES), lambda i: (0, 0)),
                  pl.BlockSpec((TB, TB), lambda i: (0, 0)),
                  pl.BlockSpec((LANES, LANES), lambda i: (0, 0))],
        out_specs=[pl.BlockSpec((TB, LANES), lambda i: (i, 0)),
                   pl.BlockSpec((1, 1, CD_LANES), lambda i: (i, 0, 0))],
        out_shape=[jax.ShapeDtypeStruct((T, LANES), jnp.int32),
                   jax.ShapeDtypeStruct((T // TB, 1, CD_LANES), jnp.int32)],
        scratch_shapes=[pltpu.VMEM((1, LANES), F32)],
        compiler_params=_params("arbitrary"),
    )(eidx, pstart, lst, ust)


def _pair_matrix(lp, weights):
    j = lax.broadcasted_iota(jnp.int32, (TB, RL), 1)
    m = jnp.zeros((TB, RL), F32)
    for kk in range(TOP_K):
        w = 1.0 if weights is None else weights[:, kk:kk + 1]
        m = m + jnp.where(j == lp[:, kk:kk + 1], w, 0.0)
    return m


def _pack_bf16_pairs(v):
    bits = pltpu.bitcast(v.astype(BF16).astype(F32), jnp.uint32)
    return bits[:, D_HALF:] | (bits[:, :D_HALF] >> 16)


def _unpack_bf16_pairs(w):
    lo = pltpu.bitcast(w << 16, F32).astype(BF16)
    hi = pltpu.bitcast(w & jnp.uint32(0xFFFF0000), F32).astype(BF16)
    return lo, hi


N_SLOTS = 3


def _dispatch_kernel(fill_ref, cdp_ref, cdc_ref, h_ref, lp_ref, xg_ref, sorted_s, sems, fill_sem):
    i = pl.program_id(0)
    n = pl.num_programs(0)
    slot = lax.rem(i, N_SLOTS)
    prev = lax.rem(i + N_SLOTS - 1, N_SLOTS)
    pprev = lax.rem(i + N_SLOTS - 2, N_SLOTS)
    spill0 = xg_ref.shape[0] - N_SLOTS * RL

    def start(cref, s, c, to_spill, sem):
        src = RUN * c if isinstance(c, int) else pl.multiple_of(RUN * c, RUN)
        d = cref[0, 0, c]
        d = pl.multiple_of(jnp.where(jnp.logical_or(d < 0, to_spill), spill0 + s * RL + src, d), RUN)
        pltpu.make_async_copy(sorted_s.at[s, pl.ds(src, RUN), :], xg_ref.at[pl.ds(d, RUN), :], sem).start()

    def drain(s, sem):
        pltpu.make_async_copy(sorted_s.at[s], xg_ref.at[pl.ds(0, RL), :], sem).wait()

    @pl.when(i == 0)
    def _():
        sorted_s[...] = jnp.zeros_like(sorted_s)
        for s in range(N_SLOTS):
            lax.fori_loop(0, RL // RUN, lambda c, carry, s=s: (start(cdp_ref, s, c, True, fill_sem), carry)[1], 0)
        for s in range(N_SLOTS):
            drain(s, fill_sem)

    for c in range(RL // RUN):
        start(cdp_ref, prev, c, i == 0, sems.at[prev])

    sorted_s[slot] = _pack_bf16_pairs(_dot_tn(_pair_matrix(lp_ref[...], None).astype(BF16), h_ref[...]))

    @pl.when(i == 0)
    def _():
        def fill(k):
            blk = pl.multiple_of(fill_ref[k] * MOE_BLOCK, MOE_BLOCK)
            return pltpu.make_async_copy(sorted_s.at[0, pl.ds(0, MOE_BLOCK), :],
                                         xg_ref.at[pl.ds(blk, MOE_BLOCK), :], fill_sem)

        for k in range(fill_ref.shape[0]):
            pl.when(fill_ref[k] >= 0)(lambda k=k: fill(k).start())
        for k in range(fill_ref.shape[0]):
            pl.when(fill_ref[k] >= 0)(lambda k=k: fill(k).wait())

    pl.when(i >= 1)(lambda: drain(pprev, sems.at[pprev]))

    @pl.when(i == n - 1)
    def _():
        lax.fori_loop(0, RL // RUN, lambda c, carry: (start(cdc_ref, slot, c, False, sems.at[slot]), carry)[1], 0,
                      unroll=4)
        drain(prev, sems.at[prev])
        drain(slot, sems.at[slot])


def _dispatch_call(fill_blocks, cd, h2, lp, P):
    T, d = h2.shape
    grid_spec = pltpu.PrefetchScalarGridSpec(
        num_scalar_prefetch=1,
        grid=(T // TB,),
        in_specs=[pl.BlockSpec((1, 1, CD_LANES), lambda i, fb: (jnp.maximum(i - 1, 0), 0, 0),
                               memory_space=pltpu.SMEM),
                  pl.BlockSpec((1, 1, CD_LANES), lambda i, fb: (i, 0, 0), memory_space=pltpu.SMEM),
                  pl.BlockSpec((TB, d), lambda i, fb: (i, 0)),
                  pl.BlockSpec((TB, LANES), lambda i, fb: (i, 0))],
        out_specs=pl.BlockSpec(memory_space=pl.ANY),
        scratch_shapes=[pltpu.VMEM((N_SLOTS, RL, D_HALF), jnp.uint32), pltpu.SemaphoreType.DMA((N_SLOTS,)),
                        pltpu.SemaphoreType.DMA(())],
    )
    return pl.pallas_call(
        _dispatch_kernel,
        grid_spec=grid_spec,
        out_shape=jax.ShapeDtypeStruct((P + N_SLOTS * RL, D_HALF), jnp.uint32),
        compiler_params=_params("arbitrary"),
    )(fill_blocks, cd, cd, h2, lp)


def _moe_kernel(be_ref, x_ref, wgu_ref, bgu_ref, wd_ref, bd_ref, y_ref, wgu_b, wd_b, act_s):
    i = pl.program_id(0)
    n_used = be_ref[pl.num_programs(0)]
    changed = jnp.logical_or(i == 0, be_ref[i] != be_ref[jnp.maximum(i - 1, 0)])

    @pl.when(changed)
    def _():
        wgu_b[...] = wgu_ref[0].astype(BF16)
        wd_b[...] = wd_ref[0].astype(BF16)

    @pl.when(i < n_used)
    def _():
        xb = jnp.concatenate(_unpack_bf16_pairs(x_ref[...]), axis=1)
        cw = 256
        for c in range(D_FF // cw):
            lo, hi = c * cw, (c + 1) * cw
            gate = _dot(xb, wgu_b[:, lo:hi]) + bgu_ref[0, :, lo:hi]
            up = _dot(xb, wgu_b[:, D_FF + lo:D_FF + hi]) + bgu_ref[0, :, D_FF + lo:D_FF + hi]
            gate = jnp.minimum(gate, SWIGLU_LIMIT)
            up = jnp.clip(up, -SWIGLU_LIMIT, SWIGLU_LIMIT)
            act_s[:, lo:hi] = ((up + 1.0) * (gate * _sigmoid(SWIGLU_ALPHA * gate))).astype(BF16)
        y_ref[...] = _pack_bf16_pairs(_dot(act_s[...], wd_b[...]) + bd_ref[0])

    @pl.when(i >= n_used)
    def _():
        y_ref[...] = jnp.zeros_like(y_ref)


def _moe_call(blk_e, xg, wgu, bgu, wd, bd, nb):
    P = nb * MOE_BLOCK
    ne, d, f2 = wgu.shape
    grid_spec = pltpu.PrefetchScalarGridSpec(
        num_scalar_prefetch=1,
        grid=(nb,),
        in_specs=[pl.BlockSpec((MOE_BLOCK, D_HALF), lambda i, be: (jnp.minimum(i, be[nb] - 1), 0)),
                  pl.BlockSpec((1, d, f2), lambda i, be: (be[i], 0, 0)),
                  pl.BlockSpec((1, 1, f2), lambda i, be: (be[i], 0, 0)),
                  pl.BlockSpec((1, f2 // 2, d), lambda i, be: (be[i], 0, 0)),
                  pl.BlockSpec((1, 1, d), lambda i, be: (be[i], 0, 0))],
        out_specs=pl.BlockSpec((MOE_BLOCK, D_HALF), lambda i, be: (i, 0)),
        scratch_shapes=[pltpu.VMEM((d, f2), BF16), pltpu.VMEM((f2 // 2, d), BF16),
                        pltpu.VMEM((MOE_BLOCK, f2 // 2), BF16)],
    )
    return pl.pallas_call(
        _moe_kernel,
        grid_spec=grid_spec,
        out_shape=jax.ShapeDtypeStruct((P, D_HALF), jnp.uint32),
        compiler_params=_params("arbitrary"),
    )(blk_e, xg, wgu, bgu, wd, bd)


def _combine_kernel(dcur_ref, dnext_ref, y_ref, lp_ref, x1_ref, gate_ref, g2_ref, fn_ref, o_ref, buf, sems):
    i = pl.program_id(0)
    n = pl.num_programs(0)
    slot = i & 1

    def start(dref, s, c):
        d = pl.multiple_of(jnp.maximum(dref[0, 0, c], 0), RUN)
        dst = pl.multiple_of(RUN * c, RUN)
        pltpu.make_async_copy(y_ref.at[pl.ds(d, RUN), :], buf.at[s, pl.ds(dst, RUN), :], sems.at[s]).start()

    def drain(s):
        pltpu.make_async_copy(y_ref.at[pl.ds(0, RL), :], buf.at[s], sems.at[s]).wait()

    @pl.when(i == 0)
    def _():
        lax.fori_loop(0, RL // RUN, lambda c, carry: (start(dcur_ref, 0, c), carry)[1], 0, unroll=4)

    drain(slot)

    for c in range(RL // RUN):
        start(dnext_ref, 1 - slot, c)

    g = _pair_matrix(lp_ref[...], gate_ref[...]).astype(BF16)
    halves = [_dot(g, yb) for yb in _unpack_bf16_pairs(buf[slot])]
    xo = x1_ref[...] + g2_ref[0] * jnp.concatenate(halves, axis=1)
    ms = jnp.mean(xo * xo, axis=-1, keepdims=True)
    o_ref[...] = xo * lax.rsqrt(ms + EPS) * fn_ref[...]

    pl.when(i == n - 1)(lambda: drain(1 - slot))


def _combine_call(cd, y, lp, x1, gates, g2, fn):
    T, d = x1.shape
    n = T // TB
    per_batch = n // g2.shape[0]
    return pl.pallas_call(
        _combine_kernel,
        grid=(n,),
        in_specs=[pl.BlockSpec((1, 1, CD_LANES), lambda i: (i, 0, 0), memory_space=pltpu.SMEM),
                  pl.BlockSpec((1, 1, CD_LANES), lambda i: (jnp.minimum(i + 1, n - 1), 0, 0),
                               memory_space=pltpu.SMEM),
                  pl.BlockSpec(memory_space=pl.ANY),
                  pl.BlockSpec((TB, LANES), lambda i: (i, 0)),
                  pl.BlockSpec((TB, d), lambda i: (i, 0)),
                  pl.BlockSpec((TB, LANES), lambda i: (i, 0)),
                  pl.BlockSpec((1, 1, d), lambda i: (i // per_batch, 0, 0)),
                  pl.BlockSpec((1, d), lambda i: (0, 0))],
        out_specs=pl.BlockSpec((TB, d), lambda i: (i, 0)),
        out_shape=jax.ShapeDtypeStruct((T, d), F32),
        scratch_shapes=[pltpu.VMEM((2, RL, D_HALF), jnp.uint32), pltpu.SemaphoreType.DMA((2,))],
        compiler_params=_params("arbitrary"),
    )(cd, cd, y, lp, x1, gates, g2, fn)


def _layer(x, c, ctx, c_ctx, w_mod, b_mod, norm1, w_in, gla_w_gk_up, gla_b_gk, gla_norm,
           ssd_conv_w, ssd_conv_b, ssd_dt_bias, ssd_A_log, ssd_D, ssd_norm, w_out,
           norm2, w_router, b_router, w_gate_up, b_gate_up, w_down, b_down, final_norm):
    bsz, L, d = x.shape
    lc = ctx.shape[1]
    assert lc == TB and L % TB == 0 and TB % GRID_W == 0

    cin = jnp.zeros((8, d), F32).at[:bsz].set(c).at[bsz].set(c_ctx)
    mod = _mod_call(cin, w_mod, b_mod.reshape(1, -1))[:bsz + 1]
    sh1, sc1, g1, sh2, sc2, g2 = [m.reshape(bsz + 1, 1, d) for m in jnp.split(mod, 6, axis=-1)]

    o = np.cumsum((0, GLA_QK, GLA_QK, GLA_V, GLA_V, GLA_RANK, SSD_INNER, SSD_CONV_DIM, SSD_HEADS))
    wq, wk, wv, wg, wlow, wz, wx, wdt = [w_in[:, int(a):int(b)] for a, b in zip(o[:-1], o[1:])]
    w_misc = jnp.concatenate([wlow, wdt, wdt, jnp.zeros((d, LANES - GLA_RANK - 2 * SSD_HEADS), F32)], axis=1)
    w_cat = jnp.concatenate([wq, wk, wv, wg, wz, wx, w_misc], axis=1).astype(BF16)
    wup = jnp.zeros((LANES, 2 * GLA_QK), F32).at[:GLA_RANK].set(
        jnp.concatenate([gla_w_gk_up[0], gla_w_gk_up[1]], axis=1)).astype(BF16)
    bup = jnp.concatenate([gla_b_gk[0], gla_b_gk[1]]).reshape(1, -1)
    dtb = jnp.zeros((1, LANES), F32).at[0, DT_F:DT_F + SSD_HEADS].set(ssd_dt_bias[0]) \
                                    .at[0, DT_B:DT_B + SSD_HEADS].set(ssd_dt_bias[1])
    q, k, v, g_all, z_all, xbc, ld, misc = _inproj_call(
        x, ctx, sh1, sc1, norm1.reshape(1, d), w_cat, wup, bup, dtb)

    xs, bc = _conv_call(xbc, ssd_conv_w.reshape(9, SSD_CONV_DIM), ssd_conv_b.reshape(1, -1), lc + L)

    o_f, o_b = _gla_call(q, k, v, ld, L)

    a_neg = -jnp.exp(ssd_A_log.astype(F32))
    a_f = jnp.zeros((1, LANES), F32).at[0, DT_F:DT_F + SSD_HEADS].set(a_neg[0])
    a_b = jnp.zeros((1, LANES), F32).at[0, DT_B:DT_B + SSD_HEADS].set(a_neg[1])
    dvec = jnp.repeat(ssd_D, SSD_HEADDIM).reshape(1, SSD_INNER)
    y_f, y_b = _ssd_call(xs, bc, misc, a_f, a_b, dvec, L)

    wr = jnp.zeros((d, LANES), F32).at[:, :N_EXPERTS].set(w_router)
    wr_hi = wr.astype(BF16)
    wr_lo = (wr - wr_hi.astype(F32)).astype(BF16)
    br = jnp.full((1, LANES), NEG_BIG, F32).at[0, :N_EXPERTS].set(b_router)
    x1, h2, eidx, gates, counts = _outproj_call(
        o_f, o_b, g_all, y_f, y_b, z_all, x, g1[:bsz], sh2[:bsz], sc2[:bsz],
        jnp.tile(gla_norm, GLA_HEADS).reshape(1, -1), ssd_norm.reshape(1, -1), w_out.astype(BF16),
        norm2.reshape(1, d), wr_hi, wr_lo, br)

    T = bsz * L
    cnt = counts[0, :N_EXPERTS].astype(jnp.int32)
    padded = ((cnt + MOE_BLOCK - 1) // MOE_BLOCK) * MOE_BLOCK
    pend = jnp.cumsum(padded)
    pstart = pend - padded
    max_rows = T * TOP_K + (T // TB) * N_EXPERTS * (RUN - 1)
    n_blocks = -(-max_rows // MOE_BLOCK) + N_EXPERTS
    blk_start = jnp.arange(n_blocks, dtype=jnp.int32) * MOE_BLOCK
    blk_e = jnp.minimum(jnp.sum(pend[None, :] <= blk_start[:, None], axis=1), N_EXPERTS - 1).astype(jnp.int32)
    n_used = (pend[-1] // MOE_BLOCK).astype(jnp.int32)
    blk_info = jnp.concatenate([blk_e, n_used[None]])
    n_tail = n_blocks - (T * TOP_K) // MOE_BLOCK
    tail = n_used + jnp.arange(n_tail, dtype=jnp.int32)
    fill_blocks = jnp.concatenate([
        jnp.where(padded > 0, pend // MOE_BLOCK - 1, -1),
        jnp.where(tail < n_blocks, tail, -1)]).astype(jnp.int32)
    pstart_row = jnp.zeros((1, LANES), F32).at[0, :N_EXPERTS].set(pstart.astype(F32))
    lp, cd = _pos_call(eidx.reshape(T, LANES), pstart_row)

    xg = _dispatch_call(fill_blocks, cd, h2.reshape(T, d), lp, n_blocks * MOE_BLOCK)
    y = _moe_call(blk_info, xg, w_gate_up, b_gate_up.reshape(N_EXPERTS, 1, -1), w_down,
                  b_down.reshape(N_EXPERTS, 1, -1), n_blocks)
    out = _combine_call(cd, y, lp, x1.reshape(T, d), gates.reshape(T, LANES), g2[:bsz], final_norm.reshape(1, d))
    return out.reshape(bsz, L, d)


def kernel(x, c, ctx, c_ctx, w_mod, b_mod, norm1, w_in, gla_w_gk_up, gla_b_gk, gla_norm, ssd_conv_w, ssd_conv_b, ssd_dt_bias, ssd_A_log, ssd_D, ssd_norm, w_out, norm2, w_router, b_router, w_gate_up, b_gate_up, w_down, b_down, final_norm):
    assert w_mod.shape[0] == 1, "single-layer kernel"
    return _layer(x, c, ctx, c_ctx, w_mod[0], b_mod[0], norm1[0], w_in[0], gla_w_gk_up[0], gla_b_gk[0],
                  gla_norm[0], ssd_conv_w[0], ssd_conv_b[0], ssd_dt_bias[0], ssd_A_log[0], ssd_D[0],
                  ssd_norm[0], w_out[0], norm2[0], w_router[0], b_router[0], w_gate_up[0], b_gate_up[0],
                  w_down[0], b_down[0], final_norm)
```

```python
import functools

import numpy as np
import jax
import jax.numpy as jnp
from jax import lax
from jax.experimental import pallas as pl
from jax.experimental.pallas import tpu as pltpu

F32 = jnp.float32
BF16 = jnp.bfloat16

EPS = 1e-6
GRID_W = 64
GLA_HEADS = 4
GLA_DK = 64
GLA_DV = 128
GLA_QK = GLA_HEADS * GLA_DK
GLA_V = GLA_HEADS * GLA_DV
GLA_RANK = 16
GLA_GATE_NORM = 16.0
SSD_HEADDIM = 64
SSD_INNER = 512
SSD_HEADS = 8
SSD_GROUPS = 2
SSD_HPG = 4
SSD_STATE = 128
SSD_CONV_DIM = 1024
N_EXPERTS = 32
TOP_K = 4
D_FF = 1024
SWIGLU_LIMIT = 7.0
SWIGLU_ALPHA = 1.702
MOE_BLOCK = 512

TB = 256
GLA_C = 64
GLA_STAGE_GROUP = 4
SSD_C = 128
SSD_STAGE_GROUP = 1
LANES = 128
EXP_CLAMP = 80.0
DT_F = 16
DT_B = 24
NEG_BIG = -1e30
RUN = 8
RL = 1280
CD_LANES = 256
SPILL_ROWS = -(-RL // MOE_BLOCK) * MOE_BLOCK
D_HALF = 512
VMEM_LIMIT = 56 * 1024 * 1024


def _dot(a, b):
    return jnp.dot(a, b, preferred_element_type=F32)


def _dot_nt(a, b):
    return lax.dot_general(a, b, (((1,), (1,)), ((), ())), preferred_element_type=F32)


def _dot_tn(a, b):
    return lax.dot_general(a, b, (((0,), (0,)), ((), ())), preferred_element_type=F32)


def _split3(a):
    hi = a.astype(BF16)
    r1 = a - hi.astype(F32)
    mid = r1.astype(BF16)
    lo = (r1 - mid.astype(F32)).astype(BF16)
    return hi, mid, lo


def _dot_exact_r(m, a):
    hi, mid, lo = _split3(a)
    return _dot(m, hi) + _dot(m, mid) + _dot(m, lo)


def _dot_hilo_r(m, a):
    hi = a.astype(BF16)
    lo = (a - hi.astype(F32)).astype(BF16)
    return _dot(m, hi) + _dot(m, lo)


def _dot_hilo_l(a, m2):
    hi = a.astype(BF16)
    lo = (a - hi.astype(F32)).astype(BF16)
    return _dot(jnp.concatenate([hi, lo], axis=1), m2)


def _sigmoid(x):
    return 1.0 / (1.0 + jnp.exp(-x))


def _softplus(x):
    return jnp.maximum(x, 0.0) + jnp.log1p(jnp.exp(-jnp.abs(x)))


def _params(*sem):
    return pltpu.CompilerParams(dimension_semantics=sem, vmem_limit_bytes=VMEM_LIMIT)


def _mod_kernel(c_ref, w_ref, b_ref, o_ref):
    c = c_ref[...]
    s = c * _sigmoid(c)
    s_hi = s.astype(BF16)
    s_lo = (s - s_hi.astype(F32)).astype(BF16)
    w = w_ref[...]
    w_hi = w.astype(BF16)
    w_lo = (w - w_hi.astype(F32)).astype(BF16)
    o_ref[...] = _dot(s_hi, w_hi) + _dot(s_lo, w_hi) + _dot(s_hi, w_lo) + b_ref[...]


def _mod_call(cin, w, b):
    rows, d = cin.shape
    n = w.shape[1]
    tn = 1536
    return pl.pallas_call(
        _mod_kernel,
        grid=(n // tn,),
        in_specs=[pl.BlockSpec((rows, d), lambda i: (0, 0)),
                  pl.BlockSpec((d, tn), lambda i: (0, i)),
                  pl.BlockSpec((1, tn), lambda i: (0, i))],
        out_specs=pl.BlockSpec((rows, tn), lambda i: (0, i)),
        out_shape=jax.ShapeDtypeStruct((rows, n), F32),
        compiler_params=_params("arbitrary"),
    )(cin, w, b)


_C_Q, _C_K, _C_V, _C_G, _C_Z, _C_X, _C_M, _C_END = 0, 256, 512, 1024, 1536, 2048, 3072, 3200


def _inproj_kernel(xa_ref, xb_ref, ctx_ref, shl_ref, scl_ref, shc_ref, scc_ref, n1_ref, w_ref, wup_ref, bup_ref,
                   dtb_ref, q_ref, k_ref, v_ref, g_ref, z_ref, xbc_ref, ld_ref, misc_ref, h_scr):
    j = pl.program_id(1)

    def normmod(xv, sh_ref, sc_ref):
        ms = jnp.mean(xv * xv, axis=-1, keepdims=True)
        y = xv * lax.rsqrt(ms + EPS) * n1_ref[...]
        return (y * (1.0 + sc_ref[0]) + sh_ref[0]).astype(BF16)

    @pl.when(j == 0)
    def _():
        h_scr[0:TB, :] = normmod(ctx_ref[0], shc_ref, scc_ref)

    @pl.when(j > 0)
    def _():
        h_scr[0:TB, :] = normmod(xa_ref[0], shl_ref, scl_ref)

    h_scr[TB:2 * TB, :] = normmod(xb_ref[0], shl_ref, scl_ref)
    h = h_scr[...]

    def mm(lo, hi):
        return _dot(h, w_ref[:, lo:hi])

    m = mm(_C_M, _C_END)
    zz = _dot(m.astype(BF16), wup_ref[...]) + bup_ref[...]
    ld_ref[0] = -_softplus(-zz) * (1.0 / GLA_GATE_NORM)
    misc_ref[0] = _softplus(m + dtb_ref[...])
    q_ref[0] = (mm(_C_Q, _C_K) * (GLA_DK ** -0.5)).astype(BF16)
    k_ref[0] = mm(_C_K, _C_V).astype(BF16)
    v_ref[0] = mm(_C_V, _C_G).astype(BF16)
    g_ref[0] = mm(_C_G, _C_Z)
    z_ref[0] = mm(_C_Z, _C_X)
    xbc_ref[0] = mm(_C_X, _C_M)


def _inproj_call(x, ctx, sh1, sc1, n1, w_cat, wup, bup, dtb):
    bsz, L, d = x.shape
    nx = L // TB
    nj = -(-(nx + 1) // 2)
    tok = lambda n: pl.BlockSpec((1, 2 * TB, n), lambda b, j: (b, j, 0))
    const = lambda a: pl.BlockSpec(a.shape, lambda b, j: (0,) * a.ndim)
    xblk = lambda off: pl.BlockSpec((1, TB, d), lambda b, j: (b, jnp.clip(2 * j + off, 0, nx - 1), 0))
    mod_lat = pl.BlockSpec((1, 1, d), lambda b, j: (b, 0, 0))
    mod_ctx = pl.BlockSpec((1, 1, d), lambda b, j: (bsz, 0, 0))
    outs = [(GLA_QK, BF16), (GLA_QK, BF16), (GLA_V, BF16), (GLA_V, F32), (SSD_INNER, F32),
            (SSD_CONV_DIM, F32), (2 * GLA_QK, F32), (LANES, F32)]
    return pl.pallas_call(
        _inproj_kernel,
        grid=(bsz, nj),
        in_specs=[xblk(-1), xblk(0), pl.BlockSpec((1, TB, d), lambda b, j: (b, 0, 0)),
                  mod_lat, mod_lat, mod_ctx, mod_ctx,
                  const(n1), const(w_cat), const(wup), const(bup), const(dtb)],
        out_specs=[tok(n) for n, _ in outs],
        out_shape=[jax.ShapeDtypeStruct((bsz, nj * 2 * TB, n), dt) for n, dt in outs],
        scratch_shapes=[pltpu.VMEM((2 * TB, d), BF16)],
        compiler_params=_params("arbitrary", "arbitrary"),
    )(x, x, ctx, sh1, sc1, sh1, sc1, n1, w_cat, wup, bup, dtb)


_EXT_PAD = 8
_EXT_BASE = _EXT_PAD + GRID_W
_EXT_ROWS = 2 * _EXT_PAD + 2 * GRID_W + TB


def _conv_kernel(prev_ref, cur_ref, next_ref, w_ref, b_ref, xs_ref, bc_ref, ext):
    j = pl.program_id(1)
    nj = pl.num_programs(1)
    is_ctx = j == 0
    zpad = jnp.zeros((_EXT_PAD, SSD_CONV_DIM), F32)
    ext[0:_EXT_PAD, :] = zpad
    ext[_EXT_ROWS - _EXT_PAD:_EXT_ROWS, :] = zpad
    ext[_EXT_PAD:_EXT_BASE, :] = jnp.where(j >= 2, prev_ref[0], 0.0)
    ext[_EXT_BASE:_EXT_BASE + TB, :] = cur_ref[0]
    ext[_EXT_BASE + TB:_EXT_BASE + TB + GRID_W, :] = jnp.where(
        jnp.logical_and(j >= 1, j <= nj - 2), next_ref[0], 0.0)

    t = lax.broadcasted_iota(jnp.int32, (TB, LANES), 0)
    col = t & (GRID_W - 1)
    ok_l = jnp.where(is_ctx, t, col) >= 1
    ok_r = jnp.where(is_ctx, t - (TB - GRID_W), col) <= GRID_W - 2
    lat = jnp.where(is_ctx, 0.0, 1.0)

    for c in range(SSD_CONV_DIM // LANES):
        lo, hi = c * LANES, (c + 1) * LANES
        acc = jnp.zeros((TB, LANES), F32)
        for dr in (-1, 0, 1):
            for dc in (-1, 0, 1):
                start = _EXT_BASE + GRID_W * dr + dc
                tap = ext[start:start + TB, lo:hi]
                wi = 3 * (dr + 1) + (dc + 1)
                wv = w_ref[wi:wi + 1, lo:hi]
                if dr != 0:
                    wv = wv * lat
                if dc == -1:
                    tap = jnp.where(ok_l, tap, 0.0)
                elif dc == 1:
                    tap = jnp.where(ok_r, tap, 0.0)
                acc = acc + tap * wv
        y = acc + b_ref[:, lo:hi]
        y = y * _sigmoid(y)
        if c < SSD_INNER // LANES:
            xs_ref[0, :, lo:hi] = y
        else:
            bc_ref[0, :, lo - SSD_INNER:hi - SSD_INNER] = y.astype(BF16)


def _conv_call(xbc, w9, bias, ls):
    bsz, _, ch = xbc.shape
    nj = ls // TB
    rpb = TB // GRID_W
    nrow = ls // GRID_W
    return pl.pallas_call(
        _conv_kernel,
        grid=(bsz, nj),
        in_specs=[pl.BlockSpec((1, GRID_W, ch), lambda b, j: (b, jnp.maximum(rpb * j - 1, 0), 0)),
                  pl.BlockSpec((1, TB, ch), lambda b, j: (b, j, 0)),
                  pl.BlockSpec((1, GRID_W, ch), lambda b, j: (b, jnp.minimum(rpb * j + rpb, nrow - 1), 0)),
                  pl.BlockSpec((9, ch), lambda b, j: (0, 0)),
                  pl.BlockSpec((1, ch), lambda b, j: (0, 0))],
        out_specs=[pl.BlockSpec((1, TB, SSD_INNER), lambda b, j: (b, j, 0)),
                   pl.BlockSpec((1, TB, ch - SSD_INNER), lambda b, j: (b, j, 0))],
        out_shape=[jax.ShapeDtypeStruct((bsz, ls, SSD_INNER), F32),
                   jax.ShapeDtypeStruct((bsz, ls, ch - SSD_INNER), BF16)],
        scratch_shapes=[pltpu.VMEM((_EXT_ROWS, ch), F32)],
        compiler_params=_params("arbitrary", "arbitrary"),
    )(xbc, xbc, xbc, w9, bias)


def _fwd_blk(s):
    return s


def _bwd_blk(s, ns):
    return jnp.where(s == 0, 0, ns - s)


def _scan_batch(bsz):
    return 2 if bsz % 2 == 0 else 1


class _GlaChunk:
    def __init__(self, q, k, v, la, st_ref, tri_m, fwd, store):
        self.q, self.k, self.v, self.la, self.st_ref, self.tri_m, self.fwd, self.store = (
            q, k, v, la, st_ref, tri_m, fwd, store)

    def stage_sums(self):
        self.b = _dot_hilo_r(self.tri_m, self.la)

    def stage_factors(self):
        C = GLA_C
        b = self.b
        self.bt = b[C - 1:C, :] if self.fwd else b[0:1, :]
        r = 0.5 * self.bt
        self.er = jnp.exp(r)
        self.qt = (self.q.astype(F32) * jnp.exp(jnp.minimum(b - r, EXP_CLAMP))).astype(BF16)
        kt = (self.k.astype(F32) * jnp.exp(jnp.minimum(r - b, EXP_CLAMP))).astype(BF16)
        head_k = lax.broadcasted_iota(jnp.int32, (C, GLA_QK), 1) >> 6
        zero = jnp.zeros_like(kt)
        self.kh = [jnp.where(head_k == h, kt, zero) for h in range(GLA_HEADS)]
        self.qh = [jnp.where(head_k == h, self.qt, zero) for h in range(GLA_HEADS)]

    def stage_products(self):
        v = self.v
        kcat = jnp.concatenate(self.kh, axis=0)
        self.sc = _dot_nt(self.qt, kcat)
        vcat = jnp.concatenate([v[:, h * GLA_DV:(h + 1) * GLA_DV] for h in range(GLA_HEADS)], axis=0)
        self.u = _dot_tn(vcat, kcat) * self.er

    def stage_mask(self):
        C = GLA_C
        ii = lax.broadcasted_iota(jnp.int32, (C, GLA_HEADS * C), 0)
        jj = lax.broadcasted_iota(jnp.int32, (C, GLA_HEADS * C), 1) & (C - 1)
        causal = (jj <= ii) if self.fwd else (jj >= ii)
        self.p = jnp.where(causal, self.sc, 0.0).astype(BF16)
        v = self.v
        head_v = lax.broadcasted_iota(jnp.int32, (C, GLA_V), 1) >> 7
        self.vst = jnp.concatenate([jnp.where(head_v == h, v, jnp.zeros_like(v)) for h in range(GLA_HEADS)], axis=0)

    def stage_intra(self):
        self.o = _dot(self.p, self.vst)

    def stage_inter(self):
        st = self.st_ref[...]
        ster = (st * self.er).astype(BF16)
        self.st_ref[...] = st * jnp.exp(self.bt) + self.u
        res = _dot_nt(jnp.concatenate(self.qh, axis=0), ster)
        self.inter = jnp.concatenate([res[h * GLA_C:(h + 1) * GLA_C, :] for h in range(GLA_HEADS)], axis=1)

    def stage_out(self):
        self.store(self.o + self.inter)


def _gla_kernel(qf_ref, kf_ref, vf_ref, lf_ref, qb_ref, kb_ref, vb_ref, lb_ref, trif_ref, trib_ref,
                of_ref, ob_ref, stf, stb):
    s = pl.program_id(1)

    @pl.when(s == 0)
    def _():
        stf[...] = jnp.zeros_like(stf)
        stb[...] = jnp.zeros_like(stb)

    nsub = TB // GLA_C

    def store_to(ref, bb, sl):
        def store(val):
            ref[bb, sl, :] = val
        return store

    group = GLA_STAGE_GROUP
    for g0 in range(0, nsub, group):
        steps = []
        for i in range(g0, g0 + group):
            sf = pl.ds(i * GLA_C, GLA_C)
            sb = pl.ds((nsub - 1 - i) * GLA_C, GLA_C)
            chunks = []
            for bb in range(qf_ref.shape[0]):
                chunks.append(_GlaChunk(qf_ref[bb, sf, :], kf_ref[bb, sf, :], vf_ref[bb, sf, :], lf_ref[bb, sf, :],
                                        stf.at[bb], trif_ref[...], True, store_to(of_ref, bb, sf)))
                chunks.append(_GlaChunk(qb_ref[bb, sb, :], kb_ref[bb, sb, :], vb_ref[bb, sb, :], lb_ref[bb, sb, :],
                                        stb.at[bb], trib_ref[...], False, store_to(ob_ref, bb, sb)))
            steps.append(chunks)
        for stage in ("stage_sums", "stage_factors", "stage_products", "stage_mask", "stage_intra"):
            for chunks in steps:
                for ch in chunks:
                    getattr(ch, stage)()
        for chunks in steps:
            for stage in ("stage_inter", "stage_out"):
                for ch in chunks:
                    getattr(ch, stage)()


def _gla_call(q, k, v, ld, L):
    bsz = q.shape[0]
    nx = L // TB
    ns = nx + 1
    trif = jnp.asarray(np.tril(np.ones((GLA_C, GLA_C), np.float32)), BF16)
    trib = jnp.asarray(np.triu(np.ones((GLA_C, GLA_C), np.float32)), BF16)
    nb = _scan_batch(bsz)
    f = lambda n, lane=0: pl.BlockSpec((nb, TB, n), lambda b, s: (b, _fwd_blk(s), lane))
    r = lambda n, lane=0: pl.BlockSpec((nb, TB, n), lambda b, s: (b, _bwd_blk(s, ns), lane))
    tri = pl.BlockSpec((GLA_C, GLA_C), lambda b, s: (0, 0))
    return pl.pallas_call(
        _gla_kernel,
        grid=(bsz // nb, ns),
        in_specs=[f(GLA_QK), f(GLA_QK), f(GLA_V), f(GLA_QK, 0),
                  r(GLA_QK), r(GLA_QK), r(GLA_V), r(GLA_QK, 1), tri, tri],
        out_specs=[pl.BlockSpec((nb, TB, GLA_V), lambda b, s: (b, jnp.maximum(s - 1, 0), 0)),
                   pl.BlockSpec((nb, TB, GLA_V), lambda b, s: (b, jnp.where(s == 0, nx - 1, nx - s), 0))],
        out_shape=[jax.ShapeDtypeStruct((bsz, L, GLA_V), F32)] * 2,
        scratch_shapes=[pltpu.VMEM((nb, GLA_DV, GLA_QK), F32)] * 2,
        compiler_params=_params("arbitrary", "arbitrary"),
    )(q, k, v, ld, q, k, v, ld, trif, trib)


class _SsdChunk:
    def __init__(self, xs, bc, dtm, avec, dvec, st_ref, tri_m, e_m, base, fwd, store):
        self.xs, self.bc, self.dtm, self.avec, self.dvec, self.st_ref = xs, bc, dtm, avec, dvec, st_ref
        self.tri_m, self.e_m, self.base, self.fwd, self.store = tri_m, e_m, base, fwd, store

    def stage_sums(self):
        self.dt_exp = _dot_hilo_l(self.dtm, self.e_m)
        self.acum = _dot_exact_r(self.tri_m, self.dtm * self.avec)

    def stage_expand(self):
        self.acum_exp = _dot_hilo_l(self.acum, self.e_m)
        self.acum_t = self.acum.T
        self.xdt = self.xs * self.dt_exp
        bc = self.bc
        self.bg = [bc[:, 128 * g:128 * (g + 1)] for g in range(SSD_GROUPS)]
        self.cg = [bc[:, 256 + 128 * g:256 + 128 * (g + 1)] for g in range(SSD_GROUPS)]
        self.cb = [_dot_nt(self.cg[g], self.bg[g]) for g in range(SSD_GROUPS)]

    def stage_decay(self):
        C = SSD_C
        ii = lax.broadcasted_iota(jnp.int32, (C, C), 0)
        jj = lax.broadcasted_iota(jnp.int32, (C, C), 1)
        tri = (jj <= ii) if self.fwd else (jj >= ii)
        self.ms = []
        for g in range(SSD_GROUPS):
            for rr in range(SSD_HPG):
                ln = self.base + SSD_HPG * g + rr
                diff = self.acum[:, ln:ln + 1] - self.acum_t[ln:ln + 1, :]
                seg = jnp.where(tri, jnp.exp(jnp.minimum(diff, 0.0)), 0.0)
                self.ms.append((self.cb[g] * seg).astype(BF16))
        ae = self.acum_exp
        self.al_exp = ae[C - 1:C, :] if self.fwd else ae[0:1, :]
        self.xw = (self.xdt * jnp.exp(self.al_exp - ae)).astype(BF16)
        xdt_b = self.xdt.astype(BF16)
        head = lax.broadcasted_iota(jnp.int32, (C, SSD_INNER), 1) >> 6
        zero = jnp.zeros_like(xdt_b)
        self.xh = [jnp.where(head == h, xdt_b, zero) for h in range(SSD_HEADS)]

    def stage_products(self):
        self.yg, self.ug = [], []
        for g in range(SSD_GROUPS):
            gl, gh = 256 * g, 256 * (g + 1)
            yg = _dot(self.ms[SSD_HPG * g], self.xh[SSD_HPG * g][:, gl:gh])
            for rr in range(1, SSD_HPG):
                yg = yg + _dot(self.ms[SSD_HPG * g + rr], self.xh[SSD_HPG * g + rr][:, gl:gh])
            self.yg.append(yg)
            self.ug.append(_dot_tn(self.bg[g], self.xw[:, gl:gh]))

    def stage_state(self):
        ys = []
        for g in range(SSD_GROUPS):
            gl, gh = 256 * g, 256 * (g + 1)
            sg = self.st_ref[g]
            yoff = _dot(self.cg[g], sg.astype(BF16)) * jnp.exp(self.acum_exp[:, gl:gh])
            self.st_ref[g] = sg * jnp.exp(self.al_exp[:, gl:gh]) + self.ug[g]
            ys.append(self.yg[g] + yoff)
        y = jnp.concatenate(ys, axis=1)
        if self.dvec is not None:
            y = y + self.dvec * self.xs
        self.store(y)


def _ssd_kernel(xf_ref, bcf_ref, mf_ref, xb_ref, bcb_ref, mb_ref, af_ref, ab_ref, d_ref,
                trif_ref, trib_ref, ef_ref, eb_ref, yf_ref, yb_ref, stf, stb):
    s = pl.program_id(1)

    @pl.when(s == 0)
    def _():
        stf[...] = jnp.zeros_like(stf)
        stb[...] = jnp.zeros_like(stb)

    nsub = TB // SSD_C

    def store_to(ref, bb, sl):
        def store(val):
            ref[bb, sl, :] = val
        return store

    steps = []
    for i in range(nsub):
        sf = pl.ds(i * SSD_C, SSD_C)
        sb = pl.ds((nsub - 1 - i) * SSD_C, SSD_C)
        chunks = []
        for bb in range(xf_ref.shape[0]):
            chunks.append(_SsdChunk(xf_ref[bb, sf, :], bcf_ref[bb, sf, :], mf_ref[bb, sf, :], af_ref[...],
                                    d_ref[...], stf.at[bb], trif_ref[...], ef_ref[...], DT_F, True,
                                    store_to(yf_ref, bb, sf)))
            chunks.append(_SsdChunk(xb_ref[bb, sb, :], bcb_ref[bb, sb, :], mb_ref[bb, sb, :], ab_ref[...],
                                    None, stb.at[bb], trib_ref[...], eb_ref[...], DT_B, False,
                                    store_to(yb_ref, bb, sb)))
        steps.append(chunks)
    for g0 in range(0, nsub, SSD_STAGE_GROUP):
        group = steps[g0:g0 + SSD_STAGE_GROUP]
        for stage in ("stage_sums", "stage_expand", "stage_decay", "stage_products"):
            for chunks in group:
                for ch in chunks:
                    getattr(ch, stage)()
        for chunks in group:
            for ch in chunks:
                ch.stage_state()


def _expand_matrix(base):
    e = np.zeros((LANES, SSD_INNER), np.float32)
    for h in range(SSD_HEADS):
        e[base + h, SSD_HEADDIM * h:SSD_HEADDIM * (h + 1)] = 1.0
    return jnp.asarray(np.concatenate([e, e], axis=0), BF16)


def _ssd_call(xs, bc, misc, a_f, a_b, dvec, L):
    bsz, ls, _ = xs.shape
    ns = ls // TB
    nx = L // TB
    trif = jnp.asarray(np.tril(np.ones((SSD_C, SSD_C), np.float32)), BF16)
    trib = jnp.asarray(np.triu(np.ones((SSD_C, SSD_C), np.float32)), BF16)
    ef, eb = _expand_matrix(DT_F), _expand_matrix(DT_B)
    nb = _scan_batch(bsz)
    f = lambda n: pl.BlockSpec((nb, TB, n), lambda b, s: (b, _fwd_blk(s), 0))
    r = lambda n: pl.BlockSpec((nb, TB, n), lambda b, s: (b, _bwd_blk(s, ns), 0))
    const = lambda a: pl.BlockSpec(a.shape, lambda b, s: (0,) * a.ndim)
    return pl.pallas_call(
        _ssd_kernel,
        grid=(bsz // nb, ns),
        in_specs=[f(SSD_INNER), f(512), f(LANES), r(SSD_INNER), r(512), r(LANES),
                  const(a_f), const(a_b), const(dvec), const(trif), const(trib), const(ef), const(eb)],
        out_specs=[pl.BlockSpec((nb, TB, SSD_INNER), lambda b, s: (b, jnp.maximum(s - 1, 0), 0)),
                   pl.BlockSpec((nb, TB, SSD_INNER), lambda b, s: (b, jnp.where(s == 0, nx - 1, nx - s), 0))],
        out_shape=[jax.ShapeDtypeStruct((bsz, L, SSD_INNER), F32)] * 2,
        scratch_shapes=[pltpu.VMEM((nb, SSD_GROUPS, SSD_STATE, SSD_HPG * SSD_HEADDIM), F32)] * 2,
        compiler_params=_params("arbitrary", "arbitrary"),
    )(xs, bc, misc, xs, bc, misc, a_f, a_b, dvec, trif, trib, ef, eb)


def _outproj_kernel(of_ref, ob_ref, ga_ref, gb_ref, yf_ref, yb_ref, za_ref, zb_ref, x_ref, g1_ref, sh2_ref, sc2_ref,
                    gn_ref, sn_ref, wo_ref, n2_ref, wrh_ref, wrl_ref, br_ref,
                    x1_ref, h2_ref, eidx_ref, gate_ref, cnt_ref, mix_s, hl_s):
    first = jnp.logical_and(pl.program_id(0) == 0, pl.program_id(1) == 0)

    @pl.when(first)
    def _():
        cnt_ref[...] = jnp.zeros_like(cnt_ref)

    halves = [(pl.ds(0, TB), ga_ref, za_ref), (pl.ds(TB, TB), gb_ref, zb_ref)]

    for rows, g_ref, z_ref in halves:
        o = of_ref[0, rows, :] + ob_ref[0, rows, :]
        gg = g_ref[0]
        for h in range(GLA_HEADS):
            lo, hi = GLA_DV * h, GLA_DV * (h + 1)
            oh = o[:, lo:hi]
            ms = jnp.mean(oh * oh, axis=-1, keepdims=True)
            gh = gg[:, lo:hi]
            mix_s[rows, lo:hi] = (oh * lax.rsqrt(ms + EPS) * gn_ref[:, lo:hi] * (gh * _sigmoid(gh))).astype(BF16)
        zz = z_ref[0]
        u = (yf_ref[0, rows, :] + yb_ref[0, rows, :]) * (zz * _sigmoid(zz))
        gw = SSD_INNER // SSD_GROUPS
        for g in range(SSD_GROUPS):
            lo, hi = gw * g, gw * (g + 1)
            ug = u[:, lo:hi]
            ms = jnp.mean(ug * ug, axis=-1, keepdims=True)
            mix_s[rows, GLA_V + lo:GLA_V + hi] = (ug * lax.rsqrt(ms + EPS) * sn_ref[:, lo:hi]).astype(BF16)

    x1_ref[0] = x_ref[0] + g1_ref[0] * _dot(mix_s[...], wo_ref[...])

    for rows, _, _ in halves:
        x1 = x1_ref[0, rows, :]
        ms = jnp.mean(x1 * x1, axis=-1, keepdims=True)
        h2 = (x1 * lax.rsqrt(ms + EPS) * n2_ref[...]) * (1.0 + sc2_ref[0]) + sh2_ref[0]
        h_hi = h2.astype(BF16)
        h2_ref[0, rows, :] = h_hi
        hl_s[rows, :] = (h2 - h_hi.astype(F32)).astype(BF16)

    h_hi = h2_ref[0]
    logits = (_dot(h_hi, wrh_ref[...]) + _dot(hl_s[...], wrh_ref[...]) + _dot(h_hi, wrl_ref[...])) + br_ref[...]

    lane = lax.broadcasted_iota(jnp.int32, (TB, LANES), 1).astype(F32)
    st = [dict(work=logits[i * TB:(i + 1) * TB, :], eidx=jnp.full((TB, LANES), -1.0, F32),
               gates=jnp.zeros((TB, LANES), F32), sel=jnp.zeros((TB, LANES), F32), m0=None,
               den=jnp.zeros((TB, 1), F32)) for i in range(2)]
    for kk in range(TOP_K):
        for t in st:
            t["m"] = jnp.max(t["work"], axis=-1, keepdims=True)
        for t in st:
            t["idx"] = jnp.min(jnp.where(t["work"] == t["m"], lane, float(LANES)), axis=-1, keepdims=True)
        for t in st:
            hit = lane == t["idx"]
            if t["m0"] is None:
                t["m0"] = t["m"]
            e = jnp.exp(t["m"] - t["m0"])
            t["den"] = t["den"] + e
            t["eidx"] = jnp.where(lane == float(kk), t["idx"], t["eidx"])
            t["gates"] = jnp.where(lane == float(kk), e, t["gates"])
            t["sel"] = jnp.where(hit, 1.0, t["sel"])
            t["work"] = jnp.where(hit, NEG_BIG, t["work"])
    for (rows, _, _), t in zip(halves, st):
        eidx_ref[0, rows, :] = t["eidx"].astype(jnp.int32)
        gate_ref[0, rows, :] = t["gates"] / t["den"]
        cnt = jnp.sum(t["sel"], axis=0, keepdims=True)
        cnt_ref[...] += jnp.floor((cnt + (RUN - 1.0)) * (1.0 / RUN)) * RUN


def _outproj_call(o_f, o_b, g_all, y_f, y_b, z_all, x, g1, sh2, sc2, gn, sn, wo, n2, wr_hi, wr_lo, br):
    bsz, L, d = x.shape
    nj = L // (2 * TB)
    tok = lambda n: pl.BlockSpec((1, 2 * TB, n), lambda b, j: (b, j, 0))
    tok_off = lambda n, half: pl.BlockSpec((1, TB, n), lambda b, j: (b, 2 * j + 1 + half, 0))
    const = lambda a: pl.BlockSpec(a.shape, lambda b, j: (0,) * a.ndim)
    mod = pl.BlockSpec((1, 1, d), lambda b, j: (b, 0, 0))
    return pl.pallas_call(
        _outproj_kernel,
        grid=(bsz, nj),
        in_specs=[tok(GLA_V), tok(GLA_V), tok_off(GLA_V, 0), tok_off(GLA_V, 1),
                  tok(SSD_INNER), tok(SSD_INNER), tok_off(SSD_INNER, 0), tok_off(SSD_INNER, 1),
                  tok(d), mod, mod, mod, const(gn), const(sn), const(wo), const(n2),
                  const(wr_hi), const(wr_lo), const(br)],
        out_specs=[tok(d), tok(d), tok(LANES), tok(LANES), pl.BlockSpec((1, LANES), lambda b, j: (0, 0))],
        out_shape=[jax.ShapeDtypeStruct((bsz, L, d), F32), jax.ShapeDtypeStruct((bsz, L, d), BF16),
                   jax.ShapeDtypeStruct((bsz, L, LANES), jnp.int32), jax.ShapeDtypeStruct((bsz, L, LANES), F32),
                   jax.ShapeDtypeStruct((1, LANES), F32)],
        scratch_shapes=[pltpu.VMEM((2 * TB, d), BF16), pltpu.VMEM((2 * TB, d), BF16)],
        compiler_params=_params("arbitrary", "arbitrary"),
    )(o_f, o_b, g_all, g_all, y_f, y_b, z_all, z_all, x, g1, sh2, sc2, gn, sn, wo, n2, wr_hi, wr_lo, br)


def _pos_kernel(eidx_ref, pstart_ref, lst_ref, ust_ref, lp_ref, cd_ref, carry):
    @pl.when(pl.program_id(0) == 0)
    def _():
        carry[...] = pstart_ref[...]

    eidx = eidx_ref[...]
    lane = lax.broadcasted_iota(jnp.int32, (TB, LANES), 1)
    hits = [lane == eidx[:, kk:kk + 1] for kk in range(TOP_K)]
    sel = jnp.zeros((TB, LANES), F32)
    for hmask in hits:
        sel = jnp.where(hmask, 1.0, sel)
    cnt = jnp.sum(sel, axis=0, keepdims=True)
    run = jnp.floor((cnt + (RUN - 1.0)) * (1.0 / RUN)) * RUN
    rank = _dot(lst_ref[...], sel.astype(BF16))
    loff = _dot(jnp.broadcast_to(run, (8, LANES)).astype(BF16), ust_ref[...])[0:1]
    pos = loff + rank
    lp = jnp.zeros((TB, LANES), jnp.int32)
    for kk, hmask in enumerate(hits):
        lk = jnp.sum(jnp.where(hmask, pos, 0.0), axis=-1, keepdims=True)
        lp = jnp.where(lane == kk, lk.astype(jnp.int32), lp)
    lp_ref[...] = lp

    base = carry[...]
    eye = lax.broadcasted_iota(jnp.int32, (LANES, LANES), 0) == lax.broadcasted_iota(jnp.int32, (LANES, LANES), 1)
    col = lambda v: jnp.sum(jnp.where(eye, v, 0.0), axis=1, keepdims=True)
    loff_c, run_c, shift_c = col(loff), col(run), col(base - loff)
    row0 = (lax.broadcasted_iota(jnp.int32, (LANES, CD_LANES), 1) * RUN).astype(F32)
    inside = jnp.where(row0 >= loff_c, 1.0, 0.0) * jnp.where(row0 < loff_c + run_c, 1.0, 0.0)
    valid = jnp.sum(inside, axis=0, keepdims=True)
    dest = jnp.sum(inside * shift_c, axis=0, keepdims=True) + row0[0:1]
    cd_ref[0] = jnp.where(valid > 0.0, dest, -1.0).astype(jnp.int32)
    carry[...] = base + run


def _pos_call(eidx, pstart):
    T = eidx.shape[0]
    lst = jnp.asarray(np.tril(np.ones((TB, TB), np.float32), -1), BF16)
    ust = jnp.asarray(np.triu(np.ones((LANES, LANES), np.float32), 1), BF16)
    return pl.pallas_call(
        _pos_kernel,
        grid=(T // TB,),
        in_specs=[pl.BlockSpec((TB, LANES), lambda i: (i, 0)),
                  pl.BlockSpec((1, LANES), lambda i: (0, 0)),
                  pl.BlockSpec((TB, TB), lambda i: (0, 0)),
                  pl.BlockSpec((LANES, LANES), lambda i: (0, 0))],
        out_specs=[pl.BlockSpec((TB, LANES), lambda i: (i, 0)),
                   pl.BlockSpec((1, 1, CD_LANES), lambda i: (i, 0, 0))],
        out_shape=[jax.ShapeDtypeStruct((T, LANES), jnp.int32),
                   jax.ShapeDtypeStruct((T // TB, 1, CD_LANES), jnp.int32)],
        scratch_shapes=[pltpu.VMEM((1, LANES), F32)],
        compiler_params=_params("arbitrary"),
    )(eidx, pstart, lst, ust)


def _pair_matrix(lp, weights):
    j = lax.broadcasted_iota(jnp.int32, (TB, RL), 1)
    m = jnp.zeros((TB, RL), F32)
    for kk in range(TOP_K):
        w = 1.0 if weights is None else weights[:, kk:kk + 1]
        m = m + jnp.where(j == lp[:, kk:kk + 1], w, 0.0)
    return m


def _pack_bf16_pairs(v, is_bf16_valued=False):
    if not is_bf16_valued:
        v = v.astype(BF16).astype(F32)
    bits = pltpu.bitcast(v, jnp.uint32)
    return bits[:, D_HALF:] | (bits[:, :D_HALF] >> 16)


def _unpack_bf16_pairs(w):
    lo = pltpu.bitcast(w << 16, F32).astype(BF16)
    hi = pltpu.bitcast(w & jnp.uint32(0xFFFF0000), F32).astype(BF16)
    return lo, hi


N_SLOTS = 3


def _dispatch_kernel(fill_ref, cdp_ref, cdc_ref, h_ref, lp_ref, xg_ref, sorted_s, sems, fill_sem):
    i = pl.program_id(0)
    n = pl.num_programs(0)
    slot = lax.rem(i, N_SLOTS)
    prev = lax.rem(i + N_SLOTS - 1, N_SLOTS)
    pprev = lax.rem(i + N_SLOTS - 2, N_SLOTS)
    spill0 = xg_ref.shape[0] - N_SLOTS * RL

    def start(cref, s, c, to_spill, sem):
        src = RUN * c if isinstance(c, int) else pl.multiple_of(RUN * c, RUN)
        d = cref[0, 0, c]
        d = pl.multiple_of(jnp.where(jnp.logical_or(d < 0, to_spill), spill0 + s * RL + src, d), RUN)
        pltpu.make_async_copy(sorted_s.at[s, pl.ds(src, RUN), :], xg_ref.at[pl.ds(d, RUN), :], sem).start()

    def drain(s, sem):
        pltpu.make_async_copy(sorted_s.at[s], xg_ref.at[pl.ds(0, RL), :], sem).wait()

    @pl.when(i == 0)
    def _():
        sorted_s[...] = jnp.zeros_like(sorted_s)
        for s in range(N_SLOTS):
            lax.fori_loop(0, RL // RUN, lambda c, carry, s=s: (start(cdp_ref, s, c, True, fill_sem), carry)[1], 0)
        for s in range(N_SLOTS):
            drain(s, fill_sem)

    for c in range(RL // RUN):
        start(cdp_ref, prev, c, i == 0, sems.at[prev])

    sorted_s[slot] = _pack_bf16_pairs(_dot_tn(_pair_matrix(lp_ref[...], None).astype(BF16), h_ref[...]),
                                      is_bf16_valued=True)

    @pl.when(i == 0)
    def _():
        def fill(k):
            blk = pl.multiple_of(fill_ref[k] * MOE_BLOCK, MOE_BLOCK)
            return pltpu.make_async_copy(sorted_s.at[0, pl.ds(0, MOE_BLOCK), :],
                                         xg_ref.at[pl.ds(blk, MOE_BLOCK), :], fill_sem)

        for k in range(fill_ref.shape[0]):
            pl.when(fill_ref[k] >= 0)(lambda k=k: fill(k).start())
        for k in range(fill_ref.shape[0]):
            pl.when(fill_ref[k] >= 0)(lambda k=k: fill(k).wait())

    pl.when(i >= 1)(lambda: drain(pprev, sems.at[pprev]))

    @pl.when(i == n - 1)
    def _():
        lax.fori_loop(0, RL // RUN, lambda c, carry: (start(cdc_ref, slot, c, False, sems.at[slot]), carry)[1], 0,
                      unroll=4)
        drain(prev, sems.at[prev])
        drain(slot, sems.at[slot])


def _dispatch_call(fill_blocks, cd, h2, lp, P):
    T, d = h2.shape
    grid_spec = pltpu.PrefetchScalarGridSpec(
        num_scalar_prefetch=1,
        grid=(T // TB,),
        in_specs=[pl.BlockSpec((1, 1, CD_LANES), lambda i, fb: (jnp.maximum(i - 1, 0), 0, 0),
                               memory_space=pltpu.SMEM),
                  pl.BlockSpec((1, 1, CD_LANES), lambda i, fb: (i, 0, 0), memory_space=pltpu.SMEM),
                  pl.BlockSpec((TB, d), lambda i, fb: (i, 0)),
                  pl.BlockSpec((TB, LANES), lambda i, fb: (i, 0))],
        out_specs=pl.BlockSpec(memory_space=pl.ANY),
        scratch_shapes=[pltpu.VMEM((N_SLOTS, RL, D_HALF), jnp.uint32), pltpu.SemaphoreType.DMA((N_SLOTS,)),
                        pltpu.SemaphoreType.DMA(())],
    )
    return pl.pallas_call(
        _dispatch_kernel,
        grid_spec=grid_spec,
        out_shape=jax.ShapeDtypeStruct((P + N_SLOTS * RL, D_HALF), jnp.uint32),
        compiler_params=_params("arbitrary"),
    )(fill_blocks, cd, cd, h2, lp)


def _moe_kernel(be_ref, x_ref, wgu_ref, bgu_ref, wd_ref, bd_ref, y_ref, wgu_b, wd_b, act_s):
    i = pl.program_id(0)
    n_used = be_ref[pl.num_programs(0)]
    changed = jnp.logical_or(i == 0, be_ref[i] != be_ref[jnp.maximum(i - 1, 0)])

    @pl.when(changed)
    def _():
        wgu_b[...] = wgu_ref[0].astype(BF16)
        wd_b[...] = wd_ref[0].astype(BF16)

    @pl.when(i < n_used)
    def _():
        xb = jnp.concatenate(_unpack_bf16_pairs(x_ref[...]), axis=1)
        cw = 256
        for c in range(D_FF // cw):
            lo, hi = c * cw, (c + 1) * cw
            gate = _dot(xb, wgu_b[:, lo:hi]) + bgu_ref[0, :, lo:hi]
            up = _dot(xb, wgu_b[:, D_FF + lo:D_FF + hi]) + bgu_ref[0, :, D_FF + lo:D_FF + hi]
            gate = jnp.minimum(gate, SWIGLU_LIMIT)
            up = jnp.clip(up, -SWIGLU_LIMIT, SWIGLU_LIMIT)
            act_s[:, lo:hi] = ((up + 1.0) * (gate * _sigmoid(SWIGLU_ALPHA * gate))).astype(BF16)
        y_ref[...] = _pack_bf16_pairs(_dot(act_s[...], wd_b[...]) + bd_ref[0])

    @pl.when(i >= n_used)
    def _():
        y_ref[...] = jnp.zeros_like(y_ref)


def _moe_call(blk_e, xg, wgu, bgu, wd, bd, nb):
    P = nb * MOE_BLOCK
    ne, d, f2 = wgu.shape
    grid_spec = pltpu.PrefetchScalarGridSpec(
        num_scalar_prefetch=1,
        grid=(nb,),
        in_specs=[pl.BlockSpec((MOE_BLOCK, D_HALF), lambda i, be: (jnp.minimum(i, be[nb] - 1), 0)),
                  pl.BlockSpec((1, d, f2), lambda i, be: (be[i], 0, 0)),
                  pl.BlockSpec((1, 1, f2), lambda i, be: (be[i], 0, 0)),
                  pl.BlockSpec((1, f2 // 2, d), lambda i, be: (be[i], 0, 0)),
                  pl.BlockSpec((1, 1, d), lambda i, be: (be[i], 0, 0))],
        out_specs=pl.BlockSpec((MOE_BLOCK, D_HALF), lambda i, be: (i, 0)),
        scratch_shapes=[pltpu.VMEM((d, f2), BF16), pltpu.VMEM((f2 // 2, d), BF16),
                        pltpu.VMEM((MOE_BLOCK, f2 // 2), BF16)],
    )
    return pl.pallas_call(
        _moe_kernel,
        grid_spec=grid_spec,
        out_shape=jax.ShapeDtypeStruct((P, D_HALF), jnp.uint32),
        compiler_params=_params("arbitrary"),
    )(blk_e, xg, wgu, bgu, wd, bd)


def _combine_kernel(d0_ref, d1_ref, d2_ref, y_ref, lp_ref, x1_ref, gate_ref, g2_ref, fn_ref, o_ref, buf, sems):
    i = pl.program_id(0)
    n = pl.num_programs(0)
    slot = lax.rem(i, N_SLOTS)
    ahead = lax.rem(i + 2, N_SLOTS)

    def start(dref, s, c):
        d = pl.multiple_of(jnp.maximum(dref[0, 0, c], 0), RUN)
        dst = pl.multiple_of(RUN * c, RUN)
        pltpu.make_async_copy(y_ref.at[pl.ds(d, RUN), :], buf.at[s, pl.ds(dst, RUN), :], sems.at[s]).start()

    def drain(s):
        pltpu.make_async_copy(y_ref.at[pl.ds(0, RL), :], buf.at[s], sems.at[s]).wait()

    @pl.when(i == 0)
    def _():
        lax.fori_loop(0, RL // RUN, lambda c, carry: (start(d0_ref, 0, c), carry)[1], 0, unroll=4)
        lax.fori_loop(0, RL // RUN, lambda c, carry: (start(d1_ref, 1, c), carry)[1], 0, unroll=4)

    drain(slot)

    for c in range(RL // RUN):
        start(d2_ref, ahead, c)

    g = _pair_matrix(lp_ref[...], gate_ref[...]).astype(BF16)
    halves = [_dot(g, yb) for yb in _unpack_bf16_pairs(buf[slot])]
    xo = x1_ref[...] + g2_ref[0] * jnp.concatenate(halves, axis=1)
    ms = jnp.mean(xo * xo, axis=-1, keepdims=True)
    o_ref[...] = xo * lax.rsqrt(ms + EPS) * fn_ref[...]

    @pl.when(i == n - 1)
    def _():
        drain(lax.rem(i + 1, N_SLOTS))
        drain(ahead)


def _combine_call(cd, y, lp, x1, gates, g2, fn):
    T, d = x1.shape
    n = T // TB
    per_batch = n // g2.shape[0]
    return pl.pallas_call(
        _combine_kernel,
        grid=(n,),
        in_specs=[pl.BlockSpec((1, 1, CD_LANES), lambda i: (i, 0, 0), memory_space=pltpu.SMEM),
                  pl.BlockSpec((1, 1, CD_LANES), lambda i: (jnp.minimum(i + 1, n - 1), 0, 0),
                               memory_space=pltpu.SMEM),
                  pl.BlockSpec((1, 1, CD_LANES), lambda i: (jnp.minimum(i + 2, n - 1), 0, 0),
                               memory_space=pltpu.SMEM),
                  pl.BlockSpec(memory_space=pl.ANY),
                  pl.BlockSpec((TB, LANES), lambda i: (i, 0)),
                  pl.BlockSpec((TB, d), lambda i: (i, 0)),
                  pl.BlockSpec((TB, LANES), lambda i: (i, 0)),
                  pl.BlockSpec((1, 1, d), lambda i: (i // per_batch, 0, 0)),
                  pl.BlockSpec((1, d), lambda i: (0, 0))],
        out_specs=pl.BlockSpec((TB, d), lambda i: (i, 0)),
        out_shape=jax.ShapeDtypeStruct((T, d), F32),
        scratch_shapes=[pltpu.VMEM((N_SLOTS, RL, D_HALF), jnp.uint32), pltpu.SemaphoreType.DMA((N_SLOTS,))],
        compiler_params=_params("arbitrary"),
    )(cd, cd, cd, y, lp, x1, gates, g2, fn)


def _layer(x, c, ctx, c_ctx, w_mod, b_mod, norm1, w_in, gla_w_gk_up, gla_b_gk, gla_norm,
           ssd_conv_w, ssd_conv_b, ssd_dt_bias, ssd_A_log, ssd_D, ssd_norm, w_out,
           norm2, w_router, b_router, w_gate_up, b_gate_up, w_down, b_down, final_norm):
    bsz, L, d = x.shape
    lc = ctx.shape[1]
    assert lc == TB and L % TB == 0 and TB % GRID_W == 0

    cin = jnp.zeros((8, d), F32).at[:bsz].set(c).at[bsz].set(c_ctx)
    mod = _mod_call(cin, w_mod, b_mod.reshape(1, -1))[:bsz + 1]
    sh1, sc1, g1, sh2, sc2, g2 = [m.reshape(bsz + 1, 1, d) for m in jnp.split(mod, 6, axis=-1)]

    o = np.cumsum((0, GLA_QK, GLA_QK, GLA_V, GLA_V, GLA_RANK, SSD_INNER, SSD_CONV_DIM, SSD_HEADS))
    wq, wk, wv, wg, wlow, wz, wx, wdt = [w_in[:, int(a):int(b)] for a, b in zip(o[:-1], o[1:])]
    w_misc = jnp.concatenate([wlow, wdt, wdt, jnp.zeros((d, LANES - GLA_RANK - 2 * SSD_HEADS), F32)], axis=1)
    w_cat = jnp.concatenate([wq, wk, wv, wg, wz, wx, w_misc], axis=1).astype(BF16)
    wup = jnp.zeros((LANES, 2 * GLA_QK), F32).at[:GLA_RANK].set(
        jnp.concatenate([gla_w_gk_up[0], gla_w_gk_up[1]], axis=1)).astype(BF16)
    bup = jnp.concatenate([gla_b_gk[0], gla_b_gk[1]]).reshape(1, -1)
    dtb = jnp.zeros((1, LANES), F32).at[0, DT_F:DT_F + SSD_HEADS].set(ssd_dt_bias[0]) \
                                    .at[0, DT_B:DT_B + SSD_HEADS].set(ssd_dt_bias[1])
    q, k, v, g_all, z_all, xbc, ld, misc = _inproj_call(
        x, ctx, sh1, sc1, norm1.reshape(1, d), w_cat, wup, bup, dtb)

    xs, bc = _conv_call(xbc, ssd_conv_w.reshape(9, SSD_CONV_DIM), ssd_conv_b.reshape(1, -1), lc + L)

    o_f, o_b = _gla_call(q, k, v, ld, L)

    a_neg = -jnp.exp(ssd_A_log.astype(F32))
    a_f = jnp.zeros((1, LANES), F32).at[0, DT_F:DT_F + SSD_HEADS].set(a_neg[0])
    a_b = jnp.zeros((1, LANES), F32).at[0, DT_B:DT_B + SSD_HEADS].set(a_neg[1])
    dvec = jnp.repeat(ssd_D, SSD_HEADDIM).reshape(1, SSD_INNER)
    y_f, y_b = _ssd_call(xs, bc, misc, a_f, a_b, dvec, L)

    wr = jnp.zeros((d, LANES), F32).at[:, :N_EXPERTS].set(w_router)
    wr_hi = wr.astype(BF16)
    wr_lo = (wr - wr_hi.astype(F32)).astype(BF16)
    br = jnp.full((1, LANES), NEG_BIG, F32).at[0, :N_EXPERTS].set(b_router)
    x1, h2, eidx, gates, counts = _outproj_call(
        o_f, o_b, g_all, y_f, y_b, z_all, x, g1[:bsz], sh2[:bsz], sc2[:bsz],
        jnp.tile(gla_norm, GLA_HEADS).reshape(1, -1), ssd_norm.reshape(1, -1), w_out.astype(BF16),
        norm2.reshape(1, d), wr_hi, wr_lo, br)

    T = bsz * L
    cnt = counts[0, :N_EXPERTS].astype(jnp.int32)
    padded = ((cnt + MOE_BLOCK - 1) // MOE_BLOCK) * MOE_BLOCK
    pend = jnp.cumsum(padded)
    pstart = pend - padded
    max_rows = T * TOP_K + (T // TB) * N_EXPERTS * (RUN - 1)
    n_blocks = -(-max_rows // MOE_BLOCK) + N_EXPERTS
    blk_start = jnp.arange(n_blocks, dtype=jnp.int32) * MOE_BLOCK
    blk_e = jnp.minimum(jnp.sum(pend[None, :] <= blk_start[:, None], axis=1), N_EXPERTS - 1).astype(jnp.int32)
    n_used = (pend[-1] // MOE_BLOCK).astype(jnp.int32)
    blk_info = jnp.concatenate([blk_e, n_used[None]])
    n_tail = n_blocks - (T * TOP_K) // MOE_BLOCK
    tail = n_used + jnp.arange(n_tail, dtype=jnp.int32)
    fill_blocks = jnp.concatenate([
        jnp.where(padded > 0, pend // MOE_BLOCK - 1, -1),
        jnp.where(tail < n_blocks, tail, -1)]).astype(jnp.int32)
    pstart_row = jnp.zeros((1, LANES), F32).at[0, :N_EXPERTS].set(pstart.astype(F32))
    lp, cd = _pos_call(eidx.reshape(T, LANES), pstart_row)

    xg = _dispatch_call(fill_blocks, cd, h2.reshape(T, d), lp, n_blocks * MOE_BLOCK)
    y = _moe_call(blk_info, xg, w_gate_up, b_gate_up.reshape(N_EXPERTS, 1, -1), w_down,
                  b_down.reshape(N_EXPERTS, 1, -1), n_blocks)
    out = _combine_call(cd, y, lp, x1.reshape(T, d), gates.reshape(T, LANES), g2[:bsz], final_norm.reshape(1, d))
    return out.reshape(bsz, L, d)


def kernel(x, c, ctx, c_ctx, w_mod, b_mod, norm1, w_in, gla_w_gk_up, gla_b_gk, gla_norm, ssd_conv_w, ssd_conv_b, ssd_dt_bias, ssd_A_log, ssd_D, ssd_norm, w_out, norm2, w_router, b_router, w_gate_up, b_gate_up, w_down, b_down, final_norm):
    assert w_mod.shape[0] == 1, "single-layer kernel"
    return _layer(x, c, ctx, c_ctx, w_mod[0], b_mod[0], norm1[0], w_in[0], gla_w_gk_up[0], gla_b_gk[0],
                  gla_norm[0], ssd_conv_w[0], ssd_conv_b[0], ssd_dt_bias[0], ssd_A_log[0], ssd_D[0],
                  ssd_norm[0], w_out[0], norm2[0], w_router[0], b_router[0], w_gate_up[0], b_gate_up[0],
                  w_down[0], b_down[0], final_norm)
```

```python
import functools

import numpy as np
import jax
import jax.numpy as jnp
from jax import lax
from jax.experimental import pallas as pl
from jax.experimental.pallas import tpu as pltpu

F32 = jnp.float32
BF16 = jnp.bfloat16

EPS = 1e-6
GRID_W = 64
GLA_HEADS = 4
GLA_DK = 64
GLA_DV = 128
GLA_QK = GLA_HEADS * GLA_DK
GLA_V = GLA_HEADS * GLA_DV
GLA_RANK = 16
GLA_GATE_NORM = 16.0
SSD_HEADDIM = 64
SSD_INNER = 512
SSD_HEADS = 8
SSD_GROUPS = 2
SSD_HPG = 4
SSD_STATE = 128
SSD_CONV_DIM = 1024
N_EXPERTS = 32
TOP_K = 4
D_FF = 1024
SWIGLU_LIMIT = 7.0
SWIGLU_ALPHA = 1.702
MOE_BLOCK = 512

TB = 256
GLA_C = 64
GLA_STAGE_GROUP = 4
SSD_C = 128
SSD_STAGE_GROUP = 1
LANES = 128
EXP_CLAMP = 80.0
DT_F = 16
DT_B = 24
NEG_BIG = -1e30
RUN = 8
RL = 1280
CD_LANES = 256
SPILL_ROWS = -(-RL // MOE_BLOCK) * MOE_BLOCK
D_HALF = 512
VMEM_LIMIT = 56 * 1024 * 1024


def _dot(a, b):
    return jnp.dot(a, b, preferred_element_type=F32)


def _dot_nt(a, b):
    return lax.dot_general(a, b, (((1,), (1,)), ((), ())), preferred_element_type=F32)


def _dot_tn(a, b):
    return lax.dot_general(a, b, (((0,), (0,)), ((), ())), preferred_element_type=F32)


def _split3(a):
    hi = a.astype(BF16)
    r1 = a - hi.astype(F32)
    mid = r1.astype(BF16)
    lo = (r1 - mid.astype(F32)).astype(BF16)
    return hi, mid, lo


def _dot_exact_r(m, a):
    hi, mid, lo = _split3(a)
    return _dot(m, hi) + _dot(m, mid) + _dot(m, lo)


def _dot_hilo_r(m, a):
    hi = a.astype(BF16)
    lo = (a - hi.astype(F32)).astype(BF16)
    return _dot(m, hi) + _dot(m, lo)


def _dot_hilo_l(a, m2):
    hi = a.astype(BF16)
    lo = (a - hi.astype(F32)).astype(BF16)
    return _dot(jnp.concatenate([hi, lo], axis=1), m2)


def _sigmoid(x):
    return 1.0 / (1.0 + jnp.exp(-x))


def _softplus(x):
    return jnp.maximum(x, 0.0) + jnp.log1p(jnp.exp(-jnp.abs(x)))


def _params(*sem):
    return pltpu.CompilerParams(dimension_semantics=sem, vmem_limit_bytes=VMEM_LIMIT)


def _mod_kernel(c_ref, w_ref, b_ref, o_ref):
    c = c_ref[...]
    s = c * _sigmoid(c)
    s_hi = s.astype(BF16)
    s_lo = (s - s_hi.astype(F32)).astype(BF16)
    w = w_ref[...]
    w_hi = w.astype(BF16)
    w_lo = (w - w_hi.astype(F32)).astype(BF16)
    o_ref[...] = _dot(s_hi, w_hi) + _dot(s_lo, w_hi) + _dot(s_hi, w_lo) + b_ref[...]


def _mod_call(cin, w, b):
    rows, d = cin.shape
    n = w.shape[1]
    tn = 1536
    return pl.pallas_call(
        _mod_kernel,
        grid=(n // tn,),
        in_specs=[pl.BlockSpec((rows, d), lambda i: (0, 0)),
                  pl.BlockSpec((d, tn), lambda i: (0, i)),
                  pl.BlockSpec((1, tn), lambda i: (0, i))],
        out_specs=pl.BlockSpec((rows, tn), lambda i: (0, i)),
        out_shape=jax.ShapeDtypeStruct((rows, n), F32),
        compiler_params=_params("arbitrary"),
    )(cin, w, b)


_C_Q, _C_K, _C_V, _C_G, _C_Z, _C_X, _C_M, _C_END = 0, 256, 512, 1024, 1536, 2048, 3072, 3200


def _inproj_kernel(xa_ref, xb_ref, ctx_ref, shl_ref, scl_ref, shc_ref, scc_ref, n1_ref, w_ref, wup_ref, bup_ref,
                   dtb_ref, q_ref, k_ref, v_ref, g_ref, z_ref, xbc_ref, ld_ref, misc_ref, h_scr):
    j = pl.program_id(1)

    def normmod(xv, sh_ref, sc_ref):
        ms = jnp.mean(xv * xv, axis=-1, keepdims=True)
        y = xv * lax.rsqrt(ms + EPS) * n1_ref[...]
        return (y * (1.0 + sc_ref[0]) + sh_ref[0]).astype(BF16)

    @pl.when(j == 0)
    def _():
        h_scr[0:TB, :] = normmod(ctx_ref[0], shc_ref, scc_ref)

    @pl.when(j > 0)
    def _():
        h_scr[0:TB, :] = normmod(xa_ref[0], shl_ref, scl_ref)

    h_scr[TB:2 * TB, :] = normmod(xb_ref[0], shl_ref, scl_ref)
    h = h_scr[...]

    def mm(lo, hi):
        return _dot(h, w_ref[:, lo:hi])

    m = mm(_C_M, _C_END)
    zz = _dot(m.astype(BF16), wup_ref[...]) + bup_ref[...]
    ld_ref[0] = -_softplus(-zz) * (1.0 / GLA_GATE_NORM)
    misc_ref[0] = _softplus(m + dtb_ref[...])
    q_ref[0] = (mm(_C_Q, _C_K) * (GLA_DK ** -0.5)).astype(BF16)
    k_ref[0] = mm(_C_K, _C_V).astype(BF16)
    v_ref[0] = mm(_C_V, _C_G).astype(BF16)
    g_ref[0] = mm(_C_G, _C_Z)
    z_ref[0] = mm(_C_Z, _C_X)
    xbc_ref[0] = mm(_C_X, _C_M)


def _inproj_call(x, ctx, sh1, sc1, n1, w_cat, wup, bup, dtb):
    bsz, L, d = x.shape
    nx = L // TB
    nj = -(-(nx + 1) // 2)
    tok = lambda n: pl.BlockSpec((1, 2 * TB, n), lambda b, j: (b, j, 0))
    const = lambda a: pl.BlockSpec(a.shape, lambda b, j: (0,) * a.ndim)
    xblk = lambda off: pl.BlockSpec((1, TB, d), lambda b, j: (b, jnp.clip(2 * j + off, 0, nx - 1), 0))
    mod_lat = pl.BlockSpec((1, 1, d), lambda b, j: (b, 0, 0))
    mod_ctx = pl.BlockSpec((1, 1, d), lambda b, j: (bsz, 0, 0))
    outs = [(GLA_QK, BF16), (GLA_QK, BF16), (GLA_V, BF16), (GLA_V, F32), (SSD_INNER, F32),
            (SSD_CONV_DIM, F32), (2 * GLA_QK, F32), (LANES, F32)]
    return pl.pallas_call(
        _inproj_kernel,
        grid=(bsz, nj),
        in_specs=[xblk(-1), xblk(0), pl.BlockSpec((1, TB, d), lambda b, j: (b, 0, 0)),
                  mod_lat, mod_lat, mod_ctx, mod_ctx,
                  const(n1), const(w_cat), const(wup), const(bup), const(dtb)],
        out_specs=[tok(n) for n, _ in outs],
        out_shape=[jax.ShapeDtypeStruct((bsz, nj * 2 * TB, n), dt) for n, dt in outs],
        scratch_shapes=[pltpu.VMEM((2 * TB, d), BF16)],
        compiler_params=_params("arbitrary", "arbitrary"),
    )(x, x, ctx, sh1, sc1, sh1, sc1, n1, w_cat, wup, bup, dtb)


_EXT_PAD = 8
_EXT_BASE = _EXT_PAD + GRID_W
_EXT_ROWS = 2 * _EXT_PAD + 2 * GRID_W + TB


def _conv_kernel(prev_ref, cur_ref, next_ref, w_ref, b_ref, xs_ref, bc_ref, ext):
    j = pl.program_id(1)
    nj = pl.num_programs(1)
    is_ctx = j == 0
    zpad = jnp.zeros((_EXT_PAD, SSD_CONV_DIM), F32)
    ext[0:_EXT_PAD, :] = zpad
    ext[_EXT_ROWS - _EXT_PAD:_EXT_ROWS, :] = zpad
    ext[_EXT_PAD:_EXT_BASE, :] = jnp.where(j >= 2, prev_ref[0], 0.0)
    ext[_EXT_BASE:_EXT_BASE + TB, :] = cur_ref[0]
    ext[_EXT_BASE + TB:_EXT_BASE + TB + GRID_W, :] = jnp.where(
        jnp.logical_and(j >= 1, j <= nj - 2), next_ref[0], 0.0)

    t = lax.broadcasted_iota(jnp.int32, (TB, LANES), 0)
    col = t & (GRID_W - 1)
    ok_l = jnp.where(is_ctx, t, col) >= 1
    ok_r = jnp.where(is_ctx, t - (TB - GRID_W), col) <= GRID_W - 2
    lat = jnp.where(is_ctx, 0.0, 1.0)

    for c in range(SSD_CONV_DIM // LANES):
        lo, hi = c * LANES, (c + 1) * LANES
        acc = jnp.zeros((TB, LANES), F32)
        for dr in (-1, 0, 1):
            for dc in (-1, 0, 1):
                start = _EXT_BASE + GRID_W * dr + dc
                tap = ext[start:start + TB, lo:hi]
                wi = 3 * (dr + 1) + (dc + 1)
                wv = w_ref[wi:wi + 1, lo:hi]
                if dr != 0:
                    wv = wv * lat
                if dc == -1:
                    tap = jnp.where(ok_l, tap, 0.0)
                elif dc == 1:
                    tap = jnp.where(ok_r, tap, 0.0)
                acc = acc + tap * wv
        y = acc + b_ref[:, lo:hi]
        y = y * _sigmoid(y)
        if c < SSD_INNER // LANES:
            xs_ref[0, :, lo:hi] = y
        else:
            bc_ref[0, :, lo - SSD_INNER:hi - SSD_INNER] = y.astype(BF16)


def _conv_call(xbc, w9, bias, ls):
    bsz, _, ch = xbc.shape
    nj = ls // TB
    rpb = TB // GRID_W
    nrow = ls // GRID_W
    return pl.pallas_call(
        _conv_kernel,
        grid=(bsz, nj),
        in_specs=[pl.BlockSpec((1, GRID_W, ch), lambda b, j: (b, jnp.maximum(rpb * j - 1, 0), 0)),
                  pl.BlockSpec((1, TB, ch), lambda b, j: (b, j, 0)),
                  pl.BlockSpec((1, GRID_W, ch), lambda b, j: (b, jnp.minimum(rpb * j + rpb, nrow - 1), 0)),
                  pl.BlockSpec((9, ch), lambda b, j: (0, 0)),
                  pl.BlockSpec((1, ch), lambda b, j: (0, 0))],
        out_specs=[pl.BlockSpec((1, TB, SSD_INNER), lambda b, j: (b, j, 0)),
                   pl.BlockSpec((1, TB, ch - SSD_INNER), lambda b, j: (b, j, 0))],
        out_shape=[jax.ShapeDtypeStruct((bsz, ls, SSD_INNER), F32),
                   jax.ShapeDtypeStruct((bsz, ls, ch - SSD_INNER), BF16)],
        scratch_shapes=[pltpu.VMEM((_EXT_ROWS, ch), F32)],
        compiler_params=_params("arbitrary", "arbitrary"),
    )(xbc, xbc, xbc, w9, bias)


def _fwd_blk(s):
    return s


def _bwd_blk(s, ns):
    return jnp.where(s == 0, 0, ns - s)


def _scan_batch(bsz):
    return 2 if bsz % 2 == 0 else 1


class _GlaChunk:
    def __init__(self, q, k, v, la, st_ref, tri_m, fwd, store):
        self.q, self.k, self.v, self.la, self.st_ref, self.tri_m, self.fwd, self.store = (
            q, k, v, la, st_ref, tri_m, fwd, store)

    def stage_sums(self):
        self.b = _dot_hilo_r(self.tri_m, self.la)

    def stage_factors(self):
        C = GLA_C
        b = self.b
        self.bt = b[C - 1:C, :] if self.fwd else b[0:1, :]
        r = 0.5 * self.bt
        self.er = jnp.exp(r)
        self.qt = (self.q.astype(F32) * jnp.exp(jnp.minimum(b - r, EXP_CLAMP))).astype(BF16)
        kt = (self.k.astype(F32) * jnp.exp(jnp.minimum(r - b, EXP_CLAMP))).astype(BF16)
        head_k = lax.broadcasted_iota(jnp.int32, (C, GLA_QK), 1) >> 6
        zero = jnp.zeros_like(kt)
        self.kh = [jnp.where(head_k == h, kt, zero) for h in range(GLA_HEADS)]
        self.qh = [jnp.where(head_k == h, self.qt, zero) for h in range(GLA_HEADS)]

    def stage_products(self):
        v = self.v
        kcat = jnp.concatenate(self.kh, axis=0)
        self.sc = _dot_nt(self.qt, kcat)
        vcat = jnp.concatenate([v[:, h * GLA_DV:(h + 1) * GLA_DV] for h in range(GLA_HEADS)], axis=0)
        self.u = _dot_tn(vcat, kcat) * self.er

    def stage_mask(self):
        C = GLA_C
        ii = lax.broadcasted_iota(jnp.int32, (C, GLA_HEADS * C), 0)
        jj = lax.broadcasted_iota(jnp.int32, (C, GLA_HEADS * C), 1) & (C - 1)
        causal = (jj <= ii) if self.fwd else (jj >= ii)
        self.p = jnp.where(causal, self.sc, 0.0).astype(BF16)
        v = self.v
        head_v = lax.broadcasted_iota(jnp.int32, (C, GLA_V), 1) >> 7
        self.vst = jnp.concatenate([jnp.where(head_v == h, v, jnp.zeros_like(v)) for h in range(GLA_HEADS)], axis=0)

    def stage_intra(self):
        self.o = _dot(self.p, self.vst)

    def stage_inter(self):
        st = self.st_ref[...]
        ster = (st * self.er).astype(BF16)
        self.st_ref[...] = st * jnp.exp(self.bt) + self.u
        res = _dot_nt(jnp.concatenate(self.qh, axis=0), ster)
        self.inter = jnp.concatenate([res[h * GLA_C:(h + 1) * GLA_C, :] for h in range(GLA_HEADS)], axis=1)

    def stage_out(self):
        self.store(self.o + self.inter)


def _gla_kernel(qf_ref, kf_ref, vf_ref, lf_ref, qb_ref, kb_ref, vb_ref, lb_ref, trif_ref, trib_ref,
                of_ref, ob_ref, stf, stb):
    s = pl.program_id(1)

    @pl.when(s == 0)
    def _():
        stf[...] = jnp.zeros_like(stf)
        stb[...] = jnp.zeros_like(stb)

    nsub = TB // GLA_C

    def store_to(ref, bb, sl):
        def store(val):
            ref[bb, sl, :] = val
        return store

    group = GLA_STAGE_GROUP
    for g0 in range(0, nsub, group):
        steps = []
        for i in range(g0, g0 + group):
            sf = pl.ds(i * GLA_C, GLA_C)
            sb = pl.ds((nsub - 1 - i) * GLA_C, GLA_C)
            chunks = []
            for bb in range(qf_ref.shape[0]):
                chunks.append(_GlaChunk(qf_ref[bb, sf, :], kf_ref[bb, sf, :], vf_ref[bb, sf, :], lf_ref[bb, sf, :],
                                        stf.at[bb], trif_ref[...], True, store_to(of_ref, bb, sf)))
                chunks.append(_GlaChunk(qb_ref[bb, sb, :], kb_ref[bb, sb, :], vb_ref[bb, sb, :], lb_ref[bb, sb, :],
                                        stb.at[bb], trib_ref[...], False, store_to(ob_ref, bb, sb)))
            steps.append(chunks)
        for stage in ("stage_sums", "stage_factors", "stage_products", "stage_mask", "stage_intra"):
            for chunks in steps:
                for ch in chunks:
                    getattr(ch, stage)()
        for chunks in steps:
            for stage in ("stage_inter", "stage_out"):
                for ch in chunks:
                    getattr(ch, stage)()


def _gla_call(q, k, v, ld, L):
    bsz = q.shape[0]
    nx = L // TB
    ns = nx + 1
    trif = jnp.asarray(np.tril(np.ones((GLA_C, GLA_C), np.float32)), BF16)
    trib = jnp.asarray(np.triu(np.ones((GLA_C, GLA_C), np.float32)), BF16)
    nb = _scan_batch(bsz)
    f = lambda n, lane=0: pl.BlockSpec((nb, TB, n), lambda b, s: (b, _fwd_blk(s), lane))
    r = lambda n, lane=0: pl.BlockSpec((nb, TB, n), lambda b, s: (b, _bwd_blk(s, ns), lane))
    tri = pl.BlockSpec((GLA_C, GLA_C), lambda b, s: (0, 0))
    return pl.pallas_call(
        _gla_kernel,
        grid=(bsz // nb, ns),
        in_specs=[f(GLA_QK), f(GLA_QK), f(GLA_V), f(GLA_QK, 0),
                  r(GLA_QK), r(GLA_QK), r(GLA_V), r(GLA_QK, 1), tri, tri],
        out_specs=[pl.BlockSpec((nb, TB, GLA_V), lambda b, s: (b, jnp.maximum(s - 1, 0), 0)),
                   pl.BlockSpec((nb, TB, GLA_V), lambda b, s: (b, jnp.where(s == 0, nx - 1, nx - s), 0))],
        out_shape=[jax.ShapeDtypeStruct((bsz, L, GLA_V), F32)] * 2,
        scratch_shapes=[pltpu.VMEM((nb, GLA_DV, GLA_QK), F32)] * 2,
        compiler_params=_params("arbitrary", "arbitrary"),
    )(q, k, v, ld, q, k, v, ld, trif, trib)


class _SsdChunk:
    def __init__(self, xs, bc, dtm, avec, dvec, st_ref, tri_m, e_m, base, fwd, store):
        self.xs, self.bc, self.dtm, self.avec, self.dvec, self.st_ref = xs, bc, dtm, avec, dvec, st_ref
        self.tri_m, self.e_m, self.base, self.fwd, self.store = tri_m, e_m, base, fwd, store

    def stage_sums(self):
        self.dt_exp = _dot_hilo_l(self.dtm, self.e_m)
        self.acum = _dot_exact_r(self.tri_m, self.dtm * self.avec)

    def stage_expand(self):
        self.acum_exp = _dot_hilo_l(self.acum, self.e_m)
        self.acum_t = self.acum.T
        self.xdt = self.xs * self.dt_exp
        bc = self.bc
        self.bg = [bc[:, 128 * g:128 * (g + 1)] for g in range(SSD_GROUPS)]
        self.cg = [bc[:, 256 + 128 * g:256 + 128 * (g + 1)] for g in range(SSD_GROUPS)]
        self.cb = [_dot_nt(self.cg[g], self.bg[g]) for g in range(SSD_GROUPS)]

    def stage_decay(self):
        C = SSD_C
        ii = lax.broadcasted_iota(jnp.int32, (C, C), 0)
        jj = lax.broadcasted_iota(jnp.int32, (C, C), 1)
        tri = (jj <= ii) if self.fwd else (jj >= ii)
        self.ms = []
        for g in range(SSD_GROUPS):
            for rr in range(SSD_HPG):
                ln = self.base + SSD_HPG * g + rr
                diff = self.acum[:, ln:ln + 1] - self.acum_t[ln:ln + 1, :]
                seg = jnp.where(tri, jnp.exp(jnp.minimum(diff, 0.0)), 0.0)
                self.ms.append((self.cb[g] * seg).astype(BF16))
        ae = self.acum_exp
        self.al_exp = ae[C - 1:C, :] if self.fwd else ae[0:1, :]
        self.xw = (self.xdt * jnp.exp(self.al_exp - ae)).astype(BF16)
        xdt_b = self.xdt.astype(BF16)
        head = lax.broadcasted_iota(jnp.int32, (C, SSD_INNER), 1) >> 6
        zero = jnp.zeros_like(xdt_b)
        self.xh = [jnp.where(head == h, xdt_b, zero) for h in range(SSD_HEADS)]

    def stage_products(self):
        self.yg, self.ug = [], []
        for g in range(SSD_GROUPS):
            gl, gh = 256 * g, 256 * (g + 1)
            yg = _dot(self.ms[SSD_HPG * g], self.xh[SSD_HPG * g][:, gl:gh])
            for rr in range(1, SSD_HPG):
                yg = yg + _dot(self.ms[SSD_HPG * g + rr], self.xh[SSD_HPG * g + rr][:, gl:gh])
            self.yg.append(yg)
            self.ug.append(_dot_tn(self.bg[g], self.xw[:, gl:gh]))

    def stage_state(self):
        ys = []
        for g in range(SSD_GROUPS):
            gl, gh = 256 * g, 256 * (g + 1)
            sg = self.st_ref[g]
            yoff = _dot(self.cg[g], sg.astype(BF16)) * jnp.exp(self.acum_exp[:, gl:gh])
            self.st_ref[g] = sg * jnp.exp(self.al_exp[:, gl:gh]) + self.ug[g]
            ys.append(self.yg[g] + yoff)
        y = jnp.concatenate(ys, axis=1)
        if self.dvec is not None:
            y = y + self.dvec * self.xs
        self.store(y)


def _ssd_kernel(xf_ref, bcf_ref, mf_ref, xb_ref, bcb_ref, mb_ref, af_ref, ab_ref, d_ref,
                trif_ref, trib_ref, ef_ref, eb_ref, yf_ref, yb_ref, stf, stb):
    s = pl.program_id(1)

    @pl.when(s == 0)
    def _():
        stf[...] = jnp.zeros_like(stf)
        stb[...] = jnp.zeros_like(stb)

    nsub = TB // SSD_C

    def store_to(ref, bb, sl):
        def store(val):
            ref[bb, sl, :] = val
        return store

    steps = []
    for i in range(nsub):
        sf = pl.ds(i * SSD_C, SSD_C)
        sb = pl.ds((nsub - 1 - i) * SSD_C, SSD_C)
        chunks = []
        for bb in range(xf_ref.shape[0]):
            chunks.append(_SsdChunk(xf_ref[bb, sf, :], bcf_ref[bb, sf, :], mf_ref[bb, sf, :], af_ref[...],
                                    d_ref[...], stf.at[bb], trif_ref[...], ef_ref[...], DT_F, True,
                                    store_to(yf_ref, bb, sf)))
            chunks.append(_SsdChunk(xb_ref[bb, sb, :], bcb_ref[bb, sb, :], mb_ref[bb, sb, :], ab_ref[...],
                                    None, stb.at[bb], trib_ref[...], eb_ref[...], DT_B, False,
                                    store_to(yb_ref, bb, sb)))
        steps.append(chunks)
    for g0 in range(0, nsub, SSD_STAGE_GROUP):
        group = steps[g0:g0 + SSD_STAGE_GROUP]
        for stage in ("stage_sums", "stage_expand", "stage_decay", "stage_products"):
            for chunks in group:
                for ch in chunks:
                    getattr(ch, stage)()
        for chunks in group:
            for ch in chunks:
                ch.stage_state()


def _expand_matrix(base):
    e = np.zeros((LANES, SSD_INNER), np.float32)
    for h in range(SSD_HEADS):
        e[base + h, SSD_HEADDIM * h:SSD_HEADDIM * (h + 1)] = 1.0
    return jnp.asarray(np.concatenate([e, e], axis=0), BF16)


def _ssd_call(xs, bc, misc, a_f, a_b, dvec, L):
    bsz, ls, _ = xs.shape
    ns = ls // TB
    nx = L // TB
    trif = jnp.asarray(np.tril(np.ones((SSD_C, SSD_C), np.float32)), BF16)
    trib = jnp.asarray(np.triu(np.ones((SSD_C, SSD_C), np.float32)), BF16)
    ef, eb = _expand_matrix(DT_F), _expand_matrix(DT_B)
    nb = _scan_batch(bsz)
    f = lambda n: pl.BlockSpec((nb, TB, n), lambda b, s: (b, _fwd_blk(s), 0))
    r = lambda n: pl.BlockSpec((nb, TB, n), lambda b, s: (b, _bwd_blk(s, ns), 0))
    const = lambda a: pl.BlockSpec(a.shape, lambda b, s: (0,) * a.ndim)
    return pl.pallas_call(
        _ssd_kernel,
        grid=(bsz // nb, ns),
        in_specs=[f(SSD_INNER), f(512), f(LANES), r(SSD_INNER), r(512), r(LANES),
                  const(a_f), const(a_b), const(dvec), const(trif), const(trib), const(ef), const(eb)],
        out_specs=[pl.BlockSpec((nb, TB, SSD_INNER), lambda b, s: (b, jnp.maximum(s - 1, 0), 0)),
                   pl.BlockSpec((nb, TB, SSD_INNER), lambda b, s: (b, jnp.where(s == 0, nx - 1, nx - s), 0))],
        out_shape=[jax.ShapeDtypeStruct((bsz, L, SSD_INNER), F32)] * 2,
        scratch_shapes=[pltpu.VMEM((nb, SSD_GROUPS, SSD_STATE, SSD_HPG * SSD_HEADDIM), F32)] * 2,
        compiler_params=_params("arbitrary", "arbitrary"),
    )(xs, bc, misc, xs, bc, misc, a_f, a_b, dvec, trif, trib, ef, eb)


def _outproj_kernel(of_ref, ob_ref, ga_ref, gb_ref, yf_ref, yb_ref, za_ref, zb_ref, x_ref, g1_ref, sh2_ref, sc2_ref,
                    gn_ref, sn_ref, wo_ref, n2_ref, wrh_ref, wrl_ref, br_ref,
                    x1_ref, h2_ref, eidx_ref, gate_ref, cnt_ref, mix_s, hl_s):
    first = jnp.logical_and(pl.program_id(0) == 0, pl.program_id(1) == 0)

    @pl.when(first)
    def _():
        cnt_ref[...] = jnp.zeros_like(cnt_ref)

    halves = [(pl.ds(0, TB), ga_ref, za_ref), (pl.ds(TB, TB), gb_ref, zb_ref)]

    for rows, g_ref, z_ref in halves:
        o = of_ref[0, rows, :] + ob_ref[0, rows, :]
        gg = g_ref[0]
        for h in range(GLA_HEADS):
            lo, hi = GLA_DV * h, GLA_DV * (h + 1)
            oh = o[:, lo:hi]
            ms = jnp.mean(oh * oh, axis=-1, keepdims=True)
            gh = gg[:, lo:hi]
            mix_s[rows, lo:hi] = (oh * lax.rsqrt(ms + EPS) * gn_ref[:, lo:hi] * (gh * _sigmoid(gh))).astype(BF16)
        zz = z_ref[0]
        u = (yf_ref[0, rows, :] + yb_ref[0, rows, :]) * (zz * _sigmoid(zz))
        gw = SSD_INNER // SSD_GROUPS
        for g in range(SSD_GROUPS):
            lo, hi = gw * g, gw * (g + 1)
            ug = u[:, lo:hi]
            ms = jnp.mean(ug * ug, axis=-1, keepdims=True)
            mix_s[rows, GLA_V + lo:GLA_V + hi] = (ug * lax.rsqrt(ms + EPS) * sn_ref[:, lo:hi]).astype(BF16)

    x1_ref[0] = x_ref[0] + g1_ref[0] * _dot(mix_s[...], wo_ref[...])

    for rows, _, _ in halves:
        x1 = x1_ref[0, rows, :]
        ms = jnp.mean(x1 * x1, axis=-1, keepdims=True)
        h2 = (x1 * lax.rsqrt(ms + EPS) * n2_ref[...]) * (1.0 + sc2_ref[0]) + sh2_ref[0]
        h_hi = h2.astype(BF16)
        h2_ref[0, rows, :] = h_hi
        hl_s[rows, :] = (h2 - h_hi.astype(F32)).astype(BF16)

    h_hi = h2_ref[0]
    logits = (_dot(h_hi, wrh_ref[...]) + _dot(hl_s[...], wrh_ref[...]) + _dot(h_hi, wrl_ref[...])) + br_ref[...]

    lane = lax.broadcasted_iota(jnp.int32, (TB, LANES), 1).astype(F32)
    st = [dict(work=logits[i * TB:(i + 1) * TB, :], eidx=jnp.full((TB, LANES), -1.0, F32),
               gates=jnp.zeros((TB, LANES), F32), sel=jnp.zeros((TB, LANES), F32), m0=None,
               den=jnp.zeros((TB, 1), F32)) for i in range(2)]
    for kk in range(TOP_K):
        for t in st:
            t["m"] = jnp.max(t["work"], axis=-1, keepdims=True)
        for t in st:
            t["idx"] = jnp.min(jnp.where(t["work"] == t["m"], lane, float(LANES)), axis=-1, keepdims=True)
        for t in st:
            hit = lane == t["idx"]
            if t["m0"] is None:
                t["m0"] = t["m"]
            e = jnp.exp(t["m"] - t["m0"])
            t["den"] = t["den"] + e
            t["eidx"] = jnp.where(lane == float(kk), t["idx"], t["eidx"])
            t["gates"] = jnp.where(lane == float(kk), e, t["gates"])
            t["sel"] = jnp.where(hit, 1.0, t["sel"])
            t["work"] = jnp.where(hit, NEG_BIG, t["work"])
    for (rows, _, _), t in zip(halves, st):
        eidx_ref[0, rows, :] = t["eidx"].astype(jnp.int32)
        gate_ref[0, rows, :] = t["gates"] / t["den"]
        cnt = jnp.sum(t["sel"], axis=0, keepdims=True)
        cnt_ref[...] += jnp.floor((cnt + (RUN - 1.0)) * (1.0 / RUN)) * RUN


def _outproj_call(o_f, o_b, g_all, y_f, y_b, z_all, x, g1, sh2, sc2, gn, sn, wo, n2, wr_hi, wr_lo, br):
    bsz, L, d = x.shape
    nj = L // (2 * TB)
    tok = lambda n: pl.BlockSpec((1, 2 * TB, n), lambda b, j: (b, j, 0))
    tok_off = lambda n, half: pl.BlockSpec((1, TB, n), lambda b, j: (b, 2 * j + 1 + half, 0))
    const = lambda a: pl.BlockSpec(a.shape, lambda b, j: (0,) * a.ndim)
    mod = pl.BlockSpec((1, 1, d), lambda b, j: (b, 0, 0))
    return pl.pallas_call(
        _outproj_kernel,
        grid=(bsz, nj),
        in_specs=[tok(GLA_V), tok(GLA_V), tok_off(GLA_V, 0), tok_off(GLA_V, 1),
                  tok(SSD_INNER), tok(SSD_INNER), tok_off(SSD_INNER, 0), tok_off(SSD_INNER, 1),
                  tok(d), mod, mod, mod, const(gn), const(sn), const(wo), const(n2),
                  const(wr_hi), const(wr_lo), const(br)],
        out_specs=[tok(d), tok(d), tok(LANES), tok(LANES), pl.BlockSpec((1, LANES), lambda b, j: (0, 0))],
        out_shape=[jax.ShapeDtypeStruct((bsz, L, d), F32), jax.ShapeDtypeStruct((bsz, L, d), BF16),
                   jax.ShapeDtypeStruct((bsz, L, LANES), jnp.int32), jax.ShapeDtypeStruct((bsz, L, LANES), F32),
                   jax.ShapeDtypeStruct((1, LANES), F32)],
        scratch_shapes=[pltpu.VMEM((2 * TB, d), BF16), pltpu.VMEM((2 * TB, d), BF16)],
        compiler_params=_params("arbitrary", "arbitrary"),
    )(o_f, o_b, g_all, g_all, y_f, y_b, z_all, z_all, x, g1, sh2, sc2, gn, sn, wo, n2, wr_hi, wr_lo, br)


def _pos_kernel(eidx_ref, pstart_ref, lst_ref, ust_ref, lp_ref, cd_ref, carry):
    @pl.when(pl.program_id(0) == 0)
    def _():
        carry[...] = pstart_ref[...]

    eidx = eidx_ref[...]
    lane = lax.broadcasted_iota(jnp.int32, (TB, LANES), 1)
    hits = [lane == eidx[:, kk:kk + 1] for kk in range(TOP_K)]
    sel = jnp.zeros((TB, LANES), F32)
    for hmask in hits:
        sel = jnp.where(hmask, 1.0, sel)
    cnt = jnp.sum(sel, axis=0, keepdims=True)
    run = jnp.floor((cnt + (RUN - 1.0)) * (1.0 / RUN)) * RUN
    rank = _dot(lst_ref[...], sel.astype(BF16))
    loff = _dot(jnp.broadcast_to(run, (8, LANES)).astype(BF16), ust_ref[...])[0:1]
    pos = loff + rank
    lp = jnp.zeros((TB, LANES), jnp.int32)
    for kk, hmask in enumerate(hits):
        lk = jnp.sum(jnp.where(hmask, pos, 0.0), axis=-1, keepdims=True)
        lp = jnp.where(lane == kk, lk.astype(jnp.int32), lp)
    lp_ref[...] = lp

    base = carry[...]
    eye = lax.broadcasted_iota(jnp.int32, (LANES, LANES), 0) == lax.broadcasted_iota(jnp.int32, (LANES, LANES), 1)
    col = lambda v: jnp.sum(jnp.where(eye, v, 0.0), axis=1, keepdims=True)
    loff_c, run_c, shift_c = col(loff), col(run), col(base - loff)
    row0 = (lax.broadcasted_iota(jnp.int32, (LANES, CD_LANES), 1) * RUN).astype(F32)
    inside = jnp.where(row0 >= loff_c, 1.0, 0.0) * jnp.where(row0 < loff_c + run_c, 1.0, 0.0)
    valid = jnp.sum(inside, axis=0, keepdims=True)
    dest = jnp.sum(inside * shift_c, axis=0, keepdims=True) + row0[0:1]
    cd_ref[0] = jnp.where(valid > 0.0, dest, -1.0).astype(jnp.int32)
    carry[...] = base + run


def _pos_call(eidx, pstart):
    T = eidx.shape[0]
    lst = jnp.asarray(np.tril(np.ones((TB, TB), np.float32), -1), BF16)
    ust = jnp.asarray(np.triu(np.ones((LANES, LANES), np.float32), 1), BF16)
    return pl.pallas_call(
        _pos_kernel,
        grid=(T // TB,),
        in_specs=[pl.BlockSpec((TB, LANES), lambda i: (i, 0)),
                  pl.BlockSpec((1, LANES), lambda i: (0, 0)),
                  pl.BlockSpec((TB, TB), lambda i: (0, 0)),
                  pl.BlockSpec((LANES, LANES), lambda i: (0, 0))],
        out_specs=[pl.BlockSpec((TB, LANES), lambda i: (i, 0)),
                   pl.BlockSpec((1, 1, CD_LANES), lambda i: (i, 0, 0))],
        out_shape=[jax.ShapeDtypeStruct((T, LANES), jnp.int32),
                   jax.ShapeDtypeStruct((T // TB, 1, CD_LANES), jnp.int32)],
        scratch_shapes=[pltpu.VMEM((1, LANES), F32)],
        compiler_params=_params("arbitrary"),
    )(eidx, pstart, lst, ust)


def _pair_matrix(lp, weights):
    j = lax.broadcasted_iota(jnp.int32, (TB, RL), 1)
    m = jnp.zeros((TB, RL), F32)
    for kk in range(TOP_K):
        w = 1.0 if weights is None else weights[:, kk:kk + 1]
        m = m + jnp.where(j == lp[:, kk:kk + 1], w, 0.0)
    return m


def _pack_bf16_pairs(v, is_bf16_valued=False):
    if not is_bf16_valued:
        v = v.astype(BF16).astype(F32)
    bits = pltpu.bitcast(v, jnp.uint32)
    return bits[:, D_HALF:] | (bits[:, :D_HALF] >> 16)


def _unpack_bf16_pairs(w):
    lo = pltpu.bitcast(w << 16, F32).astype(BF16)
    hi = pltpu.bitcast(w & jnp.uint32(0xFFFF0000), F32).astype(BF16)
    return lo, hi


N_SLOTS = 3


def _dispatch_kernel(fill_ref, cdp_ref, cdc_ref, h_ref, lp_ref, xg_ref, sorted_s, sems, fill_sem):
    i = pl.program_id(0)
    n = pl.num_programs(0)
    slot = lax.rem(i, N_SLOTS)
    prev = lax.rem(i + N_SLOTS - 1, N_SLOTS)
    pprev = lax.rem(i + N_SLOTS - 2, N_SLOTS)
    spill0 = xg_ref.shape[0] - N_SLOTS * RL

    def start(cref, s, c, to_spill, sem):
        src = RUN * c if isinstance(c, int) else pl.multiple_of(RUN * c, RUN)
        d = cref[0, 0, c]
        d = pl.multiple_of(jnp.where(jnp.logical_or(d < 0, to_spill), spill0 + s * RL + src, d), RUN)
        pltpu.make_async_copy(sorted_s.at[s, pl.ds(src, RUN), :], xg_ref.at[pl.ds(d, RUN), :], sem).start()

    def drain(s, sem):
        pltpu.make_async_copy(sorted_s.at[s], xg_ref.at[pl.ds(0, RL), :], sem).wait()

    @pl.when(i == 0)
    def _():
        sorted_s[...] = jnp.zeros_like(sorted_s)
        for s in range(N_SLOTS):
            lax.fori_loop(0, RL // RUN, lambda c, carry, s=s: (start(cdp_ref, s, c, True, fill_sem), carry)[1], 0)
        for s in range(N_SLOTS):
            drain(s, fill_sem)

    for c in range(RL // RUN):
        start(cdp_ref, prev, c, i == 0, sems.at[prev])

    sorted_s[slot] = _pack_bf16_pairs(_dot_tn(_pair_matrix(lp_ref[...], None).astype(BF16), h_ref[...]),
                                      is_bf16_valued=True)

    @pl.when(i == 0)
    def _():
        def fill(k):
            blk = pl.multiple_of(fill_ref[k] * MOE_BLOCK, MOE_BLOCK)
            return pltpu.make_async_copy(sorted_s.at[0, pl.ds(0, MOE_BLOCK), :],
                                         xg_ref.at[pl.ds(blk, MOE_BLOCK), :], fill_sem)

        for k in range(fill_ref.shape[0]):
            pl.when(fill_ref[k] >= 0)(lambda k=k: fill(k).start())
        for k in range(fill_ref.shape[0]):
            pl.when(fill_ref[k] >= 0)(lambda k=k: fill(k).wait())

    pl.when(i >= 1)(lambda: drain(pprev, sems.at[pprev]))

    @pl.when(i == n - 1)
    def _():
        lax.fori_loop(0, RL // RUN, lambda c, carry: (start(cdc_ref, slot, c, False, sems.at[slot]), carry)[1], 0,
                      unroll=4)
        drain(prev, sems.at[prev])
        drain(slot, sems.at[slot])


def _dispatch_call(fill_blocks, cd, h2, lp, P):
    T, d = h2.shape
    grid_spec = pltpu.PrefetchScalarGridSpec(
        num_scalar_prefetch=1,
        grid=(T // TB,),
        in_specs=[pl.BlockSpec((1, 1, CD_LANES), lambda i, fb: (jnp.maximum(i - 1, 0), 0, 0),
                               memory_space=pltpu.SMEM),
                  pl.BlockSpec((1, 1, CD_LANES), lambda i, fb: (i, 0, 0), memory_space=pltpu.SMEM),
                  pl.BlockSpec((TB, d), lambda i, fb: (i, 0)),
                  pl.BlockSpec((TB, LANES), lambda i, fb: (i, 0))],
        out_specs=pl.BlockSpec(memory_space=pl.ANY),
        scratch_shapes=[pltpu.VMEM((N_SLOTS, RL, D_HALF), jnp.uint32), pltpu.SemaphoreType.DMA((N_SLOTS,)),
                        pltpu.SemaphoreType.DMA(())],
    )
    return pl.pallas_call(
        _dispatch_kernel,
        grid_spec=grid_spec,
        out_shape=jax.ShapeDtypeStruct((P + N_SLOTS * RL, D_HALF), jnp.uint32),
        compiler_params=_params("arbitrary"),
    )(fill_blocks, cd, cd, h2, lp)


def _moe_kernel(info_ref, x_ref, wgu_hbm, bgu_ref, wd_hbm, bd_ref, y_ref, wgu_f, wd_f, wgu_b, wd_b, act_s, sems):
    i = pl.program_id(0)
    nb = pl.num_programs(0)
    n_used = info_ref[0, nb]
    e = info_ref[0, i]

    def fetch(expert, slot):
        return (pltpu.make_async_copy(wgu_hbm.at[expert], wgu_f.at[slot], sems.at[0, slot]),
                pltpu.make_async_copy(wd_hbm.at[expert], wd_f.at[slot], sems.at[1, slot]))

    @pl.when(i == 0)
    def _():
        for cp in fetch(e, 0):
            cp.start()

    @pl.when(info_ref[1, i] == 1)
    def _():
        slot = info_ref[2, i]
        nxt = info_ref[3, i]
        cps = fetch(e, slot)
        cps[0].wait()
        wgu_b[...] = wgu_f[slot].astype(BF16)
        cps[1].wait()
        wd_b[...] = wd_f[slot].astype(BF16)

        @pl.when(nxt >= 0)
        def _():
            for cp in fetch(nxt, 1 - slot):
                cp.start()

    @pl.when(i < n_used)
    def _():
        bgu = bgu_ref[pl.ds(e, 1), :]
        xb = jnp.concatenate(_unpack_bf16_pairs(x_ref[...]), axis=1)
        cw = 256
        for c in range(D_FF // cw):
            lo, hi = c * cw, (c + 1) * cw
            gate = _dot(xb, wgu_b[:, lo:hi]) + bgu[:, lo:hi]
            up = _dot(xb, wgu_b[:, D_FF + lo:D_FF + hi]) + bgu[:, D_FF + lo:D_FF + hi]
            gate = jnp.minimum(gate, SWIGLU_LIMIT)
            up = jnp.clip(up, -SWIGLU_LIMIT, SWIGLU_LIMIT)
            act_s[:, lo:hi] = ((up + 1.0) * (gate * _sigmoid(SWIGLU_ALPHA * gate))).astype(BF16)
        y_ref[...] = _pack_bf16_pairs(_dot(act_s[...], wd_b[...]) + bd_ref[pl.ds(e, 1), :])

    @pl.when(i >= n_used)
    def _():
        y_ref[...] = jnp.zeros_like(y_ref)


def _moe_call(info, xg, wgu, bgu, wd, bd, nb):
    P = nb * MOE_BLOCK
    ne, d, f2 = wgu.shape
    grid_spec = pltpu.PrefetchScalarGridSpec(
        num_scalar_prefetch=1,
        grid=(nb,),
        in_specs=[pl.BlockSpec((MOE_BLOCK, D_HALF), lambda i, info: (jnp.minimum(i, info[0, nb] - 1), 0)),
                  pl.BlockSpec(memory_space=pl.ANY),
                  pl.BlockSpec((ne, f2), lambda i, info: (0, 0)),
                  pl.BlockSpec(memory_space=pl.ANY),
                  pl.BlockSpec((ne, d), lambda i, info: (0, 0))],
        out_specs=pl.BlockSpec((MOE_BLOCK, D_HALF), lambda i, info: (i, 0)),
        scratch_shapes=[pltpu.VMEM((2, d, f2), F32), pltpu.VMEM((2, f2 // 2, d), F32),
                        pltpu.VMEM((d, f2), BF16), pltpu.VMEM((f2 // 2, d), BF16),
                        pltpu.VMEM((MOE_BLOCK, f2 // 2), BF16), pltpu.SemaphoreType.DMA((2, 2))],
    )
    return pl.pallas_call(
        _moe_kernel,
        grid_spec=grid_spec,
        out_shape=jax.ShapeDtypeStruct((P, D_HALF), jnp.uint32),
        compiler_params=_params("arbitrary"),
    )(info, xg, wgu, bgu, wd, bd)


def _combine_kernel(d0_ref, d1_ref, d2_ref, y_ref, lp_ref, x1_ref, gate_ref, g2_ref, fn_ref, o_ref, buf, sems):
    i = pl.program_id(0)
    n = pl.num_programs(0)
    slot = lax.rem(i, N_SLOTS)
    ahead = lax.rem(i + 2, N_SLOTS)

    def start(dref, s, c):
        d = pl.multiple_of(jnp.maximum(dref[0, 0, c], 0), RUN)
        dst = pl.multiple_of(RUN * c, RUN)
        pltpu.make_async_copy(y_ref.at[pl.ds(d, RUN), :], buf.at[s, pl.ds(dst, RUN), :], sems.at[s]).start()

    def drain(s):
        pltpu.make_async_copy(y_ref.at[pl.ds(0, RL), :], buf.at[s], sems.at[s]).wait()

    @pl.when(i == 0)
    def _():
        lax.fori_loop(0, RL // RUN, lambda c, carry: (start(d0_ref, 0, c), carry)[1], 0, unroll=4)
        lax.fori_loop(0, RL // RUN, lambda c, carry: (start(d1_ref, 1, c), carry)[1], 0, unroll=4)

    drain(slot)

    for c in range(RL // RUN):
        start(d2_ref, ahead, c)

    g = _pair_matrix(lp_ref[...], gate_ref[...]).astype(BF16)
    halves = [_dot(g, yb) for yb in _unpack_bf16_pairs(buf[slot])]
    xo = x1_ref[...] + g2_ref[0] * jnp.concatenate(halves, axis=1)
    ms = jnp.mean(xo * xo, axis=-1, keepdims=True)
    o_ref[...] = xo * lax.rsqrt(ms + EPS) * fn_ref[...]

    @pl.when(i == n - 1)
    def _():
        drain(lax.rem(i + 1, N_SLOTS))
        drain(ahead)


def _combine_call(cd, y, lp, x1, gates, g2, fn):
    T, d = x1.shape
    n = T // TB
    per_batch = n // g2.shape[0]
    return pl.pallas_call(
        _combine_kernel,
        grid=(n,),
        in_specs=[pl.BlockSpec((1, 1, CD_LANES), lambda i: (i, 0, 0), memory_space=pltpu.SMEM),
                  pl.BlockSpec((1, 1, CD_LANES), lambda i: (jnp.minimum(i + 1, n - 1), 0, 0),
                               memory_space=pltpu.SMEM),
                  pl.BlockSpec((1, 1, CD_LANES), lambda i: (jnp.minimum(i + 2, n - 1), 0, 0),
                               memory_space=pltpu.SMEM),
                  pl.BlockSpec(memory_space=pl.ANY),
                  pl.BlockSpec((TB, LANES), lambda i: (i, 0)),
                  pl.BlockSpec((TB, d), lambda i: (i, 0)),
                  pl.BlockSpec((TB, LANES), lambda i: (i, 0)),
                  pl.BlockSpec((1, 1, d), lambda i: (i // per_batch, 0, 0)),
                  pl.BlockSpec((1, d), lambda i: (0, 0))],
        out_specs=pl.BlockSpec((TB, d), lambda i: (i, 0)),
        out_shape=jax.ShapeDtypeStruct((T, d), F32),
        scratch_shapes=[pltpu.VMEM((N_SLOTS, RL, D_HALF), jnp.uint32), pltpu.SemaphoreType.DMA((N_SLOTS,))],
        compiler_params=_params("arbitrary"),
    )(cd, cd, cd, y, lp, x1, gates, g2, fn)


def _layer(x, c, ctx, c_ctx, w_mod, b_mod, norm1, w_in, gla_w_gk_up, gla_b_gk, gla_norm,
           ssd_conv_w, ssd_conv_b, ssd_dt_bias, ssd_A_log, ssd_D, ssd_norm, w_out,
           norm2, w_router, b_router, w_gate_up, b_gate_up, w_down, b_down, final_norm):
    bsz, L, d = x.shape
    lc = ctx.shape[1]
    assert lc == TB and L % TB == 0 and TB % GRID_W == 0

    cin = jnp.zeros((8, d), F32).at[:bsz].set(c).at[bsz].set(c_ctx)
    mod = _mod_call(cin, w_mod, b_mod.reshape(1, -1))[:bsz + 1]
    sh1, sc1, g1, sh2, sc2, g2 = [m.reshape(bsz + 1, 1, d) for m in jnp.split(mod, 6, axis=-1)]

    o = np.cumsum((0, GLA_QK, GLA_QK, GLA_V, GLA_V, GLA_RANK, SSD_INNER, SSD_CONV_DIM, SSD_HEADS))
    wq, wk, wv, wg, wlow, wz, wx, wdt = [w_in[:, int(a):int(b)] for a, b in zip(o[:-1], o[1:])]
    w_misc = jnp.concatenate([wlow, wdt, wdt, jnp.zeros((d, LANES - GLA_RANK - 2 * SSD_HEADS), F32)], axis=1)
    w_cat = jnp.concatenate([wq, wk, wv, wg, wz, wx, w_misc], axis=1).astype(BF16)
    wup = jnp.zeros((LANES, 2 * GLA_QK), F32).at[:GLA_RANK].set(
        jnp.concatenate([gla_w_gk_up[0], gla_w_gk_up[1]], axis=1)).astype(BF16)
    bup = jnp.concatenate([gla_b_gk[0], gla_b_gk[1]]).reshape(1, -1)
    dtb = jnp.zeros((1, LANES), F32).at[0, DT_F:DT_F + SSD_HEADS].set(ssd_dt_bias[0]) \
                                    .at[0, DT_B:DT_B + SSD_HEADS].set(ssd_dt_bias[1])
    q, k, v, g_all, z_all, xbc, ld, misc = _inproj_call(
        x, ctx, sh1, sc1, norm1.reshape(1, d), w_cat, wup, bup, dtb)

    xs, bc = _conv_call(xbc, ssd_conv_w.reshape(9, SSD_CONV_DIM), ssd_conv_b.reshape(1, -1), lc + L)

    o_f, o_b = _gla_call(q, k, v, ld, L)

    a_neg = -jnp.exp(ssd_A_log.astype(F32))
    a_f = jnp.zeros((1, LANES), F32).at[0, DT_F:DT_F + SSD_HEADS].set(a_neg[0])
    a_b = jnp.zeros((1, LANES), F32).at[0, DT_B:DT_B + SSD_HEADS].set(a_neg[1])
    dvec = jnp.repeat(ssd_D, SSD_HEADDIM).reshape(1, SSD_INNER)
    y_f, y_b = _ssd_call(xs, bc, misc, a_f, a_b, dvec, L)

    wr = jnp.zeros((d, LANES), F32).at[:, :N_EXPERTS].set(w_router)
    wr_hi = wr.astype(BF16)
    wr_lo = (wr - wr_hi.astype(F32)).astype(BF16)
    br = jnp.full((1, LANES), NEG_BIG, F32).at[0, :N_EXPERTS].set(b_router)
    x1, h2, eidx, gates, counts = _outproj_call(
        o_f, o_b, g_all, y_f, y_b, z_all, x, g1[:bsz], sh2[:bsz], sc2[:bsz],
        jnp.tile(gla_norm, GLA_HEADS).reshape(1, -1), ssd_norm.reshape(1, -1), w_out.astype(BF16),
        norm2.reshape(1, d), wr_hi, wr_lo, br)

    T = bsz * L
    cnt = counts[0, :N_EXPERTS].astype(jnp.int32)
    padded = ((cnt + MOE_BLOCK - 1) // MOE_BLOCK) * MOE_BLOCK
    pend = jnp.cumsum(padded)
    pstart = pend - padded
    max_rows = T * TOP_K + (T // TB) * N_EXPERTS * (RUN - 1)
    n_blocks = -(-max_rows // MOE_BLOCK) + N_EXPERTS
    blk_start = jnp.arange(n_blocks, dtype=jnp.int32) * MOE_BLOCK
    blk_e = jnp.minimum(jnp.sum(pend[None, :] <= blk_start[:, None], axis=1), N_EXPERTS - 1).astype(jnp.int32)
    n_used = (pend[-1] // MOE_BLOCK).astype(jnp.int32)
    blk_i = jnp.arange(n_blocks, dtype=jnp.int32)
    first = (blk_i < n_used) & ((blk_i == 0) | (blk_e != jnp.roll(blk_e, 1)))
    slot = (jnp.cumsum(first) - 1) & 1
    first_pos = jnp.where(first, blk_i, n_blocks)
    next_first = jnp.roll(lax.cummin(first_pos, reverse=True), -1).at[-1].set(n_blocks)
    nxt = jnp.where(next_first < n_blocks, blk_e[jnp.minimum(next_first, n_blocks - 1)], -1)
    col = lambda v, last: jnp.concatenate([v.astype(jnp.int32), jnp.asarray([last], jnp.int32)])
    blk_info = jnp.stack([col(blk_e, 0).at[-1].set(n_used), col(first, 0), col(slot, 0), col(nxt, -1)])
    n_tail = n_blocks - (T * TOP_K) // MOE_BLOCK
    tail = n_used + jnp.arange(n_tail, dtype=jnp.int32)
    fill_blocks = jnp.concatenate([
        jnp.where(padded > 0, pend // MOE_BLOCK - 1, -1),
        jnp.where(tail < n_blocks, tail, -1)]).astype(jnp.int32)
    pstart_row = jnp.zeros((1, LANES), F32).at[0, :N_EXPERTS].set(pstart.astype(F32))
    lp, cd = _pos_call(eidx.reshape(T, LANES), pstart_row)

    xg = _dispatch_call(fill_blocks, cd, h2.reshape(T, d), lp, n_blocks * MOE_BLOCK)
    y = _moe_call(blk_info, xg, w_gate_up, b_gate_up, w_down, b_down, n_blocks)
    out = _combine_call(cd, y, lp, x1.reshape(T, d), gates.reshape(T, LANES), g2[:bsz], final_norm.reshape(1, d))
    return out.reshape(bsz, L, d)


def kernel(x, c, ctx, c_ctx, w_mod, b_mod, norm1, w_in, gla_w_gk_up, gla_b_gk, gla_norm, ssd_conv_w, ssd_conv_b, ssd_dt_bias, ssd_A_log, ssd_D, ssd_norm, w_out, norm2, w_router, b_router, w_gate_up, b_gate_up, w_down, b_down, final_norm):
    assert w_mod.shape[0] == 1, "single-layer kernel"
    return _layer(x, c, ctx, c_ctx, w_mod[0], b_mod[0], norm1[0], w_in[0], gla_w_gk_up[0], gla_b_gk[0],
                  gla_norm[0], ssd_conv_w[0], ssd_conv_b[0], ssd_dt_bias[0], ssd_A_log[0], ssd_D[0],
                  ssd_norm[0], w_out[0], norm2[0], w_router[0], b_router[0], w_gate_up[0], b_gate_up[0],
                  w_down[0], b_down[0], final_norm)
```

```python
import functools

import numpy as np
import jax
import jax.numpy as jnp
from jax import lax
from jax.experimental import pallas as pl
from jax.experimental.pallas import tpu as pltpu

F32 = jnp.float32
BF16 = jnp.bfloat16

EPS = 1e-6
GRID_W = 64
GLA_HEADS = 4
GLA_DK = 64
GLA_DV = 128
GLA_QK = GLA_HEADS * GLA_DK
GLA_V = GLA_HEADS * GLA_DV
GLA_RANK = 16
GLA_GATE_NORM = 16.0
SSD_HEADDIM = 64
SSD_INNER = 512
SSD_HEADS = 8
SSD_GROUPS = 2
SSD_HPG = 4
SSD_STATE = 128
SSD_CONV_DIM = 1024
N_EXPERTS = 32
TOP_K = 4
D_FF = 1024
SWIGLU_LIMIT = 7.0
SWIGLU_ALPHA = 1.702
MOE_BLOCK = 512

TB = 256
GLA_C = 64
GLA_STAGE_GROUP = 4
SSD_C = 128
SSD_STAGE_GROUP = 1
LANES = 128
EXP_CLAMP = 80.0
DT_F = 16
DT_B = 24
NEG_BIG = -1e30
RUN = 8
RL = 1280
CD_LANES = 256
SPILL_ROWS = -(-RL // MOE_BLOCK) * MOE_BLOCK
D_HALF = 512
VMEM_LIMIT = 56 * 1024 * 1024


def _dot(a, b):
    return jnp.dot(a, b, preferred_element_type=F32)


def _dot_nt(a, b):
    return lax.dot_general(a, b, (((1,), (1,)), ((), ())), preferred_element_type=F32)


def _dot_tn(a, b):
    return lax.dot_general(a, b, (((0,), (0,)), ((), ())), preferred_element_type=F32)


def _split3(a):
    hi = a.astype(BF16)
    r1 = a - hi.astype(F32)
    mid = r1.astype(BF16)
    lo = (r1 - mid.astype(F32)).astype(BF16)
    return hi, mid, lo


def _dot_exact_r(m, a):
    hi, mid, lo = _split3(a)
    return _dot(m, hi) + _dot(m, mid) + _dot(m, lo)


def _dot_hilo_r(m, a):
    hi = a.astype(BF16)
    lo = (a - hi.astype(F32)).astype(BF16)
    return _dot(m, hi) + _dot(m, lo)


def _dot_hilo_l(a, m2):
    hi = a.astype(BF16)
    lo = (a - hi.astype(F32)).astype(BF16)
    return _dot(jnp.concatenate([hi, lo], axis=1), m2)


def _sigmoid(x):
    return 1.0 / (1.0 + jnp.exp(-x))


def _softplus(x):
    return jnp.maximum(x, 0.0) + jnp.log1p(jnp.exp(-jnp.abs(x)))


def _params(*sem):
    return pltpu.CompilerParams(dimension_semantics=sem, vmem_limit_bytes=VMEM_LIMIT)


def _mod_kernel(c_ref, w_ref, b_ref, o_ref):
    c = c_ref[...]
    s = c * _sigmoid(c)
    s_hi = s.astype(BF16)
    s_lo = (s - s_hi.astype(F32)).astype(BF16)
    w = w_ref[...]
    w_hi = w.astype(BF16)
    w_lo = (w - w_hi.astype(F32)).astype(BF16)
    o_ref[...] = _dot(s_hi, w_hi) + _dot(s_lo, w_hi) + _dot(s_hi, w_lo) + b_ref[...]


def _mod_call(cin, w, b):
    rows, d = cin.shape
    n = w.shape[1]
    tn = 1536
    return pl.pallas_call(
        _mod_kernel,
        grid=(n // tn,),
        in_specs=[pl.BlockSpec((rows, d), lambda i: (0, 0)),
                  pl.BlockSpec((d, tn), lambda i: (0, i)),
                  pl.BlockSpec((1, tn), lambda i: (0, i))],
        out_specs=pl.BlockSpec((rows, tn), lambda i: (0, i)),
        out_shape=jax.ShapeDtypeStruct((rows, n), F32),
        compiler_params=_params("arbitrary"),
    )(cin, w, b)


_C_Q, _C_K, _C_V, _C_G, _C_Z, _C_X, _C_M, _C_END = 0, 256, 512, 1024, 1536, 2048, 3072, 3200


def _inproj_kernel(xa_ref, xb_ref, ctx_ref, shl_ref, scl_ref, shc_ref, scc_ref, n1_ref, w_ref, wup_ref, bup_ref,
                   dtb_ref, q_ref, k_ref, v_ref, g_ref, z_ref, xbc_ref, ld_ref, misc_ref, h_scr):
    j = pl.program_id(1)

    def normmod(xv, sh_ref, sc_ref):
        ms = jnp.mean(xv * xv, axis=-1, keepdims=True)
        y = xv * lax.rsqrt(ms + EPS) * n1_ref[...]
        return (y * (1.0 + sc_ref[0]) + sh_ref[0]).astype(BF16)

    @pl.when(j == 0)
    def _():
        h_scr[0:TB, :] = normmod(ctx_ref[0], shc_ref, scc_ref)

    @pl.when(j > 0)
    def _():
        h_scr[0:TB, :] = normmod(xa_ref[0], shl_ref, scl_ref)

    h_scr[TB:2 * TB, :] = normmod(xb_ref[0], shl_ref, scl_ref)
    h = h_scr[...]

    def mm(lo, hi):
        return _dot(h, w_ref[:, lo:hi])

    m = mm(_C_M, _C_END)
    zz = _dot(m.astype(BF16), wup_ref[...]) + bup_ref[...]
    ld_ref[0] = -_softplus(-zz) * (1.0 / GLA_GATE_NORM)
    misc_ref[0] = _softplus(m + dtb_ref[...])
    q_ref[0] = (mm(_C_Q, _C_K) * (GLA_DK ** -0.5)).astype(BF16)
    k_ref[0] = mm(_C_K, _C_V).astype(BF16)
    v_ref[0] = mm(_C_V, _C_G).astype(BF16)
    g_ref[0] = mm(_C_G, _C_Z)
    z_ref[0] = mm(_C_Z, _C_X)
    xbc_ref[0] = mm(_C_X, _C_M)


def _inproj_call(x, ctx, sh1, sc1, n1, w_cat, wup, bup, dtb):
    bsz, L, d = x.shape
    nx = L // TB
    nj = -(-(nx + 1) // 2)
    tok = lambda n: pl.BlockSpec((1, 2 * TB, n), lambda b, j: (b, j, 0))
    const = lambda a: pl.BlockSpec(a.shape, lambda b, j: (0,) * a.ndim)
    xblk = lambda off: pl.BlockSpec((1, TB, d), lambda b, j: (b, jnp.clip(2 * j + off, 0, nx - 1), 0))
    mod_lat = pl.BlockSpec((1, 1, d), lambda b, j: (b, 0, 0))
    mod_ctx = pl.BlockSpec((1, 1, d), lambda b, j: (bsz, 0, 0))
    outs = [(GLA_QK, BF16), (GLA_QK, BF16), (GLA_V, BF16), (GLA_V, F32), (SSD_INNER, F32),
            (SSD_CONV_DIM, F32), (2 * GLA_QK, F32), (LANES, F32)]
    return pl.pallas_call(
        _inproj_kernel,
        grid=(bsz, nj),
        in_specs=[xblk(-1), xblk(0), pl.BlockSpec((1, TB, d), lambda b, j: (b, 0, 0)),
                  mod_lat, mod_lat, mod_ctx, mod_ctx,
                  const(n1), const(w_cat), const(wup), const(bup), const(dtb)],
        out_specs=[tok(n) for n, _ in outs],
        out_shape=[jax.ShapeDtypeStruct((bsz, nj * 2 * TB, n), dt) for n, dt in outs],
        scratch_shapes=[pltpu.VMEM((2 * TB, d), BF16)],
        compiler_params=_params("arbitrary", "arbitrary"),
    )(x, x, ctx, sh1, sc1, sh1, sc1, n1, w_cat, wup, bup, dtb)


_EXT_PAD = 8
_EXT_BASE = _EXT_PAD + GRID_W
_EXT_ROWS = 2 * _EXT_PAD + 2 * GRID_W + TB


def _conv_kernel(prev_ref, cur_ref, next_ref, w_ref, b_ref, xs_ref, bc_ref, ext):
    j = pl.program_id(1)
    nj = pl.num_programs(1)
    is_ctx = j == 0
    zpad = jnp.zeros((_EXT_PAD, SSD_CONV_DIM), F32)
    ext[0:_EXT_PAD, :] = zpad
    ext[_EXT_ROWS - _EXT_PAD:_EXT_ROWS, :] = zpad
    ext[_EXT_PAD:_EXT_BASE, :] = jnp.where(j >= 2, prev_ref[0], 0.0)
    ext[_EXT_BASE:_EXT_BASE + TB, :] = cur_ref[0]
    ext[_EXT_BASE + TB:_EXT_BASE + TB + GRID_W, :] = jnp.where(
        jnp.logical_and(j >= 1, j <= nj - 2), next_ref[0], 0.0)

    t = lax.broadcasted_iota(jnp.int32, (TB, LANES), 0)
    col = t & (GRID_W - 1)
    ok_l = jnp.where(is_ctx, t, col) >= 1
    ok_r = jnp.where(is_ctx, t - (TB - GRID_W), col) <= GRID_W - 2
    lat = jnp.where(is_ctx, 0.0, 1.0)

    for c in range(SSD_CONV_DIM // LANES):
        lo, hi = c * LANES, (c + 1) * LANES
        acc = jnp.zeros((TB, LANES), F32)
        for dr in (-1, 0, 1):
            for dc in (-1, 0, 1):
                start = _EXT_BASE + GRID_W * dr + dc
                tap = ext[start:start + TB, lo:hi]
                wi = 3 * (dr + 1) + (dc + 1)
                wv = w_ref[wi:wi + 1, lo:hi]
                if dr != 0:
                    wv = wv * lat
                if dc == -1:
                    tap = jnp.where(ok_l, tap, 0.0)
                elif dc == 1:
                    tap = jnp.where(ok_r, tap, 0.0)
                acc = acc + tap * wv
        y = acc + b_ref[:, lo:hi]
        y = y * _sigmoid(y)
        if c < SSD_INNER // LANES:
            xs_ref[0, :, lo:hi] = y
        else:
            bc_ref[0, :, lo - SSD_INNER:hi - SSD_INNER] = y.astype(BF16)


def _conv_call(xbc, w9, bias, ls):
    bsz, _, ch = xbc.shape
    nj = ls // TB
    rpb = TB // GRID_W
    nrow = ls // GRID_W
    return pl.pallas_call(
        _conv_kernel,
        grid=(bsz, nj),
        in_specs=[pl.BlockSpec((1, GRID_W, ch), lambda b, j: (b, jnp.maximum(rpb * j - 1, 0), 0)),
                  pl.BlockSpec((1, TB, ch), lambda b, j: (b, j, 0)),
                  pl.BlockSpec((1, GRID_W, ch), lambda b, j: (b, jnp.minimum(rpb * j + rpb, nrow - 1), 0)),
                  pl.BlockSpec((9, ch), lambda b, j: (0, 0)),
                  pl.BlockSpec((1, ch), lambda b, j: (0, 0))],
        out_specs=[pl.BlockSpec((1, TB, SSD_INNER), lambda b, j: (b, j, 0)),
                   pl.BlockSpec((1, TB, ch - SSD_INNER), lambda b, j: (b, j, 0))],
        out_shape=[jax.ShapeDtypeStruct((bsz, ls, SSD_INNER), F32),
                   jax.ShapeDtypeStruct((bsz, ls, ch - SSD_INNER), BF16)],
        scratch_shapes=[pltpu.VMEM((_EXT_ROWS, ch), F32)],
        compiler_params=_params("arbitrary", "arbitrary"),
    )(xbc, xbc, xbc, w9, bias)


def _fwd_blk(s):
    return s


def _bwd_blk(s, ns):
    return jnp.where(s == 0, 0, ns - s)


def _scan_batch(bsz):
    return 2 if bsz % 2 == 0 else 1


class _GlaChunk:
    def __init__(self, q, k, v, la, st_ref, tri_m, fwd, store):
        self.q, self.k, self.v, self.la, self.st_ref, self.tri_m, self.fwd, self.store = (
            q, k, v, la, st_ref, tri_m, fwd, store)

    def stage_sums(self):
        self.b = _dot_hilo_r(self.tri_m, self.la)

    def stage_factors(self):
        C = GLA_C
        b = self.b
        self.bt = b[C - 1:C, :] if self.fwd else b[0:1, :]
        r = 0.5 * self.bt
        self.er = jnp.exp(r)
        self.qt = (self.q.astype(F32) * jnp.exp(jnp.minimum(b - r, EXP_CLAMP))).astype(BF16)
        kt = (self.k.astype(F32) * jnp.exp(jnp.minimum(r - b, EXP_CLAMP))).astype(BF16)
        head_k = lax.broadcasted_iota(jnp.int32, (C, GLA_QK), 1) >> 6
        zero = jnp.zeros_like(kt)
        self.kh = [jnp.where(head_k == h, kt, zero) for h in range(GLA_HEADS)]
        self.qh = [jnp.where(head_k == h, self.qt, zero) for h in range(GLA_HEADS)]

    def stage_products(self):
        v = self.v
        kcat = jnp.concatenate(self.kh, axis=0)
        self.sc = _dot_nt(self.qt, kcat)
        vcat = jnp.concatenate([v[:, h * GLA_DV:(h + 1) * GLA_DV] for h in range(GLA_HEADS)], axis=0)
        self.u = _dot_tn(vcat, kcat) * self.er

    def stage_mask(self):
        C = GLA_C
        ii = lax.broadcasted_iota(jnp.int32, (C, GLA_HEADS * C), 0)
        jj = lax.broadcasted_iota(jnp.int32, (C, GLA_HEADS * C), 1) & (C - 1)
        causal = (jj <= ii) if self.fwd else (jj >= ii)
        self.p = jnp.where(causal, self.sc, 0.0).astype(BF16)
        v = self.v
        head_v = lax.broadcasted_iota(jnp.int32, (C, GLA_V), 1) >> 7
        self.vst = jnp.concatenate([jnp.where(head_v == h, v, jnp.zeros_like(v)) for h in range(GLA_HEADS)], axis=0)

    def stage_intra(self):
        self.o = _dot(self.p, self.vst)

    def stage_inter(self):
        st = self.st_ref[...]
        ster = (st * self.er).astype(BF16)
        self.st_ref[...] = st * jnp.exp(self.bt) + self.u
        res = _dot_nt(jnp.concatenate(self.qh, axis=0), ster)
        self.inter = jnp.concatenate([res[h * GLA_C:(h + 1) * GLA_C, :] for h in range(GLA_HEADS)], axis=1)

    def stage_out(self):
        self.store(self.o + self.inter)


def _gla_kernel(qf_ref, kf_ref, vf_ref, lf_ref, qb_ref, kb_ref, vb_ref, lb_ref, trif_ref, trib_ref,
                of_ref, ob_ref, stf, stb):
    s = pl.program_id(1)

    @pl.when(s == 0)
    def _():
        stf[...] = jnp.zeros_like(stf)
        stb[...] = jnp.zeros_like(stb)

    nsub = TB // GLA_C

    def store_to(ref, bb, sl):
        def store(val):
            ref[bb, sl, :] = val
        return store

    group = GLA_STAGE_GROUP
    for g0 in range(0, nsub, group):
        steps = []
        for i in range(g0, g0 + group):
            sf = pl.ds(i * GLA_C, GLA_C)
            sb = pl.ds((nsub - 1 - i) * GLA_C, GLA_C)
            chunks = []
            for bb in range(qf_ref.shape[0]):
                chunks.append(_GlaChunk(qf_ref[bb, sf, :], kf_ref[bb, sf, :], vf_ref[bb, sf, :], lf_ref[bb, sf, :],
                                        stf.at[bb], trif_ref[...], True, store_to(of_ref, bb, sf)))
                chunks.append(_GlaChunk(qb_ref[bb, sb, :], kb_ref[bb, sb, :], vb_ref[bb, sb, :], lb_ref[bb, sb, :],
                                        stb.at[bb], trib_ref[...], False, store_to(ob_ref, bb, sb)))
            steps.append(chunks)
        for stage in ("stage_sums", "stage_factors", "stage_products", "stage_mask", "stage_intra"):
            for chunks in steps:
                for ch in chunks:
                    getattr(ch, stage)()
        for chunks in steps:
            for stage in ("stage_inter", "stage_out"):
                for ch in chunks:
                    getattr(ch, stage)()


def _gla_call(q, k, v, ld, L):
    bsz = q.shape[0]
    nx = L // TB
    ns = nx + 1
    trif = jnp.asarray(np.tril(np.ones((GLA_C, GLA_C), np.float32)), BF16)
    trib = jnp.asarray(np.triu(np.ones((GLA_C, GLA_C), np.float32)), BF16)
    nb = _scan_batch(bsz)
    f = lambda n, lane=0: pl.BlockSpec((nb, TB, n), lambda b, s: (b, _fwd_blk(s), lane))
    r = lambda n, lane=0: pl.BlockSpec((nb, TB, n), lambda b, s: (b, _bwd_blk(s, ns), lane))
    tri = pl.BlockSpec((GLA_C, GLA_C), lambda b, s: (0, 0))
    return pl.pallas_call(
        _gla_kernel,
        grid=(bsz // nb, ns),
        in_specs=[f(GLA_QK), f(GLA_QK), f(GLA_V), f(GLA_QK, 0),
                  r(GLA_QK), r(GLA_QK), r(GLA_V), r(GLA_QK, 1), tri, tri],
        out_specs=[pl.BlockSpec((nb, TB, GLA_V), lambda b, s: (b, jnp.maximum(s - 1, 0), 0)),
                   pl.BlockSpec((nb, TB, GLA_V), lambda b, s: (b, jnp.where(s == 0, nx - 1, nx - s), 0))],
        out_shape=[jax.ShapeDtypeStruct((bsz, L, GLA_V), F32)] * 2,
        scratch_shapes=[pltpu.VMEM((nb, GLA_DV, GLA_QK), F32)] * 2,
        compiler_params=_params("arbitrary", "arbitrary"),
    )(q, k, v, ld, q, k, v, ld, trif, trib)


class _SsdChunk:
    def __init__(self, xs, bc, dtm, avec, dvec, st_ref, tri_m, e_m, base, fwd, store):
        self.xs, self.bc, self.dtm, self.avec, self.dvec, self.st_ref = xs, bc, dtm, avec, dvec, st_ref
        self.tri_m, self.e_m, self.base, self.fwd, self.store = tri_m, e_m, base, fwd, store

    def stage_sums(self):
        self.dt_exp = _dot_hilo_l(self.dtm, self.e_m)
        self.acum = _dot_exact_r(self.tri_m, self.dtm * self.avec)

    def stage_expand(self):
        self.acum_exp = _dot_hilo_l(self.acum, self.e_m)
        self.acum_t = self.acum.T
        self.xdt = self.xs * self.dt_exp
        bc = self.bc
        self.bg = [bc[:, 128 * g:128 * (g + 1)] for g in range(SSD_GROUPS)]
        self.cg = [bc[:, 256 + 128 * g:256 + 128 * (g + 1)] for g in range(SSD_GROUPS)]
        self.cb = [_dot_nt(self.cg[g], self.bg[g]) for g in range(SSD_GROUPS)]

    def stage_decay(self):
        C = SSD_C
        ii = lax.broadcasted_iota(jnp.int32, (C, C), 0)
        jj = lax.broadcasted_iota(jnp.int32, (C, C), 1)
        tri = (jj <= ii) if self.fwd else (jj >= ii)
        self.ms = []
        for g in range(SSD_GROUPS):
            for rr in range(SSD_HPG):
                ln = self.base + SSD_HPG * g + rr
                diff = self.acum[:, ln:ln + 1] - self.acum_t[ln:ln + 1, :]
                seg = jnp.where(tri, jnp.exp(jnp.minimum(diff, 0.0)), 0.0)
                self.ms.append((self.cb[g] * seg).astype(BF16))
        ae = self.acum_exp
        self.al_exp = ae[C - 1:C, :] if self.fwd else ae[0:1, :]
        self.xw = (self.xdt * jnp.exp(self.al_exp - ae)).astype(BF16)
        xdt_b = self.xdt.astype(BF16)
        head = lax.broadcasted_iota(jnp.int32, (C, SSD_INNER), 1) >> 6
        zero = jnp.zeros_like(xdt_b)
        self.xh = [jnp.where(head == h, xdt_b, zero) for h in range(SSD_HEADS)]

    def stage_products(self):
        self.yg, self.ug = [], []
        for g in range(SSD_GROUPS):
            gl, gh = 256 * g, 256 * (g + 1)
            yg = _dot(self.ms[SSD_HPG * g], self.xh[SSD_HPG * g][:, gl:gh])
            for rr in range(1, SSD_HPG):
                yg = yg + _dot(self.ms[SSD_HPG * g + rr], self.xh[SSD_HPG * g + rr][:, gl:gh])
            self.yg.append(yg)
            self.ug.append(_dot_tn(self.bg[g], self.xw[:, gl:gh]))

    def stage_state(self):
        ys = []
        for g in range(SSD_GROUPS):
            gl, gh = 256 * g, 256 * (g + 1)
            sg = self.st_ref[g]
            yoff = _dot(self.cg[g], sg.astype(BF16)) * jnp.exp(self.acum_exp[:, gl:gh])
            self.st_ref[g] = sg * jnp.exp(self.al_exp[:, gl:gh]) + self.ug[g]
            ys.append(self.yg[g] + yoff)
        y = jnp.concatenate(ys, axis=1)
        if self.dvec is not None:
            y = y + self.dvec * self.xs
        self.store(y)


def _ssd_kernel(xf_ref, bcf_ref, mf_ref, xb_ref, bcb_ref, mb_ref, af_ref, ab_ref, d_ref,
                trif_ref, trib_ref, ef_ref, eb_ref, yf_ref, yb_ref, stf, stb):
    s = pl.program_id(1)

    @pl.when(s == 0)
    def _():
        stf[...] = jnp.zeros_like(stf)
        stb[...] = jnp.zeros_like(stb)

    nsub = TB // SSD_C

    def store_to(ref, bb, sl):
        def store(val):
            ref[bb, sl, :] = val
        return store

    steps = []
    for i in range(nsub):
        sf = pl.ds(i * SSD_C, SSD_C)
        sb = pl.ds((nsub - 1 - i) * SSD_C, SSD_C)
        chunks = []
        for bb in range(xf_ref.shape[0]):
            chunks.append(_SsdChunk(xf_ref[bb, sf, :], bcf_ref[bb, sf, :], mf_ref[bb, sf, :], af_ref[...],
                                    d_ref[...], stf.at[bb], trif_ref[...], ef_ref[...], DT_F, True,
                                    store_to(yf_ref, bb, sf)))
            chunks.append(_SsdChunk(xb_ref[bb, sb, :], bcb_ref[bb, sb, :], mb_ref[bb, sb, :], ab_ref[...],
                                    None, stb.at[bb], trib_ref[...], eb_ref[...], DT_B, False,
                                    store_to(yb_ref, bb, sb)))
        steps.append(chunks)
    for g0 in range(0, nsub, SSD_STAGE_GROUP):
        group = steps[g0:g0 + SSD_STAGE_GROUP]
        for stage in ("stage_sums", "stage_expand", "stage_decay", "stage_products"):
            for chunks in group:
                for ch in chunks:
                    getattr(ch, stage)()
        for chunks in group:
            for ch in chunks:
                ch.stage_state()


def _expand_matrix(base):
    e = np.zeros((LANES, SSD_INNER), np.float32)
    for h in range(SSD_HEADS):
        e[base + h, SSD_HEADDIM * h:SSD_HEADDIM * (h + 1)] = 1.0
    return jnp.asarray(np.concatenate([e, e], axis=0), BF16)


def _ssd_call(xs, bc, misc, a_f, a_b, dvec, L):
    bsz, ls, _ = xs.shape
    ns = ls // TB
    nx = L // TB
    trif = jnp.asarray(np.tril(np.ones((SSD_C, SSD_C), np.float32)), BF16)
    trib = jnp.asarray(np.triu(np.ones((SSD_C, SSD_C), np.float32)), BF16)
    ef, eb = _expand_matrix(DT_F), _expand_matrix(DT_B)
    nb = _scan_batch(bsz)
    f = lambda n: pl.BlockSpec((nb, TB, n), lambda b, s: (b, _fwd_blk(s), 0))
    r = lambda n: pl.BlockSpec((nb, TB, n), lambda b, s: (b, _bwd_blk(s, ns), 0))
    const = lambda a: pl.BlockSpec(a.shape, lambda b, s: (0,) * a.ndim)
    return pl.pallas_call(
        _ssd_kernel,
        grid=(bsz // nb, ns),
        in_specs=[f(SSD_INNER), f(512), f(LANES), r(SSD_INNER), r(512), r(LANES),
                  const(a_f), const(a_b), const(dvec), const(trif), const(trib), const(ef), const(eb)],
        out_specs=[pl.BlockSpec((nb, TB, SSD_INNER), lambda b, s: (b, jnp.maximum(s - 1, 0), 0)),
                   pl.BlockSpec((nb, TB, SSD_INNER), lambda b, s: (b, jnp.where(s == 0, nx - 1, nx - s), 0))],
        out_shape=[jax.ShapeDtypeStruct((bsz, L, SSD_INNER), F32)] * 2,
        scratch_shapes=[pltpu.VMEM((nb, SSD_GROUPS, SSD_STATE, SSD_HPG * SSD_HEADDIM), F32)] * 2,
        compiler_params=_params("arbitrary", "arbitrary"),
    )(xs, bc, misc, xs, bc, misc, a_f, a_b, dvec, trif, trib, ef, eb)


def _outproj_kernel(of_ref, ob_ref, ga_ref, gb_ref, yf_ref, yb_ref, za_ref, zb_ref, x_ref, g1_ref, sh2_ref, sc2_ref,
                    gn_ref, sn_ref, wo_ref, n2_ref, wrh_ref, wrl_ref, br_ref,
                    x1_ref, h2_ref, eidx_ref, gate_ref, cnt_ref, mix_s, hl_s):
    first = jnp.logical_and(pl.program_id(0) == 0, pl.program_id(1) == 0)

    @pl.when(first)
    def _():
        cnt_ref[...] = jnp.zeros_like(cnt_ref)

    halves = [(pl.ds(0, TB), ga_ref, za_ref), (pl.ds(TB, TB), gb_ref, zb_ref)]

    for rows, g_ref, z_ref in halves:
        o = of_ref[0, rows, :] + ob_ref[0, rows, :]
        gg = g_ref[0]
        for h in range(GLA_HEADS):
            lo, hi = GLA_DV * h, GLA_DV * (h + 1)
            oh = o[:, lo:hi]
            ms = jnp.mean(oh * oh, axis=-1, keepdims=True)
            gh = gg[:, lo:hi]
            mix_s[rows, lo:hi] = (oh * lax.rsqrt(ms + EPS) * gn_ref[:, lo:hi] * (gh * _sigmoid(gh))).astype(BF16)
        zz = z_ref[0]
        u = (yf_ref[0, rows, :] + yb_ref[0, rows, :]) * (zz * _sigmoid(zz))
        gw = SSD_INNER // SSD_GROUPS
        for g in range(SSD_GROUPS):
            lo, hi = gw * g, gw * (g + 1)
            ug = u[:, lo:hi]
            ms = jnp.mean(ug * ug, axis=-1, keepdims=True)
            mix_s[rows, GLA_V + lo:GLA_V + hi] = (ug * lax.rsqrt(ms + EPS) * sn_ref[:, lo:hi]).astype(BF16)

    x1_ref[0] = x_ref[0] + g1_ref[0] * _dot(mix_s[...], wo_ref[...])

    for rows, _, _ in halves:
        x1 = x1_ref[0, rows, :]
        ms = jnp.mean(x1 * x1, axis=-1, keepdims=True)
        h2 = (x1 * lax.rsqrt(ms + EPS) * n2_ref[...]) * (1.0 + sc2_ref[0]) + sh2_ref[0]
        h_hi = h2.astype(BF16)
        h2_ref[0, rows, :] = h_hi
        hl_s[rows, :] = (h2 - h_hi.astype(F32)).astype(BF16)

    h_hi = h2_ref[0]
    logits = (_dot(h_hi, wrh_ref[...]) + _dot(hl_s[...], wrh_ref[...]) + _dot(h_hi, wrl_ref[...])) + br_ref[...]

    lane = lax.broadcasted_iota(jnp.int32, (TB, LANES), 1).astype(F32)
    st = [dict(work=logits[i * TB:(i + 1) * TB, :], eidx=jnp.full((TB, LANES), -1.0, F32),
               gates=jnp.zeros((TB, LANES), F32), sel=jnp.zeros((TB, LANES), F32), m0=None,
               den=jnp.zeros((TB, 1), F32)) for i in range(2)]
    for kk in range(TOP_K):
        for t in st:
            t["m"] = jnp.max(t["work"], axis=-1, keepdims=True)
        for t in st:
            t["idx"] = jnp.min(jnp.where(t["work"] == t["m"], lane, float(LANES)), axis=-1, keepdims=True)
        for t in st:
            hit = lane == t["idx"]
            if t["m0"] is None:
                t["m0"] = t["m"]
            e = jnp.exp(t["m"] - t["m0"])
            t["den"] = t["den"] + e
            t["eidx"] = jnp.where(lane == float(kk), t["idx"], t["eidx"])
            t["gates"] = jnp.where(lane == float(kk), e, t["gates"])
            t["sel"] = jnp.where(hit, 1.0, t["sel"])
            t["work"] = jnp.where(hit, NEG_BIG, t["work"])
    for (rows, _, _), t in zip(halves, st):
        eidx_ref[0, rows, :] = t["eidx"].astype(jnp.int32)
        gate_ref[0, rows, :] = t["gates"] / t["den"]
        cnt = jnp.sum(t["sel"], axis=0, keepdims=True)
        cnt_ref[...] += jnp.floor((cnt + (RUN - 1.0)) * (1.0 / RUN)) * RUN


def _outproj_call(o_f, o_b, g_all, y_f, y_b, z_all, x, g1, sh2, sc2, gn, sn, wo, n2, wr_hi, wr_lo, br):
    bsz, L, d = x.shape
    nj = L // (2 * TB)
    tok = lambda n: pl.BlockSpec((1, 2 * TB, n), lambda b, j: (b, j, 0))
    tok_off = lambda n, half: pl.BlockSpec((1, TB, n), lambda b, j: (b, 2 * j + 1 + half, 0))
    const = lambda a: pl.BlockSpec(a.shape, lambda b, j: (0,) * a.ndim)
    mod = pl.BlockSpec((1, 1, d), lambda b, j: (b, 0, 0))
    return pl.pallas_call(
        _outproj_kernel,
        grid=(bsz, nj),
        in_specs=[tok(GLA_V), tok(GLA_V), tok_off(GLA_V, 0), tok_off(GLA_V, 1),
                  tok(SSD_INNER), tok(SSD_INNER), tok_off(SSD_INNER, 0), tok_off(SSD_INNER, 1),
                  tok(d), mod, mod, mod, const(gn), const(sn), const(wo), const(n2),
                  const(wr_hi), const(wr_lo), const(br)],
        out_specs=[tok(d), tok(d), tok(LANES), tok(LANES), pl.BlockSpec((1, LANES), lambda b, j: (0, 0))],
        out_shape=[jax.ShapeDtypeStruct((bsz, L, d), F32), jax.ShapeDtypeStruct((bsz, L, d), BF16),
                   jax.ShapeDtypeStruct((bsz, L, LANES), jnp.int32), jax.ShapeDtypeStruct((bsz, L, LANES), F32),
                   jax.ShapeDtypeStruct((1, LANES), F32)],
        scratch_shapes=[pltpu.VMEM((2 * TB, d), BF16), pltpu.VMEM((2 * TB, d), BF16)],
        compiler_params=_params("arbitrary", "arbitrary"),
    )(o_f, o_b, g_all, g_all, y_f, y_b, z_all, z_all, x, g1, sh2, sc2, gn, sn, wo, n2, wr_hi, wr_lo, br)


def _pos_kernel(eidx_ref, pstart_ref, lst_ref, ust_ref, lp_ref, cd_ref, carry):
    @pl.when(pl.program_id(0) == 0)
    def _():
        carry[...] = pstart_ref[...]

    eidx = eidx_ref[...]
    lane = lax.broadcasted_iota(jnp.int32, (TB, LANES), 1)
    hits = [lane == eidx[:, kk:kk + 1] for kk in range(TOP_K)]
    sel = jnp.zeros((TB, LANES), F32)
    for hmask in hits:
        sel = jnp.where(hmask, 1.0, sel)
    cnt = jnp.sum(sel, axis=0, keepdims=True)
    run = jnp.floor((cnt + (RUN - 1.0)) * (1.0 / RUN)) * RUN
    rank = _dot(lst_ref[...], sel.astype(BF16))
    loff = _dot(jnp.broadcast_to(run, (8, LANES)).astype(BF16), ust_ref[...])[0:1]
    pos = loff + rank
    lp = jnp.zeros((TB, LANES), jnp.int32)
    for kk, hmask in enumerate(hits):
        lk = jnp.sum(jnp.where(hmask, pos, 0.0), axis=-1, keepdims=True)
        lp = jnp.where(lane == kk, lk.astype(jnp.int32), lp)
    lp_ref[...] = lp

    base = carry[...]
    eye = lax.broadcasted_iota(jnp.int32, (LANES, LANES), 0) == lax.broadcasted_iota(jnp.int32, (LANES, LANES), 1)
    col = lambda v: jnp.sum(jnp.where(eye, v, 0.0), axis=1, keepdims=True)
    loff_c, run_c, shift_c = col(loff), col(run), col(base - loff)
    row0 = (lax.broadcasted_iota(jnp.int32, (LANES, CD_LANES), 1) * RUN).astype(F32)
    inside = jnp.where(row0 >= loff_c, 1.0, 0.0) * jnp.where(row0 < loff_c + run_c, 1.0, 0.0)
    valid = jnp.sum(inside, axis=0, keepdims=True)
    dest = jnp.sum(inside * shift_c, axis=0, keepdims=True) + row0[0:1]
    cd_ref[0] = jnp.where(valid > 0.0, dest, -1.0).astype(jnp.int32)
    carry[...] = base + run


def _pos_call(eidx, pstart):
    T = eidx.shape[0]
    lst = jnp.asarray(np.tril(np.ones((TB, TB), np.float32), -1), BF16)
    ust = jnp.asarray(np.triu(np.ones((LANES, LANES), np.float32), 1), BF16)
    return pl.pallas_call(
        _pos_kernel,
        grid=(T // TB,),
        in_specs=[pl.BlockSpec((TB, LANES), lambda i: (i, 0)),
                  pl.BlockSpec((1, LANES), lambda i: (0, 0)),
                  pl.BlockSpec((TB, TB), lambda i: (0, 0)),
                  pl.BlockSpec((LANES, LANES), lambda i: (0, 0))],
        out_specs=[pl.BlockSpec((TB, LANES), lambda i: (i, 0)),
                   pl.BlockSpec((1, 1, CD_LANES), lambda i: (i, 0, 0))],
        out_shape=[jax.ShapeDtypeStruct((T, LANES), jnp.int32),
                   jax.ShapeDtypeStruct((T // TB, 1, CD_LANES), jnp.int32)],
        scratch_shapes=[pltpu.VMEM((1, LANES), F32)],
        compiler_params=_params("arbitrary"),
    )(eidx, pstart, lst, ust)


def _pair_matrix(lp, weights):
    j = lax.broadcasted_iota(jnp.int32, (TB, RL), 1)
    m = jnp.zeros((TB, RL), F32)
    for kk in range(TOP_K):
        w = 1.0 if weights is None else weights[:, kk:kk + 1]
        m = m + jnp.where(j == lp[:, kk:kk + 1], w, 0.0)
    return m


def _pack_bf16_pairs(v, is_bf16_valued=False):
    if not is_bf16_valued:
        v = v.astype(BF16).astype(F32)
    bits = pltpu.bitcast(v, jnp.uint32)
    return bits[:, D_HALF:] | (bits[:, :D_HALF] >> 16)


def _unpack_bf16_pairs(w):
    lo = pltpu.bitcast(w << 16, F32).astype(BF16)
    hi = pltpu.bitcast(w & jnp.uint32(0xFFFF0000), F32).astype(BF16)
    return lo, hi


N_SLOTS = 3


def _dispatch_kernel(fill_ref, cdp_ref, cdc_ref, h_ref, lp_ref, xg_ref, sorted_s, sems, fill_sem):
    i = pl.program_id(0)
    n = pl.num_programs(0)
    slot = lax.rem(i, N_SLOTS)
    prev = lax.rem(i + N_SLOTS - 1, N_SLOTS)
    pprev = lax.rem(i + N_SLOTS - 2, N_SLOTS)
    spill0 = xg_ref.shape[0] - N_SLOTS * RL

    def start(cref, s, c, to_spill, sem):
        src = RUN * c if isinstance(c, int) else pl.multiple_of(RUN * c, RUN)
        d = cref[0, 0, c]
        d = pl.multiple_of(jnp.where(jnp.logical_or(d < 0, to_spill), spill0 + s * RL + src, d), RUN)
        pltpu.make_async_copy(sorted_s.at[s, pl.ds(src, RUN), :], xg_ref.at[pl.ds(d, RUN), :], sem).start()

    def drain(s, sem):
        pltpu.make_async_copy(sorted_s.at[s], xg_ref.at[pl.ds(0, RL), :], sem).wait()

    @pl.when(i == 0)
    def _():
        sorted_s[...] = jnp.zeros_like(sorted_s)
        for s in range(N_SLOTS):
            lax.fori_loop(0, RL // RUN, lambda c, carry, s=s: (start(cdp_ref, s, c, True, fill_sem), carry)[1], 0)
        for s in range(N_SLOTS):
            drain(s, fill_sem)

    for c in range(RL // RUN):
        start(cdp_ref, prev, c, i == 0, sems.at[prev])

    sorted_s[slot] = _pack_bf16_pairs(_dot_tn(_pair_matrix(lp_ref[...], None).astype(BF16), h_ref[...]),
                                      is_bf16_valued=True)

    @pl.when(i == 0)
    def _():
        def fill(k):
            blk = pl.multiple_of(fill_ref[k] * MOE_BLOCK, MOE_BLOCK)
            return pltpu.make_async_copy(sorted_s.at[0, pl.ds(0, MOE_BLOCK), :],
                                         xg_ref.at[pl.ds(blk, MOE_BLOCK), :], fill_sem)

        for k in range(fill_ref.shape[0]):
            pl.when(fill_ref[k] >= 0)(lambda k=k: fill(k).start())
        for k in range(fill_ref.shape[0]):
            pl.when(fill_ref[k] >= 0)(lambda k=k: fill(k).wait())

    pl.when(i >= 1)(lambda: drain(pprev, sems.at[pprev]))

    @pl.when(i == n - 1)
    def _():
        lax.fori_loop(0, RL // RUN, lambda c, carry: (start(cdc_ref, slot, c, False, sems.at[slot]), carry)[1], 0,
                      unroll=4)
        drain(prev, sems.at[prev])
        drain(slot, sems.at[slot])


def _dispatch_call(fill_blocks, cd, h2, lp, P):
    T, d = h2.shape
    grid_spec = pltpu.PrefetchScalarGridSpec(
        num_scalar_prefetch=1,
        grid=(T // TB,),
        in_specs=[pl.BlockSpec((1, 1, CD_LANES), lambda i, fb: (jnp.maximum(i - 1, 0), 0, 0),
                               memory_space=pltpu.SMEM),
                  pl.BlockSpec((1, 1, CD_LANES), lambda i, fb: (i, 0, 0), memory_space=pltpu.SMEM),
                  pl.BlockSpec((TB, d), lambda i, fb: (i, 0)),
                  pl.BlockSpec((TB, LANES), lambda i, fb: (i, 0))],
        out_specs=pl.BlockSpec(memory_space=pl.ANY),
        scratch_shapes=[pltpu.VMEM((N_SLOTS, RL, D_HALF), jnp.uint32), pltpu.SemaphoreType.DMA((N_SLOTS,)),
                        pltpu.SemaphoreType.DMA(())],
    )
    return pl.pallas_call(
        _dispatch_kernel,
        grid_spec=grid_spec,
        out_shape=jax.ShapeDtypeStruct((P + N_SLOTS * RL, D_HALF), jnp.uint32),
        compiler_params=_params("arbitrary"),
    )(fill_blocks, cd, cd, h2, lp)


def _moe_kernel(nb, info_ref, x_hbm, wgu_hbm, bgu_ref, wd_hbm, bd_ref, y_hbm,
                wgu_f, wd_f, wgu_b, wd_b, act_s, xbuf, ybuf, zbuf, wsems, xsems, ysems, zsem):
    n_used = info_ref[0, nb]

    def rows(blk):
        return pl.ds(pl.multiple_of(blk * MOE_BLOCK, MOE_BLOCK), MOE_BLOCK)

    def x_copy(blk, slot):
        return pltpu.make_async_copy(x_hbm.at[rows(blk), :], xbuf.at[slot], xsems.at[slot])

    def y_copy(blk, slot):
        return pltpu.make_async_copy(ybuf.at[slot], y_hbm.at[rows(blk), :], ysems.at[slot])

    def z_copy(blk):
        return pltpu.make_async_copy(zbuf, y_hbm.at[rows(blk), :], zsem)

    def fetch(expert, slot):
        return (pltpu.make_async_copy(wgu_hbm.at[expert], wgu_f.at[slot], wsems.at[0, slot]),
                pltpu.make_async_copy(wd_hbm.at[expert], wd_f.at[slot], wsems.at[1, slot]))

    zbuf[...] = jnp.zeros_like(zbuf)
    lax.fori_loop(n_used, nb, lambda blk, c: (z_copy(blk).start(), c)[1], 0)

    for cp in fetch(info_ref[0, 0], 0):
        cp.start()
    x_copy(0, 0).start()

    def body(i, carry):
        slot = i & 1
        e = info_ref[0, i]

        @pl.when(info_ref[1, i] == 1)
        def _():
            wslot = info_ref[2, i]
            nxt = info_ref[3, i]
            cps = fetch(e, wslot)
            cps[0].wait()
            wgu_b[...] = wgu_f[wslot].astype(BF16)
            cps[1].wait()
            wd_b[...] = wd_f[wslot].astype(BF16)

            @pl.when(nxt >= 0)
            def _():
                for cp in fetch(nxt, 1 - wslot):
                    cp.start()

        x_copy(i, slot).wait()
        pl.when(i + 1 < n_used)(lambda: x_copy(i + 1, 1 - slot).start())
        pl.when(i >= 2)(lambda: y_copy(i - 2, slot).wait())

        bgu = bgu_ref[pl.ds(e, 1), :]
        xb = jnp.concatenate(_unpack_bf16_pairs(xbuf[slot]), axis=1)
        cw = 256
        for c in range(D_FF // cw):
            lo, hi = c * cw, (c + 1) * cw
            gate = _dot(xb, wgu_b[:, lo:hi]) + bgu[:, lo:hi]
            up = _dot(xb, wgu_b[:, D_FF + lo:D_FF + hi]) + bgu[:, D_FF + lo:D_FF + hi]
            gate = jnp.minimum(gate, SWIGLU_LIMIT)
            up = jnp.clip(up, -SWIGLU_LIMIT, SWIGLU_LIMIT)
            act_s[:, lo:hi] = ((up + 1.0) * (gate * _sigmoid(SWIGLU_ALPHA * gate))).astype(BF16)
        ybuf[slot] = _pack_bf16_pairs(_dot(act_s[...], wd_b[...]) + bd_ref[pl.ds(e, 1), :])
        y_copy(i, slot).start()
        return carry

    lax.fori_loop(0, n_used, body, 0)

    pl.when(n_used >= 2)(lambda: y_copy(n_used - 2, n_used & 1).wait())
    y_copy(n_used - 1, (n_used - 1) & 1).wait()
    lax.fori_loop(n_used, nb, lambda blk, c: (z_copy(blk).wait(), c)[1], 0)


def _moe_call(info, xg, wgu, bgu, wd, bd, nb):
    P = nb * MOE_BLOCK
    ne, d, f2 = wgu.shape
    blk_buf = pltpu.VMEM((2, MOE_BLOCK, D_HALF), jnp.uint32)
    grid_spec = pltpu.PrefetchScalarGridSpec(
        num_scalar_prefetch=1,
        grid=(1,),
        in_specs=[pl.BlockSpec(memory_space=pl.ANY),
                  pl.BlockSpec(memory_space=pl.ANY),
                  pl.BlockSpec((ne, f2), lambda i, info: (0, 0)),
                  pl.BlockSpec(memory_space=pl.ANY),
                  pl.BlockSpec((ne, d), lambda i, info: (0, 0))],
        out_specs=pl.BlockSpec(memory_space=pl.ANY),
        scratch_shapes=[pltpu.VMEM((2, d, f2), F32), pltpu.VMEM((2, f2 // 2, d), F32),
                        pltpu.VMEM((d, f2), BF16), pltpu.VMEM((f2 // 2, d), BF16),
                        pltpu.VMEM((MOE_BLOCK, f2 // 2), BF16), blk_buf, blk_buf,
                        pltpu.VMEM((MOE_BLOCK, D_HALF), jnp.uint32),
                        pltpu.SemaphoreType.DMA((2, 2)), pltpu.SemaphoreType.DMA((2,)),
                        pltpu.SemaphoreType.DMA((2,)), pltpu.SemaphoreType.DMA(())],
    )
    return pl.pallas_call(
        functools.partial(_moe_kernel, nb),
        grid_spec=grid_spec,
        out_shape=jax.ShapeDtypeStruct((P, D_HALF), jnp.uint32),
        compiler_params=_params("arbitrary"),
    )(info, xg, wgu, bgu, wd, bd)


def _combine_kernel(d0_ref, d1_ref, d2_ref, y_ref, lp_ref, x1_ref, gate_ref, g2_ref, fn_ref, o_ref, buf, sems):
    i = pl.program_id(0)
    n = pl.num_programs(0)
    slot = lax.rem(i, N_SLOTS)
    ahead = lax.rem(i + 2, N_SLOTS)

    def start(dref, s, c):
        d = pl.multiple_of(jnp.maximum(dref[0, 0, c], 0), RUN)
        dst = pl.multiple_of(RUN * c, RUN)
        pltpu.make_async_copy(y_ref.at[pl.ds(d, RUN), :], buf.at[s, pl.ds(dst, RUN), :], sems.at[s]).start()

    def drain(s):
        pltpu.make_async_copy(y_ref.at[pl.ds(0, RL), :], buf.at[s], sems.at[s]).wait()

    @pl.when(i == 0)
    def _():
        lax.fori_loop(0, RL // RUN, lambda c, carry: (start(d0_ref, 0, c), carry)[1], 0, unroll=4)
        lax.fori_loop(0, RL // RUN, lambda c, carry: (start(d1_ref, 1, c), carry)[1], 0, unroll=4)

    drain(slot)

    for c in range(RL // RUN):
        start(d2_ref, ahead, c)

    g = _pair_matrix(lp_ref[...], gate_ref[...]).astype(BF16)
    halves = [_dot(g, yb) for yb in _unpack_bf16_pairs(buf[slot])]
    xo = x1_ref[...] + g2_ref[0] * jnp.concatenate(halves, axis=1)
    ms = jnp.mean(xo * xo, axis=-1, keepdims=True)
    o_ref[...] = xo * lax.rsqrt(ms + EPS) * fn_ref[...]

    @pl.when(i == n - 1)
    def _():
        drain(lax.rem(i + 1, N_SLOTS))
        drain(ahead)


def _combine_call(cd, y, lp, x1, gates, g2, fn):
    T, d = x1.shape
    n = T // TB
    per_batch = n // g2.shape[0]
    return pl.pallas_call(
        _combine_kernel,
        grid=(n,),
        in_specs=[pl.BlockSpec((1, 1, CD_LANES), lambda i: (i, 0, 0), memory_space=pltpu.SMEM),
                  pl.BlockSpec((1, 1, CD_LANES), lambda i: (jnp.minimum(i + 1, n - 1), 0, 0),
                               memory_space=pltpu.SMEM),
                  pl.BlockSpec((1, 1, CD_LANES), lambda i: (jnp.minimum(i + 2, n - 1), 0, 0),
                               memory_space=pltpu.SMEM),
                  pl.BlockSpec(memory_space=pl.ANY),
                  pl.BlockSpec((TB, LANES), lambda i: (i, 0)),
                  pl.BlockSpec((TB, d), lambda i: (i, 0)),
                  pl.BlockSpec((TB, LANES), lambda i: (i, 0)),
                  pl.BlockSpec((1, 1, d), lambda i: (i // per_batch, 0, 0)),
                  pl.BlockSpec((1, d), lambda i: (0, 0))],
        out_specs=pl.BlockSpec((TB, d), lambda i: (i, 0)),
        out_shape=jax.ShapeDtypeStruct((T, d), F32),
        scratch_shapes=[pltpu.VMEM((N_SLOTS, RL, D_HALF), jnp.uint32), pltpu.SemaphoreType.DMA((N_SLOTS,))],
        compiler_params=_params("arbitrary"),
    )(cd, cd, cd, y, lp, x1, gates, g2, fn)


def _layer(x, c, ctx, c_ctx, w_mod, b_mod, norm1, w_in, gla_w_gk_up, gla_b_gk, gla_norm,
           ssd_conv_w, ssd_conv_b, ssd_dt_bias, ssd_A_log, ssd_D, ssd_norm, w_out,
           norm2, w_router, b_router, w_gate_up, b_gate_up, w_down, b_down, final_norm):
    bsz, L, d = x.shape
    lc = ctx.shape[1]
    assert lc == TB and L % TB == 0 and TB % GRID_W == 0

    cin = jnp.zeros((8, d), F32).at[:bsz].set(c).at[bsz].set(c_ctx)
    mod = _mod_call(cin, w_mod, b_mod.reshape(1, -1))[:bsz + 1]
    sh1, sc1, g1, sh2, sc2, g2 = [m.reshape(bsz + 1, 1, d) for m in jnp.split(mod, 6, axis=-1)]

    o = np.cumsum((0, GLA_QK, GLA_QK, GLA_V, GLA_V, GLA_RANK, SSD_INNER, SSD_CONV_DIM, SSD_HEADS))
    wq, wk, wv, wg, wlow, wz, wx, wdt = [w_in[:, int(a):int(b)] for a, b in zip(o[:-1], o[1:])]
    w_misc = jnp.concatenate([wlow, wdt, wdt, jnp.zeros((d, LANES - GLA_RANK - 2 * SSD_HEADS), F32)], axis=1)
    w_cat = jnp.concatenate([wq, wk, wv, wg, wz, wx, w_misc], axis=1).astype(BF16)
    wup = jnp.zeros((LANES, 2 * GLA_QK), F32).at[:GLA_RANK].set(
        jnp.concatenate([gla_w_gk_up[0], gla_w_gk_up[1]], axis=1)).astype(BF16)
    bup = jnp.concatenate([gla_b_gk[0], gla_b_gk[1]]).reshape(1, -1)
    dtb = jnp.zeros((1, LANES), F32).at[0, DT_F:DT_F + SSD_HEADS].set(ssd_dt_bias[0]) \
                                    .at[0, DT_B:DT_B + SSD_HEADS].set(ssd_dt_bias[1])
    q, k, v, g_all, z_all, xbc, ld, misc = _inproj_call(
        x, ctx, sh1, sc1, norm1.reshape(1, d), w_cat, wup, bup, dtb)

    xs, bc = _conv_call(xbc, ssd_conv_w.reshape(9, SSD_CONV_DIM), ssd_conv_b.reshape(1, -1), lc + L)

    o_f, o_b = _gla_call(q, k, v, ld, L)

    a_neg = -jnp.exp(ssd_A_log.astype(F32))
    a_f = jnp.zeros((1, LANES), F32).at[0, DT_F:DT_F + SSD_HEADS].set(a_neg[0])
    a_b = jnp.zeros((1, LANES), F32).at[0, DT_B:DT_B + SSD_HEADS].set(a_neg[1])
    dvec = jnp.repeat(ssd_D, SSD_HEADDIM).reshape(1, SSD_INNER)
    y_f, y_b = _ssd_call(xs, bc, misc, a_f, a_b, dvec, L)

    wr = jnp.zeros((d, LANES), F32).at[:, :N_EXPERTS].set(w_router)
    wr_hi = wr.astype(BF16)
    wr_lo = (wr - wr_hi.astype(F32)).astype(BF16)
    br = jnp.full((1, LANES), NEG_BIG, F32).at[0, :N_EXPERTS].set(b_router)
    x1, h2, eidx, gates, counts = _outproj_call(
        o_f, o_b, g_all, y_f, y_b, z_all, x, g1[:bsz], sh2[:bsz], sc2[:bsz],
        jnp.tile(gla_norm, GLA_HEADS).reshape(1, -1), ssd_norm.reshape(1, -1), w_out.astype(BF16),
        norm2.reshape(1, d), wr_hi, wr_lo, br)

    T = bsz * L
    cnt = counts[0, :N_EXPERTS].astype(jnp.int32)
    padded = ((cnt + MOE_BLOCK - 1) // MOE_BLOCK) * MOE_BLOCK
    pend = jnp.cumsum(padded)
    pstart = pend - padded
    max_rows = T * TOP_K + (T // TB) * N_EXPERTS * (RUN - 1)
    n_blocks = -(-max_rows // MOE_BLOCK) + N_EXPERTS
    blk_start = jnp.arange(n_blocks, dtype=jnp.int32) * MOE_BLOCK
    blk_e = jnp.minimum(jnp.sum(pend[None, :] <= blk_start[:, None], axis=1), N_EXPERTS - 1).astype(jnp.int32)
    n_used = (pend[-1] // MOE_BLOCK).astype(jnp.int32)
    blk_i = jnp.arange(n_blocks, dtype=jnp.int32)
    first = (blk_i < n_used) & ((blk_i == 0) | (blk_e != jnp.roll(blk_e, 1)))
    slot = (jnp.cumsum(first) - 1) & 1
    first_pos = jnp.where(first, blk_i, n_blocks)
    next_first = jnp.roll(lax.cummin(first_pos, reverse=True), -1).at[-1].set(n_blocks)
    nxt = jnp.where(next_first < n_blocks, blk_e[jnp.minimum(next_first, n_blocks - 1)], -1)
    col = lambda v, last: jnp.concatenate([v.astype(jnp.int32), jnp.asarray([last], jnp.int32)])
    blk_info = jnp.stack([col(blk_e, 0).at[-1].set(n_used), col(first, 0), col(slot, 0), col(nxt, -1)])
    n_tail = n_blocks - (T * TOP_K) // MOE_BLOCK
    tail = n_used + jnp.arange(n_tail, dtype=jnp.int32)
    fill_blocks = jnp.concatenate([
        jnp.where(padded > 0, pend // MOE_BLOCK - 1, -1),
        jnp.where(tail < n_blocks, tail, -1)]).astype(jnp.int32)
    pstart_row = jnp.zeros((1, LANES), F32).at[0, :N_EXPERTS].set(pstart.astype(F32))
    lp, cd = _pos_call(eidx.reshape(T, LANES), pstart_row)

    xg = _dispatch_call(fill_blocks, cd, h2.reshape(T, d), lp, n_blocks * MOE_BLOCK)
    y = _moe_call(blk_info, xg, w_gate_up, b_gate_up, w_down, b_down, n_blocks)
    out = _combine_call(cd, y, lp, x1.reshape(T, d), gates.reshape(T, LANES), g2[:bsz], final_norm.reshape(1, d))
    return out.reshape(bsz, L, d)


def kernel(x, c, ctx, c_ctx, w_mod, b_mod, norm1, w_in, gla_w_gk_up, gla_b_gk, gla_norm, ssd_conv_w, ssd_conv_b, ssd_dt_bias, ssd_A_log, ssd_D, ssd_norm, w_out, norm2, w_router, b_router, w_gate_up, b_gate_up, w_down, b_down, final_norm):
    assert w_mod.shape[0] == 1, "single-layer kernel"
    return _layer(x, c, ctx, c_ctx, w_mod[0], b_mod[0], norm1[0], w_in[0], gla_w_gk_up[0], gla_b_gk[0],
                  gla_norm[0], ssd_conv_w[0], ssd_conv_b[0], ssd_dt_bias[0], ssd_A_log[0], ssd_D[0],
                  ssd_norm[0], w_out[0], norm2[0], w_router[0], b_router[0], w_gate_up[0], b_gate_up[0],
                  w_down[0], b_down[0], final_norm)
```

```python
import functools

import numpy as np
import jax
import jax.numpy as jnp
from jax import lax
from jax.experimental import pallas as pl
from jax.experimental.pallas import tpu as pltpu

F32 = jnp.float32
BF16 = jnp.bfloat16

EPS = 1e-6
GRID_W = 64
GLA_HEADS = 4
GLA_DK = 64
GLA_DV = 128
GLA_QK = GLA_HEADS * GLA_DK
GLA_V = GLA_HEADS * GLA_DV
GLA_RANK = 16
GLA_GATE_NORM = 16.0
SSD_HEADDIM = 64
SSD_INNER = 512
SSD_HEADS = 8
SSD_GROUPS = 2
SSD_HPG = 4
SSD_STATE = 128
SSD_CONV_DIM = 1024
N_EXPERTS = 32
TOP_K = 4
D_FF = 1024
SWIGLU_LIMIT = 7.0
SWIGLU_ALPHA = 1.702
MOE_BLOCK = 512

TB = 256
GLA_C = 64
GLA_STAGE_GROUP = 4
SSD_C = 128
SSD_STAGE_GROUP = 1
LANES = 128
EXP_CLAMP = 80.0
DT_F = 16
DT_B = 24
NEG_BIG = -1e30
RUN = 8
RL = 1280
CD_LANES = 256
SPILL_ROWS = -(-RL // MOE_BLOCK) * MOE_BLOCK
D_HALF = 512
VMEM_LIMIT = 56 * 1024 * 1024


def _dot(a, b):
    return jnp.dot(a, b, preferred_element_type=F32)


def _dot_nt(a, b):
    return lax.dot_general(a, b, (((1,), (1,)), ((), ())), preferred_element_type=F32)


def _dot_tn(a, b):
    return lax.dot_general(a, b, (((0,), (0,)), ((), ())), preferred_element_type=F32)


def _split3(a):
    hi = a.astype(BF16)
    r1 = a - hi.astype(F32)
    mid = r1.astype(BF16)
    lo = (r1 - mid.astype(F32)).astype(BF16)
    return hi, mid, lo


def _dot_exact_r(m, a):
    hi, mid, lo = _split3(a)
    return _dot(m, hi) + _dot(m, mid) + _dot(m, lo)


def _dot_hilo_r(m, a):
    hi = a.astype(BF16)
    lo = (a - hi.astype(F32)).astype(BF16)
    return _dot(m, hi) + _dot(m, lo)


def _dot_hilo_l(a, m2):
    hi = a.astype(BF16)
    lo = (a - hi.astype(F32)).astype(BF16)
    return _dot(jnp.concatenate([hi, lo], axis=1), m2)


def _sigmoid(x):
    return 1.0 / (1.0 + jnp.exp(-x))


def _softplus(x):
    return jnp.maximum(x, 0.0) + jnp.log1p(jnp.exp(-jnp.abs(x)))


def _params(*sem):
    return pltpu.CompilerParams(dimension_semantics=sem, vmem_limit_bytes=VMEM_LIMIT)


def _mod_kernel(c_ref, w_ref, b_ref, o_ref):
    c = c_ref[...]
    s = c * _sigmoid(c)
    s_hi = s.astype(BF16)
    s_lo = (s - s_hi.astype(F32)).astype(BF16)
    w = w_ref[...]
    w_hi = w.astype(BF16)
    w_lo = (w - w_hi.astype(F32)).astype(BF16)
    o_ref[...] = _dot(s_hi, w_hi) + _dot(s_lo, w_hi) + _dot(s_hi, w_lo) + b_ref[...]


def _mod_call(cin, w, b):
    rows, d = cin.shape
    n = w.shape[1]
    tn = 1536
    return pl.pallas_call(
        _mod_kernel,
        grid=(n // tn,),
        in_specs=[pl.BlockSpec((rows, d), lambda i: (0, 0)),
                  pl.BlockSpec((d, tn), lambda i: (0, i)),
                  pl.BlockSpec((1, tn), lambda i: (0, i))],
        out_specs=pl.BlockSpec((rows, tn), lambda i: (0, i)),
        out_shape=jax.ShapeDtypeStruct((rows, n), F32),
        compiler_params=_params("arbitrary"),
    )(cin, w, b)


_C_Q, _C_K, _C_V, _C_G, _C_Z, _C_X, _C_M, _C_END = 0, 256, 512, 1024, 1536, 2048, 3072, 3200


def _inproj_kernel(xa_ref, xb_ref, ctx_ref, shl_ref, scl_ref, shc_ref, scc_ref, n1_ref, w_ref, wup_ref, bup_ref,
                   dtb_ref, q_ref, k_ref, v_ref, g_ref, z_ref, xbc_ref, ld_ref, misc_ref, h_scr):
    j = pl.program_id(1)

    def normmod(xv, sh_ref, sc_ref):
        ms = jnp.mean(xv * xv, axis=-1, keepdims=True)
        y = xv * lax.rsqrt(ms + EPS) * n1_ref[...]
        return (y * (1.0 + sc_ref[0]) + sh_ref[0]).astype(BF16)

    @pl.when(j == 0)
    def _():
        h_scr[0:TB, :] = normmod(ctx_ref[0], shc_ref, scc_ref)

    @pl.when(j > 0)
    def _():
        h_scr[0:TB, :] = normmod(xa_ref[0], shl_ref, scl_ref)

    h_scr[TB:2 * TB, :] = normmod(xb_ref[0], shl_ref, scl_ref)
    h = h_scr[...]

    def mm(lo, hi):
        return _dot(h, w_ref[:, lo:hi])

    m = mm(_C_M, _C_END)
    zz = _dot(m.astype(BF16), wup_ref[...]) + bup_ref[...]
    ld_ref[0] = -_softplus(-zz) * (1.0 / GLA_GATE_NORM)
    misc_ref[0] = _softplus(m + dtb_ref[...])
    q_ref[0] = (mm(_C_Q, _C_K) * (GLA_DK ** -0.5)).astype(BF16)
    k_ref[0] = mm(_C_K, _C_V).astype(BF16)
    v_ref[0] = mm(_C_V, _C_G).astype(BF16)
    g_ref[0] = mm(_C_G, _C_Z)
    z_ref[0] = mm(_C_Z, _C_X)
    xbc_ref[0] = mm(_C_X, _C_M)


def _inproj_call(x, ctx, sh1, sc1, n1, w_cat, wup, bup, dtb):
    bsz, L, d = x.shape
    nx = L // TB
    nj = -(-(nx + 1) // 2)
    tok = lambda n: pl.BlockSpec((1, 2 * TB, n), lambda b, j: (b, j, 0))
    const = lambda a: pl.BlockSpec(a.shape, lambda b, j: (0,) * a.ndim)
    xblk = lambda off: pl.BlockSpec((1, TB, d), lambda b, j: (b, jnp.clip(2 * j + off, 0, nx - 1), 0))
    mod_lat = pl.BlockSpec((1, 1, d), lambda b, j: (b, 0, 0))
    mod_ctx = pl.BlockSpec((1, 1, d), lambda b, j: (bsz, 0, 0))
    outs = [(GLA_QK, BF16), (GLA_QK, BF16), (GLA_V, BF16), (GLA_V, F32), (SSD_INNER, F32),
            (SSD_CONV_DIM, F32), (2 * GLA_QK, F32), (LANES, F32)]
    return pl.pallas_call(
        _inproj_kernel,
        grid=(bsz, nj),
        in_specs=[xblk(-1), xblk(0), pl.BlockSpec((1, TB, d), lambda b, j: (b, 0, 0)),
                  mod_lat, mod_lat, mod_ctx, mod_ctx,
                  const(n1), const(w_cat), const(wup), const(bup), const(dtb)],
        out_specs=[tok(n) for n, _ in outs],
        out_shape=[jax.ShapeDtypeStruct((bsz, nj * 2 * TB, n), dt) for n, dt in outs],
        scratch_shapes=[pltpu.VMEM((2 * TB, d), BF16)],
        compiler_params=_params("arbitrary", "arbitrary"),
    )(x, x, ctx, sh1, sc1, sh1, sc1, n1, w_cat, wup, bup, dtb)


_EXT_PAD = 8
_EXT_BASE = _EXT_PAD + GRID_W
_EXT_ROWS = 2 * _EXT_PAD + 2 * GRID_W + TB


def _conv_kernel(prev_ref, cur_ref, next_ref, w_ref, b_ref, xs_ref, bc_ref, ext):
    j = pl.program_id(1)
    nj = pl.num_programs(1)
    is_ctx = j == 0
    zpad = jnp.zeros((_EXT_PAD, SSD_CONV_DIM), F32)
    ext[0:_EXT_PAD, :] = zpad
    ext[_EXT_ROWS - _EXT_PAD:_EXT_ROWS, :] = zpad
    ext[_EXT_PAD:_EXT_BASE, :] = jnp.where(j >= 2, prev_ref[0], 0.0)
    ext[_EXT_BASE:_EXT_BASE + TB, :] = cur_ref[0]
    ext[_EXT_BASE + TB:_EXT_BASE + TB + GRID_W, :] = jnp.where(
        jnp.logical_and(j >= 1, j <= nj - 2), next_ref[0], 0.0)

    t = lax.broadcasted_iota(jnp.int32, (TB, LANES), 0)
    col = t & (GRID_W - 1)
    ok_l = jnp.where(is_ctx, t, col) >= 1
    ok_r = jnp.where(is_ctx, t - (TB - GRID_W), col) <= GRID_W - 2
    lat = jnp.where(is_ctx, 0.0, 1.0)

    for c in range(SSD_CONV_DIM // LANES):
        lo, hi = c * LANES, (c + 1) * LANES
        acc = jnp.zeros((TB, LANES), F32)
        for dr in (-1, 0, 1):
            for dc in (-1, 0, 1):
                start = _EXT_BASE + GRID_W * dr + dc
                tap = ext[start:start + TB, lo:hi]
                wi = 3 * (dr + 1) + (dc + 1)
                wv = w_ref[wi:wi + 1, lo:hi]
                if dr != 0:
                    wv = wv * lat
                if dc == -1:
                    tap = jnp.where(ok_l, tap, 0.0)
                elif dc == 1:
                    tap = jnp.where(ok_r, tap, 0.0)
                acc = acc + tap * wv
        y = acc + b_ref[:, lo:hi]
        y = y * _sigmoid(y)
        if c < SSD_INNER // LANES:
            xs_ref[0, :, lo:hi] = y
        else:
            bc_ref[0, :, lo - SSD_INNER:hi - SSD_INNER] = y.astype(BF16)


def _conv_call(xbc, w9, bias, ls):
    bsz, _, ch = xbc.shape
    nj = ls // TB
    rpb = TB // GRID_W
    nrow = ls // GRID_W
    return pl.pallas_call(
        _conv_kernel,
        grid=(bsz, nj),
        in_specs=[pl.BlockSpec((1, GRID_W, ch), lambda b, j: (b, jnp.maximum(rpb * j - 1, 0), 0)),
                  pl.BlockSpec((1, TB, ch), lambda b, j: (b, j, 0)),
                  pl.BlockSpec((1, GRID_W, ch), lambda b, j: (b, jnp.minimum(rpb * j + rpb, nrow - 1), 0)),
                  pl.BlockSpec((9, ch), lambda b, j: (0, 0)),
                  pl.BlockSpec((1, ch), lambda b, j: (0, 0))],
        out_specs=[pl.BlockSpec((1, TB, SSD_INNER), lambda b, j: (b, j, 0)),
                   pl.BlockSpec((1, TB, ch - SSD_INNER), lambda b, j: (b, j, 0))],
        out_shape=[jax.ShapeDtypeStruct((bsz, ls, SSD_INNER), F32),
                   jax.ShapeDtypeStruct((bsz, ls, ch - SSD_INNER), BF16)],
        scratch_shapes=[pltpu.VMEM((_EXT_ROWS, ch), F32)],
        compiler_params=_params("arbitrary", "arbitrary"),
    )(xbc, xbc, xbc, w9, bias)


def _fwd_blk(s):
    return s


def _bwd_blk(s, ns):
    return jnp.where(s == 0, 0, ns - s)


def _scan_batch(bsz):
    return 2 if bsz % 2 == 0 else 1


class _GlaChunk:
    def __init__(self, q, k, v, la, st_ref, tri_m, fwd, store):
        self.q, self.k, self.v, self.la, self.st_ref, self.tri_m, self.fwd, self.store = (
            q, k, v, la, st_ref, tri_m, fwd, store)

    def stage_sums(self):
        self.b = _dot_hilo_r(self.tri_m, self.la)

    def stage_factors(self):
        C = GLA_C
        b = self.b
        self.bt = b[C - 1:C, :] if self.fwd else b[0:1, :]
        r = 0.5 * self.bt
        self.er = jnp.exp(r)
        self.qt = (self.q.astype(F32) * jnp.exp(jnp.minimum(b - r, EXP_CLAMP))).astype(BF16)
        kt = (self.k.astype(F32) * jnp.exp(jnp.minimum(r - b, EXP_CLAMP))).astype(BF16)
        head_k = lax.broadcasted_iota(jnp.int32, (C, GLA_QK), 1) >> 6
        zero = jnp.zeros_like(kt)
        self.kh = [jnp.where(head_k == h, kt, zero) for h in range(GLA_HEADS)]
        self.qh = [jnp.where(head_k == h, self.qt, zero) for h in range(GLA_HEADS)]

    def stage_products(self):
        v = self.v
        kcat = jnp.concatenate(self.kh, axis=0)
        self.sc = _dot_nt(self.qt, kcat)
        vcat = jnp.concatenate([v[:, h * GLA_DV:(h + 1) * GLA_DV] for h in range(GLA_HEADS)], axis=0)
        self.u = _dot_tn(vcat, kcat) * self.er

    def stage_mask(self):
        C = GLA_C
        ii = lax.broadcasted_iota(jnp.int32, (C, GLA_HEADS * C), 0)
        jj = lax.broadcasted_iota(jnp.int32, (C, GLA_HEADS * C), 1) & (C - 1)
        causal = (jj <= ii) if self.fwd else (jj >= ii)
        self.p = jnp.where(causal, self.sc, 0.0).astype(BF16)
        v = self.v
        head_v = lax.broadcasted_iota(jnp.int32, (C, GLA_V), 1) >> 7
        self.vst = jnp.concatenate([jnp.where(head_v == h, v, jnp.zeros_like(v)) for h in range(GLA_HEADS)], axis=0)

    def stage_intra(self):
        self.o = _dot(self.p, self.vst)

    def stage_inter(self):
        st = self.st_ref[...]
        ster = (st * self.er).astype(BF16)
        self.st_ref[...] = st * jnp.exp(self.bt) + self.u
        res = _dot_nt(jnp.concatenate(self.qh, axis=0), ster)
        self.inter = jnp.concatenate([res[h * GLA_C:(h + 1) * GLA_C, :] for h in range(GLA_HEADS)], axis=1)

    def stage_out(self):
        self.store(self.o + self.inter)


def _gla_kernel(qf_ref, kf_ref, vf_ref, lf_ref, qb_ref, kb_ref, vb_ref, lb_ref, trif_ref, trib_ref,
                of_ref, ob_ref, stf, stb):
    s = pl.program_id(1)

    @pl.when(s == 0)
    def _():
        stf[...] = jnp.zeros_like(stf)
        stb[...] = jnp.zeros_like(stb)

    nsub = TB // GLA_C

    def store_to(ref, bb, sl):
        def store(val):
            ref[bb, sl, :] = val
        return store

    group = GLA_STAGE_GROUP
    for g0 in range(0, nsub, group):
        steps = []
        for i in range(g0, g0 + group):
            sf = pl.ds(i * GLA_C, GLA_C)
            sb = pl.ds((nsub - 1 - i) * GLA_C, GLA_C)
            chunks = []
            for bb in range(qf_ref.shape[0]):
                chunks.append(_GlaChunk(qf_ref[bb, sf, :], kf_ref[bb, sf, :], vf_ref[bb, sf, :], lf_ref[bb, sf, :],
                                        stf.at[bb], trif_ref[...], True, store_to(of_ref, bb, sf)))
                chunks.append(_GlaChunk(qb_ref[bb, sb, :], kb_ref[bb, sb, :], vb_ref[bb, sb, :], lb_ref[bb, sb, :],
                                        stb.at[bb], trib_ref[...], False, store_to(ob_ref, bb, sb)))
            steps.append(chunks)
        for stage in ("stage_sums", "stage_factors", "stage_products", "stage_mask", "stage_intra"):
            for chunks in steps:
                for ch in chunks:
                    getattr(ch, stage)()
        for chunks in steps:
            for stage in ("stage_inter", "stage_out"):
                for ch in chunks:
                    getattr(ch, stage)()


def _gla_call(q, k, v, ld, L):
    bsz = q.shape[0]
    nx = L // TB
    ns = nx + 1
    trif = jnp.asarray(np.tril(np.ones((GLA_C, GLA_C), np.float32)), BF16)
    trib = jnp.asarray(np.triu(np.ones((GLA_C, GLA_C), np.float32)), BF16)
    nb = _scan_batch(bsz)
    f = lambda n, lane=0: pl.BlockSpec((nb, TB, n), lambda b, s: (b, _fwd_blk(s), lane))
    r = lambda n, lane=0: pl.BlockSpec((nb, TB, n), lambda b, s: (b, _bwd_blk(s, ns), lane))
    tri = pl.BlockSpec((GLA_C, GLA_C), lambda b, s: (0, 0))
    return pl.pallas_call(
        _gla_kernel,
        grid=(bsz // nb, ns),
        in_specs=[f(GLA_QK), f(GLA_QK), f(GLA_V), f(GLA_QK, 0),
                  r(GLA_QK), r(GLA_QK), r(GLA_V), r(GLA_QK, 1), tri, tri],
        out_specs=[pl.BlockSpec((nb, TB, GLA_V), lambda b, s: (b, jnp.maximum(s - 1, 0), 0)),
                   pl.BlockSpec((nb, TB, GLA_V), lambda b, s: (b, jnp.where(s == 0, nx - 1, nx - s), 0))],
        out_shape=[jax.ShapeDtypeStruct((bsz, L, GLA_V), F32)] * 2,
        scratch_shapes=[pltpu.VMEM((nb, GLA_DV, GLA_QK), F32)] * 2,
        compiler_params=_params("arbitrary", "arbitrary"),
    )(q, k, v, ld, q, k, v, ld, trif, trib)


class _SsdChunk:
    def __init__(self, xs, bc, dtm, avec, dvec, st_ref, tri_m, e_m, base, fwd, store):
        self.xs, self.bc, self.dtm, self.avec, self.dvec, self.st_ref = xs, bc, dtm, avec, dvec, st_ref
        self.tri_m, self.e_m, self.base, self.fwd, self.store = tri_m, e_m, base, fwd, store

    def stage_sums(self):
        self.dt_exp = _dot_hilo_l(self.dtm, self.e_m)
        self.acum = _dot_exact_r(self.tri_m, self.dtm * self.avec)

    def stage_expand(self):
        self.acum_exp = _dot_hilo_l(self.acum, self.e_m)
        self.acum_t = self.acum.T
        self.xdt = self.xs * self.dt_exp
        bc = self.bc
        self.bg = [bc[:, 128 * g:128 * (g + 1)] for g in range(SSD_GROUPS)]
        self.cg = [bc[:, 256 + 128 * g:256 + 128 * (g + 1)] for g in range(SSD_GROUPS)]
        self.cb = [_dot_nt(self.cg[g], self.bg[g]) for g in range(SSD_GROUPS)]

    def stage_decay(self):
        C = SSD_C
        ii = lax.broadcasted_iota(jnp.int32, (C, C), 0)
        jj = lax.broadcasted_iota(jnp.int32, (C, C), 1)
        tri = (jj <= ii) if self.fwd else (jj >= ii)
        self.ms = []
        for g in range(SSD_GROUPS):
            for rr in range(SSD_HPG):
                ln = self.base + SSD_HPG * g + rr
                diff = self.acum[:, ln:ln + 1] - self.acum_t[ln:ln + 1, :]
                seg = jnp.where(tri, jnp.exp(jnp.minimum(diff, 0.0)), 0.0)
                self.ms.append((self.cb[g] * seg).astype(BF16))
        ae = self.acum_exp
        self.al_exp = ae[C - 1:C, :] if self.fwd else ae[0:1, :]
        self.xw = (self.xdt * jnp.exp(self.al_exp - ae)).astype(BF16)
        xdt_b = self.xdt.astype(BF16)
        head = lax.broadcasted_iota(jnp.int32, (C, SSD_INNER), 1) >> 6
        zero = jnp.zeros_like(xdt_b)
        self.xh = [jnp.where(head == h, xdt_b, zero) for h in range(SSD_HEADS)]

    def stage_products(self):
        self.yg, self.ug = [], []
        for g in range(SSD_GROUPS):
            gl, gh = 256 * g, 256 * (g + 1)
            yg = _dot(self.ms[SSD_HPG * g], self.xh[SSD_HPG * g][:, gl:gh])
            for rr in range(1, SSD_HPG):
                yg = yg + _dot(self.ms[SSD_HPG * g + rr], self.xh[SSD_HPG * g + rr][:, gl:gh])
            self.yg.append(yg)
            self.ug.append(_dot_tn(self.bg[g], self.xw[:, gl:gh]))

    def stage_state(self):
        ys = []
        for g in range(SSD_GROUPS):
            gl, gh = 256 * g, 256 * (g + 1)
            sg = self.st_ref[g]
            yoff = _dot(self.cg[g], sg.astype(BF16)) * jnp.exp(self.acum_exp[:, gl:gh])
            self.st_ref[g] = sg * jnp.exp(self.al_exp[:, gl:gh]) + self.ug[g]
            ys.append(self.yg[g] + yoff)
        y = jnp.concatenate(ys, axis=1)
        if self.dvec is not None:
            y = y + self.dvec * self.xs
        self.store(y)


def _ssd_kernel(xf_ref, bcf_ref, mf_ref, xb_ref, bcb_ref, mb_ref, af_ref, ab_ref, d_ref,
                trif_ref, trib_ref, ef_ref, eb_ref, yf_ref, yb_ref, stf, stb):
    s = pl.program_id(1)

    @pl.when(s == 0)
    def _():
        stf[...] = jnp.zeros_like(stf)
        stb[...] = jnp.zeros_like(stb)

    nsub = TB // SSD_C

    def store_to(ref, bb, sl):
        def store(val):
            ref[bb, sl, :] = val
        return store

    steps = []
    for i in range(nsub):
        sf = pl.ds(i * SSD_C, SSD_C)
        sb = pl.ds((nsub - 1 - i) * SSD_C, SSD_C)
        chunks = []
        for bb in range(xf_ref.shape[0]):
            chunks.append(_SsdChunk(xf_ref[bb, sf, :], bcf_ref[bb, sf, :], mf_ref[bb, sf, :], af_ref[...],
                                    d_ref[...], stf.at[bb], trif_ref[...], ef_ref[...], DT_F, True,
                                    store_to(yf_ref, bb, sf)))
            chunks.append(_SsdChunk(xb_ref[bb, sb, :], bcb_ref[bb, sb, :], mb_ref[bb, sb, :], ab_ref[...],
                                    None, stb.at[bb], trib_ref[...], eb_ref[...], DT_B, False,
                                    store_to(yb_ref, bb, sb)))
        steps.append(chunks)
    for g0 in range(0, nsub, SSD_STAGE_GROUP):
        group = steps[g0:g0 + SSD_STAGE_GROUP]
        for stage in ("stage_sums", "stage_expand", "stage_decay", "stage_products"):
            for chunks in group:
                for ch in chunks:
                    getattr(ch, stage)()
        for chunks in group:
            for ch in chunks:
                ch.stage_state()


def _expand_matrix(base):
    e = np.zeros((LANES, SSD_INNER), np.float32)
    for h in range(SSD_HEADS):
        e[base + h, SSD_HEADDIM * h:SSD_HEADDIM * (h + 1)] = 1.0
    return jnp.asarray(np.concatenate([e, e], axis=0), BF16)


def _ssd_call(xs, bc, misc, a_f, a_b, dvec, L):
    bsz, ls, _ = xs.shape
    ns = ls // TB
    nx = L // TB
    trif = jnp.asarray(np.tril(np.ones((SSD_C, SSD_C), np.float32)), BF16)
    trib = jnp.asarray(np.triu(np.ones((SSD_C, SSD_C), np.float32)), BF16)
    ef, eb = _expand_matrix(DT_F), _expand_matrix(DT_B)
    nb = _scan_batch(bsz)
    f = lambda n: pl.BlockSpec((nb, TB, n), lambda b, s: (b, _fwd_blk(s), 0))
    r = lambda n: pl.BlockSpec((nb, TB, n), lambda b, s: (b, _bwd_blk(s, ns), 0))
    const = lambda a: pl.BlockSpec(a.shape, lambda b, s: (0,) * a.ndim)
    return pl.pallas_call(
        _ssd_kernel,
        grid=(bsz // nb, ns),
        in_specs=[f(SSD_INNER), f(512), f(LANES), r(SSD_INNER), r(512), r(LANES),
                  const(a_f), const(a_b), const(dvec), const(trif), const(trib), const(ef), const(eb)],
        out_specs=[pl.BlockSpec((nb, TB, SSD_INNER), lambda b, s: (b, jnp.maximum(s - 1, 0), 0)),
                   pl.BlockSpec((nb, TB, SSD_INNER), lambda b, s: (b, jnp.where(s == 0, nx - 1, nx - s), 0))],
        out_shape=[jax.ShapeDtypeStruct((bsz, L, SSD_INNER), F32)] * 2,
        scratch_shapes=[pltpu.VMEM((nb, SSD_GROUPS, SSD_STATE, SSD_HPG * SSD_HEADDIM), F32)] * 2,
        compiler_params=_params("arbitrary", "arbitrary"),
    )(xs, bc, misc, xs, bc, misc, a_f, a_b, dvec, trif, trib, ef, eb)


def _outproj_kernel(of_ref, ob_ref, ga_ref, gb_ref, yf_ref, yb_ref, za_ref, zb_ref, x_ref, g1_ref, sh2_ref, sc2_ref,
                    gn_ref, sn_ref, wo_ref, n2_ref, wrh_ref, wrl_ref, br_ref,
                    x1_ref, h2_ref, eidx_ref, gate_ref, cnt_ref, mix_s, hl_s):
    first = jnp.logical_and(pl.program_id(0) == 0, pl.program_id(1) == 0)

    @pl.when(first)
    def _():
        cnt_ref[...] = jnp.zeros_like(cnt_ref)

    halves = [(pl.ds(0, TB), ga_ref, za_ref), (pl.ds(TB, TB), gb_ref, zb_ref)]

    for rows, g_ref, z_ref in halves:
        o = of_ref[0, rows, :] + ob_ref[0, rows, :]
        gg = g_ref[0]
        for h in range(GLA_HEADS):
            lo, hi = GLA_DV * h, GLA_DV * (h + 1)
            oh = o[:, lo:hi]
            ms = jnp.mean(oh * oh, axis=-1, keepdims=True)
            gh = gg[:, lo:hi]
            mix_s[rows, lo:hi] = (oh * lax.rsqrt(ms + EPS) * gn_ref[:, lo:hi] * (gh * _sigmoid(gh))).astype(BF16)
        zz = z_ref[0]
        u = (yf_ref[0, rows, :] + yb_ref[0, rows, :]) * (zz * _sigmoid(zz))
        gw = SSD_INNER // SSD_GROUPS
        for g in range(SSD_GROUPS):
            lo, hi = gw * g, gw * (g + 1)
            ug = u[:, lo:hi]
            ms = jnp.mean(ug * ug, axis=-1, keepdims=True)
            mix_s[rows, GLA_V + lo:GLA_V + hi] = (ug * lax.rsqrt(ms + EPS) * sn_ref[:, lo:hi]).astype(BF16)

    x1_ref[0] = x_ref[0] + g1_ref[0] * _dot(mix_s[...], wo_ref[...])

    for rows, _, _ in halves:
        x1 = x1_ref[0, rows, :]
        ms = jnp.mean(x1 * x1, axis=-1, keepdims=True)
        h2 = (x1 * lax.rsqrt(ms + EPS) * n2_ref[...]) * (1.0 + sc2_ref[0]) + sh2_ref[0]
        h_hi = h2.astype(BF16)
        h2_ref[0, rows, :] = h_hi
        hl_s[rows, :] = (h2 - h_hi.astype(F32)).astype(BF16)

    h_hi = h2_ref[0]
    logits = (_dot(h_hi, wrh_ref[...]) + _dot(hl_s[...], wrh_ref[...]) + _dot(h_hi, wrl_ref[...])) + br_ref[...]

    lane = lax.broadcasted_iota(jnp.int32, (TB, LANES), 1).astype(F32)
    st = [dict(work=logits[i * TB:(i + 1) * TB, :], eidx=jnp.full((TB, LANES), -1.0, F32),
               gates=jnp.zeros((TB, LANES), F32), sel=jnp.zeros((TB, LANES), F32), m0=None,
               den=jnp.zeros((TB, 1), F32)) for i in range(2)]
    for kk in range(TOP_K):
        for t in st:
            t["m"] = jnp.max(t["work"], axis=-1, keepdims=True)
        for t in st:
            t["idx"] = jnp.min(jnp.where(t["work"] == t["m"], lane, float(LANES)), axis=-1, keepdims=True)
        for t in st:
            hit = lane == t["idx"]
            if t["m0"] is None:
                t["m0"] = t["m"]
            e = jnp.exp(t["m"] - t["m0"])
            t["den"] = t["den"] + e
            t["eidx"] = jnp.where(lane == float(kk), t["idx"], t["eidx"])
            t["gates"] = jnp.where(lane == float(kk), e, t["gates"])
            t["sel"] = jnp.where(hit, 1.0, t["sel"])
            t["work"] = jnp.where(hit, NEG_BIG, t["work"])
    for (rows, _, _), t in zip(halves, st):
        eidx_ref[0, rows, :] = t["eidx"].astype(jnp.int32)
        gate_ref[0, rows, :] = t["gates"] / t["den"]
        cnt = jnp.sum(t["sel"], axis=0, keepdims=True)
        cnt_ref[...] += jnp.floor((cnt + (RUN - 1.0)) * (1.0 / RUN)) * RUN


def _outproj_call(o_f, o_b, g_all, y_f, y_b, z_all, x, g1, sh2, sc2, gn, sn, wo, n2, wr_hi, wr_lo, br):
    bsz, L, d = x.shape
    nj = L // (2 * TB)
    tok = lambda n: pl.BlockSpec((1, 2 * TB, n), lambda b, j: (b, j, 0))
    tok_off = lambda n, half: pl.BlockSpec((1, TB, n), lambda b, j: (b, 2 * j + 1 + half, 0))
    const = lambda a: pl.BlockSpec(a.shape, lambda b, j: (0,) * a.ndim)
    mod = pl.BlockSpec((1, 1, d), lambda b, j: (b, 0, 0))
    return pl.pallas_call(
        _outproj_kernel,
        grid=(bsz, nj),
        in_specs=[tok(GLA_V), tok(GLA_V), tok_off(GLA_V, 0), tok_off(GLA_V, 1),
                  tok(SSD_INNER), tok(SSD_INNER), tok_off(SSD_INNER, 0), tok_off(SSD_INNER, 1),
                  tok(d), mod, mod, mod, const(gn), const(sn), const(wo), const(n2),
                  const(wr_hi), const(wr_lo), const(br)],
        out_specs=[tok(d), tok(d), tok(LANES), tok(LANES), pl.BlockSpec((1, LANES), lambda b, j: (0, 0))],
        out_shape=[jax.ShapeDtypeStruct((bsz, L, d), F32), jax.ShapeDtypeStruct((bsz, L, d), BF16),
                   jax.ShapeDtypeStruct((bsz, L, LANES), jnp.int32), jax.ShapeDtypeStruct((bsz, L, LANES), F32),
                   jax.ShapeDtypeStruct((1, LANES), F32)],
        scratch_shapes=[pltpu.VMEM((2 * TB, d), BF16), pltpu.VMEM((2 * TB, d), BF16)],
        compiler_params=_params("arbitrary", "arbitrary"),
    )(o_f, o_b, g_all, g_all, y_f, y_b, z_all, z_all, x, g1, sh2, sc2, gn, sn, wo, n2, wr_hi, wr_lo, br)


def _pos_kernel(eidx_ref, pstart_ref, lst_ref, ust_ref, lp_ref, cd_ref, carry):
    @pl.when(pl.program_id(0) == 0)
    def _():
        carry[...] = pstart_ref[...]

    eidx = eidx_ref[...]
    lane = lax.broadcasted_iota(jnp.int32, (TB, LANES), 1)
    hits = [lane == eidx[:, kk:kk + 1] for kk in range(TOP_K)]
    sel = jnp.zeros((TB, LANES), F32)
    for hmask in hits:
        sel = jnp.where(hmask, 1.0, sel)
    cnt = jnp.sum(sel, axis=0, keepdims=True)
    run = jnp.floor((cnt + (RUN - 1.0)) * (1.0 / RUN)) * RUN
    rank = _dot(lst_ref[...], sel.astype(BF16))
    loff = _dot(jnp.broadcast_to(run, (8, LANES)).astype(BF16), ust_ref[...])[0:1]
    pos = loff + rank
    lp = jnp.zeros((TB, LANES), jnp.int32)
    for kk, hmask in enumerate(hits):
        lk = jnp.sum(jnp.where(hmask, pos, 0.0), axis=-1, keepdims=True)
        lp = jnp.where(lane == kk, lk.astype(jnp.int32), lp)
    lp_ref[...] = lp

    base = carry[...]
    eye = lax.broadcasted_iota(jnp.int32, (LANES, LANES), 0) == lax.broadcasted_iota(jnp.int32, (LANES, LANES), 1)
    col = lambda v: jnp.sum(jnp.where(eye, v, 0.0), axis=1, keepdims=True)
    loff_c, run_c, shift_c = col(loff), col(run), col(base - loff)
    row0 = (lax.broadcasted_iota(jnp.int32, (LANES, CD_LANES), 1) * RUN).astype(F32)
    inside = jnp.where(row0 >= loff_c, 1.0, 0.0) * jnp.where(row0 < loff_c + run_c, 1.0, 0.0)
    valid = jnp.sum(inside, axis=0, keepdims=True)
    dest = jnp.sum(inside * shift_c, axis=0, keepdims=True) + row0[0:1]
    cd_ref[0] = jnp.where(valid > 0.0, dest, -1.0).astype(jnp.int32)
    carry[...] = base + run


def _pos_call(eidx, pstart):
    T = eidx.shape[0]
    lst = jnp.asarray(np.tril(np.ones((TB, TB), np.float32), -1), BF16)
    ust = jnp.asarray(np.triu(np.ones((LANES, LANES), np.float32), 1), BF16)
    return pl.pallas_call(
        _pos_kernel,
        grid=(T // TB,),
        in_specs=[pl.BlockSpec((TB, LANES), lambda i: (i, 0)),
                  pl.BlockSpec((1, LANES), lambda i: (0, 0)),
                  pl.BlockSpec((TB, TB), lambda i: (0, 0)),
                  pl.BlockSpec((LANES, LANES), lambda i: (0, 0))],
        out_specs=[pl.BlockSpec((TB, LANES), lambda i: (i, 0)),
                   pl.BlockSpec((1, 1, CD_LANES), lambda i: (i, 0, 0))],
        out_shape=[jax.ShapeDtypeStruct((T, LANES), jnp.int32),
                   jax.ShapeDtypeStruct((T // TB, 1, CD_LANES), jnp.int32)],
        scratch_shapes=[pltpu.VMEM((1, LANES), F32)],
        compiler_params=_params("arbitrary"),
    )(eidx, pstart, lst, ust)


def _pair_matrix(lp, weights):
    j = lax.broadcasted_iota(jnp.int32, (TB, RL), 1)
    m = jnp.zeros((TB, RL), F32)
    for kk in range(TOP_K):
        w = 1.0 if weights is None else weights[:, kk:kk + 1]
        m = m + jnp.where(j == lp[:, kk:kk + 1], w, 0.0)
    return m


def _pack_bf16_pairs(v, is_bf16_valued=False):
    if not is_bf16_valued:
        v = v.astype(BF16).astype(F32)
    bits = pltpu.bitcast(v, jnp.uint32)
    return bits[:, D_HALF:] | (bits[:, :D_HALF] >> 16)


def _unpack_bf16_pairs(w):
    lo = pltpu.bitcast(w << 16, F32).astype(BF16)
    hi = pltpu.bitcast(w & jnp.uint32(0xFFFF0000), F32).astype(BF16)
    return lo, hi


N_SLOTS = 3


def _dispatch_kernel(fill_ref, cdp_ref, cdc_ref, h_ref, lp_ref, xg_ref, sorted_s, sems, fill_sem):
    i = pl.program_id(0)
    n = pl.num_programs(0)
    slot = lax.rem(i, N_SLOTS)
    prev = lax.rem(i + N_SLOTS - 1, N_SLOTS)
    pprev = lax.rem(i + N_SLOTS - 2, N_SLOTS)
    spill0 = xg_ref.shape[0] - N_SLOTS * RL

    def start(cref, s, c, to_spill, sem):
        src = RUN * c if isinstance(c, int) else pl.multiple_of(RUN * c, RUN)
        d = cref[0, 0, c]
        d = pl.multiple_of(jnp.where(jnp.logical_or(d < 0, to_spill), spill0 + s * RL + src, d), RUN)
        pltpu.make_async_copy(sorted_s.at[s, pl.ds(src, RUN), :], xg_ref.at[pl.ds(d, RUN), :], sem).start()

    def drain(s, sem):
        pltpu.make_async_copy(sorted_s.at[s], xg_ref.at[pl.ds(0, RL), :], sem).wait()

    @pl.when(i == 0)
    def _():
        sorted_s[...] = jnp.zeros_like(sorted_s)
        for s in range(N_SLOTS):
            lax.fori_loop(0, RL // RUN, lambda c, carry, s=s: (start(cdp_ref, s, c, True, fill_sem), carry)[1], 0)
        for s in range(N_SLOTS):
            drain(s, fill_sem)

    for c in range(RL // RUN):
        start(cdp_ref, prev, c, i == 0, sems.at[prev])

    sorted_s[slot] = _pack_bf16_pairs(_dot_tn(_pair_matrix(lp_ref[...], None).astype(BF16), h_ref[...]),
                                      is_bf16_valued=True)

    @pl.when(i == 0)
    def _():
        def fill(k):
            blk = pl.multiple_of(fill_ref[k] * MOE_BLOCK, MOE_BLOCK)
            return pltpu.make_async_copy(sorted_s.at[0, pl.ds(0, MOE_BLOCK), :],
                                         xg_ref.at[pl.ds(blk, MOE_BLOCK), :], fill_sem)

        for k in range(fill_ref.shape[0]):
            pl.when(fill_ref[k] >= 0)(lambda k=k: fill(k).start())
        for k in range(fill_ref.shape[0]):
            pl.when(fill_ref[k] >= 0)(lambda k=k: fill(k).wait())

    pl.when(i >= 1)(lambda: drain(pprev, sems.at[pprev]))

    @pl.when(i == n - 1)
    def _():
        lax.fori_loop(0, RL // RUN, lambda c, carry: (start(cdc_ref, slot, c, False, sems.at[slot]), carry)[1], 0,
                      unroll=4)
        drain(prev, sems.at[prev])
        drain(slot, sems.at[slot])


def _dispatch_call(fill_blocks, cd, h2, lp, P):
    T, d = h2.shape
    grid_spec = pltpu.PrefetchScalarGridSpec(
        num_scalar_prefetch=1,
        grid=(T // TB,),
        in_specs=[pl.BlockSpec((1, 1, CD_LANES), lambda i, fb: (jnp.maximum(i - 1, 0), 0, 0),
                               memory_space=pltpu.SMEM),
                  pl.BlockSpec((1, 1, CD_LANES), lambda i, fb: (i, 0, 0), memory_space=pltpu.SMEM),
                  pl.BlockSpec((TB, d), lambda i, fb: (i, 0)),
                  pl.BlockSpec((TB, LANES), lambda i, fb: (i, 0))],
        out_specs=pl.BlockSpec(memory_space=pl.ANY),
        scratch_shapes=[pltpu.VMEM((N_SLOTS, RL, D_HALF), jnp.uint32), pltpu.SemaphoreType.DMA((N_SLOTS,)),
                        pltpu.SemaphoreType.DMA(())],
    )
    return pl.pallas_call(
        _dispatch_kernel,
        grid_spec=grid_spec,
        out_shape=jax.ShapeDtypeStruct((P + N_SLOTS * RL, D_HALF), jnp.uint32),
        compiler_params=_params("arbitrary"),
    )(fill_blocks, cd, cd, h2, lp)


def _moe_kernel(nb, info_ref, x_hbm, wgu_hbm, bgu_ref, wd_hbm, bd_ref, y_hbm,
                wgu_f, wd_f, wgu_b, wd_b, act_s, xbuf, xb_s, yacc, ybuf, zbuf, wsems, xsems, ysems, zsem):
    n_used = info_ref[0, nb]

    def rows(blk):
        return pl.ds(pl.multiple_of(blk * MOE_BLOCK, MOE_BLOCK), MOE_BLOCK)

    def x_copy(blk, slot):
        return pltpu.make_async_copy(x_hbm.at[rows(blk), :], xbuf.at[slot], xsems.at[slot])

    def y_copy(blk, slot):
        return pltpu.make_async_copy(ybuf.at[slot], y_hbm.at[rows(blk), :], ysems.at[slot])

    def z_copy(blk):
        return pltpu.make_async_copy(zbuf, y_hbm.at[rows(blk), :], zsem)

    def fetch(expert, slot):
        return (pltpu.make_async_copy(wgu_hbm.at[expert], wgu_f.at[slot], wsems.at[0, slot]),
                pltpu.make_async_copy(wd_hbm.at[expert], wd_f.at[slot], wsems.at[1, slot]))

    zbuf[...] = jnp.zeros_like(zbuf)
    lax.fori_loop(n_used, nb, lambda blk, c: (z_copy(blk).start(), c)[1], 0)

    def unpack(src_slot, dst_slot):
        lo_half, hi_half = _unpack_bf16_pairs(xbuf[src_slot])
        xb_s[dst_slot, :, 0:D_HALF] = lo_half
        xb_s[dst_slot, :, D_HALF:2 * D_HALF] = hi_half

    def pack(expert, dst_slot):
        ybuf[dst_slot] = _pack_bf16_pairs(yacc[...] + bd_ref[pl.ds(expert, 1), :])

    xbuf[...] = jnp.zeros_like(xbuf)
    yacc[...] = jnp.zeros_like(yacc)
    for cp in fetch(info_ref[0, 0], 0):
        cp.start()
    x_copy(0, 0).start()
    x_copy(0, 0).wait()
    unpack(0, 0)
    pl.when(n_used > 1)(lambda: x_copy(1, 1).start())

    def body(i, carry):
        slot = i & 1
        other = 1 - slot
        e = info_ref[0, i]

        @pl.when(info_ref[1, i] == 1)
        def _():
            wslot = info_ref[2, i]
            nxt = info_ref[3, i]
            cps = fetch(e, wslot)
            cps[0].wait()
            wgu_b[...] = wgu_f[wslot].astype(BF16)
            cps[1].wait()
            wd_b[...] = wd_f[wslot].astype(BF16)

            @pl.when(nxt >= 0)
            def _():
                for cp in fetch(nxt, 1 - wslot):
                    cp.start()

        pl.when(i + 1 < n_used)(lambda: x_copy(i + 1, other).wait())
        pl.when(i + 2 < n_used)(lambda: x_copy(i + 2, slot).start())
        pl.when(i >= 3)(lambda: y_copy(i - 3, other).wait())

        bgu = bgu_ref[pl.ds(e, 1), :]
        cw = 256
        for c in range(D_FF // cw):
            lo, hi = c * cw, (c + 1) * cw
            xb = xb_s[slot]
            gate = _dot(xb, wgu_b[:, lo:hi]) + bgu[:, lo:hi]
            up = _dot(xb, wgu_b[:, D_FF + lo:D_FF + hi]) + bgu[:, D_FF + lo:D_FF + hi]
            gate = jnp.minimum(gate, SWIGLU_LIMIT)
            up = jnp.clip(up, -SWIGLU_LIMIT, SWIGLU_LIMIT)
            act_s[:, lo:hi] = ((up + 1.0) * (gate * _sigmoid(SWIGLU_ALPHA * gate))).astype(BF16)
            if c == 0:
                pack(info_ref[0, jnp.maximum(i - 1, 0)], other)
            if c == 1:
                unpack(other, other)
        yacc[...] = _dot(act_s[...], wd_b[...])
        pl.when(i >= 1)(lambda: y_copy(i - 1, other).start())
        return carry

    lax.fori_loop(0, n_used, body, 0)

    last = (n_used + 1) & 1
    pl.when(n_used >= 3)(lambda: y_copy(n_used - 3, last).wait())
    pack(info_ref[0, n_used - 1], last)
    y_copy(n_used - 1, last).start()
    pl.when(n_used >= 2)(lambda: y_copy(n_used - 2, 1 - last).wait())
    y_copy(n_used - 1, last).wait()
    lax.fori_loop(n_used, nb, lambda blk, c: (z_copy(blk).wait(), c)[1], 0)


def _moe_call(info, xg, wgu, bgu, wd, bd, nb):
    P = nb * MOE_BLOCK
    ne, d, f2 = wgu.shape
    blk_buf = pltpu.VMEM((2, MOE_BLOCK, D_HALF), jnp.uint32)
    grid_spec = pltpu.PrefetchScalarGridSpec(
        num_scalar_prefetch=1,
        grid=(1,),
        in_specs=[pl.BlockSpec(memory_space=pl.ANY),
                  pl.BlockSpec(memory_space=pl.ANY),
                  pl.BlockSpec((ne, f2), lambda i, info: (0, 0)),
                  pl.BlockSpec(memory_space=pl.ANY),
                  pl.BlockSpec((ne, d), lambda i, info: (0, 0))],
        out_specs=pl.BlockSpec(memory_space=pl.ANY),
        scratch_shapes=[pltpu.VMEM((2, d, f2), F32), pltpu.VMEM((2, f2 // 2, d), F32),
                        pltpu.VMEM((d, f2), BF16), pltpu.VMEM((f2 // 2, d), BF16),
                        pltpu.VMEM((MOE_BLOCK, f2 // 2), BF16), blk_buf,
                        pltpu.VMEM((2, MOE_BLOCK, d), BF16), pltpu.VMEM((MOE_BLOCK, d), F32), blk_buf,
                        pltpu.VMEM((MOE_BLOCK, D_HALF), jnp.uint32),
                        pltpu.SemaphoreType.DMA((2, 2)), pltpu.SemaphoreType.DMA((2,)),
                        pltpu.SemaphoreType.DMA((2,)), pltpu.SemaphoreType.DMA(())],
    )
    return pl.pallas_call(
        functools.partial(_moe_kernel, nb),
        grid_spec=grid_spec,
        out_shape=jax.ShapeDtypeStruct((P, D_HALF), jnp.uint32),
        compiler_params=_params("arbitrary"),
    )(info, xg, wgu, bgu, wd, bd)


def _combine_kernel(d0_ref, d1_ref, d2_ref, y_ref, lp_ref, x1_ref, gate_ref, g2_ref, fn_ref, o_ref, buf, sems):
    i = pl.program_id(0)
    n = pl.num_programs(0)
    slot = lax.rem(i, N_SLOTS)
    ahead = lax.rem(i + 2, N_SLOTS)

    def start(dref, s, c):
        d = pl.multiple_of(jnp.maximum(dref[0, 0, c], 0), RUN)
        dst = pl.multiple_of(RUN * c, RUN)
        pltpu.make_async_copy(y_ref.at[pl.ds(d, RUN), :], buf.at[s, pl.ds(dst, RUN), :], sems.at[s]).start()

    def drain(s):
        pltpu.make_async_copy(y_ref.at[pl.ds(0, RL), :], buf.at[s], sems.at[s]).wait()

    @pl.when(i == 0)
    def _():
        lax.fori_loop(0, RL // RUN, lambda c, carry: (start(d0_ref, 0, c), carry)[1], 0, unroll=4)
        lax.fori_loop(0, RL // RUN, lambda c, carry: (start(d1_ref, 1, c), carry)[1], 0, unroll=4)

    drain(slot)

    for c in range(RL // RUN):
        start(d2_ref, ahead, c)

    g = _pair_matrix(lp_ref[...], gate_ref[...]).astype(BF16)
    halves = [_dot(g, yb) for yb in _unpack_bf16_pairs(buf[slot])]
    xo = x1_ref[...] + g2_ref[0] * jnp.concatenate(halves, axis=1)
    ms = jnp.mean(xo * xo, axis=-1, keepdims=True)
    o_ref[...] = xo * lax.rsqrt(ms + EPS) * fn_ref[...]

    @pl.when(i == n - 1)
    def _():
        drain(lax.rem(i + 1, N_SLOTS))
        drain(ahead)


def _combine_call(cd, y, lp, x1, gates, g2, fn):
    T, d = x1.shape
    n = T // TB
    per_batch = n // g2.shape[0]
    return pl.pallas_call(
        _combine_kernel,
        grid=(n,),
        in_specs=[pl.BlockSpec((1, 1, CD_LANES), lambda i: (i, 0, 0), memory_space=pltpu.SMEM),
                  pl.BlockSpec((1, 1, CD_LANES), lambda i: (jnp.minimum(i + 1, n - 1), 0, 0),
                               memory_space=pltpu.SMEM),
                  pl.BlockSpec((1, 1, CD_LANES), lambda i: (jnp.minimum(i + 2, n - 1), 0, 0),
                               memory_space=pltpu.SMEM),
                  pl.BlockSpec(memory_space=pl.ANY),
                  pl.BlockSpec((TB, LANES), lambda i: (i, 0)),
                  pl.BlockSpec((TB, d), lambda i: (i, 0)),
                  pl.BlockSpec((TB, LANES), lambda i: (i, 0)),
                  pl.BlockSpec((1, 1, d), lambda i: (i // per_batch, 0, 0)),
                  pl.BlockSpec((1, d), lambda i: (0, 0))],
        out_specs=pl.BlockSpec((TB, d), lambda i: (i, 0)),
        out_shape=jax.ShapeDtypeStruct((T, d), F32),
        scratch_shapes=[pltpu.VMEM((N_SLOTS, RL, D_HALF), jnp.uint32), pltpu.SemaphoreType.DMA((N_SLOTS,))],
        compiler_params=_params("arbitrary"),
    )(cd, cd, cd, y, lp, x1, gates, g2, fn)


def _layer(x, c, ctx, c_ctx, w_mod, b_mod, norm1, w_in, gla_w_gk_up, gla_b_gk, gla_norm,
           ssd_conv_w, ssd_conv_b, ssd_dt_bias, ssd_A_log, ssd_D, ssd_norm, w_out,
           norm2, w_router, b_router, w_gate_up, b_gate_up, w_down, b_down, final_norm):
    bsz, L, d = x.shape
    lc = ctx.shape[1]
    assert lc == TB and L % TB == 0 and TB % GRID_W == 0

    cin = jnp.zeros((8, d), F32).at[:bsz].set(c).at[bsz].set(c_ctx)
    mod = _mod_call(cin, w_mod, b_mod.reshape(1, -1))[:bsz + 1]
    sh1, sc1, g1, sh2, sc2, g2 = [m.reshape(bsz + 1, 1, d) for m in jnp.split(mod, 6, axis=-1)]

    o = np.cumsum((0, GLA_QK, GLA_QK, GLA_V, GLA_V, GLA_RANK, SSD_INNER, SSD_CONV_DIM, SSD_HEADS))
    wq, wk, wv, wg, wlow, wz, wx, wdt = [w_in[:, int(a):int(b)] for a, b in zip(o[:-1], o[1:])]
    w_misc = jnp.concatenate([wlow, wdt, wdt, jnp.zeros((d, LANES - GLA_RANK - 2 * SSD_HEADS), F32)], axis=1)
    w_cat = jnp.concatenate([wq, wk, wv, wg, wz, wx, w_misc], axis=1).astype(BF16)
    wup = jnp.zeros((LANES, 2 * GLA_QK), F32).at[:GLA_RANK].set(
        jnp.concatenate([gla_w_gk_up[0], gla_w_gk_up[1]], axis=1)).astype(BF16)
    bup = jnp.concatenate([gla_b_gk[0], gla_b_gk[1]]).reshape(1, -1)
    dtb = jnp.zeros((1, LANES), F32).at[0, DT_F:DT_F + SSD_HEADS].set(ssd_dt_bias[0]) \
                                    .at[0, DT_B:DT_B + SSD_HEADS].set(ssd_dt_bias[1])
    q, k, v, g_all, z_all, xbc, ld, misc = _inproj_call(
        x, ctx, sh1, sc1, norm1.reshape(1, d), w_cat, wup, bup, dtb)

    xs, bc = _conv_call(xbc, ssd_conv_w.reshape(9, SSD_CONV_DIM), ssd_conv_b.reshape(1, -1), lc + L)

    o_f, o_b = _gla_call(q, k, v, ld, L)

    a_neg = -jnp.exp(ssd_A_log.astype(F32))
    a_f = jnp.zeros((1, LANES), F32).at[0, DT_F:DT_F + SSD_HEADS].set(a_neg[0])
    a_b = jnp.zeros((1, LANES), F32).at[0, DT_B:DT_B + SSD_HEADS].set(a_neg[1])
    dvec = jnp.repeat(ssd_D, SSD_HEADDIM).reshape(1, SSD_INNER)
    y_f, y_b = _ssd_call(xs, bc, misc, a_f, a_b, dvec, L)

    wr = jnp.zeros((d, LANES), F32).at[:, :N_EXPERTS].set(w_router)
    wr_hi = wr.astype(BF16)
    wr_lo = (wr - wr_hi.astype(F32)).astype(BF16)
    br = jnp.full((1, LANES), NEG_BIG, F32).at[0, :N_EXPERTS].set(b_router)
    x1, h2, eidx, gates, counts = _outproj_call(
        o_f, o_b, g_all, y_f, y_b, z_all, x, g1[:bsz], sh2[:bsz], sc2[:bsz],
        jnp.tile(gla_norm, GLA_HEADS).reshape(1, -1), ssd_norm.reshape(1, -1), w_out.astype(BF16),
        norm2.reshape(1, d), wr_hi, wr_lo, br)

    T = bsz * L
    cnt = counts[0, :N_EXPERTS].astype(jnp.int32)
    padded = ((cnt + MOE_BLOCK - 1) // MOE_BLOCK) * MOE_BLOCK
    pend = jnp.cumsum(padded)
    pstart = pend - padded
    max_rows = T * TOP_K + (T // TB) * N_EXPERTS * (RUN - 1)
    n_blocks = -(-max_rows // MOE_BLOCK) + N_EXPERTS
    blk_start = jnp.arange(n_blocks, dtype=jnp.int32) * MOE_BLOCK
    blk_e = jnp.minimum(jnp.sum(pend[None, :] <= blk_start[:, None], axis=1), N_EXPERTS - 1).astype(jnp.int32)
    n_used = (pend[-1] // MOE_BLOCK).astype(jnp.int32)
    blk_i = jnp.arange(n_blocks, dtype=jnp.int32)
    first = (blk_i < n_used) & ((blk_i == 0) | (blk_e != jnp.roll(blk_e, 1)))
    slot = (jnp.cumsum(first) - 1) & 1
    first_pos = jnp.where(first, blk_i, n_blocks)
    next_first = jnp.roll(lax.cummin(first_pos, reverse=True), -1).at[-1].set(n_blocks)
    nxt = jnp.where(next_first < n_blocks, blk_e[jnp.minimum(next_first, n_blocks - 1)], -1)
    col = lambda v, last: jnp.concatenate([v.astype(jnp.int32), jnp.asarray([last], jnp.int32)])
    blk_info = jnp.stack([col(blk_e, 0).at[-1].set(n_used), col(first, 0), col(slot, 0), col(nxt, -1)])
    n_tail = n_blocks - (T * TOP_K) // MOE_BLOCK
    tail = n_used + jnp.arange(n_tail, dtype=jnp.int32)
    fill_blocks = jnp.concatenate([
        jnp.where(padded > 0, pend // MOE_BLOCK - 1, -1),
        jnp.where(tail < n_blocks, tail, -1)]).astype(jnp.int32)
    pstart_row = jnp.zeros((1, LANES), F32).at[0, :N_EXPERTS].set(pstart.astype(F32))
    lp, cd = _pos_call(eidx.reshape(T, LANES), pstart_row)

    xg = _dispatch_call(fill_blocks, cd, h2.reshape(T, d), lp, n_blocks * MOE_BLOCK)
    y = _moe_call(blk_info, xg, w_gate_up, b_gate_up, w_down, b_down, n_blocks)
    out = _combine_call(cd, y, lp, x1.reshape(T, d), gates.reshape(T, LANES), g2[:bsz], final_norm.reshape(1, d))
    return out.reshape(bsz, L, d)


def kernel(x, c, ctx, c_ctx, w_mod, b_mod, norm1, w_in, gla_w_gk_up, gla_b_gk, gla_norm, ssd_conv_w, ssd_conv_b, ssd_dt_bias, ssd_A_log, ssd_D, ssd_norm, w_out, norm2, w_router, b_router, w_gate_up, b_gate_up, w_down, b_down, final_norm):
    assert w_mod.shape[0] == 1, "single-layer kernel"
    return _layer(x, c, ctx, c_ctx, w_mod[0], b_mod[0], norm1[0], w_in[0], gla_w_gk_up[0], gla_b_gk[0],
                  gla_norm[0], ssd_conv_w[0], ssd_conv_b[0], ssd_dt_bias[0], ssd_A_log[0], ssd_D[0],
                  ssd_norm[0], w_out[0], norm2[0], w_router[0], b_router[0], w_gate_up[0], b_gate_up[0],
                  w_down[0], b_down[0], final_norm)
```

```python
import functools

import numpy as np
import jax
import jax.numpy as jnp
from jax import lax
from jax.experimental import pallas as pl
from jax.experimental.pallas import tpu as pltpu

F32 = jnp.float32
BF16 = jnp.bfloat16

EPS = 1e-6
GRID_W = 64
GLA_HEADS = 4
GLA_DK = 64
GLA_DV = 128
GLA_QK = GLA_HEADS * GLA_DK
GLA_V = GLA_HEADS * GLA_DV
GLA_RANK = 16
GLA_GATE_NORM = 16.0
SSD_HEADDIM = 64
SSD_INNER = 512
SSD_HEADS = 8
SSD_GROUPS = 2
SSD_HPG = 4
SSD_STATE = 128
SSD_CONV_DIM = 1024
N_EXPERTS = 32
TOP_K = 4
D_FF = 1024
SWIGLU_LIMIT = 7.0
SWIGLU_ALPHA = 1.702
MOE_BLOCK = 512

TB = 256
GLA_C = 64
GLA_STAGE_GROUP = 4
SSD_C = 128
SSD_STAGE_GROUP = 1
LANES = 128
EXP_CLAMP = 80.0
DT_F = 16
DT_B = 24
NEG_BIG = -1e30
RUN = 8
RL = 1280
CD_LANES = 256
SPILL_ROWS = -(-RL // MOE_BLOCK) * MOE_BLOCK
D_HALF = 512
VMEM_LIMIT = 56 * 1024 * 1024


def _dot(a, b):
    return jnp.dot(a, b, preferred_element_type=F32)


def _dot_nt(a, b):
    return lax.dot_general(a, b, (((1,), (1,)), ((), ())), preferred_element_type=F32)


def _dot_tn(a, b):
    return lax.dot_general(a, b, (((0,), (0,)), ((), ())), preferred_element_type=F32)


def _split3(a):
    hi = a.astype(BF16)
    r1 = a - hi.astype(F32)
    mid = r1.astype(BF16)
    lo = (r1 - mid.astype(F32)).astype(BF16)
    return hi, mid, lo


def _dot_exact_r(m, a):
    hi, mid, lo = _split3(a)
    return _dot(m, hi) + _dot(m, mid) + _dot(m, lo)


def _dot_hilo_r(m, a):
    hi = a.astype(BF16)
    lo = (a - hi.astype(F32)).astype(BF16)
    return _dot(m, hi) + _dot(m, lo)


def _dot_hilo_l(a, m2):
    hi = a.astype(BF16)
    lo = (a - hi.astype(F32)).astype(BF16)
    return _dot(jnp.concatenate([hi, lo], axis=1), m2)


def _sigmoid(x):
    return 1.0 / (1.0 + jnp.exp(-x))


def _softplus(x):
    return jnp.maximum(x, 0.0) + jnp.log1p(jnp.exp(-jnp.abs(x)))


def _params(*sem):
    return pltpu.CompilerParams(dimension_semantics=sem, vmem_limit_bytes=VMEM_LIMIT)


def _mod_kernel(c_ref, w_ref, b_ref, o_ref):
    c = c_ref[...]
    s = c * _sigmoid(c)
    s_hi = s.astype(BF16)
    s_lo = (s - s_hi.astype(F32)).astype(BF16)
    w = w_ref[...]
    w_hi = w.astype(BF16)
    w_lo = (w - w_hi.astype(F32)).astype(BF16)
    o_ref[...] = _dot(s_hi, w_hi) + _dot(s_lo, w_hi) + _dot(s_hi, w_lo) + b_ref[...]


def _mod_call(cin, w, b):
    rows, d = cin.shape
    n = w.shape[1]
    tn = 1536
    return pl.pallas_call(
        _mod_kernel,
        grid=(n // tn,),
        in_specs=[pl.BlockSpec((rows, d), lambda i: (0, 0)),
                  pl.BlockSpec((d, tn), lambda i: (0, i)),
                  pl.BlockSpec((1, tn), lambda i: (0, i))],
        out_specs=pl.BlockSpec((rows, tn), lambda i: (0, i)),
        out_shape=jax.ShapeDtypeStruct((rows, n), F32),
        compiler_params=_params("arbitrary"),
    )(cin, w, b)


_C_Q, _C_K, _C_V, _C_G, _C_Z, _C_X, _C_M, _C_END = 0, 256, 512, 1024, 1536, 2048, 3072, 3200


def _inproj_kernel(x0_ref, xa_ref, xb_ref, ctx_ref, shl_ref, scl_ref, shc_ref, scc_ref, n1_ref, w_ref, wup_ref,
                   bup_ref, dtb_ref, q_ref, k_ref, v_ref, g_ref, z_ref, xbc_ref, ld_ref, misc_ref, h_scr):
    j = pl.program_id(1)
    slot = j & 1

    def normmod(xv, sh_ref, sc_ref):
        ms = jnp.mean(xv * xv, axis=-1, keepdims=True)
        y = xv * lax.rsqrt(ms + EPS) * n1_ref[...]
        return (y * (1.0 + sc_ref[0]) + sh_ref[0]).astype(BF16)

    @pl.when(j == 0)
    def _():
        h_scr[0, 0:TB, :] = normmod(ctx_ref[0], shc_ref, scc_ref)
        h_scr[0, TB:2 * TB, :] = normmod(x0_ref[0], shl_ref, scl_ref)

    def mm(lo, hi):
        return _dot(h_scr[slot], w_ref[:, lo:hi])

    m = mm(_C_M, _C_END)
    zz = _dot(m.astype(BF16), wup_ref[...]) + bup_ref[...]
    ld_ref[0] = -_softplus(-zz) * (1.0 / GLA_GATE_NORM)
    misc_ref[0] = _softplus(m + dtb_ref[...])
    q_ref[0] = (mm(_C_Q, _C_K) * (GLA_DK ** -0.5)).astype(BF16)
    h_scr[1 - slot, 0:TB, :] = normmod(xa_ref[0], shl_ref, scl_ref)
    k_ref[0] = mm(_C_K, _C_V).astype(BF16)
    h_scr[1 - slot, TB:2 * TB, :] = normmod(xb_ref[0], shl_ref, scl_ref)
    v_ref[0] = mm(_C_V, _C_G).astype(BF16)
    g_ref[0] = mm(_C_G, _C_Z)
    z_ref[0] = mm(_C_Z, _C_X)
    xbc_ref[0] = mm(_C_X, _C_M)


def _inproj_call(x, ctx, sh1, sc1, n1, w_cat, wup, bup, dtb):
    bsz, L, d = x.shape
    nx = L // TB
    nj = -(-(nx + 1) // 2)
    tok = lambda n: pl.BlockSpec((1, 2 * TB, n), lambda b, j: (b, j, 0))
    const = lambda a: pl.BlockSpec(a.shape, lambda b, j: (0,) * a.ndim)
    xblk = lambda off: pl.BlockSpec((1, TB, d), lambda b, j: (b, jnp.minimum(2 * j + off, nx - 1), 0))
    first = pl.BlockSpec((1, TB, d), lambda b, j: (b, 0, 0))
    mod_lat = pl.BlockSpec((1, 1, d), lambda b, j: (b, 0, 0))
    mod_ctx = pl.BlockSpec((1, 1, d), lambda b, j: (bsz, 0, 0))
    outs = [(GLA_QK, BF16), (GLA_QK, BF16), (GLA_V, BF16), (GLA_V, F32), (SSD_INNER, F32),
            (SSD_CONV_DIM, F32), (2 * GLA_QK, F32), (LANES, F32)]
    return pl.pallas_call(
        _inproj_kernel,
        grid=(bsz, nj),
        in_specs=[first, xblk(1), xblk(2), first,
                  mod_lat, mod_lat, mod_ctx, mod_ctx,
                  const(n1), const(w_cat), const(wup), const(bup), const(dtb)],
        out_specs=[tok(n) for n, _ in outs],
        out_shape=[jax.ShapeDtypeStruct((bsz, nj * 2 * TB, n), dt) for n, dt in outs],
        scratch_shapes=[pltpu.VMEM((2, 2 * TB, d), BF16)],
        compiler_params=_params("arbitrary", "arbitrary"),
    )(x, x, x, ctx, sh1, sc1, sh1, sc1, n1, w_cat, wup, bup, dtb)


_EXT_PAD = 8
_EXT_BASE = _EXT_PAD + GRID_W
_EXT_ROWS = 2 * _EXT_PAD + 2 * GRID_W + TB


def _conv_kernel(prev_ref, cur_ref, next_ref, w_ref, b_ref, xs_ref, bc_ref, ext, xl_s, xr_s):
    j = pl.program_id(1)
    nj = pl.num_programs(1)
    is_ctx = j == 0
    zpad = jnp.zeros((_EXT_PAD, SSD_CONV_DIM), F32)
    ext[0:_EXT_PAD, :] = zpad
    ext[_EXT_ROWS - _EXT_PAD:_EXT_ROWS, :] = zpad
    ext[_EXT_PAD:_EXT_BASE, :] = jnp.where(j >= 2, prev_ref[0], 0.0)
    ext[_EXT_BASE:_EXT_BASE + TB, :] = cur_ref[0]
    ext[_EXT_BASE + TB:_EXT_BASE + TB + GRID_W, :] = jnp.where(
        jnp.logical_and(j >= 1, j <= nj - 2), next_ref[0], 0.0)

    win = TB + 2 * GRID_W
    u = lax.broadcasted_iota(jnp.int32, (win, LANES), 0)
    pos = jnp.where(is_ctx, u - GRID_W, u & (GRID_W - 1))
    ok_l = pos >= 1
    ok_r = jnp.where(is_ctx, pos - (TB - GRID_W), pos) <= GRID_W - 2
    lat = jnp.where(is_ctx, 0.0, 1.0)
    side = {-1: xl_s, 0: None, 1: xr_s}

    for c in range(SSD_CONV_DIM // LANES):
        lo, hi = c * LANES, (c + 1) * LANES
        xl_s[:, lo:hi] = jnp.where(ok_l, ext[_EXT_PAD - 1:_EXT_PAD - 1 + win, lo:hi], 0.0)
        xr_s[:, lo:hi] = jnp.where(ok_r, ext[_EXT_PAD + 1:_EXT_PAD + 1 + win, lo:hi], 0.0)
        acc = jnp.zeros((TB, LANES), F32)
        for dr in (-1, 0, 1):
            for dc in (-1, 0, 1):
                if dc == 0:
                    start = _EXT_BASE + GRID_W * dr
                    tap = ext[start:start + TB, lo:hi]
                else:
                    start = GRID_W + GRID_W * dr
                    tap = side[dc][start:start + TB, lo:hi]
                wi = 3 * (dr + 1) + (dc + 1)
                wv = w_ref[wi:wi + 1, lo:hi]
                if dr != 0:
                    wv = wv * lat
                acc = acc + tap * wv
        y = acc + b_ref[:, lo:hi]
        y = y * _sigmoid(y)
        if c < SSD_INNER // LANES:
            xs_ref[0, :, lo:hi] = y
        else:
            bc_ref[0, :, lo - SSD_INNER:hi - SSD_INNER] = y.astype(BF16)


def _conv_call(xbc, w9, bias, ls):
    bsz, _, ch = xbc.shape
    nj = ls // TB
    rpb = TB // GRID_W
    nrow = ls // GRID_W
    return pl.pallas_call(
        _conv_kernel,
        grid=(bsz, nj),
        in_specs=[pl.BlockSpec((1, GRID_W, ch), lambda b, j: (b, jnp.maximum(rpb * j - 1, 0), 0)),
                  pl.BlockSpec((1, TB, ch), lambda b, j: (b, j, 0)),
                  pl.BlockSpec((1, GRID_W, ch), lambda b, j: (b, jnp.minimum(rpb * j + rpb, nrow - 1), 0)),
                  pl.BlockSpec((9, ch), lambda b, j: (0, 0)),
                  pl.BlockSpec((1, ch), lambda b, j: (0, 0))],
        out_specs=[pl.BlockSpec((1, TB, SSD_INNER), lambda b, j: (b, j, 0)),
                   pl.BlockSpec((1, TB, ch - SSD_INNER), lambda b, j: (b, j, 0))],
        out_shape=[jax.ShapeDtypeStruct((bsz, ls, SSD_INNER), F32),
                   jax.ShapeDtypeStruct((bsz, ls, ch - SSD_INNER), BF16)],
        scratch_shapes=[pltpu.VMEM((_EXT_ROWS, ch), F32), pltpu.VMEM((TB + 2 * GRID_W, ch), F32),
                        pltpu.VMEM((TB + 2 * GRID_W, ch), F32)],
        compiler_params=_params("arbitrary", "arbitrary"),
    )(xbc, xbc, xbc, w9, bias)


def _fwd_blk(s):
    return s


def _bwd_blk(s, ns):
    return jnp.where(s == 0, 0, ns - s)


def _scan_batch(bsz):
    return 2 if bsz % 2 == 0 else 1


class _GlaChunk:
    def __init__(self, q, k, v, la, st_ref, tri_m, fwd, store):
        self.q, self.k, self.v, self.la, self.st_ref, self.tri_m, self.fwd, self.store = (
            q, k, v, la, st_ref, tri_m, fwd, store)

    def stage_sums(self):
        self.b = _dot_hilo_r(self.tri_m, self.la)

    def stage_factors(self):
        C = GLA_C
        b = self.b
        self.bt = b[C - 1:C, :] if self.fwd else b[0:1, :]
        r = 0.5 * self.bt
        self.er = jnp.exp(r)
        self.qt = (self.q.astype(F32) * jnp.exp(jnp.minimum(b - r, EXP_CLAMP))).astype(BF16)
        kt = (self.k.astype(F32) * jnp.exp(jnp.minimum(r - b, EXP_CLAMP))).astype(BF16)
        head_k = lax.broadcasted_iota(jnp.int32, (C, GLA_QK), 1) >> 6
        zero = jnp.zeros_like(kt)
        self.kh = [jnp.where(head_k == h, kt, zero) for h in range(GLA_HEADS)]
        self.qh = [jnp.where(head_k == h, self.qt, zero) for h in range(GLA_HEADS)]

    def stage_products(self):
        v = self.v
        kcat = jnp.concatenate(self.kh, axis=0)
        self.sc = _dot_nt(self.qt, kcat)
        vcat = jnp.concatenate([v[:, h * GLA_DV:(h + 1) * GLA_DV] for h in range(GLA_HEADS)], axis=0)
        self.u = _dot_tn(vcat, kcat) * self.er

    def stage_mask(self):
        C = GLA_C
        ii = lax.broadcasted_iota(jnp.int32, (C, GLA_HEADS * C), 0)
        jj = lax.broadcasted_iota(jnp.int32, (C, GLA_HEADS * C), 1) & (C - 1)
        causal = (jj <= ii) if self.fwd else (jj >= ii)
        self.p = jnp.where(causal, self.sc, 0.0).astype(BF16)
        v = self.v
        head_v = lax.broadcasted_iota(jnp.int32, (C, GLA_V), 1) >> 7
        self.vst = jnp.concatenate([jnp.where(head_v == h, v, jnp.zeros_like(v)) for h in range(GLA_HEADS)], axis=0)

    def stage_intra(self):
        self.o = _dot(self.p, self.vst)

    def stage_inter(self):
        st = self.st_ref[...]
        ster = (st * self.er).astype(BF16)
        self.st_ref[...] = st * jnp.exp(self.bt) + self.u
        res = _dot_nt(jnp.concatenate(self.qh, axis=0), ster)
        self.inter = jnp.concatenate([res[h * GLA_C:(h + 1) * GLA_C, :] for h in range(GLA_HEADS)], axis=1)

    def stage_out(self):
        self.store(self.o + self.inter)


def _gla_kernel(qf_ref, kf_ref, vf_ref, lf_ref, qb_ref, kb_ref, vb_ref, lb_ref, trif_ref, trib_ref,
                of_ref, ob_ref, stf, stb):
    s = pl.program_id(1)

    @pl.when(s == 0)
    def _():
        stf[...] = jnp.zeros_like(stf)
        stb[...] = jnp.zeros_like(stb)

    nsub = TB // GLA_C

    def store_to(ref, bb, sl):
        def store(val):
            ref[bb, sl, :] = val
        return store

    group = GLA_STAGE_GROUP
    for g0 in range(0, nsub, group):
        steps = []
        for i in range(g0, g0 + group):
            sf = pl.ds(i * GLA_C, GLA_C)
            sb = pl.ds((nsub - 1 - i) * GLA_C, GLA_C)
            chunks = []
            for bb in range(qf_ref.shape[0]):
                chunks.append(_GlaChunk(qf_ref[bb, sf, :], kf_ref[bb, sf, :], vf_ref[bb, sf, :], lf_ref[bb, sf, :],
                                        stf.at[bb], trif_ref[...], True, store_to(of_ref, bb, sf)))
                chunks.append(_GlaChunk(qb_ref[bb, sb, :], kb_ref[bb, sb, :], vb_ref[bb, sb, :], lb_ref[bb, sb, :],
                                        stb.at[bb], trib_ref[...], False, store_to(ob_ref, bb, sb)))
            steps.append(chunks)
        for stage in ("stage_sums", "stage_factors", "stage_products", "stage_mask", "stage_intra"):
            for chunks in steps:
                for ch in chunks:
                    getattr(ch, stage)()
        for chunks in steps:
            for stage in ("stage_inter", "stage_out"):
                for ch in chunks:
                    getattr(ch, stage)()


def _gla_call(q, k, v, ld, L):
    bsz = q.shape[0]
    nx = L // TB
    ns = nx + 1
    trif = jnp.asarray(np.tril(np.ones((GLA_C, GLA_C), np.float32)), BF16)
    trib = jnp.asarray(np.triu(np.ones((GLA_C, GLA_C), np.float32)), BF16)
    nb = _scan_batch(bsz)
    f = lambda n, lane=0: pl.BlockSpec((nb, TB, n), lambda b, s: (b, _fwd_blk(s), lane))
    r = lambda n, lane=0: pl.BlockSpec((nb, TB, n), lambda b, s: (b, _bwd_blk(s, ns), lane))
    tri = pl.BlockSpec((GLA_C, GLA_C), lambda b, s: (0, 0))
    return pl.pallas_call(
        _gla_kernel,
        grid=(bsz // nb, ns),
        in_specs=[f(GLA_QK), f(GLA_QK), f(GLA_V), f(GLA_QK, 0),
                  r(GLA_QK), r(GLA_QK), r(GLA_V), r(GLA_QK, 1), tri, tri],
        out_specs=[pl.BlockSpec((nb, TB, GLA_V), lambda b, s: (b, jnp.maximum(s - 1, 0), 0)),
                   pl.BlockSpec((nb, TB, GLA_V), lambda b, s: (b, jnp.where(s == 0, nx - 1, nx - s), 0))],
        out_shape=[jax.ShapeDtypeStruct((bsz, L, GLA_V), F32)] * 2,
        scratch_shapes=[pltpu.VMEM((nb, GLA_DV, GLA_QK), F32)] * 2,
        compiler_params=_params("arbitrary", "arbitrary"),
    )(q, k, v, ld, q, k, v, ld, trif, trib)


class _SsdChunk:
    def __init__(self, xs, bc, dtm, avec, dvec, st_ref, tri_m, e_m, base, fwd, store):
        self.xs, self.bc, self.dtm, self.avec, self.dvec, self.st_ref = xs, bc, dtm, avec, dvec, st_ref
        self.tri_m, self.e_m, self.base, self.fwd, self.store = tri_m, e_m, base, fwd, store

    def stage_sums(self):
        self.dt_exp = _dot_hilo_l(self.dtm, self.e_m)
        self.acum = _dot_exact_r(self.tri_m, self.dtm * self.avec)

    def stage_expand(self):
        self.acum_exp = _dot_hilo_l(self.acum, self.e_m)
        self.acum_t = self.acum.T
        self.xdt = self.xs * self.dt_exp
        bc = self.bc
        self.bg = [bc[:, 128 * g:128 * (g + 1)] for g in range(SSD_GROUPS)]
        self.cg = [bc[:, 256 + 128 * g:256 + 128 * (g + 1)] for g in range(SSD_GROUPS)]
        self.cb = [_dot_nt(self.cg[g], self.bg[g]) for g in range(SSD_GROUPS)]

    def stage_decay(self):
        C = SSD_C
        ii = lax.broadcasted_iota(jnp.int32, (C, C), 0)
        jj = lax.broadcasted_iota(jnp.int32, (C, C), 1)
        tri = (jj <= ii) if self.fwd else (jj >= ii)
        self.ms = []
        for g in range(SSD_GROUPS):
            for rr in range(SSD_HPG):
                ln = self.base + SSD_HPG * g + rr
                diff = self.acum[:, ln:ln + 1] - self.acum_t[ln:ln + 1, :]
                seg = jnp.where(tri, jnp.exp(jnp.minimum(diff, 0.0)), 0.0)
                self.ms.append((self.cb[g] * seg).astype(BF16))
        ae = self.acum_exp
        self.al_exp = ae[C - 1:C, :] if self.fwd else ae[0:1, :]
        self.xw = (self.xdt * jnp.exp(self.al_exp - ae)).astype(BF16)
        xdt_b = self.xdt.astype(BF16)
        head = lax.broadcasted_iota(jnp.int32, (C, SSD_INNER), 1) >> 6
        zero = jnp.zeros_like(xdt_b)
        self.xh = [jnp.where(head == h, xdt_b, zero) for h in range(SSD_HEADS)]

    def stage_products(self):
        self.yg, self.ug = [], []
        for g in range(SSD_GROUPS):
            gl, gh = 256 * g, 256 * (g + 1)
            yg = _dot(self.ms[SSD_HPG * g], self.xh[SSD_HPG * g][:, gl:gh])
            for rr in range(1, SSD_HPG):
                yg = yg + _dot(self.ms[SSD_HPG * g + rr], self.xh[SSD_HPG * g + rr][:, gl:gh])
            self.yg.append(yg)
            self.ug.append(_dot_tn(self.bg[g], self.xw[:, gl:gh]))

    def stage_state(self):
        ys = []
        for g in range(SSD_GROUPS):
            gl, gh = 256 * g, 256 * (g + 1)
            sg = self.st_ref[g]
            yoff = _dot(self.cg[g], sg.astype(BF16)) * jnp.exp(self.acum_exp[:, gl:gh])
            self.st_ref[g] = sg * jnp.exp(self.al_exp[:, gl:gh]) + self.ug[g]
            ys.append(self.yg[g] + yoff)
        y = jnp.concatenate(ys, axis=1)
        if self.dvec is not None:
            y = y + self.dvec * self.xs
        self.store(y)


def _ssd_kernel(xf_ref, bcf_ref, mf_ref, xb_ref, bcb_ref, mb_ref, af_ref, ab_ref, d_ref,
                trif_ref, trib_ref, ef_ref, eb_ref, yf_ref, yb_ref, stf, stb):
    s = pl.program_id(1)

    @pl.when(s == 0)
    def _():
        stf[...] = jnp.zeros_like(stf)
        stb[...] = jnp.zeros_like(stb)

    nsub = TB // SSD_C

    def store_to(ref, bb, sl):
        def store(val):
            ref[bb, sl, :] = val
        return store

    steps = []
    for i in range(nsub):
        sf = pl.ds(i * SSD_C, SSD_C)
        sb = pl.ds((nsub - 1 - i) * SSD_C, SSD_C)
        chunks = []
        for bb in range(xf_ref.shape[0]):
            chunks.append(_SsdChunk(xf_ref[bb, sf, :], bcf_ref[bb, sf, :], mf_ref[bb, sf, :], af_ref[...],
                                    d_ref[...], stf.at[bb], trif_ref[...], ef_ref[...], DT_F, True,
                                    store_to(yf_ref, bb, sf)))
            chunks.append(_SsdChunk(xb_ref[bb, sb, :], bcb_ref[bb, sb, :], mb_ref[bb, sb, :], ab_ref[...],
                                    None, stb.at[bb], trib_ref[...], eb_ref[...], DT_B, False,
                                    store_to(yb_ref, bb, sb)))
        steps.append(chunks)
    for g0 in range(0, nsub, SSD_STAGE_GROUP):
        group = steps[g0:g0 + SSD_STAGE_GROUP]
        for stage in ("stage_sums", "stage_expand", "stage_decay", "stage_products"):
            for chunks in group:
                for ch in chunks:
                    getattr(ch, stage)()
        for chunks in group:
            for ch in chunks:
                ch.stage_state()


def _expand_matrix(base):
    e = np.zeros((LANES, SSD_INNER), np.float32)
    for h in range(SSD_HEADS):
        e[base + h, SSD_HEADDIM * h:SSD_HEADDIM * (h + 1)] = 1.0
    return jnp.asarray(np.concatenate([e, e], axis=0), BF16)


def _ssd_call(xs, bc, misc, a_f, a_b, dvec, L):
    bsz, ls, _ = xs.shape
    ns = ls // TB
    nx = L // TB
    trif = jnp.asarray(np.tril(np.ones((SSD_C, SSD_C), np.float32)), BF16)
    trib = jnp.asarray(np.triu(np.ones((SSD_C, SSD_C), np.float32)), BF16)
    ef, eb = _expand_matrix(DT_F), _expand_matrix(DT_B)
    nb = _scan_batch(bsz)
    f = lambda n: pl.BlockSpec((nb, TB, n), lambda b, s: (b, _fwd_blk(s), 0))
    r = lambda n: pl.BlockSpec((nb, TB, n), lambda b, s: (b, _bwd_blk(s, ns), 0))
    const = lambda a: pl.BlockSpec(a.shape, lambda b, s: (0,) * a.ndim)
    return pl.pallas_call(
        _ssd_kernel,
        grid=(bsz // nb, ns),
        in_specs=[f(SSD_INNER), f(512), f(LANES), r(SSD_INNER), r(512), r(LANES),
                  const(a_f), const(a_b), const(dvec), const(trif), const(trib), const(ef), const(eb)],
        out_specs=[pl.BlockSpec((nb, TB, SSD_INNER), lambda b, s: (b, jnp.maximum(s - 1, 0), 0)),
                   pl.BlockSpec((nb, TB, SSD_INNER), lambda b, s: (b, jnp.where(s == 0, nx - 1, nx - s), 0))],
        out_shape=[jax.ShapeDtypeStruct((bsz, L, SSD_INNER), F32)] * 2,
        scratch_shapes=[pltpu.VMEM((nb, SSD_GROUPS, SSD_STATE, SSD_HPG * SSD_HEADDIM), F32)] * 2,
        compiler_params=_params("arbitrary", "arbitrary"),
    )(xs, bc, misc, xs, bc, misc, a_f, a_b, dvec, trif, trib, ef, eb)


def _outproj_kernel(of_ref, ob_ref, ga_ref, gb_ref, yf_ref, yb_ref, za_ref, zb_ref, x_ref, g1_ref, sh2_ref, sc2_ref,
                    gn_ref, sn_ref, wo_ref, n2_ref, wrh_ref, wrl_ref, br_ref,
                    x1_ref, h2_ref, eidx_ref, gate_ref, cnt_ref, mix_s, hl_s):
    first = jnp.logical_and(pl.program_id(0) == 0, pl.program_id(1) == 0)

    @pl.when(first)
    def _():
        cnt_ref[...] = jnp.zeros_like(cnt_ref)

    halves = [(pl.ds(0, TB), ga_ref, za_ref), (pl.ds(TB, TB), gb_ref, zb_ref)]

    for rows, g_ref, z_ref in halves:
        o = of_ref[0, rows, :] + ob_ref[0, rows, :]
        gg = g_ref[0]
        for h in range(GLA_HEADS):
            lo, hi = GLA_DV * h, GLA_DV * (h + 1)
            oh = o[:, lo:hi]
            ms = jnp.mean(oh * oh, axis=-1, keepdims=True)
            gh = gg[:, lo:hi]
            mix_s[rows, lo:hi] = (oh * lax.rsqrt(ms + EPS) * gn_ref[:, lo:hi] * (gh * _sigmoid(gh))).astype(BF16)
        zz = z_ref[0]
        u = (yf_ref[0, rows, :] + yb_ref[0, rows, :]) * (zz * _sigmoid(zz))
        gw = SSD_INNER // SSD_GROUPS
        for g in range(SSD_GROUPS):
            lo, hi = gw * g, gw * (g + 1)
            ug = u[:, lo:hi]
            ms = jnp.mean(ug * ug, axis=-1, keepdims=True)
            mix_s[rows, GLA_V + lo:GLA_V + hi] = (ug * lax.rsqrt(ms + EPS) * sn_ref[:, lo:hi]).astype(BF16)

    x1_ref[0] = x_ref[0] + g1_ref[0] * _dot(mix_s[...], wo_ref[...])

    for rows, _, _ in halves:
        x1 = x1_ref[0, rows, :]
        ms = jnp.mean(x1 * x1, axis=-1, keepdims=True)
        h2 = (x1 * lax.rsqrt(ms + EPS) * n2_ref[...]) * (1.0 + sc2_ref[0]) + sh2_ref[0]
        h_hi = h2.astype(BF16)
        h2_ref[0, rows, :] = h_hi
        hl_s[rows, :] = (h2 - h_hi.astype(F32)).astype(BF16)

    h_hi = h2_ref[0]
    logits = (_dot(h_hi, wrh_ref[...]) + _dot(hl_s[...], wrh_ref[...]) + _dot(h_hi, wrl_ref[...])) + br_ref[...]

    lane = lax.broadcasted_iota(jnp.int32, (TB, LANES), 1).astype(F32)
    st = [dict(work=logits[i * TB:(i + 1) * TB, :], eidx=jnp.full((TB, LANES), -1.0, F32),
               gates=jnp.zeros((TB, LANES), F32), sel=jnp.zeros((TB, LANES), F32), m0=None,
               den=jnp.zeros((TB, 1), F32)) for i in range(2)]
    for kk in range(TOP_K):
        for t in st:
            t["m"] = jnp.max(t["work"], axis=-1, keepdims=True)
        for t in st:
            t["idx"] = jnp.min(jnp.where(t["work"] == t["m"], lane, float(LANES)), axis=-1, keepdims=True)
        for t in st:
            hit = lane == t["idx"]
            if t["m0"] is None:
                t["m0"] = t["m"]
            e = jnp.exp(t["m"] - t["m0"])
            t["den"] = t["den"] + e
            t["eidx"] = jnp.where(lane == float(kk), t["idx"], t["eidx"])
            t["gates"] = jnp.where(lane == float(kk), e, t["gates"])
            t["sel"] = jnp.where(hit, 1.0, t["sel"])
            t["work"] = jnp.where(hit, NEG_BIG, t["work"])
    for (rows, _, _), t in zip(halves, st):
        eidx_ref[0, rows, :] = t["eidx"].astype(jnp.int32)
        gate_ref[0, rows, :] = t["gates"] / t["den"]
        cnt = jnp.sum(t["sel"], axis=0, keepdims=True)
        cnt_ref[...] += jnp.floor((cnt + (RUN - 1.0)) * (1.0 / RUN)) * RUN


def _outproj_call(o_f, o_b, g_all, y_f, y_b, z_all, x, g1, sh2, sc2, gn, sn, wo, n2, wr_hi, wr_lo, br):
    bsz, L, d = x.shape
    nj = L // (2 * TB)
    tok = lambda n: pl.BlockSpec((1, 2 * TB, n), lambda b, j: (b, j, 0))
    tok_off = lambda n, half: pl.BlockSpec((1, TB, n), lambda b, j: (b, 2 * j + 1 + half, 0))
    const = lambda a: pl.BlockSpec(a.shape, lambda b, j: (0,) * a.ndim)
    mod = pl.BlockSpec((1, 1, d), lambda b, j: (b, 0, 0))
    return pl.pallas_call(
        _outproj_kernel,
        grid=(bsz, nj),
        in_specs=[tok(GLA_V), tok(GLA_V), tok_off(GLA_V, 0), tok_off(GLA_V, 1),
                  tok(SSD_INNER), tok(SSD_INNER), tok_off(SSD_INNER, 0), tok_off(SSD_INNER, 1),
                  tok(d), mod, mod, mod, const(gn), const(sn), const(wo), const(n2),
                  const(wr_hi), const(wr_lo), const(br)],
        out_specs=[tok(d), tok(d), tok(LANES), tok(LANES), pl.BlockSpec((1, LANES), lambda b, j: (0, 0))],
        out_shape=[jax.ShapeDtypeStruct((bsz, L, d), F32), jax.ShapeDtypeStruct((bsz, L, d), BF16),
                   jax.ShapeDtypeStruct((bsz, L, LANES), jnp.int32), jax.ShapeDtypeStruct((bsz, L, LANES), F32),
                   jax.ShapeDtypeStruct((1, LANES), F32)],
        scratch_shapes=[pltpu.VMEM((2 * TB, d), BF16), pltpu.VMEM((2 * TB, d), BF16)],
        compiler_params=_params("arbitrary", "arbitrary"),
    )(o_f, o_b, g_all, g_all, y_f, y_b, z_all, z_all, x, g1, sh2, sc2, gn, sn, wo, n2, wr_hi, wr_lo, br)


def _pos_kernel(eidx_ref, pstart_ref, lst_ref, ust_ref, lp_ref, cd_ref, carry):
    @pl.when(pl.program_id(0) == 0)
    def _():
        carry[...] = pstart_ref[...]

    eidx = eidx_ref[...]
    lane = lax.broadcasted_iota(jnp.int32, (TB, LANES), 1)
    hits = [lane == eidx[:, kk:kk + 1] for kk in range(TOP_K)]
    sel = jnp.zeros((TB, LANES), F32)
    for hmask in hits:
        sel = jnp.where(hmask, 1.0, sel)
    cnt = jnp.sum(sel, axis=0, keepdims=True)
    run = jnp.floor((cnt + (RUN - 1.0)) * (1.0 / RUN)) * RUN
    rank = _dot(lst_ref[...], sel.astype(BF16))
    loff = _dot(jnp.broadcast_to(run, (8, LANES)).astype(BF16), ust_ref[...])[0:1]
    pos = loff + rank
    lp = jnp.zeros((TB, LANES), jnp.int32)
    for kk, hmask in enumerate(hits):
        lk = jnp.sum(jnp.where(hmask, pos, 0.0), axis=-1, keepdims=True)
        lp = jnp.where(lane == kk, lk.astype(jnp.int32), lp)
    lp_ref[...] = lp

    base = carry[...]
    eye = lax.broadcasted_iota(jnp.int32, (LANES, LANES), 0) == lax.broadcasted_iota(jnp.int32, (LANES, LANES), 1)
    col = lambda v: jnp.sum(jnp.where(eye, v, 0.0), axis=1, keepdims=True)
    loff_c, run_c, shift_c = col(loff), col(run), col(base - loff)
    row0 = (lax.broadcasted_iota(jnp.int32, (LANES, CD_LANES), 1) * RUN).astype(F32)
    inside = jnp.where(row0 >= loff_c, 1.0, 0.0) * jnp.where(row0 < loff_c + run_c, 1.0, 0.0)
    valid = jnp.sum(inside, axis=0, keepdims=True)
    dest = jnp.sum(inside * shift_c, axis=0, keepdims=True) + row0[0:1]
    cd_ref[0] = jnp.where(valid > 0.0, dest, -1.0).astype(jnp.int32)
    carry[...] = base + run


def _pos_call(eidx, pstart):
    T = eidx.shape[0]
    lst = jnp.asarray(np.tril(np.ones((TB, TB), np.float32), -1), BF16)
    ust = jnp.asarray(np.triu(np.ones((LANES, LANES), np.float32), 1), BF16)
    return pl.pallas_call(
        _pos_kernel,
        grid=(T // TB,),
        in_specs=[pl.BlockSpec((TB, LANES), lambda i: (i, 0)),
                  pl.BlockSpec((1, LANES), lambda i: (0, 0)),
                  pl.BlockSpec((TB, TB), lambda i: (0, 0)),
                  pl.BlockSpec((LANES, LANES), lambda i: (0, 0))],
        out_specs=[pl.BlockSpec((TB, LANES), lambda i: (i, 0)),
                   pl.BlockSpec((1, 1, CD_LANES), lambda i: (i, 0, 0))],
        out_shape=[jax.ShapeDtypeStruct((T, LANES), jnp.int32),
                   jax.ShapeDtypeStruct((T // TB, 1, CD_LANES), jnp.int32)],
        scratch_shapes=[pltpu.VMEM((1, LANES), F32)],
        compiler_params=_params("arbitrary"),
    )(eidx, pstart, lst, ust)


def _pair_matrix(lp, weights):
    j = lax.broadcasted_iota(jnp.int32, (TB, RL), 1)
    m = jnp.zeros((TB, RL), F32)
    for kk in range(TOP_K):
        w = 1.0 if weights is None else weights[:, kk:kk + 1]
        m = jnp.where(j == lp[:, kk:kk + 1], w, m)
    return m


def _pack_bf16_pairs(v, is_bf16_valued=False):
    if not is_bf16_valued:
        v = v.astype(BF16).astype(F32)
    bits = pltpu.bitcast(v, jnp.uint32)
    return bits[:, D_HALF:] | (bits[:, :D_HALF] >> 16)


def _unpack_bf16_pairs(w):
    lo = pltpu.bitcast(w << 16, F32).astype(BF16)
    hi = pltpu.bitcast(w & jnp.uint32(0xFFFF0000), F32).astype(BF16)
    return lo, hi


N_SLOTS = 3


def _dispatch_kernel(fill_ref, cdp_ref, cdc_ref, h_ref, lp_ref, xg_ref, sorted_s, sems, fill_sem):
    i = pl.program_id(0)
    n = pl.num_programs(0)
    slot = lax.rem(i, N_SLOTS)
    prev = lax.rem(i + N_SLOTS - 1, N_SLOTS)
    pprev = lax.rem(i + N_SLOTS - 2, N_SLOTS)
    spill0 = xg_ref.shape[0] - N_SLOTS * RL

    def start(cref, s, c, to_spill, sem):
        src = RUN * c if isinstance(c, int) else pl.multiple_of(RUN * c, RUN)
        d = cref[0, 0, c]
        d = pl.multiple_of(jnp.where(jnp.logical_or(d < 0, to_spill), spill0 + s * RL + src, d), RUN)
        pltpu.make_async_copy(sorted_s.at[s, pl.ds(src, RUN), :], xg_ref.at[pl.ds(d, RUN), :], sem).start()

    def drain(s, sem):
        pltpu.make_async_copy(sorted_s.at[s], xg_ref.at[pl.ds(0, RL), :], sem).wait()

    @pl.when(i == 0)
    def _():
        sorted_s[...] = jnp.zeros_like(sorted_s)
        for s in range(N_SLOTS):
            lax.fori_loop(0, RL // RUN, lambda c, carry, s=s: (start(cdp_ref, s, c, True, fill_sem), carry)[1], 0)
        for s in range(N_SLOTS):
            drain(s, fill_sem)

    for c in range(RL // RUN):
        start(cdp_ref, prev, c, i == 0, sems.at[prev])

    sorted_s[slot] = _pack_bf16_pairs(_dot_tn(_pair_matrix(lp_ref[...], None).astype(BF16), h_ref[...]),
                                      is_bf16_valued=True)

    @pl.when(i == 0)
    def _():
        def fill(k):
            blk = pl.multiple_of(fill_ref[k] * MOE_BLOCK, MOE_BLOCK)
            return pltpu.make_async_copy(sorted_s.at[0, pl.ds(0, MOE_BLOCK), :],
                                         xg_ref.at[pl.ds(blk, MOE_BLOCK), :], fill_sem)

        for k in range(fill_ref.shape[0]):
            pl.when(fill_ref[k] >= 0)(lambda k=k: fill(k).start())
        for k in range(fill_ref.shape[0]):
            pl.when(fill_ref[k] >= 0)(lambda k=k: fill(k).wait())

    pl.when(i >= 1)(lambda: drain(pprev, sems.at[pprev]))

    @pl.when(i == n - 1)
    def _():
        lax.fori_loop(0, RL // RUN, lambda c, carry: (start(cdc_ref, slot, c, False, sems.at[slot]), carry)[1], 0,
                      unroll=4)
        drain(prev, sems.at[prev])
        drain(slot, sems.at[slot])


def _dispatch_call(fill_blocks, cd, h2, lp, P):
    T, d = h2.shape
    grid_spec = pltpu.PrefetchScalarGridSpec(
        num_scalar_prefetch=1,
        grid=(T // TB,),
        in_specs=[pl.BlockSpec((1, 1, CD_LANES), lambda i, fb: (jnp.maximum(i - 1, 0), 0, 0),
                               memory_space=pltpu.SMEM),
                  pl.BlockSpec((1, 1, CD_LANES), lambda i, fb: (i, 0, 0), memory_space=pltpu.SMEM),
                  pl.BlockSpec((TB, d), lambda i, fb: (i, 0)),
                  pl.BlockSpec((TB, LANES), lambda i, fb: (i, 0))],
        out_specs=pl.BlockSpec(memory_space=pl.ANY),
        scratch_shapes=[pltpu.VMEM((N_SLOTS, RL, D_HALF), jnp.uint32), pltpu.SemaphoreType.DMA((N_SLOTS,)),
                        pltpu.SemaphoreType.DMA(())],
    )
    return pl.pallas_call(
        _dispatch_kernel,
        grid_spec=grid_spec,
        out_shape=jax.ShapeDtypeStruct((P + N_SLOTS * RL, D_HALF), jnp.uint32),
        compiler_params=_params("arbitrary"),
    )(fill_blocks, cd, cd, h2, lp)


def _moe_kernel(nb, info_ref, x_hbm, wgu_hbm, bgu_ref, wd_hbm, bd_ref, y_hbm,
                wgu_f, wd_f, wgu_b, wd_b, act_s, xbuf, ybuf, zbuf, wsems, xsems, ysems, zsem):
    n_used = info_ref[0, nb]

    def rows(blk):
        return pl.ds(pl.multiple_of(blk * MOE_BLOCK, MOE_BLOCK), MOE_BLOCK)

    def x_copy(blk, slot):
        return pltpu.make_async_copy(x_hbm.at[rows(blk), :], xbuf.at[slot], xsems.at[slot])

    def y_copy(blk, slot):
        return pltpu.make_async_copy(ybuf.at[slot], y_hbm.at[rows(blk), :], ysems.at[slot])

    def z_copy(blk):
        return pltpu.make_async_copy(zbuf, y_hbm.at[rows(blk), :], zsem)

    def fetch(expert, slot):
        return (pltpu.make_async_copy(wgu_hbm.at[expert], wgu_f.at[slot], wsems.at[0, slot]),
                pltpu.make_async_copy(wd_hbm.at[expert], wd_f.at[slot], wsems.at[1, slot]))

    zbuf[...] = jnp.zeros_like(zbuf)
    lax.fori_loop(n_used, nb, lambda blk, c: (z_copy(blk).start(), c)[1], 0)

    for cp in fetch(info_ref[0, 0], 0):
        cp.start()
    x_copy(0, 0).start()

    def body(i, carry):
        slot = i & 1
        e = info_ref[0, i]

        @pl.when(info_ref[1, i] == 1)
        def _():
            wslot = info_ref[2, i]
            nxt = info_ref[3, i]
            cps = fetch(e, wslot)
            cps[0].wait()
            wgu_b[...] = wgu_f[wslot].astype(BF16)
            cps[1].wait()
            wd_b[...] = wd_f[wslot].astype(BF16)

            @pl.when(nxt >= 0)
            def _():
                for cp in fetch(nxt, 1 - wslot):
                    cp.start()

        x_copy(i, slot).wait()
        pl.when(i + 1 < n_used)(lambda: x_copy(i + 1, 1 - slot).start())
        pl.when(i >= 2)(lambda: y_copy(i - 2, slot).wait())

        bgu = bgu_ref[pl.ds(e, 1), :]
        xb = jnp.concatenate(_unpack_bf16_pairs(xbuf[slot]), axis=1)
        cw = 256
        for c in range(D_FF // cw):
            lo, hi = c * cw, (c + 1) * cw
            gate = _dot(xb, wgu_b[:, lo:hi]) + bgu[:, lo:hi]
            up = _dot(xb, wgu_b[:, D_FF + lo:D_FF + hi]) + bgu[:, D_FF + lo:D_FF + hi]
            gate = jnp.minimum(gate, SWIGLU_LIMIT)
            up = jnp.clip(up, -SWIGLU_LIMIT, SWIGLU_LIMIT)
            act_s[:, lo:hi] = ((up + 1.0) * (gate * _sigmoid(SWIGLU_ALPHA * gate))).astype(BF16)
        ybuf[slot] = _pack_bf16_pairs(_dot(act_s[...], wd_b[...]) + bd_ref[pl.ds(e, 1), :])
        y_copy(i, slot).start()
        return carry

    lax.fori_loop(0, n_used, body, 0)

    pl.when(n_used >= 2)(lambda: y_copy(n_used - 2, n_used & 1).wait())
    y_copy(n_used - 1, (n_used - 1) & 1).wait()
    lax.fori_loop(n_used, nb, lambda blk, c: (z_copy(blk).wait(), c)[1], 0)


def _moe_call(info, xg, wgu, bgu, wd, bd, nb):
    P = nb * MOE_BLOCK
    ne, d, f2 = wgu.shape
    blk_buf = pltpu.VMEM((2, MOE_BLOCK, D_HALF), jnp.uint32)
    grid_spec = pltpu.PrefetchScalarGridSpec(
        num_scalar_prefetch=1,
        grid=(1,),
        in_specs=[pl.BlockSpec(memory_space=pl.ANY),
                  pl.BlockSpec(memory_space=pl.ANY),
                  pl.BlockSpec((ne, f2), lambda i, info: (0, 0)),
                  pl.BlockSpec(memory_space=pl.ANY),
                  pl.BlockSpec((ne, d), lambda i, info: (0, 0))],
        out_specs=pl.BlockSpec(memory_space=pl.ANY),
        scratch_shapes=[pltpu.VMEM((2, d, f2), F32), pltpu.VMEM((2, f2 // 2, d), F32),
                        pltpu.VMEM((d, f2), BF16), pltpu.VMEM((f2 // 2, d), BF16),
                        pltpu.VMEM((MOE_BLOCK, f2 // 2), BF16), blk_buf, blk_buf,
                        pltpu.VMEM((MOE_BLOCK, D_HALF), jnp.uint32),
                        pltpu.SemaphoreType.DMA((2, 2)), pltpu.SemaphoreType.DMA((2,)),
                        pltpu.SemaphoreType.DMA((2,)), pltpu.SemaphoreType.DMA(())],
    )
    return pl.pallas_call(
        functools.partial(_moe_kernel, nb),
        grid_spec=grid_spec,
        out_shape=jax.ShapeDtypeStruct((P, D_HALF), jnp.uint32),
        compiler_params=_params("arbitrary"),
    )(info, xg, wgu, bgu, wd, bd)


def _combine_kernel(d0_ref, d1_ref, d2_ref, y_ref, lp_ref, x1_ref, gate_ref, g2_ref, fn_ref, o_ref, buf, sems):
    i = pl.program_id(0)
    n = pl.num_programs(0)
    slot = lax.rem(i, N_SLOTS)
    ahead = lax.rem(i + 2, N_SLOTS)

    def start(dref, s, c):
        d = pl.multiple_of(jnp.maximum(dref[0, 0, c], 0), RUN)
        dst = pl.multiple_of(RUN * c, RUN)
        pltpu.make_async_copy(y_ref.at[pl.ds(d, RUN), :], buf.at[s, pl.ds(dst, RUN), :], sems.at[s]).start()

    def drain(s):
        pltpu.make_async_copy(y_ref.at[pl.ds(0, RL), :], buf.at[s], sems.at[s]).wait()

    @pl.when(i == 0)
    def _():
        lax.fori_loop(0, RL // RUN, lambda c, carry: (start(d0_ref, 0, c), carry)[1], 0, unroll=4)
        lax.fori_loop(0, RL // RUN, lambda c, carry: (start(d1_ref, 1, c), carry)[1], 0, unroll=4)

    drain(slot)

    for c in range(RL // RUN):
        start(d2_ref, ahead, c)

    g = _pair_matrix(lp_ref[...], gate_ref[...]).astype(BF16)
    halves = [_dot(g, yb) for yb in _unpack_bf16_pairs(buf[slot])]
    xo = x1_ref[...] + g2_ref[0] * jnp.concatenate(halves, axis=1)
    ms = jnp.mean(xo * xo, axis=-1, keepdims=True)
    o_ref[...] = xo * lax.rsqrt(ms + EPS) * fn_ref[...]

    @pl.when(i == n - 1)
    def _():
        drain(lax.rem(i + 1, N_SLOTS))
        drain(ahead)


def _combine_call(cd, y, lp, x1, gates, g2, fn):
    T, d = x1.shape
    n = T // TB
    per_batch = n // g2.shape[0]
    return pl.pallas_call(
        _combine_kernel,
        grid=(n,),
        in_specs=[pl.BlockSpec((1, 1, CD_LANES), lambda i: (i, 0, 0), memory_space=pltpu.SMEM),
                  pl.BlockSpec((1, 1, CD_LANES), lambda i: (jnp.minimum(i + 1, n - 1), 0, 0),
                               memory_space=pltpu.SMEM),
                  pl.BlockSpec((1, 1, CD_LANES), lambda i: (jnp.minimum(i + 2, n - 1), 0, 0),
                               memory_space=pltpu.SMEM),
                  pl.BlockSpec(memory_space=pl.ANY),
                  pl.BlockSpec((TB, LANES), lambda i: (i, 0)),
                  pl.BlockSpec((TB, d), lambda i: (i, 0)),
                  pl.BlockSpec((TB, LANES), lambda i: (i, 0)),
                  pl.BlockSpec((1, 1, d), lambda i: (i // per_batch, 0, 0)),
                  pl.BlockSpec((1, d), lambda i: (0, 0))],
        out_specs=pl.BlockSpec((TB, d), lambda i: (i, 0)),
        out_shape=jax.ShapeDtypeStruct((T, d), F32),
        scratch_shapes=[pltpu.VMEM((N_SLOTS, RL, D_HALF), jnp.uint32), pltpu.SemaphoreType.DMA((N_SLOTS,))],
        compiler_params=_params("arbitrary"),
    )(cd, cd, cd, y, lp, x1, gates, g2, fn)


def _layer(x, c, ctx, c_ctx, w_mod, b_mod, norm1, w_in, gla_w_gk_up, gla_b_gk, gla_norm,
           ssd_conv_w, ssd_conv_b, ssd_dt_bias, ssd_A_log, ssd_D, ssd_norm, w_out,
           norm2, w_router, b_router, w_gate_up, b_gate_up, w_down, b_down, final_norm):
    bsz, L, d = x.shape
    lc = ctx.shape[1]
    assert lc == TB and L % TB == 0 and TB % GRID_W == 0

    cin = jnp.zeros((8, d), F32).at[:bsz].set(c).at[bsz].set(c_ctx)
    mod = _mod_call(cin, w_mod, b_mod.reshape(1, -1))[:bsz + 1]
    sh1, sc1, g1, sh2, sc2, g2 = [m.reshape(bsz + 1, 1, d) for m in jnp.split(mod, 6, axis=-1)]

    o = np.cumsum((0, GLA_QK, GLA_QK, GLA_V, GLA_V, GLA_RANK, SSD_INNER, SSD_CONV_DIM, SSD_HEADS))
    wq, wk, wv, wg, wlow, wz, wx, wdt = [w_in[:, int(a):int(b)] for a, b in zip(o[:-1], o[1:])]
    w_misc = jnp.concatenate([wlow, wdt, wdt, jnp.zeros((d, LANES - GLA_RANK - 2 * SSD_HEADS), F32)], axis=1)
    w_cat = jnp.concatenate([wq, wk, wv, wg, wz, wx, w_misc], axis=1).astype(BF16)
    wup = jnp.zeros((LANES, 2 * GLA_QK), F32).at[:GLA_RANK].set(
        jnp.concatenate([gla_w_gk_up[0], gla_w_gk_up[1]], axis=1)).astype(BF16)
    bup = jnp.concatenate([gla_b_gk[0], gla_b_gk[1]]).reshape(1, -1)
    dtb = jnp.zeros((1, LANES), F32).at[0, DT_F:DT_F + SSD_HEADS].set(ssd_dt_bias[0]) \
                                    .at[0, DT_B:DT_B + SSD_HEADS].set(ssd_dt_bias[1])
    q, k, v, g_all, z_all, xbc, ld, misc = _inproj_call(
        x, ctx, sh1, sc1, norm1.reshape(1, d), w_cat, wup, bup, dtb)

    xs, bc = _conv_call(xbc, ssd_conv_w.reshape(9, SSD_CONV_DIM), ssd_conv_b.reshape(1, -1), lc + L)

    o_f, o_b = _gla_call(q, k, v, ld, L)

    a_neg = -jnp.exp(ssd_A_log.astype(F32))
    a_f = jnp.zeros((1, LANES), F32).at[0, DT_F:DT_F + SSD_HEADS].set(a_neg[0])
    a_b = jnp.zeros((1, LANES), F32).at[0, DT_B:DT_B + SSD_HEADS].set(a_neg[1])
    dvec = jnp.repeat(ssd_D, SSD_HEADDIM).reshape(1, SSD_INNER)
    y_f, y_b = _ssd_call(xs, bc, misc, a_f, a_b, dvec, L)

    wr = jnp.zeros((d, LANES), F32).at[:, :N_EXPERTS].set(w_router)
    wr_hi = wr.astype(BF16)
    wr_lo = (wr - wr_hi.astype(F32)).astype(BF16)
    br = jnp.full((1, LANES), NEG_BIG, F32).at[0, :N_EXPERTS].set(b_router)
    x1, h2, eidx, gates, counts = _outproj_call(
        o_f, o_b, g_all, y_f, y_b, z_all, x, g1[:bsz], sh2[:bsz], sc2[:bsz],
        jnp.tile(gla_norm, GLA_HEADS).reshape(1, -1), ssd_norm.reshape(1, -1), w_out.astype(BF16),
        norm2.reshape(1, d), wr_hi, wr_lo, br)

    T = bsz * L
    cnt = counts[0, :N_EXPERTS].astype(jnp.int32)
    padded = ((cnt + MOE_BLOCK - 1) // MOE_BLOCK) * MOE_BLOCK
    pend = jnp.cumsum(padded)
    pstart = pend - padded
    max_rows = T * TOP_K + (T // TB) * N_EXPERTS * (RUN - 1)
    n_blocks = -(-max_rows // MOE_BLOCK) + N_EXPERTS
    blk_start = jnp.arange(n_blocks, dtype=jnp.int32) * MOE_BLOCK
    blk_e = jnp.minimum(jnp.sum(pend[None, :] <= blk_start[:, None], axis=1), N_EXPERTS - 1).astype(jnp.int32)
    n_used = (pend[-1] // MOE_BLOCK).astype(jnp.int32)
    blk_i = jnp.arange(n_blocks, dtype=jnp.int32)
    first = (blk_i < n_used) & ((blk_i == 0) | (blk_e != jnp.roll(blk_e, 1)))
    slot = (jnp.cumsum(first) - 1) & 1
    first_pos = jnp.where(first, blk_i, n_blocks)
    next_first = jnp.roll(lax.cummin(first_pos, reverse=True), -1).at[-1].set(n_blocks)
    nxt = jnp.where(next_first < n_blocks, blk_e[jnp.minimum(next_first, n_blocks - 1)], -1)
    col = lambda v, last: jnp.concatenate([v.astype(jnp.int32), jnp.asarray([last], jnp.int32)])
    blk_info = jnp.stack([col(blk_e, 0).at[-1].set(n_used), col(first, 0), col(slot, 0), col(nxt, -1)])
    n_tail = n_blocks - (T * TOP_K) // MOE_BLOCK
    tail = n_used + jnp.arange(n_tail, dtype=jnp.int32)
    fill_blocks = jnp.concatenate([
        jnp.where(padded > 0, pend // MOE_BLOCK - 1, -1),
        jnp.where(tail < n_blocks, tail, -1)]).astype(jnp.int32)
    pstart_row = jnp.zeros((1, LANES), F32).at[0, :N_EXPERTS].set(pstart.astype(F32))
    lp, cd = _pos_call(eidx.reshape(T, LANES), pstart_row)

    xg = _dispatch_call(fill_blocks, cd, h2.reshape(T, d), lp, n_blocks * MOE_BLOCK)
    y = _moe_call(blk_info, xg, w_gate_up, b_gate_up, w_down, b_down, n_blocks)
    out = _combine_call(cd, y, lp, x1.reshape(T, d), gates.reshape(T, LANES), g2[:bsz], final_norm.reshape(1, d))
    return out.reshape(bsz, L, d)


def kernel(x, c, ctx, c_ctx, w_mod, b_mod, norm1, w_in, gla_w_gk_up, gla_b_gk, gla_norm, ssd_conv_w, ssd_conv_b, ssd_dt_bias, ssd_A_log, ssd_D, ssd_norm, w_out, norm2, w_router, b_router, w_gate_up, b_gate_up, w_down, b_down, final_norm):
    assert w_mod.shape[0] == 1, "single-layer kernel"
    return _layer(x, c, ctx, c_ctx, w_mod[0], b_mod[0], norm1[0], w_in[0], gla_w_gk_up[0], gla_b_gk[0],
                  gla_norm[0], ssd_conv_w[0], ssd_conv_b[0], ssd_dt_bias[0], ssd_A_log[0], ssd_D[0],
                  ssd_norm[0], w_out[0], norm2[0], w_router[0], b_router[0], w_gate_up[0], b_gate_up[0],
                  w_down[0], b_down[0], final_norm)
```

```python
import functools

import numpy as np
import jax
import jax.numpy as jnp
from jax import lax
from jax.experimental import pallas as pl
from jax.experimental.pallas import tpu as pltpu

F32 = jnp.float32
BF16 = jnp.bfloat16

EPS = 1e-6
GRID_W = 64
GLA_HEADS = 4
GLA_DK = 64
GLA_DV = 128
GLA_QK = GLA_HEADS * GLA_DK
GLA_V = GLA_HEADS * GLA_DV
GLA_RANK = 16
GLA_GATE_NORM = 16.0
SSD_HEADDIM = 64
SSD_INNER = 512
SSD_HEADS = 8
SSD_GROUPS = 2
SSD_HPG = 4
SSD_STATE = 128
SSD_CONV_DIM = 1024
N_EXPERTS = 32
TOP_K = 4
D_FF = 1024
SWIGLU_LIMIT = 7.0
SWIGLU_ALPHA = 1.702
MOE_BLOCK = 512

TB = 256
GLA_C = 64
GLA_STAGE_GROUP = 4
SSD_C = 128
SSD_STAGE_GROUP = 1
LANES = 128
EXP_CLAMP = 80.0
DT_F = 16
DT_B = 24
NEG_BIG = -1e30
RUN = 8
RL = -(-(TB * TOP_K + N_EXPERTS * (RUN - 1)) // LANES) * LANES
CD_LANES = -(-(RL // RUN) // LANES) * LANES
D_HALF = 512
VMEM_LIMIT = 56 * 1024 * 1024


def _dot(a, b):
    return jnp.dot(a, b, preferred_element_type=F32)


def _dot_nt(a, b):
    return lax.dot_general(a, b, (((1,), (1,)), ((), ())), preferred_element_type=F32)


def _dot_tn(a, b):
    return lax.dot_general(a, b, (((0,), (0,)), ((), ())), preferred_element_type=F32)


def _split3(a):
    hi = a.astype(BF16)
    r1 = a - hi.astype(F32)
    mid = r1.astype(BF16)
    lo = (r1 - mid.astype(F32)).astype(BF16)
    return hi, mid, lo


def _dot_exact_r(m, a):
    hi, mid, lo = _split3(a)
    return _dot(m, hi) + _dot(m, mid) + _dot(m, lo)


def _dot_hilo_r(m, a):
    hi = a.astype(BF16)
    lo = (a - hi.astype(F32)).astype(BF16)
    return _dot(m, hi) + _dot(m, lo)


def _dot_hilo_l(a, m2):
    hi = a.astype(BF16)
    lo = (a - hi.astype(F32)).astype(BF16)
    return _dot(jnp.concatenate([hi, lo], axis=1), m2)


def _sigmoid(x):
    return 1.0 / (1.0 + jnp.exp(-x))


def _softplus(x):
    return jnp.maximum(x, 0.0) + jnp.log1p(jnp.exp(-jnp.abs(x)))


def _params(*sem):
    return pltpu.CompilerParams(dimension_semantics=sem, vmem_limit_bytes=VMEM_LIMIT)


def _mod_kernel(c_ref, w_ref, b_ref, o_ref):
    c = c_ref[...]
    s = c * _sigmoid(c)
    s_hi = s.astype(BF16)
    s_lo = (s - s_hi.astype(F32)).astype(BF16)
    w = w_ref[...]
    w_hi = w.astype(BF16)
    w_lo = (w - w_hi.astype(F32)).astype(BF16)
    o_ref[...] = _dot(s_hi, w_hi) + _dot(s_lo, w_hi) + _dot(s_hi, w_lo) + b_ref[...]


def _mod_call(cin, w, b):
    rows, d = cin.shape
    n = w.shape[1]
    tn = 1536
    return pl.pallas_call(
        _mod_kernel,
        grid=(n // tn,),
        in_specs=[pl.BlockSpec((rows, d), lambda i: (0, 0)),
                  pl.BlockSpec((d, tn), lambda i: (0, i)),
                  pl.BlockSpec((1, tn), lambda i: (0, i))],
        out_specs=pl.BlockSpec((rows, tn), lambda i: (0, i)),
        out_shape=jax.ShapeDtypeStruct((rows, n), F32),
        compiler_params=_params("arbitrary"),
    )(cin, w, b)


_C_Q, _C_K, _C_V, _C_G, _C_Z, _C_X, _C_M, _C_END = 0, 256, 512, 1024, 1536, 2048, 3072, 3200


def _inproj_kernel(x0_ref, xa_ref, xb_ref, ctx_ref, shl_ref, scl_ref, shc_ref, scc_ref, n1_ref, w_ref, wup_ref,
                   bup_ref, dtb_ref, q_ref, k_ref, v_ref, g_ref, z_ref, xbc_ref, ld_ref, misc_ref, h_scr):
    j = pl.program_id(1)
    slot = j & 1

    def normmod(xv, sh_ref, sc_ref):
        ms = jnp.mean(xv * xv, axis=-1, keepdims=True)
        y = xv * lax.rsqrt(ms + EPS) * n1_ref[...]
        return (y * (1.0 + sc_ref[0]) + sh_ref[0]).astype(BF16)

    @pl.when(j == 0)
    def _():
        h_scr[0, 0:TB, :] = normmod(ctx_ref[0], shc_ref, scc_ref)
        h_scr[0, TB:2 * TB, :] = normmod(x0_ref[0], shl_ref, scl_ref)

    def mm(lo, hi):
        return _dot(h_scr[slot], w_ref[:, lo:hi])

    m = mm(_C_M, _C_END)
    zz = _dot(m.astype(BF16), wup_ref[...]) + bup_ref[...]
    ld_ref[0] = -_softplus(-zz) * (1.0 / GLA_GATE_NORM)
    misc_ref[0] = _softplus(m + dtb_ref[...])
    q_ref[0] = (mm(_C_Q, _C_K) * (GLA_DK ** -0.5)).astype(BF16)
    h_scr[1 - slot, 0:TB, :] = normmod(xa_ref[0], shl_ref, scl_ref)
    k_ref[0] = mm(_C_K, _C_V).astype(BF16)
    h_scr[1 - slot, TB:2 * TB, :] = normmod(xb_ref[0], shl_ref, scl_ref)
    v_ref[0] = mm(_C_V, _C_G).astype(BF16)
    g_ref[0] = mm(_C_G, _C_Z)
    z_ref[0] = mm(_C_Z, _C_X)
    xbc_ref[0] = mm(_C_X, _C_M)


def _inproj_call(x, ctx, sh1, sc1, n1, w_cat, wup, bup, dtb):
    bsz, L, d = x.shape
    nx = L // TB
    nj = -(-(nx + 1) // 2)
    tok = lambda n: pl.BlockSpec((1, 2 * TB, n), lambda b, j: (b, j, 0))
    const = lambda a: pl.BlockSpec(a.shape, lambda b, j: (0,) * a.ndim)
    xblk = lambda off: pl.BlockSpec((1, TB, d), lambda b, j: (b, jnp.minimum(2 * j + off, nx - 1), 0))
    first = pl.BlockSpec((1, TB, d), lambda b, j: (b, 0, 0))
    mod_lat = pl.BlockSpec((1, 1, d), lambda b, j: (b, 0, 0))
    mod_ctx = pl.BlockSpec((1, 1, d), lambda b, j: (bsz, 0, 0))
    outs = [(GLA_QK, BF16), (GLA_QK, BF16), (GLA_V, BF16), (GLA_V, F32), (SSD_INNER, F32),
            (SSD_CONV_DIM, F32), (2 * GLA_QK, F32), (LANES, F32)]
    return pl.pallas_call(
        _inproj_kernel,
        grid=(bsz, nj),
        in_specs=[first, xblk(1), xblk(2), first,
                  mod_lat, mod_lat, mod_ctx, mod_ctx,
                  const(n1), const(w_cat), const(wup), const(bup), const(dtb)],
        out_specs=[tok(n) for n, _ in outs],
        out_shape=[jax.ShapeDtypeStruct((bsz, nj * 2 * TB, n), dt) for n, dt in outs],
        scratch_shapes=[pltpu.VMEM((2, 2 * TB, d), BF16)],
        compiler_params=_params("arbitrary", "arbitrary"),
    )(x, x, x, ctx, sh1, sc1, sh1, sc1, n1, w_cat, wup, bup, dtb)


_EXT_PAD = 8
_EXT_BASE = _EXT_PAD + GRID_W
_EXT_ROWS = 2 * _EXT_PAD + 2 * GRID_W + TB


def _conv_kernel(prev_ref, cur_ref, next_ref, w_ref, b_ref, xs_ref, bc_ref, ext, xl_s, xr_s):
    j = pl.program_id(1)
    nj = pl.num_programs(1)
    is_ctx = j == 0
    zpad = jnp.zeros((_EXT_PAD, SSD_CONV_DIM), F32)
    ext[0:_EXT_PAD, :] = zpad
    ext[_EXT_ROWS - _EXT_PAD:_EXT_ROWS, :] = zpad
    ext[_EXT_PAD:_EXT_BASE, :] = jnp.where(j >= 2, prev_ref[0], 0.0)
    ext[_EXT_BASE:_EXT_BASE + TB, :] = cur_ref[0]
    ext[_EXT_BASE + TB:_EXT_BASE + TB + GRID_W, :] = jnp.where(
        jnp.logical_and(j >= 1, j <= nj - 2), next_ref[0], 0.0)

    win = TB + 2 * GRID_W
    u = lax.broadcasted_iota(jnp.int32, (win, LANES), 0)
    pos = jnp.where(is_ctx, u - GRID_W, u & (GRID_W - 1))
    ok_l = pos >= 1
    ok_r = jnp.where(is_ctx, pos - (TB - GRID_W), pos) <= GRID_W - 2
    lat = jnp.where(is_ctx, 0.0, 1.0)
    side = {-1: xl_s, 0: None, 1: xr_s}

    for c in range(SSD_CONV_DIM // LANES):
        lo, hi = c * LANES, (c + 1) * LANES
        xl_s[:, lo:hi] = jnp.where(ok_l, ext[_EXT_PAD - 1:_EXT_PAD - 1 + win, lo:hi], 0.0)
        xr_s[:, lo:hi] = jnp.where(ok_r, ext[_EXT_PAD + 1:_EXT_PAD + 1 + win, lo:hi], 0.0)
        acc = jnp.zeros((TB, LANES), F32)
        for dr in (-1, 0, 1):
            for dc in (-1, 0, 1):
                if dc == 0:
                    start = _EXT_BASE + GRID_W * dr
                    tap = ext[start:start + TB, lo:hi]
                else:
                    start = GRID_W + GRID_W * dr
                    tap = side[dc][start:start + TB, lo:hi]
                wi = 3 * (dr + 1) + (dc + 1)
                wv = w_ref[wi:wi + 1, lo:hi]
                if dr != 0:
                    wv = wv * lat
                acc = acc + tap * wv
        y = acc + b_ref[:, lo:hi]
        y = y * _sigmoid(y)
        if c < SSD_INNER // LANES:
            xs_ref[0, :, lo:hi] = y
        else:
            bc_ref[0, :, lo - SSD_INNER:hi - SSD_INNER] = y.astype(BF16)


def _conv_call(xbc, w9, bias, ls):
    bsz, _, ch = xbc.shape
    nj = ls // TB
    rpb = TB // GRID_W
    nrow = ls // GRID_W
    return pl.pallas_call(
        _conv_kernel,
        grid=(bsz, nj),
        in_specs=[pl.BlockSpec((1, GRID_W, ch), lambda b, j: (b, jnp.maximum(rpb * j - 1, 0), 0)),
                  pl.BlockSpec((1, TB, ch), lambda b, j: (b, j, 0)),
                  pl.BlockSpec((1, GRID_W, ch), lambda b, j: (b, jnp.minimum(rpb * j + rpb, nrow - 1), 0)),
                  pl.BlockSpec((9, ch), lambda b, j: (0, 0)),
                  pl.BlockSpec((1, ch), lambda b, j: (0, 0))],
        out_specs=[pl.BlockSpec((1, TB, SSD_INNER), lambda b, j: (b, j, 0)),
                   pl.BlockSpec((1, TB, ch - SSD_INNER), lambda b, j: (b, j, 0))],
        out_shape=[jax.ShapeDtypeStruct((bsz, ls, SSD_INNER), F32),
                   jax.ShapeDtypeStruct((bsz, ls, ch - SSD_INNER), BF16)],
        scratch_shapes=[pltpu.VMEM((_EXT_ROWS, ch), F32), pltpu.VMEM((TB + 2 * GRID_W, ch), F32),
                        pltpu.VMEM((TB + 2 * GRID_W, ch), F32)],
        compiler_params=_params("arbitrary", "arbitrary"),
    )(xbc, xbc, xbc, w9, bias)


def _fwd_blk(s):
    return s


def _bwd_blk(s, ns):
    return jnp.where(s == 0, 0, ns - s)


def _scan_batch(bsz):
    return 2 if bsz % 2 == 0 else 1


class _GlaChunk:
    def __init__(self, q, k, v, la, st_ref, tri_m, fwd, store):
        self.q, self.k, self.v, self.la, self.st_ref, self.tri_m, self.fwd, self.store = (
            q, k, v, la, st_ref, tri_m, fwd, store)

    def stage_sums(self):
        self.b = _dot_hilo_r(self.tri_m, self.la)

    def stage_factors(self):
        C = GLA_C
        b = self.b
        self.bt = b[C - 1:C, :] if self.fwd else b[0:1, :]
        r = 0.5 * self.bt
        self.er = jnp.exp(r)
        self.qt = (self.q.astype(F32) * jnp.exp(jnp.minimum(b - r, EXP_CLAMP))).astype(BF16)
        kt = (self.k.astype(F32) * jnp.exp(jnp.minimum(r - b, EXP_CLAMP))).astype(BF16)
        head_k = lax.broadcasted_iota(jnp.int32, (C, GLA_QK), 1) >> 6
        zero = jnp.zeros_like(kt)
        self.kh = [jnp.where(head_k == h, kt, zero) for h in range(GLA_HEADS)]
        self.qh = [jnp.where(head_k == h, self.qt, zero) for h in range(GLA_HEADS)]

    def stage_products(self):
        v = self.v
        kcat = jnp.concatenate(self.kh, axis=0)
        self.sc = _dot_nt(self.qt, kcat)
        vcat = jnp.concatenate([v[:, h * GLA_DV:(h + 1) * GLA_DV] for h in range(GLA_HEADS)], axis=0)
        self.u = _dot_tn(vcat, kcat) * self.er

    def stage_mask(self):
        C = GLA_C
        ii = lax.broadcasted_iota(jnp.int32, (C, GLA_HEADS * C), 0)
        jj = lax.broadcasted_iota(jnp.int32, (C, GLA_HEADS * C), 1) & (C - 1)
        causal = (jj <= ii) if self.fwd else (jj >= ii)
        self.p = jnp.where(causal, self.sc, 0.0).astype(BF16)
        v = self.v
        head_v = lax.broadcasted_iota(jnp.int32, (C, GLA_V), 1) >> 7
        self.vst = jnp.concatenate([jnp.where(head_v == h, v, jnp.zeros_like(v)) for h in range(GLA_HEADS)], axis=0)

    def stage_intra(self):
        self.o = _dot(self.p, self.vst)

    def stage_inter(self):
        st = self.st_ref[...]
        ster = (st * self.er).astype(BF16)
        self.st_ref[...] = st * jnp.exp(self.bt) + self.u
        res = _dot_nt(jnp.concatenate(self.qh, axis=0), ster)
        self.inter = jnp.concatenate([res[h * GLA_C:(h + 1) * GLA_C, :] for h in range(GLA_HEADS)], axis=1)

    def stage_out(self):
        self.store(self.o + self.inter)


def _gla_kernel(qf_ref, kf_ref, vf_ref, lf_ref, qb_ref, kb_ref, vb_ref, lb_ref, trif_ref, trib_ref,
                of_ref, ob_ref, stf, stb):
    s = pl.program_id(1)

    @pl.when(s == 0)
    def _():
        stf[...] = jnp.zeros_like(stf)
        stb[...] = jnp.zeros_like(stb)

    nsub = TB // GLA_C

    def store_to(ref, bb, sl):
        def store(val):
            ref[bb, sl, :] = val
        return store

    group = GLA_STAGE_GROUP
    for g0 in range(0, nsub, group):
        steps = []
        for i in range(g0, g0 + group):
            sf = pl.ds(i * GLA_C, GLA_C)
            sb = pl.ds((nsub - 1 - i) * GLA_C, GLA_C)
            chunks = []
            for bb in range(qf_ref.shape[0]):
                chunks.append(_GlaChunk(qf_ref[bb, sf, :], kf_ref[bb, sf, :], vf_ref[bb, sf, :], lf_ref[bb, sf, :],
                                        stf.at[bb], trif_ref[...], True, store_to(of_ref, bb, sf)))
                chunks.append(_GlaChunk(qb_ref[bb, sb, :], kb_ref[bb, sb, :], vb_ref[bb, sb, :], lb_ref[bb, sb, :],
                                        stb.at[bb], trib_ref[...], False, store_to(ob_ref, bb, sb)))
            steps.append(chunks)
        for stage in ("stage_sums", "stage_factors", "stage_products", "stage_mask", "stage_intra"):
            for chunks in steps:
                for ch in chunks:
                    getattr(ch, stage)()
        for chunks in steps:
            for stage in ("stage_inter", "stage_out"):
                for ch in chunks:
                    getattr(ch, stage)()


def _gla_call(q, k, v, ld, L):
    bsz = q.shape[0]
    nx = L // TB
    ns = nx + 1
    trif = jnp.asarray(np.tril(np.ones((GLA_C, GLA_C), np.float32)), BF16)
    trib = jnp.asarray(np.triu(np.ones((GLA_C, GLA_C), np.float32)), BF16)
    nb = _scan_batch(bsz)
    f = lambda n, lane=0: pl.BlockSpec((nb, TB, n), lambda b, s: (b, _fwd_blk(s), lane))
    r = lambda n, lane=0: pl.BlockSpec((nb, TB, n), lambda b, s: (b, _bwd_blk(s, ns), lane))
    tri = pl.BlockSpec((GLA_C, GLA_C), lambda b, s: (0, 0))
    return pl.pallas_call(
        _gla_kernel,
        grid=(bsz // nb, ns),
        in_specs=[f(GLA_QK), f(GLA_QK), f(GLA_V), f(GLA_QK, 0),
                  r(GLA_QK), r(GLA_QK), r(GLA_V), r(GLA_QK, 1), tri, tri],
        out_specs=[pl.BlockSpec((nb, TB, GLA_V), lambda b, s: (b, jnp.maximum(s - 1, 0), 0)),
                   pl.BlockSpec((nb, TB, GLA_V), lambda b, s: (b, jnp.where(s == 0, nx - 1, nx - s), 0))],
        out_shape=[jax.ShapeDtypeStruct((bsz, L, GLA_V), F32)] * 2,
        scratch_shapes=[pltpu.VMEM((nb, GLA_DV, GLA_QK), F32)] * 2,
        compiler_params=_params("arbitrary", "arbitrary"),
    )(q, k, v, ld, q, k, v, ld, trif, trib)


class _SsdChunk:
    def __init__(self, xs, bc, dtm, avec, dvec, st_ref, tri_m, e_m, base, fwd, store):
        self.xs, self.bc, self.dtm, self.avec, self.dvec, self.st_ref = xs, bc, dtm, avec, dvec, st_ref
        self.tri_m, self.e_m, self.base, self.fwd, self.store = tri_m, e_m, base, fwd, store

    def stage_sums(self):
        self.dt_exp = _dot_hilo_l(self.dtm, self.e_m)
        self.acum = _dot_exact_r(self.tri_m, self.dtm * self.avec)

    def stage_expand(self):
        self.acum_exp = _dot_hilo_l(self.acum, self.e_m)
        self.acum_t = self.acum.T
        self.xdt = self.xs * self.dt_exp
        bc = self.bc
        self.bg = [bc[:, 128 * g:128 * (g + 1)] for g in range(SSD_GROUPS)]
        self.cg = [bc[:, 256 + 128 * g:256 + 128 * (g + 1)] for g in range(SSD_GROUPS)]
        self.cb = [_dot_nt(self.cg[g], self.bg[g]) for g in range(SSD_GROUPS)]

    def stage_decay(self):
        C = SSD_C
        ii = lax.broadcasted_iota(jnp.int32, (C, C), 0)
        jj = lax.broadcasted_iota(jnp.int32, (C, C), 1)
        tri = (jj <= ii) if self.fwd else (jj >= ii)
        self.ms = []
        for g in range(SSD_GROUPS):
            for rr in range(SSD_HPG):
                ln = self.base + SSD_HPG * g + rr
                diff = self.acum[:, ln:ln + 1] - self.acum_t[ln:ln + 1, :]
                seg = jnp.where(tri, jnp.exp(jnp.minimum(diff, 0.0)), 0.0)
                self.ms.append((self.cb[g] * seg).astype(BF16))
        ae = self.acum_exp
        self.al_exp = ae[C - 1:C, :] if self.fwd else ae[0:1, :]
        self.xw = (self.xdt * jnp.exp(self.al_exp - ae)).astype(BF16)
        xdt_b = self.xdt.astype(BF16)
        gw = SSD_HPG * SSD_HEADDIM
        head = lax.broadcasted_iota(jnp.int32, (C, gw), 1) >> 6
        zero = jnp.zeros((C, gw), BF16)
        self.xh = [jnp.where(head == h % SSD_HPG, xdt_b[:, gw * (h // SSD_HPG):gw * (h // SSD_HPG + 1)], zero)
                   for h in range(SSD_HEADS)]

    def stage_products(self):
        self.yg, self.ug = [], []
        for g in range(SSD_GROUPS):
            gl, gh = 256 * g, 256 * (g + 1)
            yg = _dot(self.ms[SSD_HPG * g], self.xh[SSD_HPG * g])
            for rr in range(1, SSD_HPG):
                yg = yg + _dot(self.ms[SSD_HPG * g + rr], self.xh[SSD_HPG * g + rr])
            self.yg.append(yg)
            self.ug.append(_dot_tn(self.bg[g], self.xw[:, gl:gh]))

    def stage_state(self):
        ys = []
        for g in range(SSD_GROUPS):
            gl, gh = 256 * g, 256 * (g + 1)
            sg = self.st_ref[g]
            yoff = _dot(self.cg[g], sg.astype(BF16)) * jnp.exp(self.acum_exp[:, gl:gh])
            self.st_ref[g] = sg * jnp.exp(self.al_exp[:, gl:gh]) + self.ug[g]
            ys.append(self.yg[g] + yoff)
        y = jnp.concatenate(ys, axis=1)
        if self.dvec is not None:
            y = y + self.dvec * self.xs
        self.store(y)


def _ssd_kernel(xf_ref, bcf_ref, mf_ref, xb_ref, bcb_ref, mb_ref, af_ref, ab_ref, d_ref,
                trif_ref, trib_ref, ef_ref, eb_ref, yf_ref, yb_ref, stf, stb):
    s = pl.program_id(1)

    @pl.when(s == 0)
    def _():
        stf[...] = jnp.zeros_like(stf)
        stb[...] = jnp.zeros_like(stb)

    nsub = TB // SSD_C

    def store_to(ref, bb, sl):
        def store(val):
            ref[bb, sl, :] = val
        return store

    steps = []
    for i in range(nsub):
        sf = pl.ds(i * SSD_C, SSD_C)
        sb = pl.ds((nsub - 1 - i) * SSD_C, SSD_C)
        chunks = []
        for bb in range(xf_ref.shape[0]):
            chunks.append(_SsdChunk(xf_ref[bb, sf, :], bcf_ref[bb, sf, :], mf_ref[bb, sf, :], af_ref[...],
                                    d_ref[...], stf.at[bb], trif_ref[...], ef_ref[...], DT_F, True,
                                    store_to(yf_ref, bb, sf)))
            chunks.append(_SsdChunk(xb_ref[bb, sb, :], bcb_ref[bb, sb, :], mb_ref[bb, sb, :], ab_ref[...],
                                    None, stb.at[bb], trib_ref[...], eb_ref[...], DT_B, False,
                                    store_to(yb_ref, bb, sb)))
        steps.append(chunks)
    for g0 in range(0, nsub, SSD_STAGE_GROUP):
        group = steps[g0:g0 + SSD_STAGE_GROUP]
        for stage in ("stage_sums", "stage_expand", "stage_decay", "stage_products"):
            for chunks in group:
                for ch in chunks:
                    getattr(ch, stage)()
        for chunks in group:
            for ch in chunks:
                ch.stage_state()


def _expand_matrix(base):
    e = np.zeros((LANES, SSD_INNER), np.float32)
    for h in range(SSD_HEADS):
        e[base + h, SSD_HEADDIM * h:SSD_HEADDIM * (h + 1)] = 1.0
    return jnp.asarray(np.concatenate([e, e], axis=0), BF16)


def _ssd_call(xs, bc, misc, a_f, a_b, dvec, L):
    bsz, ls, _ = xs.shape
    ns = ls // TB
    nx = L // TB
    trif = jnp.asarray(np.tril(np.ones((SSD_C, SSD_C), np.float32)), BF16)
    trib = jnp.asarray(np.triu(np.ones((SSD_C, SSD_C), np.float32)), BF16)
    ef, eb = _expand_matrix(DT_F), _expand_matrix(DT_B)
    nb = _scan_batch(bsz)
    f = lambda n: pl.BlockSpec((nb, TB, n), lambda b, s: (b, _fwd_blk(s), 0))
    r = lambda n: pl.BlockSpec((nb, TB, n), lambda b, s: (b, _bwd_blk(s, ns), 0))
    const = lambda a: pl.BlockSpec(a.shape, lambda b, s: (0,) * a.ndim)
    return pl.pallas_call(
        _ssd_kernel,
        grid=(bsz // nb, ns),
        in_specs=[f(SSD_INNER), f(512), f(LANES), r(SSD_INNER), r(512), r(LANES),
                  const(a_f), const(a_b), const(dvec), const(trif), const(trib), const(ef), const(eb)],
        out_specs=[pl.BlockSpec((nb, TB, SSD_INNER), lambda b, s: (b, jnp.maximum(s - 1, 0), 0)),
                   pl.BlockSpec((nb, TB, SSD_INNER), lambda b, s: (b, jnp.where(s == 0, nx - 1, nx - s), 0))],
        out_shape=[jax.ShapeDtypeStruct((bsz, L, SSD_INNER), F32)] * 2,
        scratch_shapes=[pltpu.VMEM((nb, SSD_GROUPS, SSD_STATE, SSD_HPG * SSD_HEADDIM), F32)] * 2,
        compiler_params=_params("arbitrary", "arbitrary"),
    )(xs, bc, misc, xs, bc, misc, a_f, a_b, dvec, trif, trib, ef, eb)


def _outproj_kernel(of_ref, ob_ref, ga_ref, gb_ref, yf_ref, yb_ref, za_ref, zb_ref, x_ref, g1_ref, sh2_ref, sc2_ref,
                    gn_ref, sn_ref, wo_ref, n2_ref, wrh_ref, wrl_ref, br_ref,
                    x1_ref, h2_ref, eidx_ref, gate_ref, cnt_ref, mix_s, hl_s):
    first = jnp.logical_and(pl.program_id(0) == 0, pl.program_id(1) == 0)

    @pl.when(first)
    def _():
        cnt_ref[...] = jnp.zeros_like(cnt_ref)

    halves = [(pl.ds(0, TB), ga_ref, za_ref), (pl.ds(TB, TB), gb_ref, zb_ref)]

    for rows, g_ref, z_ref in halves:
        o = of_ref[0, rows, :] + ob_ref[0, rows, :]
        gg = g_ref[0]
        for h in range(GLA_HEADS):
            lo, hi = GLA_DV * h, GLA_DV * (h + 1)
            oh = o[:, lo:hi]
            ms = jnp.mean(oh * oh, axis=-1, keepdims=True)
            gh = gg[:, lo:hi]
            mix_s[rows, lo:hi] = (oh * lax.rsqrt(ms + EPS) * gn_ref[:, lo:hi] * (gh * _sigmoid(gh))).astype(BF16)
        zz = z_ref[0]
        u = (yf_ref[0, rows, :] + yb_ref[0, rows, :]) * (zz * _sigmoid(zz))
        gw = SSD_INNER // SSD_GROUPS
        for g in range(SSD_GROUPS):
            lo, hi = gw * g, gw * (g + 1)
            ug = u[:, lo:hi]
            ms = jnp.mean(ug * ug, axis=-1, keepdims=True)
            mix_s[rows, GLA_V + lo:GLA_V + hi] = (ug * lax.rsqrt(ms + EPS) * sn_ref[:, lo:hi]).astype(BF16)

    x1_ref[0] = x_ref[0] + g1_ref[0] * _dot(mix_s[...], wo_ref[...])

    for rows, _, _ in halves:
        x1 = x1_ref[0, rows, :]
        ms = jnp.mean(x1 * x1, axis=-1, keepdims=True)
        h2 = x1 * lax.rsqrt(ms + EPS) * (n2_ref[...] * (1.0 + sc2_ref[0])) + sh2_ref[0]
        h_hi = h2.astype(BF16)
        h2_ref[0, rows, :] = h_hi
        hl_s[rows, :] = (h2 - h_hi.astype(F32)).astype(BF16)

    h_hi = h2_ref[0]
    logits = (_dot(h_hi, wrh_ref[...]) + _dot(hl_s[...], wrh_ref[...]) + _dot(h_hi, wrl_ref[...])) + br_ref[...]

    lane = lax.broadcasted_iota(jnp.int32, (TB, LANES), 1).astype(F32)
    st = [dict(work=logits[i * TB:(i + 1) * TB, :], eidx=jnp.full((TB, LANES), -1.0, F32),
               gates=jnp.zeros((TB, LANES), F32), sel=jnp.zeros((TB, LANES), F32), m0=None,
               den=jnp.zeros((TB, 1), F32)) for i in range(2)]
    for kk in range(TOP_K):
        for t in st:
            t["m"] = jnp.max(t["work"], axis=-1, keepdims=True)
        for t in st:
            t["idx"] = jnp.min(jnp.where(t["work"] == t["m"], lane, float(LANES)), axis=-1, keepdims=True)
        for t in st:
            hit = lane == t["idx"]
            if t["m0"] is None:
                t["m0"] = t["m"]
            e = jnp.exp(t["m"] - t["m0"])
            t["den"] = t["den"] + e
            t["eidx"] = jnp.where(lane == float(kk), t["idx"], t["eidx"])
            t["gates"] = jnp.where(lane == float(kk), e, t["gates"])
            t["sel"] = jnp.where(hit, 1.0, t["sel"])
            t["work"] = jnp.where(hit, NEG_BIG, t["work"])
    for (rows, _, _), t in zip(halves, st):
        eidx_ref[0, rows, :] = t["eidx"].astype(jnp.int32)
        gate_ref[0, rows, :] = t["gates"] / t["den"]
        cnt = jnp.sum(t["sel"], axis=0, keepdims=True)
        cnt_ref[...] += jnp.floor((cnt + (RUN - 1.0)) * (1.0 / RUN)) * RUN


def _outproj_call(o_f, o_b, g_all, y_f, y_b, z_all, x, g1, sh2, sc2, gn, sn, wo, n2, wr_hi, wr_lo, br):
    bsz, L, d = x.shape
    nj = L // (2 * TB)
    tok = lambda n: pl.BlockSpec((1, 2 * TB, n), lambda b, j: (b, j, 0))
    tok_off = lambda n, half: pl.BlockSpec((1, TB, n), lambda b, j: (b, 2 * j + 1 + half, 0))
    const = lambda a: pl.BlockSpec(a.shape, lambda b, j: (0,) * a.ndim)
    mod = pl.BlockSpec((1, 1, d), lambda b, j: (b, 0, 0))
    return pl.pallas_call(
        _outproj_kernel,
        grid=(bsz, nj),
        in_specs=[tok(GLA_V), tok(GLA_V), tok_off(GLA_V, 0), tok_off(GLA_V, 1),
                  tok(SSD_INNER), tok(SSD_INNER), tok_off(SSD_INNER, 0), tok_off(SSD_INNER, 1),
                  tok(d), mod, mod, mod, const(gn), const(sn), const(wo), const(n2),
                  const(wr_hi), const(wr_lo), const(br)],
        out_specs=[tok(d), tok(d), tok(LANES), tok(LANES), pl.BlockSpec((1, LANES), lambda b, j: (0, 0))],
        out_shape=[jax.ShapeDtypeStruct((bsz, L, d), F32), jax.ShapeDtypeStruct((bsz, L, d), BF16),
                   jax.ShapeDtypeStruct((bsz, L, LANES), jnp.int32), jax.ShapeDtypeStruct((bsz, L, LANES), F32),
                   jax.ShapeDtypeStruct((1, LANES), F32)],
        scratch_shapes=[pltpu.VMEM((2 * TB, d), BF16), pltpu.VMEM((2 * TB, d), BF16)],
        compiler_params=_params("arbitrary", "arbitrary"),
    )(o_f, o_b, g_all, g_all, y_f, y_b, z_all, z_all, x, g1, sh2, sc2, gn, sn, wo, n2, wr_hi, wr_lo, br)


def _pos_kernel(eidx_ref, pstart_ref, lst_ref, ust_ref, lp_ref, cd_ref, carry):
    @pl.when(pl.program_id(0) == 0)
    def _():
        carry[...] = pstart_ref[...]

    eidx = eidx_ref[...]
    lane = lax.broadcasted_iota(jnp.int32, (TB, LANES), 1)
    hits = [lane == eidx[:, kk:kk + 1] for kk in range(TOP_K)]
    sel = jnp.zeros((TB, LANES), F32)
    for hmask in hits:
        sel = jnp.where(hmask, 1.0, sel)
    cnt = jnp.sum(sel, axis=0, keepdims=True)
    run = jnp.floor((cnt + (RUN - 1.0)) * (1.0 / RUN)) * RUN
    rank = _dot(lst_ref[...], sel.astype(BF16))
    loff = _dot(jnp.broadcast_to(run, (8, LANES)).astype(BF16), ust_ref[...])[0:1]
    pos = loff + rank
    lp = jnp.zeros((TB, LANES), jnp.int32)
    for kk, hmask in enumerate(hits):
        lk = jnp.sum(jnp.where(hmask, pos, 0.0), axis=-1, keepdims=True)
        lp = jnp.where(lane == kk, lk.astype(jnp.int32), lp)
    lp_ref[...] = lp

    base = carry[...]
    eye = lax.broadcasted_iota(jnp.int32, (LANES, LANES), 0) == lax.broadcasted_iota(jnp.int32, (LANES, LANES), 1)
    col = lambda v: jnp.sum(jnp.where(eye, v, 0.0), axis=1, keepdims=True)
    loff_c, run_c, shift_c = col(loff), col(run), col(base - loff)
    row0 = (lax.broadcasted_iota(jnp.int32, (LANES, CD_LANES), 1) * RUN).astype(F32)
    inside = jnp.where(row0 >= loff_c, 1.0, 0.0) * jnp.where(row0 < loff_c + run_c, 1.0, 0.0)
    valid = jnp.sum(inside, axis=0, keepdims=True)
    dest = jnp.sum(inside * shift_c, axis=0, keepdims=True) + row0[0:1]
    cd_ref[0] = jnp.where(valid > 0.0, dest, -1.0).astype(jnp.int32)
    carry[...] = base + run


def _pos_call(eidx, pstart):
    T = eidx.shape[0]
    lst = jnp.asarray(np.tril(np.ones((TB, TB), np.float32), -1), BF16)
    ust = jnp.asarray(np.triu(np.ones((LANES, LANES), np.float32), 1), BF16)
    return pl.pallas_call(
        _pos_kernel,
        grid=(T // TB,),
        in_specs=[pl.BlockSpec((TB, LANES), lambda i: (i, 0)),
                  pl.BlockSpec((1, LANES), lambda i: (0, 0)),
                  pl.BlockSpec((TB, TB), lambda i: (0, 0)),
                  pl.BlockSpec((LANES, LANES), lambda i: (0, 0))],
        out_specs=[pl.BlockSpec((TB, LANES), lambda i: (i, 0)),
                   pl.BlockSpec((1, 1, CD_LANES), lambda i: (i, 0, 0))],
        out_shape=[jax.ShapeDtypeStruct((T, LANES), jnp.int32),
                   jax.ShapeDtypeStruct((T // TB, 1, CD_LANES), jnp.int32)],
        scratch_shapes=[pltpu.VMEM((1, LANES), F32)],
        compiler_params=_params("arbitrary"),
    )(eidx, pstart, lst, ust)


def _pair_matrix(lp, weights):
    j = lax.broadcasted_iota(jnp.int32, (TB, RL), 1)
    m = jnp.zeros((TB, RL), F32)
    for kk in range(TOP_K):
        w = 1.0 if weights is None else weights[:, kk:kk + 1]
        m = jnp.where(j == lp[:, kk:kk + 1], w, m)
    return m


def _pack_bf16_pairs(v, is_bf16_valued=False):
    if not is_bf16_valued:
        v = v.astype(BF16).astype(F32)
    bits = pltpu.bitcast(v, jnp.uint32)
    return bits[:, D_HALF:] | (bits[:, :D_HALF] >> 16)


def _unpack_bf16_pairs(w):
    lo = pltpu.bitcast(w << 16, F32).astype(BF16)
    hi = pltpu.bitcast(w & jnp.uint32(0xFFFF0000), F32).astype(BF16)
    return lo, hi


N_SLOTS = 3


def _dispatch_kernel(fill_ref, cdp_ref, cdc_ref, h_ref, lp_ref, xg_ref, sorted_s, sems, fill_sem):
    i = pl.program_id(0)
    n = pl.num_programs(0)
    slot = lax.rem(i, N_SLOTS)
    prev = lax.rem(i + N_SLOTS - 1, N_SLOTS)
    pprev = lax.rem(i + N_SLOTS - 2, N_SLOTS)
    spill0 = xg_ref.shape[0] - N_SLOTS * RL

    def start(cref, s, c, to_spill, sem):
        src = RUN * c if isinstance(c, int) else pl.multiple_of(RUN * c, RUN)
        d = cref[0, 0, c]
        d = pl.multiple_of(jnp.where(jnp.logical_or(d < 0, to_spill), spill0 + s * RL + src, d), RUN)
        pltpu.make_async_copy(sorted_s.at[s, pl.ds(src, RUN), :], xg_ref.at[pl.ds(d, RUN), :], sem).start()

    def drain(s, sem):
        pltpu.make_async_copy(sorted_s.at[s], xg_ref.at[pl.ds(0, RL), :], sem).wait()

    @pl.when(i == 0)
    def _():
        sorted_s[...] = jnp.zeros_like(sorted_s)
        for s in range(N_SLOTS):
            lax.fori_loop(0, RL // RUN, lambda c, carry, s=s: (start(cdp_ref, s, c, True, fill_sem), carry)[1], 0)
        for s in range(N_SLOTS):
            drain(s, fill_sem)

    for c in range(RL // RUN):
        start(cdp_ref, prev, c, i == 0, sems.at[prev])

    sorted_s[slot] = _pack_bf16_pairs(_dot_tn(_pair_matrix(lp_ref[...], None).astype(BF16), h_ref[...]),
                                      is_bf16_valued=True)

    @pl.when(i == 0)
    def _():
        def fill(k):
            blk = pl.multiple_of(fill_ref[k] * MOE_BLOCK, MOE_BLOCK)
            return pltpu.make_async_copy(sorted_s.at[0, pl.ds(0, MOE_BLOCK), :],
                                         xg_ref.at[pl.ds(blk, MOE_BLOCK), :], fill_sem)

        for k in range(fill_ref.shape[0]):
            pl.when(fill_ref[k] >= 0)(lambda k=k: fill(k).start())
        for k in range(fill_ref.shape[0]):
            pl.when(fill_ref[k] >= 0)(lambda k=k: fill(k).wait())

    pl.when(i >= 1)(lambda: drain(pprev, sems.at[pprev]))

    @pl.when(i == n - 1)
    def _():
        lax.fori_loop(0, RL // RUN, lambda c, carry: (start(cdc_ref, slot, c, False, sems.at[slot]), carry)[1], 0,
                      unroll=4)
        drain(prev, sems.at[prev])
        drain(slot, sems.at[slot])


def _dispatch_call(fill_blocks, cd, h2, lp, P):
    T, d = h2.shape
    grid_spec = pltpu.PrefetchScalarGridSpec(
        num_scalar_prefetch=1,
        grid=(T // TB,),
        in_specs=[pl.BlockSpec((1, 1, CD_LANES), lambda i, fb: (jnp.maximum(i - 1, 0), 0, 0),
                               memory_space=pltpu.SMEM),
                  pl.BlockSpec((1, 1, CD_LANES), lambda i, fb: (i, 0, 0), memory_space=pltpu.SMEM),
                  pl.BlockSpec((TB, d), lambda i, fb: (i, 0)),
                  pl.BlockSpec((TB, LANES), lambda i, fb: (i, 0))],
        out_specs=pl.BlockSpec(memory_space=pl.ANY),
        scratch_shapes=[pltpu.VMEM((N_SLOTS, RL, D_HALF), jnp.uint32), pltpu.SemaphoreType.DMA((N_SLOTS,)),
                        pltpu.SemaphoreType.DMA(())],
    )
    return pl.pallas_call(
        _dispatch_kernel,
        grid_spec=grid_spec,
        out_shape=jax.ShapeDtypeStruct((P + N_SLOTS * RL, D_HALF), jnp.uint32),
        compiler_params=_params("arbitrary"),
    )(fill_blocks, cd, cd, h2, lp)


def _moe_kernel(nb, info_ref, x_hbm, wgu_hbm, bgu_ref, wd_hbm, bd_ref, y_hbm,
                wgu_f, wd_f, wgu_b, wd_b, act_s, xbuf, ybuf, zbuf, wsems, xsems, ysems, zsem):
    n_used = info_ref[0, nb]

    def rows(blk):
        return pl.ds(pl.multiple_of(blk * MOE_BLOCK, MOE_BLOCK), MOE_BLOCK)

    def x_copy(blk, slot):
        return pltpu.make_async_copy(x_hbm.at[rows(blk), :], xbuf.at[slot], xsems.at[slot])

    def y_copy(blk, slot):
        return pltpu.make_async_copy(ybuf.at[slot], y_hbm.at[rows(blk), :], ysems.at[slot])

    def z_copy(blk):
        return pltpu.make_async_copy(zbuf, y_hbm.at[rows(blk), :], zsem)

    def fetch(expert, slot):
        return (pltpu.make_async_copy(wgu_hbm.at[expert], wgu_f.at[slot], wsems.at[0, slot]),
                pltpu.make_async_copy(wd_hbm.at[expert], wd_f.at[slot], wsems.at[1, slot]))

    zbuf[...] = jnp.zeros_like(zbuf)
    lax.fori_loop(n_used, nb, lambda blk, c: (z_copy(blk).start(), c)[1], 0)

    for cp in fetch(info_ref[0, 0], 0):
        cp.start()
    x_copy(0, 0).start()

    def body(i, carry):
        slot = i & 1
        e = info_ref[0, i]

        @pl.when(info_ref[1, i] == 1)
        def _():
            wslot = info_ref[2, i]
            nxt = info_ref[3, i]
            cps = fetch(e, wslot)
            cps[0].wait()
            wgu_b[...] = wgu_f[wslot].astype(BF16)
            cps[1].wait()
            wd_b[...] = wd_f[wslot].astype(BF16)

            @pl.when(nxt >= 0)
            def _():
                for cp in fetch(nxt, 1 - wslot):
                    cp.start()

        x_copy(i, slot).wait()
        pl.when(i + 1 < n_used)(lambda: x_copy(i + 1, 1 - slot).start())
        pl.when(i >= 2)(lambda: y_copy(i - 2, slot).wait())

        bgu = bgu_ref[pl.ds(e, 1), :]
        xb = jnp.concatenate(_unpack_bf16_pairs(xbuf[slot]), axis=1)
        cw = 256
        for c in range(D_FF // cw):
            lo, hi = c * cw, (c + 1) * cw
            gate = _dot(xb, wgu_b[:, lo:hi]) + bgu[:, lo:hi]
            up = _dot(xb, wgu_b[:, D_FF + lo:D_FF + hi]) + bgu[:, D_FF + lo:D_FF + hi]
            gate = jnp.minimum(gate, SWIGLU_LIMIT)
            up = jnp.clip(up, -SWIGLU_LIMIT, SWIGLU_LIMIT)
            act_s[:, lo:hi] = ((up + 1.0) * (gate * _sigmoid(SWIGLU_ALPHA * gate))).astype(BF16)
        ybuf[slot] = _pack_bf16_pairs(_dot(act_s[...], wd_b[...]) + bd_ref[pl.ds(e, 1), :])
        y_copy(i, slot).start()
        return carry

    lax.fori_loop(0, n_used, body, 0)

    pl.when(n_used >= 2)(lambda: y_copy(n_used - 2, n_used & 1).wait())
    y_copy(n_used - 1, (n_used - 1) & 1).wait()
    lax.fori_loop(n_used, nb, lambda blk, c: (z_copy(blk).wait(), c)[1], 0)


def _moe_call(info, xg, wgu, bgu, wd, bd, nb):
    P = nb * MOE_BLOCK
    ne, d, f2 = wgu.shape
    blk_buf = pltpu.VMEM((2, MOE_BLOCK, D_HALF), jnp.uint32)
    grid_spec = pltpu.PrefetchScalarGridSpec(
        num_scalar_prefetch=1,
        grid=(1,),
        in_specs=[pl.BlockSpec(memory_space=pl.ANY),
                  pl.BlockSpec(memory_space=pl.ANY),
                  pl.BlockSpec((ne, f2), lambda i, info: (0, 0)),
                  pl.BlockSpec(memory_space=pl.ANY),
                  pl.BlockSpec((ne, d), lambda i, info: (0, 0))],
        out_specs=pl.BlockSpec(memory_space=pl.ANY),
        scratch_shapes=[pltpu.VMEM((2, d, f2), F32), pltpu.VMEM((2, f2 // 2, d), F32),
                        pltpu.VMEM((d, f2), BF16), pltpu.VMEM((f2 // 2, d), BF16),
                        pltpu.VMEM((MOE_BLOCK, f2 // 2), BF16), blk_buf, blk_buf,
                        pltpu.VMEM((MOE_BLOCK, D_HALF), jnp.uint32),
                        pltpu.SemaphoreType.DMA((2, 2)), pltpu.SemaphoreType.DMA((2,)),
                        pltpu.SemaphoreType.DMA((2,)), pltpu.SemaphoreType.DMA(())],
    )
    return pl.pallas_call(
        functools.partial(_moe_kernel, nb),
        grid_spec=grid_spec,
        out_shape=jax.ShapeDtypeStruct((P, D_HALF), jnp.uint32),
        compiler_params=_params("arbitrary"),
    )(info, xg, wgu, bgu, wd, bd)


def _combine_kernel(d0_ref, d1_ref, d2_ref, y_ref, lp_ref, x1_ref, gate_ref, g2_ref, fn_ref, o_ref, buf, sems):
    i = pl.program_id(0)
    n = pl.num_programs(0)
    slot = lax.rem(i, N_SLOTS)
    ahead = lax.rem(i + 2, N_SLOTS)

    def start(dref, s, c):
        d = pl.multiple_of(jnp.maximum(dref[0, 0, c], 0), RUN)
        dst = pl.multiple_of(RUN * c, RUN)
        pltpu.make_async_copy(y_ref.at[pl.ds(d, RUN), :], buf.at[s, pl.ds(dst, RUN), :], sems.at[s]).start()

    def drain(s):
        pltpu.make_async_copy(y_ref.at[pl.ds(0, RL), :], buf.at[s], sems.at[s]).wait()

    @pl.when(i == 0)
    def _():
        lax.fori_loop(0, RL // RUN, lambda c, carry: (start(d0_ref, 0, c), carry)[1], 0, unroll=4)
        lax.fori_loop(0, RL // RUN, lambda c, carry: (start(d1_ref, 1, c), carry)[1], 0, unroll=4)

    drain(slot)

    for c in range(RL // RUN):
        start(d2_ref, ahead, c)

    g = _pair_matrix(lp_ref[...], gate_ref[...]).astype(BF16)
    halves = [_dot(g, yb) for yb in _unpack_bf16_pairs(buf[slot])]
    xo = x1_ref[...] + g2_ref[0] * jnp.concatenate(halves, axis=1)
    ms = jnp.mean(xo * xo, axis=-1, keepdims=True)
    o_ref[...] = xo * lax.rsqrt(ms + EPS) * fn_ref[...]

    @pl.when(i == n - 1)
    def _():
        drain(lax.rem(i + 1, N_SLOTS))
        drain(ahead)


def _combine_call(cd, y, lp, x1, gates, g2, fn):
    T, d = x1.shape
    n = T // TB
    per_batch = n // g2.shape[0]
    return pl.pallas_call(
        _combine_kernel,
        grid=(n,),
        in_specs=[pl.BlockSpec((1, 1, CD_LANES), lambda i: (i, 0, 0), memory_space=pltpu.SMEM),
                  pl.BlockSpec((1, 1, CD_LANES), lambda i: (jnp.minimum(i + 1, n - 1), 0, 0),
                               memory_space=pltpu.SMEM),
                  pl.BlockSpec((1, 1, CD_LANES), lambda i: (jnp.minimum(i + 2, n - 1), 0, 0),
                               memory_space=pltpu.SMEM),
                  pl.BlockSpec(memory_space=pl.ANY),
                  pl.BlockSpec((TB, LANES), lambda i: (i, 0)),
                  pl.BlockSpec((TB, d), lambda i: (i, 0)),
                  pl.BlockSpec((TB, LANES), lambda i: (i, 0)),
                  pl.BlockSpec((1, 1, d), lambda i: (i // per_batch, 0, 0)),
                  pl.BlockSpec((1, d), lambda i: (0, 0))],
        out_specs=pl.BlockSpec((TB, d), lambda i: (i, 0)),
        out_shape=jax.ShapeDtypeStruct((T, d), F32),
        scratch_shapes=[pltpu.VMEM((N_SLOTS, RL, D_HALF), jnp.uint32), pltpu.SemaphoreType.DMA((N_SLOTS,))],
        compiler_params=_params("arbitrary"),
    )(cd, cd, cd, y, lp, x1, gates, g2, fn)


def _layer(x, c, ctx, c_ctx, w_mod, b_mod, norm1, w_in, gla_w_gk_up, gla_b_gk, gla_norm,
           ssd_conv_w, ssd_conv_b, ssd_dt_bias, ssd_A_log, ssd_D, ssd_norm, w_out,
           norm2, w_router, b_router, w_gate_up, b_gate_up, w_down, b_down, final_norm):
    bsz, L, d = x.shape
    lc = ctx.shape[1]
    assert lc == TB and L % TB == 0 and TB % GRID_W == 0

    cin = jnp.zeros((8, d), F32).at[:bsz].set(c).at[bsz].set(c_ctx)
    mod = _mod_call(cin, w_mod, b_mod.reshape(1, -1))[:bsz + 1]
    sh1, sc1, g1, sh2, sc2, g2 = [m.reshape(bsz + 1, 1, d) for m in jnp.split(mod, 6, axis=-1)]

    o = np.cumsum((0, GLA_QK, GLA_QK, GLA_V, GLA_V, GLA_RANK, SSD_INNER, SSD_CONV_DIM, SSD_HEADS))
    wq, wk, wv, wg, wlow, wz, wx, wdt = [w_in[:, int(a):int(b)] for a, b in zip(o[:-1], o[1:])]
    w_misc = jnp.concatenate([wlow, wdt, wdt, jnp.zeros((d, LANES - GLA_RANK - 2 * SSD_HEADS), F32)], axis=1)
    w_cat = jnp.concatenate([wq, wk, wv, wg, wz, wx, w_misc], axis=1).astype(BF16)
    wup = jnp.zeros((LANES, 2 * GLA_QK), F32).at[:GLA_RANK].set(
        jnp.concatenate([gla_w_gk_up[0], gla_w_gk_up[1]], axis=1)).astype(BF16)
    bup = jnp.concatenate([gla_b_gk[0], gla_b_gk[1]]).reshape(1, -1)
    dtb = jnp.zeros((1, LANES), F32).at[0, DT_F:DT_F + SSD_HEADS].set(ssd_dt_bias[0]) \
                                    .at[0, DT_B:DT_B + SSD_HEADS].set(ssd_dt_bias[1])
    q, k, v, g_all, z_all, xbc, ld, misc = _inproj_call(
        x, ctx, sh1, sc1, norm1.reshape(1, d), w_cat, wup, bup, dtb)

    xs, bc = _conv_call(xbc, ssd_conv_w.reshape(9, SSD_CONV_DIM), ssd_conv_b.reshape(1, -1), lc + L)

    o_f, o_b = _gla_call(q, k, v, ld, L)

    a_neg = -jnp.exp(ssd_A_log.astype(F32))
    a_f = jnp.zeros((1, LANES), F32).at[0, DT_F:DT_F + SSD_HEADS].set(a_neg[0])
    a_b = jnp.zeros((1, LANES), F32).at[0, DT_B:DT_B + SSD_HEADS].set(a_neg[1])
    dvec = jnp.repeat(ssd_D, SSD_HEADDIM).reshape(1, SSD_INNER)
    y_f, y_b = _ssd_call(xs, bc, misc, a_f, a_b, dvec, L)

    wr = jnp.zeros((d, LANES), F32).at[:, :N_EXPERTS].set(w_router)
    wr_hi = wr.astype(BF16)
    wr_lo = (wr - wr_hi.astype(F32)).astype(BF16)
    br = jnp.full((1, LANES), NEG_BIG, F32).at[0, :N_EXPERTS].set(b_router)
    x1, h2, eidx, gates, counts = _outproj_call(
        o_f, o_b, g_all, y_f, y_b, z_all, x, g1[:bsz], sh2[:bsz], sc2[:bsz],
        jnp.tile(gla_norm, GLA_HEADS).reshape(1, -1), ssd_norm.reshape(1, -1), w_out.astype(BF16),
        norm2.reshape(1, d), wr_hi, wr_lo, br)

    T = bsz * L
    cnt = counts[0, :N_EXPERTS].astype(jnp.int32)
    padded = ((cnt + MOE_BLOCK - 1) // MOE_BLOCK) * MOE_BLOCK
    pend = jnp.cumsum(padded)
    pstart = pend - padded
    max_rows = T * TOP_K + (T // TB) * N_EXPERTS * (RUN - 1)
    n_blocks = -(-max_rows // MOE_BLOCK) + N_EXPERTS
    blk_start = jnp.arange(n_blocks, dtype=jnp.int32) * MOE_BLOCK
    blk_e = jnp.minimum(jnp.sum(pend[None, :] <= blk_start[:, None], axis=1), N_EXPERTS - 1).astype(jnp.int32)
    n_used = (pend[-1] // MOE_BLOCK).astype(jnp.int32)
    blk_i = jnp.arange(n_blocks, dtype=jnp.int32)
    first = (blk_i < n_used) & ((blk_i == 0) | (blk_e != jnp.roll(blk_e, 1)))
    slot = (jnp.cumsum(first) - 1) & 1
    first_pos = jnp.where(first, blk_i, n_blocks)
    next_first = jnp.roll(lax.cummin(first_pos, reverse=True), -1).at[-1].set(n_blocks)
    nxt = jnp.where(next_first < n_blocks, blk_e[jnp.minimum(next_first, n_blocks - 1)], -1)
    col = lambda v, last: jnp.concatenate([v.astype(jnp.int32), jnp.asarray([last], jnp.int32)])
    blk_info = jnp.stack([col(blk_e, 0).at[-1].set(n_used), col(first, 0), col(slot, 0), col(nxt, -1)])
    n_tail = n_blocks - (T * TOP_K) // MOE_BLOCK
    tail = n_used + jnp.arange(n_tail, dtype=jnp.int32)
    fill_blocks = jnp.concatenate([
        jnp.where(padded > 0, pend // MOE_BLOCK - 1, -1),
        jnp.where(tail < n_blocks, tail, -1)]).astype(jnp.int32)
    pstart_row = jnp.zeros((1, LANES), F32).at[0, :N_EXPERTS].set(pstart.astype(F32))
    lp, cd = _pos_call(eidx.reshape(T, LANES), pstart_row)

    xg = _dispatch_call(fill_blocks, cd, h2.reshape(T, d), lp, n_blocks * MOE_BLOCK)
    y = _moe_call(blk_info, xg, w_gate_up, b_gate_up, w_down, b_down, n_blocks)
    out = _combine_call(cd, y, lp, x1.reshape(T, d), gates.reshape(T, LANES), g2[:bsz], final_norm.reshape(1, d))
    return out.reshape(bsz, L, d)


def kernel(x, c, ctx, c_ctx, w_mod, b_mod, norm1, w_in, gla_w_gk_up, gla_b_gk, gla_norm, ssd_conv_w, ssd_conv_b, ssd_dt_bias, ssd_A_log, ssd_D, ssd_norm, w_out, norm2, w_router, b_router, w_gate_up, b_gate_up, w_down, b_down, final_norm):
    assert w_mod.shape[0] == 1, "single-layer kernel"
    return _layer(x, c, ctx, c_ctx, w_mod[0], b_mod[0], norm1[0], w_in[0], gla_w_gk_up[0], gla_b_gk[0],
                  gla_norm[0], ssd_conv_w[0], ssd_conv_b[0], ssd_dt_bias[0], ssd_A_log[0], ssd_D[0],
                  ssd_norm[0], w_out[0], norm2[0], w_router[0], b_router[0], w_gate_up[0], b_gate_up[0],
                  w_down[0], b_down[0], final_norm)
```

```python
import functools

import numpy as np
import jax
import jax.numpy as jnp
from jax import lax
from jax.experimental import pallas as pl
from jax.experimental.pallas import tpu as pltpu

F32 = jnp.float32
BF16 = jnp.bfloat16

EPS = 1e-6
GRID_W = 64
GLA_HEADS = 4
GLA_DK = 64
GLA_DV = 128
GLA_QK = GLA_HEADS * GLA_DK
GLA_V = GLA_HEADS * GLA_DV
GLA_RANK = 16
GLA_GATE_NORM = 16.0
SSD_HEADDIM = 64
SSD_INNER = 512
SSD_HEADS = 8
SSD_GROUPS = 2
SSD_HPG = 4
SSD_STATE = 128
SSD_CONV_DIM = 1024
N_EXPERTS = 32
TOP_K = 4
D_FF = 1024
SWIGLU_LIMIT = 7.0
SWIGLU_ALPHA = 1.702
MOE_BLOCK = 512

TB = 256
GLA_C = 64
GLA_STAGE_GROUP = 4
SSD_C = 128
SSD_STAGE_GROUP = 1
LANES = 128
EXP_CLAMP = 80.0
DT_F = 16
DT_B = 24
NEG_BIG = -1e30
RUN = 8
RL = -(-(TB * TOP_K + N_EXPERTS * (RUN - 1)) // LANES) * LANES
CD_LANES = -(-(RL // RUN) // LANES) * LANES
D_HALF = 512
VMEM_LIMIT = 56 * 1024 * 1024


def _dot(a, b):
    return jnp.dot(a, b, preferred_element_type=F32)


def _dot_nt(a, b):
    return lax.dot_general(a, b, (((1,), (1,)), ((), ())), preferred_element_type=F32)


def _dot_tn(a, b):
    return lax.dot_general(a, b, (((0,), (0,)), ((), ())), preferred_element_type=F32)


def _split3(a):
    hi = a.astype(BF16)
    r1 = a - hi.astype(F32)
    mid = r1.astype(BF16)
    lo = (r1 - mid.astype(F32)).astype(BF16)
    return hi, mid, lo


def _dot_exact_r(m, a):
    hi, mid, lo = _split3(a)
    return _dot(m, hi) + _dot(m, mid) + _dot(m, lo)


def _dot_hilo_r(m, a):
    hi = a.astype(BF16)
    lo = (a - hi.astype(F32)).astype(BF16)
    return _dot(m, hi) + _dot(m, lo)


def _dot_hilo_l(a, m2):
    hi = a.astype(BF16)
    lo = (a - hi.astype(F32)).astype(BF16)
    return _dot(jnp.concatenate([hi, lo], axis=1), m2)


def _sigmoid(x):
    return 1.0 / (1.0 + jnp.exp(-x))


def _softplus(x):
    return jnp.maximum(x, 0.0) + jnp.log1p(jnp.exp(-jnp.abs(x)))


def _params(*sem):
    return pltpu.CompilerParams(dimension_semantics=sem, vmem_limit_bytes=VMEM_LIMIT)


def _mod_kernel(c_ref, w_ref, b_ref, o_ref):
    c = c_ref[...]
    s = c * _sigmoid(c)
    s_hi = s.astype(BF16)
    s_lo = (s - s_hi.astype(F32)).astype(BF16)
    w = w_ref[...]
    w_hi = w.astype(BF16)
    w_lo = (w - w_hi.astype(F32)).astype(BF16)
    o_ref[...] = _dot(s_hi, w_hi) + _dot(s_lo, w_hi) + _dot(s_hi, w_lo) + b_ref[...]


def _mod_call(cin, w, b):
    rows, d = cin.shape
    n = w.shape[1]
    tn = 1536
    return pl.pallas_call(
        _mod_kernel,
        grid=(n // tn,),
        in_specs=[pl.BlockSpec((rows, d), lambda i: (0, 0)),
                  pl.BlockSpec((d, tn), lambda i: (0, i)),
                  pl.BlockSpec((1, tn), lambda i: (0, i))],
        out_specs=pl.BlockSpec((rows, tn), lambda i: (0, i)),
        out_shape=jax.ShapeDtypeStruct((rows, n), F32),
        compiler_params=_params("arbitrary"),
    )(cin, w, b)


_C_Q, _C_K, _C_V, _C_G, _C_Z, _C_X, _C_M, _C_END = 0, 256, 512, 1024, 1536, 2048, 3072, 3200


def _inproj_kernel(x0_ref, xa_ref, xb_ref, ctx_ref, shl_ref, scl_ref, shc_ref, scc_ref, n1_ref, w_ref, wup_ref,
                   bup_ref, dtb_ref, q_ref, k_ref, v_ref, g_ref, z_ref, xbc_ref, ld_ref, misc_ref, h_scr):
    j = pl.program_id(1)
    slot = j & 1

    def normmod(xv, sh_ref, sc_ref):
        ms = jnp.mean(xv * xv, axis=-1, keepdims=True)
        y = xv * lax.rsqrt(ms + EPS) * n1_ref[...]
        return (y * (1.0 + sc_ref[0]) + sh_ref[0]).astype(BF16)

    @pl.when(j == 0)
    def _():
        h_scr[0, 0:TB, :] = normmod(ctx_ref[0], shc_ref, scc_ref)
        h_scr[0, TB:2 * TB, :] = normmod(x0_ref[0], shl_ref, scl_ref)

    def mm(lo, hi):
        return _dot(h_scr[slot], w_ref[:, lo:hi])

    m = mm(_C_M, _C_END)
    zz = _dot(m.astype(BF16), wup_ref[...]) + bup_ref[...]
    ld_ref[0] = -_softplus(-zz) * (1.0 / GLA_GATE_NORM)
    misc_ref[0] = _softplus(m + dtb_ref[...])
    q_ref[0] = (mm(_C_Q, _C_K) * (GLA_DK ** -0.5)).astype(BF16)
    h_scr[1 - slot, 0:TB, :] = normmod(xa_ref[0], shl_ref, scl_ref)
    k_ref[0] = mm(_C_K, _C_V).astype(BF16)
    h_scr[1 - slot, TB:2 * TB, :] = normmod(xb_ref[0], shl_ref, scl_ref)
    v_ref[0] = mm(_C_V, _C_G).astype(BF16)
    g_ref[0] = mm(_C_G, _C_Z)
    z_ref[0] = mm(_C_Z, _C_X)
    xbc_ref[0] = mm(_C_X, _C_M)


def _inproj_call(x, ctx, sh1, sc1, n1, w_cat, wup, bup, dtb):
    bsz, L, d = x.shape
    nx = L // TB
    nj = -(-(nx + 1) // 2)
    tok = lambda n: pl.BlockSpec((1, 2 * TB, n), lambda b, j: (b, j, 0))
    const = lambda a: pl.BlockSpec(a.shape, lambda b, j: (0,) * a.ndim)
    xblk = lambda off: pl.BlockSpec((1, TB, d), lambda b, j: (b, jnp.minimum(2 * j + off, nx - 1), 0))
    first = pl.BlockSpec((1, TB, d), lambda b, j: (b, 0, 0))
    mod_lat = pl.BlockSpec((1, 1, d), lambda b, j: (b, 0, 0))
    mod_ctx = pl.BlockSpec((1, 1, d), lambda b, j: (bsz, 0, 0))
    outs = [(GLA_QK, BF16), (GLA_QK, BF16), (GLA_V, BF16), (GLA_V, F32), (SSD_INNER, F32),
            (SSD_CONV_DIM, F32), (2 * GLA_QK, F32), (LANES, F32)]
    return pl.pallas_call(
        _inproj_kernel,
        grid=(bsz, nj),
        in_specs=[first, xblk(1), xblk(2), first,
                  mod_lat, mod_lat, mod_ctx, mod_ctx,
                  const(n1), const(w_cat), const(wup), const(bup), const(dtb)],
        out_specs=[tok(n) for n, _ in outs],
        out_shape=[jax.ShapeDtypeStruct((bsz, nj * 2 * TB, n), dt) for n, dt in outs],
        scratch_shapes=[pltpu.VMEM((2, 2 * TB, d), BF16)],
        compiler_params=_params("arbitrary", "arbitrary"),
    )(x, x, x, ctx, sh1, sc1, sh1, sc1, n1, w_cat, wup, bup, dtb)


_EXT_PAD = 8
_EXT_BASE = _EXT_PAD + GRID_W
_EXT_ROWS = 2 * _EXT_PAD + 2 * GRID_W + TB


def _conv_kernel(prev_ref, cur_ref, next_ref, w_ref, b_ref, xs_ref, bc_ref, ext, xl_s, xr_s):
    j = pl.program_id(1)
    nj = pl.num_programs(1)
    is_ctx = j == 0
    zpad = jnp.zeros((_EXT_PAD, SSD_CONV_DIM), F32)
    ext[0:_EXT_PAD, :] = zpad
    ext[_EXT_ROWS - _EXT_PAD:_EXT_ROWS, :] = zpad
    ext[_EXT_PAD:_EXT_BASE, :] = jnp.where(j >= 2, prev_ref[0], 0.0)
    ext[_EXT_BASE:_EXT_BASE + TB, :] = cur_ref[0]
    ext[_EXT_BASE + TB:_EXT_BASE + TB + GRID_W, :] = jnp.where(
        jnp.logical_and(j >= 1, j <= nj - 2), next_ref[0], 0.0)

    win = TB + 2 * GRID_W
    u = lax.broadcasted_iota(jnp.int32, (win, LANES), 0)
    pos = jnp.where(is_ctx, u - GRID_W, u & (GRID_W - 1))
    ok_l = pos >= 1
    ok_r = jnp.where(is_ctx, pos - (TB - GRID_W), pos) <= GRID_W - 2
    lat = jnp.where(is_ctx, 0.0, 1.0)
    side = {-1: xl_s, 0: None, 1: xr_s}

    for c in range(SSD_CONV_DIM // LANES):
        lo, hi = c * LANES, (c + 1) * LANES
        xl_s[:, lo:hi] = jnp.where(ok_l, ext[_EXT_PAD - 1:_EXT_PAD - 1 + win, lo:hi], 0.0)
        xr_s[:, lo:hi] = jnp.where(ok_r, ext[_EXT_PAD + 1:_EXT_PAD + 1 + win, lo:hi], 0.0)
        acc = jnp.zeros((TB, LANES), F32)
        for dr in (-1, 0, 1):
            for dc in (-1, 0, 1):
                if dc == 0:
                    start = _EXT_BASE + GRID_W * dr
                    tap = ext[start:start + TB, lo:hi]
                else:
                    start = GRID_W + GRID_W * dr
                    tap = side[dc][start:start + TB, lo:hi]
                wi = 3 * (dr + 1) + (dc + 1)
                wv = w_ref[wi:wi + 1, lo:hi]
                if dr != 0:
                    wv = wv * lat
                acc = acc + tap * wv
        y = acc + b_ref[:, lo:hi]
        y = y * _sigmoid(y)
        if c < SSD_INNER // LANES:
            xs_ref[0, :, lo:hi] = y
        else:
            bc_ref[0, :, lo - SSD_INNER:hi - SSD_INNER] = y.astype(BF16)


def _conv_call(xbc, w9, bias, ls):
    bsz, _, ch = xbc.shape
    nj = ls // TB
    rpb = TB // GRID_W
    nrow = ls // GRID_W
    return pl.pallas_call(
        _conv_kernel,
        grid=(bsz, nj),
        in_specs=[pl.BlockSpec((1, GRID_W, ch), lambda b, j: (b, jnp.maximum(rpb * j - 1, 0), 0)),
                  pl.BlockSpec((1, TB, ch), lambda b, j: (b, j, 0)),
                  pl.BlockSpec((1, GRID_W, ch), lambda b, j: (b, jnp.minimum(rpb * j + rpb, nrow - 1), 0)),
                  pl.BlockSpec((9, ch), lambda b, j: (0, 0)),
                  pl.BlockSpec((1, ch), lambda b, j: (0, 0))],
        out_specs=[pl.BlockSpec((1, TB, SSD_INNER), lambda b, j: (b, j, 0)),
                   pl.BlockSpec((1, TB, ch - SSD_INNER), lambda b, j: (b, j, 0))],
        out_shape=[jax.ShapeDtypeStruct((bsz, ls, SSD_INNER), F32),
                   jax.ShapeDtypeStruct((bsz, ls, ch - SSD_INNER), BF16)],
        scratch_shapes=[pltpu.VMEM((_EXT_ROWS, ch), F32), pltpu.VMEM((TB + 2 * GRID_W, ch), F32),
                        pltpu.VMEM((TB + 2 * GRID_W, ch), F32)],
        compiler_params=_params("arbitrary", "arbitrary"),
    )(xbc, xbc, xbc, w9, bias)


def _fwd_blk(s):
    return s


def _bwd_blk(s, ns):
    return jnp.where(s == 0, 0, ns - s)


def _scan_batch(bsz):
    return 2 if bsz % 2 == 0 else 1


class _GlaChunk:
    def __init__(self, q, k, v, la, st_ref, tri_m, fwd, store):
        self.q, self.k, self.v, self.la, self.st_ref, self.tri_m, self.fwd, self.store = (
            q, k, v, la, st_ref, tri_m, fwd, store)

    def stage_sums(self):
        self.b = _dot_hilo_r(self.tri_m, self.la)

    def stage_factors(self):
        C = GLA_C
        b = self.b
        self.bt = b[C - 1:C, :] if self.fwd else b[0:1, :]
        r = 0.5 * self.bt
        self.er = jnp.exp(r)
        self.qt = (self.q.astype(F32) * jnp.exp(jnp.minimum(b - r, EXP_CLAMP))).astype(BF16)
        kt = (self.k.astype(F32) * jnp.exp(jnp.minimum(r - b, EXP_CLAMP))).astype(BF16)
        head_k = lax.broadcasted_iota(jnp.int32, (C, GLA_QK), 1) >> 6
        zero = jnp.zeros_like(kt)
        self.kh = [jnp.where(head_k == h, kt, zero) for h in range(GLA_HEADS)]
        self.qh = [jnp.where(head_k == h, self.qt, zero) for h in range(GLA_HEADS)]

    def stage_products(self):
        v = self.v
        kcat = jnp.concatenate(self.kh, axis=0)
        self.sc = _dot_nt(self.qt, kcat)
        vcat = jnp.concatenate([v[:, h * GLA_DV:(h + 1) * GLA_DV] for h in range(GLA_HEADS)], axis=0)
        self.u = _dot_tn(vcat, kcat) * self.er

    def stage_mask(self):
        C = GLA_C
        ii = lax.broadcasted_iota(jnp.int32, (C, GLA_HEADS * C), 0)
        jj = lax.broadcasted_iota(jnp.int32, (C, GLA_HEADS * C), 1) & (C - 1)
        causal = (jj <= ii) if self.fwd else (jj >= ii)
        self.p = jnp.where(causal, self.sc, 0.0).astype(BF16)
        v = self.v
        head_v = lax.broadcasted_iota(jnp.int32, (C, GLA_V), 1) >> 7
        self.vst = jnp.concatenate([jnp.where(head_v == h, v, jnp.zeros_like(v)) for h in range(GLA_HEADS)], axis=0)

    def stage_intra(self):
        self.o = _dot(self.p, self.vst)

    def stage_inter(self):
        st = self.st_ref[...]
        ster = (st * self.er).astype(BF16)
        self.st_ref[...] = st * jnp.exp(self.bt) + self.u
        res = _dot_nt(jnp.concatenate(self.qh, axis=0), ster)
        self.inter = jnp.concatenate([res[h * GLA_C:(h + 1) * GLA_C, :] for h in range(GLA_HEADS)], axis=1)

    def stage_out(self):
        self.store(self.o + self.inter)


def _store_to(ref, bb, sl):
    def store(val):
        ref[bb, sl, :] = val
    return store


def _gla_schedule(qf_ref, kf_ref, vf_ref, lf_ref, qb_ref, kb_ref, vb_ref, lb_ref, trif_ref, trib_ref,
                  of_ref, ob_ref, stf, stb):
    nsub = TB // GLA_C
    store_to = _store_to
    calls = []

    group = GLA_STAGE_GROUP
    for g0 in range(0, nsub, group):
        steps = []
        for i in range(g0, g0 + group):
            sf = pl.ds(i * GLA_C, GLA_C)
            sb = pl.ds((nsub - 1 - i) * GLA_C, GLA_C)
            chunks = []
            for bb in range(qf_ref.shape[0]):
                chunks.append(_GlaChunk(qf_ref[bb, sf, :], kf_ref[bb, sf, :], vf_ref[bb, sf, :], lf_ref[bb, sf, :],
                                        stf.at[bb], trif_ref[...], True, store_to(of_ref, bb, sf)))
                chunks.append(_GlaChunk(qb_ref[bb, sb, :], kb_ref[bb, sb, :], vb_ref[bb, sb, :], lb_ref[bb, sb, :],
                                        stb.at[bb], trib_ref[...], False, store_to(ob_ref, bb, sb)))
            steps.append(chunks)
        for stage in ("stage_sums", "stage_factors", "stage_products", "stage_mask", "stage_intra"):
            for chunks in steps:
                calls += [getattr(ch, stage) for ch in chunks]
        for chunks in steps:
            for stage in ("stage_inter", "stage_out"):
                calls += [getattr(ch, stage) for ch in chunks]
    return calls


class _SsdChunk:
    def __init__(self, xs, bc, dtm, avec, dvec, st_ref, tri_m, e_m, base, fwd, store):
        self.xs, self.bc, self.dtm, self.avec, self.dvec, self.st_ref = xs, bc, dtm, avec, dvec, st_ref
        self.tri_m, self.e_m, self.base, self.fwd, self.store = tri_m, e_m, base, fwd, store

    def stage_sums(self):
        self.dt_exp = _dot_hilo_l(self.dtm, self.e_m)
        self.acum = _dot_exact_r(self.tri_m, self.dtm * self.avec)

    def stage_expand(self):
        self.acum_exp = _dot_hilo_l(self.acum, self.e_m)
        self.acum_t = self.acum.T
        self.xdt = self.xs * self.dt_exp
        bc = self.bc
        self.bg = [bc[:, 128 * g:128 * (g + 1)] for g in range(SSD_GROUPS)]
        self.cg = [bc[:, 256 + 128 * g:256 + 128 * (g + 1)] for g in range(SSD_GROUPS)]
        self.cb = [_dot_nt(self.cg[g], self.bg[g]) for g in range(SSD_GROUPS)]

    def stage_decay(self):
        C = SSD_C
        ii = lax.broadcasted_iota(jnp.int32, (C, C), 0)
        jj = lax.broadcasted_iota(jnp.int32, (C, C), 1)
        tri = (jj <= ii) if self.fwd else (jj >= ii)
        self.ms = []
        for g in range(SSD_GROUPS):
            for rr in range(SSD_HPG):
                ln = self.base + SSD_HPG * g + rr
                diff = self.acum[:, ln:ln + 1] - self.acum_t[ln:ln + 1, :]
                seg = jnp.where(tri, jnp.exp(jnp.minimum(diff, 0.0)), 0.0)
                self.ms.append((self.cb[g] * seg).astype(BF16))
        ae = self.acum_exp
        self.al_exp = ae[C - 1:C, :] if self.fwd else ae[0:1, :]
        self.xw = (self.xdt * jnp.exp(self.al_exp - ae)).astype(BF16)
        xdt_b = self.xdt.astype(BF16)
        gw = SSD_HPG * SSD_HEADDIM
        head = lax.broadcasted_iota(jnp.int32, (C, gw), 1) >> 6
        zero = jnp.zeros((C, gw), BF16)
        self.xh = [jnp.where(head == h % SSD_HPG, xdt_b[:, gw * (h // SSD_HPG):gw * (h // SSD_HPG + 1)], zero)
                   for h in range(SSD_HEADS)]

    def stage_products(self):
        self.yg, self.ug = [], []
        for g in range(SSD_GROUPS):
            gl, gh = 256 * g, 256 * (g + 1)
            yg = _dot(self.ms[SSD_HPG * g], self.xh[SSD_HPG * g])
            for rr in range(1, SSD_HPG):
                yg = yg + _dot(self.ms[SSD_HPG * g + rr], self.xh[SSD_HPG * g + rr])
            self.yg.append(yg)
            self.ug.append(_dot_tn(self.bg[g], self.xw[:, gl:gh]))

    def stage_state(self):
        ys = []
        for g in range(SSD_GROUPS):
            gl, gh = 256 * g, 256 * (g + 1)
            sg = self.st_ref[g]
            yoff = _dot(self.cg[g], sg.astype(BF16)) * jnp.exp(self.acum_exp[:, gl:gh])
            self.st_ref[g] = sg * jnp.exp(self.al_exp[:, gl:gh]) + self.ug[g]
            ys.append(self.yg[g] + yoff)
        y = jnp.concatenate(ys, axis=1)
        if self.dvec is not None:
            y = y + self.dvec * self.xs
        self.store(y)


def _ssd_schedule(xf_ref, bcf_ref, mf_ref, xb_ref, bcb_ref, mb_ref, af_ref, ab_ref, d_ref,
                  trif_ref, trib_ref, ef_ref, eb_ref, yf_ref, yb_ref, stf, stb):
    nsub = TB // SSD_C
    store_to = _store_to
    calls = []
    steps = []
    for i in range(nsub):
        sf = pl.ds(i * SSD_C, SSD_C)
        sb = pl.ds((nsub - 1 - i) * SSD_C, SSD_C)
        chunks = []
        for bb in range(xf_ref.shape[0]):
            chunks.append(_SsdChunk(xf_ref[bb, sf, :], bcf_ref[bb, sf, :], mf_ref[bb, sf, :], af_ref[...],
                                    d_ref[...], stf.at[bb], trif_ref[...], ef_ref[...], DT_F, True,
                                    store_to(yf_ref, bb, sf)))
            chunks.append(_SsdChunk(xb_ref[bb, sb, :], bcb_ref[bb, sb, :], mb_ref[bb, sb, :], ab_ref[...],
                                    None, stb.at[bb], trib_ref[...], eb_ref[...], DT_B, False,
                                    store_to(yb_ref, bb, sb)))
        steps.append(chunks)
    for g0 in range(0, nsub, SSD_STAGE_GROUP):
        group = steps[g0:g0 + SSD_STAGE_GROUP]
        for stage in ("stage_sums", "stage_expand", "stage_decay", "stage_products"):
            for chunks in group:
                calls += [getattr(ch, stage) for ch in chunks]
        for chunks in group:
            calls += [ch.stage_state for ch in chunks]
    return calls


N_GLA_IN, N_SSD_IN = 10, 13


def _scan_kernel(*refs):
    gla_in, ssd_in = refs[:N_GLA_IN], refs[N_GLA_IN:N_GLA_IN + N_SSD_IN]
    of_ref, ob_ref, yf_ref, yb_ref, gstf, gstb, sstf, sstb = refs[N_GLA_IN + N_SSD_IN:]

    @pl.when(pl.program_id(1) == 0)
    def _():
        for st in (gstf, gstb, sstf, sstb):
            st[...] = jnp.zeros_like(st)

    gla = _gla_schedule(*gla_in, of_ref, ob_ref, gstf, gstb)
    ssd = _ssd_schedule(*ssd_in, yf_ref, yb_ref, sstf, sstb)
    merged = sorted([((n + 0.5) / len(gla), 0, n, c) for n, c in enumerate(gla)] +
                    [((n + 0.5) / len(ssd), 1, n, c) for n, c in enumerate(ssd)], key=lambda t: t[:3])
    for _, _, _, call in merged:
        call()


def _expand_matrix(base):
    e = np.zeros((LANES, SSD_INNER), np.float32)
    for h in range(SSD_HEADS):
        e[base + h, SSD_HEADDIM * h:SSD_HEADDIM * (h + 1)] = 1.0
    return jnp.asarray(np.concatenate([e, e], axis=0), BF16)


def _scan_call(q, k, v, ld, xs, bc, misc, a_f, a_b, dvec, L):
    bsz = q.shape[0]
    nx = L // TB
    ns = nx + 1
    tri = lambda c, up: jnp.asarray((np.triu if up else np.tril)(np.ones((c, c), np.float32)), BF16)
    g_trif, g_trib, s_trif, s_trib = tri(GLA_C, False), tri(GLA_C, True), tri(SSD_C, False), tri(SSD_C, True)
    ef, eb = _expand_matrix(DT_F), _expand_matrix(DT_B)
    nb = _scan_batch(bsz)
    f = lambda n, lane=0: pl.BlockSpec((nb, TB, n), lambda b, s: (b, _fwd_blk(s), lane))
    r = lambda n, lane=0: pl.BlockSpec((nb, TB, n), lambda b, s: (b, _bwd_blk(s, ns), lane))
    const = lambda a: pl.BlockSpec(a.shape, lambda b, s: (0,) * a.ndim)
    out_f = lambda n: pl.BlockSpec((nb, TB, n), lambda b, s: (b, jnp.maximum(s - 1, 0), 0))
    out_b = lambda n: pl.BlockSpec((nb, TB, n), lambda b, s: (b, jnp.where(s == 0, nx - 1, nx - s), 0))
    gla_specs = [f(GLA_QK), f(GLA_QK), f(GLA_V), f(GLA_QK, 0), r(GLA_QK), r(GLA_QK), r(GLA_V), r(GLA_QK, 1),
                 const(g_trif), const(g_trib)]
    ssd_specs = [f(SSD_INNER), f(512), f(LANES), r(SSD_INNER), r(512), r(LANES),
                 const(a_f), const(a_b), const(dvec), const(s_trif), const(s_trib), const(ef), const(eb)]
    assert len(gla_specs) == N_GLA_IN and len(ssd_specs) == N_SSD_IN
    return pl.pallas_call(
        _scan_kernel,
        grid=(bsz // nb, ns),
        in_specs=gla_specs + ssd_specs,
        out_specs=[out_f(GLA_V), out_b(GLA_V), out_f(SSD_INNER), out_b(SSD_INNER)],
        out_shape=[jax.ShapeDtypeStruct((bsz, L, GLA_V), F32)] * 2 +
                  [jax.ShapeDtypeStruct((bsz, L, SSD_INNER), F32)] * 2,
        scratch_shapes=[pltpu.VMEM((nb, GLA_DV, GLA_QK), F32)] * 2 +
                       [pltpu.VMEM((nb, SSD_GROUPS, SSD_STATE, SSD_HPG * SSD_HEADDIM), F32)] * 2,
        compiler_params=_params("arbitrary", "arbitrary"),
    )(q, k, v, ld, q, k, v, ld, g_trif, g_trib, xs, bc, misc, xs, bc, misc, a_f, a_b, dvec, s_trif, s_trib, ef, eb)


def _outproj_kernel(of_ref, ob_ref, ga_ref, gb_ref, yf_ref, yb_ref, za_ref, zb_ref, x_ref, g1_ref, sh2_ref, sc2_ref,
                    gn_ref, sn_ref, wo_ref, n2_ref, wrh_ref, wrl_ref, br_ref,
                    x1_ref, h2_ref, eidx_ref, gate_ref, cnt_ref, mix_s, hl_s):
    first = jnp.logical_and(pl.program_id(0) == 0, pl.program_id(1) == 0)

    @pl.when(first)
    def _():
        cnt_ref[...] = jnp.zeros_like(cnt_ref)

    halves = [(pl.ds(0, TB), ga_ref, za_ref), (pl.ds(TB, TB), gb_ref, zb_ref)]

    for rows, g_ref, z_ref in halves:
        o = of_ref[0, rows, :] + ob_ref[0, rows, :]
        gg = g_ref[0]
        for h in range(GLA_HEADS):
            lo, hi = GLA_DV * h, GLA_DV * (h + 1)
            oh = o[:, lo:hi]
            ms = jnp.mean(oh * oh, axis=-1, keepdims=True)
            gh = gg[:, lo:hi]
            mix_s[rows, lo:hi] = (oh * lax.rsqrt(ms + EPS) * gn_ref[:, lo:hi] * (gh * _sigmoid(gh))).astype(BF16)
        zz = z_ref[0]
        u = (yf_ref[0, rows, :] + yb_ref[0, rows, :]) * (zz * _sigmoid(zz))
        gw = SSD_INNER // SSD_GROUPS
        for g in range(SSD_GROUPS):
            lo, hi = gw * g, gw * (g + 1)
            ug = u[:, lo:hi]
            ms = jnp.mean(ug * ug, axis=-1, keepdims=True)
            mix_s[rows, GLA_V + lo:GLA_V + hi] = (ug * lax.rsqrt(ms + EPS) * sn_ref[:, lo:hi]).astype(BF16)

    x1_ref[0] = x_ref[0] + g1_ref[0] * _dot(mix_s[...], wo_ref[...])

    for rows, _, _ in halves:
        x1 = x1_ref[0, rows, :]
        ms = jnp.mean(x1 * x1, axis=-1, keepdims=True)
        h2 = x1 * lax.rsqrt(ms + EPS) * (n2_ref[...] * (1.0 + sc2_ref[0])) + sh2_ref[0]
        h_hi = h2.astype(BF16)
        h2_ref[0, rows, :] = h_hi
        hl_s[rows, :] = (h2 - h_hi.astype(F32)).astype(BF16)

    h_hi = h2_ref[0]
    logits = (_dot(h_hi, wrh_ref[...]) + _dot(hl_s[...], wrh_ref[...]) + _dot(h_hi, wrl_ref[...])) + br_ref[...]

    lane = lax.broadcasted_iota(jnp.int32, (TB, LANES), 1).astype(F32)
    st = [dict(work=logits[i * TB:(i + 1) * TB, :], eidx=jnp.full((TB, LANES), -1.0, F32),
               gates=jnp.zeros((TB, LANES), F32), sel=jnp.zeros((TB, LANES), F32), m0=None,
               den=jnp.zeros((TB, 1), F32)) for i in range(2)]
    for kk in range(TOP_K):
        for t in st:
            t["m"] = jnp.max(t["work"], axis=-1, keepdims=True)
        for t in st:
            t["idx"] = jnp.min(jnp.where(t["work"] == t["m"], lane, float(LANES)), axis=-1, keepdims=True)
        for t in st:
            hit = lane == t["idx"]
            if t["m0"] is None:
                t["m0"] = t["m"]
            e = jnp.exp(t["m"] - t["m0"])
            t["den"] = t["den"] + e
            t["eidx"] = jnp.where(lane == float(kk), t["idx"], t["eidx"])
            t["gates"] = jnp.where(lane == float(kk), e, t["gates"])
            t["sel"] = jnp.where(hit, 1.0, t["sel"])
            t["work"] = jnp.where(hit, NEG_BIG, t["work"])
    for (rows, _, _), t in zip(halves, st):
        eidx_ref[0, rows, :] = t["eidx"].astype(jnp.int32)
        gate_ref[0, rows, :] = t["gates"] / t["den"]
        cnt = jnp.sum(t["sel"], axis=0, keepdims=True)
        cnt_ref[...] += jnp.floor((cnt + (RUN - 1.0)) * (1.0 / RUN)) * RUN


def _outproj_call(o_f, o_b, g_all, y_f, y_b, z_all, x, g1, sh2, sc2, gn, sn, wo, n2, wr_hi, wr_lo, br):
    bsz, L, d = x.shape
    nj = L // (2 * TB)
    tok = lambda n: pl.BlockSpec((1, 2 * TB, n), lambda b, j: (b, j, 0))
    tok_off = lambda n, half: pl.BlockSpec((1, TB, n), lambda b, j: (b, 2 * j + 1 + half, 0))
    const = lambda a: pl.BlockSpec(a.shape, lambda b, j: (0,) * a.ndim)
    mod = pl.BlockSpec((1, 1, d), lambda b, j: (b, 0, 0))
    return pl.pallas_call(
        _outproj_kernel,
        grid=(bsz, nj),
        in_specs=[tok(GLA_V), tok(GLA_V), tok_off(GLA_V, 0), tok_off(GLA_V, 1),
                  tok(SSD_INNER), tok(SSD_INNER), tok_off(SSD_INNER, 0), tok_off(SSD_INNER, 1),
                  tok(d), mod, mod, mod, const(gn), const(sn), const(wo), const(n2),
                  const(wr_hi), const(wr_lo), const(br)],
        out_specs=[tok(d), tok(d), tok(LANES), tok(LANES), pl.BlockSpec((1, LANES), lambda b, j: (0, 0))],
        out_shape=[jax.ShapeDtypeStruct((bsz, L, d), F32), jax.ShapeDtypeStruct((bsz, L, d), BF16),
                   jax.ShapeDtypeStruct((bsz, L, LANES), jnp.int32), jax.ShapeDtypeStruct((bsz, L, LANES), F32),
                   jax.ShapeDtypeStruct((1, LANES), F32)],
        scratch_shapes=[pltpu.VMEM((2 * TB, d), BF16), pltpu.VMEM((2 * TB, d), BF16)],
        compiler_params=_params("arbitrary", "arbitrary"),
    )(o_f, o_b, g_all, g_all, y_f, y_b, z_all, z_all, x, g1, sh2, sc2, gn, sn, wo, n2, wr_hi, wr_lo, br)


def _pos_kernel(eidx_ref, pstart_ref, lst_ref, ust_ref, lp_ref, cd_ref, carry):
    @pl.when(pl.program_id(0) == 0)
    def _():
        carry[...] = pstart_ref[...]

    eidx = eidx_ref[...]
    lane = lax.broadcasted_iota(jnp.int32, (TB, LANES), 1)
    hits = [lane == eidx[:, kk:kk + 1] for kk in range(TOP_K)]
    sel = jnp.zeros((TB, LANES), F32)
    for hmask in hits:
        sel = jnp.where(hmask, 1.0, sel)
    cnt = jnp.sum(sel, axis=0, keepdims=True)
    run = jnp.floor((cnt + (RUN - 1.0)) * (1.0 / RUN)) * RUN
    rank = _dot(lst_ref[...], sel.astype(BF16))
    loff = _dot(jnp.broadcast_to(run, (8, LANES)).astype(BF16), ust_ref[...])[0:1]
    pos = loff + rank
    lp = jnp.zeros((TB, LANES), jnp.int32)
    for kk, hmask in enumerate(hits):
        lk = jnp.sum(jnp.where(hmask, pos, 0.0), axis=-1, keepdims=True)
        lp = jnp.where(lane == kk, lk.astype(jnp.int32), lp)
    lp_ref[...] = lp

    base = carry[...]
    eye = lax.broadcasted_iota(jnp.int32, (LANES, LANES), 0) == lax.broadcasted_iota(jnp.int32, (LANES, LANES), 1)
    col = lambda v: jnp.sum(jnp.where(eye, v, 0.0), axis=1, keepdims=True)
    loff_c, run_c, shift_c = col(loff), col(run), col(base - loff)
    row0 = (lax.broadcasted_iota(jnp.int32, (LANES, CD_LANES), 1) * RUN).astype(F32)
    inside = jnp.where(row0 >= loff_c, 1.0, 0.0) * jnp.where(row0 < loff_c + run_c, 1.0, 0.0)
    valid = jnp.sum(inside, axis=0, keepdims=True)
    dest = jnp.sum(inside * shift_c, axis=0, keepdims=True) + row0[0:1]
    cd_ref[0] = jnp.where(valid > 0.0, dest, -1.0).astype(jnp.int32)
    carry[...] = base + run


def _pos_call(eidx, pstart):
    T = eidx.shape[0]
    lst = jnp.asarray(np.tril(np.ones((TB, TB), np.float32), -1), BF16)
    ust = jnp.asarray(np.triu(np.ones((LANES, LANES), np.float32), 1), BF16)
    return pl.pallas_call(
        _pos_kernel,
        grid=(T // TB,),
        in_specs=[pl.BlockSpec((TB, LANES), lambda i: (i, 0)),
                  pl.BlockSpec((1, LANES), lambda i: (0, 0)),
                  pl.BlockSpec((TB, TB), lambda i: (0, 0)),
                  pl.BlockSpec((LANES, LANES), lambda i: (0, 0))],
        out_specs=[pl.BlockSpec((TB, LANES), lambda i: (i, 0)),
                   pl.BlockSpec((1, 1, CD_LANES), lambda i: (i, 0, 0))],
        out_shape=[jax.ShapeDtypeStruct((T, LANES), jnp.int32),
                   jax.ShapeDtypeStruct((T // TB, 1, CD_LANES), jnp.int32)],
        scratch_shapes=[pltpu.VMEM((1, LANES), F32)],
        compiler_params=_params("arbitrary"),
    )(eidx, pstart, lst, ust)


def _pair_matrix(lp, weights):
    j = lax.broadcasted_iota(jnp.int32, (TB, RL), 1)
    m = jnp.zeros((TB, RL), F32)
    for kk in range(TOP_K):
        w = 1.0 if weights is None else weights[:, kk:kk + 1]
        m = jnp.where(j == lp[:, kk:kk + 1], w, m)
    return m


def _pack_bf16_pairs(v, is_bf16_valued=False):
    if not is_bf16_valued:
        v = v.astype(BF16).astype(F32)
    bits = pltpu.bitcast(v, jnp.uint32)
    return bits[:, D_HALF:] | (bits[:, :D_HALF] >> 16)


def _unpack_bf16_pairs(w):
    lo = pltpu.bitcast(w << 16, F32).astype(BF16)
    hi = pltpu.bitcast(w & jnp.uint32(0xFFFF0000), F32).astype(BF16)
    return lo, hi


N_SLOTS = 3


def _dispatch_kernel(fill_ref, cdp_ref, cdc_ref, h_ref, lp_ref, xg_ref, sorted_s, sems, fill_sem):
    i = pl.program_id(0)
    n = pl.num_programs(0)
    slot = lax.rem(i, N_SLOTS)
    prev = lax.rem(i + N_SLOTS - 1, N_SLOTS)
    pprev = lax.rem(i + N_SLOTS - 2, N_SLOTS)
    spill0 = xg_ref.shape[0] - N_SLOTS * RL

    def start(cref, s, c, to_spill, sem):
        src = RUN * c if isinstance(c, int) else pl.multiple_of(RUN * c, RUN)
        d = cref[0, 0, c]
        d = pl.multiple_of(jnp.where(jnp.logical_or(d < 0, to_spill), spill0 + s * RL + src, d), RUN)
        pltpu.make_async_copy(sorted_s.at[s, pl.ds(src, RUN), :], xg_ref.at[pl.ds(d, RUN), :], sem).start()

    def drain(s, sem):
        pltpu.make_async_copy(sorted_s.at[s], xg_ref.at[pl.ds(0, RL), :], sem).wait()

    @pl.when(i == 0)
    def _():
        sorted_s[...] = jnp.zeros_like(sorted_s)
        for s in range(N_SLOTS):
            lax.fori_loop(0, RL // RUN, lambda c, carry, s=s: (start(cdp_ref, s, c, True, fill_sem), carry)[1], 0)
        for s in range(N_SLOTS):
            drain(s, fill_sem)

    for c in range(RL // RUN):
        start(cdp_ref, prev, c, i == 0, sems.at[prev])

    sorted_s[slot] = _pack_bf16_pairs(_dot_tn(_pair_matrix(lp_ref[...], None).astype(BF16), h_ref[...]),
                                      is_bf16_valued=True)

    @pl.when(i == 0)
    def _():
        def fill(k):
            blk = pl.multiple_of(fill_ref[k] * MOE_BLOCK, MOE_BLOCK)
            return pltpu.make_async_copy(sorted_s.at[0, pl.ds(0, MOE_BLOCK), :],
                                         xg_ref.at[pl.ds(blk, MOE_BLOCK), :], fill_sem)

        for k in range(fill_ref.shape[0]):
            pl.when(fill_ref[k] >= 0)(lambda k=k: fill(k).start())
        for k in range(fill_ref.shape[0]):
            pl.when(fill_ref[k] >= 0)(lambda k=k: fill(k).wait())

    pl.when(i >= 1)(lambda: drain(pprev, sems.at[pprev]))

    @pl.when(i == n - 1)
    def _():
        lax.fori_loop(0, RL // RUN, lambda c, carry: (start(cdc_ref, slot, c, False, sems.at[slot]), carry)[1], 0,
                      unroll=4)
        drain(prev, sems.at[prev])
        drain(slot, sems.at[slot])


def _dispatch_call(fill_blocks, cd, h2, lp, P):
    T, d = h2.shape
    grid_spec = pltpu.PrefetchScalarGridSpec(
        num_scalar_prefetch=1,
        grid=(T // TB,),
        in_specs=[pl.BlockSpec((1, 1, CD_LANES), lambda i, fb: (jnp.maximum(i - 1, 0), 0, 0),
                               memory_space=pltpu.SMEM),
                  pl.BlockSpec((1, 1, CD_LANES), lambda i, fb: (i, 0, 0), memory_space=pltpu.SMEM),
                  pl.BlockSpec((TB, d), lambda i, fb: (i, 0)),
                  pl.BlockSpec((TB, LANES), lambda i, fb: (i, 0))],
        out_specs=pl.BlockSpec(memory_space=pl.ANY),
        scratch_shapes=[pltpu.VMEM((N_SLOTS, RL, D_HALF), jnp.uint32), pltpu.SemaphoreType.DMA((N_SLOTS,)),
                        pltpu.SemaphoreType.DMA(())],
    )
    return pl.pallas_call(
        _dispatch_kernel,
        grid_spec=grid_spec,
        out_shape=jax.ShapeDtypeStruct((P + N_SLOTS * RL, D_HALF), jnp.uint32),
        compiler_params=_params("arbitrary"),
    )(fill_blocks, cd, cd, h2, lp)


def _moe_kernel(nb, info_ref, x_hbm, wgu_hbm, bgu_ref, wd_hbm, bd_ref, y_hbm,
                wgu_f, wd_f, wgu_b, wd_b, act_s, xbuf, ybuf, zbuf, wsems, xsems, ysems, zsem):
    n_used = info_ref[0, nb]

    def rows(blk):
        return pl.ds(pl.multiple_of(blk * MOE_BLOCK, MOE_BLOCK), MOE_BLOCK)

    def x_copy(blk, slot):
        return pltpu.make_async_copy(x_hbm.at[rows(blk), :], xbuf.at[slot], xsems.at[slot])

    def y_copy(blk, slot):
        return pltpu.make_async_copy(ybuf.at[slot], y_hbm.at[rows(blk), :], ysems.at[slot])

    def z_copy(blk):
        return pltpu.make_async_copy(zbuf, y_hbm.at[rows(blk), :], zsem)

    def fetch(expert, slot):
        return (pltpu.make_async_copy(wgu_hbm.at[expert], wgu_f.at[slot], wsems.at[0, slot]),
                pltpu.make_async_copy(wd_hbm.at[expert], wd_f.at[slot], wsems.at[1, slot]))

    zbuf[...] = jnp.zeros_like(zbuf)
    lax.fori_loop(n_used, nb, lambda blk, c: (z_copy(blk).start(), c)[1], 0)

    for cp in fetch(info_ref[0, 0], 0):
        cp.start()
    x_copy(0, 0).start()

    def body(i, carry):
        slot = i & 1
        e = info_ref[0, i]

        @pl.when(info_ref[1, i] == 1)
        def _():
            wslot = info_ref[2, i]
            nxt = info_ref[3, i]
            cps = fetch(e, wslot)
            cps[0].wait()
            wgu_b[...] = wgu_f[wslot].astype(BF16)
            cps[1].wait()
            wd_b[...] = wd_f[wslot].astype(BF16)

            @pl.when(nxt >= 0)
            def _():
                for cp in fetch(nxt, 1 - wslot):
                    cp.start()

        x_copy(i, slot).wait()
        pl.when(i + 1 < n_used)(lambda: x_copy(i + 1, 1 - slot).start())
        pl.when(i >= 2)(lambda: y_copy(i - 2, slot).wait())

        bgu = bgu_ref[pl.ds(e, 1), :]
        xb = jnp.concatenate(_unpack_bf16_pairs(xbuf[slot]), axis=1)
        cw = 256
        for c in range(D_FF // cw):
            lo, hi = c * cw, (c + 1) * cw
            gate = _dot(xb, wgu_b[:, lo:hi]) + bgu[:, lo:hi]
            up = _dot(xb, wgu_b[:, D_FF + lo:D_FF + hi]) + bgu[:, D_FF + lo:D_FF + hi]
            gate = jnp.minimum(gate, SWIGLU_LIMIT)
            up = jnp.clip(up, -SWIGLU_LIMIT, SWIGLU_LIMIT)
            act_s[:, lo:hi] = ((up + 1.0) * (gate * _sigmoid(SWIGLU_ALPHA * gate))).astype(BF16)
        ybuf[slot] = _pack_bf16_pairs(_dot(act_s[...], wd_b[...]) + bd_ref[pl.ds(e, 1), :])
        y_copy(i, slot).start()
        return carry

    lax.fori_loop(0, n_used, body, 0)

    pl.when(n_used >= 2)(lambda: y_copy(n_used - 2, n_used & 1).wait())
    y_copy(n_used - 1, (n_used - 1) & 1).wait()
    lax.fori_loop(n_used, nb, lambda blk, c: (z_copy(blk).wait(), c)[1], 0)


def _moe_call(info, xg, wgu, bgu, wd, bd, nb):
    P = nb * MOE_BLOCK
    ne, d, f2 = wgu.shape
    blk_buf = pltpu.VMEM((2, MOE_BLOCK, D_HALF), jnp.uint32)
    grid_spec = pltpu.PrefetchScalarGridSpec(
        num_scalar_prefetch=1,
        grid=(1,),
        in_specs=[pl.BlockSpec(memory_space=pl.ANY),
                  pl.BlockSpec(memory_space=pl.ANY),
                  pl.BlockSpec((ne, f2), lambda i, info: (0, 0)),
                  pl.BlockSpec(memory_space=pl.ANY),
                  pl.BlockSpec((ne, d), lambda i, info: (0, 0))],
        out_specs=pl.BlockSpec(memory_space=pl.ANY),
        scratch_shapes=[pltpu.VMEM((2, d, f2), F32), pltpu.VMEM((2, f2 // 2, d), F32),
                        pltpu.VMEM((d, f2), BF16), pltpu.VMEM((f2 // 2, d), BF16),
                        pltpu.VMEM((MOE_BLOCK, f2 // 2), BF16), blk_buf, blk_buf,
                        pltpu.VMEM((MOE_BLOCK, D_HALF), jnp.uint32),
                        pltpu.SemaphoreType.DMA((2, 2)), pltpu.SemaphoreType.DMA((2,)),
                        pltpu.SemaphoreType.DMA((2,)), pltpu.SemaphoreType.DMA(())],
    )
    return pl.pallas_call(
        functools.partial(_moe_kernel, nb),
        grid_spec=grid_spec,
        out_shape=jax.ShapeDtypeStruct((P, D_HALF), jnp.uint32),
        compiler_params=_params("arbitrary"),
    )(info, xg, wgu, bgu, wd, bd)


def _combine_kernel(d0_ref, d1_ref, d2_ref, y_ref, lp_ref, x1_ref, gate_ref, g2_ref, fn_ref, o_ref, buf, sems):
    i = pl.program_id(0)
    n = pl.num_programs(0)
    slot = lax.rem(i, N_SLOTS)
    ahead = lax.rem(i + 2, N_SLOTS)

    def start(dref, s, c):
        d = pl.multiple_of(jnp.maximum(dref[0, 0, c], 0), RUN)
        dst = pl.multiple_of(RUN * c, RUN)
        pltpu.make_async_copy(y_ref.at[pl.ds(d, RUN), :], buf.at[s, pl.ds(dst, RUN), :], sems.at[s]).start()

    def drain(s):
        pltpu.make_async_copy(y_ref.at[pl.ds(0, RL), :], buf.at[s], sems.at[s]).wait()

    @pl.when(i == 0)
    def _():
        lax.fori_loop(0, RL // RUN, lambda c, carry: (start(d0_ref, 0, c), carry)[1], 0, unroll=4)
        lax.fori_loop(0, RL // RUN, lambda c, carry: (start(d1_ref, 1, c), carry)[1], 0, unroll=4)

    drain(slot)

    for c in range(RL // RUN):
        start(d2_ref, ahead, c)

    g = _pair_matrix(lp_ref[...], gate_ref[...]).astype(BF16)
    halves = [_dot(g, yb) for yb in _unpack_bf16_pairs(buf[slot])]
    xo = x1_ref[...] + g2_ref[0] * jnp.concatenate(halves, axis=1)
    ms = jnp.mean(xo * xo, axis=-1, keepdims=True)
    o_ref[...] = xo * lax.rsqrt(ms + EPS) * fn_ref[...]

    @pl.when(i == n - 1)
    def _():
        drain(lax.rem(i + 1, N_SLOTS))
        drain(ahead)


def _combine_call(cd, y, lp, x1, gates, g2, fn):
    T, d = x1.shape
    n = T // TB
    per_batch = n // g2.shape[0]
    return pl.pallas_call(
        _combine_kernel,
        grid=(n,),
        in_specs=[pl.BlockSpec((1, 1, CD_LANES), lambda i: (i, 0, 0), memory_space=pltpu.SMEM),
                  pl.BlockSpec((1, 1, CD_LANES), lambda i: (jnp.minimum(i + 1, n - 1), 0, 0),
                               memory_space=pltpu.SMEM),
                  pl.BlockSpec((1, 1, CD_LANES), lambda i: (jnp.minimum(i + 2, n - 1), 0, 0),
                               memory_space=pltpu.SMEM),
                  pl.BlockSpec(memory_space=pl.ANY),
                  pl.BlockSpec((TB, LANES), lambda i: (i, 0)),
                  pl.BlockSpec((TB, d), lambda i: (i, 0)),
                  pl.BlockSpec((TB, LANES), lambda i: (i, 0)),
                  pl.BlockSpec((1, 1, d), lambda i: (i // per_batch, 0, 0)),
                  pl.BlockSpec((1, d), lambda i: (0, 0))],
        out_specs=pl.BlockSpec((TB, d), lambda i: (i, 0)),
        out_shape=jax.ShapeDtypeStruct((T, d), F32),
        scratch_shapes=[pltpu.VMEM((N_SLOTS, RL, D_HALF), jnp.uint32), pltpu.SemaphoreType.DMA((N_SLOTS,))],
        compiler_params=_params("arbitrary"),
    )(cd, cd, cd, y, lp, x1, gates, g2, fn)


def _layer(x, c, ctx, c_ctx, w_mod, b_mod, norm1, w_in, gla_w_gk_up, gla_b_gk, gla_norm,
           ssd_conv_w, ssd_conv_b, ssd_dt_bias, ssd_A_log, ssd_D, ssd_norm, w_out,
           norm2, w_router, b_router, w_gate_up, b_gate_up, w_down, b_down, final_norm):
    bsz, L, d = x.shape
    lc = ctx.shape[1]
    assert lc == TB and L % TB == 0 and TB % GRID_W == 0

    cin = jnp.zeros((8, d), F32).at[:bsz].set(c).at[bsz].set(c_ctx)
    mod = _mod_call(cin, w_mod, b_mod.reshape(1, -1))[:bsz + 1]
    sh1, sc1, g1, sh2, sc2, g2 = [m.reshape(bsz + 1, 1, d) for m in jnp.split(mod, 6, axis=-1)]

    o = np.cumsum((0, GLA_QK, GLA_QK, GLA_V, GLA_V, GLA_RANK, SSD_INNER, SSD_CONV_DIM, SSD_HEADS))
    wq, wk, wv, wg, wlow, wz, wx, wdt = [w_in[:, int(a):int(b)] for a, b in zip(o[:-1], o[1:])]
    w_misc = jnp.concatenate([wlow, wdt, wdt, jnp.zeros((d, LANES - GLA_RANK - 2 * SSD_HEADS), F32)], axis=1)
    w_cat = jnp.concatenate([wq, wk, wv, wg, wz, wx, w_misc], axis=1).astype(BF16)
    wup = jnp.zeros((LANES, 2 * GLA_QK), F32).at[:GLA_RANK].set(
        jnp.concatenate([gla_w_gk_up[0], gla_w_gk_up[1]], axis=1)).astype(BF16)
    bup = jnp.concatenate([gla_b_gk[0], gla_b_gk[1]]).reshape(1, -1)
    dtb = jnp.zeros((1, LANES), F32).at[0, DT_F:DT_F + SSD_HEADS].set(ssd_dt_bias[0]) \
                                    .at[0, DT_B:DT_B + SSD_HEADS].set(ssd_dt_bias[1])
    q, k, v, g_all, z_all, xbc, ld, misc = _inproj_call(
        x, ctx, sh1, sc1, norm1.reshape(1, d), w_cat, wup, bup, dtb)

    xs, bc = _conv_call(xbc, ssd_conv_w.reshape(9, SSD_CONV_DIM), ssd_conv_b.reshape(1, -1), lc + L)

    a_neg = -jnp.exp(ssd_A_log.astype(F32))
    a_f = jnp.zeros((1, LANES), F32).at[0, DT_F:DT_F + SSD_HEADS].set(a_neg[0])
    a_b = jnp.zeros((1, LANES), F32).at[0, DT_B:DT_B + SSD_HEADS].set(a_neg[1])
    dvec = jnp.repeat(ssd_D, SSD_HEADDIM).reshape(1, SSD_INNER)
    o_f, o_b, y_f, y_b = _scan_call(q, k, v, ld, xs, bc, misc, a_f, a_b, dvec, L)

    wr = jnp.zeros((d, LANES), F32).at[:, :N_EXPERTS].set(w_router)
    wr_hi = wr.astype(BF16)
    wr_lo = (wr - wr_hi.astype(F32)).astype(BF16)
    br = jnp.full((1, LANES), NEG_BIG, F32).at[0, :N_EXPERTS].set(b_router)
    x1, h2, eidx, gates, counts = _outproj_call(
        o_f, o_b, g_all, y_f, y_b, z_all, x, g1[:bsz], sh2[:bsz], sc2[:bsz],
        jnp.tile(gla_norm, GLA_HEADS).reshape(1, -1), ssd_norm.reshape(1, -1), w_out.astype(BF16),
        norm2.reshape(1, d), wr_hi, wr_lo, br)

    T = bsz * L
    cnt = counts[0, :N_EXPERTS].astype(jnp.int32)
    padded = ((cnt + MOE_BLOCK - 1) // MOE_BLOCK) * MOE_BLOCK
    pend = jnp.cumsum(padded)
    pstart = pend - padded
    max_rows = T * TOP_K + (T // TB) * N_EXPERTS * (RUN - 1)
    n_blocks = -(-max_rows // MOE_BLOCK) + N_EXPERTS
    blk_start = jnp.arange(n_blocks, dtype=jnp.int32) * MOE_BLOCK
    blk_e = jnp.minimum(jnp.sum(pend[None, :] <= blk_start[:, None], axis=1), N_EXPERTS - 1).astype(jnp.int32)
    n_used = (pend[-1] // MOE_BLOCK).astype(jnp.int32)
    blk_i = jnp.arange(n_blocks, dtype=jnp.int32)
    first = (blk_i < n_used) & ((blk_i == 0) | (blk_e != jnp.roll(blk_e, 1)))
    slot = (jnp.cumsum(first) - 1) & 1
    first_pos = jnp.where(first, blk_i, n_blocks)
    next_first = jnp.roll(lax.cummin(first_pos, reverse=True), -1).at[-1].set(n_blocks)
    nxt = jnp.where(next_first < n_blocks, blk_e[jnp.minimum(next_first, n_blocks - 1)], -1)
    col = lambda v, last: jnp.concatenate([v.astype(jnp.int32), jnp.asarray([last], jnp.int32)])
    blk_info = jnp.stack([col(blk_e, 0).at[-1].set(n_used), col(first, 0), col(slot, 0), col(nxt, -1)])
    n_tail = n_blocks - (T * TOP_K) // MOE_BLOCK
    tail = n_used + jnp.arange(n_tail, dtype=jnp.int32)
    fill_blocks = jnp.concatenate([
        jnp.where(padded > 0, pend // MOE_BLOCK - 1, -1),
        jnp.where(tail < n_blocks, tail, -1)]).astype(jnp.int32)
    pstart_row = jnp.zeros((1, LANES), F32).at[0, :N_EXPERTS].set(pstart.astype(F32))
    lp, cd = _pos_call(eidx.reshape(T, LANES), pstart_row)

    xg = _dispatch_call(fill_blocks, cd, h2.reshape(T, d), lp, n_blocks * MOE_BLOCK)
    y = _moe_call(blk_info, xg, w_gate_up, b_gate_up, w_down, b_down, n_blocks)
    out = _combine_call(cd, y, lp, x1.reshape(T, d), gates.reshape(T, LANES), g2[:bsz], final_norm.reshape(1, d))
    return out.reshape(bsz, L, d)


def kernel(x, c, ctx, c_ctx, w_mod, b_mod, norm1, w_in, gla_w_gk_up, gla_b_gk, gla_norm, ssd_conv_w, ssd_conv_b, ssd_dt_bias, ssd_A_log, ssd_D, ssd_norm, w_out, norm2, w_router, b_router, w_gate_up, b_gate_up, w_down, b_down, final_norm):
    assert w_mod.shape[0] == 1, "single-layer kernel"
    return _layer(x, c, ctx, c_ctx, w_mod[0], b_mod[0], norm1[0], w_in[0], gla_w_gk_up[0], gla_b_gk[0],
                  gla_norm[0], ssd_conv_w[0], ssd_conv_b[0], ssd_dt_bias[0], ssd_A_log[0], ssd_D[0],
                  ssd_norm[0], w_out[0], norm2[0], w_router[0], b_router[0], w_gate_up[0], b_gate_up[0],
                  w_down[0], b_down[0], final_norm)
```

```python
import functools

import numpy as np
import jax
import jax.numpy as jnp
from jax import lax
from jax.experimental import pallas as pl
from jax.experimental.pallas import tpu as pltpu

F32 = jnp.float32
BF16 = jnp.bfloat16

EPS = 1e-6
GRID_W = 64
GLA_HEADS = 4
GLA_DK = 64
GLA_DV = 128
GLA_QK = GLA_HEADS * GLA_DK
GLA_V = GLA_HEADS * GLA_DV
GLA_RANK = 16
GLA_GATE_NORM = 16.0
SSD_HEADDIM = 64
SSD_INNER = 512
SSD_HEADS = 8
SSD_GROUPS = 2
SSD_HPG = 4
SSD_STATE = 128
SSD_CONV_DIM = 1024
N_EXPERTS = 32
TOP_K = 4
D_FF = 1024
SWIGLU_LIMIT = 7.0
SWIGLU_ALPHA = 1.702
MOE_BLOCK = 512

TB = 256
GLA_C = 64
GLA_STAGE_GROUP = 1
SSD_C = 128
SSD_STAGE_GROUP = 1
LANES = 128
EXP_CLAMP = 80.0
DT_F = 16
DT_B = 24
NEG_BIG = -1e30
RUN = 8
RL = -(-(TB * TOP_K + N_EXPERTS * (RUN - 1)) // LANES) * LANES
CD_LANES = -(-(RL // RUN) // LANES) * LANES
D_HALF = 512
VMEM_LIMIT = 56 * 1024 * 1024


def _dot(a, b):
    return jnp.dot(a, b, preferred_element_type=F32)


def _dot_nt(a, b):
    return lax.dot_general(a, b, (((1,), (1,)), ((), ())), preferred_element_type=F32)


def _dot_tn(a, b):
    return lax.dot_general(a, b, (((0,), (0,)), ((), ())), preferred_element_type=F32)


def _split3(a):
    hi = a.astype(BF16)
    r1 = a - hi.astype(F32)
    mid = r1.astype(BF16)
    lo = (r1 - mid.astype(F32)).astype(BF16)
    return hi, mid, lo


def _dot_exact_r(m, a):
    hi, mid, lo = _split3(a)
    return _dot(m, hi) + _dot(m, mid) + _dot(m, lo)


def _dot_hilo_r(m, a):
    hi = a.astype(BF16)
    lo = (a - hi.astype(F32)).astype(BF16)
    return _dot(m, hi) + _dot(m, lo)


def _dot_hilo_l(a, m2):
    hi = a.astype(BF16)
    lo = (a - hi.astype(F32)).astype(BF16)
    return _dot(jnp.concatenate([hi, lo], axis=1), m2)


def _sigmoid(x):
    return 1.0 / (1.0 + jnp.exp(-x))


def _softplus(x):
    return jnp.maximum(x, 0.0) + jnp.log1p(jnp.exp(-jnp.abs(x)))


def _params(*sem):
    return pltpu.CompilerParams(dimension_semantics=sem, vmem_limit_bytes=VMEM_LIMIT)


def _mod_kernel(c_ref, w_ref, b_ref, o_ref):
    c = c_ref[...]
    s = c * _sigmoid(c)
    s_hi = s.astype(BF16)
    s_lo = (s - s_hi.astype(F32)).astype(BF16)
    w = w_ref[...]
    w_hi = w.astype(BF16)
    w_lo = (w - w_hi.astype(F32)).astype(BF16)
    o_ref[...] = _dot(s_hi, w_hi) + _dot(s_lo, w_hi) + _dot(s_hi, w_lo) + b_ref[...]


def _mod_call(cin, w, b):
    rows, d = cin.shape
    n = w.shape[1]
    tn = 1536
    return pl.pallas_call(
        _mod_kernel,
        grid=(n // tn,),
        in_specs=[pl.BlockSpec((rows, d), lambda i: (0, 0)),
                  pl.BlockSpec((d, tn), lambda i: (0, i)),
                  pl.BlockSpec((1, tn), lambda i: (0, i))],
        out_specs=pl.BlockSpec((rows, tn), lambda i: (0, i)),
        out_shape=jax.ShapeDtypeStruct((rows, n), F32),
        compiler_params=_params("arbitrary"),
    )(cin, w, b)


_C_Q, _C_K, _C_V, _C_G, _C_Z, _C_X, _C_M, _C_END = 0, 256, 512, 1024, 1536, 2048, 3072, 3200


def _inproj_kernel(x0_ref, xa_ref, xb_ref, ctx_ref, shl_ref, scl_ref, shc_ref, scc_ref, n1_ref, w_ref, wup_ref,
                   bup_ref, dtb_ref, q_ref, k_ref, v_ref, g_ref, z_ref, xbc_ref, ld_ref, misc_ref, h_scr):
    j = pl.program_id(1)
    slot = j & 1

    def normmod(xv, sh_ref, sc_ref):
        ms = jnp.mean(xv * xv, axis=-1, keepdims=True)
        y = xv * lax.rsqrt(ms + EPS) * n1_ref[...]
        return (y * (1.0 + sc_ref[0]) + sh_ref[0]).astype(BF16)

    @pl.when(j == 0)
    def _():
        h_scr[0, 0:TB, :] = normmod(ctx_ref[0], shc_ref, scc_ref)
        h_scr[0, TB:2 * TB, :] = normmod(x0_ref[0], shl_ref, scl_ref)

    def mm(lo, hi):
        return _dot(h_scr[slot], w_ref[:, lo:hi])

    m = mm(_C_M, _C_END)
    zz = _dot(m.astype(BF16), wup_ref[...]) + bup_ref[...]
    ld_ref[0] = -_softplus(-zz) * (1.0 / GLA_GATE_NORM)
    misc_ref[0] = _softplus(m + dtb_ref[...])
    q_ref[0] = (mm(_C_Q, _C_K) * (GLA_DK ** -0.5)).astype(BF16)
    h_scr[1 - slot, 0:TB, :] = normmod(xa_ref[0], shl_ref, scl_ref)
    k_ref[0] = mm(_C_K, _C_V).astype(BF16)
    h_scr[1 - slot, TB:2 * TB, :] = normmod(xb_ref[0], shl_ref, scl_ref)
    v_ref[0] = mm(_C_V, _C_G).astype(BF16)
    g_ref[0] = mm(_C_G, _C_Z)
    z_ref[0] = mm(_C_Z, _C_X)
    xbc_ref[0] = mm(_C_X, _C_M)


def _inproj_call(x, ctx, sh1, sc1, n1, w_cat, wup, bup, dtb):
    bsz, L, d = x.shape
    nx = L // TB
    nj = -(-(nx + 1) // 2)
    tok = lambda n: pl.BlockSpec((1, 2 * TB, n), lambda b, j: (b, j, 0))
    const = lambda a: pl.BlockSpec(a.shape, lambda b, j: (0,) * a.ndim)
    xblk = lambda off: pl.BlockSpec((1, TB, d), lambda b, j: (b, jnp.minimum(2 * j + off, nx - 1), 0))
    first = pl.BlockSpec((1, TB, d), lambda b, j: (b, 0, 0))
    mod_lat = pl.BlockSpec((1, 1, d), lambda b, j: (b, 0, 0))
    mod_ctx = pl.BlockSpec((1, 1, d), lambda b, j: (bsz, 0, 0))
    outs = [(GLA_QK, BF16), (GLA_QK, BF16), (GLA_V, BF16), (GLA_V, F32), (SSD_INNER, F32),
            (SSD_CONV_DIM, F32), (2 * GLA_QK, F32), (LANES, F32)]
    return pl.pallas_call(
        _inproj_kernel,
        grid=(bsz, nj),
        in_specs=[first, xblk(1), xblk(2), first,
                  mod_lat, mod_lat, mod_ctx, mod_ctx,
                  const(n1), const(w_cat), const(wup), const(bup), const(dtb)],
        out_specs=[tok(n) for n, _ in outs],
        out_shape=[jax.ShapeDtypeStruct((bsz, nj * 2 * TB, n), dt) for n, dt in outs],
        scratch_shapes=[pltpu.VMEM((2, 2 * TB, d), BF16)],
        compiler_params=_params("arbitrary", "arbitrary"),
    )(x, x, x, ctx, sh1, sc1, sh1, sc1, n1, w_cat, wup, bup, dtb)


_EXT_PAD = 8
_EXT_BASE = _EXT_PAD + GRID_W
_EXT_ROWS = 2 * _EXT_PAD + 2 * GRID_W + TB


def _conv_kernel(prev_ref, cur_ref, next_ref, w_ref, b_ref, xs_ref, bc_ref, ext_s, xl_all, xr_all):
    for bb in range(cur_ref.shape[0]):
        _conv_block(prev_ref.at[bb], cur_ref.at[bb], next_ref.at[bb], w_ref, b_ref, xs_ref.at[bb], bc_ref.at[bb],
                    ext_s.at[bb], xl_all.at[bb], xr_all.at[bb])


def _conv_block(prev_ref, cur_ref, next_ref, w_ref, b_ref, xs_ref, bc_ref, ext, xl_s, xr_s):
    j = pl.program_id(1)
    nj = pl.num_programs(1)
    is_ctx = j == 0
    zpad = jnp.zeros((_EXT_PAD, SSD_CONV_DIM), F32)
    ext[0:_EXT_PAD, :] = zpad
    ext[_EXT_ROWS - _EXT_PAD:_EXT_ROWS, :] = zpad
    ext[_EXT_PAD:_EXT_BASE, :] = jnp.where(j >= 2, prev_ref[...], 0.0)
    ext[_EXT_BASE:_EXT_BASE + TB, :] = cur_ref[...]
    ext[_EXT_BASE + TB:_EXT_BASE + TB + GRID_W, :] = jnp.where(
        jnp.logical_and(j >= 1, j <= nj - 2), next_ref[...], 0.0)

    win = TB + 2 * GRID_W
    u = lax.broadcasted_iota(jnp.int32, (win, LANES), 0)
    pos = jnp.where(is_ctx, u - GRID_W, u & (GRID_W - 1))
    ok_l = pos >= 1
    ok_r = jnp.where(is_ctx, pos - (TB - GRID_W), pos) <= GRID_W - 2
    lat = jnp.where(is_ctx, 0.0, 1.0)
    side = {-1: xl_s, 0: None, 1: xr_s}

    for c in range(SSD_CONV_DIM // LANES):
        lo, hi = c * LANES, (c + 1) * LANES
        xl_s[:, lo:hi] = jnp.where(ok_l, ext[_EXT_PAD - 1:_EXT_PAD - 1 + win, lo:hi], 0.0)
        xr_s[:, lo:hi] = jnp.where(ok_r, ext[_EXT_PAD + 1:_EXT_PAD + 1 + win, lo:hi], 0.0)
        acc = jnp.zeros((TB, LANES), F32)
        for dr in (-1, 0, 1):
            for dc in (-1, 0, 1):
                if dc == 0:
                    start = _EXT_BASE + GRID_W * dr
                    tap = ext[start:start + TB, lo:hi]
                else:
                    start = GRID_W + GRID_W * dr
                    tap = side[dc][start:start + TB, lo:hi]
                wi = 3 * (dr + 1) + (dc + 1)
                wv = w_ref[wi:wi + 1, lo:hi]
                if dr != 0:
                    wv = wv * lat
                acc = acc + tap * wv
        y = acc + b_ref[:, lo:hi]
        y = y * _sigmoid(y)
        if c < SSD_INNER // LANES:
            xs_ref[:, lo:hi] = y
        else:
            bc_ref[:, lo - SSD_INNER:hi - SSD_INNER] = y.astype(BF16)


def _conv_call(xbc, w9, bias, ls):
    bsz, _, ch = xbc.shape
    nj = ls // TB
    rpb = TB // GRID_W
    nrow = ls // GRID_W
    nb = _scan_batch(bsz)
    return pl.pallas_call(
        _conv_kernel,
        grid=(bsz // nb, nj),
        in_specs=[pl.BlockSpec((nb, GRID_W, ch), lambda b, j: (b, jnp.maximum(rpb * j - 1, 0), 0)),
                  pl.BlockSpec((nb, TB, ch), lambda b, j: (b, j, 0)),
                  pl.BlockSpec((nb, GRID_W, ch), lambda b, j: (b, jnp.minimum(rpb * j + rpb, nrow - 1), 0)),
                  pl.BlockSpec((9, ch), lambda b, j: (0, 0)),
                  pl.BlockSpec((1, ch), lambda b, j: (0, 0))],
        out_specs=[pl.BlockSpec((nb, TB, SSD_INNER), lambda b, j: (b, j, 0)),
                   pl.BlockSpec((nb, TB, ch - SSD_INNER), lambda b, j: (b, j, 0))],
        out_shape=[jax.ShapeDtypeStruct((bsz, ls, SSD_INNER), F32),
                   jax.ShapeDtypeStruct((bsz, ls, ch - SSD_INNER), BF16)],
        scratch_shapes=[pltpu.VMEM((nb, _EXT_ROWS, ch), F32), pltpu.VMEM((nb, TB + 2 * GRID_W, ch), F32),
                        pltpu.VMEM((nb, TB + 2 * GRID_W, ch), F32)],
        compiler_params=_params("arbitrary", "arbitrary"),
    )(xbc, xbc, xbc, w9, bias)


def _fwd_blk(s):
    return s


def _bwd_blk(s, ns):
    return jnp.where(s == 0, 0, ns - s)


def _scan_batch(bsz):
    return 2 if bsz % 2 == 0 else 1


class _GlaChunk:
    def __init__(self, q, k, v, la, st_ref, tri_m, fwd, store):
        self.q, self.k, self.v, self.la, self.st_ref, self.tri_m, self.fwd, self.store = (
            q, k, v, la, st_ref, tri_m, fwd, store)

    def stage_sums(self):
        self.b = _dot_hilo_r(self.tri_m, self.la)

    def stage_factors(self):
        C = GLA_C
        b = self.b
        self.bt = b[C - 1:C, :] if self.fwd else b[0:1, :]
        r = 0.5 * self.bt
        self.er = jnp.exp(r)
        self.qt = (self.q.astype(F32) * jnp.exp(jnp.minimum(b - r, EXP_CLAMP))).astype(BF16)
        kt = (self.k.astype(F32) * jnp.exp(jnp.minimum(r - b, EXP_CLAMP))).astype(BF16)
        head_k = lax.broadcasted_iota(jnp.int32, (C, GLA_QK), 1) >> 6
        zero = jnp.zeros_like(kt)
        self.kh = [jnp.where(head_k == h, kt, zero) for h in range(GLA_HEADS)]
        self.qh = [jnp.where(head_k == h, self.qt, zero) for h in range(GLA_HEADS)]

    def stage_products(self):
        v = self.v
        kcat = jnp.concatenate(self.kh, axis=0)
        self.sc = _dot_nt(self.qt, kcat)
        vcat = jnp.concatenate([v[:, h * GLA_DV:(h + 1) * GLA_DV] for h in range(GLA_HEADS)], axis=0)
        self.u = _dot_tn(vcat, kcat) * self.er

    def stage_mask(self):
        C = GLA_C
        ii = lax.broadcasted_iota(jnp.int32, (C, GLA_HEADS * C), 0)
        jj = lax.broadcasted_iota(jnp.int32, (C, GLA_HEADS * C), 1) & (C - 1)
        causal = (jj <= ii) if self.fwd else (jj >= ii)
        self.p = jnp.where(causal, self.sc, 0.0).astype(BF16)
        v = self.v
        head_v = lax.broadcasted_iota(jnp.int32, (C, GLA_V), 1) >> 7
        self.vst = jnp.concatenate([jnp.where(head_v == h, v, jnp.zeros_like(v)) for h in range(GLA_HEADS)], axis=0)

    def stage_intra(self):
        self.o = _dot(self.p, self.vst)

    def stage_inter(self):
        st = self.st_ref[...]
        ster = (st * self.er).astype(BF16)
        self.st_ref[...] = st * jnp.exp(self.bt) + self.u
        res = _dot_nt(jnp.concatenate(self.qh, axis=0), ster)
        self.inter = jnp.concatenate([res[h * GLA_C:(h + 1) * GLA_C, :] for h in range(GLA_HEADS)], axis=1)

    def stage_out(self):
        self.store(self.o + self.inter)


def _store_to(ref, bb, sl):
    def store(val):
        ref[bb, sl, :] = val
    return store


def _gla_schedule(qf_ref, kf_ref, vf_ref, lf_ref, qb_ref, kb_ref, vb_ref, lb_ref, trif_ref, trib_ref,
                  of_ref, ob_ref, stf, stb):
    nsub = TB // GLA_C
    store_to = _store_to
    calls = []

    group = GLA_STAGE_GROUP
    for g0 in range(0, nsub, group):
        steps = []
        for i in range(g0, g0 + group):
            sf = pl.ds(i * GLA_C, GLA_C)
            sb = pl.ds((nsub - 1 - i) * GLA_C, GLA_C)
            chunks = []
            for bb in range(qf_ref.shape[0]):
                chunks.append(_GlaChunk(qf_ref[bb, sf, :], kf_ref[bb, sf, :], vf_ref[bb, sf, :], lf_ref[bb, sf, :],
                                        stf.at[bb], trif_ref[...], True, store_to(of_ref, bb, sf)))
                chunks.append(_GlaChunk(qb_ref[bb, sb, :], kb_ref[bb, sb, :], vb_ref[bb, sb, :], lb_ref[bb, sb, :],
                                        stb.at[bb], trib_ref[...], False, store_to(ob_ref, bb, sb)))
            steps.append(chunks)
        for stage in ("stage_sums", "stage_factors", "stage_products", "stage_mask", "stage_intra"):
            for chunks in steps:
                calls += [getattr(ch, stage) for ch in chunks]
        for chunks in steps:
            for stage in ("stage_inter", "stage_out"):
                calls += [getattr(ch, stage) for ch in chunks]
    return calls


class _SsdChunk:
    def __init__(self, xs, bc, dtm, avec, dvec, st_ref, tri_m, e_m, base, fwd, store):
        self.xs, self.bc, self.dtm, self.avec, self.dvec, self.st_ref = xs, bc, dtm, avec, dvec, st_ref
        self.tri_m, self.e_m, self.base, self.fwd, self.store = tri_m, e_m, base, fwd, store

    def stage_sums(self):
        self.dt_exp = _dot_hilo_l(self.dtm, self.e_m)
        self.acum = _dot_exact_r(self.tri_m, self.dtm * self.avec)

    def stage_expand(self):
        self.acum_exp = _dot_hilo_l(self.acum, self.e_m)
        self.acum_t = self.acum.T
        self.xdt = self.xs * self.dt_exp
        bc = self.bc
        self.bg = [bc[:, 128 * g:128 * (g + 1)] for g in range(SSD_GROUPS)]
        self.cg = [bc[:, 256 + 128 * g:256 + 128 * (g + 1)] for g in range(SSD_GROUPS)]
        self.cb = [_dot_nt(self.cg[g], self.bg[g]) for g in range(SSD_GROUPS)]

    def stage_decay(self):
        C = SSD_C
        ii = lax.broadcasted_iota(jnp.int32, (C, C), 0)
        jj = lax.broadcasted_iota(jnp.int32, (C, C), 1)
        tri = (jj <= ii) if self.fwd else (jj >= ii)
        self.ms = []
        for g in range(SSD_GROUPS):
            for rr in range(SSD_HPG):
                ln = self.base + SSD_HPG * g + rr
                diff = self.acum[:, ln:ln + 1] - self.acum_t[ln:ln + 1, :]
                seg = jnp.where(tri, jnp.exp(jnp.minimum(diff, 0.0)), 0.0)
                self.ms.append((self.cb[g] * seg).astype(BF16))
        ae = self.acum_exp
        self.al_exp = ae[C - 1:C, :] if self.fwd else ae[0:1, :]
        self.xw = (self.xdt * jnp.exp(self.al_exp - ae)).astype(BF16)
        xdt_b = self.xdt.astype(BF16)
        gw = SSD_HPG * SSD_HEADDIM
        head = lax.broadcasted_iota(jnp.int32, (C, gw), 1) >> 6
        zero = jnp.zeros((C, gw), BF16)
        self.xh = [jnp.where(head == h % SSD_HPG, xdt_b[:, gw * (h // SSD_HPG):gw * (h // SSD_HPG + 1)], zero)
                   for h in range(SSD_HEADS)]

    def stage_products(self):
        self.yg, self.ug = [], []
        for g in range(SSD_GROUPS):
            gl, gh = 256 * g, 256 * (g + 1)
            yg = _dot(self.ms[SSD_HPG * g], self.xh[SSD_HPG * g])
            for rr in range(1, SSD_HPG):
                yg = yg + _dot(self.ms[SSD_HPG * g + rr], self.xh[SSD_HPG * g + rr])
            self.yg.append(yg)
            self.ug.append(_dot_tn(self.bg[g], self.xw[:, gl:gh]))

    def stage_state(self):
        ys = []
        for g in range(SSD_GROUPS):
            gl, gh = 256 * g, 256 * (g + 1)
            sg = self.st_ref[g]
            yoff = _dot(self.cg[g], sg.astype(BF16)) * jnp.exp(self.acum_exp[:, gl:gh])
            self.st_ref[g] = sg * jnp.exp(self.al_exp[:, gl:gh]) + self.ug[g]
            ys.append(self.yg[g] + yoff)
        y = jnp.concatenate(ys, axis=1)
        if self.dvec is not None:
            y = y + self.dvec * self.xs
        self.store(y)


def _ssd_schedule(xf_ref, bcf_ref, mf_ref, xb_ref, bcb_ref, mb_ref, af_ref, ab_ref, d_ref,
                  trif_ref, trib_ref, ef_ref, eb_ref, yf_ref, yb_ref, stf, stb):
    nsub = TB // SSD_C
    store_to = _store_to
    calls = []
    steps = []
    for i in range(nsub):
        sf = pl.ds(i * SSD_C, SSD_C)
        sb = pl.ds((nsub - 1 - i) * SSD_C, SSD_C)
        chunks = []
        for bb in range(xf_ref.shape[0]):
            chunks.append(_SsdChunk(xf_ref[bb, sf, :], bcf_ref[bb, sf, :], mf_ref[bb, sf, :], af_ref[...],
                                    d_ref[...], stf.at[bb], trif_ref[...], ef_ref[...], DT_F, True,
                                    store_to(yf_ref, bb, sf)))
            chunks.append(_SsdChunk(xb_ref[bb, sb, :], bcb_ref[bb, sb, :], mb_ref[bb, sb, :], ab_ref[...],
                                    None, stb.at[bb], trib_ref[...], eb_ref[...], DT_B, False,
                                    store_to(yb_ref, bb, sb)))
        steps.append(chunks)
    for g0 in range(0, nsub, SSD_STAGE_GROUP):
        group = steps[g0:g0 + SSD_STAGE_GROUP]
        for stage in ("stage_sums", "stage_expand", "stage_decay", "stage_products"):
            for chunks in group:
                calls += [getattr(ch, stage) for ch in chunks]
        for chunks in group:
            calls += [ch.stage_state for ch in chunks]
    return calls


N_GLA_IN, N_SSD_IN = 10, 13


def _scan_kernel(*refs):
    gla_in, ssd_in = refs[:N_GLA_IN], refs[N_GLA_IN:N_GLA_IN + N_SSD_IN]
    of_ref, ob_ref, yf_ref, yb_ref, gstf, gstb, sstf, sstb = refs[N_GLA_IN + N_SSD_IN:]

    @pl.when(pl.program_id(1) == 0)
    def _():
        for st in (gstf, gstb, sstf, sstb):
            st[...] = jnp.zeros_like(st)

    gla = _gla_schedule(*gla_in, of_ref, ob_ref, gstf, gstb)
    ssd = _ssd_schedule(*ssd_in, yf_ref, yb_ref, sstf, sstb)
    merged = sorted([((n + 0.5) / len(gla), 0, n, c) for n, c in enumerate(gla)] +
                    [((n + 0.5) / len(ssd), 1, n, c) for n, c in enumerate(ssd)], key=lambda t: t[:3])
    for _, _, _, call in merged:
        call()


def _expand_matrix(base):
    e = np.zeros((LANES, SSD_INNER), np.float32)
    for h in range(SSD_HEADS):
        e[base + h, SSD_HEADDIM * h:SSD_HEADDIM * (h + 1)] = 1.0
    return jnp.asarray(np.concatenate([e, e], axis=0), BF16)


def _scan_call(q, k, v, ld, xs, bc, misc, a_f, a_b, dvec, L):
    bsz = q.shape[0]
    nx = L // TB
    ns = nx + 1
    tri = lambda c, up: jnp.asarray((np.triu if up else np.tril)(np.ones((c, c), np.float32)), BF16)
    g_trif, g_trib, s_trif, s_trib = tri(GLA_C, False), tri(GLA_C, True), tri(SSD_C, False), tri(SSD_C, True)
    ef, eb = _expand_matrix(DT_F), _expand_matrix(DT_B)
    nb = _scan_batch(bsz)
    f = lambda n, lane=0: pl.BlockSpec((nb, TB, n), lambda b, s: (b, _fwd_blk(s), lane))
    r = lambda n, lane=0: pl.BlockSpec((nb, TB, n), lambda b, s: (b, _bwd_blk(s, ns), lane))
    const = lambda a: pl.BlockSpec(a.shape, lambda b, s: (0,) * a.ndim)
    out_f = lambda n: pl.BlockSpec((nb, TB, n), lambda b, s: (b, jnp.maximum(s - 1, 0), 0))
    out_b = lambda n: pl.BlockSpec((nb, TB, n), lambda b, s: (b, jnp.where(s == 0, nx - 1, nx - s), 0))
    gla_specs = [f(GLA_QK), f(GLA_QK), f(GLA_V), f(GLA_QK, 0), r(GLA_QK), r(GLA_QK), r(GLA_V), r(GLA_QK, 1),
                 const(g_trif), const(g_trib)]
    ssd_specs = [f(SSD_INNER), f(512), f(LANES), r(SSD_INNER), r(512), r(LANES),
                 const(a_f), const(a_b), const(dvec), const(s_trif), const(s_trib), const(ef), const(eb)]
    assert len(gla_specs) == N_GLA_IN and len(ssd_specs) == N_SSD_IN
    return pl.pallas_call(
        _scan_kernel,
        grid=(bsz // nb, ns),
        in_specs=gla_specs + ssd_specs,
        out_specs=[out_f(GLA_V), out_b(GLA_V), out_f(SSD_INNER), out_b(SSD_INNER)],
        out_shape=[jax.ShapeDtypeStruct((bsz, L, GLA_V), F32)] * 2 +
                  [jax.ShapeDtypeStruct((bsz, L, SSD_INNER), F32)] * 2,
        scratch_shapes=[pltpu.VMEM((nb, GLA_DV, GLA_QK), F32)] * 2 +
                       [pltpu.VMEM((nb, SSD_GROUPS, SSD_STATE, SSD_HPG * SSD_HEADDIM), F32)] * 2,
        compiler_params=_params("arbitrary", "arbitrary"),
    )(q, k, v, ld, q, k, v, ld, g_trif, g_trib, xs, bc, misc, xs, bc, misc, a_f, a_b, dvec, s_trif, s_trib, ef, eb)


def _outproj_kernel(of_ref, ob_ref, ga_ref, gb_ref, yf_ref, yb_ref, za_ref, zb_ref, x_ref, g1_ref, sh2_ref, sc2_ref,
                    gn_ref, sn_ref, wo_ref, n2_ref, wrh_ref, wrl_ref, br_ref,
                    x1_ref, h2_ref, eidx_ref, gate_ref, cnt_ref, mix_s, hl_s):
    first = jnp.logical_and(pl.program_id(0) == 0, pl.program_id(1) == 0)

    @pl.when(first)
    def _():
        cnt_ref[...] = jnp.zeros_like(cnt_ref)

    halves = [(pl.ds(0, TB), ga_ref, za_ref), (pl.ds(TB, TB), gb_ref, zb_ref)]

    for rows, g_ref, z_ref in halves:
        o = of_ref[0, rows, :] + ob_ref[0, rows, :]
        gg = g_ref[0]
        for h in range(GLA_HEADS):
            lo, hi = GLA_DV * h, GLA_DV * (h + 1)
            oh = o[:, lo:hi]
            ms = jnp.mean(oh * oh, axis=-1, keepdims=True)
            gh = gg[:, lo:hi]
            mix_s[rows, lo:hi] = (oh * lax.rsqrt(ms + EPS) * gn_ref[:, lo:hi] * (gh * _sigmoid(gh))).astype(BF16)
        zz = z_ref[0]
        u = (yf_ref[0, rows, :] + yb_ref[0, rows, :]) * (zz * _sigmoid(zz))
        gw = SSD_INNER // SSD_GROUPS
        for g in range(SSD_GROUPS):
            lo, hi = gw * g, gw * (g + 1)
            ug = u[:, lo:hi]
            ms = jnp.mean(ug * ug, axis=-1, keepdims=True)
            mix_s[rows, GLA_V + lo:GLA_V + hi] = (ug * lax.rsqrt(ms + EPS) * sn_ref[:, lo:hi]).astype(BF16)

    x1_ref[0] = x_ref[0] + g1_ref[0] * _dot(mix_s[...], wo_ref[...])

    for rows, _, _ in halves:
        x1 = x1_ref[0, rows, :]
        ms = jnp.mean(x1 * x1, axis=-1, keepdims=True)
        h2 = x1 * lax.rsqrt(ms + EPS) * (n2_ref[...] * (1.0 + sc2_ref[0])) + sh2_ref[0]
        h_hi = h2.astype(BF16)
        h2_ref[0, rows, :] = h_hi
        hl_s[rows, :] = (h2 - h_hi.astype(F32)).astype(BF16)

    h_hi = h2_ref[0]
    logits = (_dot(h_hi, wrh_ref[...]) + _dot(hl_s[...], wrh_ref[...]) + _dot(h_hi, wrl_ref[...])) + br_ref[...]

    lane = lax.broadcasted_iota(jnp.int32, (TB, LANES), 1).astype(F32)
    st = [dict(work=logits[i * TB:(i + 1) * TB, :], eidx=jnp.full((TB, LANES), -1.0, F32),
               gates=jnp.zeros((TB, LANES), F32), sel=jnp.zeros((TB, LANES), F32), m0=None,
               den=jnp.zeros((TB, 1), F32)) for i in range(2)]
    for kk in range(TOP_K):
        for t in st:
            t["m"] = jnp.max(t["work"], axis=-1, keepdims=True)
        for t in st:
            t["idx"] = jnp.min(jnp.where(t["work"] == t["m"], lane, float(LANES)), axis=-1, keepdims=True)
        for t in st:
            hit = lane == t["idx"]
            if t["m0"] is None:
                t["m0"] = t["m"]
            e = jnp.exp(t["m"] - t["m0"])
            t["den"] = t["den"] + e
            t["eidx"] = jnp.where(lane == float(kk), t["idx"], t["eidx"])
            t["gates"] = jnp.where(lane == float(kk), e, t["gates"])
            t["sel"] = jnp.where(hit, 1.0, t["sel"])
            t["work"] = jnp.where(hit, NEG_BIG, t["work"])
    for (rows, _, _), t in zip(halves, st):
        eidx_ref[0, rows, :] = t["eidx"].astype(jnp.int32)
        gate_ref[0, rows, :] = t["gates"] / t["den"]
        cnt = jnp.sum(t["sel"], axis=0, keepdims=True)
        cnt_ref[...] += jnp.floor((cnt + (RUN - 1.0)) * (1.0 / RUN)) * RUN


def _outproj_call(o_f, o_b, g_all, y_f, y_b, z_all, x, g1, sh2, sc2, gn, sn, wo, n2, wr_hi, wr_lo, br):
    bsz, L, d = x.shape
    nj = L // (2 * TB)
    tok = lambda n: pl.BlockSpec((1, 2 * TB, n), lambda b, j: (b, j, 0))
    tok_off = lambda n, half: pl.BlockSpec((1, TB, n), lambda b, j: (b, 2 * j + 1 + half, 0))
    const = lambda a: pl.BlockSpec(a.shape, lambda b, j: (0,) * a.ndim)
    mod = pl.BlockSpec((1, 1, d), lambda b, j: (b, 0, 0))
    return pl.pallas_call(
        _outproj_kernel,
        grid=(bsz, nj),
        in_specs=[tok(GLA_V), tok(GLA_V), tok_off(GLA_V, 0), tok_off(GLA_V, 1),
                  tok(SSD_INNER), tok(SSD_INNER), tok_off(SSD_INNER, 0), tok_off(SSD_INNER, 1),
                  tok(d), mod, mod, mod, const(gn), const(sn), const(wo), const(n2),
                  const(wr_hi), const(wr_lo), const(br)],
        out_specs=[tok(d), tok(d), tok(LANES), tok(LANES), pl.BlockSpec((1, LANES), lambda b, j: (0, 0))],
        out_shape=[jax.ShapeDtypeStruct((bsz, L, d), F32), jax.ShapeDtypeStruct((bsz, L, d), BF16),
                   jax.ShapeDtypeStruct((bsz, L, LANES), jnp.int32), jax.ShapeDtypeStruct((bsz, L, LANES), F32),
                   jax.ShapeDtypeStruct((1, LANES), F32)],
        scratch_shapes=[pltpu.VMEM((2 * TB, d), BF16), pltpu.VMEM((2 * TB, d), BF16)],
        compiler_params=_params("arbitrary", "arbitrary"),
    )(o_f, o_b, g_all, g_all, y_f, y_b, z_all, z_all, x, g1, sh2, sc2, gn, sn, wo, n2, wr_hi, wr_lo, br)


def _pos_kernel(eidx_ref, pstart_ref, lst_ref, ust_ref, lp_ref, cd_ref, carry):
    @pl.when(pl.program_id(0) == 0)
    def _():
        carry[...] = pstart_ref[...]

    lane = lax.broadcasted_iota(jnp.int32, (TB, LANES), 1)
    tiles = []
    for t in range(cd_ref.shape[0]):
        eidx = eidx_ref[t * TB:(t + 1) * TB, :]
        hits = [lane == eidx[:, kk:kk + 1] for kk in range(TOP_K)]
        sel = jnp.zeros((TB, LANES), F32)
        for hmask in hits:
            sel = jnp.where(hmask, 1.0, sel)
        cnt = jnp.sum(sel, axis=0, keepdims=True)
        run = jnp.floor((cnt + (RUN - 1.0)) * (1.0 / RUN)) * RUN
        tiles.append(dict(hits=hits, sel=sel, run=run))
    for tl in tiles:
        tl["rank"] = _dot(lst_ref[...], tl["sel"].astype(BF16))
        tl["loff"] = _dot(jnp.broadcast_to(tl["run"], (8, LANES)).astype(BF16), ust_ref[...])[0:1]
    for t, tl in enumerate(tiles):
        pos = tl["loff"] + tl["rank"]
        lp = jnp.zeros((TB, LANES), jnp.int32)
        for kk, hmask in enumerate(tl["hits"]):
            lk = jnp.sum(jnp.where(hmask, pos, 0.0), axis=-1, keepdims=True)
            lp = jnp.where(lane == kk, lk.astype(jnp.int32), lp)
        lp_ref[t * TB:(t + 1) * TB, :] = lp

    eye = lax.broadcasted_iota(jnp.int32, (LANES, LANES), 0) == lax.broadcasted_iota(jnp.int32, (LANES, LANES), 1)
    col = lambda v: jnp.sum(jnp.where(eye, v, 0.0), axis=1, keepdims=True)
    row0 = (lax.broadcasted_iota(jnp.int32, (LANES, CD_LANES), 1) * RUN).astype(F32)
    base = carry[...]
    for t, tl in enumerate(tiles):
        loff_c, run_c, shift_c = col(tl["loff"]), col(tl["run"]), col(base - tl["loff"])
        inside = jnp.where(row0 >= loff_c, 1.0, 0.0) * jnp.where(row0 < loff_c + run_c, 1.0, 0.0)
        valid = jnp.sum(inside, axis=0, keepdims=True)
        dest = jnp.sum(inside * shift_c, axis=0, keepdims=True) + row0[0:1]
        cd_ref[t] = jnp.where(valid > 0.0, dest, -1.0).astype(jnp.int32)
        base = base + tl["run"]
    carry[...] = base


def _pos_call(eidx, pstart):
    T = eidx.shape[0]
    lst = jnp.asarray(np.tril(np.ones((TB, TB), np.float32), -1), BF16)
    ust = jnp.asarray(np.triu(np.ones((LANES, LANES), np.float32), 1), BF16)
    tps = 2 if (T // TB) % 2 == 0 else 1
    return pl.pallas_call(
        _pos_kernel,
        grid=(T // (tps * TB),),
        in_specs=[pl.BlockSpec((tps * TB, LANES), lambda i: (i, 0)),
                  pl.BlockSpec((1, LANES), lambda i: (0, 0)),
                  pl.BlockSpec((TB, TB), lambda i: (0, 0)),
                  pl.BlockSpec((LANES, LANES), lambda i: (0, 0))],
        out_specs=[pl.BlockSpec((tps * TB, LANES), lambda i: (i, 0)),
                   pl.BlockSpec((tps, 1, CD_LANES), lambda i: (i, 0, 0))],
        out_shape=[jax.ShapeDtypeStruct((T, LANES), jnp.int32),
                   jax.ShapeDtypeStruct((T // TB, 1, CD_LANES), jnp.int32)],
        scratch_shapes=[pltpu.VMEM((1, LANES), F32)],
        compiler_params=_params("arbitrary"),
    )(eidx, pstart, lst, ust)


def _pair_matrix(lp, weights):
    j = lax.broadcasted_iota(jnp.int32, (TB, RL), 1)
    m = jnp.zeros((TB, RL), F32)
    for kk in range(TOP_K):
        w = 1.0 if weights is None else weights[:, kk:kk + 1]
        m = jnp.where(j == lp[:, kk:kk + 1], w, m)
    return m


def _pack_bf16_pairs(v, is_bf16_valued=False):
    if not is_bf16_valued:
        v = v.astype(BF16).astype(F32)
    bits = pltpu.bitcast(v, jnp.uint32)
    return bits[:, D_HALF:] | (bits[:, :D_HALF] >> 16)


def _unpack_bf16_pairs(w):
    lo = pltpu.bitcast(w << 16, F32).astype(BF16)
    hi = pltpu.bitcast(w & jnp.uint32(0xFFFF0000), F32).astype(BF16)
    return lo, hi


N_SLOTS = 3


def _dispatch_kernel(fill_ref, cdp_ref, cdc_ref, h_ref, lp_ref, xg_ref, sorted_s, sems, fill_sem):
    i = pl.program_id(0)
    n = pl.num_programs(0)
    slot = lax.rem(i, N_SLOTS)
    prev = lax.rem(i + N_SLOTS - 1, N_SLOTS)
    pprev = lax.rem(i + N_SLOTS - 2, N_SLOTS)
    spill0 = xg_ref.shape[0] - N_SLOTS * RL

    def start(cref, s, c, to_spill, sem):
        src = RUN * c if isinstance(c, int) else pl.multiple_of(RUN * c, RUN)
        d = cref[0, 0, c]
        d = pl.multiple_of(jnp.where(jnp.logical_or(d < 0, to_spill), spill0 + s * RL + src, d), RUN)
        pltpu.make_async_copy(sorted_s.at[s, pl.ds(src, RUN), :], xg_ref.at[pl.ds(d, RUN), :], sem).start()

    def drain(s, sem):
        pltpu.make_async_copy(sorted_s.at[s], xg_ref.at[pl.ds(0, RL), :], sem).wait()

    @pl.when(i == 0)
    def _():
        sorted_s[...] = jnp.zeros_like(sorted_s)
        for s in range(N_SLOTS):
            lax.fori_loop(0, RL // RUN, lambda c, carry, s=s: (start(cdp_ref, s, c, True, fill_sem), carry)[1], 0)
        for s in range(N_SLOTS):
            drain(s, fill_sem)

    for c in range(RL // RUN):
        start(cdp_ref, prev, c, i == 0, sems.at[prev])

    sorted_s[slot] = _pack_bf16_pairs(_dot_tn(_pair_matrix(lp_ref[...], None).astype(BF16), h_ref[...]),
                                      is_bf16_valued=True)

    @pl.when(i == 0)
    def _():
        def fill(k):
            blk = pl.multiple_of(fill_ref[k] * MOE_BLOCK, MOE_BLOCK)
            return pltpu.make_async_copy(sorted_s.at[0, pl.ds(0, MOE_BLOCK), :],
                                         xg_ref.at[pl.ds(blk, MOE_BLOCK), :], fill_sem)

        for k in range(fill_ref.shape[0]):
            pl.when(fill_ref[k] >= 0)(lambda k=k: fill(k).start())
        for k in range(fill_ref.shape[0]):
            pl.when(fill_ref[k] >= 0)(lambda k=k: fill(k).wait())

    pl.when(i >= 1)(lambda: drain(pprev, sems.at[pprev]))

    @pl.when(i == n - 1)
    def _():
        lax.fori_loop(0, RL // RUN, lambda c, carry: (start(cdc_ref, slot, c, False, sems.at[slot]), carry)[1], 0,
                      unroll=4)
        drain(prev, sems.at[prev])
        drain(slot, sems.at[slot])


def _dispatch_call(fill_blocks, cd, h2, lp, P):
    T, d = h2.shape
    grid_spec = pltpu.PrefetchScalarGridSpec(
        num_scalar_prefetch=1,
        grid=(T // TB,),
        in_specs=[pl.BlockSpec((1, 1, CD_LANES), lambda i, fb: (jnp.maximum(i - 1, 0), 0, 0),
                               memory_space=pltpu.SMEM),
                  pl.BlockSpec((1, 1, CD_LANES), lambda i, fb: (i, 0, 0), memory_space=pltpu.SMEM),
                  pl.BlockSpec((TB, d), lambda i, fb: (i, 0)),
                  pl.BlockSpec((TB, LANES), lambda i, fb: (i, 0))],
        out_specs=pl.BlockSpec(memory_space=pl.ANY),
        scratch_shapes=[pltpu.VMEM((N_SLOTS, RL, D_HALF), jnp.uint32), pltpu.SemaphoreType.DMA((N_SLOTS,)),
                        pltpu.SemaphoreType.DMA(())],
    )
    return pl.pallas_call(
        _dispatch_kernel,
        grid_spec=grid_spec,
        out_shape=jax.ShapeDtypeStruct((P + N_SLOTS * RL, D_HALF), jnp.uint32),
        compiler_params=_params("arbitrary"),
    )(fill_blocks, cd, cd, h2, lp)


def _moe_kernel(nb, info_ref, x_hbm, wgu_hbm, bgu_ref, wd_hbm, bd_ref, y_hbm,
                wgu_f, wd_f, wgu_b, wd_b, act_s, xbuf, ybuf, zbuf, wsems, xsems, ysems, zsem):
    n_used = info_ref[0, nb]

    def rows(blk):
        return pl.ds(pl.multiple_of(blk * MOE_BLOCK, MOE_BLOCK), MOE_BLOCK)

    def x_copy(blk, slot):
        return pltpu.make_async_copy(x_hbm.at[rows(blk), :], xbuf.at[slot], xsems.at[slot])

    def y_copy(blk, slot):
        return pltpu.make_async_copy(ybuf.at[slot], y_hbm.at[rows(blk), :], ysems.at[slot])

    def z_copy(blk):
        return pltpu.make_async_copy(zbuf, y_hbm.at[rows(blk), :], zsem)

    def fetch(expert, slot):
        return (pltpu.make_async_copy(wgu_hbm.at[expert], wgu_f.at[slot], wsems.at[0, slot]),
                pltpu.make_async_copy(wd_hbm.at[expert], wd_f.at[slot], wsems.at[1, slot]))

    zbuf[...] = jnp.zeros_like(zbuf)
    lax.fori_loop(n_used, nb, lambda blk, c: (z_copy(blk).start(), c)[1], 0)

    for cp in fetch(info_ref[0, 0], 0):
        cp.start()
    x_copy(0, 0).start()

    def body(i, carry):
        slot = i & 1
        e = info_ref[0, i]

        @pl.when(info_ref[1, i] == 1)
        def _():
            wslot = info_ref[2, i]
            nxt = info_ref[3, i]
            cps = fetch(e, wslot)
            cps[0].wait()
            wgu_b[...] = wgu_f[wslot].astype(BF16)
            cps[1].wait()
            wd_b[...] = wd_f[wslot].astype(BF16)

            @pl.when(nxt >= 0)
            def _():
                for cp in fetch(nxt, 1 - wslot):
                    cp.start()

        x_copy(i, slot).wait()
        pl.when(i + 1 < n_used)(lambda: x_copy(i + 1, 1 - slot).start())
        pl.when(i >= 2)(lambda: y_copy(i - 2, slot).wait())

        bgu = bgu_ref[pl.ds(e, 1), :]
        xb = jnp.concatenate(_unpack_bf16_pairs(xbuf[slot]), axis=1)
        cw = 256
        for c in range(D_FF // cw):
            lo, hi = c * cw, (c + 1) * cw
            gate = _dot(xb, wgu_b[:, lo:hi]) + bgu[:, lo:hi]
            up = _dot(xb, wgu_b[:, D_FF + lo:D_FF + hi]) + bgu[:, D_FF + lo:D_FF + hi]
            gate = jnp.minimum(gate, SWIGLU_LIMIT)
            up = jnp.clip(up, -SWIGLU_LIMIT, SWIGLU_LIMIT)
            act_s[:, lo:hi] = ((up + 1.0) * (gate * _sigmoid(SWIGLU_ALPHA * gate))).astype(BF16)
        ybuf[slot] = _pack_bf16_pairs(_dot(act_s[...], wd_b[...]) + bd_ref[pl.ds(e, 1), :])
        y_copy(i, slot).start()
        return carry

    lax.fori_loop(0, n_used, body, 0)

    pl.when(n_used >= 2)(lambda: y_copy(n_used - 2, n_used & 1).wait())
    y_copy(n_used - 1, (n_used - 1) & 1).wait()
    lax.fori_loop(n_used, nb, lambda blk, c: (z_copy(blk).wait(), c)[1], 0)


def _moe_call(info, xg, wgu, bgu, wd, bd, nb):
    P = nb * MOE_BLOCK
    ne, d, f2 = wgu.shape
    blk_buf = pltpu.VMEM((2, MOE_BLOCK, D_HALF), jnp.uint32)
    grid_spec = pltpu.PrefetchScalarGridSpec(
        num_scalar_prefetch=1,
        grid=(1,),
        in_specs=[pl.BlockSpec(memory_space=pl.ANY),
                  pl.BlockSpec(memory_space=pl.ANY),
                  pl.BlockSpec((ne, f2), lambda i, info: (0, 0)),
                  pl.BlockSpec(memory_space=pl.ANY),
                  pl.BlockSpec((ne, d), lambda i, info: (0, 0))],
        out_specs=pl.BlockSpec(memory_space=pl.ANY),
        scratch_shapes=[pltpu.VMEM((2, d, f2), F32), pltpu.VMEM((2, f2 // 2, d), F32),
                        pltpu.VMEM((d, f2), BF16), pltpu.VMEM((f2 // 2, d), BF16),
                        pltpu.VMEM((MOE_BLOCK, f2 // 2), BF16), blk_buf, blk_buf,
                        pltpu.VMEM((MOE_BLOCK, D_HALF), jnp.uint32),
                        pltpu.SemaphoreType.DMA((2, 2)), pltpu.SemaphoreType.DMA((2,)),
                        pltpu.SemaphoreType.DMA((2,)), pltpu.SemaphoreType.DMA(())],
    )
    return pl.pallas_call(
        functools.partial(_moe_kernel, nb),
        grid_spec=grid_spec,
        out_shape=jax.ShapeDtypeStruct((P, D_HALF), jnp.uint32),
        compiler_params=_params("arbitrary"),
    )(info, xg, wgu, bgu, wd, bd)


def _combine_kernel(d0_ref, d1_ref, d2_ref, y_ref, lp_ref, x1_ref, gate_ref, g2_ref, fn_ref, o_ref, buf, sems):
    i = pl.program_id(0)
    n = pl.num_programs(0)
    slot = lax.rem(i, N_SLOTS)
    ahead = lax.rem(i + 2, N_SLOTS)

    def start(dref, s, c):
        d = pl.multiple_of(jnp.maximum(dref[0, 0, c], 0), RUN)
        dst = pl.multiple_of(RUN * c, RUN)
        pltpu.make_async_copy(y_ref.at[pl.ds(d, RUN), :], buf.at[s, pl.ds(dst, RUN), :], sems.at[s]).start()

    def drain(s):
        pltpu.make_async_copy(y_ref.at[pl.ds(0, RL), :], buf.at[s], sems.at[s]).wait()

    @pl.when(i == 0)
    def _():
        lax.fori_loop(0, RL // RUN, lambda c, carry: (start(d0_ref, 0, c), carry)[1], 0, unroll=4)
        lax.fori_loop(0, RL // RUN, lambda c, carry: (start(d1_ref, 1, c), carry)[1], 0, unroll=4)

    drain(slot)

    for c in range(RL // RUN):
        start(d2_ref, ahead, c)

    g = _pair_matrix(lp_ref[...], gate_ref[...]).astype(BF16)
    halves = [_dot(g, yb) for yb in _unpack_bf16_pairs(buf[slot])]
    xo = x1_ref[...] + g2_ref[0] * jnp.concatenate(halves, axis=1)
    ms = jnp.mean(xo * xo, axis=-1, keepdims=True)
    o_ref[...] = xo * lax.rsqrt(ms + EPS) * fn_ref[...]

    @pl.when(i == n - 1)
    def _():
        drain(lax.rem(i + 1, N_SLOTS))
        drain(ahead)


def _combine_call(cd, y, lp, x1, gates, g2, fn):
    T, d = x1.shape
    n = T // TB
    per_batch = n // g2.shape[0]
    return pl.pallas_call(
        _combine_kernel,
        grid=(n,),
        in_specs=[pl.BlockSpec((1, 1, CD_LANES), lambda i: (i, 0, 0), memory_space=pltpu.SMEM),
                  pl.BlockSpec((1, 1, CD_LANES), lambda i: (jnp.minimum(i + 1, n - 1), 0, 0),
                               memory_space=pltpu.SMEM),
                  pl.BlockSpec((1, 1, CD_LANES), lambda i: (jnp.minimum(i + 2, n - 1), 0, 0),
                               memory_space=pltpu.SMEM),
                  pl.BlockSpec(memory_space=pl.ANY),
                  pl.BlockSpec((TB, LANES), lambda i: (i, 0)),
                  pl.BlockSpec((TB, d), lambda i: (i, 0)),
                  pl.BlockSpec((TB, LANES), lambda i: (i, 0)),
                  pl.BlockSpec((1, 1, d), lambda i: (i // per_batch, 0, 0)),
                  pl.BlockSpec((1, d), lambda i: (0, 0))],
        out_specs=pl.BlockSpec((TB, d), lambda i: (i, 0)),
        out_shape=jax.ShapeDtypeStruct((T, d), F32),
        scratch_shapes=[pltpu.VMEM((N_SLOTS, RL, D_HALF), jnp.uint32), pltpu.SemaphoreType.DMA((N_SLOTS,))],
        compiler_params=_params("arbitrary"),
    )(cd, cd, cd, y, lp, x1, gates, g2, fn)


def _layer(x, c, ctx, c_ctx, w_mod, b_mod, norm1, w_in, gla_w_gk_up, gla_b_gk, gla_norm,
           ssd_conv_w, ssd_conv_b, ssd_dt_bias, ssd_A_log, ssd_D, ssd_norm, w_out,
           norm2, w_router, b_router, w_gate_up, b_gate_up, w_down, b_down, final_norm):
    bsz, L, d = x.shape
    lc = ctx.shape[1]
    assert lc == TB and L % TB == 0 and TB % GRID_W == 0

    cin = jnp.zeros((8, d), F32).at[:bsz].set(c).at[bsz].set(c_ctx)
    mod = _mod_call(cin, w_mod, b_mod.reshape(1, -1))[:bsz + 1]
    sh1, sc1, g1, sh2, sc2, g2 = [m.reshape(bsz + 1, 1, d) for m in jnp.split(mod, 6, axis=-1)]

    o = np.cumsum((0, GLA_QK, GLA_QK, GLA_V, GLA_V, GLA_RANK, SSD_INNER, SSD_CONV_DIM, SSD_HEADS))
    wq, wk, wv, wg, wlow, wz, wx, wdt = [w_in[:, int(a):int(b)] for a, b in zip(o[:-1], o[1:])]
    w_misc = jnp.concatenate([wlow, wdt, wdt, jnp.zeros((d, LANES - GLA_RANK - 2 * SSD_HEADS), F32)], axis=1)
    w_cat = jnp.concatenate([wq, wk, wv, wg, wz, wx, w_misc], axis=1).astype(BF16)
    wup = jnp.zeros((LANES, 2 * GLA_QK), F32).at[:GLA_RANK].set(
        jnp.concatenate([gla_w_gk_up[0], gla_w_gk_up[1]], axis=1)).astype(BF16)
    bup = jnp.concatenate([gla_b_gk[0], gla_b_gk[1]]).reshape(1, -1)
    dtb = jnp.zeros((1, LANES), F32).at[0, DT_F:DT_F + SSD_HEADS].set(ssd_dt_bias[0]) \
                                    .at[0, DT_B:DT_B + SSD_HEADS].set(ssd_dt_bias[1])
    q, k, v, g_all, z_all, xbc, ld, misc = _inproj_call(
        x, ctx, sh1, sc1, norm1.reshape(1, d), w_cat, wup, bup, dtb)

    xs, bc = _conv_call(xbc, ssd_conv_w.reshape(9, SSD_CONV_DIM), ssd_conv_b.reshape(1, -1), lc + L)

    a_neg = -jnp.exp(ssd_A_log.astype(F32))
    a_f = jnp.zeros((1, LANES), F32).at[0, DT_F:DT_F + SSD_HEADS].set(a_neg[0])
    a_b = jnp.zeros((1, LANES), F32).at[0, DT_B:DT_B + SSD_HEADS].set(a_neg[1])
    dvec = jnp.repeat(ssd_D, SSD_HEADDIM).reshape(1, SSD_INNER)
    o_f, o_b, y_f, y_b = _scan_call(q, k, v, ld, xs, bc, misc, a_f, a_b, dvec, L)

    wr = jnp.zeros((d, LANES), F32).at[:, :N_EXPERTS].set(w_router)
    wr_hi = wr.astype(BF16)
    wr_lo = (wr - wr_hi.astype(F32)).astype(BF16)
    br = jnp.full((1, LANES), NEG_BIG, F32).at[0, :N_EXPERTS].set(b_router)
    x1, h2, eidx, gates, counts = _outproj_call(
        o_f, o_b, g_all, y_f, y_b, z_all, x, g1[:bsz], sh2[:bsz], sc2[:bsz],
        jnp.tile(gla_norm, GLA_HEADS).reshape(1, -1), ssd_norm.reshape(1, -1), w_out.astype(BF16),
        norm2.reshape(1, d), wr_hi, wr_lo, br)

    T = bsz * L
    cnt = counts[0, :N_EXPERTS].astype(jnp.int32)
    padded = ((cnt + MOE_BLOCK - 1) // MOE_BLOCK) * MOE_BLOCK
    pend = jnp.cumsum(padded)
    pstart = pend - padded
    max_rows = T * TOP_K + (T // TB) * N_EXPERTS * (RUN - 1)
    n_blocks = -(-max_rows // MOE_BLOCK) + N_EXPERTS
    blk_start = jnp.arange(n_blocks, dtype=jnp.int32) * MOE_BLOCK
    blk_e = jnp.minimum(jnp.sum(pend[None, :] <= blk_start[:, None], axis=1), N_EXPERTS - 1).astype(jnp.int32)
    n_used = (pend[-1] // MOE_BLOCK).astype(jnp.int32)
    blk_i = jnp.arange(n_blocks, dtype=jnp.int32)
    first = (blk_i < n_used) & ((blk_i == 0) | (blk_e != jnp.roll(blk_e, 1)))
    slot = (jnp.cumsum(first) - 1) & 1
    first_pos = jnp.where(first, blk_i, n_blocks)
    next_first = jnp.roll(lax.cummin(first_pos, reverse=True), -1).at[-1].set(n_blocks)
    nxt = jnp.where(next_first < n_blocks, blk_e[jnp.minimum(next_first, n_blocks - 1)], -1)
    col = lambda v, last: jnp.concatenate([v.astype(jnp.int32), jnp.asarray([last], jnp.int32)])
    blk_info = jnp.stack([col(blk_e, 0).at[-1].set(n_used), col(first, 0), col(slot, 0), col(nxt, -1)])
    n_tail = n_blocks - (T * TOP_K) // MOE_BLOCK
    tail = n_used + jnp.arange(n_tail, dtype=jnp.int32)
    fill_blocks = jnp.concatenate([
        jnp.where(padded > 0, pend // MOE_BLOCK - 1, -1),
        jnp.where(tail < n_blocks, tail, -1)]).astype(jnp.int32)
    pstart_row = jnp.zeros((1, LANES), F32).at[0, :N_EXPERTS].set(pstart.astype(F32))
    lp, cd = _pos_call(eidx.reshape(T, LANES), pstart_row)

    xg = _dispatch_call(fill_blocks, cd, h2.reshape(T, d), lp, n_blocks * MOE_BLOCK)
    y = _moe_call(blk_info, xg, w_gate_up, b_gate_up, w_down, b_down, n_blocks)
    out = _combine_call(cd, y, lp, x1.reshape(T, d), gates.reshape(T, LANES), g2[:bsz], final_norm.reshape(1, d))
    return out.reshape(bsz, L, d)


def kernel(x, c, ctx, c_ctx, w_mod, b_mod, norm1, w_in, gla_w_gk_up, gla_b_gk, gla_norm, ssd_conv_w, ssd_conv_b, ssd_dt_bias, ssd_A_log, ssd_D, ssd_norm, w_out, norm2, w_router, b_router, w_gate_up, b_gate_up, w_down, b_down, final_norm):
    assert w_mod.shape[0] == 1, "single-layer kernel"
    return _layer(x, c, ctx, c_ctx, w_mod[0], b_mod[0], norm1[0], w_in[0], gla_w_gk_up[0], gla_b_gk[0],
                  gla_norm[0], ssd_conv_w[0], ssd_conv_b[0], ssd_dt_bias[0], ssd_A_log[0], ssd_D[0],
                  ssd_norm[0], w_out[0], norm2[0], w_router[0], b_router[0], w_gate_up[0], b_gate_up[0],
                  w_down[0], b_down[0], final_norm)
```

```python
import functools

import numpy as np
import jax
import jax.numpy as jnp
from jax import lax
from jax.experimental import pallas as pl
from jax.experimental.pallas import tpu as pltpu

F32 = jnp.float32
BF16 = jnp.bfloat16

EPS = 1e-6
GRID_W = 64
GLA_HEADS = 4
GLA_DK = 64
GLA_DV = 128
GLA_QK = GLA_HEADS * GLA_DK
GLA_V = GLA_HEADS * GLA_DV
GLA_RANK = 16
GLA_GATE_NORM = 16.0
SSD_HEADDIM = 64
SSD_INNER = 512
SSD_HEADS = 8
SSD_GROUPS = 2
SSD_HPG = 4
SSD_STATE = 128
SSD_CONV_DIM = 1024
N_EXPERTS = 32
TOP_K = 4
D_FF = 1024
SWIGLU_LIMIT = 7.0
SWIGLU_ALPHA = 1.702
MOE_BLOCK = 1024

TB = 256
GLA_C = 64
GLA_STAGE_GROUP = 1
SSD_C = 128
SSD_STAGE_GROUP = 1
LANES = 128
EXP_CLAMP = 80.0
DT_F = 16
DT_B = 24
NEG_BIG = -1e30
RUN = 8
RL = -(-(TB * TOP_K + N_EXPERTS * (RUN - 1)) // LANES) * LANES
CD_LANES = -(-(RL // RUN) // LANES) * LANES
D_HALF = 512
VMEM_LIMIT = 56 * 1024 * 1024


def _dot(a, b):
    return jnp.dot(a, b, preferred_element_type=F32)


def _dot_nt(a, b):
    return lax.dot_general(a, b, (((1,), (1,)), ((), ())), preferred_element_type=F32)


def _dot_tn(a, b):
    return lax.dot_general(a, b, (((0,), (0,)), ((), ())), preferred_element_type=F32)


def _split3(a):
    hi = a.astype(BF16)
    r1 = a - hi.astype(F32)
    mid = r1.astype(BF16)
    lo = (r1 - mid.astype(F32)).astype(BF16)
    return hi, mid, lo


def _dot_exact_r(m, a):
    hi, mid, lo = _split3(a)
    return _dot(m, hi) + _dot(m, mid) + _dot(m, lo)


def _dot_hilo_r(m, a):
    hi = a.astype(BF16)
    lo = (a - hi.astype(F32)).astype(BF16)
    return _dot(m, hi) + _dot(m, lo)


def _dot_hilo_l(a, m2):
    hi = a.astype(BF16)
    lo = (a - hi.astype(F32)).astype(BF16)
    return _dot(jnp.concatenate([hi, lo], axis=1), m2)


def _sigmoid(x):
    return 1.0 / (1.0 + jnp.exp(-x))


def _softplus(x):
    return jnp.maximum(x, 0.0) + jnp.log1p(jnp.exp(-jnp.abs(x)))


def _params(*sem):
    return pltpu.CompilerParams(dimension_semantics=sem, vmem_limit_bytes=VMEM_LIMIT)


def _mod_kernel(c_ref, w_ref, b_ref, o_ref):
    c = c_ref[...]
    s = c * _sigmoid(c)
    s_hi = s.astype(BF16)
    s_lo = (s - s_hi.astype(F32)).astype(BF16)
    w = w_ref[...]
    w_hi = w.astype(BF16)
    w_lo = (w - w_hi.astype(F32)).astype(BF16)
    o_ref[...] = _dot(s_hi, w_hi) + _dot(s_lo, w_hi) + _dot(s_hi, w_lo) + b_ref[...]


def _mod_call(cin, w, b):
    rows, d = cin.shape
    n = w.shape[1]
    tn = 1536
    return pl.pallas_call(
        _mod_kernel,
        grid=(n // tn,),
        in_specs=[pl.BlockSpec((rows, d), lambda i: (0, 0)),
                  pl.BlockSpec((d, tn), lambda i: (0, i)),
                  pl.BlockSpec((1, tn), lambda i: (0, i))],
        out_specs=pl.BlockSpec((rows, tn), lambda i: (0, i)),
        out_shape=jax.ShapeDtypeStruct((rows, n), F32),
        compiler_params=_params("arbitrary"),
    )(cin, w, b)


_C_Q, _C_K, _C_V, _C_G, _C_Z, _C_X, _C_M, _C_END = 0, 256, 512, 1024, 1536, 2048, 3072, 3200


def _inproj_kernel(x0_ref, xa_ref, xb_ref, ctx_ref, shl_ref, scl_ref, shc_ref, scc_ref, n1_ref, w_ref, wup_ref,
                   bup_ref, dtb_ref, q_ref, k_ref, v_ref, g_ref, z_ref, xbc_ref, ld_ref, misc_ref, h_scr):
    j = pl.program_id(1)
    slot = j & 1

    def normmod(xv, sh_ref, sc_ref):
        ms = jnp.mean(xv * xv, axis=-1, keepdims=True)
        y = xv * lax.rsqrt(ms + EPS) * n1_ref[...]
        return (y * (1.0 + sc_ref[0]) + sh_ref[0]).astype(BF16)

    @pl.when(j == 0)
    def _():
        h_scr[0, 0:TB, :] = normmod(ctx_ref[0], shc_ref, scc_ref)
        h_scr[0, TB:2 * TB, :] = normmod(x0_ref[0], shl_ref, scl_ref)

    def mm(lo, hi):
        return _dot(h_scr[slot], w_ref[:, lo:hi])

    m = mm(_C_M, _C_END)
    zz = _dot(m.astype(BF16), wup_ref[...]) + bup_ref[...]
    ld_ref[0] = -_softplus(-zz) * (1.0 / GLA_GATE_NORM)
    misc_ref[0] = _softplus(m + dtb_ref[...])
    q_ref[0] = (mm(_C_Q, _C_K) * (GLA_DK ** -0.5)).astype(BF16)
    h_scr[1 - slot, 0:TB, :] = normmod(xa_ref[0], shl_ref, scl_ref)
    k_ref[0] = mm(_C_K, _C_V).astype(BF16)
    h_scr[1 - slot, TB:2 * TB, :] = normmod(xb_ref[0], shl_ref, scl_ref)
    v_ref[0] = mm(_C_V, _C_G).astype(BF16)
    g_ref[0] = mm(_C_G, _C_Z)
    z_ref[0] = mm(_C_Z, _C_X)
    xbc_ref[0] = mm(_C_X, _C_M)


def _inproj_call(x, ctx, sh1, sc1, n1, w_cat, wup, bup, dtb):
    bsz, L, d = x.shape
    nx = L // TB
    nj = -(-(nx + 1) // 2)
    tok = lambda n: pl.BlockSpec((1, 2 * TB, n), lambda b, j: (b, j, 0))
    const = lambda a: pl.BlockSpec(a.shape, lambda b, j: (0,) * a.ndim)
    xblk = lambda off: pl.BlockSpec((1, TB, d), lambda b, j: (b, jnp.minimum(2 * j + off, nx - 1), 0))
    first = pl.BlockSpec((1, TB, d), lambda b, j: (b, 0, 0))
    mod_lat = pl.BlockSpec((1, 1, d), lambda b, j: (b, 0, 0))
    mod_ctx = pl.BlockSpec((1, 1, d), lambda b, j: (bsz, 0, 0))
    outs = [(GLA_QK, BF16), (GLA_QK, BF16), (GLA_V, BF16), (GLA_V, F32), (SSD_INNER, F32),
            (SSD_CONV_DIM, F32), (2 * GLA_QK, F32), (LANES, F32)]
    return pl.pallas_call(
        _inproj_kernel,
        grid=(bsz, nj),
        in_specs=[first, xblk(1), xblk(2), first,
                  mod_lat, mod_lat, mod_ctx, mod_ctx,
                  const(n1), const(w_cat), const(wup), const(bup), const(dtb)],
        out_specs=[tok(n) for n, _ in outs],
        out_shape=[jax.ShapeDtypeStruct((bsz, nj * 2 * TB, n), dt) for n, dt in outs],
        scratch_shapes=[pltpu.VMEM((2, 2 * TB, d), BF16)],
        compiler_params=_params("arbitrary", "arbitrary"),
    )(x, x, x, ctx, sh1, sc1, sh1, sc1, n1, w_cat, wup, bup, dtb)


_EXT_PAD = 8
_EXT_BASE = _EXT_PAD + GRID_W
_EXT_ROWS = 2 * _EXT_PAD + 2 * GRID_W + TB


def _conv_kernel(prev_ref, cur_ref, next_ref, w_ref, b_ref, xs_ref, bc_ref, ext_s, xl_all, xr_all):
    for bb in range(cur_ref.shape[0]):
        _conv_block(prev_ref.at[bb], cur_ref.at[bb], next_ref.at[bb], w_ref, b_ref, xs_ref.at[bb], bc_ref.at[bb],
                    ext_s.at[bb], xl_all.at[bb], xr_all.at[bb])


def _conv_block(prev_ref, cur_ref, next_ref, w_ref, b_ref, xs_ref, bc_ref, ext, xl_s, xr_s):
    j = pl.program_id(1)
    nj = pl.num_programs(1)
    is_ctx = j == 0
    zpad = jnp.zeros((_EXT_PAD, SSD_CONV_DIM), F32)
    ext[0:_EXT_PAD, :] = zpad
    ext[_EXT_ROWS - _EXT_PAD:_EXT_ROWS, :] = zpad
    ext[_EXT_PAD:_EXT_BASE, :] = jnp.where(j >= 2, prev_ref[...], 0.0)
    ext[_EXT_BASE:_EXT_BASE + TB, :] = cur_ref[...]
    ext[_EXT_BASE + TB:_EXT_BASE + TB + GRID_W, :] = jnp.where(
        jnp.logical_and(j >= 1, j <= nj - 2), next_ref[...], 0.0)

    win = TB + 2 * GRID_W
    u = lax.broadcasted_iota(jnp.int32, (win, LANES), 0)
    pos = jnp.where(is_ctx, u - GRID_W, u & (GRID_W - 1))
    ok_l = pos >= 1
    ok_r = jnp.where(is_ctx, pos - (TB - GRID_W), pos) <= GRID_W - 2
    lat = jnp.where(is_ctx, 0.0, 1.0)
    side = {-1: xl_s, 0: None, 1: xr_s}

    for c in range(SSD_CONV_DIM // LANES):
        lo, hi = c * LANES, (c + 1) * LANES
        xl_s[:, lo:hi] = jnp.where(ok_l, ext[_EXT_PAD - 1:_EXT_PAD - 1 + win, lo:hi], 0.0)
        xr_s[:, lo:hi] = jnp.where(ok_r, ext[_EXT_PAD + 1:_EXT_PAD + 1 + win, lo:hi], 0.0)
        acc = jnp.zeros((TB, LANES), F32)
        for dr in (-1, 0, 1):
            for dc in (-1, 0, 1):
                if dc == 0:
                    start = _EXT_BASE + GRID_W * dr
                    tap = ext[start:start + TB, lo:hi]
                else:
                    start = GRID_W + GRID_W * dr
                    tap = side[dc][start:start + TB, lo:hi]
                wi = 3 * (dr + 1) + (dc + 1)
                wv = w_ref[wi:wi + 1, lo:hi]
                if dr != 0:
                    wv = wv * lat
                acc = acc + tap * wv
        y = acc + b_ref[:, lo:hi]
        y = y * _sigmoid(y)
        if c < SSD_INNER // LANES:
            xs_ref[:, lo:hi] = y
        else:
            bc_ref[:, lo - SSD_INNER:hi - SSD_INNER] = y.astype(BF16)


def _conv_call(xbc, w9, bias, ls):
    bsz, _, ch = xbc.shape
    nj = ls // TB
    rpb = TB // GRID_W
    nrow = ls // GRID_W
    nb = _scan_batch(bsz)
    return pl.pallas_call(
        _conv_kernel,
        grid=(bsz // nb, nj),
        in_specs=[pl.BlockSpec((nb, GRID_W, ch), lambda b, j: (b, jnp.maximum(rpb * j - 1, 0), 0)),
                  pl.BlockSpec((nb, TB, ch), lambda b, j: (b, j, 0)),
                  pl.BlockSpec((nb, GRID_W, ch), lambda b, j: (b, jnp.minimum(rpb * j + rpb, nrow - 1), 0)),
                  pl.BlockSpec((9, ch), lambda b, j: (0, 0)),
                  pl.BlockSpec((1, ch), lambda b, j: (0, 0))],
        out_specs=[pl.BlockSpec((nb, TB, SSD_INNER), lambda b, j: (b, j, 0)),
                   pl.BlockSpec((nb, TB, ch - SSD_INNER), lambda b, j: (b, j, 0))],
        out_shape=[jax.ShapeDtypeStruct((bsz, ls, SSD_INNER), F32),
                   jax.ShapeDtypeStruct((bsz, ls, ch - SSD_INNER), BF16)],
        scratch_shapes=[pltpu.VMEM((nb, _EXT_ROWS, ch), F32), pltpu.VMEM((nb, TB + 2 * GRID_W, ch), F32),
                        pltpu.VMEM((nb, TB + 2 * GRID_W, ch), F32)],
        compiler_params=_params("arbitrary", "arbitrary"),
    )(xbc, xbc, xbc, w9, bias)


def _fwd_blk(s):
    return s


def _bwd_blk(s, ns):
    return jnp.where(s == 0, 0, ns - s)


def _scan_batch(bsz):
    return 2 if bsz % 2 == 0 else 1


class _GlaChunk:
    def __init__(self, q, k, v, la, st_ref, tri_m, fwd, store):
        self.q, self.k, self.v, self.la, self.st_ref, self.tri_m, self.fwd, self.store = (
            q, k, v, la, st_ref, tri_m, fwd, store)

    def stage_sums(self):
        self.b = _dot_hilo_r(self.tri_m, self.la)

    def stage_factors(self):
        C = GLA_C
        b = self.b
        self.bt = b[C - 1:C, :] if self.fwd else b[0:1, :]
        r = 0.5 * self.bt
        self.er = jnp.exp(r)
        self.qt = (self.q.astype(F32) * jnp.exp(jnp.minimum(b - r, EXP_CLAMP))).astype(BF16)
        kt = (self.k.astype(F32) * jnp.exp(jnp.minimum(r - b, EXP_CLAMP))).astype(BF16)
        head_k = lax.broadcasted_iota(jnp.int32, (C, GLA_QK), 1) >> 6
        zero = jnp.zeros_like(kt)
        self.kh = [jnp.where(head_k == h, kt, zero) for h in range(GLA_HEADS)]
        self.qh = [jnp.where(head_k == h, self.qt, zero) for h in range(GLA_HEADS)]

    def stage_products(self):
        v = self.v
        kcat = jnp.concatenate(self.kh, axis=0)
        self.sc = _dot_nt(self.qt, kcat)
        vcat = jnp.concatenate([v[:, h * GLA_DV:(h + 1) * GLA_DV] for h in range(GLA_HEADS)], axis=0)
        self.u = _dot_tn(vcat, kcat) * self.er

    def stage_mask(self):
        C = GLA_C
        ii = lax.broadcasted_iota(jnp.int32, (C, GLA_HEADS * C), 0)
        jj = lax.broadcasted_iota(jnp.int32, (C, GLA_HEADS * C), 1) & (C - 1)
        causal = (jj <= ii) if self.fwd else (jj >= ii)
        self.p = jnp.where(causal, self.sc, 0.0).astype(BF16)
        v = self.v
        head_v = lax.broadcasted_iota(jnp.int32, (C, GLA_V), 1) >> 7
        self.vst = jnp.concatenate([jnp.where(head_v == h, v, jnp.zeros_like(v)) for h in range(GLA_HEADS)], axis=0)

    def stage_intra(self):
        self.o = _dot(self.p, self.vst)

    def stage_inter(self):
        st = self.st_ref[...]
        ster = (st * self.er).astype(BF16)
        self.st_ref[...] = st * jnp.exp(self.bt) + self.u
        res = _dot_nt(jnp.concatenate(self.qh, axis=0), ster)
        self.inter = jnp.concatenate([res[h * GLA_C:(h + 1) * GLA_C, :] for h in range(GLA_HEADS)], axis=1)

    def stage_out(self):
        self.store(self.o + self.inter)


def _store_to(ref, bb, sl):
    def store(val):
        ref[bb, sl, :] = val
    return store


def _gla_schedule(qf_ref, kf_ref, vf_ref, lf_ref, qb_ref, kb_ref, vb_ref, lb_ref, trif_ref, trib_ref,
                  of_ref, ob_ref, stf, stb):
    nsub = TB // GLA_C
    store_to = _store_to
    calls = []

    group = GLA_STAGE_GROUP
    for g0 in range(0, nsub, group):
        steps = []
        for i in range(g0, g0 + group):
            sf = pl.ds(i * GLA_C, GLA_C)
            sb = pl.ds((nsub - 1 - i) * GLA_C, GLA_C)
            chunks = []
            for bb in range(qf_ref.shape[0]):
                chunks.append(_GlaChunk(qf_ref[bb, sf, :], kf_ref[bb, sf, :], vf_ref[bb, sf, :], lf_ref[bb, sf, :],
                                        stf.at[bb], trif_ref[...], True, store_to(of_ref, bb, sf)))
                chunks.append(_GlaChunk(qb_ref[bb, sb, :], kb_ref[bb, sb, :], vb_ref[bb, sb, :], lb_ref[bb, sb, :],
                                        stb.at[bb], trib_ref[...], False, store_to(ob_ref, bb, sb)))
            steps.append(chunks)
        for stage in ("stage_sums", "stage_factors", "stage_products", "stage_mask", "stage_intra"):
            for chunks in steps:
                calls += [getattr(ch, stage) for ch in chunks]
        for chunks in steps:
            for stage in ("stage_inter", "stage_out"):
                calls += [getattr(ch, stage) for ch in chunks]
    return calls


class _SsdChunk:
    def __init__(self, xs, bc, dtm, avec, dvec, st_ref, tri_m, e_m, base, fwd, store):
        self.xs, self.bc, self.dtm, self.avec, self.dvec, self.st_ref = xs, bc, dtm, avec, dvec, st_ref
        self.tri_m, self.e_m, self.base, self.fwd, self.store = tri_m, e_m, base, fwd, store

    def stage_sums(self):
        self.dt_exp = _dot_hilo_l(self.dtm, self.e_m)
        self.acum = _dot_exact_r(self.tri_m, self.dtm * self.avec)

    def stage_expand(self):
        self.acum_exp = _dot_hilo_l(self.acum, self.e_m)
        self.acum_t = self.acum.T
        self.xdt = self.xs * self.dt_exp
        bc = self.bc
        self.bg = [bc[:, 128 * g:128 * (g + 1)] for g in range(SSD_GROUPS)]
        self.cg = [bc[:, 256 + 128 * g:256 + 128 * (g + 1)] for g in range(SSD_GROUPS)]
        self.cb = [_dot_nt(self.cg[g], self.bg[g]) for g in range(SSD_GROUPS)]

    def stage_decay(self):
        C = SSD_C
        ii = lax.broadcasted_iota(jnp.int32, (C, C), 0)
        jj = lax.broadcasted_iota(jnp.int32, (C, C), 1)
        tri = (jj <= ii) if self.fwd else (jj >= ii)
        self.ms = []
        for g in range(SSD_GROUPS):
            for rr in range(SSD_HPG):
                ln = self.base + SSD_HPG * g + rr
                diff = self.acum[:, ln:ln + 1] - self.acum_t[ln:ln + 1, :]
                seg = jnp.where(tri, jnp.exp(jnp.minimum(diff, 0.0)), 0.0)
                self.ms.append((self.cb[g] * seg).astype(BF16))
        ae = self.acum_exp
        self.al_exp = ae[C - 1:C, :] if self.fwd else ae[0:1, :]
        self.xw = (self.xdt * jnp.exp(self.al_exp - ae)).astype(BF16)
        xdt_b = self.xdt.astype(BF16)
        gw = SSD_HPG * SSD_HEADDIM
        head = lax.broadcasted_iota(jnp.int32, (C, gw), 1) >> 6
        zero = jnp.zeros((C, gw), BF16)
        self.xh = [jnp.where(head == h % SSD_HPG, xdt_b[:, gw * (h // SSD_HPG):gw * (h // SSD_HPG + 1)], zero)
                   for h in range(SSD_HEADS)]

    def stage_products(self):
        self.yg, self.ug = [], []
        for g in range(SSD_GROUPS):
            gl, gh = 256 * g, 256 * (g + 1)
            yg = _dot(self.ms[SSD_HPG * g], self.xh[SSD_HPG * g])
            for rr in range(1, SSD_HPG):
                yg = yg + _dot(self.ms[SSD_HPG * g + rr], self.xh[SSD_HPG * g + rr])
            self.yg.append(yg)
            self.ug.append(_dot_tn(self.bg[g], self.xw[:, gl:gh]))

    def stage_state(self):
        ys = []
        for g in range(SSD_GROUPS):
            gl, gh = 256 * g, 256 * (g + 1)
            sg = self.st_ref[g]
            yoff = _dot(self.cg[g], sg.astype(BF16)) * jnp.exp(self.acum_exp[:, gl:gh])
            self.st_ref[g] = sg * jnp.exp(self.al_exp[:, gl:gh]) + self.ug[g]
            ys.append(self.yg[g] + yoff)
        y = jnp.concatenate(ys, axis=1)
        if self.dvec is not None:
            y = y + self.dvec * self.xs
        self.store(y)


def _ssd_schedule(xf_ref, bcf_ref, mf_ref, xb_ref, bcb_ref, mb_ref, af_ref, ab_ref, d_ref,
                  trif_ref, trib_ref, ef_ref, eb_ref, yf_ref, yb_ref, stf, stb):
    nsub = TB // SSD_C
    store_to = _store_to
    calls = []
    steps = []
    for i in range(nsub):
        sf = pl.ds(i * SSD_C, SSD_C)
        sb = pl.ds((nsub - 1 - i) * SSD_C, SSD_C)
        chunks = []
        for bb in range(xf_ref.shape[0]):
            chunks.append(_SsdChunk(xf_ref[bb, sf, :], bcf_ref[bb, sf, :], mf_ref[bb, sf, :], af_ref[...],
                                    d_ref[...], stf.at[bb], trif_ref[...], ef_ref[...], DT_F, True,
                                    store_to(yf_ref, bb, sf)))
            chunks.append(_SsdChunk(xb_ref[bb, sb, :], bcb_ref[bb, sb, :], mb_ref[bb, sb, :], ab_ref[...],
                                    None, stb.at[bb], trib_ref[...], eb_ref[...], DT_B, False,
                                    store_to(yb_ref, bb, sb)))
        steps.append(chunks)
    for g0 in range(0, nsub, SSD_STAGE_GROUP):
        group = steps[g0:g0 + SSD_STAGE_GROUP]
        for stage in ("stage_sums", "stage_expand", "stage_decay", "stage_products"):
            for chunks in group:
                calls += [getattr(ch, stage) for ch in chunks]
        for chunks in group:
            calls += [ch.stage_state for ch in chunks]
    return calls


N_GLA_IN, N_SSD_IN = 10, 13


def _scan_kernel(*refs):
    gla_in, ssd_in = refs[:N_GLA_IN], refs[N_GLA_IN:N_GLA_IN + N_SSD_IN]
    of_ref, ob_ref, yf_ref, yb_ref, gstf, gstb, sstf, sstb = refs[N_GLA_IN + N_SSD_IN:]

    @pl.when(pl.program_id(1) == 0)
    def _():
        for st in (gstf, gstb, sstf, sstb):
            st[...] = jnp.zeros_like(st)

    gla = _gla_schedule(*gla_in, of_ref, ob_ref, gstf, gstb)
    ssd = _ssd_schedule(*ssd_in, yf_ref, yb_ref, sstf, sstb)
    merged = sorted([((n + 0.5) / len(gla), 0, n, c) for n, c in enumerate(gla)] +
                    [((n + 0.5) / len(ssd), 1, n, c) for n, c in enumerate(ssd)], key=lambda t: t[:3])
    for _, _, _, call in merged:
        call()


def _expand_matrix(base):
    e = np.zeros((LANES, SSD_INNER), np.float32)
    for h in range(SSD_HEADS):
        e[base + h, SSD_HEADDIM * h:SSD_HEADDIM * (h + 1)] = 1.0
    return jnp.asarray(np.concatenate([e, e], axis=0), BF16)


def _scan_call(q, k, v, ld, xs, bc, misc, a_f, a_b, dvec, L):
    bsz = q.shape[0]
    nx = L // TB
    ns = nx + 1
    tri = lambda c, up: jnp.asarray((np.triu if up else np.tril)(np.ones((c, c), np.float32)), BF16)
    g_trif, g_trib, s_trif, s_trib = tri(GLA_C, False), tri(GLA_C, True), tri(SSD_C, False), tri(SSD_C, True)
    ef, eb = _expand_matrix(DT_F), _expand_matrix(DT_B)
    nb = _scan_batch(bsz)
    f = lambda n, lane=0: pl.BlockSpec((nb, TB, n), lambda b, s: (b, _fwd_blk(s), lane))
    r = lambda n, lane=0: pl.BlockSpec((nb, TB, n), lambda b, s: (b, _bwd_blk(s, ns), lane))
    const = lambda a: pl.BlockSpec(a.shape, lambda b, s: (0,) * a.ndim)
    out_f = lambda n: pl.BlockSpec((nb, TB, n), lambda b, s: (b, jnp.maximum(s - 1, 0), 0))
    out_b = lambda n: pl.BlockSpec((nb, TB, n), lambda b, s: (b, jnp.where(s == 0, nx - 1, nx - s), 0))
    gla_specs = [f(GLA_QK), f(GLA_QK), f(GLA_V), f(GLA_QK, 0), r(GLA_QK), r(GLA_QK), r(GLA_V), r(GLA_QK, 1),
                 const(g_trif), const(g_trib)]
    ssd_specs = [f(SSD_INNER), f(512), f(LANES), r(SSD_INNER), r(512), r(LANES),
                 const(a_f), const(a_b), const(dvec), const(s_trif), const(s_trib), const(ef), const(eb)]
    assert len(gla_specs) == N_GLA_IN and len(ssd_specs) == N_SSD_IN
    return pl.pallas_call(
        _scan_kernel,
        grid=(bsz // nb, ns),
        in_specs=gla_specs + ssd_specs,
        out_specs=[out_f(GLA_V), out_b(GLA_V), out_f(SSD_INNER), out_b(SSD_INNER)],
        out_shape=[jax.ShapeDtypeStruct((bsz, L, GLA_V), F32)] * 2 +
                  [jax.ShapeDtypeStruct((bsz, L, SSD_INNER), F32)] * 2,
        scratch_shapes=[pltpu.VMEM((nb, GLA_DV, GLA_QK), F32)] * 2 +
                       [pltpu.VMEM((nb, SSD_GROUPS, SSD_STATE, SSD_HPG * SSD_HEADDIM), F32)] * 2,
        compiler_params=_params("arbitrary", "arbitrary"),
    )(q, k, v, ld, q, k, v, ld, g_trif, g_trib, xs, bc, misc, xs, bc, misc, a_f, a_b, dvec, s_trif, s_trib, ef, eb)


def _outproj_kernel(of_ref, ob_ref, ga_ref, gb_ref, yf_ref, yb_ref, za_ref, zb_ref, x_ref, g1_ref, sh2_ref, sc2_ref,
                    gn_ref, sn_ref, wo_ref, n2_ref, wrh_ref, wrl_ref, br_ref,
                    x1_ref, h2_ref, eidx_ref, gate_ref, cnt_ref, mix_s, hl_s):
    first = jnp.logical_and(pl.program_id(0) == 0, pl.program_id(1) == 0)

    @pl.when(first)
    def _():
        cnt_ref[...] = jnp.zeros_like(cnt_ref)

    halves = [(pl.ds(0, TB), ga_ref, za_ref), (pl.ds(TB, TB), gb_ref, zb_ref)]

    for rows, g_ref, z_ref in halves:
        o = of_ref[0, rows, :] + ob_ref[0, rows, :]
        gg = g_ref[0]
        for h in range(GLA_HEADS):
            lo, hi = GLA_DV * h, GLA_DV * (h + 1)
            oh = o[:, lo:hi]
            ms = jnp.mean(oh * oh, axis=-1, keepdims=True)
            gh = gg[:, lo:hi]
            mix_s[rows, lo:hi] = (oh * lax.rsqrt(ms + EPS) * gn_ref[:, lo:hi] * (gh * _sigmoid(gh))).astype(BF16)
        zz = z_ref[0]
        u = (yf_ref[0, rows, :] + yb_ref[0, rows, :]) * (zz * _sigmoid(zz))
        gw = SSD_INNER // SSD_GROUPS
        for g in range(SSD_GROUPS):
            lo, hi = gw * g, gw * (g + 1)
            ug = u[:, lo:hi]
            ms = jnp.mean(ug * ug, axis=-1, keepdims=True)
            mix_s[rows, GLA_V + lo:GLA_V + hi] = (ug * lax.rsqrt(ms + EPS) * sn_ref[:, lo:hi]).astype(BF16)

    x1_ref[0] = x_ref[0] + g1_ref[0] * _dot(mix_s[...], wo_ref[...])

    for rows, _, _ in halves:
        x1 = x1_ref[0, rows, :]
        ms = jnp.mean(x1 * x1, axis=-1, keepdims=True)
        h2 = x1 * lax.rsqrt(ms + EPS) * (n2_ref[...] * (1.0 + sc2_ref[0])) + sh2_ref[0]
        h_hi = h2.astype(BF16)
        h2_ref[0, rows, :] = h_hi
        hl_s[rows, :] = (h2 - h_hi.astype(F32)).astype(BF16)

    h_hi = h2_ref[0]
    logits = (_dot(h_hi, wrh_ref[...]) + _dot(hl_s[...], wrh_ref[...]) + _dot(h_hi, wrl_ref[...])) + br_ref[...]

    lane = lax.broadcasted_iota(jnp.int32, (TB, LANES), 1).astype(F32)
    st = [dict(work=logits[i * TB:(i + 1) * TB, :], eidx=jnp.full((TB, LANES), -1.0, F32),
               gates=jnp.zeros((TB, LANES), F32), sel=jnp.zeros((TB, LANES), F32), m0=None,
               den=jnp.zeros((TB, 1), F32)) for i in range(2)]
    for kk in range(TOP_K):
        for t in st:
            t["m"] = jnp.max(t["work"], axis=-1, keepdims=True)
        for t in st:
            t["idx"] = jnp.min(jnp.where(t["work"] == t["m"], lane, float(LANES)), axis=-1, keepdims=True)
        for t in st:
            hit = lane == t["idx"]
            if t["m0"] is None:
                t["m0"] = t["m"]
            e = jnp.exp(t["m"] - t["m0"])
            t["den"] = t["den"] + e
            t["eidx"] = jnp.where(lane == float(kk), t["idx"], t["eidx"])
            t["gates"] = jnp.where(lane == float(kk), e, t["gates"])
            t["sel"] = jnp.where(hit, 1.0, t["sel"])
            t["work"] = jnp.where(hit, NEG_BIG, t["work"])
    for (rows, _, _), t in zip(halves, st):
        eidx_ref[0, rows, :] = t["eidx"].astype(jnp.int32)
        gate_ref[0, rows, :] = t["gates"] / t["den"]
        cnt = jnp.sum(t["sel"], axis=0, keepdims=True)
        cnt_ref[...] += jnp.floor((cnt + (RUN - 1.0)) * (1.0 / RUN)) * RUN


def _outproj_call(o_f, o_b, g_all, y_f, y_b, z_all, x, g1, sh2, sc2, gn, sn, wo, n2, wr_hi, wr_lo, br):
    bsz, L, d = x.shape
    nj = L // (2 * TB)
    tok = lambda n: pl.BlockSpec((1, 2 * TB, n), lambda b, j: (b, j, 0))
    tok_off = lambda n, half: pl.BlockSpec((1, TB, n), lambda b, j: (b, 2 * j + 1 + half, 0))
    const = lambda a: pl.BlockSpec(a.shape, lambda b, j: (0,) * a.ndim)
    mod = pl.BlockSpec((1, 1, d), lambda b, j: (b, 0, 0))
    return pl.pallas_call(
        _outproj_kernel,
        grid=(bsz, nj),
        in_specs=[tok(GLA_V), tok(GLA_V), tok_off(GLA_V, 0), tok_off(GLA_V, 1),
                  tok(SSD_INNER), tok(SSD_INNER), tok_off(SSD_INNER, 0), tok_off(SSD_INNER, 1),
                  tok(d), mod, mod, mod, const(gn), const(sn), const(wo), const(n2),
                  const(wr_hi), const(wr_lo), const(br)],
        out_specs=[tok(d), tok(d), tok(LANES), tok(LANES), pl.BlockSpec((1, LANES), lambda b, j: (0, 0))],
        out_shape=[jax.ShapeDtypeStruct((bsz, L, d), F32), jax.ShapeDtypeStruct((bsz, L, d), BF16),
                   jax.ShapeDtypeStruct((bsz, L, LANES), jnp.int32), jax.ShapeDtypeStruct((bsz, L, LANES), F32),
                   jax.ShapeDtypeStruct((1, LANES), F32)],
        scratch_shapes=[pltpu.VMEM((2 * TB, d), BF16), pltpu.VMEM((2 * TB, d), BF16)],
        compiler_params=_params("arbitrary", "arbitrary"),
    )(o_f, o_b, g_all, g_all, y_f, y_b, z_all, z_all, x, g1, sh2, sc2, gn, sn, wo, n2, wr_hi, wr_lo, br)


def _pos_kernel(eidx_ref, pstart_ref, lst_ref, ust_ref, lp_ref, cd_ref, carry):
    @pl.when(pl.program_id(0) == 0)
    def _():
        carry[...] = pstart_ref[...]

    lane = lax.broadcasted_iota(jnp.int32, (TB, LANES), 1)
    tiles = []
    for t in range(cd_ref.shape[0]):
        eidx = eidx_ref[t * TB:(t + 1) * TB, :]
        hits = [lane == eidx[:, kk:kk + 1] for kk in range(TOP_K)]
        sel = jnp.zeros((TB, LANES), F32)
        for hmask in hits:
            sel = jnp.where(hmask, 1.0, sel)
        cnt = jnp.sum(sel, axis=0, keepdims=True)
        run = jnp.floor((cnt + (RUN - 1.0)) * (1.0 / RUN)) * RUN
        tiles.append(dict(hits=hits, sel=sel, run=run))
    for tl in tiles:
        tl["rank"] = _dot(lst_ref[...], tl["sel"].astype(BF16))
        tl["loff"] = _dot(jnp.broadcast_to(tl["run"], (8, LANES)).astype(BF16), ust_ref[...])[0:1]
    for t, tl in enumerate(tiles):
        pos = tl["loff"] + tl["rank"]
        lp = jnp.zeros((TB, LANES), jnp.int32)
        for kk, hmask in enumerate(tl["hits"]):
            lk = jnp.sum(jnp.where(hmask, pos, 0.0), axis=-1, keepdims=True)
            lp = jnp.where(lane == kk, lk.astype(jnp.int32), lp)
        lp_ref[t * TB:(t + 1) * TB, :] = lp

    eye = lax.broadcasted_iota(jnp.int32, (LANES, LANES), 0) == lax.broadcasted_iota(jnp.int32, (LANES, LANES), 1)
    col = lambda v: jnp.sum(jnp.where(eye, v, 0.0), axis=1, keepdims=True)
    row0 = (lax.broadcasted_iota(jnp.int32, (LANES, CD_LANES), 1) * RUN).astype(F32)
    base = carry[...]
    for t, tl in enumerate(tiles):
        loff_c, run_c, shift_c = col(tl["loff"]), col(tl["run"]), col(base - tl["loff"])
        inside = jnp.where(row0 >= loff_c, 1.0, 0.0) * jnp.where(row0 < loff_c + run_c, 1.0, 0.0)
        valid = jnp.sum(inside, axis=0, keepdims=True)
        dest = jnp.sum(inside * shift_c, axis=0, keepdims=True) + row0[0:1]
        cd_ref[t] = jnp.where(valid > 0.0, dest, -1.0).astype(jnp.int32)
        base = base + tl["run"]
    carry[...] = base


def _pos_call(eidx, pstart):
    T = eidx.shape[0]
    lst = jnp.asarray(np.tril(np.ones((TB, TB), np.float32), -1), BF16)
    ust = jnp.asarray(np.triu(np.ones((LANES, LANES), np.float32), 1), BF16)
    tps = 2 if (T // TB) % 2 == 0 else 1
    return pl.pallas_call(
        _pos_kernel,
        grid=(T // (tps * TB),),
        in_specs=[pl.BlockSpec((tps * TB, LANES), lambda i: (i, 0)),
                  pl.BlockSpec((1, LANES), lambda i: (0, 0)),
                  pl.BlockSpec((TB, TB), lambda i: (0, 0)),
                  pl.BlockSpec((LANES, LANES), lambda i: (0, 0))],
        out_specs=[pl.BlockSpec((tps * TB, LANES), lambda i: (i, 0)),
                   pl.BlockSpec((tps, 1, CD_LANES), lambda i: (i, 0, 0))],
        out_shape=[jax.ShapeDtypeStruct((T, LANES), jnp.int32),
                   jax.ShapeDtypeStruct((T // TB, 1, CD_LANES), jnp.int32)],
        scratch_shapes=[pltpu.VMEM((1, LANES), F32)],
        compiler_params=_params("arbitrary"),
    )(eidx, pstart, lst, ust)


def _pair_matrix(lp, weights):
    j = lax.broadcasted_iota(jnp.int32, (TB, RL), 1)
    m = jnp.zeros((TB, RL), F32)
    for kk in range(TOP_K):
        w = 1.0 if weights is None else weights[:, kk:kk + 1]
        m = jnp.where(j == lp[:, kk:kk + 1], w, m)
    return m


def _pack_bf16_pairs(v, is_bf16_valued=False):
    if not is_bf16_valued:
        v = v.astype(BF16).astype(F32)
    bits = pltpu.bitcast(v, jnp.uint32)
    return bits[:, D_HALF:] | (bits[:, :D_HALF] >> 16)


def _unpack_bf16_pairs(w):
    lo = pltpu.bitcast(w << 16, F32).astype(BF16)
    hi = pltpu.bitcast(w & jnp.uint32(0xFFFF0000), F32).astype(BF16)
    return lo, hi


N_SLOTS = 3


def _dispatch_kernel(fill_ref, cdp_ref, cdc_ref, h_ref, lp_ref, xg_ref, sorted_s, sems, fill_sem):
    i = pl.program_id(0)
    n = pl.num_programs(0)
    slot = lax.rem(i, N_SLOTS)
    prev = lax.rem(i + N_SLOTS - 1, N_SLOTS)
    pprev = lax.rem(i + N_SLOTS - 2, N_SLOTS)
    spill0 = xg_ref.shape[0] - N_SLOTS * RL

    def start(cref, s, c, to_spill, sem):
        src = RUN * c if isinstance(c, int) else pl.multiple_of(RUN * c, RUN)
        d = cref[0, 0, c]
        d = pl.multiple_of(jnp.where(jnp.logical_or(d < 0, to_spill), spill0 + s * RL + src, d), RUN)
        pltpu.make_async_copy(sorted_s.at[s, pl.ds(src, RUN), :], xg_ref.at[pl.ds(d, RUN), :], sem).start()

    def drain(s, sem):
        pltpu.make_async_copy(sorted_s.at[s], xg_ref.at[pl.ds(0, RL), :], sem).wait()

    @pl.when(i == 0)
    def _():
        sorted_s[...] = jnp.zeros_like(sorted_s)
        for s in range(N_SLOTS):
            lax.fori_loop(0, RL // RUN, lambda c, carry, s=s: (start(cdp_ref, s, c, True, fill_sem), carry)[1], 0)
        for s in range(N_SLOTS):
            drain(s, fill_sem)

    for c in range(RL // RUN):
        start(cdp_ref, prev, c, i == 0, sems.at[prev])

    sorted_s[slot] = _pack_bf16_pairs(_dot_tn(_pair_matrix(lp_ref[...], None).astype(BF16), h_ref[...]),
                                      is_bf16_valued=True)

    @pl.when(i == 0)
    def _():
        def fill(k):
            blk = pl.multiple_of(fill_ref[k] * MOE_BLOCK, MOE_BLOCK)
            return pltpu.make_async_copy(sorted_s.at[0, pl.ds(0, MOE_BLOCK), :],
                                         xg_ref.at[pl.ds(blk, MOE_BLOCK), :], fill_sem)

        for k in range(fill_ref.shape[0]):
            pl.when(fill_ref[k] >= 0)(lambda k=k: fill(k).start())
        for k in range(fill_ref.shape[0]):
            pl.when(fill_ref[k] >= 0)(lambda k=k: fill(k).wait())

    pl.when(i >= 1)(lambda: drain(pprev, sems.at[pprev]))

    @pl.when(i == n - 1)
    def _():
        lax.fori_loop(0, RL // RUN, lambda c, carry: (start(cdc_ref, slot, c, False, sems.at[slot]), carry)[1], 0,
                      unroll=4)
        drain(prev, sems.at[prev])
        drain(slot, sems.at[slot])


def _dispatch_call(fill_blocks, cd, h2, lp, P):
    T, d = h2.shape
    grid_spec = pltpu.PrefetchScalarGridSpec(
        num_scalar_prefetch=1,
        grid=(T // TB,),
        in_specs=[pl.BlockSpec((1, 1, CD_LANES), lambda i, fb: (jnp.maximum(i - 1, 0), 0, 0),
                               memory_space=pltpu.SMEM),
                  pl.BlockSpec((1, 1, CD_LANES), lambda i, fb: (i, 0, 0), memory_space=pltpu.SMEM),
                  pl.BlockSpec((TB, d), lambda i, fb: (i, 0)),
                  pl.BlockSpec((TB, LANES), lambda i, fb: (i, 0))],
        out_specs=pl.BlockSpec(memory_space=pl.ANY),
        scratch_shapes=[pltpu.VMEM((N_SLOTS, RL, D_HALF), jnp.uint32), pltpu.SemaphoreType.DMA((N_SLOTS,)),
                        pltpu.SemaphoreType.DMA(())],
    )
    return pl.pallas_call(
        _dispatch_kernel,
        grid_spec=grid_spec,
        out_shape=jax.ShapeDtypeStruct((P + N_SLOTS * RL, D_HALF), jnp.uint32),
        compiler_params=_params("arbitrary"),
    )(fill_blocks, cd, cd, h2, lp)


def _moe_kernel(nb, info_ref, x_hbm, wgu_hbm, bgu_ref, wd_hbm, bd_ref, y_hbm,
                wgu_f, wd_f, wgu_b, wd_b, act_s, xbuf, ybuf, zbuf, wsems, xsems, ysems, zsem):
    n_used = info_ref[0, nb]

    def rows(blk):
        return pl.ds(pl.multiple_of(blk * MOE_BLOCK, MOE_BLOCK), MOE_BLOCK)

    def x_copy(blk, slot):
        return pltpu.make_async_copy(x_hbm.at[rows(blk), :], xbuf.at[slot], xsems.at[slot])

    def y_copy(blk, slot):
        return pltpu.make_async_copy(ybuf.at[slot], y_hbm.at[rows(blk), :], ysems.at[slot])

    def z_copy(blk):
        return pltpu.make_async_copy(zbuf, y_hbm.at[rows(blk), :], zsem)

    def fetch(expert, slot):
        return (pltpu.make_async_copy(wgu_hbm.at[expert], wgu_f.at[slot], wsems.at[0, slot]),
                pltpu.make_async_copy(wd_hbm.at[expert], wd_f.at[slot], wsems.at[1, slot]))

    zbuf[...] = jnp.zeros_like(zbuf)
    lax.fori_loop(n_used, nb, lambda blk, c: (z_copy(blk).start(), c)[1], 0)

    for cp in fetch(info_ref[0, 0], 0):
        cp.start()
    x_copy(0, 0).start()

    def body(i, carry):
        slot = i & 1
        e = info_ref[0, i]

        @pl.when(info_ref[1, i] == 1)
        def _():
            wslot = info_ref[2, i]
            nxt = info_ref[3, i]
            cps = fetch(e, wslot)
            cps[0].wait()
            wgu_b[...] = wgu_f[wslot].astype(BF16)
            cps[1].wait()
            wd_b[...] = wd_f[wslot].astype(BF16)

            @pl.when(nxt >= 0)
            def _():
                for cp in fetch(nxt, 1 - wslot):
                    cp.start()

        x_copy(i, slot).wait()
        pl.when(i + 1 < n_used)(lambda: x_copy(i + 1, 1 - slot).start())
        pl.when(i >= 2)(lambda: y_copy(i - 2, slot).wait())

        bgu = bgu_ref[pl.ds(e, 1), :]
        xb = jnp.concatenate(_unpack_bf16_pairs(xbuf[slot]), axis=1)
        cw = 256
        for c in range(D_FF // cw):
            lo, hi = c * cw, (c + 1) * cw
            gate = _dot(xb, wgu_b[:, lo:hi]) + bgu[:, lo:hi]
            up = _dot(xb, wgu_b[:, D_FF + lo:D_FF + hi]) + bgu[:, D_FF + lo:D_FF + hi]
            gate = jnp.minimum(gate, SWIGLU_LIMIT)
            up = jnp.clip(up, -SWIGLU_LIMIT, SWIGLU_LIMIT)
            act_s[:, lo:hi] = ((up + 1.0) * (gate * _sigmoid(SWIGLU_ALPHA * gate))).astype(BF16)
        ybuf[slot] = _pack_bf16_pairs(_dot(act_s[...], wd_b[...]) + bd_ref[pl.ds(e, 1), :])
        y_copy(i, slot).start()
        return carry

    lax.fori_loop(0, n_used, body, 0)

    pl.when(n_used >= 2)(lambda: y_copy(n_used - 2, n_used & 1).wait())
    y_copy(n_used - 1, (n_used - 1) & 1).wait()
    lax.fori_loop(n_used, nb, lambda blk, c: (z_copy(blk).wait(), c)[1], 0)


def _moe_call(info, xg, wgu, bgu, wd, bd, nb):
    P = nb * MOE_BLOCK
    ne, d, f2 = wgu.shape
    blk_buf = pltpu.VMEM((2, MOE_BLOCK, D_HALF), jnp.uint32)
    grid_spec = pltpu.PrefetchScalarGridSpec(
        num_scalar_prefetch=1,
        grid=(1,),
        in_specs=[pl.BlockSpec(memory_space=pl.ANY),
                  pl.BlockSpec(memory_space=pl.ANY),
                  pl.BlockSpec((ne, f2), lambda i, info: (0, 0)),
                  pl.BlockSpec(memory_space=pl.ANY),
                  pl.BlockSpec((ne, d), lambda i, info: (0, 0))],
        out_specs=pl.BlockSpec(memory_space=pl.ANY),
        scratch_shapes=[pltpu.VMEM((2, d, f2), F32), pltpu.VMEM((2, f2 // 2, d), F32),
                        pltpu.VMEM((d, f2), BF16), pltpu.VMEM((f2 // 2, d), BF16),
                        pltpu.VMEM((MOE_BLOCK, f2 // 2), BF16), blk_buf, blk_buf,
                        pltpu.VMEM((MOE_BLOCK, D_HALF), jnp.uint32),
                        pltpu.SemaphoreType.DMA((2, 2)), pltpu.SemaphoreType.DMA((2,)),
                        pltpu.SemaphoreType.DMA((2,)), pltpu.SemaphoreType.DMA(())],
    )
    return pl.pallas_call(
        functools.partial(_moe_kernel, nb),
        grid_spec=grid_spec,
        out_shape=jax.ShapeDtypeStruct((P, D_HALF), jnp.uint32),
        compiler_params=_params("arbitrary"),
    )(info, xg, wgu, bgu, wd, bd)


def _combine_kernel(d0_ref, d1_ref, d2_ref, y_ref, lp_ref, x1_ref, gate_ref, g2_ref, fn_ref, o_ref, buf, sems):
    i = pl.program_id(0)
    n = pl.num_programs(0)
    slot = lax.rem(i, N_SLOTS)
    ahead = lax.rem(i + 2, N_SLOTS)

    def start(dref, s, c):
        d = pl.multiple_of(jnp.maximum(dref[0, 0, c], 0), RUN)
        dst = pl.multiple_of(RUN * c, RUN)
        pltpu.make_async_copy(y_ref.at[pl.ds(d, RUN), :], buf.at[s, pl.ds(dst, RUN), :], sems.at[s]).start()

    def drain(s):
        pltpu.make_async_copy(y_ref.at[pl.ds(0, RL), :], buf.at[s], sems.at[s]).wait()

    @pl.when(i == 0)
    def _():
        lax.fori_loop(0, RL // RUN, lambda c, carry: (start(d0_ref, 0, c), carry)[1], 0, unroll=4)
        lax.fori_loop(0, RL // RUN, lambda c, carry: (start(d1_ref, 1, c), carry)[1], 0, unroll=4)

    drain(slot)

    for c in range(RL // RUN):
        start(d2_ref, ahead, c)

    g = _pair_matrix(lp_ref[...], gate_ref[...]).astype(BF16)
    halves = [_dot(g, yb) for yb in _unpack_bf16_pairs(buf[slot])]
    xo = x1_ref[...] + g2_ref[0] * jnp.concatenate(halves, axis=1)
    ms = jnp.mean(xo * xo, axis=-1, keepdims=True)
    o_ref[...] = xo * lax.rsqrt(ms + EPS) * fn_ref[...]

    @pl.when(i == n - 1)
    def _():
        drain(lax.rem(i + 1, N_SLOTS))
        drain(ahead)


def _combine_call(cd, y, lp, x1, gates, g2, fn):
    T, d = x1.shape
    n = T // TB
    per_batch = n // g2.shape[0]
    return pl.pallas_call(
        _combine_kernel,
        grid=(n,),
        in_specs=[pl.BlockSpec((1, 1, CD_LANES), lambda i: (i, 0, 0), memory_space=pltpu.SMEM),
                  pl.BlockSpec((1, 1, CD_LANES), lambda i: (jnp.minimum(i + 1, n - 1), 0, 0),
                               memory_space=pltpu.SMEM),
                  pl.BlockSpec((1, 1, CD_LANES), lambda i: (jnp.minimum(i + 2, n - 1), 0, 0),
                               memory_space=pltpu.SMEM),
                  pl.BlockSpec(memory_space=pl.ANY),
                  pl.BlockSpec((TB, LANES), lambda i: (i, 0)),
                  pl.BlockSpec((TB, d), lambda i: (i, 0)),
                  pl.BlockSpec((TB, LANES), lambda i: (i, 0)),
                  pl.BlockSpec((1, 1, d), lambda i: (i // per_batch, 0, 0)),
                  pl.BlockSpec((1, d), lambda i: (0, 0))],
        out_specs=pl.BlockSpec((TB, d), lambda i: (i, 0)),
        out_shape=jax.ShapeDtypeStruct((T, d), F32),
        scratch_shapes=[pltpu.VMEM((N_SLOTS, RL, D_HALF), jnp.uint32), pltpu.SemaphoreType.DMA((N_SLOTS,))],
        compiler_params=_params("arbitrary"),
    )(cd, cd, cd, y, lp, x1, gates, g2, fn)


def _layer(x, c, ctx, c_ctx, w_mod, b_mod, norm1, w_in, gla_w_gk_up, gla_b_gk, gla_norm,
           ssd_conv_w, ssd_conv_b, ssd_dt_bias, ssd_A_log, ssd_D, ssd_norm, w_out,
           norm2, w_router, b_router, w_gate_up, b_gate_up, w_down, b_down, final_norm):
    bsz, L, d = x.shape
    lc = ctx.shape[1]
    assert lc == TB and L % TB == 0 and TB % GRID_W == 0

    cin = jnp.zeros((8, d), F32).at[:bsz].set(c).at[bsz].set(c_ctx)
    mod = _mod_call(cin, w_mod, b_mod.reshape(1, -1))[:bsz + 1]
    sh1, sc1, g1, sh2, sc2, g2 = [m.reshape(bsz + 1, 1, d) for m in jnp.split(mod, 6, axis=-1)]

    o = np.cumsum((0, GLA_QK, GLA_QK, GLA_V, GLA_V, GLA_RANK, SSD_INNER, SSD_CONV_DIM, SSD_HEADS))
    wq, wk, wv, wg, wlow, wz, wx, wdt = [w_in[:, int(a):int(b)] for a, b in zip(o[:-1], o[1:])]
    w_misc = jnp.concatenate([wlow, wdt, wdt, jnp.zeros((d, LANES - GLA_RANK - 2 * SSD_HEADS), F32)], axis=1)
    w_cat = jnp.concatenate([wq, wk, wv, wg, wz, wx, w_misc], axis=1).astype(BF16)
    wup = jnp.zeros((LANES, 2 * GLA_QK), F32).at[:GLA_RANK].set(
        jnp.concatenate([gla_w_gk_up[0], gla_w_gk_up[1]], axis=1)).astype(BF16)
    bup = jnp.concatenate([gla_b_gk[0], gla_b_gk[1]]).reshape(1, -1)
    dtb = jnp.zeros((1, LANES), F32).at[0, DT_F:DT_F + SSD_HEADS].set(ssd_dt_bias[0]) \
                                    .at[0, DT_B:DT_B + SSD_HEADS].set(ssd_dt_bias[1])
    q, k, v, g_all, z_all, xbc, ld, misc = _inproj_call(
        x, ctx, sh1, sc1, norm1.reshape(1, d), w_cat, wup, bup, dtb)

    xs, bc = _conv_call(xbc, ssd_conv_w.reshape(9, SSD_CONV_DIM), ssd_conv_b.reshape(1, -1), lc + L)

    a_neg = -jnp.exp(ssd_A_log.astype(F32))
    a_f = jnp.zeros((1, LANES), F32).at[0, DT_F:DT_F + SSD_HEADS].set(a_neg[0])
    a_b = jnp.zeros((1, LANES), F32).at[0, DT_B:DT_B + SSD_HEADS].set(a_neg[1])
    dvec = jnp.repeat(ssd_D, SSD_HEADDIM).reshape(1, SSD_INNER)
    o_f, o_b, y_f, y_b = _scan_call(q, k, v, ld, xs, bc, misc, a_f, a_b, dvec, L)

    wr = jnp.zeros((d, LANES), F32).at[:, :N_EXPERTS].set(w_router)
    wr_hi = wr.astype(BF16)
    wr_lo = (wr - wr_hi.astype(F32)).astype(BF16)
    br = jnp.full((1, LANES), NEG_BIG, F32).at[0, :N_EXPERTS].set(b_router)
    x1, h2, eidx, gates, counts = _outproj_call(
        o_f, o_b, g_all, y_f, y_b, z_all, x, g1[:bsz], sh2[:bsz], sc2[:bsz],
        jnp.tile(gla_norm, GLA_HEADS).reshape(1, -1), ssd_norm.reshape(1, -1), w_out.astype(BF16),
        norm2.reshape(1, d), wr_hi, wr_lo, br)

    T = bsz * L
    cnt = counts[0, :N_EXPERTS].astype(jnp.int32)
    padded = ((cnt + MOE_BLOCK - 1) // MOE_BLOCK) * MOE_BLOCK
    pend = jnp.cumsum(padded)
    pstart = pend - padded
    max_rows = T * TOP_K + (T // TB) * N_EXPERTS * (RUN - 1)
    n_blocks = -(-max_rows // MOE_BLOCK) + N_EXPERTS
    blk_start = jnp.arange(n_blocks, dtype=jnp.int32) * MOE_BLOCK
    blk_e = jnp.minimum(jnp.sum(pend[None, :] <= blk_start[:, None], axis=1), N_EXPERTS - 1).astype(jnp.int32)
    n_used = (pend[-1] // MOE_BLOCK).astype(jnp.int32)
    blk_i = jnp.arange(n_blocks, dtype=jnp.int32)
    first = (blk_i < n_used) & ((blk_i == 0) | (blk_e != jnp.roll(blk_e, 1)))
    slot = (jnp.cumsum(first) - 1) & 1
    first_pos = jnp.where(first, blk_i, n_blocks)
    next_first = jnp.roll(lax.cummin(first_pos, reverse=True), -1).at[-1].set(n_blocks)
    nxt = jnp.where(next_first < n_blocks, blk_e[jnp.minimum(next_first, n_blocks - 1)], -1)
    col = lambda v, last: jnp.concatenate([v.astype(jnp.int32), jnp.asarray([last], jnp.int32)])
    blk_info = jnp.stack([col(blk_e, 0).at[-1].set(n_used), col(first, 0), col(slot, 0), col(nxt, -1)])
    n_tail = n_blocks - (T * TOP_K) // MOE_BLOCK
    tail = n_used + jnp.arange(n_tail, dtype=jnp.int32)
    fill_blocks = jnp.concatenate([
        jnp.where(padded > 0, pend // MOE_BLOCK - 1, -1),
        jnp.where(tail < n_blocks, tail, -1)]).astype(jnp.int32)
    pstart_row = jnp.zeros((1, LANES), F32).at[0, :N_EXPERTS].set(pstart.astype(F32))
    lp, cd = _pos_call(eidx.reshape(T, LANES), pstart_row)

    xg = _dispatch_call(fill_blocks, cd, h2.reshape(T, d), lp, n_blocks * MOE_BLOCK)
    y = _moe_call(blk_info, xg, w_gate_up, b_gate_up, w_down, b_down, n_blocks)
    out = _combine_call(cd, y, lp, x1.reshape(T, d), gates.reshape(T, LANES), g2[:bsz], final_norm.reshape(1, d))
    return out.reshape(bsz, L, d)


def kernel(x, c, ctx, c_ctx, w_mod, b_mod, norm1, w_in, gla_w_gk_up, gla_b_gk, gla_norm, ssd_conv_w, ssd_conv_b, ssd_dt_bias, ssd_A_log, ssd_D, ssd_norm, w_out, norm2, w_router, b_router, w_gate_up, b_gate_up, w_down, b_down, final_norm):
    assert w_mod.shape[0] == 1, "single-layer kernel"
    return _layer(x, c, ctx, c_ctx, w_mod[0], b_mod[0], norm1[0], w_in[0], gla_w_gk_up[0], gla_b_gk[0],
                  gla_norm[0], ssd_conv_w[0], ssd_conv_b[0], ssd_dt_bias[0], ssd_A_log[0], ssd_D[0],
                  ssd_norm[0], w_out[0], norm2[0], w_router[0], b_router[0], w_gate_up[0], b_gate_up[0],
                  w_down[0], b_down[0], final_norm)
```

```python
import functools

import numpy as np
import jax
import jax.numpy as jnp
from jax import lax
from jax.experimental import pallas as pl
from jax.experimental.pallas import tpu as pltpu

F32 = jnp.float32
BF16 = jnp.bfloat16

EPS = 1e-6
GRID_W = 64
GLA_HEADS = 4
GLA_DK = 64
GLA_DV = 128
GLA_QK = GLA_HEADS * GLA_DK
GLA_V = GLA_HEADS * GLA_DV
GLA_RANK = 16
GLA_GATE_NORM = 16.0
SSD_HEADDIM = 64
SSD_INNER = 512
SSD_HEADS = 8
SSD_GROUPS = 2
SSD_HPG = 4
SSD_STATE = 128
SSD_CONV_DIM = 1024
N_EXPERTS = 32
TOP_K = 4
D_FF = 1024
SWIGLU_LIMIT = 7.0
SWIGLU_ALPHA = 1.702
MOE_BLOCK = 1024

TB = 256
GLA_C = 64
GLA_STAGE_GROUP = 1
SSD_C = 128
SSD_STAGE_GROUP = 1
LANES = 128
EXP_CLAMP = 80.0
DT_F = 16
DT_B = 24
NEG_BIG = -1e30
RUN = 8
RL = -(-(TB * TOP_K + N_EXPERTS * (RUN - 1)) // LANES) * LANES
CD_LANES = -(-(RL // RUN) // LANES) * LANES
D_HALF = 512
VMEM_LIMIT = 56 * 1024 * 1024


def _dot(a, b):
    return jnp.dot(a, b, preferred_element_type=F32)


def _dot_nt(a, b):
    return lax.dot_general(a, b, (((1,), (1,)), ((), ())), preferred_element_type=F32)


def _dot_tn(a, b):
    return lax.dot_general(a, b, (((0,), (0,)), ((), ())), preferred_element_type=F32)


def _split3(a):
    hi = a.astype(BF16)
    r1 = a - hi.astype(F32)
    mid = r1.astype(BF16)
    lo = (r1 - mid.astype(F32)).astype(BF16)
    return hi, mid, lo


def _dot_exact_r(m, a):
    hi, mid, lo = _split3(a)
    return _dot(m, hi) + _dot(m, mid) + _dot(m, lo)


def _dot_hilo_r(m, a):
    hi = a.astype(BF16)
    lo = (a - hi.astype(F32)).astype(BF16)
    return _dot(m, hi) + _dot(m, lo)


def _dot_hilo_l(a, m2):
    hi = a.astype(BF16)
    lo = (a - hi.astype(F32)).astype(BF16)
    return _dot(jnp.concatenate([hi, lo], axis=1), m2)


def _sigmoid(x):
    return 1.0 / (1.0 + jnp.exp(-x))


def _softplus(x):
    return jnp.maximum(x, 0.0) + jnp.log1p(jnp.exp(-jnp.abs(x)))


def _params(*sem):
    return pltpu.CompilerParams(dimension_semantics=sem, vmem_limit_bytes=VMEM_LIMIT)


def _mod_kernel(c_ref, w_ref, b_ref, o_ref):
    c = c_ref[...]
    s = c * _sigmoid(c)
    s_hi = s.astype(BF16)
    s_lo = (s - s_hi.astype(F32)).astype(BF16)
    w = w_ref[...]
    w_hi = w.astype(BF16)
    w_lo = (w - w_hi.astype(F32)).astype(BF16)
    o_ref[...] = _dot(s_hi, w_hi) + _dot(s_lo, w_hi) + _dot(s_hi, w_lo) + b_ref[...]


def _mod_call(cin, w, b):
    rows, d = cin.shape
    n = w.shape[1]
    tn = 1536
    return pl.pallas_call(
        _mod_kernel,
        grid=(n // tn,),
        in_specs=[pl.BlockSpec((rows, d), lambda i: (0, 0)),
                  pl.BlockSpec((d, tn), lambda i: (0, i)),
                  pl.BlockSpec((1, tn), lambda i: (0, i))],
        out_specs=pl.BlockSpec((rows, tn), lambda i: (0, i)),
        out_shape=jax.ShapeDtypeStruct((rows, n), F32),
        compiler_params=_params("arbitrary"),
    )(cin, w, b)


_C_Q, _C_K, _C_V, _C_G, _C_Z, _C_X, _C_M, _C_END = 0, 256, 512, 1024, 1536, 2048, 3072, 3200


def _inproj_kernel(x0_ref, xa_ref, xb_ref, ctx_ref, shl_ref, scl_ref, shc_ref, scc_ref, n1_ref, w_ref, wup_ref,
                   bup_ref, dtb_ref, q_ref, k_ref, v_ref, g_ref, z_ref, xbc_ref, ld_ref, misc_ref, h_scr):
    j = pl.program_id(1)
    slot = j & 1

    def normmod(xv, sh_ref, sc_ref):
        ms = jnp.mean(xv * xv, axis=-1, keepdims=True)
        y = xv * lax.rsqrt(ms + EPS) * n1_ref[...]
        return (y * (1.0 + sc_ref[0]) + sh_ref[0]).astype(BF16)

    @pl.when(j == 0)
    def _():
        h_scr[0, 0:TB, :] = normmod(ctx_ref[0], shc_ref, scc_ref)
        h_scr[0, TB:2 * TB, :] = normmod(x0_ref[0], shl_ref, scl_ref)

    def mm(lo, hi):
        return _dot(h_scr[slot], w_ref[:, lo:hi])

    m = mm(_C_M, _C_END)
    zz = _dot(m.astype(BF16), wup_ref[...]) + bup_ref[...]
    ld_ref[0] = -_softplus(-zz) * (1.0 / GLA_GATE_NORM)
    misc_ref[0] = _softplus(m + dtb_ref[...])
    q_ref[0] = (mm(_C_Q, _C_K) * (GLA_DK ** -0.5)).astype(BF16)
    h_scr[1 - slot, 0:TB, :] = normmod(xa_ref[0], shl_ref, scl_ref)
    k_ref[0] = mm(_C_K, _C_V).astype(BF16)
    h_scr[1 - slot, TB:2 * TB, :] = normmod(xb_ref[0], shl_ref, scl_ref)
    v_ref[0] = mm(_C_V, _C_G).astype(BF16)
    g_ref[0] = mm(_C_G, _C_Z)
    z_ref[0] = mm(_C_Z, _C_X)
    xbc_ref[0] = mm(_C_X, _C_M)


def _inproj_call(x, ctx, sh1, sc1, n1, w_cat, wup, bup, dtb):
    bsz, L, d = x.shape
    nx = L // TB
    nj = -(-(nx + 1) // 2)
    tok = lambda n: pl.BlockSpec((1, 2 * TB, n), lambda b, j: (b, j, 0))
    const = lambda a: pl.BlockSpec(a.shape, lambda b, j: (0,) * a.ndim)
    xblk = lambda off: pl.BlockSpec((1, TB, d), lambda b, j: (b, jnp.minimum(2 * j + off, nx - 1), 0))
    first = pl.BlockSpec((1, TB, d), lambda b, j: (b, 0, 0))
    mod_lat = pl.BlockSpec((1, 1, d), lambda b, j: (b, 0, 0))
    mod_ctx = pl.BlockSpec((1, 1, d), lambda b, j: (bsz, 0, 0))
    outs = [(GLA_QK, BF16), (GLA_QK, BF16), (GLA_V, BF16), (GLA_V, F32), (SSD_INNER, F32),
            (SSD_CONV_DIM, F32), (2 * GLA_QK, F32), (LANES, F32)]
    return pl.pallas_call(
        _inproj_kernel,
        grid=(bsz, nj),
        in_specs=[first, xblk(1), xblk(2), first,
                  mod_lat, mod_lat, mod_ctx, mod_ctx,
                  const(n1), const(w_cat), const(wup), const(bup), const(dtb)],
        out_specs=[tok(n) for n, _ in outs],
        out_shape=[jax.ShapeDtypeStruct((bsz, nj * 2 * TB, n), dt) for n, dt in outs],
        scratch_shapes=[pltpu.VMEM((2, 2 * TB, d), BF16)],
        compiler_params=_params("arbitrary", "arbitrary"),
    )(x, x, x, ctx, sh1, sc1, sh1, sc1, n1, w_cat, wup, bup, dtb)


_EXT_PAD = 8
_EXT_BASE = _EXT_PAD + GRID_W
_EXT_ROWS = 2 * _EXT_PAD + 2 * GRID_W + TB


def _conv_kernel(prev_ref, cur_ref, next_ref, w_ref, b_ref, xs_ref, bc_ref, ext_s, xl_all, xr_all):
    for bb in range(cur_ref.shape[0]):
        _conv_block(prev_ref.at[bb], cur_ref.at[bb], next_ref.at[bb], w_ref, b_ref, xs_ref.at[bb], bc_ref.at[bb],
                    ext_s.at[bb], xl_all.at[bb], xr_all.at[bb])


def _conv_block(prev_ref, cur_ref, next_ref, w_ref, b_ref, xs_ref, bc_ref, ext, xl_s, xr_s):
    j = pl.program_id(1)
    nj = pl.num_programs(1)
    is_ctx = j == 0
    zpad = jnp.zeros((_EXT_PAD, SSD_CONV_DIM), F32)
    ext[0:_EXT_PAD, :] = zpad
    ext[_EXT_ROWS - _EXT_PAD:_EXT_ROWS, :] = zpad
    ext[_EXT_PAD:_EXT_BASE, :] = jnp.where(j >= 2, prev_ref[...], 0.0)
    ext[_EXT_BASE:_EXT_BASE + TB, :] = cur_ref[...]
    ext[_EXT_BASE + TB:_EXT_BASE + TB + GRID_W, :] = jnp.where(
        jnp.logical_and(j >= 1, j <= nj - 2), next_ref[...], 0.0)

    win = TB + 2 * GRID_W
    u = lax.broadcasted_iota(jnp.int32, (win, LANES), 0)
    pos = jnp.where(is_ctx, u - GRID_W, u & (GRID_W - 1))
    ok_l = pos >= 1
    ok_r = jnp.where(is_ctx, pos - (TB - GRID_W), pos) <= GRID_W - 2
    lat = jnp.where(is_ctx, 0.0, 1.0)
    side = {-1: xl_s, 0: None, 1: xr_s}

    for c in range(SSD_CONV_DIM // LANES):
        lo, hi = c * LANES, (c + 1) * LANES
        xl_s[:, lo:hi] = jnp.where(ok_l, ext[_EXT_PAD - 1:_EXT_PAD - 1 + win, lo:hi], 0.0)
        xr_s[:, lo:hi] = jnp.where(ok_r, ext[_EXT_PAD + 1:_EXT_PAD + 1 + win, lo:hi], 0.0)
        acc = jnp.zeros((TB, LANES), F32)
        for dr in (-1, 0, 1):
            for dc in (-1, 0, 1):
                if dc == 0:
                    start = _EXT_BASE + GRID_W * dr
                    tap = ext[start:start + TB, lo:hi]
                else:
                    start = GRID_W + GRID_W * dr
                    tap = side[dc][start:start + TB, lo:hi]
                wi = 3 * (dr + 1) + (dc + 1)
                wv = w_ref[wi:wi + 1, lo:hi]
                if dr != 0:
                    wv = wv * lat
                acc = acc + tap * wv
        y = acc + b_ref[:, lo:hi]
        y = y * _sigmoid(y)
        if c < SSD_INNER // LANES:
            xs_ref[:, lo:hi] = y
        else:
            bc_ref[:, lo - SSD_INNER:hi - SSD_INNER] = y.astype(BF16)


def _conv_call(xbc, w9, bias, ls):
    bsz, _, ch = xbc.shape
    nj = ls // TB
    rpb = TB // GRID_W
    nrow = ls // GRID_W
    nb = next(n for n in (4, 2, 1) if bsz % n == 0)
    return pl.pallas_call(
        _conv_kernel,
        grid=(bsz // nb, nj),
        in_specs=[pl.BlockSpec((nb, GRID_W, ch), lambda b, j: (b, jnp.maximum(rpb * j - 1, 0), 0)),
                  pl.BlockSpec((nb, TB, ch), lambda b, j: (b, j, 0)),
                  pl.BlockSpec((nb, GRID_W, ch), lambda b, j: (b, jnp.minimum(rpb * j + rpb, nrow - 1), 0)),
                  pl.BlockSpec((9, ch), lambda b, j: (0, 0)),
                  pl.BlockSpec((1, ch), lambda b, j: (0, 0))],
        out_specs=[pl.BlockSpec((nb, TB, SSD_INNER), lambda b, j: (b, j, 0)),
                   pl.BlockSpec((nb, TB, ch - SSD_INNER), lambda b, j: (b, j, 0))],
        out_shape=[jax.ShapeDtypeStruct((bsz, ls, SSD_INNER), F32),
                   jax.ShapeDtypeStruct((bsz, ls, ch - SSD_INNER), BF16)],
        scratch_shapes=[pltpu.VMEM((nb, _EXT_ROWS, ch), F32), pltpu.VMEM((nb, TB + 2 * GRID_W, ch), F32),
                        pltpu.VMEM((nb, TB + 2 * GRID_W, ch), F32)],
        compiler_params=_params("arbitrary", "arbitrary"),
    )(xbc, xbc, xbc, w9, bias)


def _fwd_blk(s):
    return s


def _bwd_blk(s, ns):
    return jnp.where(s == 0, 0, ns - s)


def _scan_batch(bsz):
    return 2 if bsz % 2 == 0 else 1


class _GlaChunk:
    def __init__(self, q, k, v, la, st_ref, tri_m, fwd, store):
        self.q, self.k, self.v, self.la, self.st_ref, self.tri_m, self.fwd, self.store = (
            q, k, v, la, st_ref, tri_m, fwd, store)

    def stage_sums(self):
        self.b = _dot_hilo_r(self.tri_m, self.la)

    def stage_factors(self):
        C = GLA_C
        b = self.b
        self.bt = b[C - 1:C, :] if self.fwd else b[0:1, :]
        r = 0.5 * self.bt
        self.er = jnp.exp(r)
        self.qt = (self.q.astype(F32) * jnp.exp(jnp.minimum(b - r, EXP_CLAMP))).astype(BF16)
        kt = (self.k.astype(F32) * jnp.exp(jnp.minimum(r - b, EXP_CLAMP))).astype(BF16)
        head_k = lax.broadcasted_iota(jnp.int32, (C, GLA_QK), 1) >> 6
        zero = jnp.zeros_like(kt)
        self.kh = [jnp.where(head_k == h, kt, zero) for h in range(GLA_HEADS)]
        self.qh = [jnp.where(head_k == h, self.qt, zero) for h in range(GLA_HEADS)]

    def stage_products(self):
        v = self.v
        kcat = jnp.concatenate(self.kh, axis=0)
        self.sc = _dot_nt(self.qt, kcat)
        vcat = jnp.concatenate([v[:, h * GLA_DV:(h + 1) * GLA_DV] for h in range(GLA_HEADS)], axis=0)
        self.u = _dot_tn(vcat, kcat) * self.er

    def stage_mask(self):
        C = GLA_C
        ii = lax.broadcasted_iota(jnp.int32, (C, GLA_HEADS * C), 0)
        jj = lax.broadcasted_iota(jnp.int32, (C, GLA_HEADS * C), 1) & (C - 1)
        causal = (jj <= ii) if self.fwd else (jj >= ii)
        self.p = jnp.where(causal, self.sc, 0.0).astype(BF16)
        v = self.v
        head_v = lax.broadcasted_iota(jnp.int32, (C, GLA_V), 1) >> 7
        self.vst = jnp.concatenate([jnp.where(head_v == h, v, jnp.zeros_like(v)) for h in range(GLA_HEADS)], axis=0)

    def stage_intra(self):
        self.o = _dot(self.p, self.vst)

    def stage_inter(self):
        st = self.st_ref[...]
        ster = (st * self.er).astype(BF16)
        self.st_ref[...] = st * jnp.exp(self.bt) + self.u
        res = _dot_nt(jnp.concatenate(self.qh, axis=0), ster)
        self.inter = jnp.concatenate([res[h * GLA_C:(h + 1) * GLA_C, :] for h in range(GLA_HEADS)], axis=1)

    def stage_out(self):
        self.store(self.o + self.inter)


def _store_to(ref, bb, sl):
    def store(val):
        ref[bb, sl, :] = val
    return store


def _gla_schedule(qf_ref, kf_ref, vf_ref, lf_ref, qb_ref, kb_ref, vb_ref, lb_ref, trif_ref, trib_ref,
                  of_ref, ob_ref, stf, stb):
    nsub = TB // GLA_C
    store_to = _store_to
    calls = []

    group = GLA_STAGE_GROUP
    for g0 in range(0, nsub, group):
        steps = []
        for i in range(g0, g0 + group):
            sf = pl.ds(i * GLA_C, GLA_C)
            sb = pl.ds((nsub - 1 - i) * GLA_C, GLA_C)
            chunks = []
            for bb in range(qf_ref.shape[0]):
                chunks.append(_GlaChunk(qf_ref[bb, sf, :], kf_ref[bb, sf, :], vf_ref[bb, sf, :], lf_ref[bb, sf, :],
                                        stf.at[bb], trif_ref[...], True, store_to(of_ref, bb, sf)))
                chunks.append(_GlaChunk(qb_ref[bb, sb, :], kb_ref[bb, sb, :], vb_ref[bb, sb, :], lb_ref[bb, sb, :],
                                        stb.at[bb], trib_ref[...], False, store_to(ob_ref, bb, sb)))
            steps.append(chunks)
        for stage in ("stage_sums", "stage_factors", "stage_products", "stage_mask", "stage_intra"):
            for chunks in steps:
                calls += [getattr(ch, stage) for ch in chunks]
        for chunks in steps:
            for stage in ("stage_inter", "stage_out"):
                calls += [getattr(ch, stage) for ch in chunks]
    return calls


class _SsdChunk:
    def __init__(self, xs, bc, dtm, avec, dvec, st_ref, tri_m, e_m, base, fwd, store):
        self.xs, self.bc, self.dtm, self.avec, self.dvec, self.st_ref = xs, bc, dtm, avec, dvec, st_ref
        self.tri_m, self.e_m, self.base, self.fwd, self.store = tri_m, e_m, base, fwd, store

    def stage_sums(self):
        self.dt_exp = _dot_hilo_l(self.dtm, self.e_m)
        self.acum = _dot_exact_r(self.tri_m, self.dtm * self.avec)

    def stage_expand(self):
        self.acum_exp = _dot_hilo_l(self.acum, self.e_m)
        self.acum_t = self.acum.T
        self.xdt = self.xs * self.dt_exp
        bc = self.bc
        self.bg = [bc[:, 128 * g:128 * (g + 1)] for g in range(SSD_GROUPS)]
        self.cg = [bc[:, 256 + 128 * g:256 + 128 * (g + 1)] for g in range(SSD_GROUPS)]
        self.cb = [_dot_nt(self.cg[g], self.bg[g]) for g in range(SSD_GROUPS)]

    def stage_decay(self):
        C = SSD_C
        ii = lax.broadcasted_iota(jnp.int32, (C, C), 0)
        jj = lax.broadcasted_iota(jnp.int32, (C, C), 1)
        tri = (jj <= ii) if self.fwd else (jj >= ii)
        self.ms = []
        for g in range(SSD_GROUPS):
            for rr in range(SSD_HPG):
                ln = self.base + SSD_HPG * g + rr
                diff = self.acum[:, ln:ln + 1] - self.acum_t[ln:ln + 1, :]
                seg = jnp.where(tri, jnp.exp(jnp.minimum(diff, 0.0)), 0.0)
                self.ms.append((self.cb[g] * seg).astype(BF16))
        ae = self.acum_exp
        self.al_exp = ae[C - 1:C, :] if self.fwd else ae[0:1, :]
        self.xw = (self.xdt * jnp.exp(self.al_exp - ae)).astype(BF16)
        xdt_b = self.xdt.astype(BF16)
        gw = SSD_HPG * SSD_HEADDIM
        head = lax.broadcasted_iota(jnp.int32, (C, gw), 1) >> 6
        zero = jnp.zeros((C, gw), BF16)
        self.xh = [jnp.where(head == h % SSD_HPG, xdt_b[:, gw * (h // SSD_HPG):gw * (h // SSD_HPG + 1)], zero)
                   for h in range(SSD_HEADS)]

    def stage_products(self):
        self.yg, self.ug = [], []
        for g in range(SSD_GROUPS):
            gl, gh = 256 * g, 256 * (g + 1)
            yg = _dot(self.ms[SSD_HPG * g], self.xh[SSD_HPG * g])
            for rr in range(1, SSD_HPG):
                yg = yg + _dot(self.ms[SSD_HPG * g + rr], self.xh[SSD_HPG * g + rr])
            self.yg.append(yg)
            self.ug.append(_dot_tn(self.bg[g], self.xw[:, gl:gh]))

    def stage_state(self):
        ys = []
        for g in range(SSD_GROUPS):
            gl, gh = 256 * g, 256 * (g + 1)
            sg = self.st_ref[g]
            yoff = _dot(self.cg[g], sg.astype(BF16)) * jnp.exp(self.acum_exp[:, gl:gh])
            self.st_ref[g] = sg * jnp.exp(self.al_exp[:, gl:gh]) + self.ug[g]
            ys.append(self.yg[g] + yoff)
        y = jnp.concatenate(ys, axis=1)
        if self.dvec is not None:
            y = y + self.dvec * self.xs
        self.store(y)


def _ssd_schedule(xf_ref, bcf_ref, mf_ref, xb_ref, bcb_ref, mb_ref, af_ref, ab_ref, d_ref,
                  trif_ref, trib_ref, ef_ref, eb_ref, yf_ref, yb_ref, stf, stb):
    nsub = TB // SSD_C
    store_to = _store_to
    calls = []
    steps = []
    for i in range(nsub):
        sf = pl.ds(i * SSD_C, SSD_C)
        sb = pl.ds((nsub - 1 - i) * SSD_C, SSD_C)
        chunks = []
        for bb in range(xf_ref.shape[0]):
            chunks.append(_SsdChunk(xf_ref[bb, sf, :], bcf_ref[bb, sf, :], mf_ref[bb, sf, :], af_ref[...],
                                    d_ref[...], stf.at[bb], trif_ref[...], ef_ref[...], DT_F, True,
                                    store_to(yf_ref, bb, sf)))
            chunks.append(_SsdChunk(xb_ref[bb, sb, :], bcb_ref[bb, sb, :], mb_ref[bb, sb, :], ab_ref[...],
                                    None, stb.at[bb], trib_ref[...], eb_ref[...], DT_B, False,
                                    store_to(yb_ref, bb, sb)))
        steps.append(chunks)
    for g0 in range(0, nsub, SSD_STAGE_GROUP):
        group = steps[g0:g0 + SSD_STAGE_GROUP]
        for stage in ("stage_sums", "stage_expand", "stage_decay", "stage_products"):
            for chunks in group:
                calls += [getattr(ch, stage) for ch in chunks]
        for chunks in group:
            calls += [ch.stage_state for ch in chunks]
    return calls


N_GLA_IN, N_SSD_IN = 10, 13


def _scan_kernel(*refs):
    gla_in, ssd_in = refs[:N_GLA_IN], refs[N_GLA_IN:N_GLA_IN + N_SSD_IN]
    of_ref, ob_ref, yf_ref, yb_ref, gstf, gstb, sstf, sstb = refs[N_GLA_IN + N_SSD_IN:]

    @pl.when(pl.program_id(1) == 0)
    def _():
        for st in (gstf, gstb, sstf, sstb):
            st[...] = jnp.zeros_like(st)

    gla = _gla_schedule(*gla_in, of_ref, ob_ref, gstf, gstb)
    ssd = _ssd_schedule(*ssd_in, yf_ref, yb_ref, sstf, sstb)
    merged = sorted([((n + 0.5) / len(gla), 0, n, c) for n, c in enumerate(gla)] +
                    [((n + 0.5) / len(ssd), 1, n, c) for n, c in enumerate(ssd)], key=lambda t: t[:3])
    for _, _, _, call in merged:
        call()


def _expand_matrix(base):
    e = np.zeros((LANES, SSD_INNER), np.float32)
    for h in range(SSD_HEADS):
        e[base + h, SSD_HEADDIM * h:SSD_HEADDIM * (h + 1)] = 1.0
    return jnp.asarray(np.concatenate([e, e], axis=0), BF16)


def _scan_call(q, k, v, ld, xs, bc, misc, a_f, a_b, dvec, L):
    bsz = q.shape[0]
    nx = L // TB
    ns = nx + 1
    tri = lambda c, up: jnp.asarray((np.triu if up else np.tril)(np.ones((c, c), np.float32)), BF16)
    g_trif, g_trib, s_trif, s_trib = tri(GLA_C, False), tri(GLA_C, True), tri(SSD_C, False), tri(SSD_C, True)
    ef, eb = _expand_matrix(DT_F), _expand_matrix(DT_B)
    nb = _scan_batch(bsz)
    f = lambda n, lane=0: pl.BlockSpec((nb, TB, n), lambda b, s: (b, _fwd_blk(s), lane))
    r = lambda n, lane=0: pl.BlockSpec((nb, TB, n), lambda b, s: (b, _bwd_blk(s, ns), lane))
    const = lambda a: pl.BlockSpec(a.shape, lambda b, s: (0,) * a.ndim)
    out_f = lambda n: pl.BlockSpec((nb, TB, n), lambda b, s: (b, jnp.maximum(s - 1, 0), 0))
    out_b = lambda n: pl.BlockSpec((nb, TB, n), lambda b, s: (b, jnp.where(s == 0, nx - 1, nx - s), 0))
    gla_specs = [f(GLA_QK), f(GLA_QK), f(GLA_V), f(GLA_QK, 0), r(GLA_QK), r(GLA_QK), r(GLA_V), r(GLA_QK, 1),
                 const(g_trif), const(g_trib)]
    ssd_specs = [f(SSD_INNER), f(512), f(LANES), r(SSD_INNER), r(512), r(LANES),
                 const(a_f), const(a_b), const(dvec), const(s_trif), const(s_trib), const(ef), const(eb)]
    assert len(gla_specs) == N_GLA_IN and len(ssd_specs) == N_SSD_IN
    return pl.pallas_call(
        _scan_kernel,
        grid=(bsz // nb, ns),
        in_specs=gla_specs + ssd_specs,
        out_specs=[out_f(GLA_V), out_b(GLA_V), out_f(SSD_INNER), out_b(SSD_INNER)],
        out_shape=[jax.ShapeDtypeStruct((bsz, L, GLA_V), F32)] * 2 +
                  [jax.ShapeDtypeStruct((bsz, L, SSD_INNER), F32)] * 2,
        scratch_shapes=[pltpu.VMEM((nb, GLA_DV, GLA_QK), F32)] * 2 +
                       [pltpu.VMEM((nb, SSD_GROUPS, SSD_STATE, SSD_HPG * SSD_HEADDIM), F32)] * 2,
        compiler_params=_params("arbitrary", "arbitrary"),
    )(q, k, v, ld, q, k, v, ld, g_trif, g_trib, xs, bc, misc, xs, bc, misc, a_f, a_b, dvec, s_trif, s_trib, ef, eb)


def _outproj_kernel(of_ref, ob_ref, ga_ref, gb_ref, yf_ref, yb_ref, za_ref, zb_ref, x_ref, g1_ref, sh2_ref, sc2_ref,
                    gn_ref, sn_ref, wo_ref, n2_ref, wrh_ref, wrl_ref, br_ref,
                    x1_ref, h2_ref, eidx_ref, gate_ref, cnt_ref, mix_s, hl_s):
    first = jnp.logical_and(pl.program_id(0) == 0, pl.program_id(1) == 0)

    @pl.when(first)
    def _():
        cnt_ref[...] = jnp.zeros_like(cnt_ref)

    halves = [(pl.ds(0, TB), ga_ref, za_ref), (pl.ds(TB, TB), gb_ref, zb_ref)]

    for rows, g_ref, z_ref in halves:
        o = of_ref[0, rows, :] + ob_ref[0, rows, :]
        gg = g_ref[0]
        for h in range(GLA_HEADS):
            lo, hi = GLA_DV * h, GLA_DV * (h + 1)
            oh = o[:, lo:hi]
            ms = jnp.mean(oh * oh, axis=-1, keepdims=True)
            gh = gg[:, lo:hi]
            mix_s[rows, lo:hi] = (oh * lax.rsqrt(ms + EPS) * gn_ref[:, lo:hi] * (gh * _sigmoid(gh))).astype(BF16)
        zz = z_ref[0]
        u = (yf_ref[0, rows, :] + yb_ref[0, rows, :]) * (zz * _sigmoid(zz))
        gw = SSD_INNER // SSD_GROUPS
        for g in range(SSD_GROUPS):
            lo, hi = gw * g, gw * (g + 1)
            ug = u[:, lo:hi]
            ms = jnp.mean(ug * ug, axis=-1, keepdims=True)
            mix_s[rows, GLA_V + lo:GLA_V + hi] = (ug * lax.rsqrt(ms + EPS) * sn_ref[:, lo:hi]).astype(BF16)

    x1_ref[0] = x_ref[0] + g1_ref[0] * _dot(mix_s[...], wo_ref[...])

    for rows, _, _ in halves:
        x1 = x1_ref[0, rows, :]
        ms = jnp.mean(x1 * x1, axis=-1, keepdims=True)
        h2 = x1 * lax.rsqrt(ms + EPS) * (n2_ref[...] * (1.0 + sc2_ref[0])) + sh2_ref[0]
        h_hi = h2.astype(BF16)
        h2_ref[0, rows, :] = h_hi
        hl_s[rows, :] = (h2 - h_hi.astype(F32)).astype(BF16)

    h_hi = h2_ref[0]
    logits = (_dot(h_hi, wrh_ref[...]) + _dot(hl_s[...], wrh_ref[...]) + _dot(h_hi, wrl_ref[...])) + br_ref[...]

    lane = lax.broadcasted_iota(jnp.int32, (TB, LANES), 1).astype(F32)
    st = [dict(work=logits[i * TB:(i + 1) * TB, :], eidx=jnp.full((TB, LANES), -1.0, F32),
               gates=jnp.zeros((TB, LANES), F32), sel=jnp.zeros((TB, LANES), F32), m0=None,
               den=jnp.zeros((TB, 1), F32)) for i in range(2)]
    for kk in range(TOP_K):
        for t in st:
            t["m"] = jnp.max(t["work"], axis=-1, keepdims=True)
        for t in st:
            t["idx"] = jnp.min(jnp.where(t["work"] == t["m"], lane, float(LANES)), axis=-1, keepdims=True)
        for t in st:
            hit = lane == t["idx"]
            if t["m0"] is None:
                t["m0"] = t["m"]
            e = jnp.exp(t["m"] - t["m0"])
            t["den"] = t["den"] + e
            t["eidx"] = jnp.where(lane == float(kk), t["idx"], t["eidx"])
            t["gates"] = jnp.where(lane == float(kk), e, t["gates"])
            t["sel"] = jnp.where(hit, 1.0, t["sel"])
            t["work"] = jnp.where(hit, NEG_BIG, t["work"])
    for (rows, _, _), t in zip(halves, st):
        eidx_ref[0, rows, :] = t["eidx"].astype(jnp.int32)
        gate_ref[0, rows, :] = t["gates"] / t["den"]
        cnt = jnp.sum(t["sel"], axis=0, keepdims=True)
        cnt_ref[...] += jnp.floor((cnt + (RUN - 1.0)) * (1.0 / RUN)) * RUN


def _outproj_call(o_f, o_b, g_all, y_f, y_b, z_all, x, g1, sh2, sc2, gn, sn, wo, n2, wr_hi, wr_lo, br):
    bsz, L, d = x.shape
    nj = L // (2 * TB)
    tok = lambda n: pl.BlockSpec((1, 2 * TB, n), lambda b, j: (b, j, 0))
    tok_off = lambda n, half: pl.BlockSpec((1, TB, n), lambda b, j: (b, 2 * j + 1 + half, 0))
    const = lambda a: pl.BlockSpec(a.shape, lambda b, j: (0,) * a.ndim)
    mod = pl.BlockSpec((1, 1, d), lambda b, j: (b, 0, 0))
    return pl.pallas_call(
        _outproj_kernel,
        grid=(bsz, nj),
        in_specs=[tok(GLA_V), tok(GLA_V), tok_off(GLA_V, 0), tok_off(GLA_V, 1),
                  tok(SSD_INNER), tok(SSD_INNER), tok_off(SSD_INNER, 0), tok_off(SSD_INNER, 1),
                  tok(d), mod, mod, mod, const(gn), const(sn), const(wo), const(n2),
                  const(wr_hi), const(wr_lo), const(br)],
        out_specs=[tok(d), tok(d), tok(LANES), tok(LANES), pl.BlockSpec((1, LANES), lambda b, j: (0, 0))],
        out_shape=[jax.ShapeDtypeStruct((bsz, L, d), F32), jax.ShapeDtypeStruct((bsz, L, d), BF16),
                   jax.ShapeDtypeStruct((bsz, L, LANES), jnp.int32), jax.ShapeDtypeStruct((bsz, L, LANES), F32),
                   jax.ShapeDtypeStruct((1, LANES), F32)],
        scratch_shapes=[pltpu.VMEM((2 * TB, d), BF16), pltpu.VMEM((2 * TB, d), BF16)],
        compiler_params=_params("arbitrary", "arbitrary"),
    )(o_f, o_b, g_all, g_all, y_f, y_b, z_all, z_all, x, g1, sh2, sc2, gn, sn, wo, n2, wr_hi, wr_lo, br)


def _pos_kernel(eidx_ref, pstart_ref, lst_ref, ust_ref, lp_ref, cd_ref, carry):
    @pl.when(pl.program_id(0) == 0)
    def _():
        carry[...] = pstart_ref[...]

    lane = lax.broadcasted_iota(jnp.int32, (TB, LANES), 1)
    tiles = []
    for t in range(cd_ref.shape[0]):
        eidx = eidx_ref[t * TB:(t + 1) * TB, :]
        hits = [lane == eidx[:, kk:kk + 1] for kk in range(TOP_K)]
        sel = jnp.zeros((TB, LANES), F32)
        for hmask in hits:
            sel = jnp.where(hmask, 1.0, sel)
        cnt = jnp.sum(sel, axis=0, keepdims=True)
        run = jnp.floor((cnt + (RUN - 1.0)) * (1.0 / RUN)) * RUN
        tiles.append(dict(hits=hits, sel=sel, run=run))
    for tl in tiles:
        tl["rank"] = _dot(lst_ref[...], tl["sel"].astype(BF16))
        tl["loff"] = _dot(jnp.broadcast_to(tl["run"], (8, LANES)).astype(BF16), ust_ref[...])[0:1]
    for t, tl in enumerate(tiles):
        pos = tl["loff"] + tl["rank"]
        lp = jnp.zeros((TB, LANES), jnp.int32)
        for kk, hmask in enumerate(tl["hits"]):
            lk = jnp.sum(jnp.where(hmask, pos, 0.0), axis=-1, keepdims=True)
            lp = jnp.where(lane == kk, lk.astype(jnp.int32), lp)
        lp_ref[t * TB:(t + 1) * TB, :] = lp

    eye = lax.broadcasted_iota(jnp.int32, (LANES, LANES), 0) == lax.broadcasted_iota(jnp.int32, (LANES, LANES), 1)
    col = lambda v: jnp.sum(jnp.where(eye, v, 0.0), axis=1, keepdims=True)
    row0 = (lax.broadcasted_iota(jnp.int32, (LANES, CD_LANES), 1) * RUN).astype(F32)
    base = carry[...]
    for t, tl in enumerate(tiles):
        loff_c, run_c, shift_c = col(tl["loff"]), col(tl["run"]), col(base - tl["loff"])
        inside = jnp.where(row0 >= loff_c, 1.0, 0.0) * jnp.where(row0 < loff_c + run_c, 1.0, 0.0)
        valid = jnp.sum(inside, axis=0, keepdims=True)
        dest = jnp.sum(inside * shift_c, axis=0, keepdims=True) + row0[0:1]
        cd_ref[t] = jnp.where(valid > 0.0, dest, -1.0).astype(jnp.int32)
        base = base + tl["run"]
    carry[...] = base


def _pos_call(eidx, pstart):
    T = eidx.shape[0]
    lst = jnp.asarray(np.tril(np.ones((TB, TB), np.float32), -1), BF16)
    ust = jnp.asarray(np.triu(np.ones((LANES, LANES), np.float32), 1), BF16)
    tps = next(n for n in (4, 2, 1) if (T // TB) % n == 0)
    return pl.pallas_call(
        _pos_kernel,
        grid=(T // (tps * TB),),
        in_specs=[pl.BlockSpec((tps * TB, LANES), lambda i: (i, 0)),
                  pl.BlockSpec((1, LANES), lambda i: (0, 0)),
                  pl.BlockSpec((TB, TB), lambda i: (0, 0)),
                  pl.BlockSpec((LANES, LANES), lambda i: (0, 0))],
        out_specs=[pl.BlockSpec((tps * TB, LANES), lambda i: (i, 0)),
                   pl.BlockSpec((tps, 1, CD_LANES), lambda i: (i, 0, 0))],
        out_shape=[jax.ShapeDtypeStruct((T, LANES), jnp.int32),
                   jax.ShapeDtypeStruct((T // TB, 1, CD_LANES), jnp.int32)],
        scratch_shapes=[pltpu.VMEM((1, LANES), F32)],
        compiler_params=_params("arbitrary"),
    )(eidx, pstart, lst, ust)


def _pair_matrix(lp, weights):
    j = lax.broadcasted_iota(jnp.int32, (TB, RL), 1)
    m = jnp.zeros((TB, RL), F32)
    for kk in range(TOP_K):
        w = 1.0 if weights is None else weights[:, kk:kk + 1]
        m = jnp.where(j == lp[:, kk:kk + 1], w, m)
    return m


def _pack_bf16_pairs(v, is_bf16_valued=False):
    if not is_bf16_valued:
        v = v.astype(BF16).astype(F32)
    bits = pltpu.bitcast(v, jnp.uint32)
    return bits[:, D_HALF:] | (bits[:, :D_HALF] >> 16)


def _unpack_bf16_pairs(w):
    lo = pltpu.bitcast(w << 16, F32).astype(BF16)
    hi = pltpu.bitcast(w & jnp.uint32(0xFFFF0000), F32).astype(BF16)
    return lo, hi


N_SLOTS = 3
TILES_PER_STEP = 2


def _dispatch_kernel(fill_ref, *refs):
    cd_refs = refs[:TILES_PER_STEP + 1]
    h_ref, lp_ref, xg_ref, sorted_s, sems, fill_sem = refs[TILES_PER_STEP + 1:]
    step = pl.program_id(0)
    spill0 = xg_ref.shape[0] - N_SLOTS * RL

    def start(cref, s, c, to_spill, sem):
        src = RUN * c if isinstance(c, int) else pl.multiple_of(RUN * c, RUN)
        d = cref[0, 0, c]
        d = pl.multiple_of(jnp.where(jnp.logical_or(d < 0, to_spill), spill0 + s * RL + src, d), RUN)
        pltpu.make_async_copy(sorted_s.at[s, pl.ds(src, RUN), :], xg_ref.at[pl.ds(d, RUN), :], sem).start()

    def drain(s, sem):
        pltpu.make_async_copy(sorted_s.at[s], xg_ref.at[pl.ds(0, RL), :], sem).wait()

    @pl.when(step == 0)
    def _():
        sorted_s[...] = jnp.zeros_like(sorted_s)
        for s in range(N_SLOTS):
            lax.fori_loop(0, RL // RUN, lambda c, carry, s=s: (start(cd_refs[0], s, c, True, fill_sem), carry)[1], 0)
        for s in range(N_SLOTS):
            drain(s, fill_sem)

    for k in range(TILES_PER_STEP):
        t = TILES_PER_STEP * step + k
        rows = slice(k * TB, (k + 1) * TB)
        slot = lax.rem(t, N_SLOTS)
        prev = lax.rem(t + N_SLOTS - 1, N_SLOTS)
        pprev = lax.rem(t + N_SLOTS - 2, N_SLOTS)

        for c in range(RL // RUN):
            start(cd_refs[k], prev, c, t == 0, sems.at[prev])

        sorted_s[slot] = _pack_bf16_pairs(
            _dot_tn(_pair_matrix(lp_ref[rows, :], None).astype(BF16), h_ref[rows, :]), is_bf16_valued=True)

        if k == 0:
            @pl.when(step == 0)
            def _():
                def fill(j):
                    blk = pl.multiple_of(fill_ref[j] * MOE_BLOCK, MOE_BLOCK)
                    return pltpu.make_async_copy(sorted_s.at[0, pl.ds(0, MOE_BLOCK), :],
                                                 xg_ref.at[pl.ds(blk, MOE_BLOCK), :], fill_sem)

                for j in range(fill_ref.shape[0]):
                    pl.when(fill_ref[j] >= 0)(lambda j=j: fill(j).start())
                for j in range(fill_ref.shape[0]):
                    pl.when(fill_ref[j] >= 0)(lambda j=j: fill(j).wait())

        pl.when(t >= 1)(lambda pprev=pprev: drain(pprev, sems.at[pprev]))

    @pl.when(step == pl.num_programs(0) - 1)
    def _():
        lax.fori_loop(0, RL // RUN,
                      lambda c, carry: (start(cd_refs[TILES_PER_STEP], slot, c, False, sems.at[slot]), carry)[1], 0,
                      unroll=4)
        drain(prev, sems.at[prev])
        drain(slot, sems.at[slot])


def _dispatch_call(fill_blocks, cd, h2, lp, P):
    T, d = h2.shape
    tps = TILES_PER_STEP
    assert (T // TB) % tps == 0
    cd_spec = lambda k: pl.BlockSpec((1, 1, CD_LANES), lambda i, fb: (jnp.maximum(tps * i + k - 1, 0), 0, 0),
                                     memory_space=pltpu.SMEM)
    grid_spec = pltpu.PrefetchScalarGridSpec(
        num_scalar_prefetch=1,
        grid=(T // (tps * TB),),
        in_specs=[cd_spec(k) for k in range(tps + 1)] +
                 [pl.BlockSpec((tps * TB, d), lambda i, fb: (i, 0)),
                  pl.BlockSpec((tps * TB, LANES), lambda i, fb: (i, 0))],
        out_specs=pl.BlockSpec(memory_space=pl.ANY),
        scratch_shapes=[pltpu.VMEM((N_SLOTS, RL, D_HALF), jnp.uint32), pltpu.SemaphoreType.DMA((N_SLOTS,)),
                        pltpu.SemaphoreType.DMA(())],
    )
    return pl.pallas_call(
        _dispatch_kernel,
        grid_spec=grid_spec,
        out_shape=jax.ShapeDtypeStruct((P + N_SLOTS * RL, D_HALF), jnp.uint32),
        compiler_params=_params("arbitrary"),
    )(fill_blocks, *([cd] * (tps + 1)), h2, lp)


def _moe_kernel(nb, info_ref, x_hbm, wgu_hbm, bgu_ref, wd_hbm, bd_ref, y_hbm,
                wgu_f, wd_f, wgu_b, wd_b, act_s, xbuf, ybuf, zbuf, wsems, xsems, ysems, zsem):
    n_used = info_ref[0, nb]
    n_pairs = (n_used + 1) // 2

    def rows(blk):
        return pl.ds(pl.multiple_of(blk * MOE_BLOCK, MOE_BLOCK), MOE_BLOCK)

    def pair_rows(pair):
        return pl.ds(pl.multiple_of(pair * (2 * MOE_BLOCK), 2 * MOE_BLOCK), 2 * MOE_BLOCK)

    def x_copy(pair, slot):
        return pltpu.make_async_copy(x_hbm.at[pair_rows(pair), :], xbuf.at[slot], xsems.at[slot])

    def y_copy(pair, slot):
        return pltpu.make_async_copy(ybuf.at[slot], y_hbm.at[pair_rows(pair), :], ysems.at[slot])

    def z_copy(blk):
        return pltpu.make_async_copy(zbuf, y_hbm.at[rows(blk), :], zsem)

    def fetch(expert, slot):
        return (pltpu.make_async_copy(wgu_hbm.at[expert], wgu_f.at[slot], wsems.at[0, slot]),
                pltpu.make_async_copy(wd_hbm.at[expert], wd_f.at[slot], wsems.at[1, slot]))

    zbuf[...] = jnp.zeros_like(zbuf)
    ybuf[...] = jnp.zeros_like(ybuf)
    lax.fori_loop(2 * n_pairs, nb, lambda blk, c: (z_copy(blk).start(), c)[1], 0)

    for cp in fetch(info_ref[0, 0], 0):
        cp.start()
    x_copy(0, 0).start()

    def body(i, carry):
        pair = i >> 1
        half = i & 1
        slot = pair & 1
        blk_rows = pl.ds(pl.multiple_of(half * MOE_BLOCK, MOE_BLOCK), MOE_BLOCK)
        e = info_ref[0, i]

        @pl.when(info_ref[1, i] == 1)
        def _():
            wslot = info_ref[2, i]
            nxt = info_ref[3, i]
            cps = fetch(e, wslot)
            cps[0].wait()
            wgu_b[...] = wgu_f[wslot].astype(BF16)
            cps[1].wait()
            wd_b[...] = wd_f[wslot].astype(BF16)

            @pl.when(nxt >= 0)
            def _():
                for cp in fetch(nxt, 1 - wslot):
                    cp.start()

        @pl.when(half == 0)
        def _():
            x_copy(pair, slot).wait()
            pl.when(pair + 1 < n_pairs)(lambda: x_copy(pair + 1, 1 - slot).start())
            pl.when(pair >= 2)(lambda: y_copy(pair - 2, slot).wait())

        bgu = bgu_ref[pl.ds(e, 1), :]
        xb = jnp.concatenate(_unpack_bf16_pairs(xbuf[slot, blk_rows, :]), axis=1)
        cw = 256
        for c in range(D_FF // cw):
            lo, hi = c * cw, (c + 1) * cw
            gate = _dot(xb, wgu_b[:, lo:hi]) + bgu[:, lo:hi]
            up = _dot(xb, wgu_b[:, D_FF + lo:D_FF + hi]) + bgu[:, D_FF + lo:D_FF + hi]
            gate = jnp.minimum(gate, SWIGLU_LIMIT)
            up = jnp.clip(up, -SWIGLU_LIMIT, SWIGLU_LIMIT)
            act_s[:, lo:hi] = ((up + 1.0) * (gate * _sigmoid(SWIGLU_ALPHA * gate))).astype(BF16)
        ybuf[slot, blk_rows, :] = _pack_bf16_pairs(_dot(act_s[...], wd_b[...]) + bd_ref[pl.ds(e, 1), :])
        pl.when(jnp.logical_or(half == 1, i == n_used - 1))(lambda: y_copy(pair, slot).start())
        return carry

    lax.fori_loop(0, n_used, body, 0)

    pl.when(n_pairs >= 2)(lambda: y_copy(n_pairs - 2, n_pairs & 1).wait())
    y_copy(n_pairs - 1, (n_pairs - 1) & 1).wait()
    lax.fori_loop(2 * n_pairs, nb, lambda blk, c: (z_copy(blk).wait(), c)[1], 0)


def _moe_call(info, xg, wgu, bgu, wd, bd, nb):
    P = nb * MOE_BLOCK
    ne, d, f2 = wgu.shape
    assert nb % 2 == 0
    blk_buf = pltpu.VMEM((2, 2 * MOE_BLOCK, D_HALF), jnp.uint32)
    grid_spec = pltpu.PrefetchScalarGridSpec(
        num_scalar_prefetch=1,
        grid=(1,),
        in_specs=[pl.BlockSpec(memory_space=pl.ANY),
                  pl.BlockSpec(memory_space=pl.ANY),
                  pl.BlockSpec((ne, f2), lambda i, info: (0, 0)),
                  pl.BlockSpec(memory_space=pl.ANY),
                  pl.BlockSpec((ne, d), lambda i, info: (0, 0))],
        out_specs=pl.BlockSpec(memory_space=pl.ANY),
        scratch_shapes=[pltpu.VMEM((2, d, f2), F32), pltpu.VMEM((2, f2 // 2, d), F32),
                        pltpu.VMEM((d, f2), BF16), pltpu.VMEM((f2 // 2, d), BF16),
                        pltpu.VMEM((MOE_BLOCK, f2 // 2), BF16), blk_buf, blk_buf,
                        pltpu.VMEM((MOE_BLOCK, D_HALF), jnp.uint32),
                        pltpu.SemaphoreType.DMA((2, 2)), pltpu.SemaphoreType.DMA((2,)),
                        pltpu.SemaphoreType.DMA((2,)), pltpu.SemaphoreType.DMA(())],
    )
    return pl.pallas_call(
        functools.partial(_moe_kernel, nb),
        grid_spec=grid_spec,
        out_shape=jax.ShapeDtypeStruct((P, D_HALF), jnp.uint32),
        compiler_params=_params("arbitrary"),
    )(info, xg, wgu, bgu, wd, bd)


def _combine_kernel(*refs):
    cd_refs = refs[:TILES_PER_STEP + 2]
    y_ref, lp_ref, x1_ref, gate_ref, g2_ref, fn_ref, o_ref, buf, sems = refs[TILES_PER_STEP + 2:]
    step = pl.program_id(0)

    def start(dref, s, c):
        d = pl.multiple_of(jnp.maximum(dref[0, 0, c], 0), RUN)
        dst = pl.multiple_of(RUN * c, RUN)
        pltpu.make_async_copy(y_ref.at[pl.ds(d, RUN), :], buf.at[s, pl.ds(dst, RUN), :], sems.at[s]).start()

    def drain(s):
        pltpu.make_async_copy(y_ref.at[pl.ds(0, RL), :], buf.at[s], sems.at[s]).wait()

    @pl.when(step == 0)
    def _():
        lax.fori_loop(0, RL // RUN, lambda c, carry: (start(cd_refs[0], 0, c), carry)[1], 0, unroll=4)
        lax.fori_loop(0, RL // RUN, lambda c, carry: (start(cd_refs[1], 1, c), carry)[1], 0, unroll=4)

    for k in range(TILES_PER_STEP):
        t = TILES_PER_STEP * step + k
        rows = slice(k * TB, (k + 1) * TB)
        slot = lax.rem(t, N_SLOTS)
        ahead = lax.rem(t + 2, N_SLOTS)
        drain(slot)

        for c in range(RL // RUN):
            start(cd_refs[k + 2], ahead, c)

        g = _pair_matrix(lp_ref[rows, :], gate_ref[rows, :]).astype(BF16)
        halves = [_dot(g, yb) for yb in _unpack_bf16_pairs(buf[slot])]
        xo = x1_ref[rows, :] + g2_ref[0] * jnp.concatenate(halves, axis=1)
        ms = jnp.mean(xo * xo, axis=-1, keepdims=True)
        o_ref[rows, :] = xo * lax.rsqrt(ms + EPS) * fn_ref[...]

    @pl.when(step == pl.num_programs(0) - 1)
    def _():
        drain(lax.rem(t + 1, N_SLOTS))
        drain(ahead)


def _combine_call(cd, y, lp, x1, gates, g2, fn):
    T, d = x1.shape
    tps = TILES_PER_STEP
    n = T // TB
    assert n % tps == 0 and (n // g2.shape[0]) % tps == 0
    steps_per_batch = n // g2.shape[0] // tps
    cd_spec = lambda k: pl.BlockSpec((1, 1, CD_LANES), lambda i: (jnp.minimum(tps * i + k, n - 1), 0, 0),
                                     memory_space=pltpu.SMEM)
    return pl.pallas_call(
        _combine_kernel,
        grid=(n // tps,),
        in_specs=[cd_spec(k) for k in range(tps + 2)] +
                 [pl.BlockSpec(memory_space=pl.ANY),
                  pl.BlockSpec((tps * TB, LANES), lambda i: (i, 0)),
                  pl.BlockSpec((tps * TB, d), lambda i: (i, 0)),
                  pl.BlockSpec((tps * TB, LANES), lambda i: (i, 0)),
                  pl.BlockSpec((1, 1, d), lambda i: (i // steps_per_batch, 0, 0)),
                  pl.BlockSpec((1, d), lambda i: (0, 0))],
        out_specs=pl.BlockSpec((tps * TB, d), lambda i: (i, 0)),
        out_shape=jax.ShapeDtypeStruct((T, d), F32),
        scratch_shapes=[pltpu.VMEM((N_SLOTS, RL, D_HALF), jnp.uint32), pltpu.SemaphoreType.DMA((N_SLOTS,))],
        compiler_params=_params("arbitrary"),
    )(*([cd] * (tps + 2)), y, lp, x1, gates, g2, fn)


def _layer(x, c, ctx, c_ctx, w_mod, b_mod, norm1, w_in, gla_w_gk_up, gla_b_gk, gla_norm,
           ssd_conv_w, ssd_conv_b, ssd_dt_bias, ssd_A_log, ssd_D, ssd_norm, w_out,
           norm2, w_router, b_router, w_gate_up, b_gate_up, w_down, b_down, final_norm):
    bsz, L, d = x.shape
    lc = ctx.shape[1]
    assert lc == TB and L % TB == 0 and TB % GRID_W == 0

    cin = jnp.zeros((8, d), F32).at[:bsz].set(c).at[bsz].set(c_ctx)
    mod = _mod_call(cin, w_mod, b_mod.reshape(1, -1))[:bsz + 1]
    sh1, sc1, g1, sh2, sc2, g2 = [m.reshape(bsz + 1, 1, d) for m in jnp.split(mod, 6, axis=-1)]

    o = np.cumsum((0, GLA_QK, GLA_QK, GLA_V, GLA_V, GLA_RANK, SSD_INNER, SSD_CONV_DIM, SSD_HEADS))
    wq, wk, wv, wg, wlow, wz, wx, wdt = [w_in[:, int(a):int(b)] for a, b in zip(o[:-1], o[1:])]
    w_misc = jnp.concatenate([wlow, wdt, wdt, jnp.zeros((d, LANES - GLA_RANK - 2 * SSD_HEADS), F32)], axis=1)
    w_cat = jnp.concatenate([wq, wk, wv, wg, wz, wx, w_misc], axis=1).astype(BF16)
    wup = jnp.zeros((LANES, 2 * GLA_QK), F32).at[:GLA_RANK].set(
        jnp.concatenate([gla_w_gk_up[0], gla_w_gk_up[1]], axis=1)).astype(BF16)
    bup = jnp.concatenate([gla_b_gk[0], gla_b_gk[1]]).reshape(1, -1)
    dtb = jnp.zeros((1, LANES), F32).at[0, DT_F:DT_F + SSD_HEADS].set(ssd_dt_bias[0]) \
                                    .at[0, DT_B:DT_B + SSD_HEADS].set(ssd_dt_bias[1])
    q, k, v, g_all, z_all, xbc, ld, misc = _inproj_call(
        x, ctx, sh1, sc1, norm1.reshape(1, d), w_cat, wup, bup, dtb)

    xs, bc = _conv_call(xbc, ssd_conv_w.reshape(9, SSD_CONV_DIM), ssd_conv_b.reshape(1, -1), lc + L)

    a_neg = -jnp.exp(ssd_A_log.astype(F32))
    a_f = jnp.zeros((1, LANES), F32).at[0, DT_F:DT_F + SSD_HEADS].set(a_neg[0])
    a_b = jnp.zeros((1, LANES), F32).at[0, DT_B:DT_B + SSD_HEADS].set(a_neg[1])
    dvec = jnp.repeat(ssd_D, SSD_HEADDIM).reshape(1, SSD_INNER)
    o_f, o_b, y_f, y_b = _scan_call(q, k, v, ld, xs, bc, misc, a_f, a_b, dvec, L)

    wr = jnp.zeros((d, LANES), F32).at[:, :N_EXPERTS].set(w_router)
    wr_hi = wr.astype(BF16)
    wr_lo = (wr - wr_hi.astype(F32)).astype(BF16)
    br = jnp.full((1, LANES), NEG_BIG, F32).at[0, :N_EXPERTS].set(b_router)
    x1, h2, eidx, gates, counts = _outproj_call(
        o_f, o_b, g_all, y_f, y_b, z_all, x, g1[:bsz], sh2[:bsz], sc2[:bsz],
        jnp.tile(gla_norm, GLA_HEADS).reshape(1, -1), ssd_norm.reshape(1, -1), w_out.astype(BF16),
        norm2.reshape(1, d), wr_hi, wr_lo, br)

    T = bsz * L
    cnt = counts[0, :N_EXPERTS].astype(jnp.int32)
    padded = ((cnt + MOE_BLOCK - 1) // MOE_BLOCK) * MOE_BLOCK
    pend = jnp.cumsum(padded)
    pstart = pend - padded
    max_rows = T * TOP_K + (T // TB) * N_EXPERTS * (RUN - 1)
    n_blocks = -(-max_rows // MOE_BLOCK) + N_EXPERTS
    n_blocks += n_blocks % 2
    blk_start = jnp.arange(n_blocks, dtype=jnp.int32) * MOE_BLOCK
    blk_e = jnp.minimum(jnp.sum(pend[None, :] <= blk_start[:, None], axis=1), N_EXPERTS - 1).astype(jnp.int32)
    n_used = (pend[-1] // MOE_BLOCK).astype(jnp.int32)
    blk_i = jnp.arange(n_blocks, dtype=jnp.int32)
    first = (blk_i < n_used) & ((blk_i == 0) | (blk_e != jnp.roll(blk_e, 1)))
    slot = (jnp.cumsum(first) - 1) & 1
    first_pos = jnp.where(first, blk_i, n_blocks)
    next_first = jnp.roll(lax.cummin(first_pos, reverse=True), -1).at[-1].set(n_blocks)
    nxt = jnp.where(next_first < n_blocks, blk_e[jnp.minimum(next_first, n_blocks - 1)], -1)
    col = lambda v, last: jnp.concatenate([v.astype(jnp.int32), jnp.asarray([last], jnp.int32)])
    blk_info = jnp.stack([col(blk_e, 0).at[-1].set(n_used), col(first, 0), col(slot, 0), col(nxt, -1)])
    n_tail = n_blocks - (T * TOP_K) // MOE_BLOCK
    tail = n_used + jnp.arange(n_tail, dtype=jnp.int32)
    fill_blocks = jnp.concatenate([
        jnp.where(padded > 0, pend // MOE_BLOCK - 1, -1),
        jnp.where(tail < n_blocks, tail, -1)]).astype(jnp.int32)
    pstart_row = jnp.zeros((1, LANES), F32).at[0, :N_EXPERTS].set(pstart.astype(F32))
    lp, cd = _pos_call(eidx.reshape(T, LANES), pstart_row)

    xg = _dispatch_call(fill_blocks, cd, h2.reshape(T, d), lp, n_blocks * MOE_BLOCK)
    y = _moe_call(blk_info, xg, w_gate_up, b_gate_up, w_down, b_down, n_blocks)
    out = _combine_call(cd, y, lp, x1.reshape(T, d), gates.reshape(T, LANES), g2[:bsz], final_norm.reshape(1, d))
    return out.reshape(bsz, L, d)


def kernel(x, c, ctx, c_ctx, w_mod, b_mod, norm1, w_in, gla_w_gk_up, gla_b_gk, gla_norm, ssd_conv_w, ssd_conv_b, ssd_dt_bias, ssd_A_log, ssd_D, ssd_norm, w_out, norm2, w_router, b_router, w_gate_up, b_gate_up, w_down, b_down, final_norm):
    assert w_mod.shape[0] == 1, "single-layer kernel"
    return _layer(x, c, ctx, c_ctx, w_mod[0], b_mod[0], norm1[0], w_in[0], gla_w_gk_up[0], gla_b_gk[0],
                  gla_norm[0], ssd_conv_w[0], ssd_conv_b[0], ssd_dt_bias[0], ssd_A_log[0], ssd_D[0],
                  ssd_norm[0], w_out[0], norm2[0], w_router[0], b_router[0], w_gate_up[0], b_gate_up[0],
                  w_down[0], b_down[0], final_norm)
```

```python
import functools

import numpy as np
import jax
import jax.numpy as jnp
from jax import lax
from jax.experimental import pallas as pl
from jax.experimental.pallas import tpu as pltpu

F32 = jnp.float32
BF16 = jnp.bfloat16

EPS = 1e-6
GRID_W = 64
GLA_HEADS = 4
GLA_DK = 64
GLA_DV = 128
GLA_QK = GLA_HEADS * GLA_DK
GLA_V = GLA_HEADS * GLA_DV
GLA_RANK = 16
GLA_GATE_NORM = 16.0
SSD_HEADDIM = 64
SSD_INNER = 512
SSD_HEADS = 8
SSD_GROUPS = 2
SSD_HPG = 4
SSD_STATE = 128
SSD_CONV_DIM = 1024
N_EXPERTS = 32
TOP_K = 4
D_FF = 1024
SWIGLU_LIMIT = 7.0
SWIGLU_ALPHA = 1.702
MOE_BLOCK = 1024

TB = 256
GLA_C = 64
GLA_STAGE_GROUP = 1
SSD_C = 128
SSD_STAGE_GROUP = 1
LANES = 128
EXP_CLAMP = 80.0
DT_F = 16
DT_B = 24
NEG_BIG = -1e30
RUN = 8
RL = -(-(TB * TOP_K + N_EXPERTS * (RUN - 1)) // LANES) * LANES
CD_LANES = -(-(RL // RUN) // LANES) * LANES
D_HALF = 512
VMEM_LIMIT = 56 * 1024 * 1024


def _dot(a, b):
    return jnp.dot(a, b, preferred_element_type=F32)


def _dot_nt(a, b):
    return lax.dot_general(a, b, (((1,), (1,)), ((), ())), preferred_element_type=F32)


def _dot_tn(a, b):
    return lax.dot_general(a, b, (((0,), (0,)), ((), ())), preferred_element_type=F32)


def _split3(a):
    hi = a.astype(BF16)
    r1 = a - hi.astype(F32)
    mid = r1.astype(BF16)
    lo = (r1 - mid.astype(F32)).astype(BF16)
    return hi, mid, lo


def _dot_exact_r(m, a):
    hi, mid, lo = _split3(a)
    return _dot(m, hi) + _dot(m, mid) + _dot(m, lo)


def _dot_hilo_r(m, a):
    hi = a.astype(BF16)
    lo = (a - hi.astype(F32)).astype(BF16)
    return _dot(m, hi) + _dot(m, lo)


def _dot_hilo_l(a, m2):
    hi = a.astype(BF16)
    lo = (a - hi.astype(F32)).astype(BF16)
    return _dot(jnp.concatenate([hi, lo], axis=1), m2)


def _sigmoid(x):
    return 1.0 / (1.0 + jnp.exp(-x))


def _softplus(x):
    return jnp.maximum(x, 0.0) + jnp.log1p(jnp.exp(-jnp.abs(x)))


def _params(*sem):
    return pltpu.CompilerParams(dimension_semantics=sem, vmem_limit_bytes=VMEM_LIMIT)


def _mod_kernel(c_ref, w_ref, b_ref, o_ref):
    c = c_ref[...]
    s = c * _sigmoid(c)
    s_hi = s.astype(BF16)
    s_lo = (s - s_hi.astype(F32)).astype(BF16)
    w = w_ref[...]
    w_hi = w.astype(BF16)
    w_lo = (w - w_hi.astype(F32)).astype(BF16)
    o_ref[...] = _dot(s_hi, w_hi) + _dot(s_lo, w_hi) + _dot(s_hi, w_lo) + b_ref[...]


def _mod_call(cin, w, b):
    rows, d = cin.shape
    n = w.shape[1]
    tn = 1536
    return pl.pallas_call(
        _mod_kernel,
        grid=(n // tn,),
        in_specs=[pl.BlockSpec((rows, d), lambda i: (0, 0)),
                  pl.BlockSpec((d, tn), lambda i: (0, i)),
                  pl.BlockSpec((1, tn), lambda i: (0, i))],
        out_specs=pl.BlockSpec((rows, tn), lambda i: (0, i)),
        out_shape=jax.ShapeDtypeStruct((rows, n), F32),
        compiler_params=_params("arbitrary"),
    )(cin, w, b)


_C_Q, _C_K, _C_V, _C_G, _C_Z, _C_X, _C_M, _C_END = 0, 256, 512, 1024, 1536, 2048, 3072, 3200


def _inproj_kernel(x0_ref, xa_ref, xb_ref, ctx_ref, shl_ref, scl_ref, shc_ref, scc_ref, n1_ref, w_ref, wup_ref,
                   bup_ref, dtb_ref, qkv_ref, g_ref, z_ref, xbc_ref, ldm_ref, h_scr):
    j = pl.program_id(1)
    slot = j & 1

    def normmod(xv, sh_ref, sc_ref):
        ms = jnp.mean(xv * xv, axis=-1, keepdims=True)
        y = xv * lax.rsqrt(ms + EPS) * n1_ref[...]
        return (y * (1.0 + sc_ref[0]) + sh_ref[0]).astype(BF16)

    @pl.when(j == 0)
    def _():
        h_scr[0, 0:TB, :] = normmod(ctx_ref[0], shc_ref, scc_ref)
        h_scr[0, TB:2 * TB, :] = normmod(x0_ref[0], shl_ref, scl_ref)

    def mm(lo, hi):
        return _dot(h_scr[slot], w_ref[:, lo:hi])

    m = mm(_C_M, _C_END)
    zz = _dot(m.astype(BF16), wup_ref[...]) + bup_ref[...]
    ldm_ref[0, :, 0:2 * GLA_QK] = -_softplus(-zz) * (1.0 / GLA_GATE_NORM)
    ldm_ref[0, :, 2 * GLA_QK:2 * GLA_QK + LANES] = _softplus(m + dtb_ref[...])
    qkv_ref[0, :, _C_Q:_C_K] = (mm(_C_Q, _C_K) * (GLA_DK ** -0.5)).astype(BF16)
    h_scr[1 - slot, 0:TB, :] = normmod(xa_ref[0], shl_ref, scl_ref)
    qkv_ref[0, :, _C_K:_C_V] = mm(_C_K, _C_V).astype(BF16)
    h_scr[1 - slot, TB:2 * TB, :] = normmod(xb_ref[0], shl_ref, scl_ref)
    qkv_ref[0, :, _C_V:_C_G] = mm(_C_V, _C_G).astype(BF16)
    g_ref[0] = mm(_C_G, _C_Z)
    z_ref[0] = mm(_C_Z, _C_X)
    xbc_ref[0] = mm(_C_X, _C_M)


def _inproj_call(x, ctx, sh1, sc1, n1, w_cat, wup, bup, dtb):
    bsz, L, d = x.shape
    nx = L // TB
    nj = -(-(nx + 1) // 2)
    tok = lambda n: pl.BlockSpec((1, 2 * TB, n), lambda b, j: (b, j, 0))
    const = lambda a: pl.BlockSpec(a.shape, lambda b, j: (0,) * a.ndim)
    xblk = lambda off: pl.BlockSpec((1, TB, d), lambda b, j: (b, jnp.minimum(2 * j + off, nx - 1), 0))
    first = pl.BlockSpec((1, TB, d), lambda b, j: (b, 0, 0))
    mod_lat = pl.BlockSpec((1, 1, d), lambda b, j: (b, 0, 0))
    mod_ctx = pl.BlockSpec((1, 1, d), lambda b, j: (bsz, 0, 0))
    outs = [(2 * GLA_QK + GLA_V, BF16), (GLA_V, F32), (SSD_INNER, F32), (SSD_CONV_DIM, F32),
            (2 * GLA_QK + LANES, F32)]
    return pl.pallas_call(
        _inproj_kernel,
        grid=(bsz, nj),
        in_specs=[first, xblk(1), xblk(2), first,
                  mod_lat, mod_lat, mod_ctx, mod_ctx,
                  const(n1), const(w_cat), const(wup), const(bup), const(dtb)],
        out_specs=[tok(n) for n, _ in outs],
        out_shape=[jax.ShapeDtypeStruct((bsz, nj * 2 * TB, n), dt) for n, dt in outs],
        scratch_shapes=[pltpu.VMEM((2, 2 * TB, d), BF16)],
        compiler_params=_params("arbitrary", "arbitrary"),
    )(x, x, x, ctx, sh1, sc1, sh1, sc1, n1, w_cat, wup, bup, dtb)


_EXT_PAD = 8
_EXT_BASE = _EXT_PAD + GRID_W
_EXT_ROWS = 2 * _EXT_PAD + 2 * GRID_W + TB


def _conv_kernel(prev_ref, cur_ref, next_ref, w_ref, b_ref, xs_ref, bc_ref, ext_s, xl_all, xr_all):
    for bb in range(cur_ref.shape[0]):
        _conv_block(prev_ref.at[bb], cur_ref.at[bb], next_ref.at[bb], w_ref, b_ref, xs_ref.at[bb], bc_ref.at[bb],
                    ext_s.at[bb], xl_all.at[bb], xr_all.at[bb])


def _conv_block(prev_ref, cur_ref, next_ref, w_ref, b_ref, xs_ref, bc_ref, ext, xl_s, xr_s):
    j = pl.program_id(1)
    nj = pl.num_programs(1)
    is_ctx = j == 0
    zpad = jnp.zeros((_EXT_PAD, SSD_CONV_DIM), F32)
    ext[0:_EXT_PAD, :] = zpad
    ext[_EXT_ROWS - _EXT_PAD:_EXT_ROWS, :] = zpad
    ext[_EXT_PAD:_EXT_BASE, :] = jnp.where(j >= 2, prev_ref[...], 0.0)
    ext[_EXT_BASE:_EXT_BASE + TB, :] = cur_ref[...]
    ext[_EXT_BASE + TB:_EXT_BASE + TB + GRID_W, :] = jnp.where(
        jnp.logical_and(j >= 1, j <= nj - 2), next_ref[...], 0.0)

    win = TB + 2 * GRID_W
    u = lax.broadcasted_iota(jnp.int32, (win, LANES), 0)
    pos = jnp.where(is_ctx, u - GRID_W, u & (GRID_W - 1))
    ok_l = pos >= 1
    ok_r = jnp.where(is_ctx, pos - (TB - GRID_W), pos) <= GRID_W - 2
    lat = jnp.where(is_ctx, 0.0, 1.0)
    side = {-1: xl_s, 0: None, 1: xr_s}

    for c in range(SSD_CONV_DIM // LANES):
        lo, hi = c * LANES, (c + 1) * LANES
        xl_s[:, lo:hi] = jnp.where(ok_l, ext[_EXT_PAD - 1:_EXT_PAD - 1 + win, lo:hi], 0.0)
        xr_s[:, lo:hi] = jnp.where(ok_r, ext[_EXT_PAD + 1:_EXT_PAD + 1 + win, lo:hi], 0.0)
        acc = jnp.zeros((TB, LANES), F32)
        for dr in (-1, 0, 1):
            for dc in (-1, 0, 1):
                if dc == 0:
                    start = _EXT_BASE + GRID_W * dr
                    tap = ext[start:start + TB, lo:hi]
                else:
                    start = GRID_W + GRID_W * dr
                    tap = side[dc][start:start + TB, lo:hi]
                wi = 3 * (dr + 1) + (dc + 1)
                wv = w_ref[wi:wi + 1, lo:hi]
                if dr != 0:
                    wv = wv * lat
                acc = acc + tap * wv
        y = acc + b_ref[:, lo:hi]
        y = y * _sigmoid(y)
        if c < SSD_INNER // LANES:
            xs_ref[:, lo:hi] = y
        else:
            bc_ref[:, lo - SSD_INNER:hi - SSD_INNER] = y.astype(BF16)


def _conv_call(xbc, w9, bias, ls):
    bsz, _, ch = xbc.shape
    nj = ls // TB
    rpb = TB // GRID_W
    nrow = ls // GRID_W
    nb = next(n for n in (4, 2, 1) if bsz % n == 0)
    return pl.pallas_call(
        _conv_kernel,
        grid=(bsz // nb, nj),
        in_specs=[pl.BlockSpec((nb, GRID_W, ch), lambda b, j: (b, jnp.maximum(rpb * j - 1, 0), 0)),
                  pl.BlockSpec((nb, TB, ch), lambda b, j: (b, j, 0)),
                  pl.BlockSpec((nb, GRID_W, ch), lambda b, j: (b, jnp.minimum(rpb * j + rpb, nrow - 1), 0)),
                  pl.BlockSpec((9, ch), lambda b, j: (0, 0)),
                  pl.BlockSpec((1, ch), lambda b, j: (0, 0))],
        out_specs=[pl.BlockSpec((nb, TB, SSD_INNER), lambda b, j: (b, j, 0)),
                   pl.BlockSpec((nb, TB, ch - SSD_INNER), lambda b, j: (b, j, 0))],
        out_shape=[jax.ShapeDtypeStruct((bsz, ls, SSD_INNER), F32),
                   jax.ShapeDtypeStruct((bsz, ls, ch - SSD_INNER), BF16)],
        scratch_shapes=[pltpu.VMEM((nb, _EXT_ROWS, ch), F32), pltpu.VMEM((nb, TB + 2 * GRID_W, ch), F32),
                        pltpu.VMEM((nb, TB + 2 * GRID_W, ch), F32)],
        compiler_params=_params("arbitrary", "arbitrary"),
    )(xbc, xbc, xbc, w9, bias)


def _fwd_blk(s):
    return s


def _bwd_blk(s, ns):
    return jnp.where(s == 0, 0, ns - s)


def _scan_batch(bsz):
    return 2 if bsz % 2 == 0 else 1


class _GlaChunk:
    def __init__(self, q, k, v, la, st_ref, tri_m, fwd, store):
        self.q, self.k, self.v, self.la, self.st_ref, self.tri_m, self.fwd, self.store = (
            q, k, v, la, st_ref, tri_m, fwd, store)

    def stage_sums(self):
        self.b = _dot_hilo_r(self.tri_m, self.la)

    def stage_factors(self):
        C = GLA_C
        b = self.b
        self.bt = b[C - 1:C, :] if self.fwd else b[0:1, :]
        r = 0.5 * self.bt
        self.er = jnp.exp(r)
        self.qt = (self.q.astype(F32) * jnp.exp(jnp.minimum(b - r, EXP_CLAMP))).astype(BF16)
        kt = (self.k.astype(F32) * jnp.exp(jnp.minimum(r - b, EXP_CLAMP))).astype(BF16)
        head_k = lax.broadcasted_iota(jnp.int32, (C, GLA_QK), 1) >> 6
        zero = jnp.zeros_like(kt)
        self.kh = [jnp.where(head_k == h, kt, zero) for h in range(GLA_HEADS)]
        self.qh = [jnp.where(head_k == h, self.qt, zero) for h in range(GLA_HEADS)]

    def stage_products(self):
        v = self.v
        kcat = jnp.concatenate(self.kh, axis=0)
        self.sc = _dot_nt(self.qt, kcat)
        vcat = jnp.concatenate([v[:, h * GLA_DV:(h + 1) * GLA_DV] for h in range(GLA_HEADS)], axis=0)
        self.u = _dot_tn(vcat, kcat) * self.er

    def stage_mask(self):
        C = GLA_C
        ii = lax.broadcasted_iota(jnp.int32, (C, GLA_HEADS * C), 0)
        jj = lax.broadcasted_iota(jnp.int32, (C, GLA_HEADS * C), 1) & (C - 1)
        causal = (jj <= ii) if self.fwd else (jj >= ii)
        self.p = jnp.where(causal, self.sc, 0.0).astype(BF16)
        v = self.v
        head_v = lax.broadcasted_iota(jnp.int32, (C, GLA_V), 1) >> 7
        self.vst = jnp.concatenate([jnp.where(head_v == h, v, jnp.zeros_like(v)) for h in range(GLA_HEADS)], axis=0)

    def stage_intra(self):
        self.o = _dot(self.p, self.vst)

    def stage_inter(self):
        st = self.st_ref[...]
        ster = (st * self.er).astype(BF16)
        self.st_ref[...] = st * jnp.exp(self.bt) + self.u
        res = _dot_nt(jnp.concatenate(self.qh, axis=0), ster)
        self.inter = jnp.concatenate([res[h * GLA_C:(h + 1) * GLA_C, :] for h in range(GLA_HEADS)], axis=1)

    def stage_out(self):
        self.store(self.o + self.inter)


def _store_to(ref, bb, sl):
    def store(val):
        ref[bb, sl, :] = val
    return store


def _gla_schedule(qkvf_ref, ldmf_ref, qkvb_ref, ldmb_ref, trif_ref, trib_ref, of_ref, ob_ref, stf, stb):
    kq, kk, kv = slice(_C_Q, _C_K), slice(_C_K, _C_V), slice(_C_V, _C_G)
    lf, lb = slice(0, GLA_QK), slice(GLA_QK, 2 * GLA_QK)
    nsub = TB // GLA_C
    store_to = _store_to
    calls = []

    group = GLA_STAGE_GROUP
    for g0 in range(0, nsub, group):
        steps = []
        for i in range(g0, g0 + group):
            sf = pl.ds(i * GLA_C, GLA_C)
            sb = pl.ds((nsub - 1 - i) * GLA_C, GLA_C)
            chunks = []
            for bb in range(qkvf_ref.shape[0]):
                chunks.append(_GlaChunk(qkvf_ref[bb, sf, kq], qkvf_ref[bb, sf, kk], qkvf_ref[bb, sf, kv],
                                        ldmf_ref[bb, sf, lf], stf.at[bb], trif_ref[...], True,
                                        store_to(of_ref, bb, sf)))
                chunks.append(_GlaChunk(qkvb_ref[bb, sb, kq], qkvb_ref[bb, sb, kk], qkvb_ref[bb, sb, kv],
                                        ldmb_ref[bb, sb, lb], stb.at[bb], trib_ref[...], False,
                                        store_to(ob_ref, bb, sb)))
            steps.append(chunks)
        for stage in ("stage_sums", "stage_factors", "stage_products", "stage_mask", "stage_intra"):
            for chunks in steps:
                calls += [getattr(ch, stage) for ch in chunks]
        for chunks in steps:
            for stage in ("stage_inter", "stage_out"):
                calls += [getattr(ch, stage) for ch in chunks]
    return calls


class _SsdChunk:
    def __init__(self, xs, bc, dtm, avec, dvec, st_ref, tri_m, e_m, base, fwd, store):
        self.xs, self.bc, self.dtm, self.avec, self.dvec, self.st_ref = xs, bc, dtm, avec, dvec, st_ref
        self.tri_m, self.e_m, self.base, self.fwd, self.store = tri_m, e_m, base, fwd, store

    def stage_sums(self):
        self.dt_exp = _dot_hilo_l(self.dtm, self.e_m)
        self.acum = _dot_exact_r(self.tri_m, self.dtm * self.avec)

    def stage_expand(self):
        self.acum_exp = _dot_hilo_l(self.acum, self.e_m)
        self.acum_t = self.acum.T
        self.xdt = self.xs * self.dt_exp
        bc = self.bc
        self.bg = [bc[:, 128 * g:128 * (g + 1)] for g in range(SSD_GROUPS)]
        self.cg = [bc[:, 256 + 128 * g:256 + 128 * (g + 1)] for g in range(SSD_GROUPS)]
        self.cb = [_dot_nt(self.cg[g], self.bg[g]) for g in range(SSD_GROUPS)]

    def stage_decay(self):
        C = SSD_C
        ii = lax.broadcasted_iota(jnp.int32, (C, C), 0)
        jj = lax.broadcasted_iota(jnp.int32, (C, C), 1)
        tri = (jj <= ii) if self.fwd else (jj >= ii)
        self.ms = []
        for g in range(SSD_GROUPS):
            for rr in range(SSD_HPG):
                ln = self.base + SSD_HPG * g + rr
                diff = self.acum[:, ln:ln + 1] - self.acum_t[ln:ln + 1, :]
                seg = jnp.where(tri, jnp.exp(jnp.minimum(diff, 0.0)), 0.0)
                self.ms.append((self.cb[g] * seg).astype(BF16))
        ae = self.acum_exp
        self.al_exp = ae[C - 1:C, :] if self.fwd else ae[0:1, :]
        self.xw = (self.xdt * jnp.exp(self.al_exp - ae)).astype(BF16)
        xdt_b = self.xdt.astype(BF16)
        gw = SSD_HPG * SSD_HEADDIM
        head = lax.broadcasted_iota(jnp.int32, (C, gw), 1) >> 6
        zero = jnp.zeros((C, gw), BF16)
        self.xh = [jnp.where(head == h % SSD_HPG, xdt_b[:, gw * (h // SSD_HPG):gw * (h // SSD_HPG + 1)], zero)
                   for h in range(SSD_HEADS)]

    def stage_products(self):
        self.yg, self.ug = [], []
        for g in range(SSD_GROUPS):
            gl, gh = 256 * g, 256 * (g + 1)
            yg = _dot(self.ms[SSD_HPG * g], self.xh[SSD_HPG * g])
            for rr in range(1, SSD_HPG):
                yg = yg + _dot(self.ms[SSD_HPG * g + rr], self.xh[SSD_HPG * g + rr])
            self.yg.append(yg)
            self.ug.append(_dot_tn(self.bg[g], self.xw[:, gl:gh]))

    def stage_state(self):
        ys = []
        for g in range(SSD_GROUPS):
            gl, gh = 256 * g, 256 * (g + 1)
            sg = self.st_ref[g]
            yoff = _dot(self.cg[g], sg.astype(BF16)) * jnp.exp(self.acum_exp[:, gl:gh])
            self.st_ref[g] = sg * jnp.exp(self.al_exp[:, gl:gh]) + self.ug[g]
            ys.append(self.yg[g] + yoff)
        y = jnp.concatenate(ys, axis=1)
        if self.dvec is not None:
            y = y + self.dvec * self.xs
        self.store(y)


def _ssd_schedule(xf_ref, bcf_ref, ldmf_ref, xb_ref, bcb_ref, ldmb_ref, af_ref, ab_ref, d_ref,
                  trif_ref, trib_ref, ef_ref, eb_ref, yf_ref, yb_ref, stf, stb):
    dtg = slice(2 * GLA_QK, 2 * GLA_QK + LANES)
    nsub = TB // SSD_C
    store_to = _store_to
    calls = []
    steps = []
    for i in range(nsub):
        sf = pl.ds(i * SSD_C, SSD_C)
        sb = pl.ds((nsub - 1 - i) * SSD_C, SSD_C)
        chunks = []
        for bb in range(xf_ref.shape[0]):
            chunks.append(_SsdChunk(xf_ref[bb, sf, :], bcf_ref[bb, sf, :], ldmf_ref[bb, sf, dtg], af_ref[...],
                                    d_ref[...], stf.at[bb], trif_ref[...], ef_ref[...], DT_F, True,
                                    store_to(yf_ref, bb, sf)))
            chunks.append(_SsdChunk(xb_ref[bb, sb, :], bcb_ref[bb, sb, :], ldmb_ref[bb, sb, dtg], ab_ref[...],
                                    None, stb.at[bb], trib_ref[...], eb_ref[...], DT_B, False,
                                    store_to(yb_ref, bb, sb)))
        steps.append(chunks)
    for g0 in range(0, nsub, SSD_STAGE_GROUP):
        group = steps[g0:g0 + SSD_STAGE_GROUP]
        for stage in ("stage_sums", "stage_expand", "stage_decay", "stage_products"):
            for chunks in group:
                calls += [getattr(ch, stage) for ch in chunks]
        for chunks in group:
            calls += [ch.stage_state for ch in chunks]
    return calls


def _scan_kernel(qkvf_ref, ldmf_ref, xf_ref, bcf_ref, qkvb_ref, ldmb_ref, xb_ref, bcb_ref,
                 gtrif_ref, gtrib_ref, af_ref, ab_ref, d_ref, strif_ref, strib_ref, ef_ref, eb_ref,
                 of_ref, ob_ref, yf_ref, yb_ref, gstf, gstb, sstf, sstb):
    @pl.when(pl.program_id(1) == 0)
    def _():
        for st in (gstf, gstb, sstf, sstb):
            st[...] = jnp.zeros_like(st)

    gla = _gla_schedule(qkvf_ref, ldmf_ref, qkvb_ref, ldmb_ref, gtrif_ref, gtrib_ref, of_ref, ob_ref, gstf, gstb)
    ssd = _ssd_schedule(xf_ref, bcf_ref, ldmf_ref, xb_ref, bcb_ref, ldmb_ref, af_ref, ab_ref, d_ref,
                        strif_ref, strib_ref, ef_ref, eb_ref, yf_ref, yb_ref, sstf, sstb)
    merged = sorted([((n + 0.5) / len(gla), 0, n, c) for n, c in enumerate(gla)] +
                    [((n + 0.5) / len(ssd), 1, n, c) for n, c in enumerate(ssd)], key=lambda t: t[:3])
    for _, _, _, call in merged:
        call()


def _expand_matrix(base):
    e = np.zeros((LANES, SSD_INNER), np.float32)
    for h in range(SSD_HEADS):
        e[base + h, SSD_HEADDIM * h:SSD_HEADDIM * (h + 1)] = 1.0
    return jnp.asarray(np.concatenate([e, e], axis=0), BF16)


def _scan_call(qkv, ldm, xs, bc, a_f, a_b, dvec, L):
    bsz = qkv.shape[0]
    nx = L // TB
    ns = nx + 1
    tri = lambda c, up: jnp.asarray((np.triu if up else np.tril)(np.ones((c, c), np.float32)), BF16)
    g_trif, g_trib, s_trif, s_trib = tri(GLA_C, False), tri(GLA_C, True), tri(SSD_C, False), tri(SSD_C, True)
    ef, eb = _expand_matrix(DT_F), _expand_matrix(DT_B)
    nb = _scan_batch(bsz)
    f = lambda a: pl.BlockSpec((nb, TB, a.shape[2]), lambda b, s: (b, _fwd_blk(s), 0))
    r = lambda a: pl.BlockSpec((nb, TB, a.shape[2]), lambda b, s: (b, _bwd_blk(s, ns), 0))
    const = lambda a: pl.BlockSpec(a.shape, lambda b, s: (0,) * a.ndim)
    out_f = lambda n: pl.BlockSpec((nb, TB, n), lambda b, s: (b, jnp.maximum(s - 1, 0), 0))
    out_b = lambda n: pl.BlockSpec((nb, TB, n), lambda b, s: (b, jnp.where(s == 0, nx - 1, nx - s), 0))
    seq = (qkv, ldm, xs, bc)
    consts = (g_trif, g_trib, a_f, a_b, dvec, s_trif, s_trib, ef, eb)
    return pl.pallas_call(
        _scan_kernel,
        grid=(bsz // nb, ns),
        in_specs=[f(a) for a in seq] + [r(a) for a in seq] + [const(a) for a in consts],
        out_specs=[out_f(GLA_V), out_b(GLA_V), out_f(SSD_INNER), out_b(SSD_INNER)],
        out_shape=[jax.ShapeDtypeStruct((bsz, L, GLA_V), F32)] * 2 +
                  [jax.ShapeDtypeStruct((bsz, L, SSD_INNER), F32)] * 2,
        scratch_shapes=[pltpu.VMEM((nb, GLA_DV, GLA_QK), F32)] * 2 +
                       [pltpu.VMEM((nb, SSD_GROUPS, SSD_STATE, SSD_HPG * SSD_HEADDIM), F32)] * 2,
        compiler_params=_params("arbitrary", "arbitrary"),
    )(*seq, *seq, *consts)


def _outproj_kernel(of_ref, ob_ref, ga_ref, gb_ref, yf_ref, yb_ref, za_ref, zb_ref, x_ref, g1_ref, sh2_ref, sc2_ref,
                    gn_ref, sn_ref, wo_ref, n2_ref, wrh_ref, wrl_ref, br_ref,
                    x1_ref, h2_ref, eidx_ref, gate_ref, cnt_ref, mix_s, hl_s):
    first = jnp.logical_and(pl.program_id(0) == 0, pl.program_id(1) == 0)

    @pl.when(first)
    def _():
        cnt_ref[...] = jnp.zeros_like(cnt_ref)

    halves = [(pl.ds(0, TB), ga_ref, za_ref), (pl.ds(TB, TB), gb_ref, zb_ref)]

    for rows, g_ref, z_ref in halves:
        o = of_ref[0, rows, :] + ob_ref[0, rows, :]
        gg = g_ref[0]
        for h in range(GLA_HEADS):
            lo, hi = GLA_DV * h, GLA_DV * (h + 1)
            oh = o[:, lo:hi]
            ms = jnp.mean(oh * oh, axis=-1, keepdims=True)
            gh = gg[:, lo:hi]
            mix_s[rows, lo:hi] = (oh * lax.rsqrt(ms + EPS) * gn_ref[:, lo:hi] * (gh * _sigmoid(gh))).astype(BF16)
        zz = z_ref[0]
        u = (yf_ref[0, rows, :] + yb_ref[0, rows, :]) * (zz * _sigmoid(zz))
        gw = SSD_INNER // SSD_GROUPS
        for g in range(SSD_GROUPS):
            lo, hi = gw * g, gw * (g + 1)
            ug = u[:, lo:hi]
            ms = jnp.mean(ug * ug, axis=-1, keepdims=True)
            mix_s[rows, GLA_V + lo:GLA_V + hi] = (ug * lax.rsqrt(ms + EPS) * sn_ref[:, lo:hi]).astype(BF16)

    x1_ref[0] = x_ref[0] + g1_ref[0] * _dot(mix_s[...], wo_ref[...])

    for rows, _, _ in halves:
        x1 = x1_ref[0, rows, :]
        ms = jnp.mean(x1 * x1, axis=-1, keepdims=True)
        h2 = x1 * lax.rsqrt(ms + EPS) * (n2_ref[...] * (1.0 + sc2_ref[0])) + sh2_ref[0]
        h_hi = h2.astype(BF16)
        h2_ref[0, rows, :] = h_hi
        hl_s[rows, :] = (h2 - h_hi.astype(F32)).astype(BF16)

    h_hi = h2_ref[0]
    logits = (_dot(h_hi, wrh_ref[...]) + _dot(hl_s[...], wrh_ref[...]) + _dot(h_hi, wrl_ref[...])) + br_ref[...]

    lane = lax.broadcasted_iota(jnp.int32, (TB, LANES), 1).astype(F32)
    st = [dict(work=logits[i * TB:(i + 1) * TB, :], eidx=jnp.full((TB, LANES), -1.0, F32),
               gates=jnp.zeros((TB, LANES), F32), sel=jnp.zeros((TB, LANES), F32), m0=None,
               den=jnp.zeros((TB, 1), F32)) for i in range(2)]
    for kk in range(TOP_K):
        for t in st:
            t["m"] = jnp.max(t["work"], axis=-1, keepdims=True)
        for t in st:
            t["idx"] = jnp.min(jnp.where(t["work"] == t["m"], lane, float(LANES)), axis=-1, keepdims=True)
        for t in st:
            hit = lane == t["idx"]
            if t["m0"] is None:
                t["m0"] = t["m"]
            e = jnp.exp(t["m"] - t["m0"])
            t["den"] = t["den"] + e
            t["eidx"] = jnp.where(lane == float(kk), t["idx"], t["eidx"])
            t["gates"] = jnp.where(lane == float(kk), e, t["gates"])
            t["sel"] = jnp.where(hit, 1.0, t["sel"])
            t["work"] = jnp.where(hit, NEG_BIG, t["work"])
    for (rows, _, _), t in zip(halves, st):
        eidx_ref[0, rows, :] = t["eidx"].astype(jnp.int32)
        gate_ref[0, rows, :] = t["gates"] / t["den"]
        cnt = jnp.sum(t["sel"], axis=0, keepdims=True)
        cnt_ref[...] += jnp.floor((cnt + (RUN - 1.0)) * (1.0 / RUN)) * RUN


def _outproj_call(o_f, o_b, g_all, y_f, y_b, z_all, x, g1, sh2, sc2, gn, sn, wo, n2, wr_hi, wr_lo, br):
    bsz, L, d = x.shape
    nj = L // (2 * TB)
    tok = lambda n: pl.BlockSpec((1, 2 * TB, n), lambda b, j: (b, j, 0))
    tok_off = lambda n, half: pl.BlockSpec((1, TB, n), lambda b, j: (b, 2 * j + 1 + half, 0))
    const = lambda a: pl.BlockSpec(a.shape, lambda b, j: (0,) * a.ndim)
    mod = pl.BlockSpec((1, 1, d), lambda b, j: (b, 0, 0))
    return pl.pallas_call(
        _outproj_kernel,
        grid=(bsz, nj),
        in_specs=[tok(GLA_V), tok(GLA_V), tok_off(GLA_V, 0), tok_off(GLA_V, 1),
                  tok(SSD_INNER), tok(SSD_INNER), tok_off(SSD_INNER, 0), tok_off(SSD_INNER, 1),
                  tok(d), mod, mod, mod, const(gn), const(sn), const(wo), const(n2),
                  const(wr_hi), const(wr_lo), const(br)],
        out_specs=[tok(d), tok(d), tok(LANES), tok(LANES), pl.BlockSpec((1, LANES), lambda b, j: (0, 0))],
        out_shape=[jax.ShapeDtypeStruct((bsz, L, d), F32), jax.ShapeDtypeStruct((bsz, L, d), BF16),
                   jax.ShapeDtypeStruct((bsz, L, LANES), jnp.int32), jax.ShapeDtypeStruct((bsz, L, LANES), F32),
                   jax.ShapeDtypeStruct((1, LANES), F32)],
        scratch_shapes=[pltpu.VMEM((2 * TB, d), BF16), pltpu.VMEM((2 * TB, d), BF16)],
        compiler_params=_params("arbitrary", "arbitrary"),
    )(o_f, o_b, g_all, g_all, y_f, y_b, z_all, z_all, x, g1, sh2, sc2, gn, sn, wo, n2, wr_hi, wr_lo, br)


def _pos_kernel(eidx_ref, pstart_ref, lst_ref, ust_ref, lp_ref, cd_ref, carry):
    @pl.when(pl.program_id(0) == 0)
    def _():
        carry[...] = pstart_ref[...]

    lane = lax.broadcasted_iota(jnp.int32, (TB, LANES), 1)
    tiles = []
    for t in range(cd_ref.shape[0]):
        eidx = eidx_ref[t * TB:(t + 1) * TB, :]
        hits = [lane == eidx[:, kk:kk + 1] for kk in range(TOP_K)]
        sel = jnp.zeros((TB, LANES), F32)
        for hmask in hits:
            sel = jnp.where(hmask, 1.0, sel)
        cnt = jnp.sum(sel, axis=0, keepdims=True)
        run = jnp.floor((cnt + (RUN - 1.0)) * (1.0 / RUN)) * RUN
        tiles.append(dict(hits=hits, sel=sel, run=run))
    for tl in tiles:
        tl["rank"] = _dot(lst_ref[...], tl["sel"].astype(BF16))
        tl["loff"] = _dot(jnp.broadcast_to(tl["run"], (8, LANES)).astype(BF16), ust_ref[...])[0:1]
    for t, tl in enumerate(tiles):
        pos = tl["loff"] + tl["rank"]
        lp = jnp.zeros((TB, LANES), jnp.int32)
        for kk, hmask in enumerate(tl["hits"]):
            lk = jnp.sum(jnp.where(hmask, pos, 0.0), axis=-1, keepdims=True)
            lp = jnp.where(lane == kk, lk.astype(jnp.int32), lp)
        lp_ref[t * TB:(t + 1) * TB, :] = lp

    eye = lax.broadcasted_iota(jnp.int32, (LANES, LANES), 0) == lax.broadcasted_iota(jnp.int32, (LANES, LANES), 1)
    col = lambda v: jnp.sum(jnp.where(eye, v, 0.0), axis=1, keepdims=True)
    row0 = (lax.broadcasted_iota(jnp.int32, (LANES, CD_LANES), 1) * RUN).astype(F32)
    base = carry[...]
    for t, tl in enumerate(tiles):
        loff_c, run_c, shift_c = col(tl["loff"]), col(tl["run"]), col(base - tl["loff"])
        inside = jnp.where(row0 >= loff_c, 1.0, 0.0) * jnp.where(row0 < loff_c + run_c, 1.0, 0.0)
        valid = jnp.sum(inside, axis=0, keepdims=True)
        dest = jnp.sum(inside * shift_c, axis=0, keepdims=True) + row0[0:1]
        cd_ref[t] = jnp.where(valid > 0.0, dest, -1.0).astype(jnp.int32)
        base = base + tl["run"]
    carry[...] = base


def _pos_call(eidx, pstart):
    T = eidx.shape[0]
    lst = jnp.asarray(np.tril(np.ones((TB, TB), np.float32), -1), BF16)
    ust = jnp.asarray(np.triu(np.ones((LANES, LANES), np.float32), 1), BF16)
    tps = next(n for n in (4, 2, 1) if (T // TB) % n == 0)
    return pl.pallas_call(
        _pos_kernel,
        grid=(T // (tps * TB),),
        in_specs=[pl.BlockSpec((tps * TB, LANES), lambda i: (i, 0)),
                  pl.BlockSpec((1, LANES), lambda i: (0, 0)),
                  pl.BlockSpec((TB, TB), lambda i: (0, 0)),
                  pl.BlockSpec((LANES, LANES), lambda i: (0, 0))],
        out_specs=[pl.BlockSpec((tps * TB, LANES), lambda i: (i, 0)),
                   pl.BlockSpec((tps, 1, CD_LANES), lambda i: (i, 0, 0))],
        out_shape=[jax.ShapeDtypeStruct((T, LANES), jnp.int32),
                   jax.ShapeDtypeStruct((T // TB, 1, CD_LANES), jnp.int32)],
        scratch_shapes=[pltpu.VMEM((1, LANES), F32)],
        compiler_params=_params("arbitrary"),
    )(eidx, pstart, lst, ust)


def _pair_matrix(lp, weights):
    j = lax.broadcasted_iota(jnp.int32, (TB, RL), 1)
    m = jnp.zeros((TB, RL), F32)
    for kk in range(TOP_K):
        w = 1.0 if weights is None else weights[:, kk:kk + 1]
        m = jnp.where(j == lp[:, kk:kk + 1], w, m)
    return m


def _pack_bf16_pairs(v, is_bf16_valued=False):
    if not is_bf16_valued:
        v = v.astype(BF16).astype(F32)
    bits = pltpu.bitcast(v, jnp.uint32)
    return bits[:, D_HALF:] | (bits[:, :D_HALF] >> 16)


def _unpack_bf16_pairs(w):
    lo = pltpu.bitcast(w << 16, F32).astype(BF16)
    hi = pltpu.bitcast(w & jnp.uint32(0xFFFF0000), F32).astype(BF16)
    return lo, hi


N_SLOTS = 3
TILES_PER_STEP = 2


def _dispatch_kernel(fill_ref, *refs):
    cd_refs = refs[:TILES_PER_STEP + 1]
    h_ref, lp_ref, xg_ref, sorted_s, sems, fill_sem = refs[TILES_PER_STEP + 1:]
    step = pl.program_id(0)
    spill0 = xg_ref.shape[0] - N_SLOTS * RL

    def start(cref, s, c, to_spill, sem):
        src = RUN * c if isinstance(c, int) else pl.multiple_of(RUN * c, RUN)
        d = cref[0, 0, c]
        d = pl.multiple_of(jnp.where(jnp.logical_or(d < 0, to_spill), spill0 + s * RL + src, d), RUN)
        pltpu.make_async_copy(sorted_s.at[s, pl.ds(src, RUN), :], xg_ref.at[pl.ds(d, RUN), :], sem).start()

    def drain(s, sem):
        pltpu.make_async_copy(sorted_s.at[s], xg_ref.at[pl.ds(0, RL), :], sem).wait()

    @pl.when(step == 0)
    def _():
        sorted_s[...] = jnp.zeros_like(sorted_s)
        for s in range(N_SLOTS):
            lax.fori_loop(0, RL // RUN, lambda c, carry, s=s: (start(cd_refs[0], s, c, True, fill_sem), carry)[1], 0)
        for s in range(N_SLOTS):
            drain(s, fill_sem)

    for k in range(TILES_PER_STEP):
        t = TILES_PER_STEP * step + k
        rows = slice(k * TB, (k + 1) * TB)
        slot = lax.rem(t, N_SLOTS)
        prev = lax.rem(t + N_SLOTS - 1, N_SLOTS)
        pprev = lax.rem(t + N_SLOTS - 2, N_SLOTS)

        for c in range(RL // RUN):
            start(cd_refs[k], prev, c, t == 0, sems.at[prev])

        sorted_s[slot] = _pack_bf16_pairs(
            _dot_tn(_pair_matrix(lp_ref[rows, :], None).astype(BF16), h_ref[rows, :]), is_bf16_valued=True)

        if k == 0:
            @pl.when(step == 0)
            def _():
                def fill(j):
                    blk = pl.multiple_of(fill_ref[j] * MOE_BLOCK, MOE_BLOCK)
                    return pltpu.make_async_copy(sorted_s.at[0, pl.ds(0, MOE_BLOCK), :],
                                                 xg_ref.at[pl.ds(blk, MOE_BLOCK), :], fill_sem)

                for j in range(fill_ref.shape[0]):
                    pl.when(fill_ref[j] >= 0)(lambda j=j: fill(j).start())
                for j in range(fill_ref.shape[0]):
                    pl.when(fill_ref[j] >= 0)(lambda j=j: fill(j).wait())

        pl.when(t >= 1)(lambda pprev=pprev: drain(pprev, sems.at[pprev]))

    @pl.when(step == pl.num_programs(0) - 1)
    def _():
        lax.fori_loop(0, RL // RUN,
                      lambda c, carry: (start(cd_refs[TILES_PER_STEP], slot, c, False, sems.at[slot]), carry)[1], 0,
                      unroll=4)
        drain(prev, sems.at[prev])
        drain(slot, sems.at[slot])


def _dispatch_call(fill_blocks, cd, h2, lp, P):
    T, d = h2.shape
    tps = TILES_PER_STEP
    assert (T // TB) % tps == 0
    cd_spec = lambda k: pl.BlockSpec((1, 1, CD_LANES), lambda i, fb: (jnp.maximum(tps * i + k - 1, 0), 0, 0),
                                     memory_space=pltpu.SMEM)
    grid_spec = pltpu.PrefetchScalarGridSpec(
        num_scalar_prefetch=1,
        grid=(T // (tps * TB),),
        in_specs=[cd_spec(k) for k in range(tps + 1)] +
                 [pl.BlockSpec((tps * TB, d), lambda i, fb: (i, 0)),
                  pl.BlockSpec((tps * TB, LANES), lambda i, fb: (i, 0))],
        out_specs=pl.BlockSpec(memory_space=pl.ANY),
        scratch_shapes=[pltpu.VMEM((N_SLOTS, RL, D_HALF), jnp.uint32), pltpu.SemaphoreType.DMA((N_SLOTS,)),
                        pltpu.SemaphoreType.DMA(())],
    )
    return pl.pallas_call(
        _dispatch_kernel,
        grid_spec=grid_spec,
        out_shape=jax.ShapeDtypeStruct((P + N_SLOTS * RL, D_HALF), jnp.uint32),
        compiler_params=_params("arbitrary"),
    )(fill_blocks, *([cd] * (tps + 1)), h2, lp)


def _moe_kernel(nb, info_ref, x_hbm, wgu_hbm, bgu_ref, wd_hbm, bd_ref, y_hbm,
                wgu_f, wd_f, wgu_b, wd_b, act_s, xbuf, ybuf, zbuf, wsems, xsems, ysems, zsem):
    n_used = info_ref[0, nb]

    def rows(blk):
        return pl.ds(pl.multiple_of(blk * MOE_BLOCK, MOE_BLOCK), MOE_BLOCK)

    def x_copy(blk, slot):
        return pltpu.make_async_copy(x_hbm.at[rows(blk), :], xbuf.at[slot], xsems.at[slot])

    def y_copy(blk, slot):
        return pltpu.make_async_copy(ybuf.at[slot], y_hbm.at[rows(blk), :], ysems.at[slot])

    def z_copy(blk):
        return pltpu.make_async_copy(zbuf, y_hbm.at[rows(blk), :], zsem)

    def fetch(expert, slot):
        return (pltpu.make_async_copy(wgu_hbm.at[expert], wgu_f.at[slot], wsems.at[0, slot]),
                pltpu.make_async_copy(wd_hbm.at[expert], wd_f.at[slot], wsems.at[1, slot]))

    zbuf[...] = jnp.zeros_like(zbuf)
    lax.fori_loop(n_used, nb, lambda blk, c: (z_copy(blk).start(), c)[1], 0)

    for cp in fetch(info_ref[0, 0], 0):
        cp.start()
    x_copy(0, 0).start()

    def body(i, carry):
        slot = i & 1
        e = info_ref[0, i]

        @pl.when(info_ref[1, i] == 1)
        def _():
            wslot = info_ref[2, i]
            nxt = info_ref[3, i]
            cps = fetch(e, wslot)
            cps[0].wait()
            wgu_b[...] = wgu_f[wslot].astype(BF16)
            cps[1].wait()
            wd_b[...] = wd_f[wslot].astype(BF16)

            @pl.when(nxt >= 0)
            def _():
                for cp in fetch(nxt, 1 - wslot):
                    cp.start()

        x_copy(i, slot).wait()
        pl.when(i + 1 < n_used)(lambda: x_copy(i + 1, 1 - slot).start())
        pl.when(i >= 2)(lambda: y_copy(i - 2, slot).wait())

        bgu = bgu_ref[pl.ds(e, 1), :]
        xb = jnp.concatenate(_unpack_bf16_pairs(xbuf[slot]), axis=1)
        cw = 256
        for c in range(D_FF // cw):
            lo, hi = c * cw, (c + 1) * cw
            gate = _dot(xb, wgu_b[:, lo:hi]) + bgu[:, lo:hi]
            up = _dot(xb, wgu_b[:, D_FF + lo:D_FF + hi]) + bgu[:, D_FF + lo:D_FF + hi]
            gate = jnp.minimum(gate, SWIGLU_LIMIT)
            up = jnp.clip(up, -SWIGLU_LIMIT, SWIGLU_LIMIT)
            act_s[:, lo:hi] = ((up + 1.0) * (gate * _sigmoid(SWIGLU_ALPHA * gate))).astype(BF16)
        ybuf[slot] = _pack_bf16_pairs(_dot(act_s[...], wd_b[...]) + bd_ref[pl.ds(e, 1), :])
        y_copy(i, slot).start()
        return carry

    lax.fori_loop(0, n_used, body, 0)

    pl.when(n_used >= 2)(lambda: y_copy(n_used - 2, n_used & 1).wait())
    y_copy(n_used - 1, (n_used - 1) & 1).wait()
    lax.fori_loop(n_used, nb, lambda blk, c: (z_copy(blk).wait(), c)[1], 0)


def _moe_call(info, xg, wgu, bgu, wd, bd, nb):
    P = nb * MOE_BLOCK
    ne, d, f2 = wgu.shape
    blk_buf = pltpu.VMEM((2, MOE_BLOCK, D_HALF), jnp.uint32)
    grid_spec = pltpu.PrefetchScalarGridSpec(
        num_scalar_prefetch=1,
        grid=(1,),
        in_specs=[pl.BlockSpec(memory_space=pl.ANY),
                  pl.BlockSpec(memory_space=pl.ANY),
                  pl.BlockSpec((ne, f2), lambda i, info: (0, 0)),
                  pl.BlockSpec(memory_space=pl.ANY),
                  pl.BlockSpec((ne, d), lambda i, info: (0, 0))],
        out_specs=pl.BlockSpec(memory_space=pl.ANY),
        scratch_shapes=[pltpu.VMEM((2, d, f2), F32), pltpu.VMEM((2, f2 // 2, d), F32),
                        pltpu.VMEM((d, f2), BF16), pltpu.VMEM((f2 // 2, d), BF16),
                        pltpu.VMEM((MOE_BLOCK, f2 // 2), BF16), blk_buf, blk_buf,
                        pltpu.VMEM((MOE_BLOCK, D_HALF), jnp.uint32),
                        pltpu.SemaphoreType.DMA((2, 2)), pltpu.SemaphoreType.DMA((2,)),
                        pltpu.SemaphoreType.DMA((2,)), pltpu.SemaphoreType.DMA(())],
    )
    return pl.pallas_call(
        functools.partial(_moe_kernel, nb),
        grid_spec=grid_spec,
        out_shape=jax.ShapeDtypeStruct((P, D_HALF), jnp.uint32),
        compiler_params=_params("arbitrary"),
    )(info, xg, wgu, bgu, wd, bd)


def _combine_kernel(*refs):
    cd_refs = refs[:TILES_PER_STEP + 2]
    y_ref, lp_ref, x1_ref, gate_ref, g2_ref, fn_ref, o_ref, buf, sems = refs[TILES_PER_STEP + 2:]
    step = pl.program_id(0)

    def start(dref, s, c):
        d = pl.multiple_of(jnp.maximum(dref[0, 0, c], 0), RUN)
        dst = pl.multiple_of(RUN * c, RUN)
        pltpu.make_async_copy(y_ref.at[pl.ds(d, RUN), :], buf.at[s, pl.ds(dst, RUN), :], sems.at[s]).start()

    def drain(s):
        pltpu.make_async_copy(y_ref.at[pl.ds(0, RL), :], buf.at[s], sems.at[s]).wait()

    @pl.when(step == 0)
    def _():
        lax.fori_loop(0, RL // RUN, lambda c, carry: (start(cd_refs[0], 0, c), carry)[1], 0, unroll=4)
        lax.fori_loop(0, RL // RUN, lambda c, carry: (start(cd_refs[1], 1, c), carry)[1], 0, unroll=4)

    for k in range(TILES_PER_STEP):
        t = TILES_PER_STEP * step + k
        rows = slice(k * TB, (k + 1) * TB)
        slot = lax.rem(t, N_SLOTS)
        ahead = lax.rem(t + 2, N_SLOTS)
        drain(slot)

        for c in range(RL // RUN):
            start(cd_refs[k + 2], ahead, c)

        g = _pair_matrix(lp_ref[rows, :], gate_ref[rows, :]).astype(BF16)
        halves = [_dot(g, yb) for yb in _unpack_bf16_pairs(buf[slot])]
        xo = x1_ref[rows, :] + g2_ref[0] * jnp.concatenate(halves, axis=1)
        ms = jnp.mean(xo * xo, axis=-1, keepdims=True)
        o_ref[rows, :] = xo * lax.rsqrt(ms + EPS) * fn_ref[...]

    @pl.when(step == pl.num_programs(0) - 1)
    def _():
        drain(lax.rem(t + 1, N_SLOTS))
        drain(ahead)


def _combine_call(cd, y, lp, x1, gates, g2, fn):
    T, d = x1.shape
    tps = TILES_PER_STEP
    n = T // TB
    assert n % tps == 0 and (n // g2.shape[0]) % tps == 0
    steps_per_batch = n // g2.shape[0] // tps
    cd_spec = lambda k: pl.BlockSpec((1, 1, CD_LANES), lambda i: (jnp.minimum(tps * i + k, n - 1), 0, 0),
                                     memory_space=pltpu.SMEM)
    return pl.pallas_call(
        _combine_kernel,
        grid=(n // tps,),
        in_specs=[cd_spec(k) for k in range(tps + 2)] +
                 [pl.BlockSpec(memory_space=pl.ANY),
                  pl.BlockSpec((tps * TB, LANES), lambda i: (i, 0)),
                  pl.BlockSpec((tps * TB, d), lambda i: (i, 0)),
                  pl.BlockSpec((tps * TB, LANES), lambda i: (i, 0)),
                  pl.BlockSpec((1, 1, d), lambda i: (i // steps_per_batch, 0, 0)),
                  pl.BlockSpec((1, d), lambda i: (0, 0))],
        out_specs=pl.BlockSpec((tps * TB, d), lambda i: (i, 0)),
        out_shape=jax.ShapeDtypeStruct((T, d), F32),
        scratch_shapes=[pltpu.VMEM((N_SLOTS, RL, D_HALF), jnp.uint32), pltpu.SemaphoreType.DMA((N_SLOTS,))],
        compiler_params=_params("arbitrary"),
    )(*([cd] * (tps + 2)), y, lp, x1, gates, g2, fn)


def _layer(x, c, ctx, c_ctx, w_mod, b_mod, norm1, w_in, gla_w_gk_up, gla_b_gk, gla_norm,
           ssd_conv_w, ssd_conv_b, ssd_dt_bias, ssd_A_log, ssd_D, ssd_norm, w_out,
           norm2, w_router, b_router, w_gate_up, b_gate_up, w_down, b_down, final_norm):
    bsz, L, d = x.shape
    lc = ctx.shape[1]
    assert lc == TB and L % TB == 0 and TB % GRID_W == 0

    cin = jnp.zeros((8, d), F32).at[:bsz].set(c).at[bsz].set(c_ctx)
    mod = _mod_call(cin, w_mod, b_mod.reshape(1, -1))[:bsz + 1]
    sh1, sc1, g1, sh2, sc2, g2 = [m.reshape(bsz + 1, 1, d) for m in jnp.split(mod, 6, axis=-1)]

    o = np.cumsum((0, GLA_QK, GLA_QK, GLA_V, GLA_V, GLA_RANK, SSD_INNER, SSD_CONV_DIM, SSD_HEADS))
    wq, wk, wv, wg, wlow, wz, wx, wdt = [w_in[:, int(a):int(b)] for a, b in zip(o[:-1], o[1:])]
    w_misc = jnp.concatenate([wlow, wdt, wdt, jnp.zeros((d, LANES - GLA_RANK - 2 * SSD_HEADS), F32)], axis=1)
    w_cat = jnp.concatenate([wq, wk, wv, wg, wz, wx, w_misc], axis=1).astype(BF16)
    wup = jnp.zeros((LANES, 2 * GLA_QK), F32).at[:GLA_RANK].set(
        jnp.concatenate([gla_w_gk_up[0], gla_w_gk_up[1]], axis=1)).astype(BF16)
    bup = jnp.concatenate([gla_b_gk[0], gla_b_gk[1]]).reshape(1, -1)
    dtb = jnp.zeros((1, LANES), F32).at[0, DT_F:DT_F + SSD_HEADS].set(ssd_dt_bias[0]) \
                                    .at[0, DT_B:DT_B + SSD_HEADS].set(ssd_dt_bias[1])
    qkv, g_all, z_all, xbc, ldm = _inproj_call(x, ctx, sh1, sc1, norm1.reshape(1, d), w_cat, wup, bup, dtb)

    xs, bc = _conv_call(xbc, ssd_conv_w.reshape(9, SSD_CONV_DIM), ssd_conv_b.reshape(1, -1), lc + L)

    a_neg = -jnp.exp(ssd_A_log.astype(F32))
    a_f = jnp.zeros((1, LANES), F32).at[0, DT_F:DT_F + SSD_HEADS].set(a_neg[0])
    a_b = jnp.zeros((1, LANES), F32).at[0, DT_B:DT_B + SSD_HEADS].set(a_neg[1])
    dvec = jnp.repeat(ssd_D, SSD_HEADDIM).reshape(1, SSD_INNER)
    o_f, o_b, y_f, y_b = _scan_call(qkv, ldm, xs, bc, a_f, a_b, dvec, L)

    wr = jnp.zeros((d, LANES), F32).at[:, :N_EXPERTS].set(w_router)
    wr_hi = wr.astype(BF16)
    wr_lo = (wr - wr_hi.astype(F32)).astype(BF16)
    br = jnp.full((1, LANES), NEG_BIG, F32).at[0, :N_EXPERTS].set(b_router)
    x1, h2, eidx, gates, counts = _outproj_call(
        o_f, o_b, g_all, y_f, y_b, z_all, x, g1[:bsz], sh2[:bsz], sc2[:bsz],
        jnp.tile(gla_norm, GLA_HEADS).reshape(1, -1), ssd_norm.reshape(1, -1), w_out.astype(BF16),
        norm2.reshape(1, d), wr_hi, wr_lo, br)

    T = bsz * L
    cnt = counts[0, :N_EXPERTS].astype(jnp.int32)
    padded = ((cnt + MOE_BLOCK - 1) // MOE_BLOCK) * MOE_BLOCK
    pend = jnp.cumsum(padded)
    pstart = pend - padded
    max_rows = T * TOP_K + (T // TB) * N_EXPERTS * (RUN - 1)
    n_blocks = -(-max_rows // MOE_BLOCK) + N_EXPERTS
    blk_start = jnp.arange(n_blocks, dtype=jnp.int32) * MOE_BLOCK
    blk_e = jnp.minimum(jnp.sum(pend[None, :] <= blk_start[:, None], axis=1), N_EXPERTS - 1).astype(jnp.int32)
    n_used = (pend[-1] // MOE_BLOCK).astype(jnp.int32)
    blk_i = jnp.arange(n_blocks, dtype=jnp.int32)
    first = (blk_i < n_used) & ((blk_i == 0) | (blk_e != jnp.roll(blk_e, 1)))
    slot = (jnp.cumsum(first) - 1) & 1
    first_pos = jnp.where(first, blk_i, n_blocks)
    next_first = jnp.roll(lax.cummin(first_pos, reverse=True), -1).at[-1].set(n_blocks)
    nxt = jnp.where(next_first < n_blocks, blk_e[jnp.minimum(next_first, n_blocks - 1)], -1)
    col = lambda v, last: jnp.concatenate([v.astype(jnp.int32), jnp.asarray([last], jnp.int32)])
    blk_info = jnp.stack([col(blk_e, 0).at[-1].set(n_used), col(first, 0), col(slot, 0), col(nxt, -1)])
    n_tail = n_blocks - (T * TOP_K) // MOE_BLOCK
    tail = n_used + jnp.arange(n_tail, dtype=jnp.int32)
    fill_blocks = jnp.concatenate([
        jnp.where(padded > 0, pend // MOE_BLOCK - 1, -1),
        jnp.where(tail < n_blocks, tail, -1)]).astype(jnp.int32)
    pstart_row = jnp.zeros((1, LANES), F32).at[0, :N_EXPERTS].set(pstart.astype(F32))
    lp, cd = _pos_call(eidx.reshape(T, LANES), pstart_row)

    xg = _dispatch_call(fill_blocks, cd, h2.reshape(T, d), lp, n_blocks * MOE_BLOCK)
    y = _moe_call(blk_info, xg, w_gate_up, b_gate_up, w_down, b_down, n_blocks)
    out = _combine_call(cd, y, lp, x1.reshape(T, d), gates.reshape(T, LANES), g2[:bsz], final_norm.reshape(1, d))
    return out.reshape(bsz, L, d)


def kernel(x, c, ctx, c_ctx, w_mod, b_mod, norm1, w_in, gla_w_gk_up, gla_b_gk, gla_norm, ssd_conv_w, ssd_conv_b, ssd_dt_bias, ssd_A_log, ssd_D, ssd_norm, w_out, norm2, w_router, b_router, w_gate_up, b_gate_up, w_down, b_down, final_norm):
    assert w_mod.shape[0] == 1, "single-layer kernel"
    return _layer(x, c, ctx, c_ctx, w_mod[0], b_mod[0], norm1[0], w_in[0], gla_w_gk_up[0], gla_b_gk[0],
                  gla_norm[0], ssd_conv_w[0], ssd_conv_b[0], ssd_dt_bias[0], ssd_A_log[0], ssd_D[0],
                  ssd_norm[0], w_out[0], norm2[0], w_router[0], b_router[0], w_gate_up[0], b_gate_up[0],
                  w_down[0], b_down[0], final_norm)
```

```python
import functools

import numpy as np
import jax
import jax.numpy as jnp
from jax import lax
from jax.experimental import pallas as pl
from jax.experimental.pallas import tpu as pltpu

F32 = jnp.float32
BF16 = jnp.bfloat16

EPS = 1e-6
GRID_W = 64
GLA_HEADS = 4
GLA_DK = 64
GLA_DV = 128
GLA_QK = GLA_HEADS * GLA_DK
GLA_V = GLA_HEADS * GLA_DV
GLA_RANK = 16
GLA_GATE_NORM = 16.0
SSD_HEADDIM = 64
SSD_INNER = 512
SSD_HEADS = 8
SSD_GROUPS = 2
SSD_HPG = 4
SSD_STATE = 128
SSD_CONV_DIM = 1024
N_EXPERTS = 32
TOP_K = 4
D_FF = 1024
SWIGLU_LIMIT = 7.0
SWIGLU_ALPHA = 1.702
MOE_BLOCK = 1024

TB = 256
GLA_C = 64
GLA_STAGE_GROUP = 1
SSD_C = 128
SSD_STAGE_GROUP = 1
LANES = 128
EXP_CLAMP = 80.0
DT_F = 16
DT_B = 24
NEG_BIG = -1e30
RUN = 8
RL = -(-(TB * TOP_K + N_EXPERTS * (RUN - 1)) // LANES) * LANES
CD_LANES = -(-(RL // RUN) // LANES) * LANES
D_HALF = 512
VMEM_LIMIT = 56 * 1024 * 1024


def _dot(a, b):
    return jnp.dot(a, b, preferred_element_type=F32)


def _dot_nt(a, b):
    return lax.dot_general(a, b, (((1,), (1,)), ((), ())), preferred_element_type=F32)


def _dot_tn(a, b):
    return lax.dot_general(a, b, (((0,), (0,)), ((), ())), preferred_element_type=F32)


def _split3(a):
    hi = a.astype(BF16)
    r1 = a - hi.astype(F32)
    mid = r1.astype(BF16)
    lo = (r1 - mid.astype(F32)).astype(BF16)
    return hi, mid, lo


def _dot_exact_r(m, a):
    hi, mid, lo = _split3(a)
    return _dot(m, hi) + _dot(m, mid) + _dot(m, lo)


def _dot_hilo_r(m, a):
    hi = a.astype(BF16)
    lo = (a - hi.astype(F32)).astype(BF16)
    return _dot(m, hi) + _dot(m, lo)


def _dot_hilo_l(a, m2):
    hi = a.astype(BF16)
    lo = (a - hi.astype(F32)).astype(BF16)
    return _dot(jnp.concatenate([hi, lo], axis=1), m2)


def _sigmoid(x):
    return 1.0 / (1.0 + jnp.exp(-x))


def _softplus(x):
    return jnp.maximum(x, 0.0) + jnp.log1p(jnp.exp(-jnp.abs(x)))


def _params(*sem):
    return pltpu.CompilerParams(dimension_semantics=sem, vmem_limit_bytes=VMEM_LIMIT)


def _mod_kernel(c_ref, w_ref, b_ref, o_ref):
    c = c_ref[...]
    s = c * _sigmoid(c)
    s_hi = s.astype(BF16)
    s_lo = (s - s_hi.astype(F32)).astype(BF16)
    w = w_ref[...]
    w_hi = w.astype(BF16)
    w_lo = (w - w_hi.astype(F32)).astype(BF16)
    o_ref[...] = _dot(s_hi, w_hi) + _dot(s_lo, w_hi) + _dot(s_hi, w_lo) + b_ref[...]


def _mod_call(cin, w, b):
    rows, d = cin.shape
    n = w.shape[1]
    tn = 1536
    return pl.pallas_call(
        _mod_kernel,
        grid=(n // tn,),
        in_specs=[pl.BlockSpec((rows, d), lambda i: (0, 0)),
                  pl.BlockSpec((d, tn), lambda i: (0, i)),
                  pl.BlockSpec((1, tn), lambda i: (0, i))],
        out_specs=pl.BlockSpec((rows, tn), lambda i: (0, i)),
        out_shape=jax.ShapeDtypeStruct((rows, n), F32),
        compiler_params=_params("arbitrary"),
    )(cin, w, b)


_C_Q, _C_K, _C_V, _C_G, _C_Z, _C_X, _C_M, _C_END = 0, 256, 512, 1024, 1536, 2048, 3072, 3200


def _inproj_kernel(x0_ref, xa_ref, xb_ref, ctx_ref, shl_ref, scl_ref, shc_ref, scc_ref, n1_ref, w_ref, wup_ref,
                   bup_ref, dtb_ref, q_ref, k_ref, v_ref, g_ref, z_ref, xbc_ref, ld_ref, misc_ref, h_scr):
    j = pl.program_id(1)
    slot = j & 1

    def normmod(xv, sh_ref, sc_ref):
        ms = jnp.mean(xv * xv, axis=-1, keepdims=True)
        y = xv * lax.rsqrt(ms + EPS) * n1_ref[...]
        return (y * (1.0 + sc_ref[0]) + sh_ref[0]).astype(BF16)

    @pl.when(j == 0)
    def _():
        h_scr[0, 0:TB, :] = normmod(ctx_ref[0], shc_ref, scc_ref)
        h_scr[0, TB:2 * TB, :] = normmod(x0_ref[0], shl_ref, scl_ref)

    def mm(lo, hi):
        return _dot(h_scr[slot], w_ref[:, lo:hi])

    m = mm(_C_M, _C_END)
    zz = _dot(m.astype(BF16), wup_ref[...]) + bup_ref[...]
    ld_ref[0] = -_softplus(-zz) * (1.0 / GLA_GATE_NORM)
    misc_ref[0] = _softplus(m + dtb_ref[...])
    q_ref[0] = (mm(_C_Q, _C_K) * (GLA_DK ** -0.5)).astype(BF16)
    h_scr[1 - slot, 0:TB, :] = normmod(xa_ref[0], shl_ref, scl_ref)
    k_ref[0] = mm(_C_K, _C_V).astype(BF16)
    h_scr[1 - slot, TB:2 * TB, :] = normmod(xb_ref[0], shl_ref, scl_ref)
    v_ref[0] = mm(_C_V, _C_G).astype(BF16)
    g_ref[0] = mm(_C_G, _C_Z)
    z_ref[0] = mm(_C_Z, _C_X)
    xbc_ref[0] = mm(_C_X, _C_M)


def _inproj_call(x, ctx, sh1, sc1, n1, w_cat, wup, bup, dtb):
    bsz, L, d = x.shape
    nx = L // TB
    nj = -(-(nx + 1) // 2)
    tok = lambda n: pl.BlockSpec((1, 2 * TB, n), lambda b, j: (b, j, 0))
    const = lambda a: pl.BlockSpec(a.shape, lambda b, j: (0,) * a.ndim)
    xblk = lambda off: pl.BlockSpec((1, TB, d), lambda b, j: (b, jnp.minimum(2 * j + off, nx - 1), 0))
    first = pl.BlockSpec((1, TB, d), lambda b, j: (b, 0, 0))
    mod_lat = pl.BlockSpec((1, 1, d), lambda b, j: (b, 0, 0))
    mod_ctx = pl.BlockSpec((1, 1, d), lambda b, j: (bsz, 0, 0))
    outs = [(GLA_QK, BF16), (GLA_QK, BF16), (GLA_V, BF16), (GLA_V, F32), (SSD_INNER, F32),
            (SSD_CONV_DIM, F32), (2 * GLA_QK, F32), (LANES, F32)]
    return pl.pallas_call(
        _inproj_kernel,
        grid=(bsz, nj),
        in_specs=[first, xblk(1), xblk(2), first,
                  mod_lat, mod_lat, mod_ctx, mod_ctx,
                  const(n1), const(w_cat), const(wup), const(bup), const(dtb)],
        out_specs=[tok(n) for n, _ in outs],
        out_shape=[jax.ShapeDtypeStruct((bsz, nj * 2 * TB, n), dt) for n, dt in outs],
        scratch_shapes=[pltpu.VMEM((2, 2 * TB, d), BF16)],
        compiler_params=_params("arbitrary", "arbitrary"),
    )(x, x, x, ctx, sh1, sc1, sh1, sc1, n1, w_cat, wup, bup, dtb)


_EXT_PAD = 8
_EXT_BASE = _EXT_PAD + GRID_W
_EXT_ROWS = 2 * _EXT_PAD + 2 * GRID_W + TB


def _conv_kernel(prev_ref, cur_ref, next_ref, w_ref, b_ref, xs_ref, bc_ref, ext_s, xl_all, xr_all):
    for bb in range(cur_ref.shape[0]):
        _conv_block(prev_ref.at[bb], cur_ref.at[bb], next_ref.at[bb], w_ref, b_ref, xs_ref.at[bb], bc_ref.at[bb],
                    ext_s.at[bb], xl_all.at[bb], xr_all.at[bb])


def _conv_block(prev_ref, cur_ref, next_ref, w_ref, b_ref, xs_ref, bc_ref, ext, xl_s, xr_s):
    j = pl.program_id(1)
    nj = pl.num_programs(1)
    is_ctx = j == 0
    zpad = jnp.zeros((_EXT_PAD, SSD_CONV_DIM), F32)
    ext[0:_EXT_PAD, :] = zpad
    ext[_EXT_ROWS - _EXT_PAD:_EXT_ROWS, :] = zpad
    ext[_EXT_PAD:_EXT_BASE, :] = jnp.where(j >= 2, prev_ref[...], 0.0)
    ext[_EXT_BASE:_EXT_BASE + TB, :] = cur_ref[...]
    ext[_EXT_BASE + TB:_EXT_BASE + TB + GRID_W, :] = jnp.where(
        jnp.logical_and(j >= 1, j <= nj - 2), next_ref[...], 0.0)

    win = TB + 2 * GRID_W
    u = lax.broadcasted_iota(jnp.int32, (win, LANES), 0)
    pos = jnp.where(is_ctx, u - GRID_W, u & (GRID_W - 1))
    ok_l = pos >= 1
    ok_r = jnp.where(is_ctx, pos - (TB - GRID_W), pos) <= GRID_W - 2
    lat = jnp.where(is_ctx, 0.0, 1.0)
    side = {-1: xl_s, 0: None, 1: xr_s}

    for c in range(SSD_CONV_DIM // LANES):
        lo, hi = c * LANES, (c + 1) * LANES
        xl_s[:, lo:hi] = jnp.where(ok_l, ext[_EXT_PAD - 1:_EXT_PAD - 1 + win, lo:hi], 0.0)
        xr_s[:, lo:hi] = jnp.where(ok_r, ext[_EXT_PAD + 1:_EXT_PAD + 1 + win, lo:hi], 0.0)
        acc = jnp.zeros((TB, LANES), F32)
        for dr in (-1, 0, 1):
            for dc in (-1, 0, 1):
                if dc == 0:
                    start = _EXT_BASE + GRID_W * dr
                    tap = ext[start:start + TB, lo:hi]
                else:
                    start = GRID_W + GRID_W * dr
                    tap = side[dc][start:start + TB, lo:hi]
                wi = 3 * (dr + 1) + (dc + 1)
                wv = w_ref[wi:wi + 1, lo:hi]
                if dr != 0:
                    wv = wv * lat
                acc = acc + tap * wv
        y = acc + b_ref[:, lo:hi]
        y = y * _sigmoid(y)
        if c < SSD_INNER // LANES:
            xs_ref[:, lo:hi] = y
        else:
            bc_ref[:, lo - SSD_INNER:hi - SSD_INNER] = y.astype(BF16)


def _conv_call(xbc, w9, bias, ls):
    bsz, _, ch = xbc.shape
    nj = ls // TB
    rpb = TB // GRID_W
    nrow = ls // GRID_W
    nb = next(n for n in (4, 2, 1) if bsz % n == 0)
    return pl.pallas_call(
        _conv_kernel,
        grid=(bsz // nb, nj),
        in_specs=[pl.BlockSpec((nb, GRID_W, ch), lambda b, j: (b, jnp.maximum(rpb * j - 1, 0), 0)),
                  pl.BlockSpec((nb, TB, ch), lambda b, j: (b, j, 0)),
                  pl.BlockSpec((nb, GRID_W, ch), lambda b, j: (b, jnp.minimum(rpb * j + rpb, nrow - 1), 0)),
                  pl.BlockSpec((9, ch), lambda b, j: (0, 0)),
                  pl.BlockSpec((1, ch), lambda b, j: (0, 0))],
        out_specs=[pl.BlockSpec((nb, TB, SSD_INNER), lambda b, j: (b, j, 0)),
                   pl.BlockSpec((nb, TB, ch - SSD_INNER), lambda b, j: (b, j, 0))],
        out_shape=[jax.ShapeDtypeStruct((bsz, ls, SSD_INNER), F32),
                   jax.ShapeDtypeStruct((bsz, ls, ch - SSD_INNER), BF16)],
        scratch_shapes=[pltpu.VMEM((nb, _EXT_ROWS, ch), F32), pltpu.VMEM((nb, TB + 2 * GRID_W, ch), F32),
                        pltpu.VMEM((nb, TB + 2 * GRID_W, ch), F32)],
        compiler_params=_params("arbitrary", "arbitrary"),
    )(xbc, xbc, xbc, w9, bias)


def _fwd_blk(s):
    return s


def _bwd_blk(s, ns):
    return jnp.where(s == 0, 0, ns - s)


def _scan_batch(bsz):
    return 2 if bsz % 2 == 0 else 1


class _GlaChunk:
    def __init__(self, q, k, v, la, st_ref, tri_m, fwd, store):
        self.q, self.k, self.v, self.la, self.st_ref, self.tri_m, self.fwd, self.store = (
            q, k, v, la, st_ref, tri_m, fwd, store)

    def stage_sums(self):
        self.b = _dot_hilo_r(self.tri_m, self.la)

    def stage_factors(self):
        C = GLA_C
        b = self.b
        self.bt = b[C - 1:C, :] if self.fwd else b[0:1, :]
        r = 0.5 * self.bt
        self.er = jnp.exp(r)
        self.qt = (self.q.astype(F32) * jnp.exp(jnp.minimum(b - r, EXP_CLAMP))).astype(BF16)
        kt = (self.k.astype(F32) * jnp.exp(jnp.minimum(r - b, EXP_CLAMP))).astype(BF16)
        head_k = lax.broadcasted_iota(jnp.int32, (C, GLA_QK), 1) >> 6
        zero = jnp.zeros_like(kt)
        self.kh = [jnp.where(head_k == h, kt, zero) for h in range(GLA_HEADS)]
        self.qh = [jnp.where(head_k == h, self.qt, zero) for h in range(GLA_HEADS)]

    def stage_products(self):
        v = self.v
        kcat = jnp.concatenate(self.kh, axis=0)
        self.sc = _dot_nt(self.qt, kcat)
        vcat = jnp.concatenate([v[:, h * GLA_DV:(h + 1) * GLA_DV] for h in range(GLA_HEADS)], axis=0)
        self.u = _dot_tn(vcat, kcat) * self.er

    def stage_mask(self):
        C = GLA_C
        ii = lax.broadcasted_iota(jnp.int32, (C, GLA_HEADS * C), 0)
        jj = lax.broadcasted_iota(jnp.int32, (C, GLA_HEADS * C), 1) & (C - 1)
        causal = (jj <= ii) if self.fwd else (jj >= ii)
        self.p = jnp.where(causal, self.sc, 0.0).astype(BF16)
        v = self.v
        head_v = lax.broadcasted_iota(jnp.int32, (C, GLA_V), 1) >> 7
        self.vst = jnp.concatenate([jnp.where(head_v == h, v, jnp.zeros_like(v)) for h in range(GLA_HEADS)], axis=0)

    def stage_intra(self):
        self.o = _dot(self.p, self.vst)

    def stage_inter(self):
        st = self.st_ref[...]
        ster = (st * self.er).astype(BF16)
        self.st_ref[...] = st * jnp.exp(self.bt) + self.u
        res = _dot_nt(jnp.concatenate(self.qh, axis=0), ster)
        self.inter = jnp.concatenate([res[h * GLA_C:(h + 1) * GLA_C, :] for h in range(GLA_HEADS)], axis=1)

    def stage_out(self):
        self.store(self.o + self.inter)


def _store_to(ref, bb, sl):
    def store(val):
        ref[bb, sl, :] = val
    return store


def _gla_schedule(qf_ref, kf_ref, vf_ref, lf_ref, qb_ref, kb_ref, vb_ref, lb_ref, trif_ref, trib_ref,
                  of_ref, ob_ref, stf, stb):
    nsub = TB // GLA_C
    store_to = _store_to
    calls = []

    group = GLA_STAGE_GROUP
    for g0 in range(0, nsub, group):
        steps = []
        for i in range(g0, g0 + group):
            sf = pl.ds(i * GLA_C, GLA_C)
            sb = pl.ds((nsub - 1 - i) * GLA_C, GLA_C)
            chunks = []
            for bb in range(qf_ref.shape[0]):
                chunks.append(_GlaChunk(qf_ref[bb, sf, :], kf_ref[bb, sf, :], vf_ref[bb, sf, :], lf_ref[bb, sf, :],
                                        stf.at[bb], trif_ref[...], True, store_to(of_ref, bb, sf)))
                chunks.append(_GlaChunk(qb_ref[bb, sb, :], kb_ref[bb, sb, :], vb_ref[bb, sb, :], lb_ref[bb, sb, :],
                                        stb.at[bb], trib_ref[...], False, store_to(ob_ref, bb, sb)))
            steps.append(chunks)
        for stage in ("stage_sums", "stage_factors", "stage_products", "stage_mask", "stage_intra"):
            for chunks in steps:
                calls += [getattr(ch, stage) for ch in chunks]
        for chunks in steps:
            for stage in ("stage_inter", "stage_out"):
                calls += [getattr(ch, stage) for ch in chunks]
    return calls


class _SsdChunk:
    def __init__(self, xs, bc, dtm, avec, dvec, st_ref, tri_m, e_m, base, fwd, store):
        self.xs, self.bc, self.dtm, self.avec, self.dvec, self.st_ref = xs, bc, dtm, avec, dvec, st_ref
        self.tri_m, self.e_m, self.base, self.fwd, self.store = tri_m, e_m, base, fwd, store

    def stage_sums(self):
        self.dt_exp = _dot_hilo_l(self.dtm, self.e_m)
        self.acum = _dot_exact_r(self.tri_m, self.dtm * self.avec)

    def stage_expand(self):
        self.acum_exp = _dot_hilo_l(self.acum, self.e_m)
        self.acum_t = self.acum.T
        self.xdt = self.xs * self.dt_exp
        bc = self.bc
        self.bg = [bc[:, 128 * g:128 * (g + 1)] for g in range(SSD_GROUPS)]
        self.cg = [bc[:, 256 + 128 * g:256 + 128 * (g + 1)] for g in range(SSD_GROUPS)]
        self.cb = [_dot_nt(self.cg[g], self.bg[g]) for g in range(SSD_GROUPS)]

    def stage_decay(self):
        C = SSD_C
        ii = lax.broadcasted_iota(jnp.int32, (C, C), 0)
        jj = lax.broadcasted_iota(jnp.int32, (C, C), 1)
        tri = (jj <= ii) if self.fwd else (jj >= ii)
        self.ms = []
        for g in range(SSD_GROUPS):
            for rr in range(SSD_HPG):
                ln = self.base + SSD_HPG * g + rr
                diff = self.acum[:, ln:ln + 1] - self.acum_t[ln:ln + 1, :]
                seg = jnp.where(tri, jnp.exp(jnp.minimum(diff, 0.0)), 0.0)
                self.ms.append((self.cb[g] * seg).astype(BF16))
        ae = self.acum_exp
        self.al_exp = ae[C - 1:C, :] if self.fwd else ae[0:1, :]
        self.xw = (self.xdt * jnp.exp(self.al_exp - ae)).astype(BF16)
        xdt_b = self.xdt.astype(BF16)
        gw = SSD_HPG * SSD_HEADDIM
        head = lax.broadcasted_iota(jnp.int32, (C, gw), 1) >> 6
        zero = jnp.zeros((C, gw), BF16)
        self.xh = [jnp.where(head == h % SSD_HPG, xdt_b[:, gw * (h // SSD_HPG):gw * (h // SSD_HPG + 1)], zero)
                   for h in range(SSD_HEADS)]

    def stage_products(self):
        self.yg, self.ug = [], []
        for g in range(SSD_GROUPS):
            gl, gh = 256 * g, 256 * (g + 1)
            yg = _dot(self.ms[SSD_HPG * g], self.xh[SSD_HPG * g])
            for rr in range(1, SSD_HPG):
                yg = yg + _dot(self.ms[SSD_HPG * g + rr], self.xh[SSD_HPG * g + rr])
            self.yg.append(yg)
            self.ug.append(_dot_tn(self.bg[g], self.xw[:, gl:gh]))

    def stage_state(self):
        ys = []
        for g in range(SSD_GROUPS):
            gl, gh = 256 * g, 256 * (g + 1)
            sg = self.st_ref[g]
            yoff = _dot(self.cg[g], sg.astype(BF16)) * jnp.exp(self.acum_exp[:, gl:gh])
            self.st_ref[g] = sg * jnp.exp(self.al_exp[:, gl:gh]) + self.ug[g]
            ys.append(self.yg[g] + yoff)
        y = jnp.concatenate(ys, axis=1)
        if self.dvec is not None:
            y = y + self.dvec * self.xs
        self.store(y)


def _ssd_schedule(xf_ref, bcf_ref, mf_ref, xb_ref, bcb_ref, mb_ref, af_ref, ab_ref, d_ref,
                  trif_ref, trib_ref, ef_ref, eb_ref, yf_ref, yb_ref, stf, stb):
    nsub = TB // SSD_C
    store_to = _store_to
    calls = []
    steps = []
    for i in range(nsub):
        sf = pl.ds(i * SSD_C, SSD_C)
        sb = pl.ds((nsub - 1 - i) * SSD_C, SSD_C)
        chunks = []
        for bb in range(xf_ref.shape[0]):
            chunks.append(_SsdChunk(xf_ref[bb, sf, :], bcf_ref[bb, sf, :], mf_ref[bb, sf, :], af_ref[...],
                                    d_ref[...], stf.at[bb], trif_ref[...], ef_ref[...], DT_F, True,
                                    store_to(yf_ref, bb, sf)))
            chunks.append(_SsdChunk(xb_ref[bb, sb, :], bcb_ref[bb, sb, :], mb_ref[bb, sb, :], ab_ref[...],
                                    None, stb.at[bb], trib_ref[...], eb_ref[...], DT_B, False,
                                    store_to(yb_ref, bb, sb)))
        steps.append(chunks)
    for g0 in range(0, nsub, SSD_STAGE_GROUP):
        group = steps[g0:g0 + SSD_STAGE_GROUP]
        for stage in ("stage_sums", "stage_expand", "stage_decay", "stage_products"):
            for chunks in group:
                calls += [getattr(ch, stage) for ch in chunks]
        for chunks in group:
            calls += [ch.stage_state for ch in chunks]
    return calls


N_GLA_IN, N_SSD_IN = 10, 13


def _scan_kernel(*refs):
    gla_in, ssd_in = refs[:N_GLA_IN], refs[N_GLA_IN:N_GLA_IN + N_SSD_IN]
    of_ref, ob_ref, yf_ref, yb_ref, gstf, gstb, sstf, sstb = refs[N_GLA_IN + N_SSD_IN:]

    @pl.when(pl.program_id(1) == 0)
    def _():
        for st in (gstf, gstb, sstf, sstb):
            st[...] = jnp.zeros_like(st)

    gla = _gla_schedule(*gla_in, of_ref, ob_ref, gstf, gstb)
    ssd = _ssd_schedule(*ssd_in, yf_ref, yb_ref, sstf, sstb)
    merged = sorted([((n + 0.5) / len(gla), 0, n, c) for n, c in enumerate(gla)] +
                    [((n + 0.5) / len(ssd), 1, n, c) for n, c in enumerate(ssd)], key=lambda t: t[:3])
    for _, _, _, call in merged:
        call()


def _expand_matrix(base):
    e = np.zeros((LANES, SSD_INNER), np.float32)
    for h in range(SSD_HEADS):
        e[base + h, SSD_HEADDIM * h:SSD_HEADDIM * (h + 1)] = 1.0
    return jnp.asarray(np.concatenate([e, e], axis=0), BF16)


def _scan_call(q, k, v, ld, xs, bc, misc, a_f, a_b, dvec, L):
    bsz = q.shape[0]
    nx = L // TB
    ns = nx + 1
    tri = lambda c, up: jnp.asarray((np.triu if up else np.tril)(np.ones((c, c), np.float32)), BF16)
    g_trif, g_trib, s_trif, s_trib = tri(GLA_C, False), tri(GLA_C, True), tri(SSD_C, False), tri(SSD_C, True)
    ef, eb = _expand_matrix(DT_F), _expand_matrix(DT_B)
    nb = _scan_batch(bsz)
    f = lambda n, lane=0: pl.BlockSpec((nb, TB, n), lambda b, s: (b, _fwd_blk(s), lane))
    r = lambda n, lane=0: pl.BlockSpec((nb, TB, n), lambda b, s: (b, _bwd_blk(s, ns), lane))
    const = lambda a: pl.BlockSpec(a.shape, lambda b, s: (0,) * a.ndim)
    out_f = lambda n: pl.BlockSpec((nb, TB, n), lambda b, s: (b, jnp.maximum(s - 1, 0), 0))
    out_b = lambda n: pl.BlockSpec((nb, TB, n), lambda b, s: (b, jnp.where(s == 0, nx - 1, nx - s), 0))
    gla_specs = [f(GLA_QK), f(GLA_QK), f(GLA_V), f(GLA_QK, 0), r(GLA_QK), r(GLA_QK), r(GLA_V), r(GLA_QK, 1),
                 const(g_trif), const(g_trib)]
    ssd_specs = [f(SSD_INNER), f(512), f(LANES), r(SSD_INNER), r(512), r(LANES),
                 const(a_f), const(a_b), const(dvec), const(s_trif), const(s_trib), const(ef), const(eb)]
    assert len(gla_specs) == N_GLA_IN and len(ssd_specs) == N_SSD_IN
    return pl.pallas_call(
        _scan_kernel,
        grid=(bsz // nb, ns),
        in_specs=gla_specs + ssd_specs,
        out_specs=[out_f(GLA_V), out_b(GLA_V), out_f(SSD_INNER), out_b(SSD_INNER)],
        out_shape=[jax.ShapeDtypeStruct((bsz, L, GLA_V), F32)] * 2 +
                  [jax.ShapeDtypeStruct((bsz, L, SSD_INNER), F32)] * 2,
        scratch_shapes=[pltpu.VMEM((nb, GLA_DV, GLA_QK), F32)] * 2 +
                       [pltpu.VMEM((nb, SSD_GROUPS, SSD_STATE, SSD_HPG * SSD_HEADDIM), F32)] * 2,
        compiler_params=_params("arbitrary", "arbitrary"),
    )(q, k, v, ld, q, k, v, ld, g_trif, g_trib, xs, bc, misc, xs, bc, misc, a_f, a_b, dvec, s_trif, s_trib, ef, eb)


def _outproj_kernel(of_ref, ob_ref, ga_ref, gb_ref, yf_ref, yb_ref, za_ref, zb_ref, x_ref, g1_ref, sh2_ref, sc2_ref,
                    gn_ref, sn_ref, wo_ref, n2_ref, wrh_ref, wrl_ref, br_ref,
                    x1_ref, h2_ref, eidx_ref, gate_ref, cnt_ref, mix_s, hl_s):
    first = jnp.logical_and(pl.program_id(0) == 0, pl.program_id(1) == 0)

    @pl.when(first)
    def _():
        cnt_ref[...] = jnp.zeros_like(cnt_ref)

    halves = [(pl.ds(0, TB), ga_ref, za_ref), (pl.ds(TB, TB), gb_ref, zb_ref)]

    for rows, g_ref, z_ref in halves:
        o = of_ref[0, rows, :] + ob_ref[0, rows, :]
        gg = g_ref[0]
        for h in range(GLA_HEADS):
            lo, hi = GLA_DV * h, GLA_DV * (h + 1)
            oh = o[:, lo:hi]
            ms = jnp.mean(oh * oh, axis=-1, keepdims=True)
            gh = gg[:, lo:hi]
            mix_s[rows, lo:hi] = (oh * lax.rsqrt(ms + EPS) * gn_ref[:, lo:hi] * (gh * _sigmoid(gh))).astype(BF16)
        zz = z_ref[0]
        u = (yf_ref[0, rows, :] + yb_ref[0, rows, :]) * (zz * _sigmoid(zz))
        gw = SSD_INNER // SSD_GROUPS
        for g in range(SSD_GROUPS):
            lo, hi = gw * g, gw * (g + 1)
            ug = u[:, lo:hi]
            ms = jnp.mean(ug * ug, axis=-1, keepdims=True)
            mix_s[rows, GLA_V + lo:GLA_V + hi] = (ug * lax.rsqrt(ms + EPS) * sn_ref[:, lo:hi]).astype(BF16)

    x1_ref[0] = x_ref[0] + g1_ref[0] * _dot(mix_s[...], wo_ref[...])

    for rows, _, _ in halves:
        x1 = x1_ref[0, rows, :]
        ms = jnp.mean(x1 * x1, axis=-1, keepdims=True)
        h2 = x1 * lax.rsqrt(ms + EPS) * (n2_ref[...] * (1.0 + sc2_ref[0])) + sh2_ref[0]
        h_hi = h2.astype(BF16)
        h2_ref[0, rows, :] = h_hi
        hl_s[rows, :] = (h2 - h_hi.astype(F32)).astype(BF16)

    h_hi = h2_ref[0]
    logits = (_dot(h_hi, wrh_ref[...]) + _dot(hl_s[...], wrh_ref[...]) + _dot(h_hi, wrl_ref[...])) + br_ref[...]

    lane = lax.broadcasted_iota(jnp.int32, (TB, LANES), 1).astype(F32)
    st = [dict(work=logits[i * TB:(i + 1) * TB, :], eidx=jnp.full((TB, LANES), -1.0, F32),
               gates=jnp.zeros((TB, LANES), F32), sel=jnp.zeros((TB, LANES), F32), m0=None,
               den=jnp.zeros((TB, 1), F32)) for i in range(2)]
    for kk in range(TOP_K):
        for t in st:
            t["m"] = jnp.max(t["work"], axis=-1, keepdims=True)
        for t in st:
            t["idx"] = jnp.min(jnp.where(t["work"] == t["m"], lane, float(LANES)), axis=-1, keepdims=True)
        for t in st:
            hit = lane == t["idx"]
            if t["m0"] is None:
                t["m0"] = t["m"]
            e = jnp.exp(t["m"] - t["m0"])
            t["den"] = t["den"] + e
            t["eidx"] = jnp.where(lane == float(kk), t["idx"], t["eidx"])
            t["gates"] = jnp.where(lane == float(kk), e, t["gates"])
            t["sel"] = jnp.where(hit, 1.0, t["sel"])
            t["work"] = jnp.where(hit, NEG_BIG, t["work"])
    for (rows, _, _), t in zip(halves, st):
        eidx_ref[0, rows, :] = t["eidx"].astype(jnp.int32)
        gate_ref[0, rows, :] = t["gates"] / t["den"]
        cnt = jnp.sum(t["sel"], axis=0, keepdims=True)
        cnt_ref[...] += jnp.floor((cnt + (RUN - 1.0)) * (1.0 / RUN)) * RUN


def _outproj_call(o_f, o_b, g_all, y_f, y_b, z_all, x, g1, sh2, sc2, gn, sn, wo, n2, wr_hi, wr_lo, br):
    bsz, L, d = x.shape
    nj = L // (2 * TB)
    tok = lambda n: pl.BlockSpec((1, 2 * TB, n), lambda b, j: (b, j, 0))
    tok_off = lambda n, half: pl.BlockSpec((1, TB, n), lambda b, j: (b, 2 * j + 1 + half, 0))
    const = lambda a: pl.BlockSpec(a.shape, lambda b, j: (0,) * a.ndim)
    mod = pl.BlockSpec((1, 1, d), lambda b, j: (b, 0, 0))
    return pl.pallas_call(
        _outproj_kernel,
        grid=(bsz, nj),
        in_specs=[tok(GLA_V), tok(GLA_V), tok_off(GLA_V, 0), tok_off(GLA_V, 1),
                  tok(SSD_INNER), tok(SSD_INNER), tok_off(SSD_INNER, 0), tok_off(SSD_INNER, 1),
                  tok(d), mod, mod, mod, const(gn), const(sn), const(wo), const(n2),
                  const(wr_hi), const(wr_lo), const(br)],
        out_specs=[tok(d), tok(d), tok(LANES), tok(LANES), pl.BlockSpec((1, LANES), lambda b, j: (0, 0))],
        out_shape=[jax.ShapeDtypeStruct((bsz, L, d), F32), jax.ShapeDtypeStruct((bsz, L, d), BF16),
                   jax.ShapeDtypeStruct((bsz, L, LANES), jnp.int32), jax.ShapeDtypeStruct((bsz, L, LANES), F32),
                   jax.ShapeDtypeStruct((1, LANES), F32)],
        scratch_shapes=[pltpu.VMEM((2 * TB, d), BF16), pltpu.VMEM((2 * TB, d), BF16)],
        compiler_params=_params("arbitrary", "arbitrary"),
    )(o_f, o_b, g_all, g_all, y_f, y_b, z_all, z_all, x, g1, sh2, sc2, gn, sn, wo, n2, wr_hi, wr_lo, br)


def _pos_kernel(eidx_ref, pstart_ref, lst_ref, ust_ref, lp_ref, cd_ref, carry):
    @pl.when(pl.program_id(0) == 0)
    def _():
        carry[...] = pstart_ref[...]

    lane = lax.broadcasted_iota(jnp.int32, (TB, LANES), 1)
    tiles = []
    for t in range(cd_ref.shape[0]):
        eidx = eidx_ref[t * TB:(t + 1) * TB, :]
        hits = [lane == eidx[:, kk:kk + 1] for kk in range(TOP_K)]
        sel = jnp.zeros((TB, LANES), F32)
        for hmask in hits:
            sel = jnp.where(hmask, 1.0, sel)
        cnt = jnp.sum(sel, axis=0, keepdims=True)
        run = jnp.floor((cnt + (RUN - 1.0)) * (1.0 / RUN)) * RUN
        tiles.append(dict(hits=hits, sel=sel, run=run))
    for tl in tiles:
        tl["rank"] = _dot(lst_ref[...], tl["sel"].astype(BF16))
        tl["loff"] = _dot(jnp.broadcast_to(tl["run"], (8, LANES)).astype(BF16), ust_ref[...])[0:1]
    for t, tl in enumerate(tiles):
        pos = tl["loff"] + tl["rank"]
        lp = jnp.zeros((TB, LANES), jnp.int32)
        for kk, hmask in enumerate(tl["hits"]):
            lk = jnp.sum(jnp.where(hmask, pos, 0.0), axis=-1, keepdims=True)
            lp = jnp.where(lane == kk, lk.astype(jnp.int32), lp)
        lp_ref[t * TB:(t + 1) * TB, :] = lp

    eye = lax.broadcasted_iota(jnp.int32, (LANES, LANES), 0) == lax.broadcasted_iota(jnp.int32, (LANES, LANES), 1)
    col = lambda v: jnp.sum(jnp.where(eye, v, 0.0), axis=1, keepdims=True)
    row0 = (lax.broadcasted_iota(jnp.int32, (LANES, CD_LANES), 1) * RUN).astype(F32)
    base = carry[...]
    for t, tl in enumerate(tiles):
        loff_c, run_c, shift_c = col(tl["loff"]), col(tl["run"]), col(base - tl["loff"])
        inside = jnp.where(row0 >= loff_c, 1.0, 0.0) * jnp.where(row0 < loff_c + run_c, 1.0, 0.0)
        valid = jnp.sum(inside, axis=0, keepdims=True)
        dest = jnp.sum(inside * shift_c, axis=0, keepdims=True) + row0[0:1]
        cd_ref[t] = jnp.where(valid > 0.0, dest, -1.0).astype(jnp.int32)
        base = base + tl["run"]
    carry[...] = base


def _pos_call(eidx, pstart):
    T = eidx.shape[0]
    lst = jnp.asarray(np.tril(np.ones((TB, TB), np.float32), -1), BF16)
    ust = jnp.asarray(np.triu(np.ones((LANES, LANES), np.float32), 1), BF16)
    tps = next(n for n in (4, 2, 1) if (T // TB) % n == 0)
    return pl.pallas_call(
        _pos_kernel,
        grid=(T // (tps * TB),),
        in_specs=[pl.BlockSpec((tps * TB, LANES), lambda i: (i, 0)),
                  pl.BlockSpec((1, LANES), lambda i: (0, 0)),
                  pl.BlockSpec((TB, TB), lambda i: (0, 0)),
                  pl.BlockSpec((LANES, LANES), lambda i: (0, 0))],
        out_specs=[pl.BlockSpec((tps * TB, LANES), lambda i: (i, 0)),
                   pl.BlockSpec((tps, 1, CD_LANES), lambda i: (i, 0, 0))],
        out_shape=[jax.ShapeDtypeStruct((T, LANES), jnp.int32),
                   jax.ShapeDtypeStruct((T // TB, 1, CD_LANES), jnp.int32)],
        scratch_shapes=[pltpu.VMEM((1, LANES), F32)],
        compiler_params=_params("arbitrary"),
    )(eidx, pstart, lst, ust)


def _pair_matrix(lp, weights):
    j = lax.broadcasted_iota(jnp.int32, (TB, RL), 1)
    m = jnp.zeros((TB, RL), F32)
    for kk in range(TOP_K):
        w = 1.0 if weights is None else weights[:, kk:kk + 1]
        m = jnp.where(j == lp[:, kk:kk + 1], w, m)
    return m


def _pack_bf16_pairs(v, is_bf16_valued=False):
    if not is_bf16_valued:
        v = v.astype(BF16).astype(F32)
    bits = pltpu.bitcast(v, jnp.uint32)
    return bits[:, D_HALF:] | (bits[:, :D_HALF] >> 16)


def _unpack_bf16_pairs(w):
    lo = pltpu.bitcast(w << 16, F32).astype(BF16)
    hi = pltpu.bitcast(w & jnp.uint32(0xFFFF0000), F32).astype(BF16)
    return lo, hi


N_SLOTS = 3
TILES_PER_STEP = 4


def _dispatch_kernel(fill_ref, *refs):
    cd_refs = refs[:TILES_PER_STEP + 1]
    h_ref, lp_ref, xg_ref, sorted_s, sems, fill_sem = refs[TILES_PER_STEP + 1:]
    step = pl.program_id(0)
    spill0 = xg_ref.shape[0] - N_SLOTS * RL

    def start(cref, s, c, to_spill, sem):
        src = RUN * c if isinstance(c, int) else pl.multiple_of(RUN * c, RUN)
        d = cref[0, 0, c]
        d = pl.multiple_of(jnp.where(jnp.logical_or(d < 0, to_spill), spill0 + s * RL + src, d), RUN)
        pltpu.make_async_copy(sorted_s.at[s, pl.ds(src, RUN), :], xg_ref.at[pl.ds(d, RUN), :], sem).start()

    def drain(s, sem):
        pltpu.make_async_copy(sorted_s.at[s], xg_ref.at[pl.ds(0, RL), :], sem).wait()

    @pl.when(step == 0)
    def _():
        sorted_s[...] = jnp.zeros_like(sorted_s)
        for s in range(N_SLOTS):
            lax.fori_loop(0, RL // RUN, lambda c, carry, s=s: (start(cd_refs[0], s, c, True, fill_sem), carry)[1], 0)
        for s in range(N_SLOTS):
            drain(s, fill_sem)

    for k in range(TILES_PER_STEP):
        t = TILES_PER_STEP * step + k
        rows = slice(k * TB, (k + 1) * TB)
        slot = lax.rem(t, N_SLOTS)
        prev = lax.rem(t + N_SLOTS - 1, N_SLOTS)
        pprev = lax.rem(t + N_SLOTS - 2, N_SLOTS)

        for c in range(RL // RUN):
            start(cd_refs[k], prev, c, t == 0, sems.at[prev])

        sorted_s[slot] = _pack_bf16_pairs(
            _dot_tn(_pair_matrix(lp_ref[rows, :], None).astype(BF16), h_ref[rows, :]), is_bf16_valued=True)

        if k == 0:
            @pl.when(step == 0)
            def _():
                def fill(j):
                    blk = pl.multiple_of(fill_ref[j] * MOE_BLOCK, MOE_BLOCK)
                    return pltpu.make_async_copy(sorted_s.at[0, pl.ds(0, MOE_BLOCK), :],
                                                 xg_ref.at[pl.ds(blk, MOE_BLOCK), :], fill_sem)

                for j in range(fill_ref.shape[0]):
                    pl.when(fill_ref[j] >= 0)(lambda j=j: fill(j).start())
                for j in range(fill_ref.shape[0]):
                    pl.when(fill_ref[j] >= 0)(lambda j=j: fill(j).wait())

        pl.when(t >= 1)(lambda pprev=pprev: drain(pprev, sems.at[pprev]))

    @pl.when(step == pl.num_programs(0) - 1)
    def _():
        lax.fori_loop(0, RL // RUN,
                      lambda c, carry: (start(cd_refs[TILES_PER_STEP], slot, c, False, sems.at[slot]), carry)[1], 0,
                      unroll=4)
        drain(prev, sems.at[prev])
        drain(slot, sems.at[slot])


def _dispatch_call(fill_blocks, cd, h2, lp, P):
    T, d = h2.shape
    tps = TILES_PER_STEP
    assert (T // TB) % tps == 0
    cd_spec = lambda k: pl.BlockSpec((1, 1, CD_LANES), lambda i, fb: (jnp.maximum(tps * i + k - 1, 0), 0, 0),
                                     memory_space=pltpu.SMEM)
    grid_spec = pltpu.PrefetchScalarGridSpec(
        num_scalar_prefetch=1,
        grid=(T // (tps * TB),),
        in_specs=[cd_spec(k) for k in range(tps + 1)] +
                 [pl.BlockSpec((tps * TB, d), lambda i, fb: (i, 0)),
                  pl.BlockSpec((tps * TB, LANES), lambda i, fb: (i, 0))],
        out_specs=pl.BlockSpec(memory_space=pl.ANY),
        scratch_shapes=[pltpu.VMEM((N_SLOTS, RL, D_HALF), jnp.uint32), pltpu.SemaphoreType.DMA((N_SLOTS,)),
                        pltpu.SemaphoreType.DMA(())],
    )
    return pl.pallas_call(
        _dispatch_kernel,
        grid_spec=grid_spec,
        out_shape=jax.ShapeDtypeStruct((P + N_SLOTS * RL, D_HALF), jnp.uint32),
        compiler_params=_params("arbitrary"),
    )(fill_blocks, *([cd] * (tps + 1)), h2, lp)


def _moe_kernel(nb, info_ref, x_hbm, wgu_hbm, bgu_ref, wd_hbm, bd_ref, y_hbm,
                wgu_f, wd_f, wgu_b, wd_b, act_s, xbuf, ybuf, zbuf, wsems, xsems, ysems, zsem):
    n_used = info_ref[0, nb]

    def rows(blk):
        return pl.ds(pl.multiple_of(blk * MOE_BLOCK, MOE_BLOCK), MOE_BLOCK)

    def x_copy(blk, slot):
        return pltpu.make_async_copy(x_hbm.at[rows(blk), :], xbuf.at[slot], xsems.at[slot])

    def y_copy(blk, slot):
        return pltpu.make_async_copy(ybuf.at[slot], y_hbm.at[rows(blk), :], ysems.at[slot])

    def z_copy(blk):
        return pltpu.make_async_copy(zbuf, y_hbm.at[rows(blk), :], zsem)

    def fetch(expert, slot):
        return (pltpu.make_async_copy(wgu_hbm.at[expert], wgu_f.at[slot], wsems.at[0, slot]),
                pltpu.make_async_copy(wd_hbm.at[expert], wd_f.at[slot], wsems.at[1, slot]))

    zbuf[...] = jnp.zeros_like(zbuf)
    lax.fori_loop(n_used, nb, lambda blk, c: (z_copy(blk).start(), c)[1], 0)

    for cp in fetch(info_ref[0, 0], 0):
        cp.start()
    x_copy(0, 0).start()

    def body(i, carry):
        slot = i & 1
        e = info_ref[0, i]

        @pl.when(info_ref[1, i] == 1)
        def _():
            wslot = info_ref[2, i]
            nxt = info_ref[3, i]
            cps = fetch(e, wslot)
            cps[0].wait()
            wgu_b[...] = wgu_f[wslot].astype(BF16)
            cps[1].wait()
            wd_b[...] = wd_f[wslot].astype(BF16)

            @pl.when(nxt >= 0)
            def _():
                for cp in fetch(nxt, 1 - wslot):
                    cp.start()

        x_copy(i, slot).wait()
        pl.when(i + 1 < n_used)(lambda: x_copy(i + 1, 1 - slot).start())
        pl.when(i >= 2)(lambda: y_copy(i - 2, slot).wait())

        bgu = bgu_ref[pl.ds(e, 1), :]
        xb = jnp.concatenate(_unpack_bf16_pairs(xbuf[slot]), axis=1)
        cw = 256
        for c in range(D_FF // cw):
            lo, hi = c * cw, (c + 1) * cw
            gate = _dot(xb, wgu_b[:, lo:hi]) + bgu[:, lo:hi]
            up = _dot(xb, wgu_b[:, D_FF + lo:D_FF + hi]) + bgu[:, D_FF + lo:D_FF + hi]
            gate = jnp.minimum(gate, SWIGLU_LIMIT)
            up = jnp.clip(up, -SWIGLU_LIMIT, SWIGLU_LIMIT)
            act_s[:, lo:hi] = ((up + 1.0) * (gate * _sigmoid(SWIGLU_ALPHA * gate))).astype(BF16)
        ybuf[slot] = _pack_bf16_pairs(_dot(act_s[...], wd_b[...]) + bd_ref[pl.ds(e, 1), :])
        y_copy(i, slot).start()
        return carry

    lax.fori_loop(0, n_used, body, 0)

    pl.when(n_used >= 2)(lambda: y_copy(n_used - 2, n_used & 1).wait())
    y_copy(n_used - 1, (n_used - 1) & 1).wait()
    lax.fori_loop(n_used, nb, lambda blk, c: (z_copy(blk).wait(), c)[1], 0)


def _moe_call(info, xg, wgu, bgu, wd, bd, nb):
    P = nb * MOE_BLOCK
    ne, d, f2 = wgu.shape
    blk_buf = pltpu.VMEM((2, MOE_BLOCK, D_HALF), jnp.uint32)
    grid_spec = pltpu.PrefetchScalarGridSpec(
        num_scalar_prefetch=1,
        grid=(1,),
        in_specs=[pl.BlockSpec(memory_space=pl.ANY),
                  pl.BlockSpec(memory_space=pl.ANY),
                  pl.BlockSpec((ne, f2), lambda i, info: (0, 0)),
                  pl.BlockSpec(memory_space=pl.ANY),
                  pl.BlockSpec((ne, d), lambda i, info: (0, 0))],
        out_specs=pl.BlockSpec(memory_space=pl.ANY),
        scratch_shapes=[pltpu.VMEM((2, d, f2), F32), pltpu.VMEM((2, f2 // 2, d), F32),
                        pltpu.VMEM((d, f2), BF16), pltpu.VMEM((f2 // 2, d), BF16),
                        pltpu.VMEM((MOE_BLOCK, f2 // 2), BF16), blk_buf, blk_buf,
                        pltpu.VMEM((MOE_BLOCK, D_HALF), jnp.uint32),
                        pltpu.SemaphoreType.DMA((2, 2)), pltpu.SemaphoreType.DMA((2,)),
                        pltpu.SemaphoreType.DMA((2,)), pltpu.SemaphoreType.DMA(())],
    )
    return pl.pallas_call(
        functools.partial(_moe_kernel, nb),
        grid_spec=grid_spec,
        out_shape=jax.ShapeDtypeStruct((P, D_HALF), jnp.uint32),
        compiler_params=_params("arbitrary"),
    )(info, xg, wgu, bgu, wd, bd)


def _combine_kernel(*refs):
    cd_refs = refs[:TILES_PER_STEP + 2]
    y_ref, lp_ref, x1_ref, gate_ref, g2_ref, fn_ref, o_ref, buf, sems = refs[TILES_PER_STEP + 2:]
    step = pl.program_id(0)

    def start(dref, s, c):
        d = pl.multiple_of(jnp.maximum(dref[0, 0, c], 0), RUN)
        dst = pl.multiple_of(RUN * c, RUN)
        pltpu.make_async_copy(y_ref.at[pl.ds(d, RUN), :], buf.at[s, pl.ds(dst, RUN), :], sems.at[s]).start()

    def drain(s):
        pltpu.make_async_copy(y_ref.at[pl.ds(0, RL), :], buf.at[s], sems.at[s]).wait()

    @pl.when(step == 0)
    def _():
        lax.fori_loop(0, RL // RUN, lambda c, carry: (start(cd_refs[0], 0, c), carry)[1], 0, unroll=4)
        lax.fori_loop(0, RL // RUN, lambda c, carry: (start(cd_refs[1], 1, c), carry)[1], 0, unroll=4)

    for k in range(TILES_PER_STEP):
        t = TILES_PER_STEP * step + k
        rows = slice(k * TB, (k + 1) * TB)
        slot = lax.rem(t, N_SLOTS)
        ahead = lax.rem(t + 2, N_SLOTS)
        drain(slot)

        for c in range(RL // RUN):
            start(cd_refs[k + 2], ahead, c)

        g = _pair_matrix(lp_ref[rows, :], gate_ref[rows, :]).astype(BF16)
        halves = [_dot(g, yb) for yb in _unpack_bf16_pairs(buf[slot])]
        xo = x1_ref[rows, :] + g2_ref[0] * jnp.concatenate(halves, axis=1)
        ms = jnp.mean(xo * xo, axis=-1, keepdims=True)
        o_ref[rows, :] = xo * lax.rsqrt(ms + EPS) * fn_ref[...]

    @pl.when(step == pl.num_programs(0) - 1)
    def _():
        drain(lax.rem(t + 1, N_SLOTS))
        drain(ahead)


def _combine_call(cd, y, lp, x1, gates, g2, fn):
    T, d = x1.shape
    tps = TILES_PER_STEP
    n = T // TB
    assert n % tps == 0 and (n // g2.shape[0]) % tps == 0
    steps_per_batch = n // g2.shape[0] // tps
    cd_spec = lambda k: pl.BlockSpec((1, 1, CD_LANES), lambda i: (jnp.minimum(tps * i + k, n - 1), 0, 0),
                                     memory_space=pltpu.SMEM)
    return pl.pallas_call(
        _combine_kernel,
        grid=(n // tps,),
        in_specs=[cd_spec(k) for k in range(tps + 2)] +
                 [pl.BlockSpec(memory_space=pl.ANY),
                  pl.BlockSpec((tps * TB, LANES), lambda i: (i, 0)),
                  pl.BlockSpec((tps * TB, d), lambda i: (i, 0)),
                  pl.BlockSpec((tps * TB, LANES), lambda i: (i, 0)),
                  pl.BlockSpec((1, 1, d), lambda i: (i // steps_per_batch, 0, 0)),
                  pl.BlockSpec((1, d), lambda i: (0, 0))],
        out_specs=pl.BlockSpec((tps * TB, d), lambda i: (i, 0)),
        out_shape=jax.ShapeDtypeStruct((T, d), F32),
        scratch_shapes=[pltpu.VMEM((N_SLOTS, RL, D_HALF), jnp.uint32), pltpu.SemaphoreType.DMA((N_SLOTS,))],
        compiler_params=_params("arbitrary"),
    )(*([cd] * (tps + 2)), y, lp, x1, gates, g2, fn)


def _layer(x, c, ctx, c_ctx, w_mod, b_mod, norm1, w_in, gla_w_gk_up, gla_b_gk, gla_norm,
           ssd_conv_w, ssd_conv_b, ssd_dt_bias, ssd_A_log, ssd_D, ssd_norm, w_out,
           norm2, w_router, b_router, w_gate_up, b_gate_up, w_down, b_down, final_norm):
    bsz, L, d = x.shape
    lc = ctx.shape[1]
    assert lc == TB and L % TB == 0 and TB % GRID_W == 0

    cin = jnp.zeros((8, d), F32).at[:bsz].set(c).at[bsz].set(c_ctx)
    mod = _mod_call(cin, w_mod, b_mod.reshape(1, -1))[:bsz + 1]
    sh1, sc1, g1, sh2, sc2, g2 = [m.reshape(bsz + 1, 1, d) for m in jnp.split(mod, 6, axis=-1)]

    o = np.cumsum((0, GLA_QK, GLA_QK, GLA_V, GLA_V, GLA_RANK, SSD_INNER, SSD_CONV_DIM, SSD_HEADS))
    wq, wk, wv, wg, wlow, wz, wx, wdt = [w_in[:, int(a):int(b)] for a, b in zip(o[:-1], o[1:])]
    w_misc = jnp.concatenate([wlow, wdt, wdt, jnp.zeros((d, LANES - GLA_RANK - 2 * SSD_HEADS), F32)], axis=1)
    w_cat = jnp.concatenate([wq, wk, wv, wg, wz, wx, w_misc], axis=1).astype(BF16)
    wup = jnp.zeros((LANES, 2 * GLA_QK), F32).at[:GLA_RANK].set(
        jnp.concatenate([gla_w_gk_up[0], gla_w_gk_up[1]], axis=1)).astype(BF16)
    bup = jnp.concatenate([gla_b_gk[0], gla_b_gk[1]]).reshape(1, -1)
    dtb = jnp.zeros((1, LANES), F32).at[0, DT_F:DT_F + SSD_HEADS].set(ssd_dt_bias[0]) \
                                    .at[0, DT_B:DT_B + SSD_HEADS].set(ssd_dt_bias[1])
    q, k, v, g_all, z_all, xbc, ld, misc = _inproj_call(
        x, ctx, sh1, sc1, norm1.reshape(1, d), w_cat, wup, bup, dtb)

    xs, bc = _conv_call(xbc, ssd_conv_w.reshape(9, SSD_CONV_DIM), ssd_conv_b.reshape(1, -1), lc + L)

    a_neg = -jnp.exp(ssd_A_log.astype(F32))
    a_f = jnp.zeros((1, LANES), F32).at[0, DT_F:DT_F + SSD_HEADS].set(a_neg[0])
    a_b = jnp.zeros((1, LANES), F32).at[0, DT_B:DT_B + SSD_HEADS].set(a_neg[1])
    dvec = jnp.repeat(ssd_D, SSD_HEADDIM).reshape(1, SSD_INNER)
    o_f, o_b, y_f, y_b = _scan_call(q, k, v, ld, xs, bc, misc, a_f, a_b, dvec, L)

    wr = jnp.zeros((d, LANES), F32).at[:, :N_EXPERTS].set(w_router)
    wr_hi = wr.astype(BF16)
    wr_lo = (wr - wr_hi.astype(F32)).astype(BF16)
    br = jnp.full((1, LANES), NEG_BIG, F32).at[0, :N_EXPERTS].set(b_router)
    x1, h2, eidx, gates, counts = _outproj_call(
        o_f, o_b, g_all, y_f, y_b, z_all, x, g1[:bsz], sh2[:bsz], sc2[:bsz],
        jnp.tile(gla_norm, GLA_HEADS).reshape(1, -1), ssd_norm.reshape(1, -1), w_out.astype(BF16),
        norm2.reshape(1, d), wr_hi, wr_lo, br)

    T = bsz * L
    cnt = counts[0, :N_EXPERTS].astype(jnp.int32)
    padded = ((cnt + MOE_BLOCK - 1) // MOE_BLOCK) * MOE_BLOCK
    pend = jnp.cumsum(padded)
    pstart = pend - padded
    max_rows = T * TOP_K + (T // TB) * N_EXPERTS * (RUN - 1)
    n_blocks = -(-max_rows // MOE_BLOCK) + N_EXPERTS
    blk_start = jnp.arange(n_blocks, dtype=jnp.int32) * MOE_BLOCK
    blk_e = jnp.minimum(jnp.sum(pend[None, :] <= blk_start[:, None], axis=1), N_EXPERTS - 1).astype(jnp.int32)
    n_used = (pend[-1] // MOE_BLOCK).astype(jnp.int32)
    blk_i = jnp.arange(n_blocks, dtype=jnp.int32)
    first = (blk_i < n_used) & ((blk_i == 0) | (blk_e != jnp.roll(blk_e, 1)))
    slot = (jnp.cumsum(first) - 1) & 1
    first_pos = jnp.where(first, blk_i, n_blocks)
    next_first = jnp.roll(lax.cummin(first_pos, reverse=True), -1).at[-1].set(n_blocks)
    nxt = jnp.where(next_first < n_blocks, blk_e[jnp.minimum(next_first, n_blocks - 1)], -1)
    col = lambda v, last: jnp.concatenate([v.astype(jnp.int32), jnp.asarray([last], jnp.int32)])
    blk_info = jnp.stack([col(blk_e, 0).at[-1].set(n_used), col(first, 0), col(slot, 0), col(nxt, -1)])
    n_tail = n_blocks - (T * TOP_K) // MOE_BLOCK
    tail = n_used + jnp.arange(n_tail, dtype=jnp.int32)
    fill_blocks = jnp.concatenate([
        jnp.where(padded > 0, pend // MOE_BLOCK - 1, -1),
        jnp.where(tail < n_blocks, tail, -1)]).astype(jnp.int32)
    pstart_row = jnp.zeros((1, LANES), F32).at[0, :N_EXPERTS].set(pstart.astype(F32))
    lp, cd = _pos_call(eidx.reshape(T, LANES), pstart_row)

    xg = _dispatch_call(fill_blocks, cd, h2.reshape(T, d), lp, n_blocks * MOE_BLOCK)
    y = _moe_call(blk_info, xg, w_gate_up, b_gate_up, w_down, b_down, n_blocks)
    out = _combine_call(cd, y, lp, x1.reshape(T, d), gates.reshape(T, LANES), g2[:bsz], final_norm.reshape(1, d))
    return out.reshape(bsz, L, d)


def kernel(x, c, ctx, c_ctx, w_mod, b_mod, norm1, w_in, gla_w_gk_up, gla_b_gk, gla_norm, ssd_conv_w, ssd_conv_b, ssd_dt_bias, ssd_A_log, ssd_D, ssd_norm, w_out, norm2, w_router, b_router, w_gate_up, b_gate_up, w_down, b_down, final_norm):
    assert w_mod.shape[0] == 1, "single-layer kernel"
    return _layer(x, c, ctx, c_ctx, w_mod[0], b_mod[0], norm1[0], w_in[0], gla_w_gk_up[0], gla_b_gk[0],
                  gla_norm[0], ssd_conv_w[0], ssd_conv_b[0], ssd_dt_bias[0], ssd_A_log[0], ssd_D[0],
                  ssd_norm[0], w_out[0], norm2[0], w_router[0], b_router[0], w_gate_up[0], b_gate_up[0],
                  w_down[0], b_down[0], final_norm)
```

```python
import functools

import numpy as np
import jax
import jax.numpy as jnp
from jax import lax
from jax.experimental import pallas as pl
from jax.experimental.pallas import tpu as pltpu

F32 = jnp.float32
BF16 = jnp.bfloat16

EPS = 1e-6
GRID_W = 64
GLA_HEADS = 4
GLA_DK = 64
GLA_DV = 128
GLA_QK = GLA_HEADS * GLA_DK
GLA_V = GLA_HEADS * GLA_DV
GLA_RANK = 16
GLA_GATE_NORM = 16.0
SSD_HEADDIM = 64
SSD_INNER = 512
SSD_HEADS = 8
SSD_GROUPS = 2
SSD_HPG = 4
SSD_STATE = 128
SSD_CONV_DIM = 1024
N_EXPERTS = 32
TOP_K = 4
D_FF = 1024
SWIGLU_LIMIT = 7.0
SWIGLU_ALPHA = 1.702
MOE_BLOCK = 1024

TB = 256
GLA_C = 64
GLA_STAGE_GROUP = 1
SSD_C = 128
SSD_STAGE_GROUP = 1
LANES = 128
EXP_CLAMP = 80.0
DT_F = 16
DT_B = 24
NEG_BIG = -1e30
RUN = 8
RL = -(-(TB * TOP_K + N_EXPERTS * (RUN - 1)) // LANES) * LANES
CD_LANES = -(-(RL // RUN) // LANES) * LANES
D_HALF = 512
BULK_DMA_PRIORITY = 1
VMEM_LIMIT = 56 * 1024 * 1024


def _dot(a, b):
    return jnp.dot(a, b, preferred_element_type=F32)


def _dot_nt(a, b):
    return lax.dot_general(a, b, (((1,), (1,)), ((), ())), preferred_element_type=F32)


def _dot_tn(a, b):
    return lax.dot_general(a, b, (((0,), (0,)), ((), ())), preferred_element_type=F32)


def _split3(a):
    hi = a.astype(BF16)
    r1 = a - hi.astype(F32)
    mid = r1.astype(BF16)
    lo = (r1 - mid.astype(F32)).astype(BF16)
    return hi, mid, lo


def _dot_exact_r(m, a):
    hi, mid, lo = _split3(a)
    return _dot(m, hi) + _dot(m, mid) + _dot(m, lo)


def _dot_hilo_r(m, a):
    hi = a.astype(BF16)
    lo = (a - hi.astype(F32)).astype(BF16)
    return _dot(m, hi) + _dot(m, lo)


def _dot_hilo_l(a, m2):
    hi = a.astype(BF16)
    lo = (a - hi.astype(F32)).astype(BF16)
    return _dot(jnp.concatenate([hi, lo], axis=1), m2)


def _sigmoid(x):
    return 1.0 / (1.0 + jnp.exp(-x))


def _softplus(x):
    return jnp.maximum(x, 0.0) + jnp.log1p(jnp.exp(-jnp.abs(x)))


def _params(*sem):
    return pltpu.CompilerParams(dimension_semantics=sem, vmem_limit_bytes=VMEM_LIMIT)


def _mod_kernel(c_ref, w_ref, b_ref, o_ref):
    c = c_ref[...]
    s = c * _sigmoid(c)
    s_hi = s.astype(BF16)
    s_lo = (s - s_hi.astype(F32)).astype(BF16)
    w = w_ref[...]
    w_hi = w.astype(BF16)
    w_lo = (w - w_hi.astype(F32)).astype(BF16)
    o_ref[...] = _dot(s_hi, w_hi) + _dot(s_lo, w_hi) + _dot(s_hi, w_lo) + b_ref[...]


def _mod_call(cin, w, b):
    rows, d = cin.shape
    n = w.shape[1]
    tn = 1536
    return pl.pallas_call(
        _mod_kernel,
        grid=(n // tn,),
        in_specs=[pl.BlockSpec((rows, d), lambda i: (0, 0)),
                  pl.BlockSpec((d, tn), lambda i: (0, i)),
                  pl.BlockSpec((1, tn), lambda i: (0, i))],
        out_specs=pl.BlockSpec((rows, tn), lambda i: (0, i)),
        out_shape=jax.ShapeDtypeStruct((rows, n), F32),
        compiler_params=_params("arbitrary"),
    )(cin, w, b)


_C_Q, _C_K, _C_V, _C_G, _C_Z, _C_X, _C_M, _C_END = 0, 256, 512, 1024, 1536, 2048, 3072, 3200


def _inproj_kernel(x0_ref, xa_ref, xb_ref, ctx_ref, shl_ref, scl_ref, shc_ref, scc_ref, n1_ref, w_ref, wup_ref,
                   bup_ref, dtb_ref, q_ref, k_ref, v_ref, g_ref, z_ref, xbc_ref, ld_ref, misc_ref, h_scr):
    j = pl.program_id(1)
    slot = j & 1

    def normmod(xv, sh_ref, sc_ref):
        ms = jnp.mean(xv * xv, axis=-1, keepdims=True)
        y = xv * lax.rsqrt(ms + EPS) * n1_ref[...]
        return (y * (1.0 + sc_ref[0]) + sh_ref[0]).astype(BF16)

    @pl.when(j == 0)
    def _():
        h_scr[0, 0:TB, :] = normmod(ctx_ref[0], shc_ref, scc_ref)
        h_scr[0, TB:2 * TB, :] = normmod(x0_ref[0], shl_ref, scl_ref)

    def mm(lo, hi):
        return _dot(h_scr[slot], w_ref[:, lo:hi])

    m = mm(_C_M, _C_END)
    zz = _dot(m.astype(BF16), wup_ref[...]) + bup_ref[...]
    ld_ref[0] = -_softplus(-zz) * (1.0 / GLA_GATE_NORM)
    misc_ref[0] = _softplus(m + dtb_ref[...])
    q_ref[0] = (mm(_C_Q, _C_K) * (GLA_DK ** -0.5)).astype(BF16)
    h_scr[1 - slot, 0:TB, :] = normmod(xa_ref[0], shl_ref, scl_ref)
    k_ref[0] = mm(_C_K, _C_V).astype(BF16)
    h_scr[1 - slot, TB:2 * TB, :] = normmod(xb_ref[0], shl_ref, scl_ref)
    v_ref[0] = mm(_C_V, _C_G).astype(BF16)
    g_ref[0] = mm(_C_G, _C_Z)
    z_ref[0] = mm(_C_Z, _C_X)
    xbc_ref[0] = mm(_C_X, _C_M)


def _inproj_call(x, ctx, sh1, sc1, n1, w_cat, wup, bup, dtb):
    bsz, L, d = x.shape
    nx = L // TB
    nj = -(-(nx + 1) // 2)
    tok = lambda n: pl.BlockSpec((1, 2 * TB, n), lambda b, j: (b, j, 0))
    const = lambda a: pl.BlockSpec(a.shape, lambda b, j: (0,) * a.ndim)
    xblk = lambda off: pl.BlockSpec((1, TB, d), lambda b, j: (b, jnp.minimum(2 * j + off, nx - 1), 0))
    first = pl.BlockSpec((1, TB, d), lambda b, j: (b, 0, 0))
    mod_lat = pl.BlockSpec((1, 1, d), lambda b, j: (b, 0, 0))
    mod_ctx = pl.BlockSpec((1, 1, d), lambda b, j: (bsz, 0, 0))
    outs = [(GLA_QK, BF16), (GLA_QK, BF16), (GLA_V, BF16), (GLA_V, F32), (SSD_INNER, F32),
            (SSD_CONV_DIM, F32), (2 * GLA_QK, F32), (LANES, F32)]
    return pl.pallas_call(
        _inproj_kernel,
        grid=(bsz, nj),
        in_specs=[first, xblk(1), xblk(2), first,
                  mod_lat, mod_lat, mod_ctx, mod_ctx,
                  const(n1), const(w_cat), const(wup), const(bup), const(dtb)],
        out_specs=[tok(n) for n, _ in outs],
        out_shape=[jax.ShapeDtypeStruct((bsz, nj * 2 * TB, n), dt) for n, dt in outs],
        scratch_shapes=[pltpu.VMEM((2, 2 * TB, d), BF16)],
        compiler_params=_params("arbitrary", "arbitrary"),
    )(x, x, x, ctx, sh1, sc1, sh1, sc1, n1, w_cat, wup, bup, dtb)


_EXT_PAD = 8
_EXT_BASE = _EXT_PAD + GRID_W
_EXT_ROWS = 2 * _EXT_PAD + 2 * GRID_W + TB


def _conv_kernel(prev_ref, cur_ref, next_ref, w_ref, b_ref, xs_ref, bc_ref, ext_s, xl_all, xr_all):
    for bb in range(cur_ref.shape[0]):
        _conv_block(prev_ref.at[bb], cur_ref.at[bb], next_ref.at[bb], w_ref, b_ref, xs_ref.at[bb], bc_ref.at[bb],
                    ext_s.at[bb], xl_all.at[bb], xr_all.at[bb])


def _conv_block(prev_ref, cur_ref, next_ref, w_ref, b_ref, xs_ref, bc_ref, ext, xl_s, xr_s):
    j = pl.program_id(1)
    nj = pl.num_programs(1)
    is_ctx = j == 0
    zpad = jnp.zeros((_EXT_PAD, SSD_CONV_DIM), F32)
    ext[0:_EXT_PAD, :] = zpad
    ext[_EXT_ROWS - _EXT_PAD:_EXT_ROWS, :] = zpad
    ext[_EXT_PAD:_EXT_BASE, :] = jnp.where(j >= 2, prev_ref[...], 0.0)
    ext[_EXT_BASE:_EXT_BASE + TB, :] = cur_ref[...]
    ext[_EXT_BASE + TB:_EXT_BASE + TB + GRID_W, :] = jnp.where(
        jnp.logical_and(j >= 1, j <= nj - 2), next_ref[...], 0.0)

    win = TB + 2 * GRID_W
    u = lax.broadcasted_iota(jnp.int32, (win, LANES), 0)
    pos = jnp.where(is_ctx, u - GRID_W, u & (GRID_W - 1))
    ok_l = pos >= 1
    ok_r = jnp.where(is_ctx, pos - (TB - GRID_W), pos) <= GRID_W - 2
    lat = jnp.where(is_ctx, 0.0, 1.0)
    side = {-1: xl_s, 0: None, 1: xr_s}

    for c in range(SSD_CONV_DIM // LANES):
        lo, hi = c * LANES, (c + 1) * LANES
        xl_s[:, lo:hi] = jnp.where(ok_l, ext[_EXT_PAD - 1:_EXT_PAD - 1 + win, lo:hi], 0.0)
        xr_s[:, lo:hi] = jnp.where(ok_r, ext[_EXT_PAD + 1:_EXT_PAD + 1 + win, lo:hi], 0.0)
        acc = jnp.zeros((TB, LANES), F32)
        for dr in (-1, 0, 1):
            for dc in (-1, 0, 1):
                if dc == 0:
                    start = _EXT_BASE + GRID_W * dr
                    tap = ext[start:start + TB, lo:hi]
                else:
                    start = GRID_W + GRID_W * dr
                    tap = side[dc][start:start + TB, lo:hi]
                wi = 3 * (dr + 1) + (dc + 1)
                wv = w_ref[wi:wi + 1, lo:hi]
                if dr != 0:
                    wv = wv * lat
                acc = acc + tap * wv
        y = acc + b_ref[:, lo:hi]
        y = y * _sigmoid(y)
        if c < SSD_INNER // LANES:
            xs_ref[:, lo:hi] = y
        else:
            bc_ref[:, lo - SSD_INNER:hi - SSD_INNER] = y.astype(BF16)


def _conv_call(xbc, w9, bias, ls):
    bsz, _, ch = xbc.shape
    nj = ls // TB
    rpb = TB // GRID_W
    nrow = ls // GRID_W
    nb = next(n for n in (4, 2, 1) if bsz % n == 0)
    return pl.pallas_call(
        _conv_kernel,
        grid=(bsz // nb, nj),
        in_specs=[pl.BlockSpec((nb, GRID_W, ch), lambda b, j: (b, jnp.maximum(rpb * j - 1, 0), 0)),
                  pl.BlockSpec((nb, TB, ch), lambda b, j: (b, j, 0)),
                  pl.BlockSpec((nb, GRID_W, ch), lambda b, j: (b, jnp.minimum(rpb * j + rpb, nrow - 1), 0)),
                  pl.BlockSpec((9, ch), lambda b, j: (0, 0)),
                  pl.BlockSpec((1, ch), lambda b, j: (0, 0))],
        out_specs=[pl.BlockSpec((nb, TB, SSD_INNER), lambda b, j: (b, j, 0)),
                   pl.BlockSpec((nb, TB, ch - SSD_INNER), lambda b, j: (b, j, 0))],
        out_shape=[jax.ShapeDtypeStruct((bsz, ls, SSD_INNER), F32),
                   jax.ShapeDtypeStruct((bsz, ls, ch - SSD_INNER), BF16)],
        scratch_shapes=[pltpu.VMEM((nb, _EXT_ROWS, ch), F32), pltpu.VMEM((nb, TB + 2 * GRID_W, ch), F32),
                        pltpu.VMEM((nb, TB + 2 * GRID_W, ch), F32)],
        compiler_params=_params("arbitrary", "arbitrary"),
    )(xbc, xbc, xbc, w9, bias)


def _fwd_blk(s):
    return s


def _bwd_blk(s, ns):
    return jnp.where(s == 0, 0, ns - s)


def _scan_batch(bsz):
    return 2 if bsz % 2 == 0 else 1


class _GlaChunk:
    def __init__(self, q, k, v, la, st_ref, tri_m, fwd, store):
        self.q, self.k, self.v, self.la, self.st_ref, self.tri_m, self.fwd, self.store = (
            q, k, v, la, st_ref, tri_m, fwd, store)

    def stage_sums(self):
        self.b = _dot_hilo_r(self.tri_m, self.la)

    def stage_factors(self):
        C = GLA_C
        b = self.b
        self.bt = b[C - 1:C, :] if self.fwd else b[0:1, :]
        r = 0.5 * self.bt
        self.er = jnp.exp(r)
        self.qt = (self.q.astype(F32) * jnp.exp(jnp.minimum(b - r, EXP_CLAMP))).astype(BF16)
        kt = (self.k.astype(F32) * jnp.exp(jnp.minimum(r - b, EXP_CLAMP))).astype(BF16)
        head_k = lax.broadcasted_iota(jnp.int32, (C, GLA_QK), 1) >> 6
        zero = jnp.zeros_like(kt)
        self.kh = [jnp.where(head_k == h, kt, zero) for h in range(GLA_HEADS)]
        self.qh = [jnp.where(head_k == h, self.qt, zero) for h in range(GLA_HEADS)]

    def stage_products(self):
        v = self.v
        kcat = jnp.concatenate(self.kh, axis=0)
        self.sc = _dot_nt(self.qt, kcat)
        vcat = jnp.concatenate([v[:, h * GLA_DV:(h + 1) * GLA_DV] for h in range(GLA_HEADS)], axis=0)
        self.u = _dot_tn(vcat, kcat) * self.er

    def stage_mask(self):
        C = GLA_C
        ii = lax.broadcasted_iota(jnp.int32, (C, GLA_HEADS * C), 0)
        jj = lax.broadcasted_iota(jnp.int32, (C, GLA_HEADS * C), 1) & (C - 1)
        causal = (jj <= ii) if self.fwd else (jj >= ii)
        self.p = jnp.where(causal, self.sc, 0.0).astype(BF16)
        v = self.v
        head_v = lax.broadcasted_iota(jnp.int32, (C, GLA_V), 1) >> 7
        self.vst = jnp.concatenate([jnp.where(head_v == h, v, jnp.zeros_like(v)) for h in range(GLA_HEADS)], axis=0)

    def stage_intra(self):
        self.o = _dot(self.p, self.vst)

    def stage_inter(self):
        st = self.st_ref[...]
        ster = (st * self.er).astype(BF16)
        self.st_ref[...] = st * jnp.exp(self.bt) + self.u
        res = _dot_nt(jnp.concatenate(self.qh, axis=0), ster)
        self.inter = jnp.concatenate([res[h * GLA_C:(h + 1) * GLA_C, :] for h in range(GLA_HEADS)], axis=1)

    def stage_out(self):
        self.store(self.o + self.inter)


def _store_to(ref, bb, sl):
    def store(val):
        ref[bb, sl, :] = val
    return store


def _gla_schedule(qf_ref, kf_ref, vf_ref, lf_ref, qb_ref, kb_ref, vb_ref, lb_ref, trif_ref, trib_ref,
                  of_ref, ob_ref, stf, stb):
    nsub = TB // GLA_C
    store_to = _store_to
    calls = []

    group = GLA_STAGE_GROUP
    for g0 in range(0, nsub, group):
        steps = []
        for i in range(g0, g0 + group):
            sf = pl.ds(i * GLA_C, GLA_C)
            sb = pl.ds((nsub - 1 - i) * GLA_C, GLA_C)
            chunks = []
            for bb in range(qf_ref.shape[0]):
                chunks.append(_GlaChunk(qf_ref[bb, sf, :], kf_ref[bb, sf, :], vf_ref[bb, sf, :], lf_ref[bb, sf, :],
                                        stf.at[bb], trif_ref[...], True, store_to(of_ref, bb, sf)))
                chunks.append(_GlaChunk(qb_ref[bb, sb, :], kb_ref[bb, sb, :], vb_ref[bb, sb, :], lb_ref[bb, sb, :],
                                        stb.at[bb], trib_ref[...], False, store_to(ob_ref, bb, sb)))
            steps.append(chunks)
        for stage in ("stage_sums", "stage_factors", "stage_products", "stage_mask", "stage_intra"):
            for chunks in steps:
                calls += [getattr(ch, stage) for ch in chunks]
        for chunks in steps:
            for stage in ("stage_inter", "stage_out"):
                calls += [getattr(ch, stage) for ch in chunks]
    return calls


class _SsdChunk:
    def __init__(self, xs, bc, dtm, avec, dvec, st_ref, tri_m, e_m, base, fwd, store):
        self.xs, self.bc, self.dtm, self.avec, self.dvec, self.st_ref = xs, bc, dtm, avec, dvec, st_ref
        self.tri_m, self.e_m, self.base, self.fwd, self.store = tri_m, e_m, base, fwd, store

    def stage_sums(self):
        self.dt_exp = _dot_hilo_l(self.dtm, self.e_m)
        self.acum = _dot_exact_r(self.tri_m, self.dtm * self.avec)

    def stage_expand(self):
        self.acum_exp = _dot_hilo_l(self.acum, self.e_m)
        self.acum_t = self.acum.T
        self.xdt = self.xs * self.dt_exp
        bc = self.bc
        self.bg = [bc[:, 128 * g:128 * (g + 1)] for g in range(SSD_GROUPS)]
        self.cg = [bc[:, 256 + 128 * g:256 + 128 * (g + 1)] for g in range(SSD_GROUPS)]
        self.cb = [_dot_nt(self.cg[g], self.bg[g]) for g in range(SSD_GROUPS)]

    def stage_decay(self):
        C = SSD_C
        ii = lax.broadcasted_iota(jnp.int32, (C, C), 0)
        jj = lax.broadcasted_iota(jnp.int32, (C, C), 1)
        tri = (jj <= ii) if self.fwd else (jj >= ii)
        self.ms = []
        for g in range(SSD_GROUPS):
            for rr in range(SSD_HPG):
                ln = self.base + SSD_HPG * g + rr
                diff = self.acum[:, ln:ln + 1] - self.acum_t[ln:ln + 1, :]
                seg = jnp.where(tri, jnp.exp(jnp.minimum(diff, 0.0)), 0.0)
                self.ms.append((self.cb[g] * seg).astype(BF16))
        ae = self.acum_exp
        self.al_exp = ae[C - 1:C, :] if self.fwd else ae[0:1, :]
        self.xw = (self.xdt * jnp.exp(self.al_exp - ae)).astype(BF16)
        xdt_b = self.xdt.astype(BF16)
        gw = SSD_HPG * SSD_HEADDIM
        head = lax.broadcasted_iota(jnp.int32, (C, gw), 1) >> 6
        zero = jnp.zeros((C, gw), BF16)
        self.xh = [jnp.where(head == h % SSD_HPG, xdt_b[:, gw * (h // SSD_HPG):gw * (h // SSD_HPG + 1)], zero)
                   for h in range(SSD_HEADS)]

    def stage_products(self):
        self.yg, self.ug = [], []
        for g in range(SSD_GROUPS):
            gl, gh = 256 * g, 256 * (g + 1)
            yg = _dot(self.ms[SSD_HPG * g], self.xh[SSD_HPG * g])
            for rr in range(1, SSD_HPG):
                yg = yg + _dot(self.ms[SSD_HPG * g + rr], self.xh[SSD_HPG * g + rr])
            self.yg.append(yg)
            self.ug.append(_dot_tn(self.bg[g], self.xw[:, gl:gh]))

    def stage_state(self):
        ys = []
        for g in range(SSD_GROUPS):
            gl, gh = 256 * g, 256 * (g + 1)
            sg = self.st_ref[g]
            yoff = _dot(self.cg[g], sg.astype(BF16)) * jnp.exp(self.acum_exp[:, gl:gh])
            self.st_ref[g] = sg * jnp.exp(self.al_exp[:, gl:gh]) + self.ug[g]
            ys.append(self.yg[g] + yoff)
        y = jnp.concatenate(ys, axis=1)
        if self.dvec is not None:
            y = y + self.dvec * self.xs
        self.store(y)


def _ssd_schedule(xf_ref, bcf_ref, mf_ref, xb_ref, bcb_ref, mb_ref, af_ref, ab_ref, d_ref,
                  trif_ref, trib_ref, ef_ref, eb_ref, yf_ref, yb_ref, stf, stb):
    nsub = TB // SSD_C
    store_to = _store_to
    calls = []
    steps = []
    for i in range(nsub):
        sf = pl.ds(i * SSD_C, SSD_C)
        sb = pl.ds((nsub - 1 - i) * SSD_C, SSD_C)
        chunks = []
        for bb in range(xf_ref.shape[0]):
            chunks.append(_SsdChunk(xf_ref[bb, sf, :], bcf_ref[bb, sf, :], mf_ref[bb, sf, :], af_ref[...],
                                    d_ref[...], stf.at[bb], trif_ref[...], ef_ref[...], DT_F, True,
                                    store_to(yf_ref, bb, sf)))
            chunks.append(_SsdChunk(xb_ref[bb, sb, :], bcb_ref[bb, sb, :], mb_ref[bb, sb, :], ab_ref[...],
                                    None, stb.at[bb], trib_ref[...], eb_ref[...], DT_B, False,
                                    store_to(yb_ref, bb, sb)))
        steps.append(chunks)
    for g0 in range(0, nsub, SSD_STAGE_GROUP):
        group = steps[g0:g0 + SSD_STAGE_GROUP]
        for stage in ("stage_sums", "stage_expand", "stage_decay", "stage_products"):
            for chunks in group:
                calls += [getattr(ch, stage) for ch in chunks]
        for chunks in group:
            calls += [ch.stage_state for ch in chunks]
    return calls


N_GLA_IN, N_SSD_IN = 10, 13


def _scan_kernel(*refs):
    gla_in, ssd_in = refs[:N_GLA_IN], refs[N_GLA_IN:N_GLA_IN + N_SSD_IN]
    of_ref, ob_ref, yf_ref, yb_ref, gstf, gstb, sstf, sstb = refs[N_GLA_IN + N_SSD_IN:]

    @pl.when(pl.program_id(1) == 0)
    def _():
        for st in (gstf, gstb, sstf, sstb):
            st[...] = jnp.zeros_like(st)

    gla = _gla_schedule(*gla_in, of_ref, ob_ref, gstf, gstb)
    ssd = _ssd_schedule(*ssd_in, yf_ref, yb_ref, sstf, sstb)
    merged = sorted([((n + 0.5) / len(gla), 0, n, c) for n, c in enumerate(gla)] +
                    [((n + 0.5) / len(ssd), 1, n, c) for n, c in enumerate(ssd)], key=lambda t: t[:3])
    for _, _, _, call in merged:
        call()


def _expand_matrix(base):
    e = np.zeros((LANES, SSD_INNER), np.float32)
    for h in range(SSD_HEADS):
        e[base + h, SSD_HEADDIM * h:SSD_HEADDIM * (h + 1)] = 1.0
    return jnp.asarray(np.concatenate([e, e], axis=0), BF16)


def _scan_call(q, k, v, ld, xs, bc, misc, a_f, a_b, dvec, L):
    bsz = q.shape[0]
    nx = L // TB
    ns = nx + 1
    tri = lambda c, up: jnp.asarray((np.triu if up else np.tril)(np.ones((c, c), np.float32)), BF16)
    g_trif, g_trib, s_trif, s_trib = tri(GLA_C, False), tri(GLA_C, True), tri(SSD_C, False), tri(SSD_C, True)
    ef, eb = _expand_matrix(DT_F), _expand_matrix(DT_B)
    nb = _scan_batch(bsz)
    f = lambda n, lane=0: pl.BlockSpec((nb, TB, n), lambda b, s: (b, _fwd_blk(s), lane))
    r = lambda n, lane=0: pl.BlockSpec((nb, TB, n), lambda b, s: (b, _bwd_blk(s, ns), lane))
    const = lambda a: pl.BlockSpec(a.shape, lambda b, s: (0,) * a.ndim)
    out_f = lambda n: pl.BlockSpec((nb, TB, n), lambda b, s: (b, jnp.maximum(s - 1, 0), 0))
    out_b = lambda n: pl.BlockSpec((nb, TB, n), lambda b, s: (b, jnp.where(s == 0, nx - 1, nx - s), 0))
    gla_specs = [f(GLA_QK), f(GLA_QK), f(GLA_V), f(GLA_QK, 0), r(GLA_QK), r(GLA_QK), r(GLA_V), r(GLA_QK, 1),
                 const(g_trif), const(g_trib)]
    ssd_specs = [f(SSD_INNER), f(512), f(LANES), r(SSD_INNER), r(512), r(LANES),
                 const(a_f), const(a_b), const(dvec), const(s_trif), const(s_trib), const(ef), const(eb)]
    assert len(gla_specs) == N_GLA_IN and len(ssd_specs) == N_SSD_IN
    return pl.pallas_call(
        _scan_kernel,
        grid=(bsz // nb, ns),
        in_specs=gla_specs + ssd_specs,
        out_specs=[out_f(GLA_V), out_b(GLA_V), out_f(SSD_INNER), out_b(SSD_INNER)],
        out_shape=[jax.ShapeDtypeStruct((bsz, L, GLA_V), F32)] * 2 +
                  [jax.ShapeDtypeStruct((bsz, L, SSD_INNER), F32)] * 2,
        scratch_shapes=[pltpu.VMEM((nb, GLA_DV, GLA_QK), F32)] * 2 +
                       [pltpu.VMEM((nb, SSD_GROUPS, SSD_STATE, SSD_HPG * SSD_HEADDIM), F32)] * 2,
        compiler_params=_params("arbitrary", "arbitrary"),
    )(q, k, v, ld, q, k, v, ld, g_trif, g_trib, xs, bc, misc, xs, bc, misc, a_f, a_b, dvec, s_trif, s_trib, ef, eb)


def _outproj_kernel(of_ref, ob_ref, ga_ref, gb_ref, yf_ref, yb_ref, za_ref, zb_ref, x_ref, g1_ref, sh2_ref, sc2_ref,
                    gn_ref, sn_ref, wo_ref, n2_ref, wrh_ref, wrl_ref, br_ref,
                    x1_ref, h2_ref, eidx_ref, gate_ref, cnt_ref, mix_s, hl_s):
    first = jnp.logical_and(pl.program_id(0) == 0, pl.program_id(1) == 0)

    @pl.when(first)
    def _():
        cnt_ref[...] = jnp.zeros_like(cnt_ref)

    halves = [(pl.ds(0, TB), ga_ref, za_ref), (pl.ds(TB, TB), gb_ref, zb_ref)]

    for rows, g_ref, z_ref in halves:
        o = of_ref[0, rows, :] + ob_ref[0, rows, :]
        gg = g_ref[0]
        for h in range(GLA_HEADS):
            lo, hi = GLA_DV * h, GLA_DV * (h + 1)
            oh = o[:, lo:hi]
            ms = jnp.mean(oh * oh, axis=-1, keepdims=True)
            gh = gg[:, lo:hi]
            mix_s[rows, lo:hi] = (oh * lax.rsqrt(ms + EPS) * gn_ref[:, lo:hi] * (gh * _sigmoid(gh))).astype(BF16)
        zz = z_ref[0]
        u = (yf_ref[0, rows, :] + yb_ref[0, rows, :]) * (zz * _sigmoid(zz))
        gw = SSD_INNER // SSD_GROUPS
        for g in range(SSD_GROUPS):
            lo, hi = gw * g, gw * (g + 1)
            ug = u[:, lo:hi]
            ms = jnp.mean(ug * ug, axis=-1, keepdims=True)
            mix_s[rows, GLA_V + lo:GLA_V + hi] = (ug * lax.rsqrt(ms + EPS) * sn_ref[:, lo:hi]).astype(BF16)

    x1_ref[0] = x_ref[0] + g1_ref[0] * _dot(mix_s[...], wo_ref[...])

    for rows, _, _ in halves:
        x1 = x1_ref[0, rows, :]
        ms = jnp.mean(x1 * x1, axis=-1, keepdims=True)
        h2 = x1 * lax.rsqrt(ms + EPS) * (n2_ref[...] * (1.0 + sc2_ref[0])) + sh2_ref[0]
        h_hi = h2.astype(BF16)
        h2_ref[0, rows, :] = h_hi
        hl_s[rows, :] = (h2 - h_hi.astype(F32)).astype(BF16)

    h_hi = h2_ref[0]
    logits = (_dot(h_hi, wrh_ref[...]) + _dot(hl_s[...], wrh_ref[...]) + _dot(h_hi, wrl_ref[...])) + br_ref[...]

    lane = lax.broadcasted_iota(jnp.int32, (TB, LANES), 1).astype(F32)
    st = [dict(work=logits[i * TB:(i + 1) * TB, :], eidx=jnp.full((TB, LANES), -1.0, F32),
               gates=jnp.zeros((TB, LANES), F32), sel=jnp.zeros((TB, LANES), F32), m0=None,
               den=jnp.zeros((TB, 1), F32)) for i in range(2)]
    for kk in range(TOP_K):
        for t in st:
            t["m"] = jnp.max(t["work"], axis=-1, keepdims=True)
        for t in st:
            t["idx"] = jnp.min(jnp.where(t["work"] == t["m"], lane, float(LANES)), axis=-1, keepdims=True)
        for t in st:
            hit = lane == t["idx"]
            if t["m0"] is None:
                t["m0"] = t["m"]
            e = jnp.exp(t["m"] - t["m0"])
            t["den"] = t["den"] + e
            t["eidx"] = jnp.where(lane == float(kk), t["idx"], t["eidx"])
            t["gates"] = jnp.where(lane == float(kk), e, t["gates"])
            t["sel"] = jnp.where(hit, 1.0, t["sel"])
            t["work"] = jnp.where(hit, NEG_BIG, t["work"])
    for (rows, _, _), t in zip(halves, st):
        eidx_ref[0, rows, :] = t["eidx"].astype(jnp.int32)
        gate_ref[0, rows, :] = t["gates"] / t["den"]
        cnt = jnp.sum(t["sel"], axis=0, keepdims=True)
        cnt_ref[...] += jnp.floor((cnt + (RUN - 1.0)) * (1.0 / RUN)) * RUN


def _outproj_call(o_f, o_b, g_all, y_f, y_b, z_all, x, g1, sh2, sc2, gn, sn, wo, n2, wr_hi, wr_lo, br):
    bsz, L, d = x.shape
    nj = L // (2 * TB)
    tok = lambda n: pl.BlockSpec((1, 2 * TB, n), lambda b, j: (b, j, 0))
    tok_off = lambda n, half: pl.BlockSpec((1, TB, n), lambda b, j: (b, 2 * j + 1 + half, 0))
    const = lambda a: pl.BlockSpec(a.shape, lambda b, j: (0,) * a.ndim)
    mod = pl.BlockSpec((1, 1, d), lambda b, j: (b, 0, 0))
    return pl.pallas_call(
        _outproj_kernel,
        grid=(bsz, nj),
        in_specs=[tok(GLA_V), tok(GLA_V), tok_off(GLA_V, 0), tok_off(GLA_V, 1),
                  tok(SSD_INNER), tok(SSD_INNER), tok_off(SSD_INNER, 0), tok_off(SSD_INNER, 1),
                  tok(d), mod, mod, mod, const(gn), const(sn), const(wo), const(n2),
                  const(wr_hi), const(wr_lo), const(br)],
        out_specs=[tok(d), tok(d), tok(LANES), tok(LANES), pl.BlockSpec((1, LANES), lambda b, j: (0, 0))],
        out_shape=[jax.ShapeDtypeStruct((bsz, L, d), F32), jax.ShapeDtypeStruct((bsz, L, d), BF16),
                   jax.ShapeDtypeStruct((bsz, L, LANES), jnp.int32), jax.ShapeDtypeStruct((bsz, L, LANES), F32),
                   jax.ShapeDtypeStruct((1, LANES), F32)],
        scratch_shapes=[pltpu.VMEM((2 * TB, d), BF16), pltpu.VMEM((2 * TB, d), BF16)],
        compiler_params=_params("arbitrary", "arbitrary"),
    )(o_f, o_b, g_all, g_all, y_f, y_b, z_all, z_all, x, g1, sh2, sc2, gn, sn, wo, n2, wr_hi, wr_lo, br)


def _pos_kernel(eidx_ref, pstart_ref, lst_ref, ust_ref, lp_ref, cd_ref, carry):
    @pl.when(pl.program_id(0) == 0)
    def _():
        carry[...] = pstart_ref[...]

    lane = lax.broadcasted_iota(jnp.int32, (TB, LANES), 1)
    tiles = []
    for t in range(cd_ref.shape[0]):
        eidx = eidx_ref[t * TB:(t + 1) * TB, :]
        hits = [lane == eidx[:, kk:kk + 1] for kk in range(TOP_K)]
        sel = jnp.zeros((TB, LANES), F32)
        for hmask in hits:
            sel = jnp.where(hmask, 1.0, sel)
        cnt = jnp.sum(sel, axis=0, keepdims=True)
        run = jnp.floor((cnt + (RUN - 1.0)) * (1.0 / RUN)) * RUN
        tiles.append(dict(hits=hits, sel=sel, run=run))
    for tl in tiles:
        tl["rank"] = _dot(lst_ref[...], tl["sel"].astype(BF16))
        tl["loff"] = _dot(jnp.broadcast_to(tl["run"], (8, LANES)).astype(BF16), ust_ref[...])[0:1]
    for t, tl in enumerate(tiles):
        pos = tl["loff"] + tl["rank"]
        lp = jnp.zeros((TB, LANES), jnp.int32)
        for kk, hmask in enumerate(tl["hits"]):
            lk = jnp.sum(jnp.where(hmask, pos, 0.0), axis=-1, keepdims=True)
            lp = jnp.where(lane == kk, lk.astype(jnp.int32), lp)
        lp_ref[t * TB:(t + 1) * TB, :] = lp

    eye = lax.broadcasted_iota(jnp.int32, (LANES, LANES), 0) == lax.broadcasted_iota(jnp.int32, (LANES, LANES), 1)
    col = lambda v: jnp.sum(jnp.where(eye, v, 0.0), axis=1, keepdims=True)
    row0 = (lax.broadcasted_iota(jnp.int32, (LANES, CD_LANES), 1) * RUN).astype(F32)
    base = carry[...]
    for t, tl in enumerate(tiles):
        loff_c, run_c, shift_c = col(tl["loff"]), col(tl["run"]), col(base - tl["loff"])
        inside = jnp.where(row0 >= loff_c, 1.0, 0.0) * jnp.where(row0 < loff_c + run_c, 1.0, 0.0)
        valid = jnp.sum(inside, axis=0, keepdims=True)
        dest = jnp.sum(inside * shift_c, axis=0, keepdims=True) + row0[0:1]
        cd_ref[t] = jnp.where(valid > 0.0, dest, -1.0).astype(jnp.int32)
        base = base + tl["run"]
    carry[...] = base


def _pos_call(eidx, pstart):
    T = eidx.shape[0]
    lst = jnp.asarray(np.tril(np.ones((TB, TB), np.float32), -1), BF16)
    ust = jnp.asarray(np.triu(np.ones((LANES, LANES), np.float32), 1), BF16)
    tps = next(n for n in (4, 2, 1) if (T // TB) % n == 0)
    return pl.pallas_call(
        _pos_kernel,
        grid=(T // (tps * TB),),
        in_specs=[pl.BlockSpec((tps * TB, LANES), lambda i: (i, 0)),
                  pl.BlockSpec((1, LANES), lambda i: (0, 0)),
                  pl.BlockSpec((TB, TB), lambda i: (0, 0)),
                  pl.BlockSpec((LANES, LANES), lambda i: (0, 0))],
        out_specs=[pl.BlockSpec((tps * TB, LANES), lambda i: (i, 0)),
                   pl.BlockSpec((tps, 1, CD_LANES), lambda i: (i, 0, 0))],
        out_shape=[jax.ShapeDtypeStruct((T, LANES), jnp.int32),
                   jax.ShapeDtypeStruct((T // TB, 1, CD_LANES), jnp.int32)],
        scratch_shapes=[pltpu.VMEM((1, LANES), F32)],
        compiler_params=_params("arbitrary"),
    )(eidx, pstart, lst, ust)


def _pair_matrix(lp, weights):
    j = lax.broadcasted_iota(jnp.int32, (TB, RL), 1)
    m = jnp.zeros((TB, RL), F32)
    for kk in range(TOP_K):
        w = 1.0 if weights is None else weights[:, kk:kk + 1]
        m = jnp.where(j == lp[:, kk:kk + 1], w, m)
    return m


def _pack_bf16_pairs(v, is_bf16_valued=False):
    if not is_bf16_valued:
        v = v.astype(BF16).astype(F32)
    bits = pltpu.bitcast(v, jnp.uint32)
    return bits[:, D_HALF:] | (bits[:, :D_HALF] >> 16)


def _unpack_bf16_pairs(w):
    lo = pltpu.bitcast(w << 16, F32).astype(BF16)
    hi = pltpu.bitcast(w & jnp.uint32(0xFFFF0000), F32).astype(BF16)
    return lo, hi


N_SLOTS = 3
TILES_PER_STEP = 2


def _dispatch_kernel(fill_ref, *refs):
    cd_refs = refs[:TILES_PER_STEP + 1]
    h_ref, lp_ref, xg_ref, sorted_s, sems, fill_sem = refs[TILES_PER_STEP + 1:]
    step = pl.program_id(0)
    spill0 = xg_ref.shape[0] - N_SLOTS * RL

    def start(cref, s, c, to_spill, sem):
        src = RUN * c if isinstance(c, int) else pl.multiple_of(RUN * c, RUN)
        d = cref[0, 0, c]
        d = pl.multiple_of(jnp.where(jnp.logical_or(d < 0, to_spill), spill0 + s * RL + src, d), RUN)
        pltpu.make_async_copy(sorted_s.at[s, pl.ds(src, RUN), :], xg_ref.at[pl.ds(d, RUN), :], sem).start()

    def drain(s, sem):
        pltpu.make_async_copy(sorted_s.at[s], xg_ref.at[pl.ds(0, RL), :], sem).wait()

    @pl.when(step == 0)
    def _():
        sorted_s[...] = jnp.zeros_like(sorted_s)
        for s in range(N_SLOTS):
            lax.fori_loop(0, RL // RUN, lambda c, carry, s=s: (start(cd_refs[0], s, c, True, fill_sem), carry)[1], 0)
        for s in range(N_SLOTS):
            drain(s, fill_sem)

    for k in range(TILES_PER_STEP):
        t = TILES_PER_STEP * step + k
        rows = slice(k * TB, (k + 1) * TB)
        slot = lax.rem(t, N_SLOTS)
        prev = lax.rem(t + N_SLOTS - 1, N_SLOTS)
        pprev = lax.rem(t + N_SLOTS - 2, N_SLOTS)

        for c in range(RL // RUN):
            start(cd_refs[k], prev, c, t == 0, sems.at[prev])

        sorted_s[slot] = _pack_bf16_pairs(
            _dot_tn(_pair_matrix(lp_ref[rows, :], None).astype(BF16), h_ref[rows, :]), is_bf16_valued=True)

        if k == 0:
            @pl.when(step == 0)
            def _():
                def fill(j):
                    blk = pl.multiple_of(fill_ref[j] * MOE_BLOCK, MOE_BLOCK)
                    return pltpu.make_async_copy(sorted_s.at[0, pl.ds(0, MOE_BLOCK), :],
                                                 xg_ref.at[pl.ds(blk, MOE_BLOCK), :], fill_sem)

                for j in range(fill_ref.shape[0]):
                    pl.when(fill_ref[j] >= 0)(lambda j=j: fill(j).start())
                for j in range(fill_ref.shape[0]):
                    pl.when(fill_ref[j] >= 0)(lambda j=j: fill(j).wait())

        pl.when(t >= 1)(lambda pprev=pprev: drain(pprev, sems.at[pprev]))

    @pl.when(step == pl.num_programs(0) - 1)
    def _():
        lax.fori_loop(0, RL // RUN,
                      lambda c, carry: (start(cd_refs[TILES_PER_STEP], slot, c, False, sems.at[slot]), carry)[1], 0,
                      unroll=4)
        drain(prev, sems.at[prev])
        drain(slot, sems.at[slot])


def _dispatch_call(fill_blocks, cd, h2, lp, P):
    T, d = h2.shape
    tps = TILES_PER_STEP
    assert (T // TB) % tps == 0
    cd_spec = lambda k: pl.BlockSpec((1, 1, CD_LANES), lambda i, fb: (jnp.maximum(tps * i + k - 1, 0), 0, 0),
                                     memory_space=pltpu.SMEM)
    grid_spec = pltpu.PrefetchScalarGridSpec(
        num_scalar_prefetch=1,
        grid=(T // (tps * TB),),
        in_specs=[cd_spec(k) for k in range(tps + 1)] +
                 [pl.BlockSpec((tps * TB, d), lambda i, fb: (i, 0)),
                  pl.BlockSpec((tps * TB, LANES), lambda i, fb: (i, 0))],
        out_specs=pl.BlockSpec(memory_space=pl.ANY),
        scratch_shapes=[pltpu.VMEM((N_SLOTS, RL, D_HALF), jnp.uint32), pltpu.SemaphoreType.DMA((N_SLOTS,)),
                        pltpu.SemaphoreType.DMA(())],
    )
    return pl.pallas_call(
        _dispatch_kernel,
        grid_spec=grid_spec,
        out_shape=jax.ShapeDtypeStruct((P + N_SLOTS * RL, D_HALF), jnp.uint32),
        compiler_params=_params("arbitrary"),
    )(fill_blocks, *([cd] * (tps + 1)), h2, lp)


def _moe_kernel(nb, info_ref, x_hbm, wgu_hbm, bgu_ref, wd_hbm, bd_ref, y_hbm,
                wgu_f, wd_f, wgu_b, wd_b, act_s, xbuf, ybuf, zbuf, wsems, xsems, ysems, zsem):
    n_used = info_ref[0, nb]

    def rows(blk):
        return pl.ds(pl.multiple_of(blk * MOE_BLOCK, MOE_BLOCK), MOE_BLOCK)

    def x_copy(blk, slot):
        return pltpu.make_async_copy(x_hbm.at[rows(blk), :], xbuf.at[slot], xsems.at[slot])

    def y_copy(blk, slot):
        return pltpu.make_async_copy(ybuf.at[slot], y_hbm.at[rows(blk), :], ysems.at[slot])

    def z_copy(blk):
        return pltpu.make_async_copy(zbuf, y_hbm.at[rows(blk), :], zsem)

    def fetch(expert, slot):
        return (pltpu.make_async_copy(wgu_hbm.at[expert], wgu_f.at[slot], wsems.at[0, slot]),
                pltpu.make_async_copy(wd_hbm.at[expert], wd_f.at[slot], wsems.at[1, slot]))

    zbuf[...] = jnp.zeros_like(zbuf)
    lax.fori_loop(n_used, nb, lambda blk, c: (z_copy(blk).start(), c)[1], 0)

    for cp in fetch(info_ref[0, 0], 0):
        cp.start(priority=BULK_DMA_PRIORITY)
    x_copy(0, 0).start()

    def body(i, carry):
        slot = i & 1
        e = info_ref[0, i]

        @pl.when(info_ref[1, i] == 1)
        def _():
            wslot = info_ref[2, i]
            nxt = info_ref[3, i]
            cps = fetch(e, wslot)
            cps[0].wait()
            wgu_b[...] = wgu_f[wslot].astype(BF16)
            cps[1].wait()
            wd_b[...] = wd_f[wslot].astype(BF16)

            @pl.when(nxt >= 0)
            def _():
                for cp in fetch(nxt, 1 - wslot):
                    cp.start(priority=BULK_DMA_PRIORITY)

        x_copy(i, slot).wait()
        pl.when(i + 1 < n_used)(lambda: x_copy(i + 1, 1 - slot).start())
        pl.when(i >= 2)(lambda: y_copy(i - 2, slot).wait())

        bgu = bgu_ref[pl.ds(e, 1), :]
        xb = jnp.concatenate(_unpack_bf16_pairs(xbuf[slot]), axis=1)
        cw = 256
        for c in range(D_FF // cw):
            lo, hi = c * cw, (c + 1) * cw
            gate = _dot(xb, wgu_b[:, lo:hi]) + bgu[:, lo:hi]
            up = _dot(xb, wgu_b[:, D_FF + lo:D_FF + hi]) + bgu[:, D_FF + lo:D_FF + hi]
            gate = jnp.minimum(gate, SWIGLU_LIMIT)
            up = jnp.clip(up, -SWIGLU_LIMIT, SWIGLU_LIMIT)
            act_s[:, lo:hi] = ((up + 1.0) * (gate * _sigmoid(SWIGLU_ALPHA * gate))).astype(BF16)
        ybuf[slot] = _pack_bf16_pairs(_dot(act_s[...], wd_b[...]) + bd_ref[pl.ds(e, 1), :])
        y_copy(i, slot).start()
        return carry

    lax.fori_loop(0, n_used, body, 0)

    pl.when(n_used >= 2)(lambda: y_copy(n_used - 2, n_used & 1).wait())
    y_copy(n_used - 1, (n_used - 1) & 1).wait()
    lax.fori_loop(n_used, nb, lambda blk, c: (z_copy(blk).wait(), c)[1], 0)


def _moe_call(info, xg, wgu, bgu, wd, bd, nb):
    P = nb * MOE_BLOCK
    ne, d, f2 = wgu.shape
    blk_buf = pltpu.VMEM((2, MOE_BLOCK, D_HALF), jnp.uint32)
    grid_spec = pltpu.PrefetchScalarGridSpec(
        num_scalar_prefetch=1,
        grid=(1,),
        in_specs=[pl.BlockSpec(memory_space=pl.ANY),
                  pl.BlockSpec(memory_space=pl.ANY),
                  pl.BlockSpec((ne, f2), lambda i, info: (0, 0)),
                  pl.BlockSpec(memory_space=pl.ANY),
                  pl.BlockSpec((ne, d), lambda i, info: (0, 0))],
        out_specs=pl.BlockSpec(memory_space=pl.ANY),
        scratch_shapes=[pltpu.VMEM((2, d, f2), F32), pltpu.VMEM((2, f2 // 2, d), F32),
                        pltpu.VMEM((d, f2), BF16), pltpu.VMEM((f2 // 2, d), BF16),
                        pltpu.VMEM((MOE_BLOCK, f2 // 2), BF16), blk_buf, blk_buf,
                        pltpu.VMEM((MOE_BLOCK, D_HALF), jnp.uint32),
                        pltpu.SemaphoreType.DMA((2, 2)), pltpu.SemaphoreType.DMA((2,)),
                        pltpu.SemaphoreType.DMA((2,)), pltpu.SemaphoreType.DMA(())],
    )
    return pl.pallas_call(
        functools.partial(_moe_kernel, nb),
        grid_spec=grid_spec,
        out_shape=jax.ShapeDtypeStruct((P, D_HALF), jnp.uint32),
        compiler_params=_params("arbitrary"),
    )(info, xg, wgu, bgu, wd, bd)


def _combine_kernel(*refs):
    cd_refs = refs[:TILES_PER_STEP + 2]
    y_ref, lp_ref, x1_ref, gate_ref, g2_ref, fn_ref, o_ref, buf, sems = refs[TILES_PER_STEP + 2:]
    step = pl.program_id(0)

    def start(dref, s, c):
        d = pl.multiple_of(jnp.maximum(dref[0, 0, c], 0), RUN)
        dst = pl.multiple_of(RUN * c, RUN)
        pltpu.make_async_copy(y_ref.at[pl.ds(d, RUN), :], buf.at[s, pl.ds(dst, RUN), :], sems.at[s]).start()

    def drain(s):
        pltpu.make_async_copy(y_ref.at[pl.ds(0, RL), :], buf.at[s], sems.at[s]).wait()

    @pl.when(step == 0)
    def _():
        lax.fori_loop(0, RL // RUN, lambda c, carry: (start(cd_refs[0], 0, c), carry)[1], 0, unroll=4)
        lax.fori_loop(0, RL // RUN, lambda c, carry: (start(cd_refs[1], 1, c), carry)[1], 0, unroll=4)

    for k in range(TILES_PER_STEP):
        t = TILES_PER_STEP * step + k
        rows = slice(k * TB, (k + 1) * TB)
        slot = lax.rem(t, N_SLOTS)
        ahead = lax.rem(t + 2, N_SLOTS)
        drain(slot)

        for c in range(RL // RUN):
            start(cd_refs[k + 2], ahead, c)

        g = _pair_matrix(lp_ref[rows, :], gate_ref[rows, :]).astype(BF16)
        halves = [_dot(g, yb) for yb in _unpack_bf16_pairs(buf[slot])]
        xo = x1_ref[rows, :] + g2_ref[0] * jnp.concatenate(halves, axis=1)
        ms = jnp.mean(xo * xo, axis=-1, keepdims=True)
        o_ref[rows, :] = xo * lax.rsqrt(ms + EPS) * fn_ref[...]

    @pl.when(step == pl.num_programs(0) - 1)
    def _():
        drain(lax.rem(t + 1, N_SLOTS))
        drain(ahead)


def _combine_call(cd, y, lp, x1, gates, g2, fn):
    T, d = x1.shape
    tps = TILES_PER_STEP
    n = T // TB
    assert n % tps == 0 and (n // g2.shape[0]) % tps == 0
    steps_per_batch = n // g2.shape[0] // tps
    cd_spec = lambda k: pl.BlockSpec((1, 1, CD_LANES), lambda i: (jnp.minimum(tps * i + k, n - 1), 0, 0),
                                     memory_space=pltpu.SMEM)
    return pl.pallas_call(
        _combine_kernel,
        grid=(n // tps,),
        in_specs=[cd_spec(k) for k in range(tps + 2)] +
                 [pl.BlockSpec(memory_space=pl.ANY),
                  pl.BlockSpec((tps * TB, LANES), lambda i: (i, 0)),
                  pl.BlockSpec((tps * TB, d), lambda i: (i, 0)),
                  pl.BlockSpec((tps * TB, LANES), lambda i: (i, 0)),
                  pl.BlockSpec((1, 1, d), lambda i: (i // steps_per_batch, 0, 0)),
                  pl.BlockSpec((1, d), lambda i: (0, 0))],
        out_specs=pl.BlockSpec((tps * TB, d), lambda i: (i, 0)),
        out_shape=jax.ShapeDtypeStruct((T, d), F32),
        scratch_shapes=[pltpu.VMEM((N_SLOTS, RL, D_HALF), jnp.uint32), pltpu.SemaphoreType.DMA((N_SLOTS,))],
        compiler_params=_params("arbitrary"),
    )(*([cd] * (tps + 2)), y, lp, x1, gates, g2, fn)


def _layer(x, c, ctx, c_ctx, w_mod, b_mod, norm1, w_in, gla_w_gk_up, gla_b_gk, gla_norm,
           ssd_conv_w, ssd_conv_b, ssd_dt_bias, ssd_A_log, ssd_D, ssd_norm, w_out,
           norm2, w_router, b_router, w_gate_up, b_gate_up, w_down, b_down, final_norm):
    bsz, L, d = x.shape
    lc = ctx.shape[1]
    assert lc == TB and L % TB == 0 and TB % GRID_W == 0

    cin = jnp.zeros((8, d), F32).at[:bsz].set(c).at[bsz].set(c_ctx)
    mod = _mod_call(cin, w_mod, b_mod.reshape(1, -1))[:bsz + 1]
    sh1, sc1, g1, sh2, sc2, g2 = [m.reshape(bsz + 1, 1, d) for m in jnp.split(mod, 6, axis=-1)]

    o = np.cumsum((0, GLA_QK, GLA_QK, GLA_V, GLA_V, GLA_RANK, SSD_INNER, SSD_CONV_DIM, SSD_HEADS))
    wq, wk, wv, wg, wlow, wz, wx, wdt = [w_in[:, int(a):int(b)] for a, b in zip(o[:-1], o[1:])]
    w_misc = jnp.concatenate([wlow, wdt, wdt, jnp.zeros((d, LANES - GLA_RANK - 2 * SSD_HEADS), F32)], axis=1)
    w_cat = jnp.concatenate([wq, wk, wv, wg, wz, wx, w_misc], axis=1).astype(BF16)
    wup = jnp.zeros((LANES, 2 * GLA_QK), F32).at[:GLA_RANK].set(
        jnp.concatenate([gla_w_gk_up[0], gla_w_gk_up[1]], axis=1)).astype(BF16)
    bup = jnp.concatenate([gla_b_gk[0], gla_b_gk[1]]).reshape(1, -1)
    dtb = jnp.zeros((1, LANES), F32).at[0, DT_F:DT_F + SSD_HEADS].set(ssd_dt_bias[0]) \
                                    .at[0, DT_B:DT_B + SSD_HEADS].set(ssd_dt_bias[1])
    q, k, v, g_all, z_all, xbc, ld, misc = _inproj_call(
        x, ctx, sh1, sc1, norm1.reshape(1, d), w_cat, wup, bup, dtb)

    xs, bc = _conv_call(xbc, ssd_conv_w.reshape(9, SSD_CONV_DIM), ssd_conv_b.reshape(1, -1), lc + L)

    a_neg = -jnp.exp(ssd_A_log.astype(F32))
    a_f = jnp.zeros((1, LANES), F32).at[0, DT_F:DT_F + SSD_HEADS].set(a_neg[0])
    a_b = jnp.zeros((1, LANES), F32).at[0, DT_B:DT_B + SSD_HEADS].set(a_neg[1])
    dvec = jnp.repeat(ssd_D, SSD_HEADDIM).reshape(1, SSD_INNER)
    o_f, o_b, y_f, y_b = _scan_call(q, k, v, ld, xs, bc, misc, a_f, a_b, dvec, L)

    wr = jnp.zeros((d, LANES), F32).at[:, :N_EXPERTS].set(w_router)
    wr_hi = wr.astype(BF16)
    wr_lo = (wr - wr_hi.astype(F32)).astype(BF16)
    br = jnp.full((1, LANES), NEG_BIG, F32).at[0, :N_EXPERTS].set(b_router)
    x1, h2, eidx, gates, counts = _outproj_call(
        o_f, o_b, g_all, y_f, y_b, z_all, x, g1[:bsz], sh2[:bsz], sc2[:bsz],
        jnp.tile(gla_norm, GLA_HEADS).reshape(1, -1), ssd_norm.reshape(1, -1), w_out.astype(BF16),
        norm2.reshape(1, d), wr_hi, wr_lo, br)

    T = bsz * L
    cnt = counts[0, :N_EXPERTS].astype(jnp.int32)
    padded = ((cnt + MOE_BLOCK - 1) // MOE_BLOCK) * MOE_BLOCK
    pend = jnp.cumsum(padded)
    pstart = pend - padded
    max_rows = T * TOP_K + (T // TB) * N_EXPERTS * (RUN - 1)
    n_blocks = -(-max_rows // MOE_BLOCK) + N_EXPERTS
    blk_start = jnp.arange(n_blocks, dtype=jnp.int32) * MOE_BLOCK
    blk_e = jnp.minimum(jnp.sum(pend[None, :] <= blk_start[:, None], axis=1), N_EXPERTS - 1).astype(jnp.int32)
    n_used = (pend[-1] // MOE_BLOCK).astype(jnp.int32)
    blk_i = jnp.arange(n_blocks, dtype=jnp.int32)
    first = (blk_i < n_used) & ((blk_i == 0) | (blk_e != jnp.roll(blk_e, 1)))
    slot = (jnp.cumsum(first) - 1) & 1
    first_pos = jnp.where(first, blk_i, n_blocks)
    next_first = jnp.roll(lax.cummin(first_pos, reverse=True), -1).at[-1].set(n_blocks)
    nxt = jnp.where(next_first < n_blocks, blk_e[jnp.minimum(next_first, n_blocks - 1)], -1)
    col = lambda v, last: jnp.concatenate([v.astype(jnp.int32), jnp.asarray([last], jnp.int32)])
    blk_info = jnp.stack([col(blk_e, 0).at[-1].set(n_used), col(first, 0), col(slot, 0), col(nxt, -1)])
    n_tail = n_blocks - (T * TOP_K) // MOE_BLOCK
    tail = n_used + jnp.arange(n_tail, dtype=jnp.int32)
    fill_blocks = jnp.concatenate([
        jnp.where(padded > 0, pend // MOE_BLOCK - 1, -1),
        jnp.where(tail < n_blocks, tail, -1)]).astype(jnp.int32)
    pstart_row = jnp.zeros((1, LANES), F32).at[0, :N_EXPERTS].set(pstart.astype(F32))
    lp, cd = _pos_call(eidx.reshape(T, LANES), pstart_row)

    xg = _dispatch_call(fill_blocks, cd, h2.reshape(T, d), lp, n_blocks * MOE_BLOCK)
    y = _moe_call(blk_info, xg, w_gate_up, b_gate_up, w_down, b_down, n_blocks)
    out = _combine_call(cd, y, lp, x1.reshape(T, d), gates.reshape(T, LANES), g2[:bsz], final_norm.reshape(1, d))
    return out.reshape(bsz, L, d)


def kernel(x, c, ctx, c_ctx, w_mod, b_mod, norm1, w_in, gla_w_gk_up, gla_b_gk, gla_norm, ssd_conv_w, ssd_conv_b, ssd_dt_bias, ssd_A_log, ssd_D, ssd_norm, w_out, norm2, w_router, b_router, w_gate_up, b_gate_up, w_down, b_down, final_norm):
    assert w_mod.shape[0] == 1, "single-layer kernel"
    return _layer(x, c, ctx, c_ctx, w_mod[0], b_mod[0], norm1[0], w_in[0], gla_w_gk_up[0], gla_b_gk[0],
                  gla_norm[0], ssd_conv_w[0], ssd_conv_b[0], ssd_dt_bias[0], ssd_A_log[0], ssd_D[0],
                  ssd_norm[0], w_out[0], norm2[0], w_router[0], b_router[0], w_gate_up[0], b_gate_up[0],
                  w_down[0], b_down[0], final_norm)
```

```python
import functools

import numpy as np
import jax
import jax.numpy as jnp
from jax import lax
from jax.experimental import pallas as pl
from jax.experimental.pallas import tpu as pltpu

F32 = jnp.float32
BF16 = jnp.bfloat16

EPS = 1e-6
GRID_W = 64
GLA_HEADS = 4
GLA_DK = 64
GLA_DV = 128
GLA_QK = GLA_HEADS * GLA_DK
GLA_V = GLA_HEADS * GLA_DV
GLA_RANK = 16
GLA_GATE_NORM = 16.0
SSD_HEADDIM = 64
SSD_INNER = 512
SSD_HEADS = 8
SSD_GROUPS = 2
SSD_HPG = 4
SSD_STATE = 128
SSD_CONV_DIM = 1024
N_EXPERTS = 32
TOP_K = 4
D_FF = 1024
SWIGLU_LIMIT = 7.0
SWIGLU_ALPHA = 1.702
MOE_BLOCK = 1024

TB = 256
GLA_C = 64
GLA_STAGE_GROUP = 1
SSD_C = 128
SSD_STAGE_GROUP = 1
LANES = 128
EXP_CLAMP = 80.0
DT_F = 16
DT_B = 24
NEG_BIG = -1e30
RUN = 8
RL = -(-(TB * TOP_K + N_EXPERTS * (RUN - 1)) // LANES) * LANES
CD_LANES = -(-(RL // RUN) // LANES) * LANES
D_HALF = 512
DMA_QUEUES = 2
VMEM_LIMIT = 56 * 1024 * 1024


def _dot(a, b):
    return jnp.dot(a, b, preferred_element_type=F32)


def _dot_nt(a, b):
    return lax.dot_general(a, b, (((1,), (1,)), ((), ())), preferred_element_type=F32)


def _dot_tn(a, b):
    return lax.dot_general(a, b, (((0,), (0,)), ((), ())), preferred_element_type=F32)


def _split3(a):
    hi = a.astype(BF16)
    r1 = a - hi.astype(F32)
    mid = r1.astype(BF16)
    lo = (r1 - mid.astype(F32)).astype(BF16)
    return hi, mid, lo


def _dot_exact_r(m, a):
    hi, mid, lo = _split3(a)
    return _dot(m, hi) + _dot(m, mid) + _dot(m, lo)


def _dot_hilo_r(m, a):
    hi = a.astype(BF16)
    lo = (a - hi.astype(F32)).astype(BF16)
    return _dot(m, hi) + _dot(m, lo)


def _dot_hilo_l(a, m2):
    hi = a.astype(BF16)
    lo = (a - hi.astype(F32)).astype(BF16)
    return _dot(jnp.concatenate([hi, lo], axis=1), m2)


def _sigmoid(x):
    return 1.0 / (1.0 + jnp.exp(-x))


def _softplus(x):
    return jnp.maximum(x, 0.0) + jnp.log1p(jnp.exp(-jnp.abs(x)))


def _params(*sem):
    return pltpu.CompilerParams(dimension_semantics=sem, vmem_limit_bytes=VMEM_LIMIT)


def _mod_kernel(c_ref, w_ref, b_ref, o_ref):
    c = c_ref[...]
    s = c * _sigmoid(c)
    s_hi = s.astype(BF16)
    s_lo = (s - s_hi.astype(F32)).astype(BF16)
    w = w_ref[...]
    w_hi = w.astype(BF16)
    w_lo = (w - w_hi.astype(F32)).astype(BF16)
    o_ref[...] = _dot(s_hi, w_hi) + _dot(s_lo, w_hi) + _dot(s_hi, w_lo) + b_ref[...]


def _mod_call(cin, w, b):
    rows, d = cin.shape
    n = w.shape[1]
    tn = 1536
    return pl.pallas_call(
        _mod_kernel,
        grid=(n // tn,),
        in_specs=[pl.BlockSpec((rows, d), lambda i: (0, 0)),
                  pl.BlockSpec((d, tn), lambda i: (0, i)),
                  pl.BlockSpec((1, tn), lambda i: (0, i))],
        out_specs=pl.BlockSpec((rows, tn), lambda i: (0, i)),
        out_shape=jax.ShapeDtypeStruct((rows, n), F32),
        compiler_params=_params("arbitrary"),
    )(cin, w, b)


_C_Q, _C_K, _C_V, _C_G, _C_Z, _C_X, _C_M, _C_END = 0, 256, 512, 1024, 1536, 2048, 3072, 3200


def _inproj_kernel(x0_ref, xa_ref, xb_ref, ctx_ref, shl_ref, scl_ref, shc_ref, scc_ref, n1_ref, w_ref, wup_ref,
                   bup_ref, dtb_ref, q_ref, k_ref, v_ref, g_ref, z_ref, xbc_ref, ld_ref, misc_ref, h_scr):
    j = pl.program_id(1)
    slot = j & 1

    def normmod(xv, sh_ref, sc_ref):
        ms = jnp.mean(xv * xv, axis=-1, keepdims=True)
        y = xv * lax.rsqrt(ms + EPS) * n1_ref[...]
        return (y * (1.0 + sc_ref[0]) + sh_ref[0]).astype(BF16)

    @pl.when(j == 0)
    def _():
        h_scr[0, 0:TB, :] = normmod(ctx_ref[0], shc_ref, scc_ref)
        h_scr[0, TB:2 * TB, :] = normmod(x0_ref[0], shl_ref, scl_ref)

    def mm(lo, hi):
        return _dot(h_scr[slot], w_ref[:, lo:hi])

    m = mm(_C_M, _C_END)
    zz = _dot(m.astype(BF16), wup_ref[...]) + bup_ref[...]
    ld_ref[0] = -_softplus(-zz) * (1.0 / GLA_GATE_NORM)
    misc_ref[0] = _softplus(m + dtb_ref[...])
    q_ref[0] = (mm(_C_Q, _C_K) * (GLA_DK ** -0.5)).astype(BF16)
    h_scr[1 - slot, 0:TB, :] = normmod(xa_ref[0], shl_ref, scl_ref)
    k_ref[0] = mm(_C_K, _C_V).astype(BF16)
    h_scr[1 - slot, TB:2 * TB, :] = normmod(xb_ref[0], shl_ref, scl_ref)
    v_ref[0] = mm(_C_V, _C_G).astype(BF16)
    g_ref[0] = mm(_C_G, _C_Z)
    z_ref[0] = mm(_C_Z, _C_X)
    xbc_ref[0] = mm(_C_X, _C_M)


def _inproj_call(x, ctx, sh1, sc1, n1, w_cat, wup, bup, dtb):
    bsz, L, d = x.shape
    nx = L // TB
    nj = -(-(nx + 1) // 2)
    tok = lambda n: pl.BlockSpec((1, 2 * TB, n), lambda b, j: (b, j, 0))
    const = lambda a: pl.BlockSpec(a.shape, lambda b, j: (0,) * a.ndim)
    xblk = lambda off: pl.BlockSpec((1, TB, d), lambda b, j: (b, jnp.minimum(2 * j + off, nx - 1), 0))
    first = pl.BlockSpec((1, TB, d), lambda b, j: (b, 0, 0))
    mod_lat = pl.BlockSpec((1, 1, d), lambda b, j: (b, 0, 0))
    mod_ctx = pl.BlockSpec((1, 1, d), lambda b, j: (bsz, 0, 0))
    outs = [(GLA_QK, BF16), (GLA_QK, BF16), (GLA_V, BF16), (GLA_V, F32), (SSD_INNER, F32),
            (SSD_CONV_DIM, F32), (2 * GLA_QK, F32), (LANES, F32)]
    return pl.pallas_call(
        _inproj_kernel,
        grid=(bsz, nj),
        in_specs=[first, xblk(1), xblk(2), first,
                  mod_lat, mod_lat, mod_ctx, mod_ctx,
                  const(n1), const(w_cat), const(wup), const(bup), const(dtb)],
        out_specs=[tok(n) for n, _ in outs],
        out_shape=[jax.ShapeDtypeStruct((bsz, nj * 2 * TB, n), dt) for n, dt in outs],
        scratch_shapes=[pltpu.VMEM((2, 2 * TB, d), BF16)],
        compiler_params=_params("arbitrary", "arbitrary"),
    )(x, x, x, ctx, sh1, sc1, sh1, sc1, n1, w_cat, wup, bup, dtb)


_EXT_PAD = 8
_EXT_BASE = _EXT_PAD + GRID_W
_EXT_ROWS = 2 * _EXT_PAD + 2 * GRID_W + TB


def _conv_kernel(prev_ref, cur_ref, next_ref, w_ref, b_ref, xs_ref, bc_ref, ext_s, xl_all, xr_all):
    for bb in range(cur_ref.shape[0]):
        _conv_block(prev_ref.at[bb], cur_ref.at[bb], next_ref.at[bb], w_ref, b_ref, xs_ref.at[bb], bc_ref.at[bb],
                    ext_s.at[bb], xl_all.at[bb], xr_all.at[bb])


def _conv_block(prev_ref, cur_ref, next_ref, w_ref, b_ref, xs_ref, bc_ref, ext, xl_s, xr_s):
    j = pl.program_id(1)
    nj = pl.num_programs(1)
    is_ctx = j == 0
    zpad = jnp.zeros((_EXT_PAD, SSD_CONV_DIM), F32)
    ext[0:_EXT_PAD, :] = zpad
    ext[_EXT_ROWS - _EXT_PAD:_EXT_ROWS, :] = zpad
    ext[_EXT_PAD:_EXT_BASE, :] = jnp.where(j >= 2, prev_ref[...], 0.0)
    ext[_EXT_BASE:_EXT_BASE + TB, :] = cur_ref[...]
    ext[_EXT_BASE + TB:_EXT_BASE + TB + GRID_W, :] = jnp.where(
        jnp.logical_and(j >= 1, j <= nj - 2), next_ref[...], 0.0)

    win = TB + 2 * GRID_W
    u = lax.broadcasted_iota(jnp.int32, (win, LANES), 0)
    pos = jnp.where(is_ctx, u - GRID_W, u & (GRID_W - 1))
    ok_l = pos >= 1
    ok_r = jnp.where(is_ctx, pos - (TB - GRID_W), pos) <= GRID_W - 2
    lat = jnp.where(is_ctx, 0.0, 1.0)
    side = {-1: xl_s, 0: None, 1: xr_s}

    for c in range(SSD_CONV_DIM // LANES):
        lo, hi = c * LANES, (c + 1) * LANES
        xl_s[:, lo:hi] = jnp.where(ok_l, ext[_EXT_PAD - 1:_EXT_PAD - 1 + win, lo:hi], 0.0)
        xr_s[:, lo:hi] = jnp.where(ok_r, ext[_EXT_PAD + 1:_EXT_PAD + 1 + win, lo:hi], 0.0)
        acc = jnp.zeros((TB, LANES), F32)
        for dr in (-1, 0, 1):
            for dc in (-1, 0, 1):
                if dc == 0:
                    start = _EXT_BASE + GRID_W * dr
                    tap = ext[start:start + TB, lo:hi]
                else:
                    start = GRID_W + GRID_W * dr
                    tap = side[dc][start:start + TB, lo:hi]
                wi = 3 * (dr + 1) + (dc + 1)
                wv = w_ref[wi:wi + 1, lo:hi]
                if dr != 0:
                    wv = wv * lat
                acc = acc + tap * wv
        y = acc + b_ref[:, lo:hi]
        y = y * _sigmoid(y)
        if c < SSD_INNER // LANES:
            xs_ref[:, lo:hi] = y
        else:
            bc_ref[:, lo - SSD_INNER:hi - SSD_INNER] = y.astype(BF16)


def _conv_call(xbc, w9, bias, ls):
    bsz, _, ch = xbc.shape
    nj = ls // TB
    rpb = TB // GRID_W
    nrow = ls // GRID_W
    nb = next(n for n in (4, 2, 1) if bsz % n == 0)
    return pl.pallas_call(
        _conv_kernel,
        grid=(bsz // nb, nj),
        in_specs=[pl.BlockSpec((nb, GRID_W, ch), lambda b, j: (b, jnp.maximum(rpb * j - 1, 0), 0)),
                  pl.BlockSpec((nb, TB, ch), lambda b, j: (b, j, 0)),
                  pl.BlockSpec((nb, GRID_W, ch), lambda b, j: (b, jnp.minimum(rpb * j + rpb, nrow - 1), 0)),
                  pl.BlockSpec((9, ch), lambda b, j: (0, 0)),
                  pl.BlockSpec((1, ch), lambda b, j: (0, 0))],
        out_specs=[pl.BlockSpec((nb, TB, SSD_INNER), lambda b, j: (b, j, 0)),
                   pl.BlockSpec((nb, TB, ch - SSD_INNER), lambda b, j: (b, j, 0))],
        out_shape=[jax.ShapeDtypeStruct((bsz, ls, SSD_INNER), F32),
                   jax.ShapeDtypeStruct((bsz, ls, ch - SSD_INNER), BF16)],
        scratch_shapes=[pltpu.VMEM((nb, _EXT_ROWS, ch), F32), pltpu.VMEM((nb, TB + 2 * GRID_W, ch), F32),
                        pltpu.VMEM((nb, TB + 2 * GRID_W, ch), F32)],
        compiler_params=_params("arbitrary", "arbitrary"),
    )(xbc, xbc, xbc, w9, bias)


def _fwd_blk(s):
    return s


def _bwd_blk(s, ns):
    return jnp.where(s == 0, 0, ns - s)


def _scan_batch(bsz):
    return 2 if bsz % 2 == 0 else 1


class _GlaChunk:
    def __init__(self, q, k, v, la, st_ref, tri_m, fwd, store):
        self.q, self.k, self.v, self.la, self.st_ref, self.tri_m, self.fwd, self.store = (
            q, k, v, la, st_ref, tri_m, fwd, store)

    def stage_sums(self):
        self.b = _dot_hilo_r(self.tri_m, self.la)

    def stage_factors(self):
        C = GLA_C
        b = self.b
        self.bt = b[C - 1:C, :] if self.fwd else b[0:1, :]
        r = 0.5 * self.bt
        self.er = jnp.exp(r)
        self.qt = (self.q.astype(F32) * jnp.exp(jnp.minimum(b - r, EXP_CLAMP))).astype(BF16)
        kt = (self.k.astype(F32) * jnp.exp(jnp.minimum(r - b, EXP_CLAMP))).astype(BF16)
        head_k = lax.broadcasted_iota(jnp.int32, (C, GLA_QK), 1) >> 6
        zero = jnp.zeros_like(kt)
        self.kh = [jnp.where(head_k == h, kt, zero) for h in range(GLA_HEADS)]
        self.qh = [jnp.where(head_k == h, self.qt, zero) for h in range(GLA_HEADS)]

    def stage_products(self):
        v = self.v
        kcat = jnp.concatenate(self.kh, axis=0)
        self.sc = _dot_nt(self.qt, kcat)
        vcat = jnp.concatenate([v[:, h * GLA_DV:(h + 1) * GLA_DV] for h in range(GLA_HEADS)], axis=0)
        self.u = _dot_tn(vcat, kcat) * self.er

    def stage_mask(self):
        C = GLA_C
        ii = lax.broadcasted_iota(jnp.int32, (C, GLA_HEADS * C), 0)
        jj = lax.broadcasted_iota(jnp.int32, (C, GLA_HEADS * C), 1) & (C - 1)
        causal = (jj <= ii) if self.fwd else (jj >= ii)
        self.p = jnp.where(causal, self.sc, 0.0).astype(BF16)
        v = self.v
        head_v = lax.broadcasted_iota(jnp.int32, (C, GLA_V), 1) >> 7
        self.vst = jnp.concatenate([jnp.where(head_v == h, v, jnp.zeros_like(v)) for h in range(GLA_HEADS)], axis=0)

    def stage_intra(self):
        self.o = _dot(self.p, self.vst)

    def stage_inter(self):
        st = self.st_ref[...]
        ster = (st * self.er).astype(BF16)
        self.st_ref[...] = st * jnp.exp(self.bt) + self.u
        res = _dot_nt(jnp.concatenate(self.qh, axis=0), ster)
        self.inter = jnp.concatenate([res[h * GLA_C:(h + 1) * GLA_C, :] for h in range(GLA_HEADS)], axis=1)

    def stage_out(self):
        self.store(self.o + self.inter)


def _store_to(ref, bb, sl):
    def store(val):
        ref[bb, sl, :] = val
    return store


def _gla_schedule(qf_ref, kf_ref, vf_ref, lf_ref, qb_ref, kb_ref, vb_ref, lb_ref, trif_ref, trib_ref,
                  of_ref, ob_ref, stf, stb):
    nsub = TB // GLA_C
    store_to = _store_to
    calls = []

    group = GLA_STAGE_GROUP
    for g0 in range(0, nsub, group):
        steps = []
        for i in range(g0, g0 + group):
            sf = pl.ds(i * GLA_C, GLA_C)
            sb = pl.ds((nsub - 1 - i) * GLA_C, GLA_C)
            chunks = []
            for bb in range(qf_ref.shape[0]):
                chunks.append(_GlaChunk(qf_ref[bb, sf, :], kf_ref[bb, sf, :], vf_ref[bb, sf, :], lf_ref[bb, sf, :],
                                        stf.at[bb], trif_ref[...], True, store_to(of_ref, bb, sf)))
                chunks.append(_GlaChunk(qb_ref[bb, sb, :], kb_ref[bb, sb, :], vb_ref[bb, sb, :], lb_ref[bb, sb, :],
                                        stb.at[bb], trib_ref[...], False, store_to(ob_ref, bb, sb)))
            steps.append(chunks)
        for stage in ("stage_sums", "stage_factors", "stage_products", "stage_mask", "stage_intra"):
            for chunks in steps:
                calls += [getattr(ch, stage) for ch in chunks]
        for chunks in steps:
            for stage in ("stage_inter", "stage_out"):
                calls += [getattr(ch, stage) for ch in chunks]
    return calls


class _SsdChunk:
    def __init__(self, xs, bc, dtm, avec, dvec, st_ref, tri_m, e_m, base, fwd, store):
        self.xs, self.bc, self.dtm, self.avec, self.dvec, self.st_ref = xs, bc, dtm, avec, dvec, st_ref
        self.tri_m, self.e_m, self.base, self.fwd, self.store = tri_m, e_m, base, fwd, store

    def stage_sums(self):
        self.dt_exp = _dot_hilo_l(self.dtm, self.e_m)
        self.acum = _dot_exact_r(self.tri_m, self.dtm * self.avec)

    def stage_expand(self):
        self.acum_exp = _dot_hilo_l(self.acum, self.e_m)
        self.acum_t = self.acum.T
        self.xdt = self.xs * self.dt_exp
        bc = self.bc
        self.bg = [bc[:, 128 * g:128 * (g + 1)] for g in range(SSD_GROUPS)]
        self.cg = [bc[:, 256 + 128 * g:256 + 128 * (g + 1)] for g in range(SSD_GROUPS)]
        self.cb = [_dot_nt(self.cg[g], self.bg[g]) for g in range(SSD_GROUPS)]

    def stage_decay(self):
        C = SSD_C
        ii = lax.broadcasted_iota(jnp.int32, (C, C), 0)
        jj = lax.broadcasted_iota(jnp.int32, (C, C), 1)
        tri = (jj <= ii) if self.fwd else (jj >= ii)
        self.ms = []
        for g in range(SSD_GROUPS):
            for rr in range(SSD_HPG):
                ln = self.base + SSD_HPG * g + rr
                diff = self.acum[:, ln:ln + 1] - self.acum_t[ln:ln + 1, :]
                seg = jnp.where(tri, jnp.exp(jnp.minimum(diff, 0.0)), 0.0)
                self.ms.append((self.cb[g] * seg).astype(BF16))
        ae = self.acum_exp
        self.al_exp = ae[C - 1:C, :] if self.fwd else ae[0:1, :]
        self.xw = (self.xdt * jnp.exp(self.al_exp - ae)).astype(BF16)
        xdt_b = self.xdt.astype(BF16)
        gw = SSD_HPG * SSD_HEADDIM
        head = lax.broadcasted_iota(jnp.int32, (C, gw), 1) >> 6
        zero = jnp.zeros((C, gw), BF16)
        self.xh = [jnp.where(head == h % SSD_HPG, xdt_b[:, gw * (h // SSD_HPG):gw * (h // SSD_HPG + 1)], zero)
                   for h in range(SSD_HEADS)]

    def stage_products(self):
        self.yg, self.ug = [], []
        for g in range(SSD_GROUPS):
            gl, gh = 256 * g, 256 * (g + 1)
            yg = _dot(self.ms[SSD_HPG * g], self.xh[SSD_HPG * g])
            for rr in range(1, SSD_HPG):
                yg = yg + _dot(self.ms[SSD_HPG * g + rr], self.xh[SSD_HPG * g + rr])
            self.yg.append(yg)
            self.ug.append(_dot_tn(self.bg[g], self.xw[:, gl:gh]))

    def stage_state(self):
        ys = []
        for g in range(SSD_GROUPS):
            gl, gh = 256 * g, 256 * (g + 1)
            sg = self.st_ref[g]
            yoff = _dot(self.cg[g], sg.astype(BF16)) * jnp.exp(self.acum_exp[:, gl:gh])
            self.st_ref[g] = sg * jnp.exp(self.al_exp[:, gl:gh]) + self.ug[g]
            ys.append(self.yg[g] + yoff)
        y = jnp.concatenate(ys, axis=1)
        if self.dvec is not None:
            y = y + self.dvec * self.xs
        self.store(y)


def _ssd_schedule(xf_ref, bcf_ref, mf_ref, xb_ref, bcb_ref, mb_ref, af_ref, ab_ref, d_ref,
                  trif_ref, trib_ref, ef_ref, eb_ref, yf_ref, yb_ref, stf, stb):
    nsub = TB // SSD_C
    store_to = _store_to
    calls = []
    steps = []
    for i in range(nsub):
        sf = pl.ds(i * SSD_C, SSD_C)
        sb = pl.ds((nsub - 1 - i) * SSD_C, SSD_C)
        chunks = []
        for bb in range(xf_ref.shape[0]):
            chunks.append(_SsdChunk(xf_ref[bb, sf, :], bcf_ref[bb, sf, :], mf_ref[bb, sf, :], af_ref[...],
                                    d_ref[...], stf.at[bb], trif_ref[...], ef_ref[...], DT_F, True,
                                    store_to(yf_ref, bb, sf)))
            chunks.append(_SsdChunk(xb_ref[bb, sb, :], bcb_ref[bb, sb, :], mb_ref[bb, sb, :], ab_ref[...],
                                    None, stb.at[bb], trib_ref[...], eb_ref[...], DT_B, False,
                                    store_to(yb_ref, bb, sb)))
        steps.append(chunks)
    for g0 in range(0, nsub, SSD_STAGE_GROUP):
        group = steps[g0:g0 + SSD_STAGE_GROUP]
        for stage in ("stage_sums", "stage_expand", "stage_decay", "stage_products"):
            for chunks in group:
                calls += [getattr(ch, stage) for ch in chunks]
        for chunks in group:
            calls += [ch.stage_state for ch in chunks]
    return calls


N_GLA_IN, N_SSD_IN = 10, 13


def _scan_kernel(*refs):
    gla_in, ssd_in = refs[:N_GLA_IN], refs[N_GLA_IN:N_GLA_IN + N_SSD_IN]
    of_ref, ob_ref, yf_ref, yb_ref, gstf, gstb, sstf, sstb = refs[N_GLA_IN + N_SSD_IN:]

    @pl.when(pl.program_id(1) == 0)
    def _():
        for st in (gstf, gstb, sstf, sstb):
            st[...] = jnp.zeros_like(st)

    gla = _gla_schedule(*gla_in, of_ref, ob_ref, gstf, gstb)
    ssd = _ssd_schedule(*ssd_in, yf_ref, yb_ref, sstf, sstb)
    merged = sorted([((n + 0.5) / len(gla), 0, n, c) for n, c in enumerate(gla)] +
                    [((n + 0.5) / len(ssd), 1, n, c) for n, c in enumerate(ssd)], key=lambda t: t[:3])
    for _, _, _, call in merged:
        call()


def _expand_matrix(base):
    e = np.zeros((LANES, SSD_INNER), np.float32)
    for h in range(SSD_HEADS):
        e[base + h, SSD_HEADDIM * h:SSD_HEADDIM * (h + 1)] = 1.0
    return jnp.asarray(np.concatenate([e, e], axis=0), BF16)


def _scan_call(q, k, v, ld, xs, bc, misc, a_f, a_b, dvec, L):
    bsz = q.shape[0]
    nx = L // TB
    ns = nx + 1
    tri = lambda c, up: jnp.asarray((np.triu if up else np.tril)(np.ones((c, c), np.float32)), BF16)
    g_trif, g_trib, s_trif, s_trib = tri(GLA_C, False), tri(GLA_C, True), tri(SSD_C, False), tri(SSD_C, True)
    ef, eb = _expand_matrix(DT_F), _expand_matrix(DT_B)
    nb = _scan_batch(bsz)
    f = lambda n, lane=0: pl.BlockSpec((nb, TB, n), lambda b, s: (b, _fwd_blk(s), lane))
    r = lambda n, lane=0: pl.BlockSpec((nb, TB, n), lambda b, s: (b, _bwd_blk(s, ns), lane))
    const = lambda a: pl.BlockSpec(a.shape, lambda b, s: (0,) * a.ndim)
    out_f = lambda n: pl.BlockSpec((nb, TB, n), lambda b, s: (b, jnp.maximum(s - 1, 0), 0))
    out_b = lambda n: pl.BlockSpec((nb, TB, n), lambda b, s: (b, jnp.where(s == 0, nx - 1, nx - s), 0))
    gla_specs = [f(GLA_QK), f(GLA_QK), f(GLA_V), f(GLA_QK, 0), r(GLA_QK), r(GLA_QK), r(GLA_V), r(GLA_QK, 1),
                 const(g_trif), const(g_trib)]
    ssd_specs = [f(SSD_INNER), f(512), f(LANES), r(SSD_INNER), r(512), r(LANES),
                 const(a_f), const(a_b), const(dvec), const(s_trif), const(s_trib), const(ef), const(eb)]
    assert len(gla_specs) == N_GLA_IN and len(ssd_specs) == N_SSD_IN
    return pl.pallas_call(
        _scan_kernel,
        grid=(bsz // nb, ns),
        in_specs=gla_specs + ssd_specs,
        out_specs=[out_f(GLA_V), out_b(GLA_V), out_f(SSD_INNER), out_b(SSD_INNER)],
        out_shape=[jax.ShapeDtypeStruct((bsz, L, GLA_V), F32)] * 2 +
                  [jax.ShapeDtypeStruct((bsz, L, SSD_INNER), F32)] * 2,
        scratch_shapes=[pltpu.VMEM((nb, GLA_DV, GLA_QK), F32)] * 2 +
                       [pltpu.VMEM((nb, SSD_GROUPS, SSD_STATE, SSD_HPG * SSD_HEADDIM), F32)] * 2,
        compiler_params=_params("arbitrary", "arbitrary"),
    )(q, k, v, ld, q, k, v, ld, g_trif, g_trib, xs, bc, misc, xs, bc, misc, a_f, a_b, dvec, s_trif, s_trib, ef, eb)


def _outproj_kernel(of_ref, ob_ref, ga_ref, gb_ref, yf_ref, yb_ref, za_ref, zb_ref, x_ref, g1_ref, sh2_ref, sc2_ref,
                    gn_ref, sn_ref, wo_ref, n2_ref, wrh_ref, wrl_ref, br_ref,
                    x1_ref, h2_ref, eidx_ref, gate_ref, cnt_ref, mix_s, hl_s):
    first = jnp.logical_and(pl.program_id(0) == 0, pl.program_id(1) == 0)

    @pl.when(first)
    def _():
        cnt_ref[...] = jnp.zeros_like(cnt_ref)

    halves = [(pl.ds(0, TB), ga_ref, za_ref), (pl.ds(TB, TB), gb_ref, zb_ref)]

    for rows, g_ref, z_ref in halves:
        o = of_ref[0, rows, :] + ob_ref[0, rows, :]
        gg = g_ref[0]
        for h in range(GLA_HEADS):
            lo, hi = GLA_DV * h, GLA_DV * (h + 1)
            oh = o[:, lo:hi]
            ms = jnp.mean(oh * oh, axis=-1, keepdims=True)
            gh = gg[:, lo:hi]
            mix_s[rows, lo:hi] = (oh * lax.rsqrt(ms + EPS) * gn_ref[:, lo:hi] * (gh * _sigmoid(gh))).astype(BF16)
        zz = z_ref[0]
        u = (yf_ref[0, rows, :] + yb_ref[0, rows, :]) * (zz * _sigmoid(zz))
        gw = SSD_INNER // SSD_GROUPS
        for g in range(SSD_GROUPS):
            lo, hi = gw * g, gw * (g + 1)
            ug = u[:, lo:hi]
            ms = jnp.mean(ug * ug, axis=-1, keepdims=True)
            mix_s[rows, GLA_V + lo:GLA_V + hi] = (ug * lax.rsqrt(ms + EPS) * sn_ref[:, lo:hi]).astype(BF16)

    x1_ref[0] = x_ref[0] + g1_ref[0] * _dot(mix_s[...], wo_ref[...])

    for rows, _, _ in halves:
        x1 = x1_ref[0, rows, :]
        ms = jnp.mean(x1 * x1, axis=-1, keepdims=True)
        h2 = x1 * lax.rsqrt(ms + EPS) * (n2_ref[...] * (1.0 + sc2_ref[0])) + sh2_ref[0]
        h_hi = h2.astype(BF16)
        h2_ref[0, rows, :] = h_hi
        hl_s[rows, :] = (h2 - h_hi.astype(F32)).astype(BF16)

    h_hi = h2_ref[0]
    logits = (_dot(h_hi, wrh_ref[...]) + _dot(hl_s[...], wrh_ref[...]) + _dot(h_hi, wrl_ref[...])) + br_ref[...]

    lane = lax.broadcasted_iota(jnp.int32, (TB, LANES), 1).astype(F32)
    st = [dict(work=logits[i * TB:(i + 1) * TB, :], eidx=jnp.full((TB, LANES), -1.0, F32),
               gates=jnp.zeros((TB, LANES), F32), sel=jnp.zeros((TB, LANES), F32), m0=None,
               den=jnp.zeros((TB, 1), F32)) for i in range(2)]
    for kk in range(TOP_K):
        for t in st:
            t["m"] = jnp.max(t["work"], axis=-1, keepdims=True)
        for t in st:
            t["idx"] = jnp.min(jnp.where(t["work"] == t["m"], lane, float(LANES)), axis=-1, keepdims=True)
        for t in st:
            hit = lane == t["idx"]
            if t["m0"] is None:
                t["m0"] = t["m"]
            e = jnp.exp(t["m"] - t["m0"])
            t["den"] = t["den"] + e
            t["eidx"] = jnp.where(lane == float(kk), t["idx"], t["eidx"])
            t["gates"] = jnp.where(lane == float(kk), e, t["gates"])
            t["sel"] = jnp.where(hit, 1.0, t["sel"])
            t["work"] = jnp.where(hit, NEG_BIG, t["work"])
    for (rows, _, _), t in zip(halves, st):
        eidx_ref[0, rows, :] = t["eidx"].astype(jnp.int32)
        gate_ref[0, rows, :] = t["gates"] / t["den"]
        cnt = jnp.sum(t["sel"], axis=0, keepdims=True)
        cnt_ref[...] += jnp.floor((cnt + (RUN - 1.0)) * (1.0 / RUN)) * RUN


def _outproj_call(o_f, o_b, g_all, y_f, y_b, z_all, x, g1, sh2, sc2, gn, sn, wo, n2, wr_hi, wr_lo, br):
    bsz, L, d = x.shape
    nj = L // (2 * TB)
    tok = lambda n: pl.BlockSpec((1, 2 * TB, n), lambda b, j: (b, j, 0))
    tok_off = lambda n, half: pl.BlockSpec((1, TB, n), lambda b, j: (b, 2 * j + 1 + half, 0))
    const = lambda a: pl.BlockSpec(a.shape, lambda b, j: (0,) * a.ndim)
    mod = pl.BlockSpec((1, 1, d), lambda b, j: (b, 0, 0))
    return pl.pallas_call(
        _outproj_kernel,
        grid=(bsz, nj),
        in_specs=[tok(GLA_V), tok(GLA_V), tok_off(GLA_V, 0), tok_off(GLA_V, 1),
                  tok(SSD_INNER), tok(SSD_INNER), tok_off(SSD_INNER, 0), tok_off(SSD_INNER, 1),
                  tok(d), mod, mod, mod, const(gn), const(sn), const(wo), const(n2),
                  const(wr_hi), const(wr_lo), const(br)],
        out_specs=[tok(d), tok(d), tok(LANES), tok(LANES), pl.BlockSpec((1, LANES), lambda b, j: (0, 0))],
        out_shape=[jax.ShapeDtypeStruct((bsz, L, d), F32), jax.ShapeDtypeStruct((bsz, L, d), BF16),
                   jax.ShapeDtypeStruct((bsz, L, LANES), jnp.int32), jax.ShapeDtypeStruct((bsz, L, LANES), F32),
                   jax.ShapeDtypeStruct((1, LANES), F32)],
        scratch_shapes=[pltpu.VMEM((2 * TB, d), BF16), pltpu.VMEM((2 * TB, d), BF16)],
        compiler_params=_params("arbitrary", "arbitrary"),
    )(o_f, o_b, g_all, g_all, y_f, y_b, z_all, z_all, x, g1, sh2, sc2, gn, sn, wo, n2, wr_hi, wr_lo, br)


def _pos_kernel(eidx_ref, pstart_ref, lst_ref, ust_ref, lp_ref, cd_ref, carry):
    @pl.when(pl.program_id(0) == 0)
    def _():
        carry[...] = pstart_ref[...]

    lane = lax.broadcasted_iota(jnp.int32, (TB, LANES), 1)
    tiles = []
    for t in range(cd_ref.shape[0]):
        eidx = eidx_ref[t * TB:(t + 1) * TB, :]
        hits = [lane == eidx[:, kk:kk + 1] for kk in range(TOP_K)]
        sel = jnp.zeros((TB, LANES), F32)
        for hmask in hits:
            sel = jnp.where(hmask, 1.0, sel)
        cnt = jnp.sum(sel, axis=0, keepdims=True)
        run = jnp.floor((cnt + (RUN - 1.0)) * (1.0 / RUN)) * RUN
        tiles.append(dict(hits=hits, sel=sel, run=run))
    for tl in tiles:
        tl["rank"] = _dot(lst_ref[...], tl["sel"].astype(BF16))
        tl["loff"] = _dot(jnp.broadcast_to(tl["run"], (8, LANES)).astype(BF16), ust_ref[...])[0:1]
    for t, tl in enumerate(tiles):
        pos = tl["loff"] + tl["rank"]
        lp = jnp.zeros((TB, LANES), jnp.int32)
        for kk, hmask in enumerate(tl["hits"]):
            lk = jnp.sum(jnp.where(hmask, pos, 0.0), axis=-1, keepdims=True)
            lp = jnp.where(lane == kk, lk.astype(jnp.int32), lp)
        lp_ref[t * TB:(t + 1) * TB, :] = lp

    eye = lax.broadcasted_iota(jnp.int32, (LANES, LANES), 0) == lax.broadcasted_iota(jnp.int32, (LANES, LANES), 1)
    col = lambda v: jnp.sum(jnp.where(eye, v, 0.0), axis=1, keepdims=True)
    row0 = (lax.broadcasted_iota(jnp.int32, (LANES, CD_LANES), 1) * RUN).astype(F32)
    base = carry[...]
    for t, tl in enumerate(tiles):
        loff_c, run_c, shift_c = col(tl["loff"]), col(tl["run"]), col(base - tl["loff"])
        inside = jnp.where(row0 >= loff_c, 1.0, 0.0) * jnp.where(row0 < loff_c + run_c, 1.0, 0.0)
        valid = jnp.sum(inside, axis=0, keepdims=True)
        dest = jnp.sum(inside * shift_c, axis=0, keepdims=True) + row0[0:1]
        cd_ref[t] = jnp.where(valid > 0.0, dest, -1.0).astype(jnp.int32)
        base = base + tl["run"]
    carry[...] = base


def _pos_call(eidx, pstart):
    T = eidx.shape[0]
    lst = jnp.asarray(np.tril(np.ones((TB, TB), np.float32), -1), BF16)
    ust = jnp.asarray(np.triu(np.ones((LANES, LANES), np.float32), 1), BF16)
    tps = next(n for n in (4, 2, 1) if (T // TB) % n == 0)
    return pl.pallas_call(
        _pos_kernel,
        grid=(T // (tps * TB),),
        in_specs=[pl.BlockSpec((tps * TB, LANES), lambda i: (i, 0)),
                  pl.BlockSpec((1, LANES), lambda i: (0, 0)),
                  pl.BlockSpec((TB, TB), lambda i: (0, 0)),
                  pl.BlockSpec((LANES, LANES), lambda i: (0, 0))],
        out_specs=[pl.BlockSpec((tps * TB, LANES), lambda i: (i, 0)),
                   pl.BlockSpec((tps, 1, CD_LANES), lambda i: (i, 0, 0))],
        out_shape=[jax.ShapeDtypeStruct((T, LANES), jnp.int32),
                   jax.ShapeDtypeStruct((T // TB, 1, CD_LANES), jnp.int32)],
        scratch_shapes=[pltpu.VMEM((1, LANES), F32)],
        compiler_params=_params("arbitrary"),
    )(eidx, pstart, lst, ust)


def _pair_matrix(lp, weights):
    j = lax.broadcasted_iota(jnp.int32, (TB, RL), 1)
    m = jnp.zeros((TB, RL), F32)
    for kk in range(TOP_K):
        w = 1.0 if weights is None else weights[:, kk:kk + 1]
        m = jnp.where(j == lp[:, kk:kk + 1], w, m)
    return m


def _pack_bf16_pairs(v, is_bf16_valued=False):
    if not is_bf16_valued:
        v = v.astype(BF16).astype(F32)
    bits = pltpu.bitcast(v, jnp.uint32)
    return bits[:, D_HALF:] | (bits[:, :D_HALF] >> 16)


def _unpack_bf16_pairs(w):
    lo = pltpu.bitcast(w << 16, F32).astype(BF16)
    hi = pltpu.bitcast(w & jnp.uint32(0xFFFF0000), F32).astype(BF16)
    return lo, hi


N_SLOTS = 3
TILES_PER_STEP = 2


def _dispatch_kernel(fill_ref, *refs):
    cd_refs = refs[:TILES_PER_STEP + 1]
    h_ref, lp_ref, xg_ref, sorted_s, sems, fill_sem = refs[TILES_PER_STEP + 1:]
    step = pl.program_id(0)
    spill0 = xg_ref.shape[0] - N_SLOTS * RL

    def start(cref, s, c, to_spill, sem, queue=0):
        src = RUN * c if isinstance(c, int) else pl.multiple_of(RUN * c, RUN)
        d = cref[0, 0, c]
        d = pl.multiple_of(jnp.where(jnp.logical_or(d < 0, to_spill), spill0 + s * RL + src, d), RUN)
        pltpu.make_async_copy(sorted_s.at[s, pl.ds(src, RUN), :], xg_ref.at[pl.ds(d, RUN), :], sem).start(
            priority=queue)

    def drain(s, sem):
        pltpu.make_async_copy(sorted_s.at[s], xg_ref.at[pl.ds(0, RL), :], sem).wait()

    @pl.when(step == 0)
    def _():
        sorted_s[...] = jnp.zeros_like(sorted_s)
        for s in range(N_SLOTS):
            lax.fori_loop(0, RL // RUN, lambda c, carry, s=s: (start(cd_refs[0], s, c, True, fill_sem), carry)[1], 0)
        for s in range(N_SLOTS):
            drain(s, fill_sem)

    for k in range(TILES_PER_STEP):
        t = TILES_PER_STEP * step + k
        rows = slice(k * TB, (k + 1) * TB)
        slot = lax.rem(t, N_SLOTS)
        prev = lax.rem(t + N_SLOTS - 1, N_SLOTS)
        pprev = lax.rem(t + N_SLOTS - 2, N_SLOTS)

        for c in range(RL // RUN):
            start(cd_refs[k], prev, c, t == 0, sems.at[prev], queue=c % DMA_QUEUES)

        sorted_s[slot] = _pack_bf16_pairs(
            _dot_tn(_pair_matrix(lp_ref[rows, :], None).astype(BF16), h_ref[rows, :]), is_bf16_valued=True)

        if k == 0:
            @pl.when(step == 0)
            def _():
                def fill(j):
                    blk = pl.multiple_of(fill_ref[j] * MOE_BLOCK, MOE_BLOCK)
                    return pltpu.make_async_copy(sorted_s.at[0, pl.ds(0, MOE_BLOCK), :],
                                                 xg_ref.at[pl.ds(blk, MOE_BLOCK), :], fill_sem)

                for j in range(fill_ref.shape[0]):
                    pl.when(fill_ref[j] >= 0)(lambda j=j: fill(j).start())
                for j in range(fill_ref.shape[0]):
                    pl.when(fill_ref[j] >= 0)(lambda j=j: fill(j).wait())

        pl.when(t >= 1)(lambda pprev=pprev: drain(pprev, sems.at[pprev]))

    @pl.when(step == pl.num_programs(0) - 1)
    def _():
        lax.fori_loop(0, RL // RUN,
                      lambda c, carry: (start(cd_refs[TILES_PER_STEP], slot, c, False, sems.at[slot]), carry)[1], 0,
                      unroll=4)
        drain(prev, sems.at[prev])
        drain(slot, sems.at[slot])


def _dispatch_call(fill_blocks, cd, h2, lp, P):
    T, d = h2.shape
    tps = TILES_PER_STEP
    assert (T // TB) % tps == 0
    cd_spec = lambda k: pl.BlockSpec((1, 1, CD_LANES), lambda i, fb: (jnp.maximum(tps * i + k - 1, 0), 0, 0),
                                     memory_space=pltpu.SMEM)
    grid_spec = pltpu.PrefetchScalarGridSpec(
        num_scalar_prefetch=1,
        grid=(T // (tps * TB),),
        in_specs=[cd_spec(k) for k in range(tps + 1)] +
                 [pl.BlockSpec((tps * TB, d), lambda i, fb: (i, 0)),
                  pl.BlockSpec((tps * TB, LANES), lambda i, fb: (i, 0))],
        out_specs=pl.BlockSpec(memory_space=pl.ANY),
        scratch_shapes=[pltpu.VMEM((N_SLOTS, RL, D_HALF), jnp.uint32), pltpu.SemaphoreType.DMA((N_SLOTS,)),
                        pltpu.SemaphoreType.DMA(())],
    )
    return pl.pallas_call(
        _dispatch_kernel,
        grid_spec=grid_spec,
        out_shape=jax.ShapeDtypeStruct((P + N_SLOTS * RL, D_HALF), jnp.uint32),
        compiler_params=_params("arbitrary"),
    )(fill_blocks, *([cd] * (tps + 1)), h2, lp)


def _moe_kernel(nb, info_ref, x_hbm, wgu_hbm, bgu_ref, wd_hbm, bd_ref, y_hbm,
                wgu_f, wd_f, wgu_b, wd_b, act_s, xbuf, ybuf, zbuf, wsems, xsems, ysems, zsem):
    n_used = info_ref[0, nb]

    def rows(blk):
        return pl.ds(pl.multiple_of(blk * MOE_BLOCK, MOE_BLOCK), MOE_BLOCK)

    def x_copy(blk, slot):
        return pltpu.make_async_copy(x_hbm.at[rows(blk), :], xbuf.at[slot], xsems.at[slot])

    def y_copy(blk, slot):
        return pltpu.make_async_copy(ybuf.at[slot], y_hbm.at[rows(blk), :], ysems.at[slot])

    def z_copy(blk):
        return pltpu.make_async_copy(zbuf, y_hbm.at[rows(blk), :], zsem)

    def fetch(expert, slot):
        return (pltpu.make_async_copy(wgu_hbm.at[expert], wgu_f.at[slot], wsems.at[0, slot]),
                pltpu.make_async_copy(wd_hbm.at[expert], wd_f.at[slot], wsems.at[1, slot]))

    zbuf[...] = jnp.zeros_like(zbuf)
    lax.fori_loop(n_used, nb, lambda blk, c: (z_copy(blk).start(), c)[1], 0)

    for cp in fetch(info_ref[0, 0], 0):
        cp.start()
    x_copy(0, 0).start()

    def body(i, carry):
        slot = i & 1
        e = info_ref[0, i]

        @pl.when(info_ref[1, i] == 1)
        def _():
            wslot = info_ref[2, i]
            nxt = info_ref[3, i]
            cps = fetch(e, wslot)
            cps[0].wait()
            wgu_b[...] = wgu_f[wslot].astype(BF16)
            cps[1].wait()
            wd_b[...] = wd_f[wslot].astype(BF16)

            @pl.when(nxt >= 0)
            def _():
                for cp in fetch(nxt, 1 - wslot):
                    cp.start()

        x_copy(i, slot).wait()
        pl.when(i + 1 < n_used)(lambda: x_copy(i + 1, 1 - slot).start())
        pl.when(i >= 2)(lambda: y_copy(i - 2, slot).wait())

        bgu = bgu_ref[pl.ds(e, 1), :]
        xb = jnp.concatenate(_unpack_bf16_pairs(xbuf[slot]), axis=1)
        cw = 256
        for c in range(D_FF // cw):
            lo, hi = c * cw, (c + 1) * cw
            gate = _dot(xb, wgu_b[:, lo:hi]) + bgu[:, lo:hi]
            up = _dot(xb, wgu_b[:, D_FF + lo:D_FF + hi]) + bgu[:, D_FF + lo:D_FF + hi]
            gate = jnp.minimum(gate, SWIGLU_LIMIT)
            up = jnp.clip(up, -SWIGLU_LIMIT, SWIGLU_LIMIT)
            act_s[:, lo:hi] = ((up + 1.0) * (gate * _sigmoid(SWIGLU_ALPHA * gate))).astype(BF16)
        ybuf[slot] = _pack_bf16_pairs(_dot(act_s[...], wd_b[...]) + bd_ref[pl.ds(e, 1), :])
        y_copy(i, slot).start()
        return carry

    lax.fori_loop(0, n_used, body, 0)

    pl.when(n_used >= 2)(lambda: y_copy(n_used - 2, n_used & 1).wait())
    y_copy(n_used - 1, (n_used - 1) & 1).wait()
    lax.fori_loop(n_used, nb, lambda blk, c: (z_copy(blk).wait(), c)[1], 0)


def _moe_call(info, xg, wgu, bgu, wd, bd, nb):
    P = nb * MOE_BLOCK
    ne, d, f2 = wgu.shape
    blk_buf = pltpu.VMEM((2, MOE_BLOCK, D_HALF), jnp.uint32)
    grid_spec = pltpu.PrefetchScalarGridSpec(
        num_scalar_prefetch=1,
        grid=(1,),
        in_specs=[pl.BlockSpec(memory_space=pl.ANY),
                  pl.BlockSpec(memory_space=pl.ANY),
                  pl.BlockSpec((ne, f2), lambda i, info: (0, 0)),
                  pl.BlockSpec(memory_space=pl.ANY),
                  pl.BlockSpec((ne, d), lambda i, info: (0, 0))],
        out_specs=pl.BlockSpec(memory_space=pl.ANY),
        scratch_shapes=[pltpu.VMEM((2, d, f2), F32), pltpu.VMEM((2, f2 // 2, d), F32),
                        pltpu.VMEM((d, f2), BF16), pltpu.VMEM((f2 // 2, d), BF16),
                        pltpu.VMEM((MOE_BLOCK, f2 // 2), BF16), blk_buf, blk_buf,
                        pltpu.VMEM((MOE_BLOCK, D_HALF), jnp.uint32),
                        pltpu.SemaphoreType.DMA((2, 2)), pltpu.SemaphoreType.DMA((2,)),
                        pltpu.SemaphoreType.DMA((2,)), pltpu.SemaphoreType.DMA(())],
    )
    return pl.pallas_call(
        functools.partial(_moe_kernel, nb),
        grid_spec=grid_spec,
        out_shape=jax.ShapeDtypeStruct((P, D_HALF), jnp.uint32),
        compiler_params=_params("arbitrary"),
    )(info, xg, wgu, bgu, wd, bd)


def _combine_kernel(*refs):
    cd_refs = refs[:TILES_PER_STEP + 2]
    y_ref, lp_ref, x1_ref, gate_ref, g2_ref, fn_ref, o_ref, buf, sems = refs[TILES_PER_STEP + 2:]
    step = pl.program_id(0)

    def start(dref, s, c, queue=0):
        d = pl.multiple_of(jnp.maximum(dref[0, 0, c], 0), RUN)
        dst = pl.multiple_of(RUN * c, RUN)
        pltpu.make_async_copy(y_ref.at[pl.ds(d, RUN), :], buf.at[s, pl.ds(dst, RUN), :], sems.at[s]).start(
            priority=queue)

    def drain(s):
        pltpu.make_async_copy(y_ref.at[pl.ds(0, RL), :], buf.at[s], sems.at[s]).wait()

    @pl.when(step == 0)
    def _():
        lax.fori_loop(0, RL // RUN, lambda c, carry: (start(cd_refs[0], 0, c), carry)[1], 0, unroll=4)
        lax.fori_loop(0, RL // RUN, lambda c, carry: (start(cd_refs[1], 1, c), carry)[1], 0, unroll=4)

    for k in range(TILES_PER_STEP):
        t = TILES_PER_STEP * step + k
        rows = slice(k * TB, (k + 1) * TB)
        slot = lax.rem(t, N_SLOTS)
        ahead = lax.rem(t + 2, N_SLOTS)
        drain(slot)

        for c in range(RL // RUN):
            start(cd_refs[k + 2], ahead, c, queue=c % DMA_QUEUES)

        g = _pair_matrix(lp_ref[rows, :], gate_ref[rows, :]).astype(BF16)
        halves = [_dot(g, yb) for yb in _unpack_bf16_pairs(buf[slot])]
        xo = x1_ref[rows, :] + g2_ref[0] * jnp.concatenate(halves, axis=1)
        ms = jnp.mean(xo * xo, axis=-1, keepdims=True)
        o_ref[rows, :] = xo * lax.rsqrt(ms + EPS) * fn_ref[...]

    @pl.when(step == pl.num_programs(0) - 1)
    def _():
        drain(lax.rem(t + 1, N_SLOTS))
        drain(ahead)


def _combine_call(cd, y, lp, x1, gates, g2, fn):
    T, d = x1.shape
    tps = TILES_PER_STEP
    n = T // TB
    assert n % tps == 0 and (n // g2.shape[0]) % tps == 0
    steps_per_batch = n // g2.shape[0] // tps
    cd_spec = lambda k: pl.BlockSpec((1, 1, CD_LANES), lambda i: (jnp.minimum(tps * i + k, n - 1), 0, 0),
                                     memory_space=pltpu.SMEM)
    return pl.pallas_call(
        _combine_kernel,
        grid=(n // tps,),
        in_specs=[cd_spec(k) for k in range(tps + 2)] +
                 [pl.BlockSpec(memory_space=pl.ANY),
                  pl.BlockSpec((tps * TB, LANES), lambda i: (i, 0)),
                  pl.BlockSpec((tps * TB, d), lambda i: (i, 0)),
                  pl.BlockSpec((tps * TB, LANES), lambda i: (i, 0)),
                  pl.BlockSpec((1, 1, d), lambda i: (i // steps_per_batch, 0, 0)),
                  pl.BlockSpec((1, d), lambda i: (0, 0))],
        out_specs=pl.BlockSpec((tps * TB, d), lambda i: (i, 0)),
        out_shape=jax.ShapeDtypeStruct((T, d), F32),
        scratch_shapes=[pltpu.VMEM((N_SLOTS, RL, D_HALF), jnp.uint32), pltpu.SemaphoreType.DMA((N_SLOTS,))],
        compiler_params=_params("arbitrary"),
    )(*([cd] * (tps + 2)), y, lp, x1, gates, g2, fn)


def _layer(x, c, ctx, c_ctx, w_mod, b_mod, norm1, w_in, gla_w_gk_up, gla_b_gk, gla_norm,
           ssd_conv_w, ssd_conv_b, ssd_dt_bias, ssd_A_log, ssd_D, ssd_norm, w_out,
           norm2, w_router, b_router, w_gate_up, b_gate_up, w_down, b_down, final_norm):
    bsz, L, d = x.shape
    lc = ctx.shape[1]
    assert lc == TB and L % TB == 0 and TB % GRID_W == 0

    cin = jnp.zeros((8, d), F32).at[:bsz].set(c).at[bsz].set(c_ctx)
    mod = _mod_call(cin, w_mod, b_mod.reshape(1, -1))[:bsz + 1]
    sh1, sc1, g1, sh2, sc2, g2 = [m.reshape(bsz + 1, 1, d) for m in jnp.split(mod, 6, axis=-1)]

    o = np.cumsum((0, GLA_QK, GLA_QK, GLA_V, GLA_V, GLA_RANK, SSD_INNER, SSD_CONV_DIM, SSD_HEADS))
    wq, wk, wv, wg, wlow, wz, wx, wdt = [w_in[:, int(a):int(b)] for a, b in zip(o[:-1], o[1:])]
    w_misc = jnp.concatenate([wlow, wdt, wdt, jnp.zeros((d, LANES - GLA_RANK - 2 * SSD_HEADS), F32)], axis=1)
    w_cat = jnp.concatenate([wq, wk, wv, wg, wz, wx, w_misc], axis=1).astype(BF16)
    wup = jnp.zeros((LANES, 2 * GLA_QK), F32).at[:GLA_RANK].set(
        jnp.concatenate([gla_w_gk_up[0], gla_w_gk_up[1]], axis=1)).astype(BF16)
    bup = jnp.concatenate([gla_b_gk[0], gla_b_gk[1]]).reshape(1, -1)
    dtb = jnp.zeros((1, LANES), F32).at[0, DT_F:DT_F + SSD_HEADS].set(ssd_dt_bias[0]) \
                                    .at[0, DT_B:DT_B + SSD_HEADS].set(ssd_dt_bias[1])
    q, k, v, g_all, z_all, xbc, ld, misc = _inproj_call(
        x, ctx, sh1, sc1, norm1.reshape(1, d), w_cat, wup, bup, dtb)

    xs, bc = _conv_call(xbc, ssd_conv_w.reshape(9, SSD_CONV_DIM), ssd_conv_b.reshape(1, -1), lc + L)

    a_neg = -jnp.exp(ssd_A_log.astype(F32))
    a_f = jnp.zeros((1, LANES), F32).at[0, DT_F:DT_F + SSD_HEADS].set(a_neg[0])
    a_b = jnp.zeros((1, LANES), F32).at[0, DT_B:DT_B + SSD_HEADS].set(a_neg[1])
    dvec = jnp.repeat(ssd_D, SSD_HEADDIM).reshape(1, SSD_INNER)
    o_f, o_b, y_f, y_b = _scan_call(q, k, v, ld, xs, bc, misc, a_f, a_b, dvec, L)

    wr = jnp.zeros((d, LANES), F32).at[:, :N_EXPERTS].set(w_router)
    wr_hi = wr.astype(BF16)
    wr_lo = (wr - wr_hi.astype(F32)).astype(BF16)
    br = jnp.full((1, LANES), NEG_BIG, F32).at[0, :N_EXPERTS].set(b_router)
    x1, h2, eidx, gates, counts = _outproj_call(
        o_f, o_b, g_all, y_f, y_b, z_all, x, g1[:bsz], sh2[:bsz], sc2[:bsz],
        jnp.tile(gla_norm, GLA_HEADS).reshape(1, -1), ssd_norm.reshape(1, -1), w_out.astype(BF16),
        norm2.reshape(1, d), wr_hi, wr_lo, br)

    T = bsz * L
    cnt = counts[0, :N_EXPERTS].astype(jnp.int32)
    padded = ((cnt + MOE_BLOCK - 1) // MOE_BLOCK) * MOE_BLOCK
    pend = jnp.cumsum(padded)
    pstart = pend - padded
    max_rows = T * TOP_K + (T // TB) * N_EXPERTS * (RUN - 1)
    n_blocks = -(-max_rows // MOE_BLOCK) + N_EXPERTS
    blk_start = jnp.arange(n_blocks, dtype=jnp.int32) * MOE_BLOCK
    blk_e = jnp.minimum(jnp.sum(pend[None, :] <= blk_start[:, None], axis=1), N_EXPERTS - 1).astype(jnp.int32)
    n_used = (pend[-1] // MOE_BLOCK).astype(jnp.int32)
    blk_i = jnp.arange(n_blocks, dtype=jnp.int32)
    first = (blk_i < n_used) & ((blk_i == 0) | (blk_e != jnp.roll(blk_e, 1)))
    slot = (jnp.cumsum(first) - 1) & 1
    first_pos = jnp.where(first, blk_i, n_blocks)
    next_first = jnp.roll(lax.cummin(first_pos, reverse=True), -1).at[-1].set(n_blocks)
    nxt = jnp.where(next_first < n_blocks, blk_e[jnp.minimum(next_first, n_blocks - 1)], -1)
    col = lambda v, last: jnp.concatenate([v.astype(jnp.int32), jnp.asarray([last], jnp.int32)])
    blk_info = jnp.stack([col(blk_e, 0).at[-1].set(n_used), col(first, 0), col(slot, 0), col(nxt, -1)])
    n_tail = n_blocks - (T * TOP_K) // MOE_BLOCK
    tail = n_used + jnp.arange(n_tail, dtype=jnp.int32)
    fill_blocks = jnp.concatenate([
        jnp.where(padded > 0, pend // MOE_BLOCK - 1, -1),
        jnp.where(tail < n_blocks, tail, -1)]).astype(jnp.int32)
    pstart_row = jnp.zeros((1, LANES), F32).at[0, :N_EXPERTS].set(pstart.astype(F32))
    lp, cd = _pos_call(eidx.reshape(T, LANES), pstart_row)

    xg = _dispatch_call(fill_blocks, cd, h2.reshape(T, d), lp, n_blocks * MOE_BLOCK)
    y = _moe_call(blk_info, xg, w_gate_up, b_gate_up, w_down, b_down, n_blocks)
    out = _combine_call(cd, y, lp, x1.reshape(T, d), gates.reshape(T, LANES), g2[:bsz], final_norm.reshape(1, d))
    return out.reshape(bsz, L, d)


def kernel(x, c, ctx, c_ctx, w_mod, b_mod, norm1, w_in, gla_w_gk_up, gla_b_gk, gla_norm, ssd_conv_w, ssd_conv_b, ssd_dt_bias, ssd_A_log, ssd_D, ssd_norm, w_out, norm2, w_router, b_router, w_gate_up, b_gate_up, w_down, b_down, final_norm):
    assert w_mod.shape[0] == 1, "single-layer kernel"
    return _layer(x, c, ctx, c_ctx, w_mod[0], b_mod[0], norm1[0], w_in[0], gla_w_gk_up[0], gla_b_gk[0],
                  gla_norm[0], ssd_conv_w[0], ssd_conv_b[0], ssd_dt_bias[0], ssd_A_log[0], ssd_D[0],
                  ssd_norm[0], w_out[0], norm2[0], w_router[0], b_router[0], w_gate_up[0], b_gate_up[0],
                  w_down[0], b_down[0], final_norm)
```
